```python
import jax, jax.numpy as jnp
from jax import lax
import numpy as np

D_MODEL = 2048
BATCH = 8
SEQ = 4096
DEPTH = 4

MEM_TOKENS = 256
EPS = 1e-6
ROPE_THETA = 10000.0
Q_BLOCK = 128
MLA_HEADS = D_MODEL // 256
QK_NOPE_DIM = 128
QK_ROPE_DIM = 64
QK_HEAD_DIM = QK_NOPE_DIM + QK_ROPE_DIM
V_HEAD_DIM = 128
Q_LORA_RANK = D_MODEL // 4
KV_LORA_RANK = D_MODEL // 8
MLA_WIDTH = MLA_HEADS * V_HEAD_DIM
CONV_WIDTH = D_MODEL // 4
CONV_K = 3
MEM_HEADS = 4
MEM_HEAD_DIM = D_MODEL // 16
MEM_WIDTH = MEM_HEADS * MEM_HEAD_DIM
MIX_WIDTH = MLA_WIDTH + CONV_WIDTH + MEM_WIDTH
IN_SPLITS = (Q_LORA_RANK, KV_LORA_RANK, QK_ROPE_DIM, CONV_WIDTH, CONV_WIDTH, CONV_WIDTH, MEM_WIDTH, MIX_WIDTH)
IN_COLS = Q_LORA_RANK + KV_LORA_RANK + QK_ROPE_DIM + 3 * CONV_WIDTH + MEM_WIDTH + MIX_WIDTH

kernel_name = "hybrid_mla_shortconv_memory_encoder"


def rmsnorm(x, g):
    x32 = x.astype(jnp.float32)
    y = x32 * lax.rsqrt(jnp.mean(x32 * x32, axis=-1, keepdims=True) + EPS)
    return (y * g.astype(jnp.float32)).astype(x.dtype)


def rope_tables(positions):
    inv_freq = 1.0 / (ROPE_THETA ** (jnp.arange(0, QK_ROPE_DIM, 2, dtype=jnp.float32) / QK_ROPE_DIM))
    ang = positions.astype(jnp.float32)[..., None] * inv_freq
    return jnp.cos(ang), jnp.sin(ang)


def apply_rope(x, cos, sin):
    half = x.shape[-1] // 2
    x32 = x.astype(jnp.float32)
    x1, x2 = x32[..., :half], x32[..., half:]
    return jnp.concatenate([x1 * cos - x2 * sin, x2 * cos + x1 * sin], axis=-1).astype(x.dtype)


def split_cols(z):
    idx = list(np.cumsum(IN_SPLITS)[:-1])
    return jnp.split(z, idx, axis=-1)


def mla_attention(q, k, v):
    b, s, h, dq = q.shape
    nb = s // Q_BLOCK
    scale = QK_HEAD_DIM ** -0.5
    qb = q.reshape(b, nb, Q_BLOCK, h, dq).transpose(1, 0, 2, 3, 4)

    def block(qi):
        sc = jnp.einsum('bqhd,bkhd->bhqk', qi, k).astype(jnp.float32) * scale
        p = jax.nn.softmax(sc, axis=-1).astype(v.dtype)
        return jnp.einsum('bhqk,bkhd->bqhd', p, v)

    o = lax.map(block, qb)
    return o.transpose(1, 0, 2, 3, 4).reshape(b, s, h * V_HEAD_DIM)


def short_gated_conv(gb, gc, xin, w):
    u = gc * xin
    up = jnp.pad(u, ((0, 0), (1, 1), (0, 0)))
    conv = up[:, :-2] * w[0] + up[:, 1:-1] * w[1] + up[:, 2:] * w[2]
    return gb * conv


def memory_attention(q, mem_n, w_mk, w_mv):
    b, m, _ = mem_n.shape
    mk = (mem_n @ w_mk).reshape(b, m, MEM_HEADS, MEM_HEAD_DIM)
    mv = (mem_n @ w_mv).reshape(b, m, MEM_HEADS, MEM_HEAD_DIM)
    sc = jnp.einsum('bshd,bmhd->bhsm', q, mk).astype(jnp.float32) * (MEM_HEAD_DIM ** -0.5)
    p = jax.nn.softmax(sc, axis=-1).astype(mv.dtype)
    o = jnp.einsum('bhsm,bmhd->bshd', p, mv)
    return o.reshape(b, q.shape[1], MEM_WIDTH)


def _fwd_setup_inputs(seed: int = 0) -> dict:
    key = jax.random.key(seed)
    ks = jax.random.split(key, 16)
    f32 = jnp.float32

    def w(k, shape, fan_in):
        return jax.random.normal(k, shape, f32) * (fan_in ** -0.5)

    def gain(k, shape):
        return 1.0 + 0.02 * jax.random.normal(k, shape, f32)

    x = jax.random.normal(ks[0], (BATCH, SEQ, D_MODEL), f32)
    mem = jax.random.normal(ks[1], (BATCH, MEM_TOKENS, D_MODEL), f32)
    offset = jax.random.randint(ks[2], (BATCH, 1), 0, 4096, dtype=jnp.int32)
    positions = (offset + jnp.arange(SEQ, dtype=jnp.int32)[None, :]).astype(jnp.int32)
    return {
        "x": x,
        "mem": mem,
        "positions": positions,
        "pre_norm_g": gain(ks[3], (DEPTH, D_MODEL)),
        "w_in": w(ks[4], (DEPTH, D_MODEL, IN_COLS), D_MODEL),
        "q_norm_g": gain(ks[5], (DEPTH, Q_LORA_RANK)),
        "w_uq": w(ks[6], (DEPTH, Q_LORA_RANK, MLA_HEADS * QK_HEAD_DIM), Q_LORA_RANK),
        "kv_norm_g": gain(ks[7], (DEPTH, KV_LORA_RANK)),
        "w_ukv": w(ks[8], (DEPTH, KV_LORA_RANK, MLA_HEADS * (QK_NOPE_DIM + V_HEAD_DIM)), KV_LORA_RANK),
        "conv_w": w(ks[9], (DEPTH, CONV_K, CONV_WIDTH), CONV_K),
        "mem_norm_g": gain(ks[10], (DEPTH, D_MODEL)),
        "w_mk": w(ks[11], (DEPTH, D_MODEL, MEM_WIDTH), D_MODEL),
        "w_mv": w(ks[12], (DEPTH, D_MODEL, MEM_WIDTH), D_MODEL),
        "w_o": w(ks[13], (DEPTH, MIX_WIDTH, D_MODEL), MIX_WIDTH),
        "post_norm_g": gain(ks[14], (DEPTH, D_MODEL)),
    }


def _fwd_reference(x, mem, positions, pre_norm_g, w_in, q_norm_g, w_uq, kv_norm_g, w_ukv, conv_w,
              mem_norm_g, w_mk, w_mv, w_o, post_norm_g):
    b, s, _ = x.shape
    cos, sin = rope_tables(positions)
    for l in range(DEPTH):
        h = rmsnorm(x, pre_norm_g[l])
        z = h @ w_in[l]
        q_lat, kv_lat, k_pe, gb, gc, xin, q_mem, gate = split_cols(z)

        q = (rmsnorm(q_lat, q_norm_g[l]) @ w_uq[l]).reshape(b, s, MLA_HEADS, QK_HEAD_DIM)
        q = jnp.concatenate([q[..., :QK_NOPE_DIM],
                             apply_rope(q[..., QK_NOPE_DIM:], cos[:, :, None, :], sin[:, :, None, :])], axis=-1)
        kv = (rmsnorm(kv_lat, kv_norm_g[l]) @ w_ukv[l]).reshape(b, s, MLA_HEADS, QK_NOPE_DIM + V_HEAD_DIM)
        k_nope, v = kv[..., :QK_NOPE_DIM], kv[..., QK_NOPE_DIM:]
        k_pe = apply_rope(k_pe, cos, sin)
        k = jnp.concatenate([k_nope, jnp.broadcast_to(k_pe[:, :, None, :], (b, s, MLA_HEADS, QK_ROPE_DIM))], axis=-1)
        a_out = mla_attention(q, k, v)

        c_out = short_gated_conv(gb, gc, xin, conv_w[l])

        mem_n = rmsnorm(mem, mem_norm_g[l])
        m_out = memory_attention(q_mem.reshape(b, s, MEM_HEADS, MEM_HEAD_DIM), mem_n, w_mk[l], w_mv[l])

        y = jnp.concatenate([a_out, c_out, m_out], axis=-1) * jax.nn.silu(gate)
        o = y @ w_o[l]
        x = x + rmsnorm(o, post_norm_g[l])
    return x


import jax as _jax
import jax.numpy as _jnp

TWIN_FORMAT = 'train_step'
FWD_PARAMS = ['x', 'mem', 'positions', 'pre_norm_g', 'w_in', 'q_norm_g', 'w_uq', 'kv_norm_g', 'w_ukv', 'conv_w', 'mem_norm_g', 'w_mk', 'w_mv', 'w_o', 'post_norm_g']
TWIN_WEIGHTS = ['pre_norm_g', 'w_in', 'q_norm_g', 'w_uq', 'kv_norm_g', 'w_ukv', 'conv_w', 'mem_norm_g', 'w_mk', 'w_mv', 'w_o', 'post_norm_g']
TWIN_DIFF_INPUT = 'x'
TWIN_INPUTS = ['x', 'mem', 'positions', 'pre_norm_g', 'w_in', 'q_norm_g', 'w_uq', 'kv_norm_g', 'w_ukv', 'conv_w', 'mem_norm_g', 'w_mk', 'w_mv', 'w_o', 'post_norm_g', 'loss_target', 'm_pre_norm_g', 'm_w_in', 'm_q_norm_g', 'm_w_uq', 'm_kv_norm_g', 'm_w_ukv', 'm_conv_w', 'm_mem_norm_g', 'm_w_mk', 'm_w_mv', 'm_w_o', 'm_post_norm_g', 'v_pre_norm_g', 'v_w_in', 'v_q_norm_g', 'v_w_uq', 'v_kv_norm_g', 'v_w_ukv', 'v_conv_w', 'v_mem_norm_g', 'v_w_mk', 'v_w_mv', 'v_w_o', 'v_post_norm_g']
TWIN_OUTPUTS = ['loss', 'grad_x', 'grad_pre_norm_g', 'grad_w_in', 'grad_q_norm_g', 'grad_w_uq', 'grad_kv_norm_g', 'grad_w_ukv', 'grad_conv_w', 'grad_mem_norm_g', 'grad_w_mk', 'grad_w_mv', 'grad_w_o', 'grad_post_norm_g', 'delta_pre_norm_g', 'delta_w_in', 'delta_q_norm_g', 'delta_w_uq', 'delta_kv_norm_g', 'delta_w_ukv', 'delta_conv_w', 'delta_mem_norm_g', 'delta_w_mk', 'delta_w_mv', 'delta_w_o', 'delta_post_norm_g', 'new_m_pre_norm_g', 'new_m_w_in', 'new_m_q_norm_g', 'new_m_w_uq', 'new_m_kv_norm_g', 'new_m_w_ukv', 'new_m_conv_w', 'new_m_mem_norm_g', 'new_m_w_mk', 'new_m_w_mv', 'new_m_w_o', 'new_m_post_norm_g', 'new_v_pre_norm_g', 'new_v_w_in', 'new_v_q_norm_g', 'new_v_w_uq', 'new_v_kv_norm_g', 'new_v_w_ukv', 'new_v_conv_w', 'new_v_mem_norm_g', 'new_v_w_mk', 'new_v_w_mv', 'new_v_w_o', 'new_v_post_norm_g']
TWIN_LEAF_KINDS = {'loss': 'loss', 'grad_x': 'grad_x', 'grad_pre_norm_g': 'grad_w', 'grad_w_in': 'grad_w', 'grad_q_norm_g': 'grad_w', 'grad_w_uq': 'grad_w', 'grad_kv_norm_g': 'grad_w', 'grad_w_ukv': 'grad_w', 'grad_conv_w': 'grad_w', 'grad_mem_norm_g': 'grad_w', 'grad_w_mk': 'grad_w', 'grad_w_mv': 'grad_w', 'grad_w_o': 'grad_w', 'grad_post_norm_g': 'grad_w', 'delta_pre_norm_g': 'delta_w', 'delta_w_in': 'delta_w', 'delta_q_norm_g': 'delta_w', 'delta_w_uq': 'delta_w', 'delta_kv_norm_g': 'delta_w', 'delta_w_ukv': 'delta_w', 'delta_conv_w': 'delta_w', 'delta_mem_norm_g': 'delta_w', 'delta_w_mk': 'delta_w', 'delta_w_mv': 'delta_w', 'delta_w_o': 'delta_w', 'delta_post_norm_g': 'delta_w', 'new_m_pre_norm_g': 'new_m', 'new_m_w_in': 'new_m', 'new_m_q_norm_g': 'new_m', 'new_m_w_uq': 'new_m', 'new_m_kv_norm_g': 'new_m', 'new_m_w_ukv': 'new_m', 'new_m_conv_w': 'new_m', 'new_m_mem_norm_g': 'new_m', 'new_m_w_mk': 'new_m', 'new_m_w_mv': 'new_m', 'new_m_w_o': 'new_m', 'new_m_post_norm_g': 'new_m', 'new_v_pre_norm_g': 'new_v', 'new_v_w_in': 'new_v', 'new_v_q_norm_g': 'new_v', 'new_v_w_uq': 'new_v', 'new_v_kv_norm_g': 'new_v', 'new_v_w_ukv': 'new_v', 'new_v_conv_w': 'new_v', 'new_v_mem_norm_g': 'new_v', 'new_v_w_mk': 'new_v', 'new_v_w_mv': 'new_v', 'new_v_w_o': 'new_v', 'new_v_post_norm_g': 'new_v'}


def _forward(args):
    return _fwd_reference(*[args[k] for k in FWD_PARAMS])


def _output_shape():
    out = _jax.eval_shape(lambda: _forward(_fwd_setup_inputs(0)))
    return out.shape, out.dtype

N_MICROBATCH = 1
ADAM_LR = 0.001
ADAM_B1 = 0.9
ADAM_B2 = 0.999
ADAM_EPS = 1e-08
ADAM_WD = 0.01
ADAM_STEP = 10
PER_EXAMPLE_BATCH_AXIS = {'x': 0, 'mem': 0, 'positions': 0, 'loss_target': 0}
SHARED_INPUTS = []
_WEIGHT_DTYPES = {'pre_norm_g': _jnp.float32, 'w_in': _jnp.float32, 'q_norm_g': _jnp.float32, 'w_uq': _jnp.float32, 'kv_norm_g': _jnp.float32, 'w_ukv': _jnp.float32, 'conv_w': _jnp.float32, 'mem_norm_g': _jnp.float32, 'w_mk': _jnp.float32, 'w_mv': _jnp.float32, 'w_o': _jnp.float32, 'post_norm_g': _jnp.float32}
MOMENT_SCALE = {'pre_norm_g': 1.037668e+00, 'w_in': 6.631580e-01, 'q_norm_g': 1.616657e-01, 'w_uq': 9.392736e-02, 'kv_norm_g': 3.410593e-01, 'w_ukv': 1.106001e-01, 'conv_w': 9.547356e-01, 'mem_norm_g': 8.694243e-02, 'w_mk': 1.122583e-01, 'w_mv': 1.184192e-01, 'w_o': 4.914048e-01, 'post_norm_g': 1.589667e+01}


def _to_microbatches(a, axis):
    t = _jnp.moveaxis(a, axis, 0)
    t = t.reshape((N_MICROBATCH, t.shape[0] // N_MICROBATCH) + t.shape[1:])
    return _jnp.moveaxis(t, 1, axis + 1)


def setup_inputs(seed: int = 0) -> dict:
    inp = _fwd_setup_inputs(seed)
    key = _jax.random.fold_in(_jax.random.key(seed), 7919)
    shape, _ = _output_shape()
    out = dict(inp)
    out["loss_target"] = _jax.random.normal(_jax.random.fold_in(key, 0), shape, _jnp.float32)
    for i, name in enumerate(TWIN_WEIGHTS):
        w = inp[name].astype(_jnp.float32)
        if MOMENT_SCALE is None:
            s = _jnp.sqrt(_jnp.mean(_jnp.square(w)) + 1e-30)
        else:
            s = MOMENT_SCALE[name]
        km, kv = _jax.random.split(_jax.random.fold_in(key, i + 1))
        out[name] = w
        out["m_" + name] = s * _jax.random.normal(km, w.shape, _jnp.float32)
        out["v_" + name] = (s * s) * _jax.random.uniform(kv, w.shape, _jnp.float32, 0.5, 1.5)
    if N_MICROBATCH > 1:
        for name, axis in PER_EXAMPLE_BATCH_AXIS.items():
            out[name] = _to_microbatches(out[name], axis)
    return {'x': out['x'], 'mem': out['mem'], 'positions': out['positions'], 'pre_norm_g': out['pre_norm_g'], 'w_in': out['w_in'], 'q_norm_g': out['q_norm_g'], 'w_uq': out['w_uq'], 'kv_norm_g': out['kv_norm_g'], 'w_ukv': out['w_ukv'], 'conv_w': out['conv_w'], 'mem_norm_g': out['mem_norm_g'], 'w_mk': out['w_mk'], 'w_mv': out['w_mv'], 'w_o': out['w_o'], 'post_norm_g': out['post_norm_g'], 'loss_target': out['loss_target'], 'm_pre_norm_g': out['m_pre_norm_g'], 'm_w_in': out['m_w_in'], 'm_q_norm_g': out['m_q_norm_g'], 'm_w_uq': out['m_w_uq'], 'm_kv_norm_g': out['m_kv_norm_g'], 'm_w_ukv': out['m_w_ukv'], 'm_conv_w': out['m_conv_w'], 'm_mem_norm_g': out['m_mem_norm_g'], 'm_w_mk': out['m_w_mk'], 'm_w_mv': out['m_w_mv'], 'm_w_o': out['m_w_o'], 'm_post_norm_g': out['m_post_norm_g'], 'v_pre_norm_g': out['v_pre_norm_g'], 'v_w_in': out['v_w_in'], 'v_q_norm_g': out['v_q_norm_g'], 'v_w_uq': out['v_w_uq'], 'v_kv_norm_g': out['v_kv_norm_g'], 'v_w_ukv': out['v_w_ukv'], 'v_conv_w': out['v_conv_w'], 'v_mem_norm_g': out['v_mem_norm_g'], 'v_w_mk': out['v_w_mk'], 'v_w_mv': out['v_w_mv'], 'v_w_o': out['v_w_o'], 'v_post_norm_g': out['v_post_norm_g']}


def _loss(weights, diff, rest, loss_target):
    with _jax.named_scope("forward"):
        args = {**rest, TWIN_DIFF_INPUT: diff, **{k: w.astype(_WEIGHT_DTYPES[k]) for k, w in weights.items()}}
        y = _forward(args)
    with _jax.named_scope("loss_head"):
        err = _jnp.square(y.astype(_jnp.float32) - loss_target)
        return 0.5 * _jnp.sum(_jnp.mean(err, axis=-1)) if err.ndim else 0.5 * err


def _adamw(w, g, m, v):
    m = ADAM_B1 * m + (1.0 - ADAM_B1) * g
    v = ADAM_B2 * v + (1.0 - ADAM_B2) * _jnp.square(g)
    m_hat = m / (1.0 - ADAM_B1 ** ADAM_STEP)
    v_hat = v / (1.0 - ADAM_B2 ** ADAM_STEP)
    delta = -ADAM_LR * (m_hat / (_jnp.sqrt(v_hat) + ADAM_EPS) + ADAM_WD * w)
    return delta, m, v


def reference(x, mem, positions, pre_norm_g, w_in, q_norm_g, w_uq, kv_norm_g, w_ukv, conv_w, mem_norm_g, w_mk, w_mv, w_o, post_norm_g, loss_target, m_pre_norm_g, m_w_in, m_q_norm_g, m_w_uq, m_kv_norm_g, m_w_ukv, m_conv_w, m_mem_norm_g, m_w_mk, m_w_mv, m_w_o, m_post_norm_g, v_pre_norm_g, v_w_in, v_q_norm_g, v_w_uq, v_kv_norm_g, v_w_ukv, v_conv_w, v_mem_norm_g, v_w_mk, v_w_mv, v_w_o, v_post_norm_g):
    given = dict(x=x, mem=mem, positions=positions, pre_norm_g=pre_norm_g, w_in=w_in, q_norm_g=q_norm_g, w_uq=w_uq, kv_norm_g=kv_norm_g, w_ukv=w_ukv, conv_w=conv_w, mem_norm_g=mem_norm_g, w_mk=w_mk, w_mv=w_mv, w_o=w_o, post_norm_g=post_norm_g, loss_target=loss_target, m_pre_norm_g=m_pre_norm_g, m_w_in=m_w_in, m_q_norm_g=m_q_norm_g, m_w_uq=m_w_uq, m_kv_norm_g=m_kv_norm_g, m_w_ukv=m_w_ukv, m_conv_w=m_conv_w, m_mem_norm_g=m_mem_norm_g, m_w_mk=m_w_mk, m_w_mv=m_w_mv, m_w_o=m_w_o, m_post_norm_g=m_post_norm_g, v_pre_norm_g=v_pre_norm_g, v_w_in=v_w_in, v_q_norm_g=v_q_norm_g, v_w_uq=v_w_uq, v_kv_norm_g=v_kv_norm_g, v_w_ukv=v_w_ukv, v_conv_w=v_conv_w, v_mem_norm_g=v_mem_norm_g, v_w_mk=v_w_mk, v_w_mv=v_w_mv, v_w_o=v_w_o, v_post_norm_g=v_post_norm_g)
    weights = {n: given[n] for n in TWIN_WEIGHTS}
    shared = {n: given[n] for n in SHARED_INPUTS}
    per_example = {n: given[n] for n in ['x', 'mem', 'positions']}
    grad_fn = _jax.value_and_grad(_loss, argnums=(0, 1))

    def one_microbatch(ex, loss_target):
        ex = dict(ex)
        diff = ex.pop(TWIN_DIFF_INPUT)
        return grad_fn(weights, diff, {**shared, **ex}, loss_target)

    if N_MICROBATCH == 1:
        loss, (grad_w, grad_x) = one_microbatch(per_example, given["loss_target"])
    else:
        def body(carry, xs):
            loss_sum, grad_sum = carry
            l_k, (gw_k, gx_k) = one_microbatch(xs[0], xs[1])
            with _jax.named_scope("update"):
                return (loss_sum + l_k, _jax.tree.map(_jnp.add, grad_sum, gw_k)), gx_k

        init = (_jnp.zeros((), _jnp.float32), _jax.tree.map(_jnp.zeros_like, weights))
        (loss, grad_w), grad_x = _jax.lax.scan(body, init, (per_example, given["loss_target"]))
    with _jax.named_scope("update"):
        delta_w, new_m, new_v = {}, {}, {}
        for n in TWIN_WEIGHTS:
            delta_w[n], new_m[n], new_v[n] = _adamw(weights[n], grad_w[n], given["m_" + n], given["v_" + n])
    return (loss, grad_x, *[grad_w[n] for n in TWIN_WEIGHTS], *[delta_w[n] for n in TWIN_WEIGHTS],
            *[new_m[n] for n in TWIN_WEIGHTS], *[new_v[n] for n in TWIN_WEIGHTS])
```

```python
import functools

import jax
import jax.numpy as jnp
from jax import lax
from jax.experimental import pallas as pl
from jax.experimental.pallas import tpu as pltpu

F32 = jnp.float32
BF16 = jnp.bfloat16
MESH_ID = pl.DeviceIdType.MESH

D_MODEL = 2048
EPS = 1e-6
ROPE_THETA = 10000.0
MLA_HEADS = 8
NOPE = 128
ROPE = 64
HALF_ROPE = ROPE // 2
QK_HEAD = NOPE + ROPE
V_HEAD = 128
Q_RANK = 512
KV_RANK = 256
CONV_W = 512
MEM_HEADS = 4
MEM_HEAD = 128
MEM_W = MEM_HEADS * MEM_HEAD
MLA_W = MLA_HEADS * V_HEAD
MIX_W = MLA_W + CONV_W + MEM_W
IN_COLS = Q_RANK + KV_RANK + ROPE + 3 * CONV_W + MEM_W + MIX_W
N_CHIPS = 4
N_DEV = 8

LANES = 128
VMEM_LIMIT_BYTES = 56 * 1024 * 1024

QPAD = 2 * LANES
Z_GATE = 0
Z_QLAT = Z_GATE + MIX_W
Z_KVLAT = Z_QLAT + Q_RANK
Z_KPE = Z_KVLAT + KV_RANK
Z_GB = Z_KPE + LANES
Z_GC = Z_GB + CONV_W
Z_XIN = Z_GC + CONV_W
Z_QMEM = Z_XIN + CONV_W
Z_COLS = Z_QMEM + MEM_W

ADAM_LR = 0.001
ADAM_B1 = 0.9
ADAM_B2 = 0.999
ADAM_EPS = 1e-08
ADAM_WD = 0.01
ADAM_STEP = 10


def _tile(dim, cap, unit):
    if dim <= cap:
        return dim
    t = (cap // unit) * unit
    while t >= unit:
        if dim % t == 0:
            return t
        t -= unit
    raise ValueError(f"no tile of {dim} under {cap} in units of {unit}")


def _params(*semantics):
    return pltpu.CompilerParams(dimension_semantics=semantics, vmem_limit_bytes=VMEM_LIMIT_BYTES)


def _matmul(a, b, mode, out_dtype, name, tm_cap=512, tn_cap=1024, tk_cap=2048):
    if mode == "nn":
        (m, k), (k2, n) = a.shape, b.shape
    elif mode == "nt":
        (m, k), (n, k2) = a.shape, b.shape
    else:
        (k, m), (k2, n) = a.shape, b.shape
    assert k == k2, (a.shape, b.shape, mode)
    tm = _tile(m, tm_cap, LANES if mode == "tn" else 16)
    tn = _tile(n, tn_cap, LANES)
    tk = _tile(k, tk_cap, LANES if mode != "tn" else 16)
    nk = k // tk
    if mode == "nn":
        a_spec = pl.BlockSpec((tm, tk), lambda i, j, kk: (i, kk))
        b_spec = pl.BlockSpec((tk, tn), lambda i, j, kk: (kk, j))
        dims = (((1,), (0,)), ((), ()))
    elif mode == "nt":
        a_spec = pl.BlockSpec((tm, tk), lambda i, j, kk: (i, kk))
        b_spec = pl.BlockSpec((tn, tk), lambda i, j, kk: (j, kk))
        dims = (((1,), (1,)), ((), ()))
    else:
        a_spec = pl.BlockSpec((tk, tm), lambda i, j, kk: (kk, i))
        b_spec = pl.BlockSpec((tk, tn), lambda i, j, kk: (kk, j))
        dims = (((0,), (0,)), ((), ()))

    def body(a_ref, b_ref, o_ref, *scratch):
        part = lax.dot_general(a_ref[...].astype(BF16), b_ref[...].astype(BF16), dims, preferred_element_type=F32)
        if nk == 1:
            o_ref[...] = part.astype(o_ref.dtype)
            return
        (acc_ref,) = scratch
        kk = pl.program_id(2)

        @pl.when(kk == 0)
        def _():
            acc_ref[...] = part

        @pl.when(kk > 0)
        def _():
            acc_ref[...] += part

        @pl.when(kk == nk - 1)
        def _():
            o_ref[...] = acc_ref[...].astype(o_ref.dtype)

    return pl.pallas_call(
        body,
        grid=(m // tm, n // tn, nk),
        in_specs=[a_spec, b_spec],
        out_specs=pl.BlockSpec((tm, tn), lambda i, j, kk: (i, j)),
        out_shape=jax.ShapeDtypeStruct((m, n), out_dtype),
        scratch_shapes=[] if nk == 1 else [pltpu.VMEM((tm, tn), F32)],
        compiler_params=_params("parallel", "parallel", "arbitrary"),
        name=name,
    )(a, b)


def _rmsnorm_fwd(x, gain, col0, width, name):
    rows = x.shape[0]
    tr = _tile(rows, 512, 16)
    cb = col0 // width
    assert cb * width == col0

    def body(x_ref, g_ref, o_ref):
        xv = x_ref[...]
        r = lax.rsqrt(jnp.mean(xv * xv, axis=-1, keepdims=True) + EPS)
        o_ref[...] = (xv * r * g_ref[...]).astype(o_ref.dtype)

    return pl.pallas_call(
        body,
        grid=(rows // tr,),
        in_specs=[pl.BlockSpec((tr, width), lambda i: (i, cb)), pl.BlockSpec((1, width), lambda i: (0, 0))],
        out_specs=pl.BlockSpec((tr, width), lambda i: (i, 0)),
        out_shape=jax.ShapeDtypeStruct((rows, width), BF16),
        compiler_params=_params("parallel"),
        name=name,
    )(x, gain)


def _rmsnorm_bwd(x, gain, dy, resid, col0, width, out_dtype, name):
    rows = x.shape[0]
    tr = _tile(rows, 256, 16)
    cb = col0 // width
    assert cb * width == col0
    has_resid = resid is not None

    def body(*refs):
        if has_resid:
            x_ref, g_ref, dy_ref, res_ref, dx_ref, dg_ref = refs
        else:
            x_ref, g_ref, dy_ref, dx_ref, dg_ref = refs
        i = pl.program_id(0)
        xv = x_ref[...]
        dyv = dy_ref[...].astype(F32)
        r = lax.rsqrt(jnp.mean(xv * xv, axis=-1, keepdims=True) + EPS)
        xr = xv * r
        dyg = dyv * g_ref[...]
        c = jnp.mean(dyg * xr, axis=-1, keepdims=True)
        dx = r * (dyg - xr * c)
        if has_resid:
            dx = dx + res_ref[...]
        dx_ref[...] = dx.astype(dx_ref.dtype)
        part = jnp.sum(dyv * xr, axis=0, keepdims=True)

        @pl.when(i == 0)
        def _():
            dg_ref[...] = part

        @pl.when(i > 0)
        def _():
            dg_ref[...] += part

    row_spec = pl.BlockSpec((tr, width), lambda i: (i, 0))
    in_specs = [pl.BlockSpec((tr, width), lambda i: (i, cb)), pl.BlockSpec((1, width), lambda i: (0, 0)), row_spec]
    args = [x, gain, dy]
    if has_resid:
        in_specs.append(row_spec)
        args.append(resid)
    return pl.pallas_call(
        body,
        grid=(rows // tr,),
        in_specs=in_specs,
        out_specs=[row_spec, pl.BlockSpec((1, width), lambda i: (0, 0))],
        out_shape=[jax.ShapeDtypeStruct((rows, width), out_dtype), jax.ShapeDtypeStruct((1, width), F32)],
        compiler_params=_params("arbitrary"),
        name=name,
    )(*args)


def _post_norm_residual(x, o, gain, name):
    rows, width = x.shape
    tr = _tile(rows, 256, 8)

    def body(x_ref, o_ref, g_ref, out_ref):
        ov = o_ref[...]
        r = lax.rsqrt(jnp.mean(ov * ov, axis=-1, keepdims=True) + EPS)
        out_ref[...] = x_ref[...] + ov * r * g_ref[...]

    row_spec = pl.BlockSpec((tr, width), lambda i: (i, 0))
    return pl.pallas_call(
        body,
        grid=(rows // tr,),
        in_specs=[row_spec, row_spec, pl.BlockSpec((1, width), lambda i: (0, 0))],
        out_specs=row_spec,
        out_shape=jax.ShapeDtypeStruct((rows, width), F32),
        compiler_params=_params("parallel"),
        name=name,
    )(x, o, gain)


def _rope(x, tab_c, tab_a, tab_b, col0, width, heads, name):
    rows = x.shape[0]
    tr = _tile(rows, 512, 16)
    cb = col0 // width
    assert cb * width == col0

    def body(x_ref, c_ref, a_ref, b_ref, o_ref):
        xv = x_ref[...].astype(F32)
        up = pltpu.roll(xv, width - HALF_ROPE, 1)
        down = pltpu.roll(xv, HALF_ROPE, 1)
        o_ref[...] = (xv * c_ref[...] + up * a_ref[...] + down * b_ref[...]).astype(o_ref.dtype)

    tab_spec = pl.BlockSpec((tr, width), lambda i, h: (i, 0))
    return pl.pallas_call(
        body,
        grid=(rows // tr, heads),
        in_specs=[pl.BlockSpec((tr, width), lambda i, h: (i, cb + h)), tab_spec, tab_spec, tab_spec],
        out_specs=pl.BlockSpec((tr, width), lambda i, h: (i, h)),
        out_shape=jax.ShapeDtypeStruct((rows, heads * width), BF16),
        compiler_params=_params("parallel", "parallel"),
        name=name,
    )(x, tab_c, tab_a, tab_b)


def _kpe_grad(dkb, tab_c, tab_a, tab_b, heads, name):
    rows = dkb.shape[0]
    tr = _tile(rows, 512, 16)

    def body(d_ref, c_ref, a_ref, b_ref, o_ref):
        acc = d_ref[:, 0:LANES]
        for h in range(1, heads):
            acc = acc + d_ref[:, h * LANES:(h + 1) * LANES]
        up = pltpu.roll(acc, LANES - HALF_ROPE, 1)
        down = pltpu.roll(acc, HALF_ROPE, 1)
        o_ref[...] = (acc * c_ref[...] + up * a_ref[...] + down * b_ref[...]).astype(o_ref.dtype)

    tab_spec = pl.BlockSpec((tr, LANES), lambda i: (i, 0))
    return pl.pallas_call(
        body,
        grid=(rows // tr,),
        in_specs=[pl.BlockSpec((tr, heads * LANES), lambda i: (i, 0)), tab_spec, tab_spec, tab_spec],
        out_specs=tab_spec,
        out_shape=jax.ShapeDtypeStruct((rows, LANES), BF16),
        compiler_params=_params("parallel"),
        name=name,
    )(dkb, tab_c, tab_a, tab_b)


def _attn_fwd(q, ka, kb, v, heads, q_w, q_cb, ka_cb, v_cb, scale, tq_cap, name):
    s_q, s_k = q.shape[0], ka.shape[0]
    tq = _tile(s_q, tq_cap, 16)
    nq = s_q // tq
    has_kb = kb is not None

    def body(*refs):
        if has_kb:
            q_ref, ka_ref, kb_ref, v_ref, o_ref, lse_ref, k_scr = refs

            @pl.when(pl.program_id(1) == 0)
            def _():
                k_scr[:, 0:LANES] = ka_ref[...].astype(BF16)
                k_scr[:, LANES:2 * LANES] = kb_ref[...].astype(BF16)

            kmat = k_scr[...]
        else:
            q_ref, ka_ref, v_ref, o_ref, lse_ref = refs
            kmat = ka_ref[...].astype(BF16)
        s = lax.dot_general(q_ref[...].astype(BF16), kmat, (((1,), (1,)), ((), ())), preferred_element_type=F32) * scale
        m = jnp.max(s, axis=-1, keepdims=True)
        p = jnp.exp(s - m)
        l = jnp.sum(p, axis=-1, keepdims=True)
        o = jnp.dot(p.astype(BF16), v_ref[...].astype(BF16), preferred_element_type=F32)
        o_ref[...] = (o * (1.0 / l)).astype(o_ref.dtype)
        lse_ref[...] = jnp.broadcast_to(m + jnp.log(l), lse_ref.shape)

    in_specs = [pl.BlockSpec((tq, q_w), lambda h, i: (i, q_cb + h)), pl.BlockSpec((s_k, LANES), lambda h, i: (0, ka_cb + h))]
    args = [q, ka]
    if has_kb:
        in_specs.append(pl.BlockSpec((s_k, LANES), lambda h, i: (0, 0)))
        args.append(kb)
    in_specs.append(pl.BlockSpec((s_k, LANES), lambda h, i: (0, v_cb + h)))
    args.append(v)
    out_spec = pl.BlockSpec((tq, LANES), lambda h, i: (i, h))
    return pl.pallas_call(
        body,
        grid=(heads, nq),
        in_specs=in_specs,
        out_specs=[out_spec, out_spec],
        out_shape=[jax.ShapeDtypeStruct((s_q, heads * LANES), BF16), jax.ShapeDtypeStruct((s_q, heads * LANES), F32)],
        scratch_shapes=[pltpu.VMEM((s_k, 2 * LANES), BF16)] if has_kb else [],
        compiler_params=_params("arbitrary", "arbitrary"),
        name=name,
    )(*args)


def _attn_bwd(q, ka, kb, v, o, do, lse, heads, q_w, q_cb, ka_cb, v_cb, o_cb, scale, tq_cap, dq_dtype, name):
    s_q, s_k = q.shape[0], ka.shape[0]
    tq = _tile(s_q, tq_cap, 16)
    nq = s_q // tq
    has_kb = kb is not None

    def body(*refs):
        if has_kb:
            q_ref, ka_ref, kb_ref, v_ref, o_ref, do_ref, lse_ref, dq_ref, dka_ref, dkb_ref, dv_ref, k_scr, dk_acc, dv_acc = refs
        else:
            q_ref, ka_ref, v_ref, o_ref, do_ref, lse_ref, dq_ref, dka_ref, dv_ref, dk_acc, dv_acc = refs
        i = pl.program_id(1)

        @pl.when(i == 0)
        def _():
            dk_acc[...] = jnp.zeros_like(dk_acc)
            dv_acc[...] = jnp.zeros_like(dv_acc)
            if has_kb:
                k_scr[:, 0:LANES] = ka_ref[...].astype(BF16)
                k_scr[:, LANES:2 * LANES] = kb_ref[...].astype(BF16)

        kmat = k_scr[...] if has_kb else ka_ref[...].astype(BF16)
        qv = q_ref[...].astype(BF16)
        dov = do_ref[...].astype(BF16)
        delta = jnp.sum(dov.astype(F32) * o_ref[...].astype(F32), axis=-1, keepdims=True)
        s = lax.dot_general(qv, kmat, (((1,), (1,)), ((), ())), preferred_element_type=F32) * scale
        p = jnp.exp(s - lse_ref[:, 0:1])
        dp = lax.dot_general(dov, v_ref[...].astype(BF16), (((1,), (1,)), ((), ())), preferred_element_type=F32)
        ds = (p * (dp - delta) * scale).astype(BF16)
        dq_ref[...] = jnp.dot(ds, kmat, preferred_element_type=F32).astype(dq_ref.dtype)
        dk_acc[...] += lax.dot_general(ds, qv, (((0,), (0,)), ((), ())), preferred_element_type=F32)
        dv_acc[...] += lax.dot_general(p.astype(BF16), dov, (((0,), (0,)), ((), ())), preferred_element_type=F32)

        @pl.when(i == nq - 1)
        def _():
            dka_ref[...] = dk_acc[:, 0:LANES].astype(dka_ref.dtype)
            if has_kb:
                dkb_ref[...] = dk_acc[:, LANES:2 * LANES]
            dv_ref[...] = dv_acc[...].astype(dv_ref.dtype)

    key_spec = lambda cb: pl.BlockSpec((s_k, LANES), lambda h, i: (0, cb + h))
    row_spec = lambda cb: pl.BlockSpec((tq, LANES), lambda h, i: (i, cb + h))
    in_specs = [pl.BlockSpec((tq, q_w), lambda h, i: (i, q_cb + h)), key_spec(ka_cb)]
    args = [q, ka]
    if has_kb:
        in_specs.append(pl.BlockSpec((s_k, LANES), lambda h, i: (0, 0)))
        args.append(kb)
    in_specs += [key_spec(v_cb), row_spec(o_cb), row_spec(o_cb), row_spec(0)]
    args += [v, o, do, lse]
    out_specs = [pl.BlockSpec((tq, q_w), lambda h, i: (i, h)), key_spec(0)]
    out_shape = [jax.ShapeDtypeStruct((s_q, heads * q_w), dq_dtype), jax.ShapeDtypeStruct((s_k, heads * LANES), BF16)]
    scratch = []
    if has_kb:
        out_specs.append(key_spec(0))
        out_shape.append(jax.ShapeDtypeStruct((s_k, heads * LANES), F32))
        scratch.append(pltpu.VMEM((s_k, 2 * LANES), BF16))
    out_specs.append(key_spec(0))
    out_shape.append(jax.ShapeDtypeStruct((s_k, heads * LANES), BF16))
    scratch += [pltpu.VMEM((s_k, q_w), F32), pltpu.VMEM((s_k, LANES), F32)]
    return pl.pallas_call(
        body,
        grid=(heads, nq),
        in_specs=in_specs,
        out_specs=out_specs,
        out_shape=out_shape,
        scratch_shapes=scratch,
        compiler_params=_params("arbitrary", "arbitrary"),
        name=name,
    )(*args)


def _shift_rows(u, rows):
    t = lax.broadcasted_iota(jnp.int32, u.shape, 0)
    prev = jnp.where(t == 0, 0.0, pltpu.roll(u, 1, 0))
    nxt = jnp.where(t == rows - 1, 0.0, pltpu.roll(u, rows - 1, 0))
    return prev, nxt


def _conv_fwd(z, conv_w, name):
    rows = z.shape[0]
    nblk = CONV_W // LANES

    def body(gb_ref, gc_ref, xin_ref, w_ref, o_ref):
        u = gc_ref[...] * xin_ref[...]
        prev, nxt = _shift_rows(u, rows)
        conv = prev * w_ref[0:1, :] + u * w_ref[1:2, :] + nxt * w_ref[2:3, :]
        o_ref[...] = (gb_ref[...] * conv).astype(o_ref.dtype)

    col = lambda c0: pl.BlockSpec((rows, LANES), lambda j: (0, c0 // LANES + j))
    return pl.pallas_call(
        body,
        grid=(nblk,),
        in_specs=[col(Z_GB), col(Z_GC), col(Z_XIN), pl.BlockSpec((3, LANES), lambda j: (0, j))],
        out_specs=col(0),
        out_shape=jax.ShapeDtypeStruct((rows, CONV_W), BF16),
        compiler_params=_params("parallel"),
        name=name,
    )(z, z, z, conv_w)


def _conv_bwd(z, conv_w, dcat, name):
    rows = z.shape[0]
    nblk = CONV_W // LANES

    def body(gb_ref, gc_ref, xin_ref, w_ref, dc_ref, dgb_ref, dgc_ref, dxin_ref, dw_ref):
        gc = gc_ref[...]
        xin = xin_ref[...]
        dc = dc_ref[...].astype(F32)
        u = gc * xin
        prev, nxt = _shift_rows(u, rows)
        w0, w1, w2 = w_ref[0:1, :], w_ref[1:2, :], w_ref[2:3, :]
        conv = prev * w0 + u * w1 + nxt * w2
        dgb_ref[...] = (dc * conv).astype(dgb_ref.dtype)
        dconv = dc * gb_ref[...]
        dw_ref[0:1, :] = jnp.sum(dconv * prev, axis=0, keepdims=True)
        dw_ref[1:2, :] = jnp.sum(dconv * u, axis=0, keepdims=True)
        dw_ref[2:3, :] = jnp.sum(dconv * nxt, axis=0, keepdims=True)
        dprev, dnxt = _shift_rows(dconv, rows)
        du = dnxt * w0 + dconv * w1 + dprev * w2
        dgc_ref[...] = (du * xin).astype(dgc_ref.dtype)
        dxin_ref[...] = (du * gc).astype(dxin_ref.dtype)

    col = lambda c0: pl.BlockSpec((rows, LANES), lambda j: (0, c0 // LANES + j))
    w_spec = pl.BlockSpec((3, LANES), lambda j: (0, j))
    piece = jax.ShapeDtypeStruct((rows, CONV_W), BF16)
    return pl.pallas_call(
        body,
        grid=(nblk,),
        in_specs=[col(Z_GB), col(Z_GC), col(Z_XIN), w_spec, col(MLA_W)],
        out_specs=[col(0), col(0), col(0), w_spec],
        out_shape=[piece, piece, piece, jax.ShapeDtypeStruct((3, CONV_W), F32)],
        compiler_params=_params("parallel"),
        name=name,
    )(z, z, z, conv_w, dcat)


def _gate_fwd(cat, z, name):
    rows = cat.shape[0]
    tr = _tile(rows, 512, 16)
    tc = 512
    g0 = Z_GATE // tc

    def body(c_ref, g_ref, y_ref):
        g = g_ref[...]
        y_ref[...] = (c_ref[...].astype(F32) * (g * jax.nn.sigmoid(g))).astype(y_ref.dtype)

    blk = pl.BlockSpec((tr, tc), lambda i, j: (i, j))
    return pl.pallas_call(
        body,
        grid=(rows // tr, MIX_W // tc),
        in_specs=[blk, pl.BlockSpec((tr, tc), lambda i, j: (i, g0 + j))],
        out_specs=blk,
        out_shape=jax.ShapeDtypeStruct((rows, MIX_W), BF16),
        compiler_params=_params("parallel", "parallel"),
        name=name,
    )(cat, z)


def _gate_bwd(dy, cat, z, name):
    rows = cat.shape[0]
    tr = _tile(rows, 512, 16)
    tc = 512
    g0 = Z_GATE // tc

    def body(dy_ref, c_ref, g_ref, dcat_ref, dgate_ref):
        g = g_ref[...]
        sg = jax.nn.sigmoid(g)
        dyv = dy_ref[...].astype(F32)
        dcat_ref[...] = (dyv * (g * sg)).astype(dcat_ref.dtype)
        dgate_ref[...] = (dyv * c_ref[...].astype(F32) * (sg * (1.0 + g * (1.0 - sg)))).astype(dgate_ref.dtype)

    blk = pl.BlockSpec((tr, tc), lambda i, j: (i, j))
    out = jax.ShapeDtypeStruct((rows, MIX_W), BF16)
    return pl.pallas_call(
        body,
        grid=(rows // tr, MIX_W // tc),
        in_specs=[blk, blk, pl.BlockSpec((tr, tc), lambda i, j: (i, g0 + j))],
        out_specs=[blk, blk],
        out_shape=[out, out],
        compiler_params=_params("parallel", "parallel"),
        name=name,
    )(dy, cat, z)


def _loss_head(y, target, name):
    rows, width = y.shape
    tr = _tile(rows, 256, 8)

    def body(y_ref, t_ref, g_ref, loss_ref):
        i = pl.program_id(0)
        d = y_ref[...] - t_ref[...]
        g_ref[...] = d / width
        part = 0.5 * jnp.sum(jnp.mean(d * d, axis=-1, keepdims=True), axis=0, keepdims=True)
        part = jnp.broadcast_to(part, loss_ref.shape)

        @pl.when(i == 0)
        def _():
            loss_ref[...] = part

        @pl.when(i > 0)
        def _():
            loss_ref[...] += part

    row_spec = pl.BlockSpec((tr, width), lambda i: (i, 0))
    return pl.pallas_call(
        body,
        grid=(rows // tr,),
        in_specs=[row_spec, row_spec],
        out_specs=[row_spec, pl.BlockSpec((1, LANES), lambda i: (0, 0))],
        out_shape=[jax.ShapeDtypeStruct((rows, width), F32), jax.ShapeDtypeStruct((1, LANES), F32)],
        compiler_params=_params("arbitrary"),
        name=name,
    )(y, target)


CHIP_FLIPS = ((1, 0), (0, 1), (1, 1))
ANY = pl.BlockSpec(memory_space=pl.ANY)


def _chip_exchange(arrays, scatter, name):
    n = len(arrays)

    def body(*refs):
        ins, outs = refs[:n], refs[n:2 * n]
        send_sems, recv_sems, local_sems = refs[2 * n:]
        x, y, c = lax.axis_index("x"), lax.axis_index("y"), lax.axis_index("c")
        me = 2 * x + y
        copies = []
        for a in range(n):
            mine = ins[a].at[me] if scatter else ins[a]
            local = pltpu.make_async_copy(mine, outs[a].at[me], local_sems.at[a])
            local.start()
            copies.append(local)
            for k, (fx, fy) in enumerate(CHIP_FLIPS):
                px, py = (x + fx) % 2, (y + fy) % 2
                src = ins[a].at[2 * px + py] if scatter else ins[a]
                cp = pltpu.make_async_remote_copy(
                    src_ref=src, dst_ref=outs[a].at[me],
                    send_sem=send_sems.at[3 * a + k], recv_sem=recv_sems.at[3 * a + k],
                    device_id=(px, py, c), device_id_type=MESH_ID)
                cp.start()
                copies.append(cp)
        for cp in copies:
            cp.wait()

    out_shape = [jax.ShapeDtypeStruct(v.shape if scatter else (N_CHIPS,) + v.shape, v.dtype) for v in arrays]
    return pl.pallas_call(
        body,
        in_specs=[ANY] * n,
        out_specs=[ANY] * n,
        out_shape=out_shape,
        scratch_shapes=[pltpu.SemaphoreType.DMA((3 * n,)), pltpu.SemaphoreType.DMA((3 * n,)), pltpu.SemaphoreType.DMA((n,))],
        name=name,
    )(*arrays)


def _sibling_exchange(arrays, name):
    n = len(arrays)

    def body(*refs):
        ins, outs = refs[:n], refs[n:2 * n]
        send_sems, recv_sems = refs[2 * n:]
        sibling = (lax.axis_index("x"), lax.axis_index("y"), 1 - lax.axis_index("c"))
        copies = []
        for a in range(n):
            cp = pltpu.make_async_remote_copy(
                src_ref=ins[a], dst_ref=outs[a], send_sem=send_sems.at[a], recv_sem=recv_sems.at[a],
                device_id=sibling, device_id_type=MESH_ID)
            cp.start()
            copies.append(cp)
        for cp in copies:
            cp.wait()

    return pl.pallas_call(
        body,
        in_specs=[ANY] * n,
        out_specs=[ANY] * n,
        out_shape=[jax.ShapeDtypeStruct(v.shape, v.dtype) for v in arrays],
        scratch_shapes=[pltpu.SemaphoreType.DMA((n,)), pltpu.SemaphoreType.DMA((n,))],
        name=name,
    )(*arrays)


DEVICE_FLIPS = tuple((fx, fy, fc) for fx in (0, 1) for fy in (0, 1) for fc in (0, 1))[1:]


def _gather_all(v, name):
    def body(v_ref, out_ref, send_sems, recv_sems, local_sem):
        x, y, c = lax.axis_index("x"), lax.axis_index("y"), lax.axis_index("c")
        me = 4 * x + 2 * y + c
        local = pltpu.make_async_copy(v_ref, out_ref.at[me], local_sem)
        local.start()
        copies = [local]
        for k, (fx, fy, fc) in enumerate(DEVICE_FLIPS):
            cp = pltpu.make_async_remote_copy(
                src_ref=v_ref, dst_ref=out_ref.at[me], send_sem=send_sems.at[k], recv_sem=recv_sems.at[k],
                device_id=((x + fx) % 2, (y + fy) % 2, (c + fc) % 2), device_id_type=MESH_ID)
            cp.start()
            copies.append(cp)
        for cp in copies:
            cp.wait()

    return pl.pallas_call(
        body,
        in_specs=[ANY],
        out_specs=ANY,
        out_shape=jax.ShapeDtypeStruct((N_DEV,) + v.shape, v.dtype),
        scratch_shapes=[pltpu.SemaphoreType.DMA((N_DEV - 1,)), pltpu.SemaphoreType.DMA((N_DEV - 1,)), pltpu.SemaphoreType.DMA],
        name=name,
    )(v)


def _sum_slots(parts, name):
    n, rows, cols = parts.shape
    tr = _tile(rows, 256, 16)

    def body(p_ref, o_ref):
        acc = p_ref[0].astype(F32)
        for k in range(1, n):
            acc = acc + p_ref[k].astype(F32)
        o_ref[...] = acc

    return pl.pallas_call(
        body,
        grid=(rows // tr,),
        in_specs=[pl.BlockSpec((n, tr, cols), lambda i: (0, i, 0))],
        out_specs=pl.BlockSpec((tr, cols), lambda i: (i, 0)),
        out_shape=jax.ShapeDtypeStruct((rows, cols), F32),
        compiler_params=_params("parallel"),
        name=name,
    )(parts)


def _adamw(w, g_a, g_b, m, v, name):
    rows, cols = w.shape
    tr = _tile(rows, 256, 8)
    has_b = g_b is not None

    def body(*refs):
        if has_b:
            w_ref, ga_ref, gb_ref, m_ref, v_ref, g_out, d_out, m_out, v_out = refs
            g = ga_ref[...] + gb_ref[...]
        else:
            w_ref, ga_ref, m_ref, v_ref, g_out, d_out, m_out, v_out = refs
            g = ga_ref[...]
        m_new = ADAM_B1 * m_ref[...] + (1.0 - ADAM_B1) * g
        v_new = ADAM_B2 * v_ref[...] + (1.0 - ADAM_B2) * jnp.square(g)
        m_hat = m_new / (1.0 - ADAM_B1 ** ADAM_STEP)
        v_hat = v_new / (1.0 - ADAM_B2 ** ADAM_STEP)
        g_out[...] = g
        d_out[...] = -ADAM_LR * (m_hat / (jnp.sqrt(v_hat) + ADAM_EPS) + ADAM_WD * w_ref[...])
        m_out[...] = m_new
        v_out[...] = v_new

    blk = pl.BlockSpec((tr, cols), lambda i: (i, 0))
    args = [w, g_a] + ([g_b] if has_b else []) + [m, v]
    out = jax.ShapeDtypeStruct((rows, cols), F32)
    return pl.pallas_call(
        body,
        grid=(rows // tr,),
        in_specs=[blk] * len(args),
        out_specs=[blk] * 4,
        out_shape=[out] * 4,
        compiler_params=_params("parallel"),
        name=name,
    )(*args)


def _cols_from_shards(g):
    _, l, r, c = g.shape
    return jnp.transpose(g, (1, 2, 0, 3)).reshape(l, r, N_CHIPS * c)


def _rows_from_shards(g):
    _, l, r, c = g.shape
    return jnp.transpose(g, (1, 0, 2, 3)).reshape(l, N_CHIPS * r, c)


def _cols_to_shards(full):
    l, r, c4 = full.shape
    c = c4 // N_CHIPS
    return jnp.transpose(full.reshape(l, r, N_CHIPS, c), (2, 0, 1, 3)).reshape(N_CHIPS, l * r, c)


def _rows_to_shards(full):
    l, r4, c = full.shape
    r = r4 // N_CHIPS
    return jnp.transpose(full.reshape(l, N_CHIPS, r, c), (1, 0, 2, 3)).reshape(N_CHIPS, l * r, c)


IN_ORDER = (Q_RANK, KV_RANK, ROPE, CONV_W, CONV_W, CONV_W, MEM_W, MIX_W)


def _w_in_to_z_layout(w_in):
    edges = [0]
    for width in IN_ORDER:
        edges.append(edges[-1] + width)
    q_lat, kv_lat, k_pe, gb, gc, xin, q_mem, gate = [w_in[..., edges[i]:edges[i + 1]] for i in range(8)]
    pad = jnp.zeros(k_pe.shape[:-1] + (LANES - ROPE,), w_in.dtype)
    return jnp.concatenate([gate, q_lat, kv_lat, k_pe, pad, gb, gc, xin, q_mem], axis=-1)


def _w_in_from_z_layout(wz):
    cut = lambda c0, width: wz[..., c0:c0 + width]
    return jnp.concatenate(
        [cut(Z_QLAT, Q_RANK), cut(Z_KVLAT, KV_RANK), cut(Z_KPE, ROPE), cut(Z_GB, CONV_W), cut(Z_GC, CONV_W),
         cut(Z_XIN, CONV_W), cut(Z_QMEM, MEM_W), cut(Z_GATE, MIX_W)], axis=-1)


def _w_uq_pad(w_uq):
    l, r, _ = w_uq.shape
    w = w_uq.reshape(l, r, MLA_HEADS, QK_HEAD)
    w = jnp.pad(w, ((0, 0), (0, 0), (0, 0), (0, QPAD - QK_HEAD)))
    return w.reshape(l, r, MLA_HEADS * QPAD)


def _w_uq_unpad(w):
    l, r, _ = w.shape
    return w.reshape(l, r, MLA_HEADS, QPAD)[..., :QK_HEAD].reshape(l, r, MLA_HEADS * QK_HEAD)


def _w_ukv_split(w_ukv):
    l, r, _ = w_ukv.shape
    w = w_ukv.reshape(l, r, MLA_HEADS, 2, NOPE)
    return jnp.transpose(w, (0, 1, 3, 2, 4)).reshape(l, r, 2 * MLA_HEADS * NOPE)


def _w_ukv_join(w):
    l, r, _ = w.shape
    w = w.reshape(l, r, 2, MLA_HEADS, NOPE)
    return jnp.transpose(w, (0, 1, 3, 2, 4)).reshape(l, r, 2 * MLA_HEADS * NOPE)


def _rope_tables(positions):
    inv_freq = 1.0 / (ROPE_THETA ** (jnp.arange(0, ROPE, 2, dtype=F32) / ROPE))
    ang = positions.astype(F32)[:, None] * inv_freq
    cos, sin = jnp.cos(ang), jnp.sin(ang)
    s = positions.shape[0]
    zero = jnp.zeros((s, HALF_ROPE), F32)
    pad = jnp.zeros((s, LANES - ROPE), F32)
    kc = jnp.concatenate([cos, cos, pad], axis=-1)
    ka = jnp.concatenate([-sin, zero, pad], axis=-1)
    kb = jnp.concatenate([zero, sin, pad], axis=-1)
    qc = jnp.concatenate([jnp.ones((s, NOPE), F32), kc], axis=-1)
    qa = jnp.concatenate([jnp.zeros((s, NOPE), F32), ka], axis=-1)
    qb = jnp.concatenate([jnp.zeros((s, NOPE), F32), kb], axis=-1)
    return (qc, qa, qb), (kc, ka, kb)


def _layer_fwd(l, x, mem, wts, gains, tabs):
    w_in, w_uq, w_ukv, conv_w, w_mkv, w_o = wts
    g_pre, g_q, g_kv, g_mem, g_post = gains
    q_tab, k_tab = tabs
    tag = f"l{l}_"
    h = _rmsnorm_fwd(x, g_pre, 0, D_MODEL, tag + "pre_norm")
    z = _matmul(h, w_in, "nn", F32, tag + "in_proj", tn_cap=1664)
    qn = _rmsnorm_fwd(z, g_q, Z_QLAT, Q_RANK, tag + "q_norm")
    kvn = _rmsnorm_fwd(z, g_kv, Z_KVLAT, KV_RANK, tag + "kv_norm")
    q_raw = _matmul(qn, w_uq, "nn", F32, tag + "uq")
    kv = _matmul(kvn, w_ukv, "nn", BF16, tag + "ukv")
    q_r = _rope(q_raw, *q_tab, 0, QPAD, MLA_HEADS, tag + "q_rope")
    kpe = _rope(z, *k_tab, Z_KPE, LANES, 1, tag + "k_rope")
    a_out, a_lse = _attn_fwd(q_r, kv, kpe, kv, MLA_HEADS, QPAD, 0, 0, MLA_HEADS, QK_HEAD ** -0.5, 256, tag + "mla_fwd")
    c_out = _conv_fwd(z, conv_w, tag + "conv_fwd")
    mem_n = _rmsnorm_fwd(mem, g_mem, 0, D_MODEL, tag + "mem_norm")
    mkv = _matmul(mem_n, w_mkv, "nn", BF16, tag + "mem_kv")
    m_out, m_lse = _attn_fwd(z, mkv, None, mkv, MEM_HEADS, LANES, Z_QMEM // LANES, 0, MEM_HEADS, MEM_HEAD ** -0.5, 1024,
                             tag + "mem_fwd")
    cat = jnp.concatenate([a_out, c_out, m_out], axis=-1)
    y = _gate_fwd(cat, z, tag + "gate_fwd")
    o = _matmul(y, w_o, "nn", F32, tag + "out_proj")
    x_new = _post_norm_residual(x, o, g_post, tag + "post_norm")
    saved = (x, h, z, qn, kvn, q_r, kv, kpe, a_lse, mem_n, mkv, m_lse, cat, y, o)
    return x_new, saved


def _layer_bwd(l, g, mem, saved, wts, gains, tabs_bwd):
    w_in, w_uq, w_ukv, conv_w, w_mkv, w_o = wts
    g_pre, g_q, g_kv, g_mem, g_post = gains
    q_tab, k_tab = tabs_bwd
    x, h, z, qn, kvn, q_r, kv, kpe, a_lse, mem_n, mkv, m_lse, cat, y, o = saved
    tag = f"l{l}_"
    do, dg_post = _rmsnorm_bwd(o, g_post, g, None, 0, D_MODEL, BF16, tag + "post_norm_bwd")
    dy = _matmul(do, w_o, "nt", F32, tag + "out_proj_dx")
    dw_o = _matmul(y, do, "tn", F32, tag + "out_proj_dw")
    dcat, dgate = _gate_bwd(dy, cat, z, tag + "gate_bwd")
    dq_raw, dkn, dkpe_h, dv = _attn_bwd(q_r, kv, kpe, kv, cat, dcat, a_lse, MLA_HEADS, QPAD, 0, 0, MLA_HEADS, 0,
                                        QK_HEAD ** -0.5, 256, F32, tag + "mla_bwd")
    dq = _rope(dq_raw, *q_tab, 0, QPAD, MLA_HEADS, tag + "q_rope_bwd")
    dkpe = _kpe_grad(dkpe_h, *k_tab, MLA_HEADS, tag + "k_rope_bwd")
    dkv = jnp.concatenate([dkn, dv], axis=-1)
    dw_ukv = _matmul(kvn, dkv, "tn", F32, tag + "ukv_dw")
    dkvn = _matmul(dkv, w_ukv, "nt", F32, tag + "ukv_dx")
    dkv_lat, dg_kv = _rmsnorm_bwd(z, g_kv, dkvn, None, Z_KVLAT, KV_RANK, BF16, tag + "kv_norm_bwd")
    dw_uq = _matmul(qn, dq, "tn", F32, tag + "uq_dw")
    dqn = _matmul(dq, w_uq, "nt", F32, tag + "uq_dx")
    dq_lat, dg_q = _rmsnorm_bwd(z, g_q, dqn, None, Z_QLAT, Q_RANK, BF16, tag + "q_norm_bwd")
    dgb, dgc, dxin, dconv_w = _conv_bwd(z, conv_w, dcat, tag + "conv_bwd")
    dq_mem, dmk, dmv = _attn_bwd(z, mkv, None, mkv, cat, dcat, m_lse, MEM_HEADS, LANES, Z_QMEM // LANES, 0, MEM_HEADS,
                                 (MLA_W + CONV_W) // LANES, MEM_HEAD ** -0.5, 1024, BF16, tag + "mem_bwd")
    dmkv = jnp.concatenate([dmk, dmv], axis=-1)
    dw_mkv = _matmul(mem_n, dmkv, "tn", F32, tag + "mem_kv_dw")
    dmem_n = _matmul(dmkv, w_mkv, "nt", F32, tag + "mem_kv_dx")
    _, dg_mem = _rmsnorm_bwd(mem, g_mem, dmem_n, None, 0, D_MODEL, BF16, tag + "mem_norm_bwd")
    dz = jnp.concatenate([dgate, dq_lat, dkv_lat, dkpe, dgb, dgc, dxin, dq_mem], axis=-1)
    dw_in = _matmul(h, dz, "tn", F32, tag + "in_proj_dw", tn_cap=1664)
    dh = _matmul(dz, w_in, "nt", F32, tag + "in_proj_dx", tk_cap=1664)
    dx, dg_pre = _rmsnorm_bwd(x, g_pre, dh, g, 0, D_MODEL, F32, tag + "pre_norm_bwd")
    return dx, (dw_in, dw_uq, dw_ukv, dconv_w, dw_mkv, dw_o), (dg_pre, dg_q, dg_kv, dg_mem, dg_post)


GAIN_WIDTHS = (D_MODEL, Q_RANK, KV_RANK, D_MODEL, D_MODEL)


def _pack_gains(parts):
    return jnp.concatenate([p.reshape(-1) for p in parts]).reshape(-1, LANES)


def _unpack_gains(packed, depth):
    flat = packed.reshape(-1)
    out, at = [], 0
    for width in GAIN_WIDTHS:
        out.append(flat[at:at + depth * width].reshape(depth, width))
        at += depth * width
    return out


def kernel(x, mem, positions, pre_norm_g, w_in, q_norm_g, w_uq, kv_norm_g, w_ukv, conv_w, mem_norm_g, w_mk, w_mv, w_o, post_norm_g, loss_target, m_pre_norm_g, m_w_in, m_q_norm_g, m_w_uq, m_kv_norm_g, m_w_ukv, m_conv_w, m_mem_norm_g, m_w_mk, m_w_mv, m_w_o, m_post_norm_g, v_pre_norm_g, v_w_in, v_q_norm_g, v_w_uq, v_kv_norm_g, v_w_ukv, v_conv_w, v_mem_norm_g, v_w_mk, v_w_mv, v_w_o, v_post_norm_g):
    depth = w_in.shape[0]
    x0, mem0, target = x[0], mem[0], loss_target[0]
    tabs = _rope_tables(positions[0])
    tabs_bwd = tuple((c, -a, -b) for (c, a, b) in tabs)

    shards = [w_in.astype(BF16), w_uq.astype(BF16), w_ukv.astype(BF16), conv_w, w_mk.astype(BF16), w_mv.astype(BF16),
              w_o.astype(BF16)]
    g_in, g_uq, g_ukv, g_conv, g_mk, g_mv, g_o = _chip_exchange(shards, False, "weight_gather")
    w_in_f = _w_in_to_z_layout(_cols_from_shards(g_in))
    w_uq_f = _w_uq_pad(_cols_from_shards(g_uq))
    w_ukv_f = _w_ukv_split(_cols_from_shards(g_ukv))
    conv_f = _cols_from_shards(g_conv)
    w_mkv_f = jnp.concatenate([_rows_from_shards(g_mk), _rows_from_shards(g_mv)], axis=-1)
    w_o_f = _rows_from_shards(g_o)

    def layer_weights(l):
        return (w_in_f[l], w_uq_f[l], w_ukv_f[l], conv_f[l], w_mkv_f[l], w_o_f[l])

    def layer_gains(l):
        return tuple(g[l][None, :] for g in (pre_norm_g, q_norm_g, kv_norm_g, mem_norm_g, post_norm_g))

    saved = []
    act = x0
    for l in range(depth):
        act, s = _layer_fwd(l, act, mem0, layer_weights(l), layer_gains(l), tabs)
        saved.append(s)
    grad, loss_part = _loss_head(act, target, "loss_head")
    loss = lax.psum(loss_part[0, 0], ("x", "y", "c"))
    dws, dgs = [None] * depth, [None] * depth
    for l in reversed(range(depth)):
        grad, dws[l], dgs[l] = _layer_bwd(l, grad, mem0, saved[l], layer_weights(l), layer_gains(l), tabs_bwd)
    grad_x = grad[None]

    stack = lambda i: jnp.stack([dws[l][i] for l in range(depth)])
    dw_mkv = stack(4)
    contrib = [
        _cols_to_shards(_w_in_from_z_layout(stack(0))),
        _cols_to_shards(_w_uq_unpad(stack(1))),
        _cols_to_shards(_w_ukv_join(stack(2))),
        _cols_to_shards(stack(3)),
        _rows_to_shards(dw_mkv[..., :MEM_W]),
        _rows_to_shards(dw_mkv[..., MEM_W:]),
        _rows_to_shards(stack(5)),
    ]
    received = _chip_exchange([c.astype(BF16) for c in contrib], True, "grad_exchange")
    partial = [_sum_slots(r, f"grad_sum_{i}") for i, r in enumerate(received)]
    other = _sibling_exchange(partial, "grad_sibling")

    names = ("w_in", "w_uq", "w_ukv", "conv_w", "w_mk", "w_mv", "w_o")
    w_shards = (w_in, w_uq, w_ukv, conv_w, w_mk, w_mv, w_o)
    m_shards = (m_w_in, m_w_uq, m_w_ukv, m_conv_w, m_w_mk, m_w_mv, m_w_o)
    v_shards = (v_w_in, v_w_uq, v_w_ukv, v_conv_w, v_w_mk, v_w_mv, v_w_o)
    results = {}
    for i, name in enumerate(names):
        shape = w_shards[i].shape
        flat = lambda t: t.reshape(-1, shape[-1])
        outs = _adamw(flat(w_shards[i]), partial[i], other[i], flat(m_shards[i]), flat(v_shards[i]), "adamw_" + name)
        results[name] = tuple(t.reshape(shape) for t in outs)

    gain_names = ("pre_norm_g", "q_norm_g", "kv_norm_g", "mem_norm_g", "post_norm_g")
    dg_packed = _pack_gains([jnp.concatenate([dgs[l][i] for l in range(depth)], axis=0) for i in range(5)])
    dg_total = _sum_slots(_gather_all(dg_packed, "gain_gather"), "gain_sum")
    gain_outs = _adamw(
        _pack_gains((pre_norm_g, q_norm_g, kv_norm_g, mem_norm_g, post_norm_g)), dg_total, None,
        _pack_gains((m_pre_norm_g, m_q_norm_g, m_kv_norm_g, m_mem_norm_g, m_post_norm_g)),
        _pack_gains((v_pre_norm_g, v_q_norm_g, v_kv_norm_g, v_mem_norm_g, v_post_norm_g)), "adamw_gains")
    gain_outs = [_unpack_gains(t, depth) for t in gain_outs]
    for i, name in enumerate(gain_names):
        results[name] = tuple(gain_outs[k][i] for k in range(4))

    order = ("pre_norm_g", "w_in", "q_norm_g", "w_uq", "kv_norm_g", "w_ukv", "conv_w", "mem_norm_g", "w_mk", "w_mv", "w_o",
             "post_norm_g")
    out = [loss, grad_x]
    for k in range(4):
        out += [results[name][k] for name in order]
    return tuple(out)
```

```python
import functools

import jax
import jax.numpy as jnp
from jax import lax
from jax.experimental import pallas as pl
from jax.experimental.pallas import tpu as pltpu

F32 = jnp.float32
BF16 = jnp.bfloat16
MESH_ID = pl.DeviceIdType.MESH

D_MODEL = 2048
EPS = 1e-6
ROPE_THETA = 10000.0
MLA_HEADS = 8
NOPE = 128
ROPE = 64
HALF_ROPE = ROPE // 2
QK_HEAD = NOPE + ROPE
V_HEAD = 128
Q_RANK = 512
KV_RANK = 256
CONV_W = 512
MEM_HEADS = 4
MEM_HEAD = 128
MEM_W = MEM_HEADS * MEM_HEAD
MLA_W = MLA_HEADS * V_HEAD
MIX_W = MLA_W + CONV_W + MEM_W
IN_COLS = Q_RANK + KV_RANK + ROPE + 3 * CONV_W + MEM_W + MIX_W
N_CHIPS = 4
N_DEV = 8

LANES = 128
VMEM_LIMIT_BYTES = 56 * 1024 * 1024

QPAD = 2 * LANES
Z_GATE = 0
Z_QLAT = Z_GATE + MIX_W
Z_KVLAT = Z_QLAT + Q_RANK
Z_KPE = Z_KVLAT + KV_RANK
Z_GB = Z_KPE + LANES
Z_GC = Z_GB + CONV_W
Z_XIN = Z_GC + CONV_W
Z_QMEM = Z_XIN + CONV_W
Z_COLS = Z_QMEM + MEM_W

ADAM_LR = 0.001
ADAM_B1 = 0.9
ADAM_B2 = 0.999
ADAM_EPS = 1e-08
ADAM_WD = 0.01
ADAM_STEP = 10


def _tile(dim, cap, unit):
    if dim <= cap:
        return dim
    t = (cap // unit) * unit
    while t >= unit:
        if dim % t == 0:
            return t
        t -= unit
    raise ValueError(f"no tile of {dim} under {cap} in units of {unit}")


def _params(*semantics):
    return pltpu.CompilerParams(dimension_semantics=semantics, vmem_limit_bytes=VMEM_LIMIT_BYTES)


def _matmul(a, b, mode, out_dtype, name, tm_cap=512, tn_cap=1024, tk_cap=2048):
    if mode == "nn":
        (m, k), (k2, n) = a.shape, b.shape
    elif mode == "nt":
        (m, k), (n, k2) = a.shape, b.shape
    else:
        (k, m), (k2, n) = a.shape, b.shape
    assert k == k2, (a.shape, b.shape, mode)
    tm = _tile(m, tm_cap, LANES if mode == "tn" else 16)
    tn = _tile(n, tn_cap, LANES)
    tk = _tile(k, tk_cap, LANES if mode != "tn" else 16)
    nk = k // tk
    if mode == "nn":
        a_spec = pl.BlockSpec((tm, tk), lambda i, j, kk: (i, kk))
        b_spec = pl.BlockSpec((tk, tn), lambda i, j, kk: (kk, j))
        dims = (((1,), (0,)), ((), ()))
    elif mode == "nt":
        a_spec = pl.BlockSpec((tm, tk), lambda i, j, kk: (i, kk))
        b_spec = pl.BlockSpec((tn, tk), lambda i, j, kk: (j, kk))
        dims = (((1,), (1,)), ((), ()))
    else:
        a_spec = pl.BlockSpec((tk, tm), lambda i, j, kk: (kk, i))
        b_spec = pl.BlockSpec((tk, tn), lambda i, j, kk: (kk, j))
        dims = (((0,), (0,)), ((), ()))

    def body(a_ref, b_ref, o_ref, *scratch):
        part = lax.dot_general(a_ref[...].astype(BF16), b_ref[...].astype(BF16), dims, preferred_element_type=F32)
        if nk == 1:
            o_ref[...] = part.astype(o_ref.dtype)
            return
        (acc_ref,) = scratch
        kk = pl.program_id(2)

        @pl.when(kk == 0)
        def _():
            acc_ref[...] = part

        @pl.when(kk > 0)
        def _():
            acc_ref[...] += part

        @pl.when(kk == nk - 1)
        def _():
            o_ref[...] = acc_ref[...].astype(o_ref.dtype)

    return pl.pallas_call(
        body,
        grid=(m // tm, n // tn, nk),
        in_specs=[a_spec, b_spec],
        out_specs=pl.BlockSpec((tm, tn), lambda i, j, kk: (i, j)),
        out_shape=jax.ShapeDtypeStruct((m, n), out_dtype),
        scratch_shapes=[] if nk == 1 else [pltpu.VMEM((tm, tn), F32)],
        compiler_params=_params("parallel", "parallel", "arbitrary"),
        name=name,
    )(a, b)


def _rmsnorm_fwd(x, gain, col0, width, name):
    rows = x.shape[0]
    tr = _tile(rows, 512, 16)
    cb = col0 // width
    assert cb * width == col0

    def body(x_ref, g_ref, o_ref):
        xv = x_ref[...]
        r = lax.rsqrt(jnp.mean(xv * xv, axis=-1, keepdims=True) + EPS)
        o_ref[...] = (xv * r * g_ref[...]).astype(o_ref.dtype)

    return pl.pallas_call(
        body,
        grid=(rows // tr,),
        in_specs=[pl.BlockSpec((tr, width), lambda i: (i, cb)), pl.BlockSpec((1, width), lambda i: (0, 0))],
        out_specs=pl.BlockSpec((tr, width), lambda i: (i, 0)),
        out_shape=jax.ShapeDtypeStruct((rows, width), BF16),
        compiler_params=_params("parallel"),
        name=name,
    )(x, gain)


def _rmsnorm_bwd(x, gain, dy, resid, col0, width, out_dtype, name):
    rows = x.shape[0]
    tr = _tile(rows, 256, 16)
    cb = col0 // width
    assert cb * width == col0
    has_resid = resid is not None

    def body(*refs):
        if has_resid:
            x_ref, g_ref, dy_ref, res_ref, dx_ref, dg_ref = refs
        else:
            x_ref, g_ref, dy_ref, dx_ref, dg_ref = refs
        i = pl.program_id(0)
        xv = x_ref[...]
        dyv = dy_ref[...].astype(F32)
        r = lax.rsqrt(jnp.mean(xv * xv, axis=-1, keepdims=True) + EPS)
        xr = xv * r
        dyg = dyv * g_ref[...]
        c = jnp.mean(dyg * xr, axis=-1, keepdims=True)
        dx = r * (dyg - xr * c)
        if has_resid:
            dx = dx + res_ref[...]
        dx_ref[...] = dx.astype(dx_ref.dtype)
        part = jnp.sum(dyv * xr, axis=0, keepdims=True)

        @pl.when(i == 0)
        def _():
            dg_ref[...] = part

        @pl.when(i > 0)
        def _():
            dg_ref[...] += part

    row_spec = pl.BlockSpec((tr, width), lambda i: (i, 0))
    in_specs = [pl.BlockSpec((tr, width), lambda i: (i, cb)), pl.BlockSpec((1, width), lambda i: (0, 0)), row_spec]
    args = [x, gain, dy]
    if has_resid:
        in_specs.append(row_spec)
        args.append(resid)
    return pl.pallas_call(
        body,
        grid=(rows // tr,),
        in_specs=in_specs,
        out_specs=[row_spec, pl.BlockSpec((1, width), lambda i: (0, 0))],
        out_shape=[jax.ShapeDtypeStruct((rows, width), out_dtype), jax.ShapeDtypeStruct((1, width), F32)],
        compiler_params=_params("arbitrary"),
        name=name,
    )(*args)


def _post_norm_residual(x, o, gain, name):
    rows, width = x.shape
    tr = _tile(rows, 256, 8)

    def body(x_ref, o_ref, g_ref, out_ref):
        ov = o_ref[...]
        r = lax.rsqrt(jnp.mean(ov * ov, axis=-1, keepdims=True) + EPS)
        out_ref[...] = x_ref[...] + ov * r * g_ref[...]

    row_spec = pl.BlockSpec((tr, width), lambda i: (i, 0))
    return pl.pallas_call(
        body,
        grid=(rows // tr,),
        in_specs=[row_spec, row_spec, pl.BlockSpec((1, width), lambda i: (0, 0))],
        out_specs=row_spec,
        out_shape=jax.ShapeDtypeStruct((rows, width), F32),
        compiler_params=_params("parallel"),
        name=name,
    )(x, o, gain)


def _rope(x, tab_c, tab_a, tab_b, col0, width, heads, name):
    rows = x.shape[0]
    tr = _tile(rows, 512, 16)
    cb = col0 // width
    assert cb * width == col0

    def body(x_ref, c_ref, a_ref, b_ref, o_ref):
        xv = x_ref[...].astype(F32)
        up = pltpu.roll(xv, width - HALF_ROPE, 1)
        down = pltpu.roll(xv, HALF_ROPE, 1)
        o_ref[...] = (xv * c_ref[...] + up * a_ref[...] + down * b_ref[...]).astype(o_ref.dtype)

    tab_spec = pl.BlockSpec((tr, width), lambda i, h: (i, 0))
    return pl.pallas_call(
        body,
        grid=(rows // tr, heads),
        in_specs=[pl.BlockSpec((tr, width), lambda i, h: (i, cb + h)), tab_spec, tab_spec, tab_spec],
        out_specs=pl.BlockSpec((tr, width), lambda i, h: (i, h)),
        out_shape=jax.ShapeDtypeStruct((rows, heads * width), BF16),
        compiler_params=_params("parallel", "parallel"),
        name=name,
    )(x, tab_c, tab_a, tab_b)


def _kpe_grad(dkb, tab_c, tab_a, tab_b, heads, name):
    rows = dkb.shape[0]
    tr = _tile(rows, 512, 16)

    def body(d_ref, c_ref, a_ref, b_ref, o_ref):
        acc = d_ref[:, 0:LANES]
        for h in range(1, heads):
            acc = acc + d_ref[:, h * LANES:(h + 1) * LANES]
        up = pltpu.roll(acc, LANES - HALF_ROPE, 1)
        down = pltpu.roll(acc, HALF_ROPE, 1)
        o_ref[...] = (acc * c_ref[...] + up * a_ref[...] + down * b_ref[...]).astype(o_ref.dtype)

    tab_spec = pl.BlockSpec((tr, LANES), lambda i: (i, 0))
    return pl.pallas_call(
        body,
        grid=(rows // tr,),
        in_specs=[pl.BlockSpec((tr, heads * LANES), lambda i: (i, 0)), tab_spec, tab_spec, tab_spec],
        out_specs=tab_spec,
        out_shape=jax.ShapeDtypeStruct((rows, LANES), BF16),
        compiler_params=_params("parallel"),
        name=name,
    )(dkb, tab_c, tab_a, tab_b)


class _CommPlan:
    def __init__(self, ins, out_shape, build, n_copies):
        self.ins, self.out_shape, self.build, self.n_copies = list(ins), list(out_shape), build, n_copies

    def scratch(self):
        n = self.n_copies
        return [pltpu.SemaphoreType.DMA((n,)), pltpu.SemaphoreType.DMA((n,)), pltpu.SemaphoreType.DMA((n,))]


def _split_comm(refs, n_in, n_out, comm):
    if comm is None:
        return refs, None
    ci, co = len(comm.ins), len(comm.out_shape)
    ins, c_ins = refs[:n_in], refs[n_in:n_in + ci]
    outs, c_outs = refs[n_in + ci:n_in + ci + n_out], refs[n_in + ci + n_out:n_in + ci + n_out + co]
    rest = refs[n_in + ci + n_out + co:]
    scratch, sems = rest[:-3], rest[-3:]
    return tuple(ins) + tuple(outs) + tuple(scratch), functools.partial(comm.build, c_ins, c_outs, sems)


def _ride_start(copies, first):
    if copies is not None:

        @pl.when(first)
        def _():
            for cp in copies():
                cp.start()


def _ride_wait(copies, last):
    if copies is not None:

        @pl.when(last)
        def _():
            for cp in copies():
                cp.wait()


def _attn_fwd(q, ka, kb, v, heads, q_w, q_cb, ka_cb, ka_step, v_cb, v_step, scale, tq_cap, name, comm=None):
    s_q, s_k = q.shape[0], ka.shape[0]
    tq = _tile(s_q, tq_cap, 16)
    nq = s_q // tq
    has_kb = kb is not None
    n_in = 4 if has_kb else 3

    def body(*refs):
        refs, copies = _split_comm(refs, n_in, 2, comm)
        first = jnp.logical_and(pl.program_id(0) == 0, pl.program_id(1) == 0)
        last = jnp.logical_and(pl.program_id(0) == heads - 1, pl.program_id(1) == nq - 1)
        _ride_start(copies, first)
        if has_kb:
            q_ref, ka_ref, kb_ref, v_ref, o_ref, lse_ref, k_scr = refs

            @pl.when(pl.program_id(1) == 0)
            def _():
                k_scr[:, 0:LANES] = ka_ref[...].astype(BF16)
                k_scr[:, LANES:2 * LANES] = kb_ref[...].astype(BF16)

            kmat = k_scr[...]
        else:
            q_ref, ka_ref, v_ref, o_ref, lse_ref = refs
            kmat = ka_ref[...].astype(BF16)
        s = lax.dot_general(q_ref[...].astype(BF16), kmat, (((1,), (1,)), ((), ())), preferred_element_type=F32) * scale
        m = jnp.max(s, axis=-1, keepdims=True)
        p = jnp.exp(s - m)
        l = jnp.sum(p, axis=-1, keepdims=True)
        o = jnp.dot(p.astype(BF16), v_ref[...].astype(BF16), preferred_element_type=F32)
        o_ref[...] = (o * (1.0 / l)).astype(o_ref.dtype)
        lse_ref[...] = jnp.broadcast_to(m + jnp.log(l), lse_ref.shape)
        _ride_wait(copies, last)

    in_specs = [pl.BlockSpec((tq, q_w), lambda h, i: (i, q_cb + h)),
                pl.BlockSpec((s_k, LANES), lambda h, i: (0, ka_cb + ka_step * h))]
    args = [q, ka]
    if has_kb:
        in_specs.append(pl.BlockSpec((s_k, LANES), lambda h, i: (0, 0)))
        args.append(kb)
    in_specs.append(pl.BlockSpec((s_k, LANES), lambda h, i: (0, v_cb + v_step * h)))
    args.append(v)
    out_spec = pl.BlockSpec((tq, LANES), lambda h, i: (i, h))
    out_specs = [out_spec, out_spec]
    out_shape = [jax.ShapeDtypeStruct((s_q, heads * LANES), BF16), jax.ShapeDtypeStruct((s_q, heads * LANES), F32)]
    scratch = [pltpu.VMEM((s_k, 2 * LANES), BF16)] if has_kb else []
    if comm is not None:
        in_specs += [ANY] * len(comm.ins)
        args += comm.ins
        out_specs += [ANY] * len(comm.out_shape)
        out_shape += comm.out_shape
        scratch += comm.scratch()
    res = pl.pallas_call(
        body,
        grid=(heads, nq),
        in_specs=in_specs,
        out_specs=out_specs,
        out_shape=out_shape,
        scratch_shapes=scratch,
        compiler_params=_params("arbitrary", "arbitrary"),
        name=name,
    )(*args)
    return res[0], res[1], list(res[2:])


def _attn_bwd(q, ka, kb, v, o, do, lse, heads, q_w, q_cb, ka_cb, ka_step, v_cb, v_step, o_cb, scale, tq_cap, dq_dtype, name,
              comm=None):
    s_q, s_k = q.shape[0], ka.shape[0]
    tq = _tile(s_q, tq_cap, 16)
    nq = s_q // tq
    has_kb = kb is not None
    n_in = 7 if has_kb else 6
    n_out = 3

    def body(*refs):
        refs, copies = _split_comm(refs, n_in, n_out, comm)
        first = jnp.logical_and(pl.program_id(0) == 0, pl.program_id(1) == 0)
        last = jnp.logical_and(pl.program_id(0) == heads - 1, pl.program_id(1) == nq - 1)
        _ride_start(copies, first)
        if has_kb:
            q_ref, ka_ref, kb_ref, v_ref, o_ref, do_ref, lse_ref, dq_ref, dkv_ref, dkb_ref, k_scr, dk_acc, dv_acc = refs
        else:
            q_ref, ka_ref, v_ref, o_ref, do_ref, lse_ref, dq_ref, dka_ref, dv_ref, dk_acc, dv_acc = refs
        i = pl.program_id(1)

        @pl.when(i == 0)
        def _():
            dk_acc[...] = jnp.zeros_like(dk_acc)
            dv_acc[...] = jnp.zeros_like(dv_acc)
            if has_kb:
                k_scr[:, 0:LANES] = ka_ref[...].astype(BF16)
                k_scr[:, LANES:2 * LANES] = kb_ref[...].astype(BF16)

        kmat = k_scr[...] if has_kb else ka_ref[...].astype(BF16)
        qv = q_ref[...].astype(BF16)
        dov = do_ref[...].astype(BF16)
        delta = jnp.sum(dov.astype(F32) * o_ref[...].astype(F32), axis=-1, keepdims=True)
        s = lax.dot_general(qv, kmat, (((1,), (1,)), ((), ())), preferred_element_type=F32) * scale
        p = jnp.exp(s - lse_ref[:, 0:1])
        dp = lax.dot_general(dov, v_ref[...].astype(BF16), (((1,), (1,)), ((), ())), preferred_element_type=F32)
        ds = (p * (dp - delta) * scale).astype(BF16)
        dq_ref[...] = jnp.dot(ds, kmat, preferred_element_type=F32).astype(dq_ref.dtype)
        dk_acc[...] += lax.dot_general(ds, qv, (((0,), (0,)), ((), ())), preferred_element_type=F32)
        dv_acc[...] += lax.dot_general(p.astype(BF16), dov, (((0,), (0,)), ((), ())), preferred_element_type=F32)

        @pl.when(i == nq - 1)
        def _():
            if has_kb:
                dkv_ref[:, 0:LANES] = dk_acc[:, 0:LANES].astype(dkv_ref.dtype)
                dkv_ref[:, LANES:2 * LANES] = dv_acc[...].astype(dkv_ref.dtype)
                dkb_ref[...] = dk_acc[:, LANES:2 * LANES]
            else:
                dka_ref[...] = dk_acc[...].astype(dka_ref.dtype)
                dv_ref[...] = dv_acc[...].astype(dv_ref.dtype)

        _ride_wait(copies, last)

    key_spec = lambda cb, step: pl.BlockSpec((s_k, LANES), lambda h, i: (0, cb + step * h))
    row_spec = lambda cb: pl.BlockSpec((tq, LANES), lambda h, i: (i, cb + h))
    in_specs = [pl.BlockSpec((tq, q_w), lambda h, i: (i, q_cb + h)), key_spec(ka_cb, ka_step)]
    args = [q, ka]
    if has_kb:
        in_specs.append(pl.BlockSpec((s_k, LANES), lambda h, i: (0, 0)))
        args.append(kb)
    in_specs += [key_spec(v_cb, v_step), row_spec(o_cb), row_spec(o_cb), row_spec(0)]
    args += [v, o, do, lse]
    out_specs = [pl.BlockSpec((tq, q_w), lambda h, i: (i, h))]
    out_shape = [jax.ShapeDtypeStruct((s_q, heads * q_w), dq_dtype)]
    scratch = []
    if has_kb:
        out_specs += [pl.BlockSpec((s_k, 2 * LANES), lambda h, i: (0, h)), key_spec(0, 1)]
        out_shape += [jax.ShapeDtypeStruct((s_k, heads * 2 * LANES), BF16), jax.ShapeDtypeStruct((s_k, heads * LANES), F32)]
        scratch.append(pltpu.VMEM((s_k, 2 * LANES), BF16))
    else:
        out_specs += [key_spec(0, 1), key_spec(0, 1)]
        out_shape += [jax.ShapeDtypeStruct((s_k, heads * LANES), BF16)] * 2
    scratch += [pltpu.VMEM((s_k, q_w), F32), pltpu.VMEM((s_k, LANES), F32)]
    if comm is not None:
        in_specs += [ANY] * len(comm.ins)
        args += comm.ins
        out_specs += [ANY] * len(comm.out_shape)
        out_shape += comm.out_shape
        scratch += comm.scratch()
    res = pl.pallas_call(
        body,
        grid=(heads, nq),
        in_specs=in_specs,
        out_specs=out_specs,
        out_shape=out_shape,
        scratch_shapes=scratch,
        compiler_params=_params("arbitrary", "arbitrary"),
        name=name,
    )(*args)
    return res[0], res[1], res[2], list(res[3:])


def _shift_rows(u, rows):
    t = lax.broadcasted_iota(jnp.int32, u.shape, 0)
    prev = jnp.where(t == 0, 0.0, pltpu.roll(u, 1, 0))
    nxt = jnp.where(t == rows - 1, 0.0, pltpu.roll(u, rows - 1, 0))
    return prev, nxt


def _conv_fwd(z, conv_w, name):
    rows = z.shape[0]
    nblk = CONV_W // LANES

    def body(gb_ref, gc_ref, xin_ref, w_ref, o_ref):
        u = gc_ref[...] * xin_ref[...]
        prev, nxt = _shift_rows(u, rows)
        conv = prev * w_ref[0:1, :] + u * w_ref[1:2, :] + nxt * w_ref[2:3, :]
        o_ref[...] = (gb_ref[...] * conv).astype(o_ref.dtype)

    col = lambda c0: pl.BlockSpec((rows, LANES), lambda j: (0, c0 // LANES + j))
    return pl.pallas_call(
        body,
        grid=(nblk,),
        in_specs=[col(Z_GB), col(Z_GC), col(Z_XIN), pl.BlockSpec((3, LANES), lambda j: (0, j))],
        out_specs=col(0),
        out_shape=jax.ShapeDtypeStruct((rows, CONV_W), BF16),
        compiler_params=_params("parallel"),
        name=name,
    )(z, z, z, conv_w)


def _conv_bwd(z, conv_w, dcat, name):
    rows = z.shape[0]
    nblk = CONV_W // LANES

    def body(gb_ref, gc_ref, xin_ref, w_ref, dc_ref, dgb_ref, dgc_ref, dxin_ref, dw_ref):
        gc = gc_ref[...]
        xin = xin_ref[...]
        dc = dc_ref[...].astype(F32)
        u = gc * xin
        prev, nxt = _shift_rows(u, rows)
        w0, w1, w2 = w_ref[0:1, :], w_ref[1:2, :], w_ref[2:3, :]
        conv = prev * w0 + u * w1 + nxt * w2
        dgb_ref[...] = (dc * conv).astype(dgb_ref.dtype)
        dconv = dc * gb_ref[...]
        dw_ref[0:1, :] = jnp.sum(dconv * prev, axis=0, keepdims=True)
        dw_ref[1:2, :] = jnp.sum(dconv * u, axis=0, keepdims=True)
        dw_ref[2:3, :] = jnp.sum(dconv * nxt, axis=0, keepdims=True)
        dprev, dnxt = _shift_rows(dconv, rows)
        du = dnxt * w0 + dconv * w1 + dprev * w2
        dgc_ref[...] = (du * xin).astype(dgc_ref.dtype)
        dxin_ref[...] = (du * gc).astype(dxin_ref.dtype)

    col = lambda c0: pl.BlockSpec((rows, LANES), lambda j: (0, c0 // LANES + j))
    w_spec = pl.BlockSpec((3, LANES), lambda j: (0, j))
    piece = jax.ShapeDtypeStruct((rows, CONV_W), BF16)
    return pl.pallas_call(
        body,
        grid=(nblk,),
        in_specs=[col(Z_GB), col(Z_GC), col(Z_XIN), w_spec, col(MLA_W)],
        out_specs=[col(0), col(0), col(0), w_spec],
        out_shape=[piece, piece, piece, jax.ShapeDtypeStruct((3, CONV_W), F32)],
        compiler_params=_params("parallel"),
        name=name,
    )(z, z, z, conv_w, dcat)


def _gate_fwd(cat, z, name):
    rows = cat.shape[0]
    tr = _tile(rows, 512, 16)
    tc = 512
    g0 = Z_GATE // tc

    def body(c_ref, g_ref, y_ref):
        g = g_ref[...]
        y_ref[...] = (c_ref[...].astype(F32) * (g * jax.nn.sigmoid(g))).astype(y_ref.dtype)

    blk = pl.BlockSpec((tr, tc), lambda i, j: (i, j))
    return pl.pallas_call(
        body,
        grid=(rows // tr, MIX_W // tc),
        in_specs=[blk, pl.BlockSpec((tr, tc), lambda i, j: (i, g0 + j))],
        out_specs=blk,
        out_shape=jax.ShapeDtypeStruct((rows, MIX_W), BF16),
        compiler_params=_params("parallel", "parallel"),
        name=name,
    )(cat, z)


def _gate_bwd(dy, cat, z, name):
    rows = cat.shape[0]
    tr = _tile(rows, 512, 16)
    tc = 512
    g0 = Z_GATE // tc

    def body(dy_ref, c_ref, g_ref, dcat_ref, dgate_ref):
        g = g_ref[...]
        sg = jax.nn.sigmoid(g)
        dyv = dy_ref[...].astype(F32)
        dcat_ref[...] = (dyv * (g * sg)).astype(dcat_ref.dtype)
        dgate_ref[...] = (dyv * c_ref[...].astype(F32) * (sg * (1.0 + g * (1.0 - sg)))).astype(dgate_ref.dtype)

    blk = pl.BlockSpec((tr, tc), lambda i, j: (i, j))
    out = jax.ShapeDtypeStruct((rows, MIX_W), BF16)
    return pl.pallas_call(
        body,
        grid=(rows // tr, MIX_W // tc),
        in_specs=[blk, blk, pl.BlockSpec((tr, tc), lambda i, j: (i, g0 + j))],
        out_specs=[blk, blk],
        out_shape=[out, out],
        compiler_params=_params("parallel", "parallel"),
        name=name,
    )(dy, cat, z)


def _loss_head(y, target, name):
    rows, width = y.shape
    tr = _tile(rows, 256, 8)

    def body(y_ref, t_ref, g_ref, loss_ref):
        i = pl.program_id(0)
        d = y_ref[...] - t_ref[...]
        g_ref[...] = d / width
        part = 0.5 * jnp.sum(jnp.mean(d * d, axis=-1, keepdims=True), axis=0, keepdims=True)
        part = jnp.broadcast_to(part, loss_ref.shape)

        @pl.when(i == 0)
        def _():
            loss_ref[...] = part

        @pl.when(i > 0)
        def _():
            loss_ref[...] += part

    row_spec = pl.BlockSpec((tr, width), lambda i: (i, 0))
    return pl.pallas_call(
        body,
        grid=(rows // tr,),
        in_specs=[row_spec, row_spec],
        out_specs=[row_spec, pl.BlockSpec((1, LANES), lambda i: (0, 0))],
        out_shape=[jax.ShapeDtypeStruct((rows, width), F32), jax.ShapeDtypeStruct((1, LANES), F32)],
        compiler_params=_params("arbitrary"),
        name=name,
    )(y, target)


CHIP_FLIPS = ((1, 0), (0, 1), (1, 1))
ANY = pl.BlockSpec(memory_space=pl.ANY)


def _rows_of(chip, n):
    return pl.ds(pl.multiple_of(chip * n, n), n)


def _chip_copies(pieces, sems):
    send_sems, recv_sems, local_sems = sems
    x, y, c = lax.axis_index("x"), lax.axis_index("y"), lax.axis_index("c")
    me = 2 * x + y
    copies = []
    for a, (src, dst) in enumerate(pieces(me, me)):
        copies.append(pltpu.make_async_copy(src, dst, local_sems.at[a]))
    n = len(copies)
    for k, (fx, fy) in enumerate(CHIP_FLIPS):
        px, py = (x + fx) % 2, (y + fy) % 2
        for a, (src, dst) in enumerate(pieces(2 * px + py, me)):
            copies.append(pltpu.make_async_remote_copy(
                src_ref=src, dst_ref=dst, send_sem=send_sems.at[n * k + a], recv_sem=recv_sems.at[n * k + a],
                device_id=(px, py, c), device_id_type=MESH_ID))
    return copies


N_WEIGHTS = 7


def _gather_plan(l, shards):
    s_in, s_uq, s_ukv, s_conv, s_mk, s_mv, s_o = shards
    ukv_c, mk_r, mk_c, o_r = s_ukv.shape[2], s_mk.shape[1], s_mk.shape[2], s_o.shape[1]
    stack = lambda s: jax.ShapeDtypeStruct((N_CHIPS,) + s.shape[1:], s.dtype)
    out_shape = [stack(s_in), stack(s_uq), jax.ShapeDtypeStruct((s_ukv.shape[1], N_CHIPS * ukv_c), s_ukv.dtype), stack(s_conv),
                 jax.ShapeDtypeStruct((N_CHIPS * mk_r, 2 * mk_c), s_mk.dtype),
                 jax.ShapeDtypeStruct((N_CHIPS * o_r, s_o.shape[2]), s_o.dtype)]

    def build(ins, outs, sems):
        g_in, g_uq, f_ukv, g_conv, f_mkv, f_o = outs

        def pieces(target, me):
            del target
            dsts = [g_in.at[me], g_uq.at[me], f_ukv.at[:, _rows_of(me, ukv_c)], g_conv.at[me],
                    f_mkv.at[_rows_of(me, mk_r), pl.ds(0, mk_c)], f_mkv.at[_rows_of(me, mk_r), pl.ds(mk_c, mk_c)],
                    f_o.at[_rows_of(me, o_r), :]]
            return [(src.at[l], dst) for src, dst in zip(ins, dsts)]

        return _chip_copies(pieces, sems)

    return _CommPlan(shards, out_shape, build, (N_CHIPS - 1) * N_WEIGHTS)


def _scatter_plan(c_in, c_uq, dw_ukv, c_conv, dw_mkv, dw_o):
    ukv_c, mk_r, mk_c, o_r = dw_ukv.shape[1] // N_CHIPS, dw_mkv.shape[0] // N_CHIPS, dw_mkv.shape[1] // 2, dw_o.shape[0] // N_CHIPS
    out_shape = [jax.ShapeDtypeStruct(c_in.shape, c_in.dtype), jax.ShapeDtypeStruct(c_uq.shape, c_uq.dtype),
                 jax.ShapeDtypeStruct((N_CHIPS, dw_ukv.shape[0], ukv_c), dw_ukv.dtype),
                 jax.ShapeDtypeStruct(c_conv.shape, c_conv.dtype),
                 jax.ShapeDtypeStruct((N_CHIPS, mk_r, mk_c), dw_mkv.dtype), jax.ShapeDtypeStruct((N_CHIPS, mk_r, mk_c), dw_mkv.dtype),
                 jax.ShapeDtypeStruct((N_CHIPS, o_r, dw_o.shape[1]), dw_o.dtype)]

    def build(ins, outs, sems):
        r_in, r_uq, r_ukv, r_conv, r_mkv, r_o = ins

        def pieces(target, me):
            srcs = [r_in.at[target], r_uq.at[target], r_ukv.at[:, _rows_of(target, ukv_c)], r_conv.at[target],
                    r_mkv.at[_rows_of(target, mk_r), pl.ds(0, mk_c)], r_mkv.at[_rows_of(target, mk_r), pl.ds(mk_c, mk_c)],
                    r_o.at[_rows_of(target, o_r), :]]
            return [(src, dst.at[me]) for src, dst in zip(srcs, outs)]

        return _chip_copies(pieces, sems)

    return _CommPlan([c_in, c_uq, dw_ukv, c_conv, dw_mkv, dw_o], out_shape, build, (N_CHIPS - 1) * N_WEIGHTS)


def _comm_call(plan, name):
    n_in, n_out = len(plan.ins), len(plan.out_shape)

    def body(*refs):
        copies = plan.build(refs[:n_in], refs[n_in:n_in + n_out], refs[n_in + n_out:])
        for cp in copies:
            cp.start()
        for cp in copies:
            cp.wait()

    return list(pl.pallas_call(
        body,
        in_specs=[ANY] * n_in,
        out_specs=[ANY] * n_out,
        out_shape=plan.out_shape,
        scratch_shapes=plan.scratch(),
        name=name,
    )(*plan.ins))


def _sibling_exchange(arrays, name):
    n = len(arrays)

    def body(*refs):
        ins, outs = refs[:n], refs[n:2 * n]
        send_sems, recv_sems = refs[2 * n:]
        sibling = (lax.axis_index("x"), lax.axis_index("y"), 1 - lax.axis_index("c"))
        copies = []
        for a in range(n):
            cp = pltpu.make_async_remote_copy(
                src_ref=ins[a], dst_ref=outs[a], send_sem=send_sems.at[a], recv_sem=recv_sems.at[a],
                device_id=sibling, device_id_type=MESH_ID)
            cp.start()
            copies.append(cp)
        for cp in copies:
            cp.wait()

    return pl.pallas_call(
        body,
        in_specs=[ANY] * n,
        out_specs=[ANY] * n,
        out_shape=[jax.ShapeDtypeStruct(v.shape, v.dtype) for v in arrays],
        scratch_shapes=[pltpu.SemaphoreType.DMA((n,)), pltpu.SemaphoreType.DMA((n,))],
        name=name,
    )(*arrays)


DEVICE_FLIPS = tuple((fx, fy, fc) for fx in (0, 1) for fy in (0, 1) for fc in (0, 1))[1:]


def _gather_all(v, name):
    def body(v_ref, out_ref, send_sems, recv_sems, local_sem):
        x, y, c = lax.axis_index("x"), lax.axis_index("y"), lax.axis_index("c")
        me = 4 * x + 2 * y + c
        local = pltpu.make_async_copy(v_ref, out_ref.at[me], local_sem)
        local.start()
        copies = [local]
        for k, (fx, fy, fc) in enumerate(DEVICE_FLIPS):
            cp = pltpu.make_async_remote_copy(
                src_ref=v_ref, dst_ref=out_ref.at[me], send_sem=send_sems.at[k], recv_sem=recv_sems.at[k],
                device_id=((x + fx) % 2, (y + fy) % 2, (c + fc) % 2), device_id_type=MESH_ID)
            cp.start()
            copies.append(cp)
        for cp in copies:
            cp.wait()

    return pl.pallas_call(
        body,
        in_specs=[ANY],
        out_specs=ANY,
        out_shape=jax.ShapeDtypeStruct((N_DEV,) + v.shape, v.dtype),
        scratch_shapes=[pltpu.SemaphoreType.DMA((N_DEV - 1,)), pltpu.SemaphoreType.DMA((N_DEV - 1,)), pltpu.SemaphoreType.DMA],
        name=name,
    )(v)


def _sum_slots(parts, name):
    n, rows, cols = parts.shape
    tr = _tile(rows, 256, 16)

    def body(p_ref, o_ref):
        acc = p_ref[0].astype(F32)
        for k in range(1, n):
            acc = acc + p_ref[k].astype(F32)
        o_ref[...] = acc

    return pl.pallas_call(
        body,
        grid=(rows // tr,),
        in_specs=[pl.BlockSpec((n, tr, cols), lambda i: (0, i, 0))],
        out_specs=pl.BlockSpec((tr, cols), lambda i: (i, 0)),
        out_shape=jax.ShapeDtypeStruct((rows, cols), F32),
        compiler_params=_params("parallel"),
        name=name,
    )(parts)


def _adamw_math(w, g, m, v):
    m_new = ADAM_B1 * m + (1.0 - ADAM_B1) * g
    v_new = ADAM_B2 * v + (1.0 - ADAM_B2) * jnp.square(g)
    m_hat = m_new / (1.0 - ADAM_B1 ** ADAM_STEP)
    v_hat = v_new / (1.0 - ADAM_B2 ** ADAM_STEP)
    return -ADAM_LR * (m_hat / (jnp.sqrt(v_hat) + ADAM_EPS) + ADAM_WD * w), m_new, v_new


def _adamw(w, g, m, v, name):
    rows, cols = w.shape
    tr = _tile(rows, 256, 8)

    def body(w_ref, g_ref, m_ref, v_ref, d_out, m_out, v_out):
        d_out[...], m_out[...], v_out[...] = _adamw_math(w_ref[...], g_ref[...], m_ref[...], v_ref[...])

    blk = pl.BlockSpec((tr, cols), lambda i: (i, 0))
    out = jax.ShapeDtypeStruct((rows, cols), F32)
    return pl.pallas_call(
        body,
        grid=(rows // tr,),
        in_specs=[blk] * 4,
        out_specs=[blk] * 3,
        out_shape=[out] * 3,
        compiler_params=_params("parallel"),
        name=name,
    )(w, g, m, v)


def _adamw_layer(l, w, g_a, g_b, m, v, prev, name):
    depth, rows, cols = w.shape
    tr = _tile(rows, 256, 8)

    def body(w_ref, ga_ref, gb_ref, m_ref, v_ref, *rest):
        g_out, d_out, m_out, v_out = rest[-4:]
        g = ga_ref[...] + gb_ref[...]
        g_out[...] = g
        d_out[...], m_out[...], v_out[...] = _adamw_math(w_ref[...], g, m_ref[...], v_ref[...])

    stacked = pl.BlockSpec((None, tr, cols), lambda i: (l, i, 0))
    flat = pl.BlockSpec((tr, cols), lambda i: (i, 0))
    in_specs = [stacked, flat, flat, stacked, stacked]
    args = [w, g_a, g_b, m, v]
    aliases = {}
    if prev is not None:
        in_specs += [ANY] * 4
        args += list(prev)
        aliases = {5 + k: k for k in range(4)}
    out = jax.ShapeDtypeStruct((depth, rows, cols), F32)
    return pl.pallas_call(
        body,
        grid=(rows // tr,),
        in_specs=in_specs,
        out_specs=[stacked] * 4,
        out_shape=[out] * 4,
        input_output_aliases=aliases,
        compiler_params=_params("parallel"),
        name=name,
    )(*args)


def _cols_from_shards(g):
    _, r, c = g.shape
    return jnp.transpose(g, (1, 0, 2)).reshape(r, N_CHIPS * c)


def _cols_to_shards(full):
    r, c4 = full.shape
    c = c4 // N_CHIPS
    return jnp.transpose(full.reshape(r, N_CHIPS, c), (1, 0, 2))


IN_ORDER = (Q_RANK, KV_RANK, ROPE, CONV_W, CONV_W, CONV_W, MEM_W, MIX_W)


def _w_in_to_z_layout(w_in):
    edges = [0]
    for width in IN_ORDER:
        edges.append(edges[-1] + width)
    q_lat, kv_lat, k_pe, gb, gc, xin, q_mem, gate = [w_in[..., edges[i]:edges[i + 1]] for i in range(8)]
    pad = jnp.zeros(k_pe.shape[:-1] + (LANES - ROPE,), w_in.dtype)
    return jnp.concatenate([gate, q_lat, kv_lat, k_pe, pad, gb, gc, xin, q_mem], axis=-1)


def _w_in_from_z_layout(wz):
    cut = lambda c0, width: wz[..., c0:c0 + width]
    return jnp.concatenate(
        [cut(Z_QLAT, Q_RANK), cut(Z_KVLAT, KV_RANK), cut(Z_KPE, ROPE), cut(Z_GB, CONV_W), cut(Z_GC, CONV_W),
         cut(Z_XIN, CONV_W), cut(Z_QMEM, MEM_W), cut(Z_GATE, MIX_W)], axis=-1)


def _w_uq_pad(w_uq):
    r, _ = w_uq.shape
    w = jnp.pad(w_uq.reshape(r, MLA_HEADS, QK_HEAD), ((0, 0), (0, 0), (0, QPAD - QK_HEAD)))
    return w.reshape(r, MLA_HEADS * QPAD)


def _w_uq_unpad(w):
    r, _ = w.shape
    return w.reshape(r, MLA_HEADS, QPAD)[..., :QK_HEAD].reshape(r, MLA_HEADS * QK_HEAD)


def _rope_tables(positions):
    inv_freq = 1.0 / (ROPE_THETA ** (jnp.arange(0, ROPE, 2, dtype=F32) / ROPE))
    ang = positions.astype(F32)[:, None] * inv_freq
    cos, sin = jnp.cos(ang), jnp.sin(ang)
    s = positions.shape[0]
    zero = jnp.zeros((s, HALF_ROPE), F32)
    pad = jnp.zeros((s, LANES - ROPE), F32)
    kc = jnp.concatenate([cos, cos, pad], axis=-1)
    ka = jnp.concatenate([-sin, zero, pad], axis=-1)
    kb = jnp.concatenate([zero, sin, pad], axis=-1)
    qc = jnp.concatenate([jnp.ones((s, NOPE), F32), kc], axis=-1)
    qa = jnp.concatenate([jnp.zeros((s, NOPE), F32), ka], axis=-1)
    qb = jnp.concatenate([jnp.zeros((s, NOPE), F32), kb], axis=-1)
    return (qc, qa, qb), (kc, ka, kb)


def _layer_weights(gathered):
    g_in, g_uq, w_ukv, g_conv, w_mkv, w_o = gathered
    return (_w_in_to_z_layout(_cols_from_shards(g_in)), _w_uq_pad(_cols_from_shards(g_uq)), w_ukv, _cols_from_shards(g_conv),
            w_mkv, w_o)


def _layer_fwd(l, x, mem, wts, gains, tabs, comm):
    w_in, w_uq, w_ukv, conv_w, w_mkv, w_o = wts
    g_pre, g_q, g_kv, g_mem, g_post = gains
    q_tab, k_tab = tabs
    tag = f"l{l}_"
    h = _rmsnorm_fwd(x, g_pre, 0, D_MODEL, tag + "pre_norm")
    z = _matmul(h, w_in, "nn", F32, tag + "in_proj", tn_cap=1664)
    qn = _rmsnorm_fwd(z, g_q, Z_QLAT, Q_RANK, tag + "q_norm")
    kvn = _rmsnorm_fwd(z, g_kv, Z_KVLAT, KV_RANK, tag + "kv_norm")
    q_raw = _matmul(qn, w_uq, "nn", F32, tag + "uq")
    kv = _matmul(kvn, w_ukv, "nn", BF16, tag + "ukv")
    q_r = _rope(q_raw, *q_tab, 0, QPAD, MLA_HEADS, tag + "q_rope")
    kpe = _rope(z, *k_tab, Z_KPE, LANES, 1, tag + "k_rope")
    a_out, a_lse, arrived = _attn_fwd(q_r, kv, kpe, kv, MLA_HEADS, QPAD, 0, 0, 2, 1, 2, QK_HEAD ** -0.5, 256,
                                      tag + "mla_fwd", comm)
    c_out = _conv_fwd(z, conv_w, tag + "conv_fwd")
    mem_n = _rmsnorm_fwd(mem, g_mem, 0, D_MODEL, tag + "mem_norm")
    mkv = _matmul(mem_n, w_mkv, "nn", BF16, tag + "mem_kv")
    m_out, m_lse, _ = _attn_fwd(z, mkv, None, mkv, MEM_HEADS, LANES, Z_QMEM // LANES, 0, 1, MEM_HEADS, 1, MEM_HEAD ** -0.5,
                                1024, tag + "mem_fwd")
    cat = jnp.concatenate([a_out, c_out, m_out], axis=-1)
    y = _gate_fwd(cat, z, tag + "gate_fwd")
    o = _matmul(y, w_o, "nn", F32, tag + "out_proj")
    x_new = _post_norm_residual(x, o, g_post, tag + "post_norm")
    saved = (x, h, z, qn, kvn, q_r, kv, kpe, a_lse, mem_n, mkv, m_lse, cat, y, o)
    return x_new, saved, arrived


def _layer_bwd(l, g, mem, saved, wts, gains, tabs_bwd, comm):
    w_in, w_uq, w_ukv, conv_w, w_mkv, w_o = wts
    g_pre, g_q, g_kv, g_mem, g_post = gains
    q_tab, k_tab = tabs_bwd
    x, h, z, qn, kvn, q_r, kv, kpe, a_lse, mem_n, mkv, m_lse, cat, y, o = saved
    tag = f"l{l}_"
    do, dg_post = _rmsnorm_bwd(o, g_post, g, None, 0, D_MODEL, BF16, tag + "post_norm_bwd")
    dy = _matmul(do, w_o, "nt", F32, tag + "out_proj_dx")
    dw_o = _matmul(y, do, "tn", BF16, tag + "out_proj_dw")
    dcat, dgate = _gate_bwd(dy, cat, z, tag + "gate_bwd")
    dq_raw, dkv, dkpe_h, arrived = _attn_bwd(q_r, kv, kpe, kv, cat, dcat, a_lse, MLA_HEADS, QPAD, 0, 0, 2, 1, 2, 0,
                                             QK_HEAD ** -0.5, 256, F32, tag + "mla_bwd", comm)
    dq = _rope(dq_raw, *q_tab, 0, QPAD, MLA_HEADS, tag + "q_rope_bwd")
    dkpe = _kpe_grad(dkpe_h, *k_tab, MLA_HEADS, tag + "k_rope_bwd")
    dw_ukv = _matmul(kvn, dkv, "tn", BF16, tag + "ukv_dw")
    dkvn = _matmul(dkv, w_ukv, "nt", F32, tag + "ukv_dx")
    dkv_lat, dg_kv = _rmsnorm_bwd(z, g_kv, dkvn, None, Z_KVLAT, KV_RANK, BF16, tag + "kv_norm_bwd")
    dw_uq = _matmul(qn, dq, "tn", BF16, tag + "uq_dw")
    dqn = _matmul(dq, w_uq, "nt", F32, tag + "uq_dx")
    dq_lat, dg_q = _rmsnorm_bwd(z, g_q, dqn, None, Z_QLAT, Q_RANK, BF16, tag + "q_norm_bwd")
    dgb, dgc, dxin, dconv_w = _conv_bwd(z, conv_w, dcat, tag + "conv_bwd")
    dq_mem, dmk, dmv, _ = _attn_bwd(z, mkv, None, mkv, cat, dcat, m_lse, MEM_HEADS, LANES, Z_QMEM // LANES, 0, 1, MEM_HEADS, 1,
                                    (MLA_W + CONV_W) // LANES, MEM_HEAD ** -0.5, 1024, BF16, tag + "mem_bwd")
    dmkv = jnp.concatenate([dmk, dmv], axis=-1)
    dw_mkv = _matmul(mem_n, dmkv, "tn", BF16, tag + "mem_kv_dw")
    dmem_n = _matmul(dmkv, w_mkv, "nt", F32, tag + "mem_kv_dx")
    _, dg_mem = _rmsnorm_bwd(mem, g_mem, dmem_n, None, 0, D_MODEL, BF16, tag + "mem_norm_bwd")
    dz = jnp.concatenate([dgate, dq_lat, dkv_lat, dkpe, dgb, dgc, dxin, dq_mem], axis=-1)
    dw_in = _matmul(h, dz, "tn", BF16, tag + "in_proj_dw", tn_cap=1664)
    dh = _matmul(dz, w_in, "nt", F32, tag + "in_proj_dx", tk_cap=1664)
    dx, dg_pre = _rmsnorm_bwd(x, g_pre, dh, g, 0, D_MODEL, F32, tag + "pre_norm_bwd")
    contrib = _scatter_plan(_cols_to_shards(_w_in_from_z_layout(dw_in)), _cols_to_shards(_w_uq_unpad(dw_uq)), dw_ukv,
                            _cols_to_shards(dconv_w), dw_mkv, dw_o)
    return dx, contrib, (dg_pre, dg_q, dg_kv, dg_mem, dg_post), arrived


GAIN_WIDTHS = (D_MODEL, Q_RANK, KV_RANK, D_MODEL, D_MODEL)


def _pack_gains(parts):
    return jnp.concatenate([p.reshape(-1) for p in parts]).reshape(-1, LANES)


def _unpack_gains(packed, depth):
    flat = packed.reshape(-1)
    out, at = [], 0
    for width in GAIN_WIDTHS:
        out.append(flat[at:at + depth * width].reshape(depth, width))
        at += depth * width
    return out


def kernel(x, mem, positions, pre_norm_g, w_in, q_norm_g, w_uq, kv_norm_g, w_ukv, conv_w, mem_norm_g, w_mk, w_mv, w_o, post_norm_g, loss_target, m_pre_norm_g, m_w_in, m_q_norm_g, m_w_uq, m_kv_norm_g, m_w_ukv, m_conv_w, m_mem_norm_g, m_w_mk, m_w_mv, m_w_o, m_post_norm_g, v_pre_norm_g, v_w_in, v_q_norm_g, v_w_uq, v_kv_norm_g, v_w_ukv, v_conv_w, v_mem_norm_g, v_w_mk, v_w_mv, v_w_o, v_post_norm_g):
    depth = w_in.shape[0]
    x0, mem0, target = x[0], mem[0], loss_target[0]
    tabs = _rope_tables(positions[0])
    tabs_bwd = tuple((c, -a, -b) for (c, a, b) in tabs)

    shards = [w_in.astype(BF16), w_uq.astype(BF16), w_ukv.astype(BF16), conv_w, w_mk.astype(BF16), w_mv.astype(BF16),
              w_o.astype(BF16)]

    def layer_gains(l):
        return tuple(g[l][None, :] for g in (pre_norm_g, q_norm_g, kv_norm_g, mem_norm_g, post_norm_g))

    wts, saved = [None] * depth, [None] * depth
    wts[0] = _layer_weights(_comm_call(_gather_plan(0, shards), "l0_weight_gather"))
    act = x0
    for l in range(depth):
        comm = _gather_plan(l + 1, shards) if l + 1 < depth else None
        act, saved[l], arrived = _layer_fwd(l, act, mem0, wts[l], layer_gains(l), tabs, comm)
        if comm is not None:
            wts[l + 1] = _layer_weights(arrived)
    grad, loss_part = _loss_head(act, target, "loss_head")
    loss = lax.psum(loss_part[0, 0], ("x", "y", "c"))

    names = ("w_in", "w_uq", "w_ukv", "conv_w", "w_mk", "w_mv", "w_o")
    w_shards = (w_in, w_uq, w_ukv, conv_w, w_mk, w_mv, w_o)
    m_shards = (m_w_in, m_w_uq, m_w_ukv, m_conv_w, m_w_mk, m_w_mv, m_w_o)
    v_shards = (v_w_in, v_w_uq, v_w_ukv, v_conv_w, v_w_mk, v_w_mv, v_w_o)
    stacked = [None] * N_WEIGHTS

    def finish(l, received):
        partial = [_sum_slots(r, f"l{l}_grad_sum_{names[i]}") for i, r in enumerate(received)]
        other = _sibling_exchange(partial, f"l{l}_grad_sibling")
        for i, name in enumerate(names):
            stacked[i] = _adamw_layer(l, w_shards[i], partial[i], other[i], m_shards[i], v_shards[i], stacked[i],
                                      f"l{l}_adamw_{name}")

    dgs = [None] * depth
    pending = None
    for l in reversed(range(depth)):
        grad, contrib, dgs[l], arrived = _layer_bwd(l, grad, mem0, saved[l], wts[l], layer_gains(l), tabs_bwd, pending)
        if pending is not None:
            finish(l + 1, arrived)
        pending = contrib
    finish(0, _comm_call(pending, "l0_grad_exchange"))
    grad_x = grad[None]
    results = {name: tuple(stacked[i]) for i, name in enumerate(names)}

    gain_names = ("pre_norm_g", "q_norm_g", "kv_norm_g", "mem_norm_g", "post_norm_g")
    dg_packed = _pack_gains([jnp.concatenate([dgs[l][i] for l in range(depth)], axis=0) for i in range(5)])
    dg_total = _sum_slots(_gather_all(dg_packed, "gain_gather"), "gain_sum")
    gain_outs = (dg_total,) + tuple(_adamw(
        _pack_gains((pre_norm_g, q_norm_g, kv_norm_g, mem_norm_g, post_norm_g)), dg_total,
        _pack_gains((m_pre_norm_g, m_q_norm_g, m_kv_norm_g, m_mem_norm_g, m_post_norm_g)),
        _pack_gains((v_pre_norm_g, v_q_norm_g, v_kv_norm_g, v_mem_norm_g, v_post_norm_g)), "adamw_gains"))
    gain_outs = [_unpack_gains(t, depth) for t in gain_outs]
    for i, name in enumerate(gain_names):
        results[name] = tuple(gain_outs[k][i] for k in range(4))

    order = ("pre_norm_g", "w_in", "q_norm_g", "w_uq", "kv_norm_g", "w_ukv", "conv_w", "mem_norm_g", "w_mk", "w_mv", "w_o",
             "post_norm_g")
    out = [loss, grad_x]
    for k in range(4):
        out += [results[name][k] for name in order]
    return tuple(out)
```

```python
import functools

import jax
import jax.numpy as jnp
from jax import lax
from jax.experimental import pallas as pl
from jax.experimental.pallas import tpu as pltpu

F32 = jnp.float32
BF16 = jnp.bfloat16
MESH_ID = pl.DeviceIdType.MESH

D_MODEL = 2048
EPS = 1e-6
ROPE_THETA = 10000.0
MLA_HEADS = 8
NOPE = 128
ROPE = 64
HALF_ROPE = ROPE // 2
QK_HEAD = NOPE + ROPE
V_HEAD = 128
Q_RANK = 512
KV_RANK = 256
CONV_W = 512
MEM_HEADS = 4
MEM_HEAD = 128
MEM_W = MEM_HEADS * MEM_HEAD
MLA_W = MLA_HEADS * V_HEAD
MIX_W = MLA_W + CONV_W + MEM_W
IN_COLS = Q_RANK + KV_RANK + ROPE + 3 * CONV_W + MEM_W + MIX_W
N_CHIPS = 4
N_DEV = 8

LANES = 128
VMEM_LIMIT_BYTES = 56 * 1024 * 1024

QPAD = 2 * LANES
Z_GATE = 0
Z_QLAT = Z_GATE + MIX_W
Z_KVLAT = Z_QLAT + Q_RANK
Z_KPE = Z_KVLAT + KV_RANK
Z_GB = Z_KPE + LANES
Z_GC = Z_GB + CONV_W
Z_XIN = Z_GC + CONV_W
Z_QMEM = Z_XIN + CONV_W
Z_COLS = Z_QMEM + MEM_W

ADAM_LR = 0.001
ADAM_B1 = 0.9
ADAM_B2 = 0.999
ADAM_EPS = 1e-08
ADAM_WD = 0.01
ADAM_STEP = 10


def _tile(dim, cap, unit):
    if dim <= cap:
        return dim
    t = (cap // unit) * unit
    while t >= unit:
        if dim % t == 0:
            return t
        t -= unit
    raise ValueError(f"no tile of {dim} under {cap} in units of {unit}")


def _params(*semantics):
    return pltpu.CompilerParams(dimension_semantics=semantics, vmem_limit_bytes=VMEM_LIMIT_BYTES)


def _matmul(a, b, mode, out_dtype, name, tm_cap=512, tn_cap=1024, tk_cap=2048):
    if mode == "nn":
        (m, k), (k2, n) = a.shape, b.shape
    elif mode == "nt":
        (m, k), (n, k2) = a.shape, b.shape
    else:
        (k, m), (k2, n) = a.shape, b.shape
    assert k == k2, (a.shape, b.shape, mode)
    tm = _tile(m, tm_cap, LANES if mode == "tn" else 16)
    tn = _tile(n, tn_cap, LANES)
    tk = _tile(k, tk_cap, LANES if mode != "tn" else 16)
    nk = k // tk
    if mode == "nn":
        a_spec = pl.BlockSpec((tm, tk), lambda i, j, kk: (i, kk))
        b_spec = pl.BlockSpec((tk, tn), lambda i, j, kk: (kk, j))
        dims = (((1,), (0,)), ((), ()))
    elif mode == "nt":
        a_spec = pl.BlockSpec((tm, tk), lambda i, j, kk: (i, kk))
        b_spec = pl.BlockSpec((tn, tk), lambda i, j, kk: (j, kk))
        dims = (((1,), (1,)), ((), ()))
    else:
        a_spec = pl.BlockSpec((tk, tm), lambda i, j, kk: (kk, i))
        b_spec = pl.BlockSpec((tk, tn), lambda i, j, kk: (kk, j))
        dims = (((0,), (0,)), ((), ()))

    def body(a_ref, b_ref, o_ref, *scratch):
        part = lax.dot_general(a_ref[...].astype(BF16), b_ref[...].astype(BF16), dims, preferred_element_type=F32)
        if nk == 1:
            o_ref[...] = part.astype(o_ref.dtype)
            return
        (acc_ref,) = scratch
        kk = pl.program_id(2)

        @pl.when(kk == 0)
        def _():
            acc_ref[...] = part

        @pl.when(kk > 0)
        def _():
            acc_ref[...] += part

        @pl.when(kk == nk - 1)
        def _():
            o_ref[...] = acc_ref[...].astype(o_ref.dtype)

    return pl.pallas_call(
        body,
        grid=(m // tm, n // tn, nk),
        in_specs=[a_spec, b_spec],
        out_specs=pl.BlockSpec((tm, tn), lambda i, j, kk: (i, j)),
        out_shape=jax.ShapeDtypeStruct((m, n), out_dtype),
        scratch_shapes=[] if nk == 1 else [pltpu.VMEM((tm, tn), F32)],
        compiler_params=_params("parallel", "parallel", "arbitrary"),
        name=name,
    )(a, b)


def _rmsnorm_fwd(x, gain, col0, width, name):
    rows = x.shape[0]
    tr = _tile(rows, 512, 16)
    cb = col0 // width
    assert cb * width == col0

    def body(x_ref, g_ref, o_ref):
        xv = x_ref[...]
        r = lax.rsqrt(jnp.mean(xv * xv, axis=-1, keepdims=True) + EPS)
        o_ref[...] = (xv * r * g_ref[...]).astype(o_ref.dtype)

    return pl.pallas_call(
        body,
        grid=(rows // tr,),
        in_specs=[pl.BlockSpec((tr, width), lambda i: (i, cb)), pl.BlockSpec((1, width), lambda i: (0, 0))],
        out_specs=pl.BlockSpec((tr, width), lambda i: (i, 0)),
        out_shape=jax.ShapeDtypeStruct((rows, width), BF16),
        compiler_params=_params("parallel"),
        name=name,
    )(x, gain)


def _rmsnorm_bwd(x, gain, dy, resid, col0, width, out_dtype, name):
    rows = x.shape[0]
    tr = _tile(rows, 256, 16)
    cb = col0 // width
    assert cb * width == col0
    has_resid = resid is not None

    def body(*refs):
        if has_resid:
            x_ref, g_ref, dy_ref, res_ref, dx_ref, dg_ref = refs
        else:
            x_ref, g_ref, dy_ref, dx_ref, dg_ref = refs
        i = pl.program_id(0)
        xv = x_ref[...]
        dyv = dy_ref[...].astype(F32)
        r = lax.rsqrt(jnp.mean(xv * xv, axis=-1, keepdims=True) + EPS)
        xr = xv * r
        dyg = dyv * g_ref[...]
        c = jnp.mean(dyg * xr, axis=-1, keepdims=True)
        dx = r * (dyg - xr * c)
        if has_resid:
            dx = dx + res_ref[...]
        dx_ref[...] = dx.astype(dx_ref.dtype)
        part = jnp.sum(dyv * xr, axis=0, keepdims=True)

        @pl.when(i == 0)
        def _():
            dg_ref[...] = part

        @pl.when(i > 0)
        def _():
            dg_ref[...] += part

    row_spec = pl.BlockSpec((tr, width), lambda i: (i, 0))
    in_specs = [pl.BlockSpec((tr, width), lambda i: (i, cb)), pl.BlockSpec((1, width), lambda i: (0, 0)), row_spec]
    args = [x, gain, dy]
    if has_resid:
        in_specs.append(row_spec)
        args.append(resid)
    return pl.pallas_call(
        body,
        grid=(rows // tr,),
        in_specs=in_specs,
        out_specs=[row_spec, pl.BlockSpec((1, width), lambda i: (0, 0))],
        out_shape=[jax.ShapeDtypeStruct((rows, width), out_dtype), jax.ShapeDtypeStruct((1, width), F32)],
        compiler_params=_params("arbitrary"),
        name=name,
    )(*args)


def _post_norm_residual(x, o, gain, name):
    rows, width = x.shape
    tr = _tile(rows, 256, 8)

    def body(x_ref, o_ref, g_ref, out_ref):
        ov = o_ref[...]
        r = lax.rsqrt(jnp.mean(ov * ov, axis=-1, keepdims=True) + EPS)
        out_ref[...] = x_ref[...] + ov * r * g_ref[...]

    row_spec = pl.BlockSpec((tr, width), lambda i: (i, 0))
    return pl.pallas_call(
        body,
        grid=(rows // tr,),
        in_specs=[row_spec, row_spec, pl.BlockSpec((1, width), lambda i: (0, 0))],
        out_specs=row_spec,
        out_shape=jax.ShapeDtypeStruct((rows, width), F32),
        compiler_params=_params("parallel"),
        name=name,
    )(x, o, gain)


def _rope(x, tab_c, tab_a, tab_b, col0, width, heads, name):
    rows = x.shape[0]
    tr = _tile(rows, 512, 16)
    cb = col0 // width
    assert cb * width == col0

    def body(x_ref, c_ref, a_ref, b_ref, o_ref):
        xv = x_ref[...].astype(F32)
        up = pltpu.roll(xv, width - HALF_ROPE, 1)
        down = pltpu.roll(xv, HALF_ROPE, 1)
        o_ref[...] = (xv * c_ref[...] + up * a_ref[...] + down * b_ref[...]).astype(o_ref.dtype)

    tab_spec = pl.BlockSpec((tr, width), lambda i, h: (i, 0))
    return pl.pallas_call(
        body,
        grid=(rows // tr, heads),
        in_specs=[pl.BlockSpec((tr, width), lambda i, h: (i, cb + h)), tab_spec, tab_spec, tab_spec],
        out_specs=pl.BlockSpec((tr, width), lambda i, h: (i, h)),
        out_shape=jax.ShapeDtypeStruct((rows, heads * width), BF16),
        compiler_params=_params("parallel", "parallel"),
        name=name,
    )(x, tab_c, tab_a, tab_b)


def _kpe_grad(dkb, tab_c, tab_a, tab_b, heads, name):
    rows = dkb.shape[0]
    tr = _tile(rows, 512, 16)

    def body(d_ref, c_ref, a_ref, b_ref, o_ref):
        acc = d_ref[:, 0:LANES]
        for h in range(1, heads):
            acc = acc + d_ref[:, h * LANES:(h + 1) * LANES]
        up = pltpu.roll(acc, LANES - HALF_ROPE, 1)
        down = pltpu.roll(acc, HALF_ROPE, 1)
        o_ref[...] = (acc * c_ref[...] + up * a_ref[...] + down * b_ref[...]).astype(o_ref.dtype)

    tab_spec = pl.BlockSpec((tr, LANES), lambda i: (i, 0))
    return pl.pallas_call(
        body,
        grid=(rows // tr,),
        in_specs=[pl.BlockSpec((tr, heads * LANES), lambda i: (i, 0)), tab_spec, tab_spec, tab_spec],
        out_specs=tab_spec,
        out_shape=jax.ShapeDtypeStruct((rows, LANES), BF16),
        compiler_params=_params("parallel"),
        name=name,
    )(dkb, tab_c, tab_a, tab_b)


class _CommPlan:
    def __init__(self, ins, out_shape, build, n_copies):
        self.ins, self.out_shape, self.build, self.n_copies = list(ins), list(out_shape), build, n_copies

    def scratch(self):
        n = self.n_copies
        return [pltpu.SemaphoreType.DMA((n,)), pltpu.SemaphoreType.DMA((n,)), pltpu.SemaphoreType.DMA((n,))]


def _split_comm(refs, n_in, n_out, comm):
    if comm is None:
        return refs, None
    ci, co = len(comm.ins), len(comm.out_shape)
    ins, c_ins = refs[:n_in], refs[n_in:n_in + ci]
    outs, c_outs = refs[n_in + ci:n_in + ci + n_out], refs[n_in + ci + n_out:n_in + ci + n_out + co]
    rest = refs[n_in + ci + n_out + co:]
    scratch, sems = rest[:-3], rest[-3:]
    return tuple(ins) + tuple(outs) + tuple(scratch), functools.partial(comm.build, c_ins, c_outs, sems)


def _ride_start(copies, first):
    if copies is not None:
        pl.when(first)(copies()[0])


def _ride_wait(copies, last):
    if copies is not None:
        pl.when(last)(copies()[1])


def _rope_rows(x, c, a, b, sign):
    width = x.shape[-1]
    mixed = pltpu.roll(x, width - HALF_ROPE, 1) * a + pltpu.roll(x, HALF_ROPE, 1) * b
    return x * c + mixed if sign > 0 else x * c - mixed


def _attn_fwd(q, ka, kb, v, rope, heads, q_w, q_cb, ka_cb, ka_step, v_cb, v_step, scale, tq_cap, name, comm=None):
    s_q, s_k = q.shape[0], ka.shape[0]
    tq = _tile(s_q, tq_cap, 16)
    nq = s_q // tq
    has_kb = kb is not None
    n_in = 7 if has_kb else 3

    def body(*refs):
        refs, copies = _split_comm(refs, n_in, 2, comm)
        first = jnp.logical_and(pl.program_id(0) == 0, pl.program_id(1) == 0)
        last = jnp.logical_and(pl.program_id(0) == heads - 1, pl.program_id(1) == nq - 1)
        _ride_start(copies, first)
        if has_kb:
            q_ref, ka_ref, kb_ref, v_ref, c_ref, a_ref, b_ref, o_ref, lse_ref, k_scr = refs

            @pl.when(pl.program_id(1) == 0)
            def _():
                k_scr[:, 0:LANES] = ka_ref[...].astype(BF16)
                k_scr[:, LANES:2 * LANES] = kb_ref[...].astype(BF16)

            kmat = k_scr[...]
            qv = _rope_rows(q_ref[...], c_ref[...], a_ref[...], b_ref[...], 1).astype(BF16)
        else:
            q_ref, ka_ref, v_ref, o_ref, lse_ref = refs
            kmat = ka_ref[...].astype(BF16)
            qv = q_ref[...].astype(BF16)
        s = lax.dot_general(qv, kmat, (((1,), (1,)), ((), ())), preferred_element_type=F32) * scale
        m = jnp.max(s, axis=-1, keepdims=True)
        p = jnp.exp(s - m)
        l = jnp.sum(p, axis=-1, keepdims=True)
        o = jnp.dot(p.astype(BF16), v_ref[...].astype(BF16), preferred_element_type=F32)
        o_ref[...] = (o * (1.0 / l)).astype(o_ref.dtype)
        lse_ref[...] = jnp.broadcast_to(m + jnp.log(l), lse_ref.shape)
        _ride_wait(copies, last)

    in_specs = [pl.BlockSpec((tq, q_w), lambda h, i: (i, q_cb + h)),
                pl.BlockSpec((s_k, LANES), lambda h, i: (0, ka_cb + ka_step * h))]
    args = [q, ka]
    if has_kb:
        in_specs.append(pl.BlockSpec((s_k, LANES), lambda h, i: (0, 0)))
        args.append(kb)
    in_specs.append(pl.BlockSpec((s_k, LANES), lambda h, i: (0, v_cb + v_step * h)))
    args.append(v)
    if has_kb:
        in_specs += [pl.BlockSpec((tq, q_w), lambda h, i: (i, 0))] * 3
        args += list(rope)
    out_spec = pl.BlockSpec((tq, LANES), lambda h, i: (i, h))
    out_specs = [out_spec, out_spec]
    out_shape = [jax.ShapeDtypeStruct((s_q, heads * LANES), BF16), jax.ShapeDtypeStruct((s_q, heads * LANES), F32)]
    scratch = [pltpu.VMEM((s_k, 2 * LANES), BF16)] if has_kb else []
    if comm is not None:
        in_specs += [ANY] * len(comm.ins)
        args += comm.ins
        out_specs += [ANY] * len(comm.out_shape)
        out_shape += comm.out_shape
        scratch += comm.scratch()
    res = pl.pallas_call(
        body,
        grid=(heads, nq),
        in_specs=in_specs,
        out_specs=out_specs,
        out_shape=out_shape,
        scratch_shapes=scratch,
        compiler_params=_params("arbitrary", "arbitrary"),
        name=name,
    )(*args)
    return res[0], res[1], list(res[2:])


def _attn_bwd(q, ka, kb, v, o, do, lse, rope, heads, q_w, q_cb, ka_cb, ka_step, v_cb, v_step, o_cb, scale, tq_cap, name,
              comm=None):
    s_q, s_k = q.shape[0], ka.shape[0]
    tq = _tile(s_q, tq_cap, 16)
    nq = s_q // tq
    has_kb = kb is not None
    n_in = 10 if has_kb else 6
    n_out = 3

    def body(*refs):
        refs, copies = _split_comm(refs, n_in, n_out, comm)
        first = jnp.logical_and(pl.program_id(0) == 0, pl.program_id(1) == 0)
        last = jnp.logical_and(pl.program_id(0) == heads - 1, pl.program_id(1) == nq - 1)
        _ride_start(copies, first)
        if has_kb:
            (q_ref, ka_ref, kb_ref, v_ref, o_ref, do_ref, lse_ref, c_ref, a_ref, b_ref, dq_ref, dkv_ref, dkb_ref, k_scr, dk_acc,
             dv_acc) = refs
        else:
            q_ref, ka_ref, v_ref, o_ref, do_ref, lse_ref, dq_ref, dka_ref, dv_ref, dk_acc, dv_acc = refs
        i = pl.program_id(1)

        @pl.when(i == 0)
        def _():
            dk_acc[...] = jnp.zeros_like(dk_acc)
            dv_acc[...] = jnp.zeros_like(dv_acc)
            if has_kb:
                k_scr[:, 0:LANES] = ka_ref[...].astype(BF16)
                k_scr[:, LANES:2 * LANES] = kb_ref[...].astype(BF16)

        kmat = k_scr[...] if has_kb else ka_ref[...].astype(BF16)
        if has_kb:
            qv = _rope_rows(q_ref[...], c_ref[...], a_ref[...], b_ref[...], 1).astype(BF16)
        else:
            qv = q_ref[...].astype(BF16)
        dov = do_ref[...].astype(BF16)
        delta = jnp.sum(dov.astype(F32) * o_ref[...].astype(F32), axis=-1, keepdims=True)
        s = lax.dot_general(qv, kmat, (((1,), (1,)), ((), ())), preferred_element_type=F32) * scale
        p = jnp.exp(s - lse_ref[:, 0:1])
        dp = lax.dot_general(dov, v_ref[...].astype(BF16), (((1,), (1,)), ((), ())), preferred_element_type=F32)
        ds = (p * (dp - delta) * scale).astype(BF16)
        dq = jnp.dot(ds, kmat, preferred_element_type=F32)
        if has_kb:
            dq = _rope_rows(dq, c_ref[...], a_ref[...], b_ref[...], -1)
        dq_ref[...] = dq.astype(dq_ref.dtype)
        dk_acc[...] += lax.dot_general(ds, qv, (((0,), (0,)), ((), ())), preferred_element_type=F32)
        dv_acc[...] += lax.dot_general(p.astype(BF16), dov, (((0,), (0,)), ((), ())), preferred_element_type=F32)

        @pl.when(i == nq - 1)
        def _():
            if has_kb:
                dkv_ref[:, 0:LANES] = dk_acc[:, 0:LANES].astype(dkv_ref.dtype)
                dkv_ref[:, LANES:2 * LANES] = dv_acc[...].astype(dkv_ref.dtype)
                dkb_ref[...] = dk_acc[:, LANES:2 * LANES]
            else:
                dka_ref[...] = dk_acc[...].astype(dka_ref.dtype)
                dv_ref[...] = dv_acc[...].astype(dv_ref.dtype)

        _ride_wait(copies, last)

    key_spec = lambda cb, step: pl.BlockSpec((s_k, LANES), lambda h, i: (0, cb + step * h))
    row_spec = lambda cb: pl.BlockSpec((tq, LANES), lambda h, i: (i, cb + h))
    in_specs = [pl.BlockSpec((tq, q_w), lambda h, i: (i, q_cb + h)), key_spec(ka_cb, ka_step)]
    args = [q, ka]
    if has_kb:
        in_specs.append(pl.BlockSpec((s_k, LANES), lambda h, i: (0, 0)))
        args.append(kb)
    in_specs += [key_spec(v_cb, v_step), row_spec(o_cb), row_spec(o_cb), row_spec(0)]
    args += [v, o, do, lse]
    if has_kb:
        in_specs += [pl.BlockSpec((tq, q_w), lambda h, i: (i, 0))] * 3
        args += list(rope)
    out_specs = [pl.BlockSpec((tq, q_w), lambda h, i: (i, h))]
    out_shape = [jax.ShapeDtypeStruct((s_q, heads * q_w), BF16)]
    scratch = []
    if has_kb:
        out_specs += [pl.BlockSpec((s_k, 2 * LANES), lambda h, i: (0, h)), key_spec(0, 1)]
        out_shape += [jax.ShapeDtypeStruct((s_k, heads * 2 * LANES), BF16), jax.ShapeDtypeStruct((s_k, heads * LANES), F32)]
        scratch.append(pltpu.VMEM((s_k, 2 * LANES), BF16))
    else:
        out_specs += [key_spec(0, 1), key_spec(0, 1)]
        out_shape += [jax.ShapeDtypeStruct((s_k, heads * LANES), BF16)] * 2
    scratch += [pltpu.VMEM((s_k, q_w), F32), pltpu.VMEM((s_k, LANES), F32)]
    if comm is not None:
        in_specs += [ANY] * len(comm.ins)
        args += comm.ins
        out_specs += [ANY] * len(comm.out_shape)
        out_shape += comm.out_shape
        scratch += comm.scratch()
    res = pl.pallas_call(
        body,
        grid=(heads, nq),
        in_specs=in_specs,
        out_specs=out_specs,
        out_shape=out_shape,
        scratch_shapes=scratch,
        compiler_params=_params("arbitrary", "arbitrary"),
        name=name,
    )(*args)
    return res[0], res[1], res[2], list(res[3:])


def _shift_rows(u, rows):
    t = lax.broadcasted_iota(jnp.int32, u.shape, 0)
    prev = jnp.where(t == 0, 0.0, pltpu.roll(u, 1, 0))
    nxt = jnp.where(t == rows - 1, 0.0, pltpu.roll(u, rows - 1, 0))
    return prev, nxt


def _conv_fwd(z, conv_w, name):
    rows = z.shape[0]
    nblk = CONV_W // LANES

    def body(gb_ref, gc_ref, xin_ref, w_ref, o_ref):
        u = gc_ref[...] * xin_ref[...]
        prev, nxt = _shift_rows(u, rows)
        conv = prev * w_ref[0:1, :] + u * w_ref[1:2, :] + nxt * w_ref[2:3, :]
        o_ref[...] = (gb_ref[...] * conv).astype(o_ref.dtype)

    col = lambda c0: pl.BlockSpec((rows, LANES), lambda j: (0, c0 // LANES + j))
    return pl.pallas_call(
        body,
        grid=(nblk,),
        in_specs=[col(Z_GB), col(Z_GC), col(Z_XIN), pl.BlockSpec((3, LANES), lambda j: (0, j))],
        out_specs=col(0),
        out_shape=jax.ShapeDtypeStruct((rows, CONV_W), BF16),
        compiler_params=_params("parallel"),
        name=name,
    )(z, z, z, conv_w)


def _conv_bwd(z, conv_w, dcat, name):
    rows = z.shape[0]
    nblk = CONV_W // LANES

    def body(gb_ref, gc_ref, xin_ref, w_ref, dc_ref, dgb_ref, dgc_ref, dxin_ref, dw_ref):
        gc = gc_ref[...]
        xin = xin_ref[...]
        dc = dc_ref[...].astype(F32)
        u = gc * xin
        prev, nxt = _shift_rows(u, rows)
        w0, w1, w2 = w_ref[0:1, :], w_ref[1:2, :], w_ref[2:3, :]
        conv = prev * w0 + u * w1 + nxt * w2
        dgb_ref[...] = (dc * conv).astype(dgb_ref.dtype)
        dconv = dc * gb_ref[...]
        dw_ref[0:1, :] = jnp.sum(dconv * prev, axis=0, keepdims=True)
        dw_ref[1:2, :] = jnp.sum(dconv * u, axis=0, keepdims=True)
        dw_ref[2:3, :] = jnp.sum(dconv * nxt, axis=0, keepdims=True)
        dprev, dnxt = _shift_rows(dconv, rows)
        du = dnxt * w0 + dconv * w1 + dprev * w2
        dgc_ref[...] = (du * xin).astype(dgc_ref.dtype)
        dxin_ref[...] = (du * gc).astype(dxin_ref.dtype)

    col = lambda c0: pl.BlockSpec((rows, LANES), lambda j: (0, c0 // LANES + j))
    w_spec = pl.BlockSpec((3, LANES), lambda j: (0, j))
    piece = jax.ShapeDtypeStruct((rows, CONV_W), BF16)
    return pl.pallas_call(
        body,
        grid=(nblk,),
        in_specs=[col(Z_GB), col(Z_GC), col(Z_XIN), w_spec, col(MLA_W)],
        out_specs=[col(0), col(0), col(0), w_spec],
        out_shape=[piece, piece, piece, jax.ShapeDtypeStruct((3, CONV_W), F32)],
        compiler_params=_params("parallel"),
        name=name,
    )(z, z, z, conv_w, dcat)


def _gate_fwd(cat, z, name):
    rows = cat.shape[0]
    tr = _tile(rows, 512, 16)
    tc = 512
    g0 = Z_GATE // tc

    def body(c_ref, g_ref, y_ref):
        g = g_ref[...]
        y_ref[...] = (c_ref[...].astype(F32) * (g * jax.nn.sigmoid(g))).astype(y_ref.dtype)

    blk = pl.BlockSpec((tr, tc), lambda i, j: (i, j))
    return pl.pallas_call(
        body,
        grid=(rows // tr, MIX_W // tc),
        in_specs=[blk, pl.BlockSpec((tr, tc), lambda i, j: (i, g0 + j))],
        out_specs=blk,
        out_shape=jax.ShapeDtypeStruct((rows, MIX_W), BF16),
        compiler_params=_params("parallel", "parallel"),
        name=name,
    )(cat, z)


def _gate_bwd(dy, cat, z, name):
    rows = cat.shape[0]
    tr = _tile(rows, 512, 16)
    tc = 512
    g0 = Z_GATE // tc

    def body(dy_ref, c_ref, g_ref, dcat_ref, dgate_ref):
        g = g_ref[...]
        sg = jax.nn.sigmoid(g)
        dyv = dy_ref[...].astype(F32)
        dcat_ref[...] = (dyv * (g * sg)).astype(dcat_ref.dtype)
        dgate_ref[...] = (dyv * c_ref[...].astype(F32) * (sg * (1.0 + g * (1.0 - sg)))).astype(dgate_ref.dtype)

    blk = pl.BlockSpec((tr, tc), lambda i, j: (i, j))
    out = jax.ShapeDtypeStruct((rows, MIX_W), BF16)
    return pl.pallas_call(
        body,
        grid=(rows // tr, MIX_W // tc),
        in_specs=[blk, blk, pl.BlockSpec((tr, tc), lambda i, j: (i, g0 + j))],
        out_specs=[blk, blk],
        out_shape=[out, out],
        compiler_params=_params("parallel", "parallel"),
        name=name,
    )(dy, cat, z)


def _loss_head(y, target, name):
    rows, width = y.shape
    tr = _tile(rows, 256, 8)

    def body(y_ref, t_ref, g_ref, loss_ref):
        i = pl.program_id(0)
        d = y_ref[...] - t_ref[...]
        g_ref[...] = d / width
        part = 0.5 * jnp.sum(jnp.mean(d * d, axis=-1, keepdims=True), axis=0, keepdims=True)
        part = jnp.broadcast_to(part, loss_ref.shape)

        @pl.when(i == 0)
        def _():
            loss_ref[...] = part

        @pl.when(i > 0)
        def _():
            loss_ref[...] += part

    row_spec = pl.BlockSpec((tr, width), lambda i: (i, 0))
    return pl.pallas_call(
        body,
        grid=(rows // tr,),
        in_specs=[row_spec, row_spec],
        out_specs=[row_spec, pl.BlockSpec((1, LANES), lambda i: (0, 0))],
        out_shape=[jax.ShapeDtypeStruct((rows, width), F32), jax.ShapeDtypeStruct((1, LANES), F32)],
        compiler_params=_params("arbitrary"),
        name=name,
    )(y, target)


CHIP_FLIPS = ((1, 0), (0, 1), (1, 1))
ANY = pl.BlockSpec(memory_space=pl.ANY)


def _chip_copies(pieces, sems, n_slot):
    send_sems, recv_sems, local_sems = sems
    x, y, c = lax.axis_index("x"), lax.axis_index("y"), lax.axis_index("c")
    me = 2 * x + y

    def remote(j, k, a, src, dst):
        fx, fy = CHIP_FLIPS[k]
        return pltpu.make_async_remote_copy(
            src_ref=src, dst_ref=dst, send_sem=send_sems.at[n_slot * k + a], recv_sem=recv_sems.at[n_slot * k + a],
            device_id=((j // 2) ^ fx, (j % 2) ^ fy, c), device_id_type=MESH_ID)

    def peer(j, k):
        fx, fy = CHIP_FLIPS[k]
        return 2 * ((j // 2) ^ fx) + ((j % 2) ^ fy)

    def start_as(j):
        def run():
            for a, (src, dst) in enumerate(pieces(j, j)):
                pltpu.make_async_copy(src, dst, local_sems.at[a]).start()
            for k in range(len(CHIP_FLIPS)):
                for a, (src, dst) in enumerate(pieces(j, peer(j, k))):
                    remote(j, k, a, src, dst).start()
        return run

    def wait_as(j):
        def run():
            for a, (src, dst) in enumerate(pieces(j, j)):
                pltpu.make_async_copy(src, dst, local_sems.at[a]).wait()
            for k in range(len(CHIP_FLIPS)):
                for a, (src, dst) in enumerate(pieces(j, peer(j, k))):
                    remote(j, k, a, src, dst).wait_send()
                for a, (src, dst) in enumerate(pieces(peer(j, k), j)):
                    remote(j, k, a, src, dst).wait_recv()
        return run

    def start():
        for j in range(N_CHIPS):
            pl.when(me == j)(start_as(j))

    def wait():
        for j in range(N_CHIPS):
            pl.when(me == j)(wait_as(j))

    return start, wait


IN_PIECES = ((0, Q_RANK, Z_QLAT), (Q_RANK, KV_RANK, Z_KVLAT), (Q_RANK + KV_RANK, ROPE, Z_KPE),
             (Q_RANK + KV_RANK + ROPE, CONV_W, Z_GB), (Q_RANK + KV_RANK + ROPE + CONV_W, CONV_W, Z_GC),
             (Q_RANK + KV_RANK + ROPE + 2 * CONV_W, CONV_W, Z_XIN), (Q_RANK + KV_RANK + ROPE + 3 * CONV_W, MEM_W, Z_QMEM),
             (Q_RANK + KV_RANK + ROPE + 3 * CONV_W + MEM_W, MIX_W, Z_GATE))
IN_SHARD = IN_COLS // N_CHIPS


def _in_segments(j):
    lo, hi = j * IN_SHARD, (j + 1) * IN_SHARD
    segs = []
    for r0, width, z0 in IN_PIECES:
        a, b = max(lo, r0), min(hi, r0 + width)
        if a < b:
            segs.append((a - lo, z0 + a - r0, b - a))
    return segs


N_SLOT = 11


def _gather_plan(l, shards, zero_rows):
    s_in, s_uq, s_ukv, s_conv, s_mk, s_mv, s_o = shards
    ukv_c, mk_r, mk_c, o_r = s_ukv.shape[2], s_mk.shape[1], s_mk.shape[2], s_o.shape[1]
    stack = lambda s: jax.ShapeDtypeStruct((N_CHIPS,) + s.shape[1:], s.dtype)
    out_shape = [jax.ShapeDtypeStruct((Z_COLS, s_in.shape[2]), s_in.dtype), stack(s_uq),
                 jax.ShapeDtypeStruct((s_ukv.shape[1], N_CHIPS * ukv_c), s_ukv.dtype), stack(s_conv),
                 jax.ShapeDtypeStruct((N_CHIPS * mk_r, 2 * mk_c), s_mk.dtype),
                 jax.ShapeDtypeStruct((N_CHIPS * o_r, s_o.shape[2]), s_o.dtype)]

    def build(ins, outs, sems):
        r_in, r_uq, r_ukv, r_conv, r_mk, r_mv, r_o, r_zero = ins
        f_in, g_uq, f_ukv, g_conv, f_mkv, f_o = outs

        def pieces(j, t):
            out = [(r_in.at[l, pl.ds(so, n), :], f_in.at[pl.ds(zo, n), :]) for so, zo, n in _in_segments(j)]
            out += [(r_uq.at[l], g_uq.at[j]), (r_ukv.at[l], f_ukv.at[:, pl.ds(j * ukv_c, ukv_c)]), (r_conv.at[l], g_conv.at[j]),
                    (r_mk.at[l], f_mkv.at[pl.ds(j * mk_r, mk_r), pl.ds(0, mk_c)]),
                    (r_mv.at[l], f_mkv.at[pl.ds(j * mk_r, mk_r), pl.ds(mk_c, mk_c)]),
                    (r_o.at[l], f_o.at[pl.ds(j * o_r, o_r), :])]
            if j == t:
                out.append((r_zero, f_in.at[pl.ds(Z_KPE + ROPE, LANES - ROPE), :]))
            return out

        return _chip_copies(pieces, sems, N_SLOT)

    return _CommPlan(list(shards) + [zero_rows], out_shape, build, len(CHIP_FLIPS) * N_SLOT)


def _scatter_plan(dwt_in, c_uq, dw_ukv, c_conv, dw_mkv, dw_o):
    ukv_c, mk_r, mk_c, o_r = dw_ukv.shape[1] // N_CHIPS, dw_mkv.shape[0] // N_CHIPS, dw_mkv.shape[1] // 2, dw_o.shape[0] // N_CHIPS
    out_shape = [jax.ShapeDtypeStruct((N_CHIPS, IN_SHARD, dwt_in.shape[1]), dwt_in.dtype),
                 jax.ShapeDtypeStruct(c_uq.shape, c_uq.dtype),
                 jax.ShapeDtypeStruct((N_CHIPS, dw_ukv.shape[0], ukv_c), dw_ukv.dtype),
                 jax.ShapeDtypeStruct(c_conv.shape, c_conv.dtype),
                 jax.ShapeDtypeStruct((N_CHIPS, mk_r, mk_c), dw_mkv.dtype), jax.ShapeDtypeStruct((N_CHIPS, mk_r, mk_c), dw_mkv.dtype),
                 jax.ShapeDtypeStruct((N_CHIPS, o_r, dw_o.shape[1]), dw_o.dtype)]

    def build(ins, outs, sems):
        r_in, r_uq, r_ukv, r_conv, r_mkv, r_o = ins
        o_in, o_uq, o_ukv, o_conv, o_mk, o_mv, o_o = outs

        def pieces(j, t):
            out = [(r_in.at[pl.ds(zo, n), :], o_in.at[j, pl.ds(so, n), :]) for so, zo, n in _in_segments(t)]
            out += [(r_uq.at[t], o_uq.at[j]), (r_ukv.at[:, pl.ds(t * ukv_c, ukv_c)], o_ukv.at[j]), (r_conv.at[t], o_conv.at[j]),
                    (r_mkv.at[pl.ds(t * mk_r, mk_r), pl.ds(0, mk_c)], o_mk.at[j]),
                    (r_mkv.at[pl.ds(t * mk_r, mk_r), pl.ds(mk_c, mk_c)], o_mv.at[j]),
                    (r_o.at[pl.ds(t * o_r, o_r), :], o_o.at[j])]
            return out

        return _chip_copies(pieces, sems, N_SLOT)

    return _CommPlan([dwt_in, c_uq, dw_ukv, c_conv, dw_mkv, dw_o], out_shape, build, len(CHIP_FLIPS) * N_SLOT)


def _comm_call(plan, name):
    n_in, n_out = len(plan.ins), len(plan.out_shape)

    def body(*refs):
        start, wait = plan.build(refs[:n_in], refs[n_in:n_in + n_out], refs[n_in + n_out:])
        start()
        wait()

    return list(pl.pallas_call(
        body,
        in_specs=[ANY] * n_in,
        out_specs=[ANY] * n_out,
        out_shape=plan.out_shape,
        scratch_shapes=plan.scratch(),
        name=name,
    )(*plan.ins))


def _sibling_exchange(arrays, name):
    n = len(arrays)

    def body(*refs):
        ins, outs = refs[:n], refs[n:2 * n]
        send_sems, recv_sems = refs[2 * n:]
        sibling = (lax.axis_index("x"), lax.axis_index("y"), 1 - lax.axis_index("c"))
        copies = []
        for a in range(n):
            cp = pltpu.make_async_remote_copy(
                src_ref=ins[a], dst_ref=outs[a], send_sem=send_sems.at[a], recv_sem=recv_sems.at[a],
                device_id=sibling, device_id_type=MESH_ID)
            cp.start()
            copies.append(cp)
        for cp in copies:
            cp.wait()

    return pl.pallas_call(
        body,
        in_specs=[ANY] * n,
        out_specs=[ANY] * n,
        out_shape=[jax.ShapeDtypeStruct(v.shape, v.dtype) for v in arrays],
        scratch_shapes=[pltpu.SemaphoreType.DMA((n,)), pltpu.SemaphoreType.DMA((n,))],
        name=name,
    )(*arrays)


DEVICE_FLIPS = tuple((fx, fy, fc) for fx in (0, 1) for fy in (0, 1) for fc in (0, 1))[1:]


def _gather_all(v, name):
    def body(v_ref, out_ref, send_sems, recv_sems, local_sem):
        x, y, c = lax.axis_index("x"), lax.axis_index("y"), lax.axis_index("c")
        me = 4 * x + 2 * y + c
        local = pltpu.make_async_copy(v_ref, out_ref.at[me], local_sem)
        local.start()
        copies = [local]
        for k, (fx, fy, fc) in enumerate(DEVICE_FLIPS):
            cp = pltpu.make_async_remote_copy(
                src_ref=v_ref, dst_ref=out_ref.at[me], send_sem=send_sems.at[k], recv_sem=recv_sems.at[k],
                device_id=((x + fx) % 2, (y + fy) % 2, (c + fc) % 2), device_id_type=MESH_ID)
            cp.start()
            copies.append(cp)
        for cp in copies:
            cp.wait()

    return pl.pallas_call(
        body,
        in_specs=[ANY],
        out_specs=ANY,
        out_shape=jax.ShapeDtypeStruct((N_DEV,) + v.shape, v.dtype),
        scratch_shapes=[pltpu.SemaphoreType.DMA((N_DEV - 1,)), pltpu.SemaphoreType.DMA((N_DEV - 1,)), pltpu.SemaphoreType.DMA],
        name=name,
    )(v)


def _sum_slots(parts, name):
    n, rows, cols = parts.shape
    tr = _tile(rows, 256, 16)

    def body(p_ref, o_ref):
        acc = p_ref[0].astype(F32)
        for k in range(1, n):
            acc = acc + p_ref[k].astype(F32)
        o_ref[...] = acc

    return pl.pallas_call(
        body,
        grid=(rows // tr,),
        in_specs=[pl.BlockSpec((n, tr, cols), lambda i: (0, i, 0))],
        out_specs=pl.BlockSpec((tr, cols), lambda i: (i, 0)),
        out_shape=jax.ShapeDtypeStruct((rows, cols), F32),
        compiler_params=_params("parallel"),
        name=name,
    )(parts)


def _adamw_math(w, g, m, v):
    m_new = ADAM_B1 * m + (1.0 - ADAM_B1) * g
    v_new = ADAM_B2 * v + (1.0 - ADAM_B2) * jnp.square(g)
    m_hat = m_new / (1.0 - ADAM_B1 ** ADAM_STEP)
    v_hat = v_new / (1.0 - ADAM_B2 ** ADAM_STEP)
    return -ADAM_LR * (m_hat / (jnp.sqrt(v_hat) + ADAM_EPS) + ADAM_WD * w), m_new, v_new


def _adamw(w, g, m, v, name):
    rows, cols = w.shape
    tr = _tile(rows, 256, 8)

    def body(w_ref, g_ref, m_ref, v_ref, d_out, m_out, v_out):
        d_out[...], m_out[...], v_out[...] = _adamw_math(w_ref[...], g_ref[...], m_ref[...], v_ref[...])

    blk = pl.BlockSpec((tr, cols), lambda i: (i, 0))
    out = jax.ShapeDtypeStruct((rows, cols), F32)
    return pl.pallas_call(
        body,
        grid=(rows // tr,),
        in_specs=[blk] * 4,
        out_specs=[blk] * 3,
        out_shape=[out] * 3,
        compiler_params=_params("parallel"),
        name=name,
    )(w, g, m, v)


def _adamw_layer(l, w, g_a, g_b, m, v, prev, name):
    depth, rows, cols = w.shape
    tr = _tile(rows, 256, 8)

    def body(w_ref, ga_ref, gb_ref, m_ref, v_ref, *rest):
        g_out, d_out, m_out, v_out = rest[-4:]
        g = ga_ref[...] + gb_ref[...]
        g_out[...] = g
        d_out[...], m_out[...], v_out[...] = _adamw_math(w_ref[...], g, m_ref[...], v_ref[...])

    stacked = pl.BlockSpec((None, tr, cols), lambda i: (l, i, 0))
    flat = pl.BlockSpec((tr, cols), lambda i: (i, 0))
    in_specs = [stacked, flat, flat, stacked, stacked]
    args = [w, g_a, g_b, m, v]
    aliases = {}
    if prev is not None:
        in_specs += [ANY] * 4
        args += list(prev)
        aliases = {5 + k: k for k in range(4)}
    out = jax.ShapeDtypeStruct((depth, rows, cols), F32)
    return pl.pallas_call(
        body,
        grid=(rows // tr,),
        in_specs=in_specs,
        out_specs=[stacked] * 4,
        out_shape=[out] * 4,
        input_output_aliases=aliases,
        compiler_params=_params("parallel"),
        name=name,
    )(*args)


def _cols_from_shards(g):
    _, r, c = g.shape
    return jnp.transpose(g, (1, 0, 2)).reshape(r, N_CHIPS * c)


def _cols_to_shards(full):
    r, c4 = full.shape
    c = c4 // N_CHIPS
    return jnp.transpose(full.reshape(r, N_CHIPS, c), (1, 0, 2))


IN_ORDER = (Q_RANK, KV_RANK, ROPE, CONV_W, CONV_W, CONV_W, MEM_W, MIX_W)


def _w_in_to_z_layout(w_in):
    edges = [0]
    for width in IN_ORDER:
        edges.append(edges[-1] + width)
    q_lat, kv_lat, k_pe, gb, gc, xin, q_mem, gate = [w_in[..., edges[i]:edges[i + 1]] for i in range(8)]
    pad = jnp.zeros(k_pe.shape[:-1] + (LANES - ROPE,), w_in.dtype)
    return jnp.concatenate([gate, q_lat, kv_lat, k_pe, pad, gb, gc, xin, q_mem], axis=-1)


def _w_in_from_z_layout(wz):
    cut = lambda c0, width: wz[..., c0:c0 + width]
    return jnp.concatenate(
        [cut(Z_QLAT, Q_RANK), cut(Z_KVLAT, KV_RANK), cut(Z_KPE, ROPE), cut(Z_GB, CONV_W), cut(Z_GC, CONV_W),
         cut(Z_XIN, CONV_W), cut(Z_QMEM, MEM_W), cut(Z_GATE, MIX_W)], axis=-1)


def _w_uq_pad(w_uq):
    r, _ = w_uq.shape
    w = jnp.pad(w_uq.reshape(r, MLA_HEADS, QK_HEAD), ((0, 0), (0, 0), (0, QPAD - QK_HEAD)))
    return w.reshape(r, MLA_HEADS * QPAD)


def _w_uq_unpad(w):
    r, _ = w.shape
    return w.reshape(r, MLA_HEADS, QPAD)[..., :QK_HEAD].reshape(r, MLA_HEADS * QK_HEAD)


def _rope_tables(positions):
    inv_freq = 1.0 / (ROPE_THETA ** (jnp.arange(0, ROPE, 2, dtype=F32) / ROPE))
    ang = positions.astype(F32)[:, None] * inv_freq
    cos, sin = jnp.cos(ang), jnp.sin(ang)
    s = positions.shape[0]
    zero = jnp.zeros((s, HALF_ROPE), F32)
    pad = jnp.zeros((s, LANES - ROPE), F32)
    kc = jnp.concatenate([cos, cos, pad], axis=-1)
    ka = jnp.concatenate([-sin, zero, pad], axis=-1)
    kb = jnp.concatenate([zero, sin, pad], axis=-1)
    qc = jnp.concatenate([jnp.ones((s, NOPE), F32), kc], axis=-1)
    qa = jnp.concatenate([jnp.zeros((s, NOPE), F32), ka], axis=-1)
    qb = jnp.concatenate([jnp.zeros((s, NOPE), F32), kb], axis=-1)
    return (qc, qa, qb), (kc, ka, kb)


def _layer_weights(gathered):
    wt_in, g_uq, w_ukv, g_conv, w_mkv, w_o = gathered
    return (wt_in, _w_uq_pad(_cols_from_shards(g_uq)), w_ukv, _cols_from_shards(g_conv), w_mkv, w_o)


def _layer_fwd(l, x, mem, wts, gains, tabs, comm):
    wt_in, w_uq, w_ukv, conv_w, w_mkv, w_o = wts
    g_pre, g_q, g_kv, g_mem, g_post = gains
    q_tab, k_tab = tabs
    tag = f"l{l}_"
    h = _rmsnorm_fwd(x, g_pre, 0, D_MODEL, tag + "pre_norm")
    z = _matmul(h, wt_in, "nt", F32, tag + "in_proj", tn_cap=1664)
    qn = _rmsnorm_fwd(z, g_q, Z_QLAT, Q_RANK, tag + "q_norm")
    kvn = _rmsnorm_fwd(z, g_kv, Z_KVLAT, KV_RANK, tag + "kv_norm")
    q_raw = _matmul(qn, w_uq, "nn", F32, tag + "uq")
    kv = _matmul(kvn, w_ukv, "nn", BF16, tag + "ukv")
    kpe = _rope(z, *k_tab, Z_KPE, LANES, 1, tag + "k_rope")
    a_out, a_lse, arrived = _attn_fwd(q_raw, kv, kpe, kv, q_tab, MLA_HEADS, QPAD, 0, 0, 2, 1, 2, QK_HEAD ** -0.5, 256,
                                      tag + "mla_fwd", comm)
    c_out = _conv_fwd(z, conv_w, tag + "conv_fwd")
    mem_n = _rmsnorm_fwd(mem, g_mem, 0, D_MODEL, tag + "mem_norm")
    mkv = _matmul(mem_n, w_mkv, "nn", BF16, tag + "mem_kv")
    m_out, m_lse, _ = _attn_fwd(z, mkv, None, mkv, None, MEM_HEADS, LANES, Z_QMEM // LANES, 0, 1, MEM_HEADS, 1,
                                MEM_HEAD ** -0.5, 1024, tag + "mem_fwd")
    cat = jnp.concatenate([a_out, c_out, m_out], axis=-1)
    y = _gate_fwd(cat, z, tag + "gate_fwd")
    o = _matmul(y, w_o, "nn", F32, tag + "out_proj")
    x_new = _post_norm_residual(x, o, g_post, tag + "post_norm")
    saved = (x, h, z, qn, kvn, q_raw, kv, kpe, a_lse, mem_n, mkv, m_lse, cat, y, o)
    return x_new, saved, arrived


def _layer_bwd(l, g, mem, saved, wts, gains, tabs_bwd, comm):
    wt_in, w_uq, w_ukv, conv_w, w_mkv, w_o = wts
    g_pre, g_q, g_kv, g_mem, g_post = gains
    q_tab, k_tab_bwd = tabs_bwd
    x, h, z, qn, kvn, q_raw, kv, kpe, a_lse, mem_n, mkv, m_lse, cat, y, o = saved
    tag = f"l{l}_"
    do, dg_post = _rmsnorm_bwd(o, g_post, g, None, 0, D_MODEL, BF16, tag + "post_norm_bwd")
    dy = _matmul(do, w_o, "nt", F32, tag + "out_proj_dx")
    dw_o = _matmul(y, do, "tn", BF16, tag + "out_proj_dw")
    dcat, dgate = _gate_bwd(dy, cat, z, tag + "gate_bwd")
    dq, dkv, dkpe_h, arrived = _attn_bwd(q_raw, kv, kpe, kv, cat, dcat, a_lse, q_tab, MLA_HEADS, QPAD, 0, 0, 2, 1, 2, 0,
                                         QK_HEAD ** -0.5, 256, tag + "mla_bwd", comm)
    dkpe = _kpe_grad(dkpe_h, *k_tab_bwd, MLA_HEADS, tag + "k_rope_bwd")
    dw_ukv = _matmul(kvn, dkv, "tn", BF16, tag + "ukv_dw")
    dkvn = _matmul(dkv, w_ukv, "nt", F32, tag + "ukv_dx")
    dkv_lat, dg_kv = _rmsnorm_bwd(z, g_kv, dkvn, None, Z_KVLAT, KV_RANK, BF16, tag + "kv_norm_bwd")
    dw_uq = _matmul(qn, dq, "tn", BF16, tag + "uq_dw")
    dqn = _matmul(dq, w_uq, "nt", F32, tag + "uq_dx")
    dq_lat, dg_q = _rmsnorm_bwd(z, g_q, dqn, None, Z_QLAT, Q_RANK, BF16, tag + "q_norm_bwd")
    dgb, dgc, dxin, dconv_w = _conv_bwd(z, conv_w, dcat, tag + "conv_bwd")
    dq_mem, dmk, dmv, _ = _attn_bwd(z, mkv, None, mkv, cat, dcat, m_lse, None, MEM_HEADS, LANES, Z_QMEM // LANES, 0, 1,
                                    MEM_HEADS, 1, (MLA_W + CONV_W) // LANES, MEM_HEAD ** -0.5, 1024, tag + "mem_bwd")
    dmkv = jnp.concatenate([dmk, dmv], axis=-1)
    dw_mkv = _matmul(mem_n, dmkv, "tn", BF16, tag + "mem_kv_dw")
    dmem_n = _matmul(dmkv, w_mkv, "nt", F32, tag + "mem_kv_dx")
    _, dg_mem = _rmsnorm_bwd(mem, g_mem, dmem_n, None, 0, D_MODEL, BF16, tag + "mem_norm_bwd")
    dz = jnp.concatenate([dgate, dq_lat, dkv_lat, dkpe, dgb, dgc, dxin, dq_mem], axis=-1)
    dwt_in = _matmul(dz, h, "tn", BF16, tag + "in_proj_dw", tm_cap=1664, tk_cap=1024)
    dh = _matmul(dz, wt_in, "nn", F32, tag + "in_proj_dx", tk_cap=1664)
    dx, dg_pre = _rmsnorm_bwd(x, g_pre, dh, g, 0, D_MODEL, F32, tag + "pre_norm_bwd")
    contrib = _scatter_plan(dwt_in, _cols_to_shards(_w_uq_unpad(dw_uq)), dw_ukv, _cols_to_shards(dconv_w), dw_mkv, dw_o)
    return dx, contrib, (dg_pre, dg_q, dg_kv, dg_mem, dg_post), arrived


GAIN_WIDTHS = (D_MODEL, Q_RANK, KV_RANK, D_MODEL, D_MODEL)


def _pack_gains(parts):
    return jnp.concatenate([p.reshape(-1) for p in parts]).reshape(-1, LANES)


def _unpack_gains(packed, depth):
    flat = packed.reshape(-1)
    out, at = [], 0
    for width in GAIN_WIDTHS:
        out.append(flat[at:at + depth * width].reshape(depth, width))
        at += depth * width
    return out


def kernel(x, mem, positions, pre_norm_g, w_in, q_norm_g, w_uq, kv_norm_g, w_ukv, conv_w, mem_norm_g, w_mk, w_mv, w_o, post_norm_g, loss_target, m_pre_norm_g, m_w_in, m_q_norm_g, m_w_uq, m_kv_norm_g, m_w_ukv, m_conv_w, m_mem_norm_g, m_w_mk, m_w_mv, m_w_o, m_post_norm_g, v_pre_norm_g, v_w_in, v_q_norm_g, v_w_uq, v_kv_norm_g, v_w_ukv, v_conv_w, v_mem_norm_g, v_w_mk, v_w_mv, v_w_o, v_post_norm_g):
    depth = w_in.shape[0]
    x0, mem0, target = x[0], mem[0], loss_target[0]
    tabs = _rope_tables(positions[0])
    tabs_bwd = (tabs[0], (tabs[1][0], -tabs[1][1], -tabs[1][2]))

    flip = lambda t: jnp.transpose(t, (0, 2, 1))
    w_in, m_w_in, v_w_in = flip(w_in), flip(m_w_in), flip(v_w_in)
    shards = [w_in.astype(BF16), w_uq.astype(BF16), w_ukv.astype(BF16), conv_w, w_mk.astype(BF16), w_mv.astype(BF16),
              w_o.astype(BF16)]
    zero_rows = jnp.zeros((LANES - ROPE, D_MODEL), BF16)

    def layer_gains(l):
        return tuple(g[l][None, :] for g in (pre_norm_g, q_norm_g, kv_norm_g, mem_norm_g, post_norm_g))

    wts, saved = [None] * depth, [None] * depth
    wts[0] = _layer_weights(_comm_call(_gather_plan(0, shards, zero_rows), "l0_weight_gather"))
    act = x0
    for l in range(depth):
        comm = _gather_plan(l + 1, shards, zero_rows) if l + 1 < depth else None
        act, saved[l], arrived = _layer_fwd(l, act, mem0, wts[l], layer_gains(l), tabs, comm)
        if comm is not None:
            wts[l + 1] = _layer_weights(arrived)
    grad, loss_part = _loss_head(act, target, "loss_head")
    loss = lax.psum(loss_part[0, 0], ("x", "y", "c"))

    names = ("w_in", "w_uq", "w_ukv", "conv_w", "w_mk", "w_mv", "w_o")
    w_shards = (w_in, w_uq, w_ukv, conv_w, w_mk, w_mv, w_o)
    m_shards = (m_w_in, m_w_uq, m_w_ukv, m_conv_w, m_w_mk, m_w_mv, m_w_o)
    v_shards = (v_w_in, v_w_uq, v_w_ukv, v_conv_w, v_w_mk, v_w_mv, v_w_o)
    stacked = [None] * len(names)

    def finish(l, received):
        partial = [_sum_slots(r, f"l{l}_grad_sum_{names[i]}") for i, r in enumerate(received)]
        other = _sibling_exchange(partial, f"l{l}_grad_sibling")
        for i, name in enumerate(names):
            stacked[i] = _adamw_layer(l, w_shards[i], partial[i], other[i], m_shards[i], v_shards[i], stacked[i],
                                      f"l{l}_adamw_{name}")

    dgs = [None] * depth
    pending = None
    for l in reversed(range(depth)):
        grad, contrib, dgs[l], arrived = _layer_bwd(l, grad, mem0, saved[l], wts[l], layer_gains(l), tabs_bwd, pending)
        if pending is not None:
            finish(l + 1, arrived)
        pending = contrib
    finish(0, _comm_call(pending, "l0_grad_exchange"))
    grad_x = grad[None]
    results = {name: tuple(stacked[i]) for i, name in enumerate(names)}
    results["w_in"] = tuple(flip(t) for t in results["w_in"])

    gain_names = ("pre_norm_g", "q_norm_g", "kv_norm_g", "mem_norm_g", "post_norm_g")
    dg_packed = _pack_gains([jnp.concatenate([dgs[l][i] for l in range(depth)], axis=0) for i in range(5)])
    dg_total = _sum_slots(_gather_all(dg_packed, "gain_gather"), "gain_sum")
    gain_outs = (dg_total,) + tuple(_adamw(
        _pack_gains((pre_norm_g, q_norm_g, kv_norm_g, mem_norm_g, post_norm_g)), dg_total,
        _pack_gains((m_pre_norm_g, m_q_norm_g, m_kv_norm_g, m_mem_norm_g, m_post_norm_g)),
        _pack_gains((v_pre_norm_g, v_q_norm_g, v_kv_norm_g, v_mem_norm_g, v_post_norm_g)), "adamw_gains"))
    gain_outs = [_unpack_gains(t, depth) for t in gain_outs]
    for i, name in enumerate(gain_names):
        results[name] = tuple(gain_outs[k][i] for k in range(4))

    order = ("pre_norm_g", "w_in", "q_norm_g", "w_uq", "kv_norm_g", "w_ukv", "conv_w", "mem_norm_g", "w_mk", "w_mv", "w_o",
             "post_norm_g")
    out = [loss, grad_x]
    for k in range(4):
        out += [results[name][k] for name in order]
    return tuple(out)
```

```python
import functools

import jax
import jax.numpy as jnp
from jax import lax
from jax.experimental import pallas as pl
from jax.experimental.pallas import tpu as pltpu

F32 = jnp.float32
BF16 = jnp.bfloat16
MESH_ID = pl.DeviceIdType.MESH

D_MODEL = 2048
EPS = 1e-6
LOG2_E = 1.4426950408889634
ROPE_THETA = 10000.0
MLA_HEADS = 8
NOPE = 128
ROPE = 64
HALF_ROPE = ROPE // 2
QK_HEAD = NOPE + ROPE
V_HEAD = 128
Q_RANK = 512
KV_RANK = 256
CONV_W = 512
MEM_HEADS = 4
MEM_HEAD = 128
MEM_W = MEM_HEADS * MEM_HEAD
MLA_W = MLA_HEADS * V_HEAD
MIX_W = MLA_W + CONV_W + MEM_W
IN_COLS = Q_RANK + KV_RANK + ROPE + 3 * CONV_W + MEM_W + MIX_W
N_CHIPS = 4
N_DEV = 8

LANES = 128
VMEM_LIMIT_BYTES = 56 * 1024 * 1024

QPAD = 2 * LANES
Z_GATE = 0
Z_QLAT = Z_GATE + MIX_W
Z_KVLAT = Z_QLAT + Q_RANK
Z_KPE = Z_KVLAT + KV_RANK
Z_GB = Z_KPE + LANES
Z_GC = Z_GB + CONV_W
Z_XIN = Z_GC + CONV_W
Z_QMEM = Z_XIN + CONV_W
Z_COLS = Z_QMEM + MEM_W

ADAM_LR = 0.001
ADAM_B1 = 0.9
ADAM_B2 = 0.999
ADAM_EPS = 1e-08
ADAM_WD = 0.01
ADAM_STEP = 10


def _tile(dim, cap, unit):
    if dim <= cap:
        return dim
    t = (cap // unit) * unit
    while t >= unit:
        if dim % t == 0:
            return t
        t -= unit
    raise ValueError(f"no tile of {dim} under {cap} in units of {unit}")


def _params(*semantics):
    return pltpu.CompilerParams(dimension_semantics=semantics, vmem_limit_bytes=VMEM_LIMIT_BYTES)


def _matmul(a, b, mode, out_dtype, name, tm_cap=512, tn_cap=1024, tk_cap=2048):
    if mode == "nn":
        (m, k), (k2, n) = a.shape, b.shape
    elif mode == "nt":
        (m, k), (n, k2) = a.shape, b.shape
    else:
        (k, m), (k2, n) = a.shape, b.shape
    assert k == k2, (a.shape, b.shape, mode)
    tm = _tile(m, tm_cap, LANES if mode == "tn" else 16)
    tn = _tile(n, tn_cap, LANES)
    tk = _tile(k, tk_cap, LANES if mode != "tn" else 16)
    nk = k // tk
    if mode == "nn":
        a_spec = pl.BlockSpec((tm, tk), lambda i, j, kk: (i, kk))
        b_spec = pl.BlockSpec((tk, tn), lambda i, j, kk: (kk, j))
        dims = (((1,), (0,)), ((), ()))
    elif mode == "nt":
        a_spec = pl.BlockSpec((tm, tk), lambda i, j, kk: (i, kk))
        b_spec = pl.BlockSpec((tn, tk), lambda i, j, kk: (j, kk))
        dims = (((1,), (1,)), ((), ()))
    else:
        a_spec = pl.BlockSpec((tk, tm), lambda i, j, kk: (kk, i))
        b_spec = pl.BlockSpec((tk, tn), lambda i, j, kk: (kk, j))
        dims = (((0,), (0,)), ((), ()))

    def body(a_ref, b_ref, o_ref, *scratch):
        part = lax.dot_general(a_ref[...].astype(BF16), b_ref[...].astype(BF16), dims, preferred_element_type=F32)
        if nk == 1:
            o_ref[...] = part.astype(o_ref.dtype)
            return
        (acc_ref,) = scratch
        kk = pl.program_id(2)

        @pl.when(kk == 0)
        def _():
            acc_ref[...] = part

        @pl.when(kk > 0)
        def _():
            acc_ref[...] += part

        @pl.when(kk == nk - 1)
        def _():
            o_ref[...] = acc_ref[...].astype(o_ref.dtype)

    return pl.pallas_call(
        body,
        grid=(m // tm, n // tn, nk),
        in_specs=[a_spec, b_spec],
        out_specs=pl.BlockSpec((tm, tn), lambda i, j, kk: (i, j)),
        out_shape=jax.ShapeDtypeStruct((m, n), out_dtype),
        scratch_shapes=[] if nk == 1 else [pltpu.VMEM((tm, tn), F32)],
        compiler_params=_params("parallel", "parallel", "arbitrary"),
        name=name,
    )(a, b)


def _rmsnorm_fwd(x, gain, col0, width, name):
    rows = x.shape[0]
    tr = _tile(rows, 512, 16)
    cb = col0 // width
    assert cb * width == col0

    def body(x_ref, g_ref, o_ref):
        xv = x_ref[...]
        r = lax.rsqrt(jnp.mean(xv * xv, axis=-1, keepdims=True) + EPS)
        o_ref[...] = (xv * r * g_ref[...]).astype(o_ref.dtype)

    return pl.pallas_call(
        body,
        grid=(rows // tr,),
        in_specs=[pl.BlockSpec((tr, width), lambda i: (i, cb)), pl.BlockSpec((1, width), lambda i: (0, 0))],
        out_specs=pl.BlockSpec((tr, width), lambda i: (i, 0)),
        out_shape=jax.ShapeDtypeStruct((rows, width), BF16),
        compiler_params=_params("parallel"),
        name=name,
    )(x, gain)


def _rmsnorm_bwd(x, gain, dy, resid, col0, width, out_dtype, name):
    rows = x.shape[0]
    tr = _tile(rows, 256, 16)
    cb = col0 // width
    assert cb * width == col0
    has_resid = resid is not None

    def body(*refs):
        if has_resid:
            x_ref, g_ref, dy_ref, res_ref, dx_ref, dg_ref = refs
        else:
            x_ref, g_ref, dy_ref, dx_ref, dg_ref = refs
        i = pl.program_id(0)
        xv = x_ref[...]
        dyv = dy_ref[...].astype(F32)
        r = lax.rsqrt(jnp.mean(xv * xv, axis=-1, keepdims=True) + EPS)
        xr = xv * r
        dyg = dyv * g_ref[...]
        c = jnp.mean(dyg * xr, axis=-1, keepdims=True)
        dx = r * (dyg - xr * c)
        if has_resid:
            dx = dx + res_ref[...]
        dx_ref[...] = dx.astype(dx_ref.dtype)
        part = jnp.sum(dyv * xr, axis=0, keepdims=True)

        @pl.when(i == 0)
        def _():
            dg_ref[...] = part

        @pl.when(i > 0)
        def _():
            dg_ref[...] += part

    row_spec = pl.BlockSpec((tr, width), lambda i: (i, 0))
    in_specs = [pl.BlockSpec((tr, width), lambda i: (i, cb)), pl.BlockSpec((1, width), lambda i: (0, 0)), row_spec]
    args = [x, gain, dy]
    if has_resid:
        in_specs.append(row_spec)
        args.append(resid)
    return pl.pallas_call(
        body,
        grid=(rows // tr,),
        in_specs=in_specs,
        out_specs=[row_spec, pl.BlockSpec((1, width), lambda i: (0, 0))],
        out_shape=[jax.ShapeDtypeStruct((rows, width), out_dtype), jax.ShapeDtypeStruct((1, width), F32)],
        compiler_params=_params("arbitrary"),
        name=name,
    )(*args)


def _post_norm_residual(x, o, gain, name):
    rows, width = x.shape
    tr = _tile(rows, 256, 8)

    def body(x_ref, o_ref, g_ref, out_ref):
        ov = o_ref[...]
        r = lax.rsqrt(jnp.mean(ov * ov, axis=-1, keepdims=True) + EPS)
        out_ref[...] = x_ref[...] + ov * r * g_ref[...]

    row_spec = pl.BlockSpec((tr, width), lambda i: (i, 0))
    return pl.pallas_call(
        body,
        grid=(rows // tr,),
        in_specs=[row_spec, row_spec, pl.BlockSpec((1, width), lambda i: (0, 0))],
        out_specs=row_spec,
        out_shape=jax.ShapeDtypeStruct((rows, width), F32),
        compiler_params=_params("parallel"),
        name=name,
    )(x, o, gain)


def _rope(x, tab_c, tab_a, tab_b, col0, width, heads, name):
    rows = x.shape[0]
    tr = _tile(rows, 512, 16)
    cb = col0 // width
    assert cb * width == col0

    def body(x_ref, c_ref, a_ref, b_ref, o_ref):
        xv = x_ref[...].astype(F32)
        up = pltpu.roll(xv, width - HALF_ROPE, 1)
        down = pltpu.roll(xv, HALF_ROPE, 1)
        o_ref[...] = (xv * c_ref[...] + up * a_ref[...] + down * b_ref[...]).astype(o_ref.dtype)

    tab_spec = pl.BlockSpec((tr, width), lambda i, h: (i, 0))
    return pl.pallas_call(
        body,
        grid=(rows // tr, heads),
        in_specs=[pl.BlockSpec((tr, width), lambda i, h: (i, cb + h)), tab_spec, tab_spec, tab_spec],
        out_specs=pl.BlockSpec((tr, width), lambda i, h: (i, h)),
        out_shape=jax.ShapeDtypeStruct((rows, heads * width), BF16),
        compiler_params=_params("parallel", "parallel"),
        name=name,
    )(x, tab_c, tab_a, tab_b)


def _kpe_grad(dkb, tab_c, tab_a, tab_b, heads, name):
    rows = dkb.shape[0]
    tr = _tile(rows, 512, 16)

    def body(d_ref, c_ref, a_ref, b_ref, o_ref):
        acc = d_ref[:, 0:LANES]
        for h in range(1, heads):
            acc = acc + d_ref[:, h * LANES:(h + 1) * LANES]
        up = pltpu.roll(acc, LANES - HALF_ROPE, 1)
        down = pltpu.roll(acc, HALF_ROPE, 1)
        o_ref[...] = (acc * c_ref[...] + up * a_ref[...] + down * b_ref[...]).astype(o_ref.dtype)

    tab_spec = pl.BlockSpec((tr, LANES), lambda i: (i, 0))
    return pl.pallas_call(
        body,
        grid=(rows // tr,),
        in_specs=[pl.BlockSpec((tr, heads * LANES), lambda i: (i, 0)), tab_spec, tab_spec, tab_spec],
        out_specs=tab_spec,
        out_shape=jax.ShapeDtypeStruct((rows, LANES), BF16),
        compiler_params=_params("parallel"),
        name=name,
    )(dkb, tab_c, tab_a, tab_b)


class _CommPlan:
    def __init__(self, ins, out_shape, build, n_copies):
        self.ins, self.out_shape, self.build, self.n_copies = list(ins), list(out_shape), build, n_copies

    def scratch(self):
        n = self.n_copies
        return [pltpu.SemaphoreType.DMA((n,)), pltpu.SemaphoreType.DMA((n,)), pltpu.SemaphoreType.DMA((n,))]


def _split_comm(refs, n_in, n_out, comm):
    if comm is None:
        return refs, None
    ci, co = len(comm.ins), len(comm.out_shape)
    ins, c_ins = refs[:n_in], refs[n_in:n_in + ci]
    outs, c_outs = refs[n_in + ci:n_in + ci + n_out], refs[n_in + ci + n_out:n_in + ci + n_out + co]
    rest = refs[n_in + ci + n_out + co:]
    scratch, sems = rest[:-3], rest[-3:]
    return tuple(ins) + tuple(outs) + tuple(scratch), functools.partial(comm.build, c_ins, c_outs, sems)


def _ride_start(copies, first):
    if copies is not None:
        pl.when(first)(copies()[0])


def _ride_wait(copies, last):
    if copies is not None:
        pl.when(last)(copies()[1])


def _rope_rows(x, c, a, b, sign):
    width = x.shape[-1]
    mixed = pltpu.roll(x, width - HALF_ROPE, 1) * a + pltpu.roll(x, HALF_ROPE, 1) * b
    return x * c + mixed if sign > 0 else x * c - mixed


def _attn_fwd(q, ka, kb, v, rope, heads, q_w, q_cb, ka_cb, ka_step, v_cb, v_step, scale, tq_cap, name, comm=None, tk_cap=512):
    s_q, s_k = q.shape[0], ka.shape[0]
    tq = _tile(s_q, tq_cap, 16)
    nq = s_q // tq
    has_kb = kb is not None
    n_in = 7 if has_kb else 3
    tk = _tile(s_k, tk_cap, LANES)

    def body(*refs):
        refs, copies = _split_comm(refs, n_in, 2, comm)
        first = jnp.logical_and(pl.program_id(0) == 0, pl.program_id(1) == 0)
        last = jnp.logical_and(pl.program_id(0) == heads - 1, pl.program_id(1) == nq - 1)
        _ride_start(copies, first)
        if has_kb:
            q_ref, ka_ref, kb_ref, v_ref, c_ref, a_ref, b_ref, o_ref, lse_ref, k_scr = refs

            @pl.when(pl.program_id(1) == 0)
            def _():
                k_scr[:, 0:LANES] = ka_ref[...].astype(BF16)
                k_scr[:, LANES:2 * LANES] = kb_ref[...].astype(BF16)

            keys = k_scr
            qv = _rope_rows(q_ref[...], c_ref[...], a_ref[...], b_ref[...], 1).astype(BF16)
        else:
            q_ref, ka_ref, v_ref, o_ref, lse_ref = refs
            keys = ka_ref
            qv = q_ref[...].astype(BF16)
        c2 = scale * LOG2_E
        m = l = o = None
        nk = s_k // tk
        scores = lambda j: lax.dot_general(qv, keys[j * tk:(j + 1) * tk, :].astype(BF16), (((1,), (1,)), ((), ())),
                                           preferred_element_type=F32)
        s_next = scores(0)
        for j in range(nk):
            sj = s_next
            if j + 1 < nk:
                s_next = scores(j + 1)
            mj = jnp.max(sj, axis=-1, keepdims=True)
            m_new = mj if m is None else jnp.maximum(m, mj)
            pj = jnp.exp2((sj - m_new) * c2)
            lj = jnp.sum(pj, axis=-1, keepdims=True)
            oj = jnp.dot(pj.astype(BF16), v_ref[j * tk:(j + 1) * tk, :].astype(BF16), preferred_element_type=F32)
            if m is None:
                l, o = lj, oj
            else:
                alpha = jnp.exp2((m - m_new) * c2)
                l, o = l * alpha + lj, o * alpha + oj
            m = m_new
        o_ref[...] = (o * (1.0 / l)).astype(o_ref.dtype)
        lse_ref[...] = jnp.broadcast_to(m * c2 + jnp.log2(l), lse_ref.shape)
        _ride_wait(copies, last)

    in_specs = [pl.BlockSpec((tq, q_w), lambda h, i: (i, q_cb + h)),
                pl.BlockSpec((s_k, LANES), lambda h, i: (0, ka_cb + ka_step * h))]
    args = [q, ka]
    if has_kb:
        in_specs.append(pl.BlockSpec((s_k, LANES), lambda h, i: (0, 0)))
        args.append(kb)
    in_specs.append(pl.BlockSpec((s_k, LANES), lambda h, i: (0, v_cb + v_step * h)))
    args.append(v)
    if has_kb:
        in_specs += [pl.BlockSpec((tq, q_w), lambda h, i: (i, 0))] * 3
        args += list(rope)
    out_spec = pl.BlockSpec((tq, LANES), lambda h, i: (i, h))
    out_specs = [out_spec, out_spec]
    out_shape = [jax.ShapeDtypeStruct((s_q, heads * LANES), BF16), jax.ShapeDtypeStruct((s_q, heads * LANES), F32)]
    scratch = [pltpu.VMEM((s_k, 2 * LANES), BF16)] if has_kb else []
    if comm is not None:
        in_specs += [ANY] * len(comm.ins)
        args += comm.ins
        out_specs += [ANY] * len(comm.out_shape)
        out_shape += comm.out_shape
        scratch += comm.scratch()
    res = pl.pallas_call(
        body,
        grid=(heads, nq),
        in_specs=in_specs,
        out_specs=out_specs,
        out_shape=out_shape,
        scratch_shapes=scratch,
        compiler_params=_params("arbitrary", "arbitrary"),
        name=name,
    )(*args)
    return res[0], res[1], list(res[2:])


def _attn_bwd(q, ka, kb, v, o, do, lse, rope, heads, q_w, q_cb, ka_cb, ka_step, v_cb, v_step, o_cb, scale, tq_cap, name,
              comm=None, tk_cap=512):
    s_q, s_k = q.shape[0], ka.shape[0]
    tq = _tile(s_q, tq_cap, 16)
    nq = s_q // tq
    has_kb = kb is not None
    n_in = 10 if has_kb else 6
    n_out = 3
    tk = _tile(s_k, tk_cap, LANES)

    def body(*refs):
        refs, copies = _split_comm(refs, n_in, n_out, comm)
        first = jnp.logical_and(pl.program_id(0) == 0, pl.program_id(1) == 0)
        last = jnp.logical_and(pl.program_id(0) == heads - 1, pl.program_id(1) == nq - 1)
        _ride_start(copies, first)
        if has_kb:
            (q_ref, ka_ref, kb_ref, v_ref, o_ref, do_ref, lse_ref, c_ref, a_ref, b_ref, dq_ref, dkv_ref, dkb_ref, k_scr, dk_acc,
             dv_acc) = refs
        else:
            q_ref, ka_ref, v_ref, o_ref, do_ref, lse_ref, dq_ref, dka_ref, dv_ref, dk_acc, dv_acc = refs
        i = pl.program_id(1)

        @pl.when(i == 0)
        def _():
            dk_acc[...] = jnp.zeros_like(dk_acc)
            dv_acc[...] = jnp.zeros_like(dv_acc)
            if has_kb:
                k_scr[:, 0:LANES] = ka_ref[...].astype(BF16)
                k_scr[:, LANES:2 * LANES] = kb_ref[...].astype(BF16)

        keys = k_scr if has_kb else ka_ref
        if has_kb:
            qv = _rope_rows(q_ref[...], c_ref[...], a_ref[...], b_ref[...], 1).astype(BF16)
        else:
            qv = q_ref[...].astype(BF16)
        dov = do_ref[...].astype(BF16)
        delta = jnp.sum(dov.astype(F32) * o_ref[...].astype(F32), axis=-1, keepdims=True)
        lse2 = lse_ref[:, 0:1]
        c2 = scale * LOG2_E
        nk = s_k // tk
        rows = lambda j: slice(j * tk, (j + 1) * tk)
        nt = (((1,), (1,)), ((), ()))
        tn = (((0,), (0,)), ((), ()))

        def scores(j):
            return (lax.dot_general(qv, keys[rows(j), :].astype(BF16), nt, preferred_element_type=F32),
                    lax.dot_general(dov, v_ref[rows(j), :].astype(BF16), nt, preferred_element_type=F32))

        nxt = scores(0)
        dq = None
        for j in range(nk):
            sj, dpj = nxt
            if j + 1 < nk:
                nxt = scores(j + 1)
            pj = jnp.exp2(sj * c2 - lse2)
            dsj = (pj * (dpj - delta)).astype(BF16)
            dqj = jnp.dot(dsj, keys[rows(j), :].astype(BF16), preferred_element_type=F32)
            dq = dqj if dq is None else dq + dqj
            dk_acc[rows(j), :] += lax.dot_general(dsj, qv, tn, preferred_element_type=F32)
            dv_acc[rows(j), :] += lax.dot_general(pj.astype(BF16), dov, tn, preferred_element_type=F32)
        dq = dq * scale
        if has_kb:
            dq = _rope_rows(dq, c_ref[...], a_ref[...], b_ref[...], -1)
        dq_ref[...] = dq.astype(dq_ref.dtype)

        @pl.when(i == nq - 1)
        def _():
            if has_kb:
                dkv_ref[:, 0:LANES] = (dk_acc[:, 0:LANES] * scale).astype(dkv_ref.dtype)
                dkv_ref[:, LANES:2 * LANES] = dv_acc[...].astype(dkv_ref.dtype)
                dkb_ref[...] = dk_acc[:, LANES:2 * LANES] * scale
            else:
                dka_ref[...] = (dk_acc[...] * scale).astype(dka_ref.dtype)
                dv_ref[...] = dv_acc[...].astype(dv_ref.dtype)

        _ride_wait(copies, last)

    key_spec = lambda cb, step: pl.BlockSpec((s_k, LANES), lambda h, i: (0, cb + step * h))
    row_spec = lambda cb: pl.BlockSpec((tq, LANES), lambda h, i: (i, cb + h))
    in_specs = [pl.BlockSpec((tq, q_w), lambda h, i: (i, q_cb + h)), key_spec(ka_cb, ka_step)]
    args = [q, ka]
    if has_kb:
        in_specs.append(pl.BlockSpec((s_k, LANES), lambda h, i: (0, 0)))
        args.append(kb)
    in_specs += [key_spec(v_cb, v_step), row_spec(o_cb), row_spec(o_cb), row_spec(0)]
    args += [v, o, do, lse]
    if has_kb:
        in_specs += [pl.BlockSpec((tq, q_w), lambda h, i: (i, 0))] * 3
        args += list(rope)
    out_specs = [pl.BlockSpec((tq, q_w), lambda h, i: (i, h))]
    out_shape = [jax.ShapeDtypeStruct((s_q, heads * q_w), BF16)]
    scratch = []
    if has_kb:
        out_specs += [pl.BlockSpec((s_k, 2 * LANES), lambda h, i: (0, h)), key_spec(0, 1)]
        out_shape += [jax.ShapeDtypeStruct((s_k, heads * 2 * LANES), BF16), jax.ShapeDtypeStruct((s_k, heads * LANES), F32)]
        scratch.append(pltpu.VMEM((s_k, 2 * LANES), BF16))
    else:
        out_specs += [key_spec(0, 1), key_spec(0, 1)]
        out_shape += [jax.ShapeDtypeStruct((s_k, heads * LANES), BF16)] * 2
    scratch += [pltpu.VMEM((s_k, q_w), F32), pltpu.VMEM((s_k, LANES), F32)]
    if comm is not None:
        in_specs += [ANY] * len(comm.ins)
        args += comm.ins
        out_specs += [ANY] * len(comm.out_shape)
        out_shape += comm.out_shape
        scratch += comm.scratch()
    res = pl.pallas_call(
        body,
        grid=(heads, nq),
        in_specs=in_specs,
        out_specs=out_specs,
        out_shape=out_shape,
        scratch_shapes=scratch,
        compiler_params=_params("arbitrary", "arbitrary"),
        name=name,
    )(*args)
    return res[0], res[1], res[2], list(res[3:])


def _shift_rows(u, rows):
    t = lax.broadcasted_iota(jnp.int32, u.shape, 0)
    prev = jnp.where(t == 0, 0.0, pltpu.roll(u, 1, 0))
    nxt = jnp.where(t == rows - 1, 0.0, pltpu.roll(u, rows - 1, 0))
    return prev, nxt


def _conv_fwd(z, conv_w, name):
    rows = z.shape[0]
    nblk = CONV_W // LANES

    def body(gb_ref, gc_ref, xin_ref, w_ref, o_ref):
        u = gc_ref[...] * xin_ref[...]
        prev, nxt = _shift_rows(u, rows)
        conv = prev * w_ref[0:1, :] + u * w_ref[1:2, :] + nxt * w_ref[2:3, :]
        o_ref[...] = (gb_ref[...] * conv).astype(o_ref.dtype)

    col = lambda c0: pl.BlockSpec((rows, LANES), lambda j: (0, c0 // LANES + j))
    return pl.pallas_call(
        body,
        grid=(nblk,),
        in_specs=[col(Z_GB), col(Z_GC), col(Z_XIN), pl.BlockSpec((3, LANES), lambda j: (0, j))],
        out_specs=col(0),
        out_shape=jax.ShapeDtypeStruct((rows, CONV_W), BF16),
        compiler_params=_params("parallel"),
        name=name,
    )(z, z, z, conv_w)


def _conv_bwd(z, conv_w, dcat, name):
    rows = z.shape[0]
    nblk = CONV_W // LANES

    def body(gb_ref, gc_ref, xin_ref, w_ref, dc_ref, dgb_ref, dgc_ref, dxin_ref, dw_ref):
        gc = gc_ref[...]
        xin = xin_ref[...]
        dc = dc_ref[...].astype(F32)
        u = gc * xin
        prev, nxt = _shift_rows(u, rows)
        w0, w1, w2 = w_ref[0:1, :], w_ref[1:2, :], w_ref[2:3, :]
        conv = prev * w0 + u * w1 + nxt * w2
        dgb_ref[...] = (dc * conv).astype(dgb_ref.dtype)
        dconv = dc * gb_ref[...]
        dw_ref[0:1, :] = jnp.sum(dconv * prev, axis=0, keepdims=True)
        dw_ref[1:2, :] = jnp.sum(dconv * u, axis=0, keepdims=True)
        dw_ref[2:3, :] = jnp.sum(dconv * nxt, axis=0, keepdims=True)
        dprev, dnxt = _shift_rows(dconv, rows)
        du = dnxt * w0 + dconv * w1 + dprev * w2
        dgc_ref[...] = (du * xin).astype(dgc_ref.dtype)
        dxin_ref[...] = (du * gc).astype(dxin_ref.dtype)

    col = lambda c0: pl.BlockSpec((rows, LANES), lambda j: (0, c0 // LANES + j))
    w_spec = pl.BlockSpec((3, LANES), lambda j: (0, j))
    piece = jax.ShapeDtypeStruct((rows, CONV_W), BF16)
    return pl.pallas_call(
        body,
        grid=(nblk,),
        in_specs=[col(Z_GB), col(Z_GC), col(Z_XIN), w_spec, col(MLA_W)],
        out_specs=[col(0), col(0), col(0), w_spec],
        out_shape=[piece, piece, piece, jax.ShapeDtypeStruct((3, CONV_W), F32)],
        compiler_params=_params("parallel"),
        name=name,
    )(z, z, z, conv_w, dcat)


def _gate_fwd(cat, z, name):
    rows = cat.shape[0]
    tr = _tile(rows, 512, 16)
    tc = 512
    g0 = Z_GATE // tc

    def body(c_ref, g_ref, y_ref):
        g = g_ref[...]
        y_ref[...] = (c_ref[...].astype(F32) * (g * jax.nn.sigmoid(g))).astype(y_ref.dtype)

    blk = pl.BlockSpec((tr, tc), lambda i, j: (i, j))
    return pl.pallas_call(
        body,
        grid=(rows // tr, MIX_W // tc),
        in_specs=[blk, pl.BlockSpec((tr, tc), lambda i, j: (i, g0 + j))],
        out_specs=blk,
        out_shape=jax.ShapeDtypeStruct((rows, MIX_W), BF16),
        compiler_params=_params("parallel", "parallel"),
        name=name,
    )(cat, z)


def _gate_bwd(dy, cat, z, name):
    rows = cat.shape[0]
    tr = _tile(rows, 512, 16)
    tc = 512
    g0 = Z_GATE // tc

    def body(dy_ref, c_ref, g_ref, dcat_ref, dgate_ref):
        g = g_ref[...]
        sg = jax.nn.sigmoid(g)
        dyv = dy_ref[...].astype(F32)
        dcat_ref[...] = (dyv * (g * sg)).astype(dcat_ref.dtype)
        dgate_ref[...] = (dyv * c_ref[...].astype(F32) * (sg * (1.0 + g * (1.0 - sg)))).astype(dgate_ref.dtype)

    blk = pl.BlockSpec((tr, tc), lambda i, j: (i, j))
    out = jax.ShapeDtypeStruct((rows, MIX_W), BF16)
    return pl.pallas_call(
        body,
        grid=(rows // tr, MIX_W // tc),
        in_specs=[blk, blk, pl.BlockSpec((tr, tc), lambda i, j: (i, g0 + j))],
        out_specs=[blk, blk],
        out_shape=[out, out],
        compiler_params=_params("parallel", "parallel"),
        name=name,
    )(dy, cat, z)


def _loss_head(y, target, name):
    rows, width = y.shape
    tr = _tile(rows, 256, 8)

    def body(y_ref, t_ref, g_ref, loss_ref):
        i = pl.program_id(0)
        d = y_ref[...] - t_ref[...]
        g_ref[...] = d / width
        part = 0.5 * jnp.sum(jnp.mean(d * d, axis=-1, keepdims=True), axis=0, keepdims=True)
        part = jnp.broadcast_to(part, loss_ref.shape)

        @pl.when(i == 0)
        def _():
            loss_ref[...] = part

        @pl.when(i > 0)
        def _():
            loss_ref[...] += part

    row_spec = pl.BlockSpec((tr, width), lambda i: (i, 0))
    return pl.pallas_call(
        body,
        grid=(rows // tr,),
        in_specs=[row_spec, row_spec],
        out_specs=[row_spec, pl.BlockSpec((1, LANES), lambda i: (0, 0))],
        out_shape=[jax.ShapeDtypeStruct((rows, width), F32), jax.ShapeDtypeStruct((1, LANES), F32)],
        compiler_params=_params("arbitrary"),
        name=name,
    )(y, target)


CHIP_FLIPS = ((1, 0), (0, 1), (1, 1))
ANY = pl.BlockSpec(memory_space=pl.ANY)


def _chip_copies(pieces, sems, n_slot):
    send_sems, recv_sems, local_sems = sems
    x, y, c = lax.axis_index("x"), lax.axis_index("y"), lax.axis_index("c")
    me = 2 * x + y

    def remote(j, k, a, src, dst):
        fx, fy = CHIP_FLIPS[k]
        return pltpu.make_async_remote_copy(
            src_ref=src, dst_ref=dst, send_sem=send_sems.at[n_slot * k + a], recv_sem=recv_sems.at[n_slot * k + a],
            device_id=((j // 2) ^ fx, (j % 2) ^ fy, c), device_id_type=MESH_ID)

    def peer(j, k):
        fx, fy = CHIP_FLIPS[k]
        return 2 * ((j // 2) ^ fx) + ((j % 2) ^ fy)

    def start_as(j):
        def run():
            for a, (src, dst) in enumerate(pieces(j, j)):
                pltpu.make_async_copy(src, dst, local_sems.at[a]).start()
            for k in range(len(CHIP_FLIPS)):
                for a, (src, dst) in enumerate(pieces(j, peer(j, k))):
                    remote(j, k, a, src, dst).start()
        return run

    def wait_as(j):
        def run():
            for a, (src, dst) in enumerate(pieces(j, j)):
                pltpu.make_async_copy(src, dst, local_sems.at[a]).wait()
            for k in range(len(CHIP_FLIPS)):
                for a, (src, dst) in enumerate(pieces(j, peer(j, k))):
                    remote(j, k, a, src, dst).wait_send()
                for a, (src, dst) in enumerate(pieces(peer(j, k), j)):
                    remote(j, k, a, src, dst).wait_recv()
        return run

    def start():
        for j in range(N_CHIPS):
            pl.when(me == j)(start_as(j))

    def wait():
        for j in range(N_CHIPS):
            pl.when(me == j)(wait_as(j))

    return start, wait


IN_PIECES = ((0, Q_RANK, Z_QLAT), (Q_RANK, KV_RANK, Z_KVLAT), (Q_RANK + KV_RANK, ROPE, Z_KPE),
             (Q_RANK + KV_RANK + ROPE, CONV_W, Z_GB), (Q_RANK + KV_RANK + ROPE + CONV_W, CONV_W, Z_GC),
             (Q_RANK + KV_RANK + ROPE + 2 * CONV_W, CONV_W, Z_XIN), (Q_RANK + KV_RANK + ROPE + 3 * CONV_W, MEM_W, Z_QMEM),
             (Q_RANK + KV_RANK + ROPE + 3 * CONV_W + MEM_W, MIX_W, Z_GATE))
IN_SHARD = IN_COLS // N_CHIPS


def _in_segments(j):
    lo, hi = j * IN_SHARD, (j + 1) * IN_SHARD
    segs = []
    for r0, width, z0 in IN_PIECES:
        a, b = max(lo, r0), min(hi, r0 + width)
        if a < b:
            segs.append((a - lo, z0 + a - r0, b - a))
    return segs


N_SLOT = 11


def _gather_plan(l, shards, zero_rows):
    s_in, s_uq, s_ukv, s_conv, s_mk, s_mv, s_o = shards
    ukv_c, mk_r, mk_c, o_r = s_ukv.shape[2], s_mk.shape[1], s_mk.shape[2], s_o.shape[1]
    stack = lambda s: jax.ShapeDtypeStruct((N_CHIPS,) + s.shape[1:], s.dtype)
    out_shape = [jax.ShapeDtypeStruct((Z_COLS, s_in.shape[2]), s_in.dtype), stack(s_uq),
                 jax.ShapeDtypeStruct((s_ukv.shape[1], N_CHIPS * ukv_c), s_ukv.dtype), stack(s_conv),
                 jax.ShapeDtypeStruct((N_CHIPS * mk_r, 2 * mk_c), s_mk.dtype),
                 jax.ShapeDtypeStruct((N_CHIPS * o_r, s_o.shape[2]), s_o.dtype)]

    def build(ins, outs, sems):
        r_in, r_uq, r_ukv, r_conv, r_mk, r_mv, r_o, r_zero = ins
        f_in, g_uq, f_ukv, g_conv, f_mkv, f_o = outs

        def pieces(j, t):
            out = [(r_in.at[l, pl.ds(so, n), :], f_in.at[pl.ds(zo, n), :]) for so, zo, n in _in_segments(j)]
            out += [(r_uq.at[l], g_uq.at[j]), (r_ukv.at[l], f_ukv.at[:, pl.ds(j * ukv_c, ukv_c)]), (r_conv.at[l], g_conv.at[j]),
                    (r_mk.at[l], f_mkv.at[pl.ds(j * mk_r, mk_r), pl.ds(0, mk_c)]),
                    (r_mv.at[l], f_mkv.at[pl.ds(j * mk_r, mk_r), pl.ds(mk_c, mk_c)]),
                    (r_o.at[l], f_o.at[pl.ds(j * o_r, o_r), :])]
            if j == t:
                out.append((r_zero, f_in.at[pl.ds(Z_KPE + ROPE, LANES - ROPE), :]))
            return out

        return _chip_copies(pieces, sems, N_SLOT)

    return _CommPlan(list(shards) + [zero_rows], out_shape, build, len(CHIP_FLIPS) * N_SLOT)


def _scatter_plan(dwt_in, c_uq, dw_ukv, c_conv, dw_mkv, dw_o):
    ukv_c, mk_r, mk_c, o_r = dw_ukv.shape[1] // N_CHIPS, dw_mkv.shape[0] // N_CHIPS, dw_mkv.shape[1] // 2, dw_o.shape[0] // N_CHIPS
    out_shape = [jax.ShapeDtypeStruct((N_CHIPS, IN_SHARD, dwt_in.shape[1]), dwt_in.dtype),
                 jax.ShapeDtypeStruct(c_uq.shape, c_uq.dtype),
                 jax.ShapeDtypeStruct((N_CHIPS, dw_ukv.shape[0], ukv_c), dw_ukv.dtype),
                 jax.ShapeDtypeStruct(c_conv.shape, c_conv.dtype),
                 jax.ShapeDtypeStruct((N_CHIPS, mk_r, mk_c), dw_mkv.dtype), jax.ShapeDtypeStruct((N_CHIPS, mk_r, mk_c), dw_mkv.dtype),
                 jax.ShapeDtypeStruct((N_CHIPS, o_r, dw_o.shape[1]), dw_o.dtype)]

    def build(ins, outs, sems):
        r_in, r_uq, r_ukv, r_conv, r_mkv, r_o = ins
        o_in, o_uq, o_ukv, o_conv, o_mk, o_mv, o_o = outs

        def pieces(j, t):
            out = [(r_in.at[pl.ds(zo, n), :], o_in.at[j, pl.ds(so, n), :]) for so, zo, n in _in_segments(t)]
            out += [(r_uq.at[t], o_uq.at[j]), (r_ukv.at[:, pl.ds(t * ukv_c, ukv_c)], o_ukv.at[j]), (r_conv.at[t], o_conv.at[j]),
                    (r_mkv.at[pl.ds(t * mk_r, mk_r), pl.ds(0, mk_c)], o_mk.at[j]),
                    (r_mkv.at[pl.ds(t * mk_r, mk_r), pl.ds(mk_c, mk_c)], o_mv.at[j]),
                    (r_o.at[pl.ds(t * o_r, o_r), :], o_o.at[j])]
            return out

        return _chip_copies(pieces, sems, N_SLOT)

    return _CommPlan([dwt_in, c_uq, dw_ukv, c_conv, dw_mkv, dw_o], out_shape, build, len(CHIP_FLIPS) * N_SLOT)


def _comm_call(plan, name):
    n_in, n_out = len(plan.ins), len(plan.out_shape)

    def body(*refs):
        start, wait = plan.build(refs[:n_in], refs[n_in:n_in + n_out], refs[n_in + n_out:])
        start()
        wait()

    return list(pl.pallas_call(
        body,
        in_specs=[ANY] * n_in,
        out_specs=[ANY] * n_out,
        out_shape=plan.out_shape,
        scratch_shapes=plan.scratch(),
        name=name,
    )(*plan.ins))


def _sibling_exchange(arrays, name):
    n = len(arrays)

    def body(*refs):
        ins, outs = refs[:n], refs[n:2 * n]
        send_sems, recv_sems = refs[2 * n:]
        sibling = (lax.axis_index("x"), lax.axis_index("y"), 1 - lax.axis_index("c"))
        copies = []
        for a in range(n):
            cp = pltpu.make_async_remote_copy(
                src_ref=ins[a], dst_ref=outs[a], send_sem=send_sems.at[a], recv_sem=recv_sems.at[a],
                device_id=sibling, device_id_type=MESH_ID)
            cp.start()
            copies.append(cp)
        for cp in copies:
            cp.wait()

    return pl.pallas_call(
        body,
        in_specs=[ANY] * n,
        out_specs=[ANY] * n,
        out_shape=[jax.ShapeDtypeStruct(v.shape, v.dtype) for v in arrays],
        scratch_shapes=[pltpu.SemaphoreType.DMA((n,)), pltpu.SemaphoreType.DMA((n,))],
        name=name,
    )(*arrays)


DEVICE_FLIPS = tuple((fx, fy, fc) for fx in (0, 1) for fy in (0, 1) for fc in (0, 1))[1:]


def _gather_all(v, name):
    def body(v_ref, out_ref, send_sems, recv_sems, local_sem):
        x, y, c = lax.axis_index("x"), lax.axis_index("y"), lax.axis_index("c")
        me = 4 * x + 2 * y + c
        local = pltpu.make_async_copy(v_ref, out_ref.at[me], local_sem)
        local.start()
        copies = [local]
        for k, (fx, fy, fc) in enumerate(DEVICE_FLIPS):
            cp = pltpu.make_async_remote_copy(
                src_ref=v_ref, dst_ref=out_ref.at[me], send_sem=send_sems.at[k], recv_sem=recv_sems.at[k],
                device_id=((x + fx) % 2, (y + fy) % 2, (c + fc) % 2), device_id_type=MESH_ID)
            cp.start()
            copies.append(cp)
        for cp in copies:
            cp.wait()

    return pl.pallas_call(
        body,
        in_specs=[ANY],
        out_specs=ANY,
        out_shape=jax.ShapeDtypeStruct((N_DEV,) + v.shape, v.dtype),
        scratch_shapes=[pltpu.SemaphoreType.DMA((N_DEV - 1,)), pltpu.SemaphoreType.DMA((N_DEV - 1,)), pltpu.SemaphoreType.DMA],
        name=name,
    )(v)


def _sum_slots(parts, name):
    n, rows, cols = parts.shape
    tr = _tile(rows, 256, 16)

    def body(p_ref, o_ref):
        acc = p_ref[0].astype(F32)
        for k in range(1, n):
            acc = acc + p_ref[k].astype(F32)
        o_ref[...] = acc

    return pl.pallas_call(
        body,
        grid=(rows // tr,),
        in_specs=[pl.BlockSpec((n, tr, cols), lambda i: (0, i, 0))],
        out_specs=pl.BlockSpec((tr, cols), lambda i: (i, 0)),
        out_shape=jax.ShapeDtypeStruct((rows, cols), F32),
        compiler_params=_params("parallel"),
        name=name,
    )(parts)


def _adamw_math(w, g, m, v):
    m_new = ADAM_B1 * m + (1.0 - ADAM_B1) * g
    v_new = ADAM_B2 * v + (1.0 - ADAM_B2) * jnp.square(g)
    m_hat = m_new / (1.0 - ADAM_B1 ** ADAM_STEP)
    v_hat = v_new / (1.0 - ADAM_B2 ** ADAM_STEP)
    return -ADAM_LR * (m_hat / (jnp.sqrt(v_hat) + ADAM_EPS) + ADAM_WD * w), m_new, v_new


def _adamw(w, g, m, v, name):
    rows, cols = w.shape
    tr = _tile(rows, 256, 8)

    def body(w_ref, g_ref, m_ref, v_ref, d_out, m_out, v_out):
        d_out[...], m_out[...], v_out[...] = _adamw_math(w_ref[...], g_ref[...], m_ref[...], v_ref[...])

    blk = pl.BlockSpec((tr, cols), lambda i: (i, 0))
    out = jax.ShapeDtypeStruct((rows, cols), F32)
    return pl.pallas_call(
        body,
        grid=(rows // tr,),
        in_specs=[blk] * 4,
        out_specs=[blk] * 3,
        out_shape=[out] * 3,
        compiler_params=_params("parallel"),
        name=name,
    )(w, g, m, v)


def _adamw_layer(l, w, g_a, g_b, m, v, prev, name):
    depth, rows, cols = w.shape
    tr = _tile(rows, 256, 8)

    def body(w_ref, ga_ref, gb_ref, m_ref, v_ref, *rest):
        g_out, d_out, m_out, v_out = rest[-4:]
        g = ga_ref[...] + gb_ref[...]
        g_out[...] = g
        d_out[...], m_out[...], v_out[...] = _adamw_math(w_ref[...], g, m_ref[...], v_ref[...])

    stacked = pl.BlockSpec((None, tr, cols), lambda i: (l, i, 0))
    flat = pl.BlockSpec((tr, cols), lambda i: (i, 0))
    in_specs = [stacked, flat, flat, stacked, stacked]
    args = [w, g_a, g_b, m, v]
    aliases = {}
    if prev is not None:
        in_specs += [ANY] * 4
        args += list(prev)
        aliases = {5 + k: k for k in range(4)}
    out = jax.ShapeDtypeStruct((depth, rows, cols), F32)
    return pl.pallas_call(
        body,
        grid=(rows // tr,),
        in_specs=in_specs,
        out_specs=[stacked] * 4,
        out_shape=[out] * 4,
        input_output_aliases=aliases,
        compiler_params=_params("parallel"),
        name=name,
    )(*args)


def _cols_from_shards(g):
    _, r, c = g.shape
    return jnp.transpose(g, (1, 0, 2)).reshape(r, N_CHIPS * c)


def _cols_to_shards(full):
    r, c4 = full.shape
    c = c4 // N_CHIPS
    return jnp.transpose(full.reshape(r, N_CHIPS, c), (1, 0, 2))


IN_ORDER = (Q_RANK, KV_RANK, ROPE, CONV_W, CONV_W, CONV_W, MEM_W, MIX_W)


def _w_in_to_z_layout(w_in):
    edges = [0]
    for width in IN_ORDER:
        edges.append(edges[-1] + width)
    q_lat, kv_lat, k_pe, gb, gc, xin, q_mem, gate = [w_in[..., edges[i]:edges[i + 1]] for i in range(8)]
    pad = jnp.zeros(k_pe.shape[:-1] + (LANES - ROPE,), w_in.dtype)
    return jnp.concatenate([gate, q_lat, kv_lat, k_pe, pad, gb, gc, xin, q_mem], axis=-1)


def _w_in_from_z_layout(wz):
    cut = lambda c0, width: wz[..., c0:c0 + width]
    return jnp.concatenate(
        [cut(Z_QLAT, Q_RANK), cut(Z_KVLAT, KV_RANK), cut(Z_KPE, ROPE), cut(Z_GB, CONV_W), cut(Z_GC, CONV_W),
         cut(Z_XIN, CONV_W), cut(Z_QMEM, MEM_W), cut(Z_GATE, MIX_W)], axis=-1)


def _w_uq_pad(w_uq):
    r, _ = w_uq.shape
    w = jnp.pad(w_uq.reshape(r, MLA_HEADS, QK_HEAD), ((0, 0), (0, 0), (0, QPAD - QK_HEAD)))
    return w.reshape(r, MLA_HEADS * QPAD)


def _w_uq_unpad(w):
    r, _ = w.shape
    return w.reshape(r, MLA_HEADS, QPAD)[..., :QK_HEAD].reshape(r, MLA_HEADS * QK_HEAD)


def _rope_tables(positions):
    inv_freq = 1.0 / (ROPE_THETA ** (jnp.arange(0, ROPE, 2, dtype=F32) / ROPE))
    ang = positions.astype(F32)[:, None] * inv_freq
    cos, sin = jnp.cos(ang), jnp.sin(ang)
    s = positions.shape[0]
    zero = jnp.zeros((s, HALF_ROPE), F32)
    pad = jnp.zeros((s, LANES - ROPE), F32)
    kc = jnp.concatenate([cos, cos, pad], axis=-1)
    ka = jnp.concatenate([-sin, zero, pad], axis=-1)
    kb = jnp.concatenate([zero, sin, pad], axis=-1)
    qc = jnp.concatenate([jnp.ones((s, NOPE), F32), kc], axis=-1)
    qa = jnp.concatenate([jnp.zeros((s, NOPE), F32), ka], axis=-1)
    qb = jnp.concatenate([jnp.zeros((s, NOPE), F32), kb], axis=-1)
    return (qc, qa, qb), (kc, ka, kb)


def _layer_weights(gathered):
    wt_in, g_uq, w_ukv, g_conv, w_mkv, w_o = gathered
    return (wt_in, _w_uq_pad(_cols_from_shards(g_uq)), w_ukv, _cols_from_shards(g_conv), w_mkv, w_o)


def _layer_fwd(l, x, mem, wts, gains, tabs, comm):
    wt_in, w_uq, w_ukv, conv_w, w_mkv, w_o = wts
    g_pre, g_q, g_kv, g_mem, g_post = gains
    q_tab, k_tab = tabs
    tag = f"l{l}_"
    h = _rmsnorm_fwd(x, g_pre, 0, D_MODEL, tag + "pre_norm")
    z = _matmul(h, wt_in, "nt", F32, tag + "in_proj", tn_cap=1664)
    qn = _rmsnorm_fwd(z, g_q, Z_QLAT, Q_RANK, tag + "q_norm")
    kvn = _rmsnorm_fwd(z, g_kv, Z_KVLAT, KV_RANK, tag + "kv_norm")
    q_raw = _matmul(qn, w_uq, "nn", F32, tag + "uq")
    kv = _matmul(kvn, w_ukv, "nn", BF16, tag + "ukv")
    kpe = _rope(z, *k_tab, Z_KPE, LANES, 1, tag + "k_rope")
    a_out, a_lse, arrived = _attn_fwd(q_raw, kv, kpe, kv, q_tab, MLA_HEADS, QPAD, 0, 0, 2, 1, 2, QK_HEAD ** -0.5, 512,
                                      tag + "mla_fwd", comm)
    c_out = _conv_fwd(z, conv_w, tag + "conv_fwd")
    mem_n = _rmsnorm_fwd(mem, g_mem, 0, D_MODEL, tag + "mem_norm")
    mkv = _matmul(mem_n, w_mkv, "nn", BF16, tag + "mem_kv")
    m_out, m_lse, _ = _attn_fwd(z, mkv, None, mkv, None, MEM_HEADS, LANES, Z_QMEM // LANES, 0, 1, MEM_HEADS, 1,
                                MEM_HEAD ** -0.5, 1024, tag + "mem_fwd")
    cat = jnp.concatenate([a_out, c_out, m_out], axis=-1)
    y = _gate_fwd(cat, z, tag + "gate_fwd")
    o = _matmul(y, w_o, "nn", F32, tag + "out_proj")
    x_new = _post_norm_residual(x, o, g_post, tag + "post_norm")
    saved = (x, h, z, qn, kvn, q_raw, kv, kpe, a_lse, mem_n, mkv, m_lse, cat, y, o)
    return x_new, saved, arrived


def _layer_bwd(l, g, mem, saved, wts, gains, tabs_bwd, comm):
    wt_in, w_uq, w_ukv, conv_w, w_mkv, w_o = wts
    g_pre, g_q, g_kv, g_mem, g_post = gains
    q_tab, k_tab_bwd = tabs_bwd
    x, h, z, qn, kvn, q_raw, kv, kpe, a_lse, mem_n, mkv, m_lse, cat, y, o = saved
    tag = f"l{l}_"
    do, dg_post = _rmsnorm_bwd(o, g_post, g, None, 0, D_MODEL, BF16, tag + "post_norm_bwd")
    dy = _matmul(do, w_o, "nt", F32, tag + "out_proj_dx")
    dw_o = _matmul(y, do, "tn", BF16, tag + "out_proj_dw")
    dcat, dgate = _gate_bwd(dy, cat, z, tag + "gate_bwd")
    dq, dkv, dkpe_h, arrived = _attn_bwd(q_raw, kv, kpe, kv, cat, dcat, a_lse, q_tab, MLA_HEADS, QPAD, 0, 0, 2, 1, 2, 0,
                                         QK_HEAD ** -0.5, 512, tag + "mla_bwd", comm)
    dkpe = _kpe_grad(dkpe_h, *k_tab_bwd, MLA_HEADS, tag + "k_rope_bwd")
    dw_ukv = _matmul(kvn, dkv, "tn", BF16, tag + "ukv_dw")
    dkvn = _matmul(dkv, w_ukv, "nt", F32, tag + "ukv_dx")
    dkv_lat, dg_kv = _rmsnorm_bwd(z, g_kv, dkvn, None, Z_KVLAT, KV_RANK, BF16, tag + "kv_norm_bwd")
    dw_uq = _matmul(qn, dq, "tn", BF16, tag + "uq_dw")
    dqn = _matmul(dq, w_uq, "nt", F32, tag + "uq_dx")
    dq_lat, dg_q = _rmsnorm_bwd(z, g_q, dqn, None, Z_QLAT, Q_RANK, BF16, tag + "q_norm_bwd")
    dgb, dgc, dxin, dconv_w = _conv_bwd(z, conv_w, dcat, tag + "conv_bwd")
    dq_mem, dmk, dmv, _ = _attn_bwd(z, mkv, None, mkv, cat, dcat, m_lse, None, MEM_HEADS, LANES, Z_QMEM // LANES, 0, 1,
                                    MEM_HEADS, 1, (MLA_W + CONV_W) // LANES, MEM_HEAD ** -0.5, 1024, tag + "mem_bwd")
    dmkv = jnp.concatenate([dmk, dmv], axis=-1)
    dw_mkv = _matmul(mem_n, dmkv, "tn", BF16, tag + "mem_kv_dw")
    dmem_n = _matmul(dmkv, w_mkv, "nt", F32, tag + "mem_kv_dx")
    _, dg_mem = _rmsnorm_bwd(mem, g_mem, dmem_n, None, 0, D_MODEL, BF16, tag + "mem_norm_bwd")
    dz = jnp.concatenate([dgate, dq_lat, dkv_lat, dkpe, dgb, dgc, dxin, dq_mem], axis=-1)
    dwt_in = _matmul(dz, h, "tn", BF16, tag + "in_proj_dw", tm_cap=1664, tk_cap=1024)
    dh = _matmul(dz, wt_in, "nn", F32, tag + "in_proj_dx", tk_cap=1664)
    dx, dg_pre = _rmsnorm_bwd(x, g_pre, dh, g, 0, D_MODEL, F32, tag + "pre_norm_bwd")
    contrib = _scatter_plan(dwt_in, _cols_to_shards(_w_uq_unpad(dw_uq)), dw_ukv, _cols_to_shards(dconv_w), dw_mkv, dw_o)
    return dx, contrib, (dg_pre, dg_q, dg_kv, dg_mem, dg_post), arrived


GAIN_WIDTHS = (D_MODEL, Q_RANK, KV_RANK, D_MODEL, D_MODEL)


def _pack_gains(parts):
    return jnp.concatenate([p.reshape(-1) for p in parts]).reshape(-1, LANES)


def _unpack_gains(packed, depth):
    flat = packed.reshape(-1)
    out, at = [], 0
    for width in GAIN_WIDTHS:
        out.append(flat[at:at + depth * width].reshape(depth, width))
        at += depth * width
    return out


def kernel(x, mem, positions, pre_norm_g, w_in, q_norm_g, w_uq, kv_norm_g, w_ukv, conv_w, mem_norm_g, w_mk, w_mv, w_o, post_norm_g, loss_target, m_pre_norm_g, m_w_in, m_q_norm_g, m_w_uq, m_kv_norm_g, m_w_ukv, m_conv_w, m_mem_norm_g, m_w_mk, m_w_mv, m_w_o, m_post_norm_g, v_pre_norm_g, v_w_in, v_q_norm_g, v_w_uq, v_kv_norm_g, v_w_ukv, v_conv_w, v_mem_norm_g, v_w_mk, v_w_mv, v_w_o, v_post_norm_g):
    depth = w_in.shape[0]
    x0, mem0, target = x[0], mem[0], loss_target[0]
    tabs = _rope_tables(positions[0])
    tabs_bwd = (tabs[0], (tabs[1][0], -tabs[1][1], -tabs[1][2]))

    flip = lambda t: jnp.transpose(t, (0, 2, 1))
    w_in, m_w_in, v_w_in = flip(w_in), flip(m_w_in), flip(v_w_in)
    shards = [w_in.astype(BF16), w_uq.astype(BF16), w_ukv.astype(BF16), conv_w, w_mk.astype(BF16), w_mv.astype(BF16),
              w_o.astype(BF16)]
    zero_rows = jnp.zeros((LANES - ROPE, D_MODEL), BF16)

    def layer_gains(l):
        return tuple(g[l][None, :] for g in (pre_norm_g, q_norm_g, kv_norm_g, mem_norm_g, post_norm_g))

    wts, saved = [None] * depth, [None] * depth
    wts[0] = _layer_weights(_comm_call(_gather_plan(0, shards, zero_rows), "l0_weight_gather"))
    act = x0
    for l in range(depth):
        comm = _gather_plan(l + 1, shards, zero_rows) if l + 1 < depth else None
        act, saved[l], arrived = _layer_fwd(l, act, mem0, wts[l], layer_gains(l), tabs, comm)
        if comm is not None:
            wts[l + 1] = _layer_weights(arrived)
    grad, loss_part = _loss_head(act, target, "loss_head")
    loss = lax.psum(loss_part[0, 0], ("x", "y", "c"))

    names = ("w_in", "w_uq", "w_ukv", "conv_w", "w_mk", "w_mv", "w_o")
    w_shards = (w_in, w_uq, w_ukv, conv_w, w_mk, w_mv, w_o)
    m_shards = (m_w_in, m_w_uq, m_w_ukv, m_conv_w, m_w_mk, m_w_mv, m_w_o)
    v_shards = (v_w_in, v_w_uq, v_w_ukv, v_conv_w, v_w_mk, v_w_mv, v_w_o)
    stacked = [None] * len(names)

    def finish(l, received):
        partial = [_sum_slots(r, f"l{l}_grad_sum_{names[i]}") for i, r in enumerate(received)]
        other = _sibling_exchange(partial, f"l{l}_grad_sibling")
        for i, name in enumerate(names):
            stacked[i] = _adamw_layer(l, w_shards[i], partial[i], other[i], m_shards[i], v_shards[i], stacked[i],
                                      f"l{l}_adamw_{name}")

    dgs = [None] * depth
    pending = None
    for l in reversed(range(depth)):
        grad, contrib, dgs[l], arrived = _layer_bwd(l, grad, mem0, saved[l], wts[l], layer_gains(l), tabs_bwd, pending)
        if pending is not None:
            finish(l + 1, arrived)
        pending = contrib
    finish(0, _comm_call(pending, "l0_grad_exchange"))
    grad_x = grad[None]
    results = {name: tuple(stacked[i]) for i, name in enumerate(names)}
    results["w_in"] = tuple(flip(t) for t in results["w_in"])

    gain_names = ("pre_norm_g", "q_norm_g", "kv_norm_g", "mem_norm_g", "post_norm_g")
    dg_packed = _pack_gains([jnp.concatenate([dgs[l][i] for l in range(depth)], axis=0) for i in range(5)])
    dg_total = _sum_slots(_gather_all(dg_packed, "gain_gather"), "gain_sum")
    gain_outs = (dg_total,) + tuple(_adamw(
        _pack_gains((pre_norm_g, q_norm_g, kv_norm_g, mem_norm_g, post_norm_g)), dg_total,
        _pack_gains((m_pre_norm_g, m_q_norm_g, m_kv_norm_g, m_mem_norm_g, m_post_norm_g)),
        _pack_gains((v_pre_norm_g, v_q_norm_g, v_kv_norm_g, v_mem_norm_g, v_post_norm_g)), "adamw_gains"))
    gain_outs = [_unpack_gains(t, depth) for t in gain_outs]
    for i, name in enumerate(gain_names):
        results[name] = tuple(gain_outs[k][i] for k in range(4))

    order = ("pre_norm_g", "w_in", "q_norm_g", "w_uq", "kv_norm_g", "w_ukv", "conv_w", "mem_norm_g", "w_mk", "w_mv", "w_o",
             "post_norm_g")
    out = [loss, grad_x]
    for k in range(4):
        out += [results[name][k] for name in order]
    return tuple(out)
```

```python
import functools

import jax
import jax.numpy as jnp
from jax import lax
from jax.experimental import pallas as pl
from jax.experimental.pallas import tpu as pltpu

F32 = jnp.float32
BF16 = jnp.bfloat16
MESH_ID = pl.DeviceIdType.MESH

D_MODEL = 2048
EPS = 1e-6
LOG2_E = 1.4426950408889634
ROPE_THETA = 10000.0
MLA_HEADS = 8
NOPE = 128
ROPE = 64
HALF_ROPE = ROPE // 2
QK_HEAD = NOPE + ROPE
V_HEAD = 128
Q_RANK = 512
KV_RANK = 256
CONV_W = 512
MEM_HEADS = 4
MEM_HEAD = 128
MEM_W = MEM_HEADS * MEM_HEAD
MLA_W = MLA_HEADS * V_HEAD
MIX_W = MLA_W + CONV_W + MEM_W
IN_COLS = Q_RANK + KV_RANK + ROPE + 3 * CONV_W + MEM_W + MIX_W
N_CHIPS = 4
N_DEV = 8

LANES = 128
VMEM_LIMIT_BYTES = 56 * 1024 * 1024

QPAD = 2 * LANES
Z_GATE = 0
Z_QLAT = Z_GATE + MIX_W
Z_KVLAT = Z_QLAT + Q_RANK
Z_KPE = Z_KVLAT + KV_RANK
Z_GB = Z_KPE + LANES
Z_GC = Z_GB + CONV_W
Z_XIN = Z_GC + CONV_W
Z_QMEM = Z_XIN + CONV_W
Z_COLS = Z_QMEM + MEM_W

ADAM_LR = 0.001
ADAM_B1 = 0.9
ADAM_B2 = 0.999
ADAM_EPS = 1e-08
ADAM_WD = 0.01
ADAM_STEP = 10


def _tile(dim, cap, unit):
    if dim <= cap:
        return dim
    t = (cap // unit) * unit
    while t >= unit:
        if dim % t == 0:
            return t
        t -= unit
    raise ValueError(f"no tile of {dim} under {cap} in units of {unit}")


def _params(*semantics):
    return pltpu.CompilerParams(dimension_semantics=semantics, vmem_limit_bytes=VMEM_LIMIT_BYTES)


def _matmul(a, b, mode, out_dtype, name, tm_cap=512, tn_cap=1024, tk_cap=2048):
    if mode == "nn":
        (m, k), (k2, n) = a.shape, b.shape
    elif mode == "nt":
        (m, k), (n, k2) = a.shape, b.shape
    else:
        (k, m), (k2, n) = a.shape, b.shape
    assert k == k2, (a.shape, b.shape, mode)
    tm = _tile(m, tm_cap, LANES if mode == "tn" else 16)
    tn = _tile(n, tn_cap, LANES)
    tk = _tile(k, tk_cap, LANES if mode != "tn" else 16)
    nk = k // tk
    if mode == "nn":
        a_spec = pl.BlockSpec((tm, tk), lambda i, j, kk: (i, kk))
        b_spec = pl.BlockSpec((tk, tn), lambda i, j, kk: (kk, j))
        dims = (((1,), (0,)), ((), ()))
    elif mode == "nt":
        a_spec = pl.BlockSpec((tm, tk), lambda i, j, kk: (i, kk))
        b_spec = pl.BlockSpec((tn, tk), lambda i, j, kk: (j, kk))
        dims = (((1,), (1,)), ((), ()))
    else:
        a_spec = pl.BlockSpec((tk, tm), lambda i, j, kk: (kk, i))
        b_spec = pl.BlockSpec((tk, tn), lambda i, j, kk: (kk, j))
        dims = (((0,), (0,)), ((), ()))

    def body(a_ref, b_ref, o_ref, *scratch):
        part = lax.dot_general(a_ref[...].astype(BF16), b_ref[...].astype(BF16), dims, preferred_element_type=F32)
        if nk == 1:
            o_ref[...] = part.astype(o_ref.dtype)
            return
        (acc_ref,) = scratch
        kk = pl.program_id(2)

        @pl.when(kk == 0)
        def _():
            acc_ref[...] = part

        @pl.when(kk > 0)
        def _():
            acc_ref[...] += part

        @pl.when(kk == nk - 1)
        def _():
            o_ref[...] = acc_ref[...].astype(o_ref.dtype)

    return pl.pallas_call(
        body,
        grid=(m // tm, n // tn, nk),
        in_specs=[a_spec, b_spec],
        out_specs=pl.BlockSpec((tm, tn), lambda i, j, kk: (i, j)),
        out_shape=jax.ShapeDtypeStruct((m, n), out_dtype),
        scratch_shapes=[] if nk == 1 else [pltpu.VMEM((tm, tn), F32)],
        compiler_params=_params("parallel", "parallel", "arbitrary"),
        name=name,
    )(a, b)


def _rmsnorm_fwd(x, gain, col0, width, name):
    rows = x.shape[0]
    tr = _tile(rows, 512, 16)
    cb = col0 // width
    assert cb * width == col0

    def body(x_ref, g_ref, o_ref):
        xv = x_ref[...]
        r = lax.rsqrt(jnp.mean(xv * xv, axis=-1, keepdims=True) + EPS)
        o_ref[...] = (xv * r * g_ref[...]).astype(o_ref.dtype)

    return pl.pallas_call(
        body,
        grid=(rows // tr,),
        in_specs=[pl.BlockSpec((tr, width), lambda i: (i, cb)), pl.BlockSpec((1, width), lambda i: (0, 0))],
        out_specs=pl.BlockSpec((tr, width), lambda i: (i, 0)),
        out_shape=jax.ShapeDtypeStruct((rows, width), BF16),
        compiler_params=_params("parallel"),
        name=name,
    )(x, gain)


def _rmsnorm_bwd(x, gain, dy, resid, col0, width, out_dtype, name):
    rows = x.shape[0]
    tr = _tile(rows, 256, 16)
    cb = col0 // width
    assert cb * width == col0
    has_resid = resid is not None

    def body(*refs):
        if has_resid:
            x_ref, g_ref, dy_ref, res_ref, dx_ref, dg_ref = refs
        else:
            x_ref, g_ref, dy_ref, dx_ref, dg_ref = refs
        i = pl.program_id(0)
        xv = x_ref[...]
        dyv = dy_ref[...].astype(F32)
        r = lax.rsqrt(jnp.mean(xv * xv, axis=-1, keepdims=True) + EPS)
        xr = xv * r
        dyg = dyv * g_ref[...]
        c = jnp.mean(dyg * xr, axis=-1, keepdims=True)
        dx = r * (dyg - xr * c)
        if has_resid:
            dx = dx + res_ref[...]
        dx_ref[...] = dx.astype(dx_ref.dtype)
        part = jnp.sum(dyv * xr, axis=0, keepdims=True)

        @pl.when(i == 0)
        def _():
            dg_ref[...] = part

        @pl.when(i > 0)
        def _():
            dg_ref[...] += part

    row_spec = pl.BlockSpec((tr, width), lambda i: (i, 0))
    in_specs = [pl.BlockSpec((tr, width), lambda i: (i, cb)), pl.BlockSpec((1, width), lambda i: (0, 0)), row_spec]
    args = [x, gain, dy]
    if has_resid:
        in_specs.append(row_spec)
        args.append(resid)
    return pl.pallas_call(
        body,
        grid=(rows // tr,),
        in_specs=in_specs,
        out_specs=[row_spec, pl.BlockSpec((1, width), lambda i: (0, 0))],
        out_shape=[jax.ShapeDtypeStruct((rows, width), out_dtype), jax.ShapeDtypeStruct((1, width), F32)],
        compiler_params=_params("arbitrary"),
        name=name,
    )(*args)


def _post_norm_residual(x, o, gain, name):
    rows, width = x.shape
    tr = _tile(rows, 256, 8)

    def body(x_ref, o_ref, g_ref, out_ref):
        ov = o_ref[...]
        r = lax.rsqrt(jnp.mean(ov * ov, axis=-1, keepdims=True) + EPS)
        out_ref[...] = x_ref[...] + ov * r * g_ref[...]

    row_spec = pl.BlockSpec((tr, width), lambda i: (i, 0))
    return pl.pallas_call(
        body,
        grid=(rows // tr,),
        in_specs=[row_spec, row_spec, pl.BlockSpec((1, width), lambda i: (0, 0))],
        out_specs=row_spec,
        out_shape=jax.ShapeDtypeStruct((rows, width), F32),
        compiler_params=_params("parallel"),
        name=name,
    )(x, o, gain)


def _rope(x, tab_c, tab_a, tab_b, col0, width, heads, name):
    rows = x.shape[0]
    tr = _tile(rows, 512, 16)
    cb = col0 // width
    assert cb * width == col0

    def body(x_ref, c_ref, a_ref, b_ref, o_ref):
        xv = x_ref[...].astype(F32)
        up = pltpu.roll(xv, width - HALF_ROPE, 1)
        down = pltpu.roll(xv, HALF_ROPE, 1)
        o_ref[...] = (xv * c_ref[...] + up * a_ref[...] + down * b_ref[...]).astype(o_ref.dtype)

    tab_spec = pl.BlockSpec((tr, width), lambda i, h: (i, 0))
    return pl.pallas_call(
        body,
        grid=(rows // tr, heads),
        in_specs=[pl.BlockSpec((tr, width), lambda i, h: (i, cb + h)), tab_spec, tab_spec, tab_spec],
        out_specs=pl.BlockSpec((tr, width), lambda i, h: (i, h)),
        out_shape=jax.ShapeDtypeStruct((rows, heads * width), BF16),
        compiler_params=_params("parallel", "parallel"),
        name=name,
    )(x, tab_c, tab_a, tab_b)


def _kpe_grad(dkb, tab_c, tab_a, tab_b, heads, name):
    rows = dkb.shape[0]
    tr = _tile(rows, 512, 16)

    def body(d_ref, c_ref, a_ref, b_ref, o_ref):
        acc = d_ref[:, 0:LANES]
        for h in range(1, heads):
            acc = acc + d_ref[:, h * LANES:(h + 1) * LANES]
        up = pltpu.roll(acc, LANES - HALF_ROPE, 1)
        down = pltpu.roll(acc, HALF_ROPE, 1)
        o_ref[...] = (acc * c_ref[...] + up * a_ref[...] + down * b_ref[...]).astype(o_ref.dtype)

    tab_spec = pl.BlockSpec((tr, LANES), lambda i: (i, 0))
    return pl.pallas_call(
        body,
        grid=(rows // tr,),
        in_specs=[pl.BlockSpec((tr, heads * LANES), lambda i: (i, 0)), tab_spec, tab_spec, tab_spec],
        out_specs=tab_spec,
        out_shape=jax.ShapeDtypeStruct((rows, LANES), BF16),
        compiler_params=_params("parallel"),
        name=name,
    )(dkb, tab_c, tab_a, tab_b)


class _CommPlan:
    def __init__(self, ins, out_shape, build, n_copies):
        self.ins, self.out_shape, self.build, self.n_copies = list(ins), list(out_shape), build, n_copies

    def scratch(self):
        n = self.n_copies
        return [pltpu.SemaphoreType.DMA((n,)), pltpu.SemaphoreType.DMA((n,)), pltpu.SemaphoreType.DMA((n,))]


def _split_comm(refs, n_in, n_out, comm):
    if comm is None:
        return refs, None
    ci, co = len(comm.ins), len(comm.out_shape)
    ins, c_ins = refs[:n_in], refs[n_in:n_in + ci]
    outs, c_outs = refs[n_in + ci:n_in + ci + n_out], refs[n_in + ci + n_out:n_in + ci + n_out + co]
    rest = refs[n_in + ci + n_out + co:]
    scratch, sems = rest[:-3], rest[-3:]
    return tuple(ins) + tuple(outs) + tuple(scratch), functools.partial(comm.build, c_ins, c_outs, sems)


def _ride_start(copies, first):
    if copies is not None:
        pl.when(first)(copies()[0])


def _ride_wait(copies, last):
    if copies is not None:
        pl.when(last)(copies()[1])


def _rope_rows(x, c, a, b, sign):
    width = x.shape[-1]
    mixed = pltpu.roll(x, width - HALF_ROPE, 1) * a + pltpu.roll(x, HALF_ROPE, 1) * b
    return x * c + mixed if sign > 0 else x * c - mixed


def _attn_fwd(q, ka, kb, v, rope, heads, q_w, q_cb, ka_cb, ka_step, v_cb, v_step, scale, tq_cap, name, comm=None, tk_cap=512):
    s_q, s_k = q.shape[0], ka.shape[0]
    tq = _tile(s_q, tq_cap, 16)
    nq = s_q // tq
    has_kb = kb is not None
    n_in = 7 if has_kb else 3
    tk = _tile(s_k, tk_cap, LANES)

    def body(*refs):
        refs, copies = _split_comm(refs, n_in, 2, comm)
        first = jnp.logical_and(pl.program_id(0) == 0, pl.program_id(1) == 0)
        last = jnp.logical_and(pl.program_id(0) == heads - 1, pl.program_id(1) == nq - 1)
        _ride_start(copies, first)
        if has_kb:
            q_ref, ka_ref, kb_ref, v_ref, c_ref, a_ref, b_ref, o_ref, lse_ref, k_scr = refs

            @pl.when(pl.program_id(1) == 0)
            def _():
                k_scr[:, 0:LANES] = ka_ref[...].astype(BF16)
                k_scr[:, LANES:2 * LANES] = kb_ref[...].astype(BF16)

            keys = k_scr
            qv = _rope_rows(q_ref[...], c_ref[...], a_ref[...], b_ref[...], 1).astype(BF16)
        else:
            q_ref, ka_ref, v_ref, o_ref, lse_ref = refs
            keys = ka_ref
            qv = q_ref[...].astype(BF16)
        c2 = scale * LOG2_E
        m = l = o = None
        nk = s_k // tk
        scores = lambda j: lax.dot_general(qv, keys[j * tk:(j + 1) * tk, :].astype(BF16), (((1,), (1,)), ((), ())),
                                           preferred_element_type=F32)
        s_next = scores(0)
        for j in range(nk):
            sj = s_next
            if j + 1 < nk:
                s_next = scores(j + 1)
            mj = jnp.max(sj, axis=-1, keepdims=True)
            m_new = mj if m is None else jnp.maximum(m, mj)
            pj = jnp.exp2((sj - m_new) * c2)
            lj = jnp.sum(pj, axis=-1, keepdims=True)
            oj = jnp.dot(pj.astype(BF16), v_ref[j * tk:(j + 1) * tk, :].astype(BF16), preferred_element_type=F32)
            if m is None:
                l, o = lj, oj
            else:
                alpha = jnp.exp2((m - m_new) * c2)
                l, o = l * alpha + lj, o * alpha + oj
            m = m_new
        o_ref[...] = (o * (1.0 / l)).astype(o_ref.dtype)
        lse_ref[...] = jnp.broadcast_to(m * c2 + jnp.log2(l), lse_ref.shape)
        _ride_wait(copies, last)

    in_specs = [pl.BlockSpec((tq, q_w), lambda h, i: (i, q_cb + h)),
                pl.BlockSpec((s_k, LANES), lambda h, i: (0, ka_cb + ka_step * h))]
    args = [q, ka]
    if has_kb:
        in_specs.append(pl.BlockSpec((s_k, LANES), lambda h, i: (0, 0)))
        args.append(kb)
    in_specs.append(pl.BlockSpec((s_k, LANES), lambda h, i: (0, v_cb + v_step * h)))
    args.append(v)
    if has_kb:
        in_specs += [pl.BlockSpec((tq, q_w), lambda h, i: (i, 0))] * 3
        args += list(rope)
    out_spec = pl.BlockSpec((tq, LANES), lambda h, i: (i, h))
    out_specs = [out_spec, out_spec]
    out_shape = [jax.ShapeDtypeStruct((s_q, heads * LANES), BF16), jax.ShapeDtypeStruct((s_q, heads * LANES), F32)]
    scratch = [pltpu.VMEM((s_k, 2 * LANES), BF16)] if has_kb else []
    if comm is not None:
        in_specs += [ANY] * len(comm.ins)
        args += comm.ins
        out_specs += [ANY] * len(comm.out_shape)
        out_shape += comm.out_shape
        scratch += comm.scratch()
    res = pl.pallas_call(
        body,
        grid=(heads, nq),
        in_specs=in_specs,
        out_specs=out_specs,
        out_shape=out_shape,
        scratch_shapes=scratch,
        compiler_params=_params("arbitrary", "arbitrary"),
        name=name,
    )(*args)
    return res[0], res[1], list(res[2:])


def _attn_bwd(q, ka, kb, v, o, do, lse, rope, heads, q_w, q_cb, ka_cb, ka_step, v_cb, v_step, o_cb, scale, tq_cap, name,
              comm=None, tk_cap=512):
    s_q, s_k = q.shape[0], ka.shape[0]
    tq = _tile(s_q, tq_cap, 16)
    nq = s_q // tq
    has_kb = kb is not None
    n_in = 10 if has_kb else 6
    n_out = 3
    tk = _tile(s_k, tk_cap, LANES)

    def body(*refs):
        refs, copies = _split_comm(refs, n_in, n_out, comm)
        first = jnp.logical_and(pl.program_id(0) == 0, pl.program_id(1) == 0)
        last = jnp.logical_and(pl.program_id(0) == heads - 1, pl.program_id(1) == nq - 1)
        _ride_start(copies, first)
        if has_kb:
            (q_ref, ka_ref, kb_ref, v_ref, o_ref, do_ref, lse_ref, c_ref, a_ref, b_ref, dq_ref, dkv_ref, dkb_ref, k_scr, dk_acc,
             dv_acc) = refs
        else:
            q_ref, ka_ref, v_ref, o_ref, do_ref, lse_ref, dq_ref, dka_ref, dv_ref, dk_acc, dv_acc = refs
        i = pl.program_id(1)

        @pl.when(i == 0)
        def _():
            dk_acc[...] = jnp.zeros_like(dk_acc)
            dv_acc[...] = jnp.zeros_like(dv_acc)
            if has_kb:
                k_scr[:, 0:LANES] = ka_ref[...].astype(BF16)
                k_scr[:, LANES:2 * LANES] = kb_ref[...].astype(BF16)

        keys = k_scr if has_kb else ka_ref
        if has_kb:
            qv = _rope_rows(q_ref[...], c_ref[...], a_ref[...], b_ref[...], 1).astype(BF16)
        else:
            qv = q_ref[...].astype(BF16)
        dov = do_ref[...].astype(BF16)
        delta = jnp.sum(dov.astype(F32) * o_ref[...].astype(F32), axis=-1, keepdims=True)
        lse2 = lse_ref[:, 0:1]
        c2 = scale * LOG2_E
        nk = s_k // tk
        rows = lambda j: slice(j * tk, (j + 1) * tk)
        nt = (((1,), (1,)), ((), ()))
        tn = (((0,), (0,)), ((), ()))

        def scores(j):
            return (lax.dot_general(qv, keys[rows(j), :].astype(BF16), nt, preferred_element_type=F32),
                    lax.dot_general(dov, v_ref[rows(j), :].astype(BF16), nt, preferred_element_type=F32))

        nxt = scores(0)
        dq = None
        for j in range(nk):
            sj, dpj = nxt
            if j + 1 < nk:
                nxt = scores(j + 1)
            pj = jnp.exp2(sj * c2 - lse2)
            dsj = (pj * (dpj - delta)).astype(BF16)
            dqj = jnp.dot(dsj, keys[rows(j), :].astype(BF16), preferred_element_type=F32)
            dq = dqj if dq is None else dq + dqj
            dk_acc[rows(j), :] += lax.dot_general(dsj, qv, tn, preferred_element_type=F32)
            dv_acc[rows(j), :] += lax.dot_general(pj.astype(BF16), dov, tn, preferred_element_type=F32)
        dq = dq * scale
        if has_kb:
            dq = _rope_rows(dq, c_ref[...], a_ref[...], b_ref[...], -1)
        dq_ref[...] = dq.astype(dq_ref.dtype)

        @pl.when(i == nq - 1)
        def _():
            if has_kb:
                dkv_ref[:, 0:LANES] = (dk_acc[:, 0:LANES] * scale).astype(dkv_ref.dtype)
                dkv_ref[:, LANES:2 * LANES] = dv_acc[...].astype(dkv_ref.dtype)
                dkb_ref[...] = dk_acc[:, LANES:2 * LANES] * scale
            else:
                dka_ref[...] = (dk_acc[...] * scale).astype(dka_ref.dtype)
                dv_ref[...] = dv_acc[...].astype(dv_ref.dtype)

        _ride_wait(copies, last)

    key_spec = lambda cb, step: pl.BlockSpec((s_k, LANES), lambda h, i: (0, cb + step * h))
    row_spec = lambda cb: pl.BlockSpec((tq, LANES), lambda h, i: (i, cb + h))
    in_specs = [pl.BlockSpec((tq, q_w), lambda h, i: (i, q_cb + h)), key_spec(ka_cb, ka_step)]
    args = [q, ka]
    if has_kb:
        in_specs.append(pl.BlockSpec((s_k, LANES), lambda h, i: (0, 0)))
        args.append(kb)
    in_specs += [key_spec(v_cb, v_step), row_spec(o_cb), row_spec(o_cb), row_spec(0)]
    args += [v, o, do, lse]
    if has_kb:
        in_specs += [pl.BlockSpec((tq, q_w), lambda h, i: (i, 0))] * 3
        args += list(rope)
    out_specs = [pl.BlockSpec((tq, q_w), lambda h, i: (i, h))]
    out_shape = [jax.ShapeDtypeStruct((s_q, heads * q_w), BF16)]
    scratch = []
    if has_kb:
        out_specs += [pl.BlockSpec((s_k, 2 * LANES), lambda h, i: (0, h)), key_spec(0, 1)]
        out_shape += [jax.ShapeDtypeStruct((s_k, heads * 2 * LANES), BF16), jax.ShapeDtypeStruct((s_k, heads * LANES), F32)]
        scratch.append(pltpu.VMEM((s_k, 2 * LANES), BF16))
    else:
        out_specs += [key_spec(0, 1), key_spec(0, 1)]
        out_shape += [jax.ShapeDtypeStruct((s_k, heads * LANES), BF16)] * 2
    scratch += [pltpu.VMEM((s_k, q_w), F32), pltpu.VMEM((s_k, LANES), F32)]
    if comm is not None:
        in_specs += [ANY] * len(comm.ins)
        args += comm.ins
        out_specs += [ANY] * len(comm.out_shape)
        out_shape += comm.out_shape
        scratch += comm.scratch()
    res = pl.pallas_call(
        body,
        grid=(heads, nq),
        in_specs=in_specs,
        out_specs=out_specs,
        out_shape=out_shape,
        scratch_shapes=scratch,
        compiler_params=_params("arbitrary", "arbitrary"),
        name=name,
    )(*args)
    return res[0], res[1], res[2], list(res[3:])


def _shift_rows(u, rows):
    t = lax.broadcasted_iota(jnp.int32, u.shape, 0)
    prev = jnp.where(t == 0, 0.0, pltpu.roll(u, 1, 0))
    nxt = jnp.where(t == rows - 1, 0.0, pltpu.roll(u, rows - 1, 0))
    return prev, nxt


def _conv_fwd(z, conv_w, name):
    rows = z.shape[0]
    nblk = CONV_W // LANES

    def body(gb_ref, gc_ref, xin_ref, w_ref, o_ref):
        u = gc_ref[...] * xin_ref[...]
        prev, nxt = _shift_rows(u, rows)
        conv = prev * w_ref[0:1, :] + u * w_ref[1:2, :] + nxt * w_ref[2:3, :]
        o_ref[...] = (gb_ref[...] * conv).astype(o_ref.dtype)

    col = lambda c0: pl.BlockSpec((rows, LANES), lambda j: (0, c0 // LANES + j))
    return pl.pallas_call(
        body,
        grid=(nblk,),
        in_specs=[col(Z_GB), col(Z_GC), col(Z_XIN), pl.BlockSpec((3, LANES), lambda j: (0, j))],
        out_specs=col(0),
        out_shape=jax.ShapeDtypeStruct((rows, CONV_W), BF16),
        compiler_params=_params("parallel"),
        name=name,
    )(z, z, z, conv_w)


def _conv_bwd(z, conv_w, dcat, name):
    rows = z.shape[0]
    nblk = CONV_W // LANES

    def body(gb_ref, gc_ref, xin_ref, w_ref, dc_ref, dgb_ref, dgc_ref, dxin_ref, dw_ref):
        gc = gc_ref[...]
        xin = xin_ref[...]
        dc = dc_ref[...].astype(F32)
        u = gc * xin
        prev, nxt = _shift_rows(u, rows)
        w0, w1, w2 = w_ref[0:1, :], w_ref[1:2, :], w_ref[2:3, :]
        conv = prev * w0 + u * w1 + nxt * w2
        dgb_ref[...] = (dc * conv).astype(dgb_ref.dtype)
        dconv = dc * gb_ref[...]
        dw_ref[0:1, :] = jnp.sum(dconv * prev, axis=0, keepdims=True)
        dw_ref[1:2, :] = jnp.sum(dconv * u, axis=0, keepdims=True)
        dw_ref[2:3, :] = jnp.sum(dconv * nxt, axis=0, keepdims=True)
        dprev, dnxt = _shift_rows(dconv, rows)
        du = dnxt * w0 + dconv * w1 + dprev * w2
        dgc_ref[...] = (du * xin).astype(dgc_ref.dtype)
        dxin_ref[...] = (du * gc).astype(dxin_ref.dtype)

    col = lambda c0: pl.BlockSpec((rows, LANES), lambda j: (0, c0 // LANES + j))
    w_spec = pl.BlockSpec((3, LANES), lambda j: (0, j))
    piece = jax.ShapeDtypeStruct((rows, CONV_W), BF16)
    return pl.pallas_call(
        body,
        grid=(nblk,),
        in_specs=[col(Z_GB), col(Z_GC), col(Z_XIN), w_spec, col(MLA_W)],
        out_specs=[col(0), col(0), col(0), w_spec],
        out_shape=[piece, piece, piece, jax.ShapeDtypeStruct((3, CONV_W), F32)],
        compiler_params=_params("parallel"),
        name=name,
    )(z, z, z, conv_w, dcat)


def _gate_fwd(cat, z, name):
    rows = cat.shape[0]
    tr = _tile(rows, 512, 16)
    tc = 512
    g0 = Z_GATE // tc

    def body(c_ref, g_ref, y_ref):
        g = g_ref[...]
        y_ref[...] = (c_ref[...].astype(F32) * (g * jax.nn.sigmoid(g))).astype(y_ref.dtype)

    blk = pl.BlockSpec((tr, tc), lambda i, j: (i, j))
    return pl.pallas_call(
        body,
        grid=(rows // tr, MIX_W // tc),
        in_specs=[blk, pl.BlockSpec((tr, tc), lambda i, j: (i, g0 + j))],
        out_specs=blk,
        out_shape=jax.ShapeDtypeStruct((rows, MIX_W), BF16),
        compiler_params=_params("parallel", "parallel"),
        name=name,
    )(cat, z)


def _gate_bwd(dy, cat, z, name):
    rows = cat.shape[0]
    tr = _tile(rows, 512, 16)
    tc = 512
    g0 = Z_GATE // tc

    def body(dy_ref, c_ref, g_ref, dcat_ref, dgate_ref):
        g = g_ref[...]
        sg = jax.nn.sigmoid(g)
        dyv = dy_ref[...].astype(F32)
        dcat_ref[...] = (dyv * (g * sg)).astype(dcat_ref.dtype)
        dgate_ref[...] = (dyv * c_ref[...].astype(F32) * (sg * (1.0 + g * (1.0 - sg)))).astype(dgate_ref.dtype)

    blk = pl.BlockSpec((tr, tc), lambda i, j: (i, j))
    out = jax.ShapeDtypeStruct((rows, MIX_W), BF16)
    return pl.pallas_call(
        body,
        grid=(rows // tr, MIX_W // tc),
        in_specs=[blk, blk, pl.BlockSpec((tr, tc), lambda i, j: (i, g0 + j))],
        out_specs=[blk, blk],
        out_shape=[out, out],
        compiler_params=_params("parallel", "parallel"),
        name=name,
    )(dy, cat, z)


def _loss_head(y, target, name):
    rows, width = y.shape
    tr = _tile(rows, 256, 8)

    def body(y_ref, t_ref, g_ref, loss_ref):
        i = pl.program_id(0)
        d = y_ref[...] - t_ref[...]
        g_ref[...] = d / width
        part = 0.5 * jnp.sum(jnp.mean(d * d, axis=-1, keepdims=True), axis=0, keepdims=True)
        part = jnp.broadcast_to(part, loss_ref.shape)

        @pl.when(i == 0)
        def _():
            loss_ref[...] = part

        @pl.when(i > 0)
        def _():
            loss_ref[...] += part

    row_spec = pl.BlockSpec((tr, width), lambda i: (i, 0))
    return pl.pallas_call(
        body,
        grid=(rows // tr,),
        in_specs=[row_spec, row_spec],
        out_specs=[row_spec, pl.BlockSpec((1, LANES), lambda i: (0, 0))],
        out_shape=[jax.ShapeDtypeStruct((rows, width), F32), jax.ShapeDtypeStruct((1, LANES), F32)],
        compiler_params=_params("arbitrary"),
        name=name,
    )(y, target)


CHIP_FLIPS = ((1, 0), (0, 1), (1, 1))
ANY = pl.BlockSpec(memory_space=pl.ANY)


def _chip_copies(pieces, sems, n_slot):
    send_sems, recv_sems, local_sems = sems
    x, y, c = lax.axis_index("x"), lax.axis_index("y"), lax.axis_index("c")
    me = 2 * x + y

    def remote(j, k, a, src, dst):
        fx, fy = CHIP_FLIPS[k]
        return pltpu.make_async_remote_copy(
            src_ref=src, dst_ref=dst, send_sem=send_sems.at[n_slot * k + a], recv_sem=recv_sems.at[n_slot * k + a],
            device_id=((j // 2) ^ fx, (j % 2) ^ fy, c), device_id_type=MESH_ID)

    def peer(j, k):
        fx, fy = CHIP_FLIPS[k]
        return 2 * ((j // 2) ^ fx) + ((j % 2) ^ fy)

    def start_as(j):
        def run():
            for a, (src, dst) in enumerate(pieces(j, j)):
                pltpu.make_async_copy(src, dst, local_sems.at[a]).start()
            for k in range(len(CHIP_FLIPS)):
                for a, (src, dst) in enumerate(pieces(j, peer(j, k))):
                    remote(j, k, a, src, dst).start()
        return run

    def wait_as(j):
        def run():
            for a, (src, dst) in enumerate(pieces(j, j)):
                pltpu.make_async_copy(src, dst, local_sems.at[a]).wait()
            for k in range(len(CHIP_FLIPS)):
                for a, (src, dst) in enumerate(pieces(j, peer(j, k))):
                    remote(j, k, a, src, dst).wait_send()
                for a, (src, dst) in enumerate(pieces(peer(j, k), j)):
                    remote(j, k, a, src, dst).wait_recv()
        return run

    def start():
        for j in range(N_CHIPS):
            pl.when(me == j)(start_as(j))

    def wait():
        for j in range(N_CHIPS):
            pl.when(me == j)(wait_as(j))

    return start, wait


IN_PIECES = ((0, Q_RANK, Z_QLAT), (Q_RANK, KV_RANK, Z_KVLAT), (Q_RANK + KV_RANK, ROPE, Z_KPE),
             (Q_RANK + KV_RANK + ROPE, CONV_W, Z_GB), (Q_RANK + KV_RANK + ROPE + CONV_W, CONV_W, Z_GC),
             (Q_RANK + KV_RANK + ROPE + 2 * CONV_W, CONV_W, Z_XIN), (Q_RANK + KV_RANK + ROPE + 3 * CONV_W, MEM_W, Z_QMEM),
             (Q_RANK + KV_RANK + ROPE + 3 * CONV_W + MEM_W, MIX_W, Z_GATE))
IN_SHARD = IN_COLS // N_CHIPS


def _in_segments(j):
    lo, hi = j * IN_SHARD, (j + 1) * IN_SHARD
    segs = []
    for r0, width, z0 in IN_PIECES:
        a, b = max(lo, r0), min(hi, r0 + width)
        if a < b:
            segs.append((a - lo, z0 + a - r0, b - a))
    return segs


N_SLOT = 11


def _gather_plan(l, shards, zero_rows):
    s_in, s_uq, s_ukv, s_conv, s_mk, s_mv, s_o = shards
    ukv_c, mk_r, mk_c, o_r = s_ukv.shape[2], s_mk.shape[1], s_mk.shape[2], s_o.shape[1]
    stack = lambda s: jax.ShapeDtypeStruct((N_CHIPS,) + s.shape[1:], s.dtype)
    out_shape = [jax.ShapeDtypeStruct((Z_COLS, s_in.shape[2]), s_in.dtype), stack(s_uq),
                 jax.ShapeDtypeStruct((s_ukv.shape[1], N_CHIPS * ukv_c), s_ukv.dtype), stack(s_conv),
                 jax.ShapeDtypeStruct((N_CHIPS * mk_r, 2 * mk_c), s_mk.dtype),
                 jax.ShapeDtypeStruct((N_CHIPS * o_r, s_o.shape[2]), s_o.dtype)]

    def build(ins, outs, sems):
        r_in, r_uq, r_ukv, r_conv, r_mk, r_mv, r_o, r_zero = ins
        f_in, g_uq, f_ukv, g_conv, f_mkv, f_o = outs

        def pieces(j, t):
            out = [(r_in.at[l, pl.ds(so, n), :], f_in.at[pl.ds(zo, n), :]) for so, zo, n in _in_segments(j)]
            out += [(r_uq.at[l], g_uq.at[j]), (r_ukv.at[l], f_ukv.at[:, pl.ds(j * ukv_c, ukv_c)]), (r_conv.at[l], g_conv.at[j]),
                    (r_mk.at[l], f_mkv.at[pl.ds(j * mk_r, mk_r), pl.ds(0, mk_c)]),
                    (r_mv.at[l], f_mkv.at[pl.ds(j * mk_r, mk_r), pl.ds(mk_c, mk_c)]),
                    (r_o.at[l], f_o.at[pl.ds(j * o_r, o_r), :])]
            if j == t:
                out.append((r_zero, f_in.at[pl.ds(Z_KPE + ROPE, LANES - ROPE), :]))
            return out

        return _chip_copies(pieces, sems, N_SLOT)

    return _CommPlan(list(shards) + [zero_rows], out_shape, build, len(CHIP_FLIPS) * N_SLOT)


def _scatter_plan(dwt_in, c_uq, dw_ukv, c_conv, dw_mkv, dw_o):
    ukv_c, mk_r, mk_c, o_r = dw_ukv.shape[1] // N_CHIPS, dw_mkv.shape[0] // N_CHIPS, dw_mkv.shape[1] // 2, dw_o.shape[0] // N_CHIPS
    out_shape = [jax.ShapeDtypeStruct((N_CHIPS, IN_SHARD, dwt_in.shape[1]), dwt_in.dtype),
                 jax.ShapeDtypeStruct(c_uq.shape, c_uq.dtype),
                 jax.ShapeDtypeStruct((N_CHIPS, dw_ukv.shape[0], ukv_c), dw_ukv.dtype),
                 jax.ShapeDtypeStruct(c_conv.shape, c_conv.dtype),
                 jax.ShapeDtypeStruct((N_CHIPS, mk_r, mk_c), dw_mkv.dtype), jax.ShapeDtypeStruct((N_CHIPS, mk_r, mk_c), dw_mkv.dtype),
                 jax.ShapeDtypeStruct((N_CHIPS, o_r, dw_o.shape[1]), dw_o.dtype)]

    def build(ins, outs, sems):
        r_in, r_uq, r_ukv, r_conv, r_mkv, r_o = ins
        o_in, o_uq, o_ukv, o_conv, o_mk, o_mv, o_o = outs

        def pieces(j, t):
            out = [(r_in.at[pl.ds(zo, n), :], o_in.at[j, pl.ds(so, n), :]) for so, zo, n in _in_segments(t)]
            out += [(r_uq.at[t], o_uq.at[j]), (r_ukv.at[:, pl.ds(t * ukv_c, ukv_c)], o_ukv.at[j]), (r_conv.at[t], o_conv.at[j]),
                    (r_mkv.at[pl.ds(t * mk_r, mk_r), pl.ds(0, mk_c)], o_mk.at[j]),
                    (r_mkv.at[pl.ds(t * mk_r, mk_r), pl.ds(mk_c, mk_c)], o_mv.at[j]),
                    (r_o.at[pl.ds(t * o_r, o_r), :], o_o.at[j])]
            return out

        return _chip_copies(pieces, sems, N_SLOT)

    return _CommPlan([dwt_in, c_uq, dw_ukv, c_conv, dw_mkv, dw_o], out_shape, build, len(CHIP_FLIPS) * N_SLOT)


HBM = pl.BlockSpec(memory_space=pltpu.HBM)
SEM = pl.BlockSpec(memory_space=pltpu.SEMAPHORE)
SIDE_EFFECT = pltpu.SideEffectType.DATAFLOW_SIDE_EFFECTING


def _comm_start(plan, name):
    n_in, n_out = len(plan.ins), len(plan.out_shape)
    n_buf = n_in + n_out

    def body(*refs):
        bufs, sems, token = refs[:n_buf], refs[n_buf:n_buf + 3], refs[-1]
        start, _ = plan.build(bufs[:n_in], bufs[n_in:], sems)
        start()
        token[...] = jnp.zeros_like(token)

    lands = [lax.empty(s.shape, s.dtype) for s in plan.out_shape]
    args = [pltpu.with_memory_space_constraint(a, pltpu.HBM) for a in list(plan.ins) + lands]
    res = pl.pallas_call(
        body,
        in_specs=[HBM] * n_buf,
        out_specs=[SEM] * 3 + [HBM] * n_buf + [pl.BlockSpec(memory_space=pltpu.VMEM)],
        out_shape=plan.scratch() + [pltpu.HBM(a.shape, a.dtype) for a in args] + [jax.ShapeDtypeStruct((8, LANES), F32)],
        input_output_aliases={i: 3 + i for i in range(n_buf)},
        compiler_params=pltpu.CompilerParams(has_side_effects=SIDE_EFFECT),
        name=name,
    )(*args)
    return list(res[:3]), list(res[3:3 + n_buf]), res[-1]


def _comm_finish(plan, started, after, name):
    sems, bufs, _ = started
    n_in, n_out = len(plan.ins), len(plan.out_shape)
    n_buf = n_in + n_out

    def body(*refs):
        bufs_in, sem_refs = refs[:n_buf], refs[n_buf:n_buf + 3]
        _, wait = plan.build(bufs_in[:n_in], bufs_in[n_in:], sem_refs)
        wait()

    res = pl.pallas_call(
        body,
        in_specs=[HBM] * n_buf + [SEM] * 3 + [ANY],
        out_specs=[HBM] * n_buf,
        out_shape=[pltpu.HBM(b.shape, b.dtype) for b in bufs],
        input_output_aliases={i: i for i in range(n_buf)},
        compiler_params=pltpu.CompilerParams(has_side_effects=SIDE_EFFECT),
        name=name,
    )(*bufs, *sems, after)
    return list(res[n_in:])


def _comm_call(plan, name):
    n_in, n_out = len(plan.ins), len(plan.out_shape)

    def body(*refs):
        start, wait = plan.build(refs[:n_in], refs[n_in:n_in + n_out], refs[n_in + n_out:])
        start()
        wait()

    return list(pl.pallas_call(
        body,
        in_specs=[ANY] * n_in,
        out_specs=[ANY] * n_out,
        out_shape=plan.out_shape,
        scratch_shapes=plan.scratch(),
        name=name,
    )(*plan.ins))


def _sibling_exchange(arrays, name):
    n = len(arrays)

    def body(*refs):
        ins, outs = refs[:n], refs[n:2 * n]
        send_sems, recv_sems = refs[2 * n:]
        sibling = (lax.axis_index("x"), lax.axis_index("y"), 1 - lax.axis_index("c"))
        copies = []
        for a in range(n):
            cp = pltpu.make_async_remote_copy(
                src_ref=ins[a], dst_ref=outs[a], send_sem=send_sems.at[a], recv_sem=recv_sems.at[a],
                device_id=sibling, device_id_type=MESH_ID)
            cp.start()
            copies.append(cp)
        for cp in copies:
            cp.wait()

    return pl.pallas_call(
        body,
        in_specs=[ANY] * n,
        out_specs=[ANY] * n,
        out_shape=[jax.ShapeDtypeStruct(v.shape, v.dtype) for v in arrays],
        scratch_shapes=[pltpu.SemaphoreType.DMA((n,)), pltpu.SemaphoreType.DMA((n,))],
        name=name,
    )(*arrays)


DEVICE_FLIPS = tuple((fx, fy, fc) for fx in (0, 1) for fy in (0, 1) for fc in (0, 1))[1:]


def _gather_all(v, name):
    def body(v_ref, out_ref, send_sems, recv_sems, local_sem):
        x, y, c = lax.axis_index("x"), lax.axis_index("y"), lax.axis_index("c")
        me = 4 * x + 2 * y + c
        local = pltpu.make_async_copy(v_ref, out_ref.at[me], local_sem)
        local.start()
        copies = [local]
        for k, (fx, fy, fc) in enumerate(DEVICE_FLIPS):
            cp = pltpu.make_async_remote_copy(
                src_ref=v_ref, dst_ref=out_ref.at[me], send_sem=send_sems.at[k], recv_sem=recv_sems.at[k],
                device_id=((x + fx) % 2, (y + fy) % 2, (c + fc) % 2), device_id_type=MESH_ID)
            cp.start()
            copies.append(cp)
        for cp in copies:
            cp.wait()

    return pl.pallas_call(
        body,
        in_specs=[ANY],
        out_specs=ANY,
        out_shape=jax.ShapeDtypeStruct((N_DEV,) + v.shape, v.dtype),
        scratch_shapes=[pltpu.SemaphoreType.DMA((N_DEV - 1,)), pltpu.SemaphoreType.DMA((N_DEV - 1,)), pltpu.SemaphoreType.DMA],
        name=name,
    )(v)


def _sum_slots(parts, name):
    n, rows, cols = parts.shape
    tr = _tile(rows, 256, 16)

    def body(p_ref, o_ref):
        acc = p_ref[0].astype(F32)
        for k in range(1, n):
            acc = acc + p_ref[k].astype(F32)
        o_ref[...] = acc

    return pl.pallas_call(
        body,
        grid=(rows // tr,),
        in_specs=[pl.BlockSpec((n, tr, cols), lambda i: (0, i, 0))],
        out_specs=pl.BlockSpec((tr, cols), lambda i: (i, 0)),
        out_shape=jax.ShapeDtypeStruct((rows, cols), F32),
        compiler_params=_params("parallel"),
        name=name,
    )(parts)


def _adamw_math(w, g, m, v):
    m_new = ADAM_B1 * m + (1.0 - ADAM_B1) * g
    v_new = ADAM_B2 * v + (1.0 - ADAM_B2) * jnp.square(g)
    m_hat = m_new / (1.0 - ADAM_B1 ** ADAM_STEP)
    v_hat = v_new / (1.0 - ADAM_B2 ** ADAM_STEP)
    return -ADAM_LR * (m_hat / (jnp.sqrt(v_hat) + ADAM_EPS) + ADAM_WD * w), m_new, v_new


def _adamw(w, g, m, v, name):
    rows, cols = w.shape
    tr = _tile(rows, 256, 8)

    def body(w_ref, g_ref, m_ref, v_ref, d_out, m_out, v_out):
        d_out[...], m_out[...], v_out[...] = _adamw_math(w_ref[...], g_ref[...], m_ref[...], v_ref[...])

    blk = pl.BlockSpec((tr, cols), lambda i: (i, 0))
    out = jax.ShapeDtypeStruct((rows, cols), F32)
    return pl.pallas_call(
        body,
        grid=(rows // tr,),
        in_specs=[blk] * 4,
        out_specs=[blk] * 3,
        out_shape=[out] * 3,
        compiler_params=_params("parallel"),
        name=name,
    )(w, g, m, v)


def _adamw_layer(l, w, g_a, g_b, m, v, prev, name):
    depth, rows, cols = w.shape
    tr = _tile(rows, 256, 8)

    def body(w_ref, ga_ref, gb_ref, m_ref, v_ref, *rest):
        g_out, d_out, m_out, v_out = rest[-4:]
        g = ga_ref[...] + gb_ref[...]
        g_out[...] = g
        d_out[...], m_out[...], v_out[...] = _adamw_math(w_ref[...], g, m_ref[...], v_ref[...])

    stacked = pl.BlockSpec((None, tr, cols), lambda i: (l, i, 0))
    flat = pl.BlockSpec((tr, cols), lambda i: (i, 0))
    in_specs = [stacked, flat, flat, stacked, stacked]
    args = [w, g_a, g_b, m, v]
    aliases = {}
    if prev is not None:
        in_specs += [ANY] * 4
        args += list(prev)
        aliases = {5 + k: k for k in range(4)}
    out = jax.ShapeDtypeStruct((depth, rows, cols), F32)
    return pl.pallas_call(
        body,
        grid=(rows // tr,),
        in_specs=in_specs,
        out_specs=[stacked] * 4,
        out_shape=[out] * 4,
        input_output_aliases=aliases,
        compiler_params=_params("parallel"),
        name=name,
    )(*args)


def _cols_from_shards(g):
    _, r, c = g.shape
    return jnp.transpose(g, (1, 0, 2)).reshape(r, N_CHIPS * c)


def _cols_to_shards(full):
    r, c4 = full.shape
    c = c4 // N_CHIPS
    return jnp.transpose(full.reshape(r, N_CHIPS, c), (1, 0, 2))


IN_ORDER = (Q_RANK, KV_RANK, ROPE, CONV_W, CONV_W, CONV_W, MEM_W, MIX_W)


def _w_in_to_z_layout(w_in):
    edges = [0]
    for width in IN_ORDER:
        edges.append(edges[-1] + width)
    q_lat, kv_lat, k_pe, gb, gc, xin, q_mem, gate = [w_in[..., edges[i]:edges[i + 1]] for i in range(8)]
    pad = jnp.zeros(k_pe.shape[:-1] + (LANES - ROPE,), w_in.dtype)
    return jnp.concatenate([gate, q_lat, kv_lat, k_pe, pad, gb, gc, xin, q_mem], axis=-1)


def _w_in_from_z_layout(wz):
    cut = lambda c0, width: wz[..., c0:c0 + width]
    return jnp.concatenate(
        [cut(Z_QLAT, Q_RANK), cut(Z_KVLAT, KV_RANK), cut(Z_KPE, ROPE), cut(Z_GB, CONV_W), cut(Z_GC, CONV_W),
         cut(Z_XIN, CONV_W), cut(Z_QMEM, MEM_W), cut(Z_GATE, MIX_W)], axis=-1)


def _w_uq_pad(w_uq):
    r, _ = w_uq.shape
    w = jnp.pad(w_uq.reshape(r, MLA_HEADS, QK_HEAD), ((0, 0), (0, 0), (0, QPAD - QK_HEAD)))
    return w.reshape(r, MLA_HEADS * QPAD)


def _w_uq_unpad(w):
    r, _ = w.shape
    return w.reshape(r, MLA_HEADS, QPAD)[..., :QK_HEAD].reshape(r, MLA_HEADS * QK_HEAD)


def _rope_tables(positions):
    inv_freq = 1.0 / (ROPE_THETA ** (jnp.arange(0, ROPE, 2, dtype=F32) / ROPE))
    ang = positions.astype(F32)[:, None] * inv_freq
    cos, sin = jnp.cos(ang), jnp.sin(ang)
    s = positions.shape[0]
    zero = jnp.zeros((s, HALF_ROPE), F32)
    pad = jnp.zeros((s, LANES - ROPE), F32)
    kc = jnp.concatenate([cos, cos, pad], axis=-1)
    ka = jnp.concatenate([-sin, zero, pad], axis=-1)
    kb = jnp.concatenate([zero, sin, pad], axis=-1)
    qc = jnp.concatenate([jnp.ones((s, NOPE), F32), kc], axis=-1)
    qa = jnp.concatenate([jnp.zeros((s, NOPE), F32), ka], axis=-1)
    qb = jnp.concatenate([jnp.zeros((s, NOPE), F32), kb], axis=-1)
    return (qc, qa, qb), (kc, ka, kb)


def _layer_weights(gathered):
    wt_in, g_uq, w_ukv, g_conv, w_mkv, w_o = gathered
    return (wt_in, _w_uq_pad(_cols_from_shards(g_uq)), w_ukv, _cols_from_shards(g_conv), w_mkv, w_o)


def _layer_fwd(l, x, mem, wts, gains, tabs, comm):
    wt_in, w_uq, w_ukv, conv_w, w_mkv, w_o = wts
    g_pre, g_q, g_kv, g_mem, g_post = gains
    q_tab, k_tab = tabs
    tag = f"l{l}_"
    h = _rmsnorm_fwd(x, g_pre, 0, D_MODEL, tag + "pre_norm")
    z = _matmul(h, wt_in, "nt", F32, tag + "in_proj", tn_cap=1664)
    qn = _rmsnorm_fwd(z, g_q, Z_QLAT, Q_RANK, tag + "q_norm")
    kvn = _rmsnorm_fwd(z, g_kv, Z_KVLAT, KV_RANK, tag + "kv_norm")
    q_raw = _matmul(qn, w_uq, "nn", F32, tag + "uq")
    kv = _matmul(kvn, w_ukv, "nn", BF16, tag + "ukv")
    kpe = _rope(z, *k_tab, Z_KPE, LANES, 1, tag + "k_rope")
    a_out, a_lse, arrived = _attn_fwd(q_raw, kv, kpe, kv, q_tab, MLA_HEADS, QPAD, 0, 0, 2, 1, 2, QK_HEAD ** -0.5, 512,
                                      tag + "mla_fwd", comm)
    c_out = _conv_fwd(z, conv_w, tag + "conv_fwd")
    mem_n = _rmsnorm_fwd(mem, g_mem, 0, D_MODEL, tag + "mem_norm")
    mkv = _matmul(mem_n, w_mkv, "nn", BF16, tag + "mem_kv")
    m_out, m_lse, _ = _attn_fwd(z, mkv, None, mkv, None, MEM_HEADS, LANES, Z_QMEM // LANES, 0, 1, MEM_HEADS, 1,
                                MEM_HEAD ** -0.5, 1024, tag + "mem_fwd")
    cat = jnp.concatenate([a_out, c_out, m_out], axis=-1)
    y = _gate_fwd(cat, z, tag + "gate_fwd")
    o = _matmul(y, w_o, "nn", F32, tag + "out_proj")
    x_new = _post_norm_residual(x, o, g_post, tag + "post_norm")
    saved = (x, h, z, qn, kvn, q_raw, kv, kpe, a_lse, mem_n, mkv, m_lse, cat, y, o)
    return x_new, saved, arrived


def _layer_bwd(l, g, mem, saved, wts, gains, tabs_bwd, comm):
    wt_in, w_uq, w_ukv, conv_w, w_mkv, w_o = wts
    g_pre, g_q, g_kv, g_mem, g_post = gains
    q_tab, k_tab_bwd = tabs_bwd
    x, h, z, qn, kvn, q_raw, kv, kpe, a_lse, mem_n, mkv, m_lse, cat, y, o = saved
    tag = f"l{l}_"
    do, dg_post = _rmsnorm_bwd(o, g_post, g, None, 0, D_MODEL, BF16, tag + "post_norm_bwd")
    dy = _matmul(do, w_o, "nt", F32, tag + "out_proj_dx")
    dw_o = _matmul(y, do, "tn", BF16, tag + "out_proj_dw")
    dcat, dgate = _gate_bwd(dy, cat, z, tag + "gate_bwd")
    dq, dkv, dkpe_h, arrived = _attn_bwd(q_raw, kv, kpe, kv, cat, dcat, a_lse, q_tab, MLA_HEADS, QPAD, 0, 0, 2, 1, 2, 0,
                                         QK_HEAD ** -0.5, 512, tag + "mla_bwd", comm)
    dkpe = _kpe_grad(dkpe_h, *k_tab_bwd, MLA_HEADS, tag + "k_rope_bwd")
    dw_ukv = _matmul(kvn, dkv, "tn", BF16, tag + "ukv_dw")
    dkvn = _matmul(dkv, w_ukv, "nt", F32, tag + "ukv_dx")
    dkv_lat, dg_kv = _rmsnorm_bwd(z, g_kv, dkvn, None, Z_KVLAT, KV_RANK, BF16, tag + "kv_norm_bwd")
    dw_uq = _matmul(qn, dq, "tn", BF16, tag + "uq_dw")
    dqn = _matmul(dq, w_uq, "nt", F32, tag + "uq_dx")
    dq_lat, dg_q = _rmsnorm_bwd(z, g_q, dqn, None, Z_QLAT, Q_RANK, BF16, tag + "q_norm_bwd")
    dgb, dgc, dxin, dconv_w = _conv_bwd(z, conv_w, dcat, tag + "conv_bwd")
    dq_mem, dmk, dmv, _ = _attn_bwd(z, mkv, None, mkv, cat, dcat, m_lse, None, MEM_HEADS, LANES, Z_QMEM // LANES, 0, 1,
                                    MEM_HEADS, 1, (MLA_W + CONV_W) // LANES, MEM_HEAD ** -0.5, 1024, tag + "mem_bwd")
    dmkv = jnp.concatenate([dmk, dmv], axis=-1)
    dw_mkv = _matmul(mem_n, dmkv, "tn", BF16, tag + "mem_kv_dw")
    dmem_n = _matmul(dmkv, w_mkv, "nt", F32, tag + "mem_kv_dx")
    _, dg_mem = _rmsnorm_bwd(mem, g_mem, dmem_n, None, 0, D_MODEL, BF16, tag + "mem_norm_bwd")
    dz = jnp.concatenate([dgate, dq_lat, dkv_lat, dkpe, dgb, dgc, dxin, dq_mem], axis=-1)
    dwt_in = _matmul(dz, h, "tn", BF16, tag + "in_proj_dw", tm_cap=1664, tk_cap=1024)
    dh = _matmul(dz, wt_in, "nn", F32, tag + "in_proj_dx", tk_cap=1664)
    dx, dg_pre = _rmsnorm_bwd(x, g_pre, dh, g, 0, D_MODEL, F32, tag + "pre_norm_bwd")
    contrib = _scatter_plan(dwt_in, _cols_to_shards(_w_uq_unpad(dw_uq)), dw_ukv, _cols_to_shards(dconv_w), dw_mkv, dw_o)
    return dx, contrib, (dg_pre, dg_q, dg_kv, dg_mem, dg_post), arrived


GAIN_WIDTHS = (D_MODEL, Q_RANK, KV_RANK, D_MODEL, D_MODEL)


def _pack_gains(parts):
    return jnp.concatenate([p.reshape(-1) for p in parts]).reshape(-1, LANES)


def _unpack_gains(packed, depth):
    flat = packed.reshape(-1)
    out, at = [], 0
    for width in GAIN_WIDTHS:
        out.append(flat[at:at + depth * width].reshape(depth, width))
        at += depth * width
    return out


def kernel(x, mem, positions, pre_norm_g, w_in, q_norm_g, w_uq, kv_norm_g, w_ukv, conv_w, mem_norm_g, w_mk, w_mv, w_o, post_norm_g, loss_target, m_pre_norm_g, m_w_in, m_q_norm_g, m_w_uq, m_kv_norm_g, m_w_ukv, m_conv_w, m_mem_norm_g, m_w_mk, m_w_mv, m_w_o, m_post_norm_g, v_pre_norm_g, v_w_in, v_q_norm_g, v_w_uq, v_kv_norm_g, v_w_ukv, v_conv_w, v_mem_norm_g, v_w_mk, v_w_mv, v_w_o, v_post_norm_g):
    depth = w_in.shape[0]
    x0, mem0, target = x[0], mem[0], loss_target[0]
    tabs = _rope_tables(positions[0])
    tabs_bwd = (tabs[0], (tabs[1][0], -tabs[1][1], -tabs[1][2]))

    flip = lambda t: jnp.transpose(t, (0, 2, 1))
    w_in, m_w_in, v_w_in = flip(w_in), flip(m_w_in), flip(v_w_in)
    shards = [w_in.astype(BF16), w_uq.astype(BF16), w_ukv.astype(BF16), conv_w, w_mk.astype(BF16), w_mv.astype(BF16),
              w_o.astype(BF16)]
    zero_rows = lambda: jnp.zeros((LANES - ROPE, D_MODEL), BF16)

    def layer_gains(l):
        return tuple(g[l][None, :] for g in (pre_norm_g, q_norm_g, kv_norm_g, mem_norm_g, post_norm_g))

    wts, saved = [None] * depth, [None] * depth
    wts[0] = _layer_weights(_comm_call(_gather_plan(0, shards, zero_rows()), "l0_weight_gather"))
    act = x0
    for l in range(depth):
        gains = layer_gains(l)
        if l + 1 < depth:
            plan = _gather_plan(0, [s[l + 1:l + 2] for s in shards], zero_rows())
            started = _comm_start(plan, f"l{l + 1}_gather_start")
            gains = (gains[0] + started[2][0:1, 0:1],) + gains[1:]
        act, saved[l], _ = _layer_fwd(l, act, mem0, wts[l], gains, tabs, None)
        if l + 1 < depth:
            wts[l + 1] = _layer_weights(_comm_finish(plan, started, act, f"l{l + 1}_gather_wait"))
    grad, loss_part = _loss_head(act, target, "loss_head")
    loss = lax.psum(loss_part[0, 0], ("x", "y", "c"))

    names = ("w_in", "w_uq", "w_ukv", "conv_w", "w_mk", "w_mv", "w_o")
    w_shards = (w_in, w_uq, w_ukv, conv_w, w_mk, w_mv, w_o)
    m_shards = (m_w_in, m_w_uq, m_w_ukv, m_conv_w, m_w_mk, m_w_mv, m_w_o)
    v_shards = (v_w_in, v_w_uq, v_w_ukv, v_conv_w, v_w_mk, v_w_mv, v_w_o)
    stacked = [None] * len(names)

    def finish(l, received):
        partial = [_sum_slots(r, f"l{l}_grad_sum_{names[i]}") for i, r in enumerate(received)]
        other = _sibling_exchange(partial, f"l{l}_grad_sibling")
        for i, name in enumerate(names):
            stacked[i] = _adamw_layer(l, w_shards[i], partial[i], other[i], m_shards[i], v_shards[i], stacked[i],
                                      f"l{l}_adamw_{name}")

    dgs = [None] * depth
    pending = None
    for l in reversed(range(depth)):
        gains = layer_gains(l)
        if pending is not None:
            gains = gains[:4] + (gains[4] + pending[1][2][0:1, 0:1],)
        grad, contrib, dgs[l], _ = _layer_bwd(l, grad, mem0, saved[l], wts[l], gains, tabs_bwd, None)
        if pending is not None:
            finish(l + 1, _comm_finish(pending[0], pending[1], grad, f"l{l + 1}_exchange_wait"))
        if l > 0:
            pending = (contrib, _comm_start(contrib, f"l{l}_exchange_start"))
    finish(0, _comm_call(contrib, "l0_grad_exchange"))
    grad_x = grad[None]
    results = {name: tuple(stacked[i]) for i, name in enumerate(names)}
    results["w_in"] = tuple(flip(t) for t in results["w_in"])

    gain_names = ("pre_norm_g", "q_norm_g", "kv_norm_g", "mem_norm_g", "post_norm_g")
    dg_packed = _pack_gains([jnp.concatenate([dgs[l][i] for l in range(depth)], axis=0) for i in range(5)])
    dg_total = _sum_slots(_gather_all(dg_packed, "gain_gather"), "gain_sum")
    gain_outs = (dg_total,) + tuple(_adamw(
        _pack_gains((pre_norm_g, q_norm_g, kv_norm_g, mem_norm_g, post_norm_g)), dg_total,
        _pack_gains((m_pre_norm_g, m_q_norm_g, m_kv_norm_g, m_mem_norm_g, m_post_norm_g)),
        _pack_gains((v_pre_norm_g, v_q_norm_g, v_kv_norm_g, v_mem_norm_g, v_post_norm_g)), "adamw_gains"))
    gain_outs = [_unpack_gains(t, depth) for t in gain_outs]
    for i, name in enumerate(gain_names):
        results[name] = tuple(gain_outs[k][i] for k in range(4))

    order = ("pre_norm_g", "w_in", "q_norm_g", "w_uq", "kv_norm_g", "w_ukv", "conv_w", "mem_norm_g", "w_mk", "w_mv", "w_o",
             "post_norm_g")
    out = [loss, grad_x]
    for k in range(4):
        out += [results[name][k] for name in order]
    return tuple(out)
```

```python
import functools

import jax
import jax.numpy as jnp
from jax import lax
from jax.experimental import pallas as pl
from jax.experimental.pallas import tpu as pltpu

F32 = jnp.float32
BF16 = jnp.bfloat16
MESH_ID = pl.DeviceIdType.MESH

D_MODEL = 2048
EPS = 1e-6
LOG2_E = 1.4426950408889634
ROPE_THETA = 10000.0
MLA_HEADS = 8
NOPE = 128
ROPE = 64
HALF_ROPE = ROPE // 2
QK_HEAD = NOPE + ROPE
V_HEAD = 128
Q_RANK = 512
KV_RANK = 256
CONV_W = 512
MEM_HEADS = 4
MEM_HEAD = 128
MEM_W = MEM_HEADS * MEM_HEAD
MLA_W = MLA_HEADS * V_HEAD
MIX_W = MLA_W + CONV_W + MEM_W
IN_COLS = Q_RANK + KV_RANK + ROPE + 3 * CONV_W + MEM_W + MIX_W
N_CHIPS = 4
N_DEV = 8

LANES = 128
VMEM_LIMIT_BYTES = 56 * 1024 * 1024

QPAD = 2 * LANES
Z_GATE = 0
Z_QLAT = Z_GATE + MIX_W
Z_KVLAT = Z_QLAT + Q_RANK
Z_KPE = Z_KVLAT + KV_RANK
Z_GB = Z_KPE + LANES
Z_GC = Z_GB + CONV_W
Z_XIN = Z_GC + CONV_W
Z_QMEM = Z_XIN + CONV_W
Z_COLS = Z_QMEM + MEM_W

ADAM_LR = 0.001
ADAM_B1 = 0.9
ADAM_B2 = 0.999
ADAM_EPS = 1e-08
ADAM_WD = 0.01
ADAM_STEP = 10


def _tile(dim, cap, unit):
    if dim <= cap:
        return dim
    t = (cap // unit) * unit
    while t >= unit:
        if dim % t == 0:
            return t
        t -= unit
    raise ValueError(f"no tile of {dim} under {cap} in units of {unit}")


def _params(*semantics):
    return pltpu.CompilerParams(dimension_semantics=semantics, vmem_limit_bytes=VMEM_LIMIT_BYTES)


def _matmul(a, b, mode, out_dtype, name, tm_cap=512, tn_cap=1024, tk_cap=2048):
    if mode == "nn":
        (m, k), (k2, n) = a.shape, b.shape
    elif mode == "nt":
        (m, k), (n, k2) = a.shape, b.shape
    else:
        (k, m), (k2, n) = a.shape, b.shape
    assert k == k2, (a.shape, b.shape, mode)
    tm = _tile(m, tm_cap, LANES if mode == "tn" else 16)
    tn = _tile(n, tn_cap, LANES)
    tk = _tile(k, tk_cap, LANES if mode != "tn" else 16)
    nk = k // tk
    if mode == "nn":
        a_spec = pl.BlockSpec((tm, tk), lambda i, j, kk: (i, kk))
        b_spec = pl.BlockSpec((tk, tn), lambda i, j, kk: (kk, j))
        dims = (((1,), (0,)), ((), ()))
    elif mode == "nt":
        a_spec = pl.BlockSpec((tm, tk), lambda i, j, kk: (i, kk))
        b_spec = pl.BlockSpec((tn, tk), lambda i, j, kk: (j, kk))
        dims = (((1,), (1,)), ((), ()))
    else:
        a_spec = pl.BlockSpec((tk, tm), lambda i, j, kk: (kk, i))
        b_spec = pl.BlockSpec((tk, tn), lambda i, j, kk: (kk, j))
        dims = (((0,), (0,)), ((), ()))

    def body(a_ref, b_ref, o_ref, *scratch):
        part = lax.dot_general(a_ref[...].astype(BF16), b_ref[...].astype(BF16), dims, preferred_element_type=F32)
        if nk == 1:
            o_ref[...] = part.astype(o_ref.dtype)
            return
        (acc_ref,) = scratch
        kk = pl.program_id(2)

        @pl.when(kk == 0)
        def _():
            acc_ref[...] = part

        @pl.when(kk > 0)
        def _():
            acc_ref[...] += part

        @pl.when(kk == nk - 1)
        def _():
            o_ref[...] = acc_ref[...].astype(o_ref.dtype)

    return pl.pallas_call(
        body,
        grid=(m // tm, n // tn, nk),
        in_specs=[a_spec, b_spec],
        out_specs=pl.BlockSpec((tm, tn), lambda i, j, kk: (i, j)),
        out_shape=jax.ShapeDtypeStruct((m, n), out_dtype),
        scratch_shapes=[] if nk == 1 else [pltpu.VMEM((tm, tn), F32)],
        compiler_params=_params("parallel", "parallel", "arbitrary"),
        name=name,
    )(a, b)


def _rmsnorm_fwd(x, gain, col0, width, name):
    rows = x.shape[0]
    tr = _tile(rows, 512, 16)
    cb = col0 // width
    assert cb * width == col0

    def body(x_ref, g_ref, o_ref):
        xv = x_ref[...]
        r = lax.rsqrt(jnp.mean(xv * xv, axis=-1, keepdims=True) + EPS)
        o_ref[...] = (xv * r * g_ref[...]).astype(o_ref.dtype)

    return pl.pallas_call(
        body,
        grid=(rows // tr,),
        in_specs=[pl.BlockSpec((tr, width), lambda i: (i, cb)), pl.BlockSpec((1, width), lambda i: (0, 0))],
        out_specs=pl.BlockSpec((tr, width), lambda i: (i, 0)),
        out_shape=jax.ShapeDtypeStruct((rows, width), BF16),
        compiler_params=_params("parallel"),
        name=name,
    )(x, gain)


def _rmsnorm_bwd(x, gain, dy, resid, col0, width, out_dtype, name):
    rows = x.shape[0]
    tr = _tile(rows, 256, 16)
    cb = col0 // width
    assert cb * width == col0
    has_resid = resid is not None

    def body(*refs):
        if has_resid:
            x_ref, g_ref, dy_ref, res_ref, dx_ref, dg_ref = refs
        else:
            x_ref, g_ref, dy_ref, dx_ref, dg_ref = refs
        i = pl.program_id(0)
        xv = x_ref[...]
        dyv = dy_ref[...].astype(F32)
        r = lax.rsqrt(jnp.mean(xv * xv, axis=-1, keepdims=True) + EPS)
        xr = xv * r
        dyg = dyv * g_ref[...]
        c = jnp.mean(dyg * xr, axis=-1, keepdims=True)
        dx = r * (dyg - xr * c)
        if has_resid:
            dx = dx + res_ref[...]
        dx_ref[...] = dx.astype(dx_ref.dtype)
        part = jnp.sum(dyv * xr, axis=0, keepdims=True)

        @pl.when(i == 0)
        def _():
            dg_ref[...] = part

        @pl.when(i > 0)
        def _():
            dg_ref[...] += part

    row_spec = pl.BlockSpec((tr, width), lambda i: (i, 0))
    in_specs = [pl.BlockSpec((tr, width), lambda i: (i, cb)), pl.BlockSpec((1, width), lambda i: (0, 0)), row_spec]
    args = [x, gain, dy]
    if has_resid:
        in_specs.append(row_spec)
        args.append(resid)
    return pl.pallas_call(
        body,
        grid=(rows // tr,),
        in_specs=in_specs,
        out_specs=[row_spec, pl.BlockSpec((1, width), lambda i: (0, 0))],
        out_shape=[jax.ShapeDtypeStruct((rows, width), out_dtype), jax.ShapeDtypeStruct((1, width), F32)],
        compiler_params=_params("arbitrary"),
        name=name,
    )(*args)


def _post_norm_residual(x, o, gain, name):
    rows, width = x.shape
    tr = _tile(rows, 256, 8)

    def body(x_ref, o_ref, g_ref, out_ref):
        ov = o_ref[...]
        r = lax.rsqrt(jnp.mean(ov * ov, axis=-1, keepdims=True) + EPS)
        out_ref[...] = x_ref[...] + ov * r * g_ref[...]

    row_spec = pl.BlockSpec((tr, width), lambda i: (i, 0))
    return pl.pallas_call(
        body,
        grid=(rows // tr,),
        in_specs=[row_spec, row_spec, pl.BlockSpec((1, width), lambda i: (0, 0))],
        out_specs=row_spec,
        out_shape=jax.ShapeDtypeStruct((rows, width), F32),
        compiler_params=_params("parallel"),
        name=name,
    )(x, o, gain)


def _rope(x, tab_c, tab_a, tab_b, col0, width, heads, name):
    rows = x.shape[0]
    tr = _tile(rows, 512, 16)
    cb = col0 // width
    assert cb * width == col0

    def body(x_ref, c_ref, a_ref, b_ref, o_ref):
        xv = x_ref[...].astype(F32)
        up = pltpu.roll(xv, width - HALF_ROPE, 1)
        down = pltpu.roll(xv, HALF_ROPE, 1)
        o_ref[...] = (xv * c_ref[...] + up * a_ref[...] + down * b_ref[...]).astype(o_ref.dtype)

    tab_spec = pl.BlockSpec((tr, width), lambda i, h: (i, 0))
    return pl.pallas_call(
        body,
        grid=(rows // tr, heads),
        in_specs=[pl.BlockSpec((tr, width), lambda i, h: (i, cb + h)), tab_spec, tab_spec, tab_spec],
        out_specs=pl.BlockSpec((tr, width), lambda i, h: (i, h)),
        out_shape=jax.ShapeDtypeStruct((rows, heads * width), BF16),
        compiler_params=_params("parallel", "parallel"),
        name=name,
    )(x, tab_c, tab_a, tab_b)


def _kpe_grad(dkb, tab_c, tab_a, tab_b, heads, name):
    rows = dkb.shape[0]
    tr = _tile(rows, 512, 16)

    def body(d_ref, c_ref, a_ref, b_ref, o_ref):
        acc = d_ref[:, 0:LANES]
        for h in range(1, heads):
            acc = acc + d_ref[:, h * LANES:(h + 1) * LANES]
        up = pltpu.roll(acc, LANES - HALF_ROPE, 1)
        down = pltpu.roll(acc, HALF_ROPE, 1)
        o_ref[...] = (acc * c_ref[...] + up * a_ref[...] + down * b_ref[...]).astype(o_ref.dtype)

    tab_spec = pl.BlockSpec((tr, LANES), lambda i: (i, 0))
    return pl.pallas_call(
        body,
        grid=(rows // tr,),
        in_specs=[pl.BlockSpec((tr, heads * LANES), lambda i: (i, 0)), tab_spec, tab_spec, tab_spec],
        out_specs=tab_spec,
        out_shape=jax.ShapeDtypeStruct((rows, LANES), BF16),
        compiler_params=_params("parallel"),
        name=name,
    )(dkb, tab_c, tab_a, tab_b)


class _CommPlan:
    def __init__(self, ins, out_shape, build, n_copies):
        self.ins, self.out_shape, self.build, self.n_copies = list(ins), list(out_shape), build, n_copies

    def scratch(self):
        n = self.n_copies
        return [pltpu.SemaphoreType.DMA((n,)), pltpu.SemaphoreType.DMA((n,)), pltpu.SemaphoreType.DMA((n,))]


def _split_comm(refs, n_in, n_out, comm):
    if comm is None:
        return refs, None
    ci, co = len(comm.ins), len(comm.out_shape)
    ins, c_ins = refs[:n_in], refs[n_in:n_in + ci]
    outs, c_outs = refs[n_in + ci:n_in + ci + n_out], refs[n_in + ci + n_out:n_in + ci + n_out + co]
    rest = refs[n_in + ci + n_out + co:]
    scratch, sems = rest[:-3], rest[-3:]
    return tuple(ins) + tuple(outs) + tuple(scratch), functools.partial(comm.build, c_ins, c_outs, sems)


def _ride_start(copies, first):
    if copies is not None:
        pl.when(first)(copies()[0])


def _ride_wait(copies, last):
    if copies is not None:
        pl.when(last)(copies()[1])


def _rope_rows(x, c, a, b, sign):
    width = x.shape[-1]
    mixed = pltpu.roll(x, width - HALF_ROPE, 1) * a + pltpu.roll(x, HALF_ROPE, 1) * b
    return x * c + mixed if sign > 0 else x * c - mixed


def _attn_fwd(q, ka, kb, v, rope, heads, q_w, q_cb, ka_cb, ka_step, v_cb, v_step, scale, tq_cap, name, comm=None, tk_cap=512):
    s_q, s_k = q.shape[0], ka.shape[0]
    tq = _tile(s_q, tq_cap, 16)
    nq = s_q // tq
    has_kb = kb is not None
    n_in = 7 if has_kb else 3
    tk = _tile(s_k, tk_cap, LANES)

    def body(*refs):
        refs, copies = _split_comm(refs, n_in, 2, comm)
        first = jnp.logical_and(pl.program_id(0) == 0, pl.program_id(1) == 0)
        last = jnp.logical_and(pl.program_id(0) == heads - 1, pl.program_id(1) == nq - 1)
        _ride_start(copies, first)
        if has_kb:
            q_ref, ka_ref, kb_ref, v_ref, c_ref, a_ref, b_ref, o_ref, lse_ref, k_scr = refs

            @pl.when(pl.program_id(1) == 0)
            def _():
                k_scr[:, 0:LANES] = ka_ref[...].astype(BF16)
                k_scr[:, LANES:2 * LANES] = kb_ref[...].astype(BF16)

            keys = k_scr
            qv = _rope_rows(q_ref[...], c_ref[...], a_ref[...], b_ref[...], 1).astype(BF16)
        else:
            q_ref, ka_ref, v_ref, o_ref, lse_ref = refs
            keys = ka_ref
            qv = q_ref[...].astype(BF16)
        c2 = scale * LOG2_E
        m = l = o = None
        nk = s_k // tk
        scores = lambda j: lax.dot_general(qv, keys[j * tk:(j + 1) * tk, :].astype(BF16), (((1,), (1,)), ((), ())),
                                           preferred_element_type=F32)
        s_next = scores(0)
        for j in range(nk):
            sj = s_next
            if j + 1 < nk:
                s_next = scores(j + 1)
            mj = jnp.max(sj, axis=-1, keepdims=True)
            m_new = mj if m is None else jnp.maximum(m, mj)
            pj = jnp.exp2((sj - m_new) * c2)
            lj = jnp.sum(pj, axis=-1, keepdims=True)
            oj = jnp.dot(pj.astype(BF16), v_ref[j * tk:(j + 1) * tk, :].astype(BF16), preferred_element_type=F32)
            if m is None:
                l, o = lj, oj
            else:
                alpha = jnp.exp2((m - m_new) * c2)
                l, o = l * alpha + lj, o * alpha + oj
            m = m_new
        o_ref[...] = (o * (1.0 / l)).astype(o_ref.dtype)
        lse_ref[...] = jnp.broadcast_to(m * c2 + jnp.log2(l), lse_ref.shape)
        _ride_wait(copies, last)

    in_specs = [pl.BlockSpec((tq, q_w), lambda h, i: (i, q_cb + h)),
                pl.BlockSpec((s_k, LANES), lambda h, i: (0, ka_cb + ka_step * h))]
    args = [q, ka]
    if has_kb:
        in_specs.append(pl.BlockSpec((s_k, LANES), lambda h, i: (0, 0)))
        args.append(kb)
    in_specs.append(pl.BlockSpec((s_k, LANES), lambda h, i: (0, v_cb + v_step * h)))
    args.append(v)
    if has_kb:
        in_specs += [pl.BlockSpec((tq, q_w), lambda h, i: (i, 0))] * 3
        args += list(rope)
    out_spec = pl.BlockSpec((tq, LANES), lambda h, i: (i, h))
    out_specs = [out_spec, out_spec]
    out_shape = [jax.ShapeDtypeStruct((s_q, heads * LANES), BF16), jax.ShapeDtypeStruct((s_q, heads * LANES), F32)]
    scratch = [pltpu.VMEM((s_k, 2 * LANES), BF16)] if has_kb else []
    if comm is not None:
        in_specs += [ANY] * len(comm.ins)
        args += comm.ins
        out_specs += [ANY] * len(comm.out_shape)
        out_shape += comm.out_shape
        scratch += comm.scratch()
    res = pl.pallas_call(
        body,
        grid=(heads, nq),
        in_specs=in_specs,
        out_specs=out_specs,
        out_shape=out_shape,
        scratch_shapes=scratch,
        compiler_params=_params("arbitrary", "arbitrary"),
        name=name,
    )(*args)
    return res[0], res[1], list(res[2:])


def _attn_bwd(q, ka, kb, v, o, do, lse, rope, heads, q_w, q_cb, ka_cb, ka_step, v_cb, v_step, o_cb, scale, tq_cap, name,
              comm=None, tk_cap=512):
    s_q, s_k = q.shape[0], ka.shape[0]
    tq = _tile(s_q, tq_cap, 16)
    nq = s_q // tq
    has_kb = kb is not None
    n_in = 10 if has_kb else 6
    n_out = 3
    tk = _tile(s_k, tk_cap, LANES)

    def body(*refs):
        refs, copies = _split_comm(refs, n_in, n_out, comm)
        first = jnp.logical_and(pl.program_id(0) == 0, pl.program_id(1) == 0)
        last = jnp.logical_and(pl.program_id(0) == heads - 1, pl.program_id(1) == nq - 1)
        _ride_start(copies, first)
        if has_kb:
            (q_ref, ka_ref, kb_ref, v_ref, o_ref, do_ref, lse_ref, c_ref, a_ref, b_ref, dq_ref, dkv_ref, dkb_ref, k_scr, dk_acc,
             dv_acc) = refs
        else:
            q_ref, ka_ref, v_ref, o_ref, do_ref, lse_ref, dq_ref, dka_ref, dv_ref, dk_acc, dv_acc = refs
        i = pl.program_id(1)

        @pl.when(i == 0)
        def _():
            dk_acc[...] = jnp.zeros_like(dk_acc)
            dv_acc[...] = jnp.zeros_like(dv_acc)
            if has_kb:
                k_scr[:, 0:LANES] = ka_ref[...].astype(BF16)
                k_scr[:, LANES:2 * LANES] = kb_ref[...].astype(BF16)

        keys = k_scr if has_kb else ka_ref
        if has_kb:
            qv = _rope_rows(q_ref[...], c_ref[...], a_ref[...], b_ref[...], 1).astype(BF16)
        else:
            qv = q_ref[...].astype(BF16)
        dov = do_ref[...].astype(BF16)
        delta = jnp.sum(dov.astype(F32) * o_ref[...].astype(F32), axis=-1, keepdims=True)
        lse2 = lse_ref[:, 0:1]
        c2 = scale * LOG2_E
        nk = s_k // tk
        rows = lambda j: slice(j * tk, (j + 1) * tk)
        nt = (((1,), (1,)), ((), ()))
        tn = (((0,), (0,)), ((), ()))

        def scores(j):
            return (lax.dot_general(qv, keys[rows(j), :].astype(BF16), nt, preferred_element_type=F32),
                    lax.dot_general(dov, v_ref[rows(j), :].astype(BF16), nt, preferred_element_type=F32))

        nxt = scores(0)
        dq = None
        for j in range(nk):
            sj, dpj = nxt
            if j + 1 < nk:
                nxt = scores(j + 1)
            pj = jnp.exp2(sj * c2 - lse2)
            dsj = (pj * (dpj - delta)).astype(BF16)
            dqj = jnp.dot(dsj, keys[rows(j), :].astype(BF16), preferred_element_type=F32)
            dq = dqj if dq is None else dq + dqj
            dk_acc[rows(j), :] += lax.dot_general(dsj, qv, tn, preferred_element_type=F32)
            dv_acc[rows(j), :] += lax.dot_general(pj.astype(BF16), dov, tn, preferred_element_type=F32)
        dq = dq * scale
        if has_kb:
            dq = _rope_rows(dq, c_ref[...], a_ref[...], b_ref[...], -1)
        dq_ref[...] = dq.astype(dq_ref.dtype)

        @pl.when(i == nq - 1)
        def _():
            if has_kb:
                dkv_ref[:, 0:LANES] = (dk_acc[:, 0:LANES] * scale).astype(dkv_ref.dtype)
                dkv_ref[:, LANES:2 * LANES] = dv_acc[...].astype(dkv_ref.dtype)
                dkb_ref[...] = dk_acc[:, LANES:2 * LANES] * scale
            else:
                dka_ref[...] = (dk_acc[...] * scale).astype(dka_ref.dtype)
                dv_ref[...] = dv_acc[...].astype(dv_ref.dtype)

        _ride_wait(copies, last)

    key_spec = lambda cb, step: pl.BlockSpec((s_k, LANES), lambda h, i: (0, cb + step * h))
    row_spec = lambda cb: pl.BlockSpec((tq, LANES), lambda h, i: (i, cb + h))
    in_specs = [pl.BlockSpec((tq, q_w), lambda h, i: (i, q_cb + h)), key_spec(ka_cb, ka_step)]
    args = [q, ka]
    if has_kb:
        in_specs.append(pl.BlockSpec((s_k, LANES), lambda h, i: (0, 0)))
        args.append(kb)
    in_specs += [key_spec(v_cb, v_step), row_spec(o_cb), row_spec(o_cb), row_spec(0)]
    args += [v, o, do, lse]
    if has_kb:
        in_specs += [pl.BlockSpec((tq, q_w), lambda h, i: (i, 0))] * 3
        args += list(rope)
    out_specs = [pl.BlockSpec((tq, q_w), lambda h, i: (i, h))]
    out_shape = [jax.ShapeDtypeStruct((s_q, heads * q_w), BF16)]
    scratch = []
    if has_kb:
        out_specs += [pl.BlockSpec((s_k, 2 * LANES), lambda h, i: (0, h)), key_spec(0, 1)]
        out_shape += [jax.ShapeDtypeStruct((s_k, heads * 2 * LANES), BF16), jax.ShapeDtypeStruct((s_k, heads * LANES), F32)]
        scratch.append(pltpu.VMEM((s_k, 2 * LANES), BF16))
    else:
        out_specs += [key_spec(0, 1), key_spec(0, 1)]
        out_shape += [jax.ShapeDtypeStruct((s_k, heads * LANES), BF16)] * 2
    scratch += [pltpu.VMEM((s_k, q_w), F32), pltpu.VMEM((s_k, LANES), F32)]
    if comm is not None:
        in_specs += [ANY] * len(comm.ins)
        args += comm.ins
        out_specs += [ANY] * len(comm.out_shape)
        out_shape += comm.out_shape
        scratch += comm.scratch()
    res = pl.pallas_call(
        body,
        grid=(heads, nq),
        in_specs=in_specs,
        out_specs=out_specs,
        out_shape=out_shape,
        scratch_shapes=scratch,
        compiler_params=_params("arbitrary", "arbitrary"),
        name=name,
    )(*args)
    return res[0], res[1], res[2], list(res[3:])


def _shift_rows(u, rows):
    t = lax.broadcasted_iota(jnp.int32, u.shape, 0)
    prev = jnp.where(t == 0, 0.0, pltpu.roll(u, 1, 0))
    nxt = jnp.where(t == rows - 1, 0.0, pltpu.roll(u, rows - 1, 0))
    return prev, nxt


def _conv_fwd(z, conv_w, name):
    rows = z.shape[0]
    nblk = CONV_W // LANES

    def body(gb_ref, gc_ref, xin_ref, w_ref, o_ref):
        u = gc_ref[...] * xin_ref[...]
        prev, nxt = _shift_rows(u, rows)
        conv = prev * w_ref[0:1, :] + u * w_ref[1:2, :] + nxt * w_ref[2:3, :]
        o_ref[...] = (gb_ref[...] * conv).astype(o_ref.dtype)

    col = lambda c0: pl.BlockSpec((rows, LANES), lambda j: (0, c0 // LANES + j))
    return pl.pallas_call(
        body,
        grid=(nblk,),
        in_specs=[col(Z_GB), col(Z_GC), col(Z_XIN), pl.BlockSpec((3, LANES), lambda j: (0, j))],
        out_specs=col(0),
        out_shape=jax.ShapeDtypeStruct((rows, CONV_W), BF16),
        compiler_params=_params("parallel"),
        name=name,
    )(z, z, z, conv_w)


def _conv_bwd(z, conv_w, dcat, name):
    rows = z.shape[0]
    nblk = CONV_W // LANES

    def body(gb_ref, gc_ref, xin_ref, w_ref, dc_ref, dgb_ref, dgc_ref, dxin_ref, dw_ref):
        gc = gc_ref[...]
        xin = xin_ref[...]
        dc = dc_ref[...].astype(F32)
        u = gc * xin
        prev, nxt = _shift_rows(u, rows)
        w0, w1, w2 = w_ref[0:1, :], w_ref[1:2, :], w_ref[2:3, :]
        conv = prev * w0 + u * w1 + nxt * w2
        dgb_ref[...] = (dc * conv).astype(dgb_ref.dtype)
        dconv = dc * gb_ref[...]
        dw_ref[0:1, :] = jnp.sum(dconv * prev, axis=0, keepdims=True)
        dw_ref[1:2, :] = jnp.sum(dconv * u, axis=0, keepdims=True)
        dw_ref[2:3, :] = jnp.sum(dconv * nxt, axis=0, keepdims=True)
        dprev, dnxt = _shift_rows(dconv, rows)
        du = dnxt * w0 + dconv * w1 + dprev * w2
        dgc_ref[...] = (du * xin).astype(dgc_ref.dtype)
        dxin_ref[...] = (du * gc).astype(dxin_ref.dtype)

    col = lambda c0: pl.BlockSpec((rows, LANES), lambda j: (0, c0 // LANES + j))
    w_spec = pl.BlockSpec((3, LANES), lambda j: (0, j))
    piece = jax.ShapeDtypeStruct((rows, CONV_W), BF16)
    return pl.pallas_call(
        body,
        grid=(nblk,),
        in_specs=[col(Z_GB), col(Z_GC), col(Z_XIN), w_spec, col(MLA_W)],
        out_specs=[col(0), col(0), col(0), w_spec],
        out_shape=[piece, piece, piece, jax.ShapeDtypeStruct((3, CONV_W), F32)],
        compiler_params=_params("parallel"),
        name=name,
    )(z, z, z, conv_w, dcat)


def _gate_fwd(cat, z, name):
    rows = cat.shape[0]
    tr = _tile(rows, 512, 16)
    tc = 512
    g0 = Z_GATE // tc

    def body(c_ref, g_ref, y_ref):
        g = g_ref[...]
        y_ref[...] = (c_ref[...].astype(F32) * (g * jax.nn.sigmoid(g))).astype(y_ref.dtype)

    blk = pl.BlockSpec((tr, tc), lambda i, j: (i, j))
    return pl.pallas_call(
        body,
        grid=(rows // tr, MIX_W // tc),
        in_specs=[blk, pl.BlockSpec((tr, tc), lambda i, j: (i, g0 + j))],
        out_specs=blk,
        out_shape=jax.ShapeDtypeStruct((rows, MIX_W), BF16),
        compiler_params=_params("parallel", "parallel"),
        name=name,
    )(cat, z)


def _gate_bwd(dy, cat, z, name):
    rows = cat.shape[0]
    tr = _tile(rows, 512, 16)
    tc = 512
    g0 = Z_GATE // tc

    def body(dy_ref, c_ref, g_ref, dcat_ref, dgate_ref):
        g = g_ref[...]
        sg = jax.nn.sigmoid(g)
        dyv = dy_ref[...].astype(F32)
        dcat_ref[...] = (dyv * (g * sg)).astype(dcat_ref.dtype)
        dgate_ref[...] = (dyv * c_ref[...].astype(F32) * (sg * (1.0 + g * (1.0 - sg)))).astype(dgate_ref.dtype)

    blk = pl.BlockSpec((tr, tc), lambda i, j: (i, j))
    out = jax.ShapeDtypeStruct((rows, MIX_W), BF16)
    return pl.pallas_call(
        body,
        grid=(rows // tr, MIX_W // tc),
        in_specs=[blk, blk, pl.BlockSpec((tr, tc), lambda i, j: (i, g0 + j))],
        out_specs=[blk, blk],
        out_shape=[out, out],
        compiler_params=_params("parallel", "parallel"),
        name=name,
    )(dy, cat, z)


def _loss_head(y, target, name):
    rows, width = y.shape
    tr = _tile(rows, 256, 8)

    def body(y_ref, t_ref, g_ref, loss_ref):
        i = pl.program_id(0)
        d = y_ref[...] - t_ref[...]
        g_ref[...] = d / width
        part = 0.5 * jnp.sum(jnp.mean(d * d, axis=-1, keepdims=True), axis=0, keepdims=True)
        part = jnp.broadcast_to(part, loss_ref.shape)

        @pl.when(i == 0)
        def _():
            loss_ref[...] = part

        @pl.when(i > 0)
        def _():
            loss_ref[...] += part

    row_spec = pl.BlockSpec((tr, width), lambda i: (i, 0))
    return pl.pallas_call(
        body,
        grid=(rows // tr,),
        in_specs=[row_spec, row_spec],
        out_specs=[row_spec, pl.BlockSpec((1, LANES), lambda i: (0, 0))],
        out_shape=[jax.ShapeDtypeStruct((rows, width), F32), jax.ShapeDtypeStruct((1, LANES), F32)],
        compiler_params=_params("arbitrary"),
        name=name,
    )(y, target)


CHIP_FLIPS = ((1, 0), (0, 1), (1, 1))
ANY = pl.BlockSpec(memory_space=pl.ANY)


def _chip_copies(pieces, sems, n_slot):
    send_sems, recv_sems, local_sems = sems
    x, y, c = lax.axis_index("x"), lax.axis_index("y"), lax.axis_index("c")
    me = 2 * x + y

    def remote(j, k, a, src, dst):
        fx, fy = CHIP_FLIPS[k]
        return pltpu.make_async_remote_copy(
            src_ref=src, dst_ref=dst, send_sem=send_sems.at[n_slot * k + a], recv_sem=recv_sems.at[n_slot * k + a],
            device_id=((j // 2) ^ fx, (j % 2) ^ fy, c), device_id_type=MESH_ID)

    def peer(j, k):
        fx, fy = CHIP_FLIPS[k]
        return 2 * ((j // 2) ^ fx) + ((j % 2) ^ fy)

    def start_as(j):
        def run():
            for a, (src, dst) in enumerate(pieces(j, j)):
                pltpu.make_async_copy(src, dst, local_sems.at[a]).start()
            for k in range(len(CHIP_FLIPS)):
                for a, (src, dst) in enumerate(pieces(j, peer(j, k))):
                    remote(j, k, a, src, dst).start()
        return run

    def wait_as(j):
        def run():
            for a, (src, dst) in enumerate(pieces(j, j)):
                pltpu.make_async_copy(src, dst, local_sems.at[a]).wait()
            for k in range(len(CHIP_FLIPS)):
                for a, (src, dst) in enumerate(pieces(j, peer(j, k))):
                    remote(j, k, a, src, dst).wait_send()
                for a, (src, dst) in enumerate(pieces(peer(j, k), j)):
                    remote(j, k, a, src, dst).wait_recv()
        return run

    def start():
        for j in range(N_CHIPS):
            pl.when(me == j)(start_as(j))

    def wait():
        for j in range(N_CHIPS):
            pl.when(me == j)(wait_as(j))

    return start, wait


IN_PIECES = ((0, Q_RANK, Z_QLAT), (Q_RANK, KV_RANK, Z_KVLAT), (Q_RANK + KV_RANK, ROPE, Z_KPE),
             (Q_RANK + KV_RANK + ROPE, CONV_W, Z_GB), (Q_RANK + KV_RANK + ROPE + CONV_W, CONV_W, Z_GC),
             (Q_RANK + KV_RANK + ROPE + 2 * CONV_W, CONV_W, Z_XIN), (Q_RANK + KV_RANK + ROPE + 3 * CONV_W, MEM_W, Z_QMEM),
             (Q_RANK + KV_RANK + ROPE + 3 * CONV_W + MEM_W, MIX_W, Z_GATE))
IN_SHARD = IN_COLS // N_CHIPS


def _in_segments(j):
    lo, hi = j * IN_SHARD, (j + 1) * IN_SHARD
    segs = []
    for r0, width, z0 in IN_PIECES:
        a, b = max(lo, r0), min(hi, r0 + width)
        if a < b:
            segs.append((a - lo, z0 + a - r0, b - a))
    return segs


N_SLOT = 11


def _gather_plan(l, shards, zero_rows, part="all"):
    s_in, s_uq, s_ukv, s_conv, s_mk, s_mv, s_o = shards
    ukv_c, mk_r, mk_c, o_r = s_ukv.shape[2], s_mk.shape[1], s_mk.shape[2], s_o.shape[1]
    stack = lambda s: jax.ShapeDtypeStruct((N_CHIPS,) + s.shape[1:], s.dtype)
    in_ins, in_outs = [s_in, zero_rows], [jax.ShapeDtypeStruct((Z_COLS, s_in.shape[2]), s_in.dtype)]
    rest_ins = [s_uq, s_ukv, s_conv, s_mk, s_mv, s_o]
    rest_outs = [stack(s_uq), jax.ShapeDtypeStruct((s_ukv.shape[1], N_CHIPS * ukv_c), s_ukv.dtype), stack(s_conv),
                 jax.ShapeDtypeStruct((N_CHIPS * mk_r, 2 * mk_c), s_mk.dtype),
                 jax.ShapeDtypeStruct((N_CHIPS * o_r, s_o.shape[2]), s_o.dtype)]
    with_in, with_rest = part != "rest", part != "in"

    def build(ins, outs, sems):
        ins, outs = list(ins), list(outs)
        if with_in:
            r_in, r_zero, f_in = ins.pop(0), ins.pop(0), outs.pop(0)
        if with_rest:
            r_uq, r_ukv, r_conv, r_mk, r_mv, r_o = ins
            g_uq, f_ukv, g_conv, f_mkv, f_o = outs

        def pieces(j, t):
            out = []
            if with_in:
                out += [(r_in.at[l, pl.ds(so, n), :], f_in.at[pl.ds(zo, n), :]) for so, zo, n in _in_segments(j)]
            if with_rest:
                out += [(r_uq.at[l], g_uq.at[j]), (r_ukv.at[l], f_ukv.at[:, pl.ds(j * ukv_c, ukv_c)]),
                        (r_conv.at[l], g_conv.at[j]),
                        (r_mk.at[l], f_mkv.at[pl.ds(j * mk_r, mk_r), pl.ds(0, mk_c)]),
                        (r_mv.at[l], f_mkv.at[pl.ds(j * mk_r, mk_r), pl.ds(mk_c, mk_c)]),
                        (r_o.at[l], f_o.at[pl.ds(j * o_r, o_r), :])]
            if with_in and j == t:
                out.append((r_zero, f_in.at[pl.ds(Z_KPE + ROPE, LANES - ROPE), :]))
            return out

        return _chip_copies(pieces, sems, N_SLOT)

    ins = (in_ins if with_in else []) + (rest_ins if with_rest else [])
    outs = (in_outs if with_in else []) + (rest_outs if with_rest else [])
    return _CommPlan(ins, outs, build, len(CHIP_FLIPS) * N_SLOT)


def _scatter_plan(dwt_in, c_uq, dw_ukv, c_conv, dw_mkv, dw_o, part="all"):
    ukv_c, mk_r, mk_c, o_r = dw_ukv.shape[1] // N_CHIPS, dw_mkv.shape[0] // N_CHIPS, dw_mkv.shape[1] // 2, dw_o.shape[0] // N_CHIPS
    with_in, with_rest = part != "rest", part != "in"
    in_outs = [jax.ShapeDtypeStruct((N_CHIPS, IN_SHARD, D_MODEL), BF16)]
    rest_ins = [c_uq, dw_ukv, c_conv, dw_mkv, dw_o]
    rest_outs = [jax.ShapeDtypeStruct(c_uq.shape, c_uq.dtype),
                 jax.ShapeDtypeStruct((N_CHIPS, dw_ukv.shape[0], ukv_c), dw_ukv.dtype),
                 jax.ShapeDtypeStruct(c_conv.shape, c_conv.dtype),
                 jax.ShapeDtypeStruct((N_CHIPS, mk_r, mk_c), dw_mkv.dtype), jax.ShapeDtypeStruct((N_CHIPS, mk_r, mk_c), dw_mkv.dtype),
                 jax.ShapeDtypeStruct((N_CHIPS, o_r, dw_o.shape[1]), dw_o.dtype)]

    def build(ins, outs, sems):
        ins, outs = list(ins), list(outs)
        if with_in:
            r_in, o_in = ins.pop(0), outs.pop(0)
        if with_rest:
            r_uq, r_ukv, r_conv, r_mkv, r_o = ins
            o_uq, o_ukv, o_conv, o_mk, o_mv, o_o = outs

        def pieces(j, t):
            out = []
            if with_in:
                out += [(r_in.at[pl.ds(zo, n), :], o_in.at[j, pl.ds(so, n), :]) for so, zo, n in _in_segments(t)]
            if with_rest:
                out += [(r_uq.at[t], o_uq.at[j]), (r_ukv.at[:, pl.ds(t * ukv_c, ukv_c)], o_ukv.at[j]),
                        (r_conv.at[t], o_conv.at[j]),
                        (r_mkv.at[pl.ds(t * mk_r, mk_r), pl.ds(0, mk_c)], o_mk.at[j]),
                        (r_mkv.at[pl.ds(t * mk_r, mk_r), pl.ds(mk_c, mk_c)], o_mv.at[j]),
                        (r_o.at[pl.ds(t * o_r, o_r), :], o_o.at[j])]
            return out

        return _chip_copies(pieces, sems, N_SLOT)

    ins = ([dwt_in] if with_in else []) + (rest_ins if with_rest else [])
    outs = (in_outs if with_in else []) + (rest_outs if with_rest else [])
    return _CommPlan(ins, outs, build, len(CHIP_FLIPS) * N_SLOT)


HBM = pl.BlockSpec(memory_space=pltpu.HBM)
SEM = pl.BlockSpec(memory_space=pltpu.SEMAPHORE)
SIDE_EFFECT = pltpu.SideEffectType.DATAFLOW_SIDE_EFFECTING


def _comm_start(plan, after, name):
    n_in, n_out = len(plan.ins), len(plan.out_shape)
    n_buf = n_in + n_out

    def body(*refs):
        bufs, sems, token = refs[:n_buf], refs[n_buf + 1:n_buf + 4], refs[-1]
        start, _ = plan.build(bufs[:n_in], bufs[n_in:], sems)
        start()
        token[...] = jnp.zeros_like(token)

    lands = [lax.empty(s.shape, s.dtype) for s in plan.out_shape]
    args = [pltpu.with_memory_space_constraint(a, pltpu.HBM) for a in list(plan.ins) + lands]
    res = pl.pallas_call(
        body,
        in_specs=[HBM] * n_buf + [ANY],
        out_specs=[SEM] * 3 + [HBM] * n_buf + [pl.BlockSpec(memory_space=pltpu.VMEM)],
        out_shape=plan.scratch() + [pltpu.HBM(a.shape, a.dtype) for a in args] + [jax.ShapeDtypeStruct((8, LANES), F32)],
        input_output_aliases={i: 3 + i for i in range(n_buf)},
        compiler_params=pltpu.CompilerParams(has_side_effects=SIDE_EFFECT),
        name=name,
    )(*args, after)
    return list(res[:3]), list(res[3:3 + n_buf]), res[-1]


def _comm_finish(plan, started, after, name):
    sems, bufs, _ = started
    n_in, n_out = len(plan.ins), len(plan.out_shape)
    n_buf = n_in + n_out

    def body(*refs):
        bufs_in, sem_refs = refs[:n_buf], refs[n_buf:n_buf + 3]
        _, wait = plan.build(bufs_in[:n_in], bufs_in[n_in:], sem_refs)
        wait()

    res = pl.pallas_call(
        body,
        in_specs=[HBM] * n_buf + [SEM] * 3 + [ANY],
        out_specs=[HBM] * n_buf,
        out_shape=[pltpu.HBM(b.shape, b.dtype) for b in bufs],
        input_output_aliases={i: i for i in range(n_buf)},
        compiler_params=pltpu.CompilerParams(has_side_effects=SIDE_EFFECT),
        name=name,
    )(*bufs, *sems, after)
    return list(res[n_in:])


def _comm_call(plan, name):
    n_in, n_out = len(plan.ins), len(plan.out_shape)

    def body(*refs):
        start, wait = plan.build(refs[:n_in], refs[n_in:n_in + n_out], refs[n_in + n_out:])
        start()
        wait()

    return list(pl.pallas_call(
        body,
        in_specs=[ANY] * n_in,
        out_specs=[ANY] * n_out,
        out_shape=plan.out_shape,
        scratch_shapes=plan.scratch(),
        name=name,
    )(*plan.ins))


def _sibling_exchange(arrays, name):
    n = len(arrays)

    def body(*refs):
        ins, outs = refs[:n], refs[n:2 * n]
        send_sems, recv_sems = refs[2 * n:]
        sibling = (lax.axis_index("x"), lax.axis_index("y"), 1 - lax.axis_index("c"))
        copies = []
        for a in range(n):
            cp = pltpu.make_async_remote_copy(
                src_ref=ins[a], dst_ref=outs[a], send_sem=send_sems.at[a], recv_sem=recv_sems.at[a],
                device_id=sibling, device_id_type=MESH_ID)
            cp.start()
            copies.append(cp)
        for cp in copies:
            cp.wait()

    return pl.pallas_call(
        body,
        in_specs=[ANY] * n,
        out_specs=[ANY] * n,
        out_shape=[jax.ShapeDtypeStruct(v.shape, v.dtype) for v in arrays],
        scratch_shapes=[pltpu.SemaphoreType.DMA((n,)), pltpu.SemaphoreType.DMA((n,))],
        name=name,
    )(*arrays)


DEVICE_FLIPS = tuple((fx, fy, fc) for fx in (0, 1) for fy in (0, 1) for fc in (0, 1))[1:]


def _gather_all(v, name):
    def body(v_ref, out_ref, send_sems, recv_sems, local_sem):
        x, y, c = lax.axis_index("x"), lax.axis_index("y"), lax.axis_index("c")
        me = 4 * x + 2 * y + c
        local = pltpu.make_async_copy(v_ref, out_ref.at[me], local_sem)
        local.start()
        copies = [local]
        for k, (fx, fy, fc) in enumerate(DEVICE_FLIPS):
            cp = pltpu.make_async_remote_copy(
                src_ref=v_ref, dst_ref=out_ref.at[me], send_sem=send_sems.at[k], recv_sem=recv_sems.at[k],
                device_id=((x + fx) % 2, (y + fy) % 2, (c + fc) % 2), device_id_type=MESH_ID)
            cp.start()
            copies.append(cp)
        for cp in copies:
            cp.wait()

    return pl.pallas_call(
        body,
        in_specs=[ANY],
        out_specs=ANY,
        out_shape=jax.ShapeDtypeStruct((N_DEV,) + v.shape, v.dtype),
        scratch_shapes=[pltpu.SemaphoreType.DMA((N_DEV - 1,)), pltpu.SemaphoreType.DMA((N_DEV - 1,)), pltpu.SemaphoreType.DMA],
        name=name,
    )(v)


def _sum_slots(parts, name):
    n, rows, cols = parts.shape
    tr = _tile(rows, 256, 16)

    def body(p_ref, o_ref):
        acc = p_ref[0].astype(F32)
        for k in range(1, n):
            acc = acc + p_ref[k].astype(F32)
        o_ref[...] = acc

    return pl.pallas_call(
        body,
        grid=(rows // tr,),
        in_specs=[pl.BlockSpec((n, tr, cols), lambda i: (0, i, 0))],
        out_specs=pl.BlockSpec((tr, cols), lambda i: (i, 0)),
        out_shape=jax.ShapeDtypeStruct((rows, cols), F32),
        compiler_params=_params("parallel"),
        name=name,
    )(parts)


def _adamw_math(w, g, m, v):
    m_new = ADAM_B1 * m + (1.0 - ADAM_B1) * g
    v_new = ADAM_B2 * v + (1.0 - ADAM_B2) * jnp.square(g)
    m_hat = m_new / (1.0 - ADAM_B1 ** ADAM_STEP)
    v_hat = v_new / (1.0 - ADAM_B2 ** ADAM_STEP)
    return -ADAM_LR * (m_hat / (jnp.sqrt(v_hat) + ADAM_EPS) + ADAM_WD * w), m_new, v_new


def _adamw(w, g, m, v, name):
    rows, cols = w.shape
    tr = _tile(rows, 256, 8)

    def body(w_ref, g_ref, m_ref, v_ref, d_out, m_out, v_out):
        d_out[...], m_out[...], v_out[...] = _adamw_math(w_ref[...], g_ref[...], m_ref[...], v_ref[...])

    blk = pl.BlockSpec((tr, cols), lambda i: (i, 0))
    out = jax.ShapeDtypeStruct((rows, cols), F32)
    return pl.pallas_call(
        body,
        grid=(rows // tr,),
        in_specs=[blk] * 4,
        out_specs=[blk] * 3,
        out_shape=[out] * 3,
        compiler_params=_params("parallel"),
        name=name,
    )(w, g, m, v)


def _adamw_layer(l, w, g_a, g_b, m, v, prev, name):
    depth, rows, cols = w.shape
    tr = _tile(rows, 256, 8)

    def body(w_ref, ga_ref, gb_ref, m_ref, v_ref, *rest):
        g_out, d_out, m_out, v_out = rest[-4:]
        g = ga_ref[...] + gb_ref[...]
        g_out[...] = g
        d_out[...], m_out[...], v_out[...] = _adamw_math(w_ref[...], g, m_ref[...], v_ref[...])

    stacked = pl.BlockSpec((None, tr, cols), lambda i: (l, i, 0))
    flat = pl.BlockSpec((tr, cols), lambda i: (i, 0))
    in_specs = [stacked, flat, flat, stacked, stacked]
    args = [w, g_a, g_b, m, v]
    aliases = {}
    if prev is not None:
        in_specs += [ANY] * 4
        args += list(prev)
        aliases = {5 + k: k for k in range(4)}
    out = jax.ShapeDtypeStruct((depth, rows, cols), F32)
    return pl.pallas_call(
        body,
        grid=(rows // tr,),
        in_specs=in_specs,
        out_specs=[stacked] * 4,
        out_shape=[out] * 4,
        input_output_aliases=aliases,
        compiler_params=_params("parallel"),
        name=name,
    )(*args)


def _cols_from_shards(g):
    _, r, c = g.shape
    return jnp.transpose(g, (1, 0, 2)).reshape(r, N_CHIPS * c)


def _cols_to_shards(full):
    r, c4 = full.shape
    c = c4 // N_CHIPS
    return jnp.transpose(full.reshape(r, N_CHIPS, c), (1, 0, 2))


IN_ORDER = (Q_RANK, KV_RANK, ROPE, CONV_W, CONV_W, CONV_W, MEM_W, MIX_W)


def _w_in_to_z_layout(w_in):
    edges = [0]
    for width in IN_ORDER:
        edges.append(edges[-1] + width)
    q_lat, kv_lat, k_pe, gb, gc, xin, q_mem, gate = [w_in[..., edges[i]:edges[i + 1]] for i in range(8)]
    pad = jnp.zeros(k_pe.shape[:-1] + (LANES - ROPE,), w_in.dtype)
    return jnp.concatenate([gate, q_lat, kv_lat, k_pe, pad, gb, gc, xin, q_mem], axis=-1)


def _w_in_from_z_layout(wz):
    cut = lambda c0, width: wz[..., c0:c0 + width]
    return jnp.concatenate(
        [cut(Z_QLAT, Q_RANK), cut(Z_KVLAT, KV_RANK), cut(Z_KPE, ROPE), cut(Z_GB, CONV_W), cut(Z_GC, CONV_W),
         cut(Z_XIN, CONV_W), cut(Z_QMEM, MEM_W), cut(Z_GATE, MIX_W)], axis=-1)


def _w_uq_pad(w_uq):
    r, _ = w_uq.shape
    w = jnp.pad(w_uq.reshape(r, MLA_HEADS, QK_HEAD), ((0, 0), (0, 0), (0, QPAD - QK_HEAD)))
    return w.reshape(r, MLA_HEADS * QPAD)


def _w_uq_unpad(w):
    r, _ = w.shape
    return w.reshape(r, MLA_HEADS, QPAD)[..., :QK_HEAD].reshape(r, MLA_HEADS * QK_HEAD)


def _rope_tables(positions):
    inv_freq = 1.0 / (ROPE_THETA ** (jnp.arange(0, ROPE, 2, dtype=F32) / ROPE))
    ang = positions.astype(F32)[:, None] * inv_freq
    cos, sin = jnp.cos(ang), jnp.sin(ang)
    s = positions.shape[0]
    zero = jnp.zeros((s, HALF_ROPE), F32)
    pad = jnp.zeros((s, LANES - ROPE), F32)
    kc = jnp.concatenate([cos, cos, pad], axis=-1)
    ka = jnp.concatenate([-sin, zero, pad], axis=-1)
    kb = jnp.concatenate([zero, sin, pad], axis=-1)
    qc = jnp.concatenate([jnp.ones((s, NOPE), F32), kc], axis=-1)
    qa = jnp.concatenate([jnp.zeros((s, NOPE), F32), ka], axis=-1)
    qb = jnp.concatenate([jnp.zeros((s, NOPE), F32), kb], axis=-1)
    return (qc, qa, qb), (kc, ka, kb)


def _layer_weights(gathered):
    return (gathered[0],) + _late_weights(gathered[1:])


def _late_weights(gathered):
    g_uq, w_ukv, g_conv, w_mkv, w_o = gathered
    return (_w_uq_pad(_cols_from_shards(g_uq)), w_ukv, _cols_from_shards(g_conv), w_mkv, w_o)


def _layer_fwd(l, x, mem, wts, gains, tabs, comm, late=None):
    wt_in = wts[0]
    g_pre, g_q, g_kv, g_mem, g_post = gains
    q_tab, k_tab = tabs
    tag = f"l{l}_"
    h = _rmsnorm_fwd(x, g_pre, 0, D_MODEL, tag + "pre_norm")
    z = _matmul(h, wt_in, "nt", F32, tag + "in_proj", tn_cap=1664)
    w_uq, w_ukv, conv_w, w_mkv, w_o = wts[1:] if late is None else late(z)
    wts = (wt_in, w_uq, w_ukv, conv_w, w_mkv, w_o)
    qn = _rmsnorm_fwd(z, g_q, Z_QLAT, Q_RANK, tag + "q_norm")
    kvn = _rmsnorm_fwd(z, g_kv, Z_KVLAT, KV_RANK, tag + "kv_norm")
    q_raw = _matmul(qn, w_uq, "nn", F32, tag + "uq")
    kv = _matmul(kvn, w_ukv, "nn", BF16, tag + "ukv")
    kpe = _rope(z, *k_tab, Z_KPE, LANES, 1, tag + "k_rope")
    a_out, a_lse, arrived = _attn_fwd(q_raw, kv, kpe, kv, q_tab, MLA_HEADS, QPAD, 0, 0, 2, 1, 2, QK_HEAD ** -0.5, 512,
                                      tag + "mla_fwd", comm)
    c_out = _conv_fwd(z, conv_w, tag + "conv_fwd")
    mem_n = _rmsnorm_fwd(mem, g_mem, 0, D_MODEL, tag + "mem_norm")
    mkv = _matmul(mem_n, w_mkv, "nn", BF16, tag + "mem_kv")
    m_out, m_lse, _ = _attn_fwd(z, mkv, None, mkv, None, MEM_HEADS, LANES, Z_QMEM // LANES, 0, 1, MEM_HEADS, 1,
                                MEM_HEAD ** -0.5, 1024, tag + "mem_fwd")
    cat = jnp.concatenate([a_out, c_out, m_out], axis=-1)
    y = _gate_fwd(cat, z, tag + "gate_fwd")
    o = _matmul(y, w_o, "nn", F32, tag + "out_proj")
    x_new = _post_norm_residual(x, o, g_post, tag + "post_norm")
    saved = (x, h, z, qn, kvn, q_raw, kv, kpe, a_lse, mem_n, mkv, m_lse, cat, y, o)
    return x_new, saved, arrived


def _layer_bwd(l, g, mem, saved, wts, gains, tabs_bwd, comm, split_exchange=False):
    wt_in, w_uq, w_ukv, conv_w, w_mkv, w_o = wts
    g_pre, g_q, g_kv, g_mem, g_post = gains
    q_tab, k_tab_bwd = tabs_bwd
    x, h, z, qn, kvn, q_raw, kv, kpe, a_lse, mem_n, mkv, m_lse, cat, y, o = saved
    tag = f"l{l}_"
    do, dg_post = _rmsnorm_bwd(o, g_post, g, None, 0, D_MODEL, BF16, tag + "post_norm_bwd")
    dy = _matmul(do, w_o, "nt", F32, tag + "out_proj_dx")
    dw_o = _matmul(y, do, "tn", BF16, tag + "out_proj_dw")
    dcat, dgate = _gate_bwd(dy, cat, z, tag + "gate_bwd")
    dq, dkv, dkpe_h, arrived = _attn_bwd(q_raw, kv, kpe, kv, cat, dcat, a_lse, q_tab, MLA_HEADS, QPAD, 0, 0, 2, 1, 2, 0,
                                         QK_HEAD ** -0.5, 512, tag + "mla_bwd", comm)
    dkpe = _kpe_grad(dkpe_h, *k_tab_bwd, MLA_HEADS, tag + "k_rope_bwd")
    dw_ukv = _matmul(kvn, dkv, "tn", BF16, tag + "ukv_dw")
    dkvn = _matmul(dkv, w_ukv, "nt", F32, tag + "ukv_dx")
    dkv_lat, dg_kv = _rmsnorm_bwd(z, g_kv, dkvn, None, Z_KVLAT, KV_RANK, BF16, tag + "kv_norm_bwd")
    dw_uq = _matmul(qn, dq, "tn", BF16, tag + "uq_dw")
    dqn = _matmul(dq, w_uq, "nt", F32, tag + "uq_dx")
    dq_lat, dg_q = _rmsnorm_bwd(z, g_q, dqn, None, Z_QLAT, Q_RANK, BF16, tag + "q_norm_bwd")
    dgb, dgc, dxin, dconv_w = _conv_bwd(z, conv_w, dcat, tag + "conv_bwd")
    dq_mem, dmk, dmv, _ = _attn_bwd(z, mkv, None, mkv, cat, dcat, m_lse, None, MEM_HEADS, LANES, Z_QMEM // LANES, 0, 1,
                                    MEM_HEADS, 1, (MLA_W + CONV_W) // LANES, MEM_HEAD ** -0.5, 1024, tag + "mem_bwd")
    dmkv = jnp.concatenate([dmk, dmv], axis=-1)
    dw_mkv = _matmul(mem_n, dmkv, "tn", BF16, tag + "mem_kv_dw")
    dmem_n = _matmul(dmkv, w_mkv, "nt", F32, tag + "mem_kv_dx")
    _, dg_mem = _rmsnorm_bwd(mem, g_mem, dmem_n, None, 0, D_MODEL, BF16, tag + "mem_norm_bwd")
    others = (_cols_to_shards(_w_uq_unpad(dw_uq)), dw_ukv, _cols_to_shards(dconv_w), dw_mkv, dw_o)
    early = None
    if split_exchange:
        early_plan = _scatter_plan(None, *others, part="rest")
        early = (early_plan, _comm_start(early_plan, dmem_n, tag + "exchange_rest_start"))
        g_pre = g_pre + early[1][2][0:1, 0:1]
    dz = jnp.concatenate([dgate, dq_lat, dkv_lat, dkpe, dgb, dgc, dxin, dq_mem], axis=-1)
    dwt_in = _matmul(dz, h, "tn", BF16, tag + "in_proj_dw", tm_cap=1664, tk_cap=1024)
    dh = _matmul(dz, wt_in, "nn", F32, tag + "in_proj_dx", tk_cap=1664)
    dx, dg_pre = _rmsnorm_bwd(x, g_pre, dh, g, 0, D_MODEL, F32, tag + "pre_norm_bwd")
    contrib = _scatter_plan(dwt_in, *others, part="in" if split_exchange else "all")
    return dx, contrib, (dg_pre, dg_q, dg_kv, dg_mem, dg_post), early


GAIN_WIDTHS = (D_MODEL, Q_RANK, KV_RANK, D_MODEL, D_MODEL)


def _pack_gains(parts):
    return jnp.concatenate([p.reshape(-1) for p in parts]).reshape(-1, LANES)


def _unpack_gains(packed, depth):
    flat = packed.reshape(-1)
    out, at = [], 0
    for width in GAIN_WIDTHS:
        out.append(flat[at:at + depth * width].reshape(depth, width))
        at += depth * width
    return out


def kernel(x, mem, positions, pre_norm_g, w_in, q_norm_g, w_uq, kv_norm_g, w_ukv, conv_w, mem_norm_g, w_mk, w_mv, w_o, post_norm_g, loss_target, m_pre_norm_g, m_w_in, m_q_norm_g, m_w_uq, m_kv_norm_g, m_w_ukv, m_conv_w, m_mem_norm_g, m_w_mk, m_w_mv, m_w_o, m_post_norm_g, v_pre_norm_g, v_w_in, v_q_norm_g, v_w_uq, v_kv_norm_g, v_w_ukv, v_conv_w, v_mem_norm_g, v_w_mk, v_w_mv, v_w_o, v_post_norm_g):
    depth = w_in.shape[0]
    x0, mem0, target = x[0], mem[0], loss_target[0]
    tabs = _rope_tables(positions[0])
    tabs_bwd = (tabs[0], (tabs[1][0], -tabs[1][1], -tabs[1][2]))

    flip = lambda t: jnp.transpose(t, (0, 2, 1))
    w_in, m_w_in, v_w_in = flip(w_in), flip(m_w_in), flip(v_w_in)
    shards = [w_in.astype(BF16), w_uq.astype(BF16), w_ukv.astype(BF16), conv_w, w_mk.astype(BF16), w_mv.astype(BF16),
              w_o.astype(BF16)]
    zero_rows = lambda: jnp.zeros((LANES - ROPE, D_MODEL), BF16)

    def layer_gains(l):
        return tuple(g[l][None, :] for g in (pre_norm_g, q_norm_g, kv_norm_g, mem_norm_g, post_norm_g))

    wts, saved = [None] * depth, [None] * depth
    first = [s[0:1] for s in shards]
    plan_in, plan_rest = _gather_plan(0, first, zero_rows(), "in"), _gather_plan(0, first, zero_rows(), "rest")
    started_in = _comm_start(plan_in, positions, "l0_gather_in_start")
    started_rest = _comm_start(plan_rest, started_in[2], "l0_gather_rest_start")
    wts[0] = tuple(_comm_finish(plan_in, started_in, started_rest[2], "l0_gather_in_wait"))

    next_gather = {}

    def start_next_gather(l, after):
        plan = _gather_plan(0, [s[l + 1:l + 2] for s in shards], zero_rows())
        next_gather[l + 1] = (plan, _comm_start(plan, after, f"l{l + 1}_gather_start"))
        return next_gather[l + 1][1][2][0:1, 0:1]

    def rest_of_layer0(z):
        got = _late_weights(_comm_finish(plan_rest, started_rest, z, "l0_gather_rest_wait"))
        wts[0] = wts[0] + got
        if depth > 1:
            got = got[:2] + (got[2] + start_next_gather(0, got[4]),) + got[3:]
        return got

    act = x0
    for l in range(depth):
        gains = layer_gains(l)
        if 0 < l < depth - 1:
            gains = (gains[0] + start_next_gather(l, wts[l][5]),) + gains[1:]
        act, saved[l], _ = _layer_fwd(l, act, mem0, wts[l], gains, tabs, None, rest_of_layer0 if l == 0 else None)
        if l + 1 < depth:
            plan, started = next_gather[l + 1]
            wts[l + 1] = _layer_weights(_comm_finish(plan, started, act, f"l{l + 1}_gather_wait"))
    grad, loss_part = _loss_head(act, target, "loss_head")
    loss = lax.psum(loss_part[0, 0], ("x", "y", "c"))

    names = ("w_in", "w_uq", "w_ukv", "conv_w", "w_mk", "w_mv", "w_o")
    w_shards = (w_in, w_uq, w_ukv, conv_w, w_mk, w_mv, w_o)
    m_shards = (m_w_in, m_w_uq, m_w_ukv, m_conv_w, m_w_mk, m_w_mv, m_w_o)
    v_shards = (v_w_in, v_w_uq, v_w_ukv, v_conv_w, v_w_mk, v_w_mv, v_w_o)
    stacked = [None] * len(names)

    def finish(l, received):
        partial = [_sum_slots(r, f"l{l}_grad_sum_{names[i]}") for i, r in enumerate(received)]
        other = _sibling_exchange(partial, f"l{l}_grad_sibling")
        for i, name in enumerate(names):
            stacked[i] = _adamw_layer(l, w_shards[i], partial[i], other[i], m_shards[i], v_shards[i], stacked[i],
                                      f"l{l}_adamw_{name}")

    dgs = [None] * depth
    pending = None
    for l in reversed(range(depth)):
        gains = layer_gains(l)
        if pending is not None:
            gains = gains[:4] + (gains[4] + pending[1][2][0:1, 0:1],)
        grad, contrib, dgs[l], early = _layer_bwd(l, grad, mem0, saved[l], wts[l], gains, tabs_bwd, None, l == 0)
        if pending is not None:
            finish(l + 1, _comm_finish(pending[0], pending[1], grad, f"l{l + 1}_exchange_wait"))
        if l > 0:
            pending = (contrib, _comm_start(contrib, grad, f"l{l}_exchange_start"))
    got_in = _comm_call(contrib, "l0_grad_exchange_in")
    finish(0, got_in + _comm_finish(early[0], early[1], got_in[0], "l0_exchange_rest_wait"))
    grad_x = grad[None]
    results = {name: tuple(stacked[i]) for i, name in enumerate(names)}
    results["w_in"] = tuple(flip(t) for t in results["w_in"])

    gain_names = ("pre_norm_g", "q_norm_g", "kv_norm_g", "mem_norm_g", "post_norm_g")
    dg_packed = _pack_gains([jnp.concatenate([dgs[l][i] for l in range(depth)], axis=0) for i in range(5)])
    dg_total = _sum_slots(_gather_all(dg_packed, "gain_gather"), "gain_sum")
    gain_outs = (dg_total,) + tuple(_adamw(
        _pack_gains((pre_norm_g, q_norm_g, kv_norm_g, mem_norm_g, post_norm_g)), dg_total,
        _pack_gains((m_pre_norm_g, m_q_norm_g, m_kv_norm_g, m_mem_norm_g, m_post_norm_g)),
        _pack_gains((v_pre_norm_g, v_q_norm_g, v_kv_norm_g, v_mem_norm_g, v_post_norm_g)), "adamw_gains"))
    gain_outs = [_unpack_gains(t, depth) for t in gain_outs]
    for i, name in enumerate(gain_names):
        results[name] = tuple(gain_outs[k][i] for k in range(4))

    order = ("pre_norm_g", "w_in", "q_norm_g", "w_uq", "kv_norm_g", "w_ukv", "conv_w", "mem_norm_g", "w_mk", "w_mv", "w_o",
             "post_norm_g")
    out = [loss, grad_x]
    for k in range(4):
        out += [results[name][k] for name in order]
    return tuple(out)
```

```python
import functools

import jax
import jax.numpy as jnp
from jax import lax
from jax.experimental import pallas as pl
from jax.experimental.pallas import tpu as pltpu

F32 = jnp.float32
BF16 = jnp.bfloat16
MESH_ID = pl.DeviceIdType.MESH

D_MODEL = 2048
EPS = 1e-6
LOG2_E = 1.4426950408889634
ROPE_THETA = 10000.0
MLA_HEADS = 8
NOPE = 128
ROPE = 64
HALF_ROPE = ROPE // 2
QK_HEAD = NOPE + ROPE
V_HEAD = 128
Q_RANK = 512
KV_RANK = 256
CONV_W = 512
MEM_HEADS = 4
MEM_HEAD = 128
MEM_W = MEM_HEADS * MEM_HEAD
MLA_W = MLA_HEADS * V_HEAD
MIX_W = MLA_W + CONV_W + MEM_W
IN_COLS = Q_RANK + KV_RANK + ROPE + 3 * CONV_W + MEM_W + MIX_W
N_CHIPS = 4
N_DEV = 8

LANES = 128
VMEM_LIMIT_BYTES = 56 * 1024 * 1024

QPAD = 2 * LANES
Z_GATE = 0
Z_QLAT = Z_GATE + MIX_W
Z_KVLAT = Z_QLAT + Q_RANK
Z_KPE = Z_KVLAT + KV_RANK
Z_GB = Z_KPE + LANES
Z_GC = Z_GB + CONV_W
Z_XIN = Z_GC + CONV_W
Z_QMEM = Z_XIN + CONV_W
Z_COLS = Z_QMEM + MEM_W

ADAM_LR = 0.001
ADAM_B1 = 0.9
ADAM_B2 = 0.999
ADAM_EPS = 1e-08
ADAM_WD = 0.01
ADAM_STEP = 10


def _tile(dim, cap, unit):
    if dim <= cap:
        return dim
    t = (cap // unit) * unit
    while t >= unit:
        if dim % t == 0:
            return t
        t -= unit
    raise ValueError(f"no tile of {dim} under {cap} in units of {unit}")


def _params(*semantics):
    return pltpu.CompilerParams(dimension_semantics=semantics, vmem_limit_bytes=VMEM_LIMIT_BYTES)


def _matmul(a, b, mode, out_dtype, name, tm_cap=512, tn_cap=1024, tk_cap=2048):
    if mode == "nn":
        (m, k), (k2, n) = a.shape, b.shape
    elif mode == "nt":
        (m, k), (n, k2) = a.shape, b.shape
    else:
        (k, m), (k2, n) = a.shape, b.shape
    assert k == k2, (a.shape, b.shape, mode)
    tm = _tile(m, tm_cap, LANES if mode == "tn" else 16)
    tn = _tile(n, tn_cap, LANES)
    tk = _tile(k, tk_cap, LANES if mode != "tn" else 16)
    nk = k // tk
    if mode == "nn":
        a_spec = pl.BlockSpec((tm, tk), lambda i, j, kk: (i, kk))
        b_spec = pl.BlockSpec((tk, tn), lambda i, j, kk: (kk, j))
        dims = (((1,), (0,)), ((), ()))
    elif mode == "nt":
        a_spec = pl.BlockSpec((tm, tk), lambda i, j, kk: (i, kk))
        b_spec = pl.BlockSpec((tn, tk), lambda i, j, kk: (j, kk))
        dims = (((1,), (1,)), ((), ()))
    else:
        a_spec = pl.BlockSpec((tk, tm), lambda i, j, kk: (kk, i))
        b_spec = pl.BlockSpec((tk, tn), lambda i, j, kk: (kk, j))
        dims = (((0,), (0,)), ((), ()))

    def body(a_ref, b_ref, o_ref, *scratch):
        part = lax.dot_general(a_ref[...].astype(BF16), b_ref[...].astype(BF16), dims, preferred_element_type=F32)
        if nk == 1:
            o_ref[...] = part.astype(o_ref.dtype)
            return
        (acc_ref,) = scratch
        kk = pl.program_id(2)

        @pl.when(kk == 0)
        def _():
            acc_ref[...] = part

        @pl.when(kk > 0)
        def _():
            acc_ref[...] += part

        @pl.when(kk == nk - 1)
        def _():
            o_ref[...] = acc_ref[...].astype(o_ref.dtype)

    return pl.pallas_call(
        body,
        grid=(m // tm, n // tn, nk),
        in_specs=[a_spec, b_spec],
        out_specs=pl.BlockSpec((tm, tn), lambda i, j, kk: (i, j)),
        out_shape=jax.ShapeDtypeStruct((m, n), out_dtype),
        scratch_shapes=[] if nk == 1 else [pltpu.VMEM((tm, tn), F32)],
        compiler_params=_params("parallel", "parallel", "arbitrary"),
        name=name,
    )(a, b)


def _rmsnorm_fwd(x, gain, col0, width, name):
    rows = x.shape[0]
    tr = _tile(rows, 512, 16)
    cb = col0 // width
    assert cb * width == col0

    def body(x_ref, g_ref, o_ref):
        xv = x_ref[...]
        r = lax.rsqrt(jnp.mean(xv * xv, axis=-1, keepdims=True) + EPS)
        o_ref[...] = (xv * r * g_ref[...]).astype(o_ref.dtype)

    return pl.pallas_call(
        body,
        grid=(rows // tr,),
        in_specs=[pl.BlockSpec((tr, width), lambda i: (i, cb)), pl.BlockSpec((1, width), lambda i: (0, 0))],
        out_specs=pl.BlockSpec((tr, width), lambda i: (i, 0)),
        out_shape=jax.ShapeDtypeStruct((rows, width), BF16),
        compiler_params=_params("parallel"),
        name=name,
    )(x, gain)


def _rmsnorm_bwd(x, gain, dy, resid, col0, width, out_dtype, name):
    rows = x.shape[0]
    tr = _tile(rows, 256, 16)
    cb = col0 // width
    assert cb * width == col0
    has_resid = resid is not None

    def body(*refs):
        if has_resid:
            x_ref, g_ref, dy_ref, res_ref, dx_ref, dg_ref = refs
        else:
            x_ref, g_ref, dy_ref, dx_ref, dg_ref = refs
        i = pl.program_id(0)
        xv = x_ref[...]
        dyv = dy_ref[...].astype(F32)
        r = lax.rsqrt(jnp.mean(xv * xv, axis=-1, keepdims=True) + EPS)
        xr = xv * r
        dyg = dyv * g_ref[...]
        c = jnp.mean(dyg * xr, axis=-1, keepdims=True)
        dx = r * (dyg - xr * c)
        if has_resid:
            dx = dx + res_ref[...]
        dx_ref[...] = dx.astype(dx_ref.dtype)
        part = jnp.sum(dyv * xr, axis=0, keepdims=True)

        @pl.when(i == 0)
        def _():
            dg_ref[...] = part

        @pl.when(i > 0)
        def _():
            dg_ref[...] += part

    row_spec = pl.BlockSpec((tr, width), lambda i: (i, 0))
    in_specs = [pl.BlockSpec((tr, width), lambda i: (i, cb)), pl.BlockSpec((1, width), lambda i: (0, 0)), row_spec]
    args = [x, gain, dy]
    if has_resid:
        in_specs.append(row_spec)
        args.append(resid)
    return pl.pallas_call(
        body,
        grid=(rows // tr,),
        in_specs=in_specs,
        out_specs=[row_spec, pl.BlockSpec((1, width), lambda i: (0, 0))],
        out_shape=[jax.ShapeDtypeStruct((rows, width), out_dtype), jax.ShapeDtypeStruct((1, width), F32)],
        compiler_params=_params("arbitrary"),
        name=name,
    )(*args)


def _post_norm_residual(x, o, gain, name):
    rows, width = x.shape
    tr = _tile(rows, 256, 8)

    def body(x_ref, o_ref, g_ref, out_ref):
        ov = o_ref[...]
        r = lax.rsqrt(jnp.mean(ov * ov, axis=-1, keepdims=True) + EPS)
        out_ref[...] = x_ref[...] + ov * r * g_ref[...]

    row_spec = pl.BlockSpec((tr, width), lambda i: (i, 0))
    return pl.pallas_call(
        body,
        grid=(rows // tr,),
        in_specs=[row_spec, row_spec, pl.BlockSpec((1, width), lambda i: (0, 0))],
        out_specs=row_spec,
        out_shape=jax.ShapeDtypeStruct((rows, width), F32),
        compiler_params=_params("parallel"),
        name=name,
    )(x, o, gain)


def _rope(x, tab_c, tab_a, tab_b, col0, width, heads, name):
    rows = x.shape[0]
    tr = _tile(rows, 512, 16)
    cb = col0 // width
    assert cb * width == col0

    def body(x_ref, c_ref, a_ref, b_ref, o_ref):
        xv = x_ref[...].astype(F32)
        up = pltpu.roll(xv, width - HALF_ROPE, 1)
        down = pltpu.roll(xv, HALF_ROPE, 1)
        o_ref[...] = (xv * c_ref[...] + up * a_ref[...] + down * b_ref[...]).astype(o_ref.dtype)

    tab_spec = pl.BlockSpec((tr, width), lambda i, h: (i, 0))
    return pl.pallas_call(
        body,
        grid=(rows // tr, heads),
        in_specs=[pl.BlockSpec((tr, width), lambda i, h: (i, cb + h)), tab_spec, tab_spec, tab_spec],
        out_specs=pl.BlockSpec((tr, width), lambda i, h: (i, h)),
        out_shape=jax.ShapeDtypeStruct((rows, heads * width), BF16),
        compiler_params=_params("parallel", "parallel"),
        name=name,
    )(x, tab_c, tab_a, tab_b)


def _kpe_grad(dkb, tab_c, tab_a, tab_b, heads, name):
    rows = dkb.shape[0]
    tr = _tile(rows, 512, 16)

    def body(d_ref, c_ref, a_ref, b_ref, o_ref):
        acc = d_ref[:, 0:LANES]
        for h in range(1, heads):
            acc = acc + d_ref[:, h * LANES:(h + 1) * LANES]
        up = pltpu.roll(acc, LANES - HALF_ROPE, 1)
        down = pltpu.roll(acc, HALF_ROPE, 1)
        o_ref[...] = (acc * c_ref[...] + up * a_ref[...] + down * b_ref[...]).astype(o_ref.dtype)

    tab_spec = pl.BlockSpec((tr, LANES), lambda i: (i, 0))
    return pl.pallas_call(
        body,
        grid=(rows // tr,),
        in_specs=[pl.BlockSpec((tr, heads * LANES), lambda i: (i, 0)), tab_spec, tab_spec, tab_spec],
        out_specs=tab_spec,
        out_shape=jax.ShapeDtypeStruct((rows, LANES), BF16),
        compiler_params=_params("parallel"),
        name=name,
    )(dkb, tab_c, tab_a, tab_b)


class _CommPlan:
    def __init__(self, ins, out_shape, build, n_copies):
        self.ins, self.out_shape, self.build, self.n_copies = list(ins), list(out_shape), build, n_copies

    def scratch(self):
        n = self.n_copies
        return [pltpu.SemaphoreType.DMA((n,)), pltpu.SemaphoreType.DMA((n,)), pltpu.SemaphoreType.DMA((n,))]


def _split_comm(refs, n_in, n_out, comm):
    if comm is None:
        return refs, None
    ci, co = len(comm.ins), len(comm.out_shape)
    ins, c_ins = refs[:n_in], refs[n_in:n_in + ci]
    outs, c_outs = refs[n_in + ci:n_in + ci + n_out], refs[n_in + ci + n_out:n_in + ci + n_out + co]
    rest = refs[n_in + ci + n_out + co:]
    scratch, sems = rest[:-3], rest[-3:]
    return tuple(ins) + tuple(outs) + tuple(scratch), functools.partial(comm.build, c_ins, c_outs, sems)


def _ride_start(copies, first):
    if copies is not None:
        pl.when(first)(copies()[0])


def _ride_wait(copies, last):
    if copies is not None:
        pl.when(last)(copies()[1])


def _rope_rows(x, c, a, b, sign):
    width = x.shape[-1]
    mixed = pltpu.roll(x, width - HALF_ROPE, 1) * a + pltpu.roll(x, HALF_ROPE, 1) * b
    return x * c + mixed if sign > 0 else x * c - mixed


def _attn_fwd(q, ka, kb, v, rope, heads, q_w, q_cb, ka_cb, ka_step, v_cb, v_step, scale, tq_cap, name, comm=None, tk_cap=512):
    s_q, s_k = q.shape[0], ka.shape[0]
    tq = _tile(s_q, tq_cap, 16)
    nq = s_q // tq
    has_kb = kb is not None
    n_in = 7 if has_kb else 3
    tk = _tile(s_k, tk_cap, LANES)

    def body(*refs):
        refs, copies = _split_comm(refs, n_in, 2, comm)
        first = jnp.logical_and(pl.program_id(0) == 0, pl.program_id(1) == 0)
        last = jnp.logical_and(pl.program_id(0) == heads - 1, pl.program_id(1) == nq - 1)
        _ride_start(copies, first)
        if has_kb:
            q_ref, ka_ref, kb_ref, v_ref, c_ref, a_ref, b_ref, o_ref, lse_ref, k_scr = refs

            @pl.when(pl.program_id(1) == 0)
            def _():
                k_scr[:, 0:LANES] = ka_ref[...].astype(BF16)
                k_scr[:, LANES:2 * LANES] = kb_ref[...].astype(BF16)

            keys = k_scr
            qv = _rope_rows(q_ref[...], c_ref[...], a_ref[...], b_ref[...], 1).astype(BF16)
        else:
            q_ref, ka_ref, v_ref, o_ref, lse_ref = refs
            keys = ka_ref
            qv = q_ref[...].astype(BF16)
        c2 = scale * LOG2_E
        m = l = o = None
        nk = s_k // tk
        scores = lambda j: lax.dot_general(qv, keys[j * tk:(j + 1) * tk, :].astype(BF16), (((1,), (1,)), ((), ())),
                                           preferred_element_type=F32)
        s_next = scores(0)
        for j in range(nk):
            sj = s_next
            if j + 1 < nk:
                s_next = scores(j + 1)
            mj = jnp.max(sj, axis=-1, keepdims=True)
            m_new = mj if m is None else jnp.maximum(m, mj)
            pj = jnp.exp2((sj - m_new) * c2)
            lj = jnp.sum(pj, axis=-1, keepdims=True)
            oj = jnp.dot(pj.astype(BF16), v_ref[j * tk:(j + 1) * tk, :].astype(BF16), preferred_element_type=F32)
            if m is None:
                l, o = lj, oj
            else:
                alpha = jnp.exp2((m - m_new) * c2)
                l, o = l * alpha + lj, o * alpha + oj
            m = m_new
        o_ref[...] = (o * (1.0 / l)).astype(o_ref.dtype)
        lse_ref[...] = jnp.broadcast_to(m * c2 + jnp.log2(l), lse_ref.shape)
        _ride_wait(copies, last)

    in_specs = [pl.BlockSpec((tq, q_w), lambda h, i: (i, q_cb + h)),
                pl.BlockSpec((s_k, LANES), lambda h, i: (0, ka_cb + ka_step * h))]
    args = [q, ka]
    if has_kb:
        in_specs.append(pl.BlockSpec((s_k, LANES), lambda h, i: (0, 0)))
        args.append(kb)
    in_specs.append(pl.BlockSpec((s_k, LANES), lambda h, i: (0, v_cb + v_step * h)))
    args.append(v)
    if has_kb:
        in_specs += [pl.BlockSpec((tq, q_w), lambda h, i: (i, 0))] * 3
        args += list(rope)
    out_spec = pl.BlockSpec((tq, LANES), lambda h, i: (i, h))
    out_specs = [out_spec, out_spec]
    out_shape = [jax.ShapeDtypeStruct((s_q, heads * LANES), BF16), jax.ShapeDtypeStruct((s_q, heads * LANES), F32)]
    scratch = [pltpu.VMEM((s_k, 2 * LANES), BF16)] if has_kb else []
    if comm is not None:
        in_specs += [ANY] * len(comm.ins)
        args += comm.ins
        out_specs += [ANY] * len(comm.out_shape)
        out_shape += comm.out_shape
        scratch += comm.scratch()
    res = pl.pallas_call(
        body,
        grid=(heads, nq),
        in_specs=in_specs,
        out_specs=out_specs,
        out_shape=out_shape,
        scratch_shapes=scratch,
        compiler_params=_params("arbitrary", "arbitrary"),
        name=name,
    )(*args)
    return res[0], res[1], list(res[2:])


def _attn_bwd(q, ka, kb, v, o, do, lse, rope, heads, q_w, q_cb, ka_cb, ka_step, v_cb, v_step, o_cb, scale, tq_cap, name,
              comm=None, tk_cap=512):
    s_q, s_k = q.shape[0], ka.shape[0]
    tq = _tile(s_q, tq_cap, 16)
    nq = s_q // tq
    has_kb = kb is not None
    n_in = 10 if has_kb else 6
    n_out = 3
    tk = _tile(s_k, tk_cap, LANES)

    def body(*refs):
        refs, copies = _split_comm(refs, n_in, n_out, comm)
        first = jnp.logical_and(pl.program_id(0) == 0, pl.program_id(1) == 0)
        last = jnp.logical_and(pl.program_id(0) == heads - 1, pl.program_id(1) == nq - 1)
        _ride_start(copies, first)
        if has_kb:
            (q_ref, ka_ref, kb_ref, v_ref, o_ref, do_ref, lse_ref, c_ref, a_ref, b_ref, dq_ref, dkv_ref, dkb_ref, k_scr, dk_acc,
             dv_acc) = refs
        else:
            q_ref, ka_ref, v_ref, o_ref, do_ref, lse_ref, dq_ref, dka_ref, dv_ref, dk_acc, dv_acc = refs
        i = pl.program_id(1)

        @pl.when(i == 0)
        def _():
            dk_acc[...] = jnp.zeros_like(dk_acc)
            dv_acc[...] = jnp.zeros_like(dv_acc)
            if has_kb:
                k_scr[:, 0:LANES] = ka_ref[...].astype(BF16)
                k_scr[:, LANES:2 * LANES] = kb_ref[...].astype(BF16)

        keys = k_scr if has_kb else ka_ref
        if has_kb:
            qv = _rope_rows(q_ref[...], c_ref[...], a_ref[...], b_ref[...], 1).astype(BF16)
        else:
            qv = q_ref[...].astype(BF16)
        dov = do_ref[...].astype(BF16)
        delta = jnp.sum(dov.astype(F32) * o_ref[...].astype(F32), axis=-1, keepdims=True)
        lse2 = lse_ref[:, 0:1]
        c2 = scale * LOG2_E
        nk = s_k // tk
        rows = lambda j: slice(j * tk, (j + 1) * tk)
        nt = (((1,), (1,)), ((), ()))
        tn = (((0,), (0,)), ((), ()))

        def scores(j):
            return (lax.dot_general(qv, keys[rows(j), :].astype(BF16), nt, preferred_element_type=F32),
                    lax.dot_general(dov, v_ref[rows(j), :].astype(BF16), nt, preferred_element_type=F32))

        nxt = scores(0)
        dq = None
        for j in range(nk):
            sj, dpj = nxt
            if j + 1 < nk:
                nxt = scores(j + 1)
            pj = jnp.exp2(sj * c2 - lse2)
            dsj = (pj * (dpj - delta)).astype(BF16)
            dqj = jnp.dot(dsj, keys[rows(j), :].astype(BF16), preferred_element_type=F32)
            dq = dqj if dq is None else dq + dqj
            dk_acc[rows(j), :] += lax.dot_general(dsj, qv, tn, preferred_element_type=F32)
            dv_acc[rows(j), :] += lax.dot_general(pj.astype(BF16), dov, tn, preferred_element_type=F32)
        dq = dq * scale
        if has_kb:
            dq = _rope_rows(dq, c_ref[...], a_ref[...], b_ref[...], -1)
        dq_ref[...] = dq.astype(dq_ref.dtype)

        @pl.when(i == nq - 1)
        def _():
            if has_kb:
                dkv_ref[:, 0:LANES] = (dk_acc[:, 0:LANES] * scale).astype(dkv_ref.dtype)
                dkv_ref[:, LANES:2 * LANES] = dv_acc[...].astype(dkv_ref.dtype)
                dkb_ref[...] = dk_acc[:, LANES:2 * LANES] * scale
            else:
                dka_ref[...] = (dk_acc[...] * scale).astype(dka_ref.dtype)
                dv_ref[...] = dv_acc[...].astype(dv_ref.dtype)

        _ride_wait(copies, last)

    key_spec = lambda cb, step: pl.BlockSpec((s_k, LANES), lambda h, i: (0, cb + step * h))
    row_spec = lambda cb: pl.BlockSpec((tq, LANES), lambda h, i: (i, cb + h))
    in_specs = [pl.BlockSpec((tq, q_w), lambda h, i: (i, q_cb + h)), key_spec(ka_cb, ka_step)]
    args = [q, ka]
    if has_kb:
        in_specs.append(pl.BlockSpec((s_k, LANES), lambda h, i: (0, 0)))
        args.append(kb)
    in_specs += [key_spec(v_cb, v_step), row_spec(o_cb), row_spec(o_cb), row_spec(0)]
    args += [v, o, do, lse]
    if has_kb:
        in_specs += [pl.BlockSpec((tq, q_w), lambda h, i: (i, 0))] * 3
        args += list(rope)
    out_specs = [pl.BlockSpec((tq, q_w), lambda h, i: (i, h))]
    out_shape = [jax.ShapeDtypeStruct((s_q, heads * q_w), BF16)]
    scratch = []
    if has_kb:
        out_specs += [pl.BlockSpec((s_k, 2 * LANES), lambda h, i: (0, h)), key_spec(0, 1)]
        out_shape += [jax.ShapeDtypeStruct((s_k, heads * 2 * LANES), BF16), jax.ShapeDtypeStruct((s_k, heads * LANES), F32)]
        scratch.append(pltpu.VMEM((s_k, 2 * LANES), BF16))
    else:
        out_specs += [key_spec(0, 1), key_spec(0, 1)]
        out_shape += [jax.ShapeDtypeStruct((s_k, heads * LANES), BF16)] * 2
    scratch += [pltpu.VMEM((s_k, q_w), F32), pltpu.VMEM((s_k, LANES), F32)]
    if comm is not None:
        in_specs += [ANY] * len(comm.ins)
        args += comm.ins
        out_specs += [ANY] * len(comm.out_shape)
        out_shape += comm.out_shape
        scratch += comm.scratch()
    res = pl.pallas_call(
        body,
        grid=(heads, nq),
        in_specs=in_specs,
        out_specs=out_specs,
        out_shape=out_shape,
        scratch_shapes=scratch,
        compiler_params=_params("arbitrary", "arbitrary"),
        name=name,
    )(*args)
    return res[0], res[1], res[2], list(res[3:])


def _shift_rows(u, rows):
    t = lax.broadcasted_iota(jnp.int32, u.shape, 0)
    prev = jnp.where(t == 0, 0.0, pltpu.roll(u, 1, 0))
    nxt = jnp.where(t == rows - 1, 0.0, pltpu.roll(u, rows - 1, 0))
    return prev, nxt


def _conv_fwd(z, conv_w, name):
    rows = z.shape[0]
    nblk = CONV_W // LANES

    def body(gb_ref, gc_ref, xin_ref, w_ref, o_ref):
        u = gc_ref[...] * xin_ref[...]
        prev, nxt = _shift_rows(u, rows)
        conv = prev * w_ref[0:1, :] + u * w_ref[1:2, :] + nxt * w_ref[2:3, :]
        o_ref[...] = (gb_ref[...] * conv).astype(o_ref.dtype)

    col = lambda c0: pl.BlockSpec((rows, LANES), lambda j: (0, c0 // LANES + j))
    return pl.pallas_call(
        body,
        grid=(nblk,),
        in_specs=[col(Z_GB), col(Z_GC), col(Z_XIN), pl.BlockSpec((3, LANES), lambda j: (0, j))],
        out_specs=col(0),
        out_shape=jax.ShapeDtypeStruct((rows, CONV_W), BF16),
        compiler_params=_params("parallel"),
        name=name,
    )(z, z, z, conv_w)


def _conv_bwd(z, conv_w, dcat, name):
    rows = z.shape[0]
    nblk = CONV_W // LANES

    def body(gb_ref, gc_ref, xin_ref, w_ref, dc_ref, dgb_ref, dgc_ref, dxin_ref, dw_ref):
        gc = gc_ref[...]
        xin = xin_ref[...]
        dc = dc_ref[...].astype(F32)
        u = gc * xin
        prev, nxt = _shift_rows(u, rows)
        w0, w1, w2 = w_ref[0:1, :], w_ref[1:2, :], w_ref[2:3, :]
        conv = prev * w0 + u * w1 + nxt * w2
        dgb_ref[...] = (dc * conv).astype(dgb_ref.dtype)
        dconv = dc * gb_ref[...]
        dw_ref[0:1, :] = jnp.sum(dconv * prev, axis=0, keepdims=True)
        dw_ref[1:2, :] = jnp.sum(dconv * u, axis=0, keepdims=True)
        dw_ref[2:3, :] = jnp.sum(dconv * nxt, axis=0, keepdims=True)
        dprev, dnxt = _shift_rows(dconv, rows)
        du = dnxt * w0 + dconv * w1 + dprev * w2
        dgc_ref[...] = (du * xin).astype(dgc_ref.dtype)
        dxin_ref[...] = (du * gc).astype(dxin_ref.dtype)

    col = lambda c0: pl.BlockSpec((rows, LANES), lambda j: (0, c0 // LANES + j))
    w_spec = pl.BlockSpec((3, LANES), lambda j: (0, j))
    piece = jax.ShapeDtypeStruct((rows, CONV_W), BF16)
    return pl.pallas_call(
        body,
        grid=(nblk,),
        in_specs=[col(Z_GB), col(Z_GC), col(Z_XIN), w_spec, col(MLA_W)],
        out_specs=[col(0), col(0), col(0), w_spec],
        out_shape=[piece, piece, piece, jax.ShapeDtypeStruct((3, CONV_W), F32)],
        compiler_params=_params("parallel"),
        name=name,
    )(z, z, z, conv_w, dcat)


def _gate_fwd(cat, z, name):
    rows = cat.shape[0]
    tr = _tile(rows, 512, 16)
    tc = 512
    g0 = Z_GATE // tc

    def body(c_ref, g_ref, y_ref):
        g = g_ref[...]
        y_ref[...] = (c_ref[...].astype(F32) * (g * jax.nn.sigmoid(g))).astype(y_ref.dtype)

    blk = pl.BlockSpec((tr, tc), lambda i, j: (i, j))
    return pl.pallas_call(
        body,
        grid=(rows // tr, MIX_W // tc),
        in_specs=[blk, pl.BlockSpec((tr, tc), lambda i, j: (i, g0 + j))],
        out_specs=blk,
        out_shape=jax.ShapeDtypeStruct((rows, MIX_W), BF16),
        compiler_params=_params("parallel", "parallel"),
        name=name,
    )(cat, z)


def _out_proj_dx_gate_bwd(do, w_o, cat, z, name):
    rows, k = do.shape
    tm = _tile(rows, 512, 16)
    tn = _tile(MIX_W, 1024, LANES)
    g0 = Z_GATE // tn

    def body(do_ref, w_ref, c_ref, g_ref, dcat_ref, dgate_ref):
        dy = lax.dot_general(do_ref[...], w_ref[...], (((1,), (1,)), ((), ())), preferred_element_type=F32)
        g = g_ref[...]
        sg = jax.nn.sigmoid(g)
        dcat_ref[...] = (dy * (g * sg)).astype(dcat_ref.dtype)
        dgate_ref[...] = (dy * c_ref[...].astype(F32) * (sg * (1.0 + g * (1.0 - sg)))).astype(dgate_ref.dtype)

    blk = pl.BlockSpec((tm, tn), lambda i, j: (i, j))
    out = jax.ShapeDtypeStruct((rows, MIX_W), BF16)
    return pl.pallas_call(
        body,
        grid=(rows // tm, MIX_W // tn),
        in_specs=[pl.BlockSpec((tm, k), lambda i, j: (i, 0)), pl.BlockSpec((tn, k), lambda i, j: (j, 0)), blk,
                  pl.BlockSpec((tm, tn), lambda i, j: (i, g0 + j))],
        out_specs=[blk, blk],
        out_shape=[out, out],
        compiler_params=_params("parallel", "parallel"),
        name=name,
    )(do, w_o, cat, z)


def _in_proj_dx_norm_bwd(dz, wt_in, x, gain, resid, name):
    rows, k = dz.shape
    width = wt_in.shape[1]
    tm = _tile(rows, 512, 16)
    tk = _tile(k, 832, LANES)
    nk = k // tk

    def body(dz_ref, w_ref, x_ref, g_ref, res_ref, dx_ref, dg_ref, acc_ref):
        i, kk = pl.program_id(0), pl.program_id(1)
        part = jnp.dot(dz_ref[...], w_ref[...], preferred_element_type=F32)

        @pl.when(kk == 0)
        def _():
            acc_ref[...] = part

        @pl.when(kk > 0)
        def _():
            acc_ref[...] += part

        @pl.when(kk == nk - 1)
        def _():
            dh = acc_ref[...]
            xv = x_ref[...]
            r = lax.rsqrt(jnp.mean(xv * xv, axis=-1, keepdims=True) + EPS)
            xr = xv * r
            dyg = dh * g_ref[...]
            c = jnp.mean(dyg * xr, axis=-1, keepdims=True)
            dx_ref[...] = r * (dyg - xr * c) + res_ref[...]
            dg = jnp.sum(dh * xr, axis=0, keepdims=True)

            @pl.when(i == 0)
            def _():
                dg_ref[...] = dg

            @pl.when(i > 0)
            def _():
                dg_ref[...] += dg

    row_spec = pl.BlockSpec((tm, width), lambda i, kk: (i, 0))
    gain_spec = pl.BlockSpec((1, width), lambda i, kk: (0, 0))
    return pl.pallas_call(
        body,
        grid=(rows // tm, nk),
        in_specs=[pl.BlockSpec((tm, tk), lambda i, kk: (i, kk)), pl.BlockSpec((tk, width), lambda i, kk: (kk, 0)), row_spec,
                  gain_spec, row_spec],
        out_specs=[row_spec, gain_spec],
        out_shape=[jax.ShapeDtypeStruct((rows, width), F32), jax.ShapeDtypeStruct((1, width), F32)],
        scratch_shapes=[pltpu.VMEM((tm, width), F32)],
        compiler_params=_params("arbitrary", "arbitrary"),
        name=name,
    )(dz, wt_in, x, gain, resid)


def _loss_head(y, target, name):
    rows, width = y.shape
    tr = _tile(rows, 256, 8)

    def body(y_ref, t_ref, g_ref, loss_ref):
        i = pl.program_id(0)
        d = y_ref[...] - t_ref[...]
        g_ref[...] = d / width
        part = 0.5 * jnp.sum(jnp.mean(d * d, axis=-1, keepdims=True), axis=0, keepdims=True)
        part = jnp.broadcast_to(part, loss_ref.shape)

        @pl.when(i == 0)
        def _():
            loss_ref[...] = part

        @pl.when(i > 0)
        def _():
            loss_ref[...] += part

    row_spec = pl.BlockSpec((tr, width), lambda i: (i, 0))
    return pl.pallas_call(
        body,
        grid=(rows // tr,),
        in_specs=[row_spec, row_spec],
        out_specs=[row_spec, pl.BlockSpec((1, LANES), lambda i: (0, 0))],
        out_shape=[jax.ShapeDtypeStruct((rows, width), F32), jax.ShapeDtypeStruct((1, LANES), F32)],
        compiler_params=_params("arbitrary"),
        name=name,
    )(y, target)


CHIP_FLIPS = ((1, 0), (0, 1), (1, 1))
ANY = pl.BlockSpec(memory_space=pl.ANY)


def _chip_copies(pieces, sems, n_slot):
    send_sems, recv_sems, local_sems = sems
    x, y, c = lax.axis_index("x"), lax.axis_index("y"), lax.axis_index("c")
    me = 2 * x + y

    def remote(j, k, a, src, dst):
        fx, fy = CHIP_FLIPS[k]
        return pltpu.make_async_remote_copy(
            src_ref=src, dst_ref=dst, send_sem=send_sems.at[n_slot * k + a], recv_sem=recv_sems.at[n_slot * k + a],
            device_id=((j // 2) ^ fx, (j % 2) ^ fy, c), device_id_type=MESH_ID)

    def peer(j, k):
        fx, fy = CHIP_FLIPS[k]
        return 2 * ((j // 2) ^ fx) + ((j % 2) ^ fy)

    def start_as(j):
        def run():
            for a, (src, dst) in enumerate(pieces(j, j)):
                pltpu.make_async_copy(src, dst, local_sems.at[a]).start()
            for k in range(len(CHIP_FLIPS)):
                for a, (src, dst) in enumerate(pieces(j, peer(j, k))):
                    remote(j, k, a, src, dst).start()
        return run

    def wait_as(j):
        def run():
            for a, (src, dst) in enumerate(pieces(j, j)):
                pltpu.make_async_copy(src, dst, local_sems.at[a]).wait()
            for k in range(len(CHIP_FLIPS)):
                for a, (src, dst) in enumerate(pieces(j, peer(j, k))):
                    remote(j, k, a, src, dst).wait_send()
                for a, (src, dst) in enumerate(pieces(peer(j, k), j)):
                    remote(j, k, a, src, dst).wait_recv()
        return run

    def start():
        for j in range(N_CHIPS):
            pl.when(me == j)(start_as(j))

    def wait():
        for j in range(N_CHIPS):
            pl.when(me == j)(wait_as(j))

    return start, wait


IN_PIECES = ((0, Q_RANK, Z_QLAT), (Q_RANK, KV_RANK, Z_KVLAT), (Q_RANK + KV_RANK, ROPE, Z_KPE),
             (Q_RANK + KV_RANK + ROPE, CONV_W, Z_GB), (Q_RANK + KV_RANK + ROPE + CONV_W, CONV_W, Z_GC),
             (Q_RANK + KV_RANK + ROPE + 2 * CONV_W, CONV_W, Z_XIN), (Q_RANK + KV_RANK + ROPE + 3 * CONV_W, MEM_W, Z_QMEM),
             (Q_RANK + KV_RANK + ROPE + 3 * CONV_W + MEM_W, MIX_W, Z_GATE))
IN_SHARD = IN_COLS // N_CHIPS


def _in_segments(j):
    lo, hi = j * IN_SHARD, (j + 1) * IN_SHARD
    segs = []
    for r0, width, z0 in IN_PIECES:
        a, b = max(lo, r0), min(hi, r0 + width)
        if a < b:
            segs.append((a - lo, z0 + a - r0, b - a))
    return segs


N_SLOT = 11


def _gather_plan(l, shards, zero_rows, part="all"):
    s_in, s_uq, s_ukv, s_conv, s_mk, s_mv, s_o = shards
    ukv_c, mk_r, mk_c, o_r = s_ukv.shape[2], s_mk.shape[1], s_mk.shape[2], s_o.shape[1]
    stack = lambda s: jax.ShapeDtypeStruct((N_CHIPS,) + s.shape[1:], s.dtype)
    in_ins, in_outs = [s_in, zero_rows], [jax.ShapeDtypeStruct((Z_COLS, s_in.shape[2]), s_in.dtype)]
    rest_ins = [s_uq, s_ukv, s_conv, s_mk, s_mv, s_o]
    rest_outs = [stack(s_uq), jax.ShapeDtypeStruct((s_ukv.shape[1], N_CHIPS * ukv_c), s_ukv.dtype), stack(s_conv),
                 jax.ShapeDtypeStruct((N_CHIPS * mk_r, 2 * mk_c), s_mk.dtype),
                 jax.ShapeDtypeStruct((N_CHIPS * o_r, s_o.shape[2]), s_o.dtype)]
    with_in, with_rest = part != "rest", part != "in"

    def build(ins, outs, sems):
        ins, outs = list(ins), list(outs)
        if with_in:
            r_in, r_zero, f_in = ins.pop(0), ins.pop(0), outs.pop(0)
        if with_rest:
            r_uq, r_ukv, r_conv, r_mk, r_mv, r_o = ins
            g_uq, f_ukv, g_conv, f_mkv, f_o = outs

        def pieces(j, t):
            out = []
            if with_in:
                out += [(r_in.at[l, pl.ds(so, n), :], f_in.at[pl.ds(zo, n), :]) for so, zo, n in _in_segments(j)]
            if with_rest:
                out += [(r_uq.at[l], g_uq.at[j]), (r_ukv.at[l], f_ukv.at[:, pl.ds(j * ukv_c, ukv_c)]),
                        (r_conv.at[l], g_conv.at[j]),
                        (r_mk.at[l], f_mkv.at[pl.ds(j * mk_r, mk_r), pl.ds(0, mk_c)]),
                        (r_mv.at[l], f_mkv.at[pl.ds(j * mk_r, mk_r), pl.ds(mk_c, mk_c)]),
                        (r_o.at[l], f_o.at[pl.ds(j * o_r, o_r), :])]
            if with_in and j == t:
                out.append((r_zero, f_in.at[pl.ds(Z_KPE + ROPE, LANES - ROPE), :]))
            return out

        return _chip_copies(pieces, sems, N_SLOT)

    ins = (in_ins if with_in else []) + (rest_ins if with_rest else [])
    outs = (in_outs if with_in else []) + (rest_outs if with_rest else [])
    return _CommPlan(ins, outs, build, len(CHIP_FLIPS) * N_SLOT)


def _scatter_plan(dwt_in, c_uq, dw_ukv, c_conv, dw_mkv, dw_o, part="all"):
    ukv_c, mk_r, mk_c, o_r = dw_ukv.shape[1] // N_CHIPS, dw_mkv.shape[0] // N_CHIPS, dw_mkv.shape[1] // 2, dw_o.shape[0] // N_CHIPS
    with_in, with_rest = part != "rest", part != "in"
    in_outs = [jax.ShapeDtypeStruct((N_CHIPS, IN_SHARD, D_MODEL), BF16)]
    rest_ins = [c_uq, dw_ukv, c_conv, dw_mkv, dw_o]
    rest_outs = [jax.ShapeDtypeStruct(c_uq.shape, c_uq.dtype),
                 jax.ShapeDtypeStruct((N_CHIPS, dw_ukv.shape[0], ukv_c), dw_ukv.dtype),
                 jax.ShapeDtypeStruct(c_conv.shape, c_conv.dtype),
                 jax.ShapeDtypeStruct((N_CHIPS, mk_r, mk_c), dw_mkv.dtype), jax.ShapeDtypeStruct((N_CHIPS, mk_r, mk_c), dw_mkv.dtype),
                 jax.ShapeDtypeStruct((N_CHIPS, o_r, dw_o.shape[1]), dw_o.dtype)]

    def build(ins, outs, sems):
        ins, outs = list(ins), list(outs)
        if with_in:
            r_in, o_in = ins.pop(0), outs.pop(0)
        if with_rest:
            r_uq, r_ukv, r_conv, r_mkv, r_o = ins
            o_uq, o_ukv, o_conv, o_mk, o_mv, o_o = outs

        def pieces(j, t):
            out = []
            if with_in:
                out += [(r_in.at[pl.ds(zo, n), :], o_in.at[j, pl.ds(so, n), :]) for so, zo, n in _in_segments(t)]
            if with_rest:
                out += [(r_uq.at[t], o_uq.at[j]), (r_ukv.at[:, pl.ds(t * ukv_c, ukv_c)], o_ukv.at[j]),
                        (r_conv.at[t], o_conv.at[j]),
                        (r_mkv.at[pl.ds(t * mk_r, mk_r), pl.ds(0, mk_c)], o_mk.at[j]),
                        (r_mkv.at[pl.ds(t * mk_r, mk_r), pl.ds(mk_c, mk_c)], o_mv.at[j]),
                        (r_o.at[pl.ds(t * o_r, o_r), :], o_o.at[j])]
            return out

        return _chip_copies(pieces, sems, N_SLOT)

    ins = ([dwt_in] if with_in else []) + (rest_ins if with_rest else [])
    outs = (in_outs if with_in else []) + (rest_outs if with_rest else [])
    return _CommPlan(ins, outs, build, len(CHIP_FLIPS) * N_SLOT)


HBM = pl.BlockSpec(memory_space=pltpu.HBM)
SEM = pl.BlockSpec(memory_space=pltpu.SEMAPHORE)
SIDE_EFFECT = pltpu.SideEffectType.DATAFLOW_SIDE_EFFECTING


def _comm_start(plan, after, name):
    n_in, n_out = len(plan.ins), len(plan.out_shape)
    n_buf = n_in + n_out

    def body(*refs):
        bufs, sems, token = refs[:n_buf], refs[n_buf + 1:n_buf + 4], refs[-1]
        start, _ = plan.build(bufs[:n_in], bufs[n_in:], sems)
        start()
        token[...] = jnp.zeros_like(token)

    lands = [lax.empty(s.shape, s.dtype) for s in plan.out_shape]
    args = [pltpu.with_memory_space_constraint(a, pltpu.HBM) for a in list(plan.ins) + lands]
    res = pl.pallas_call(
        body,
        in_specs=[HBM] * n_buf + [ANY],
        out_specs=[SEM] * 3 + [HBM] * n_buf + [pl.BlockSpec(memory_space=pltpu.VMEM)],
        out_shape=plan.scratch() + [pltpu.HBM(a.shape, a.dtype) for a in args] + [jax.ShapeDtypeStruct((8, LANES), F32)],
        input_output_aliases={i: 3 + i for i in range(n_buf)},
        compiler_params=pltpu.CompilerParams(has_side_effects=SIDE_EFFECT),
        name=name,
    )(*args, after)
    return list(res[:3]), list(res[3:3 + n_buf]), res[-1]


def _comm_finish(plan, started, after, name):
    sems, bufs, _ = started
    n_in, n_out = len(plan.ins), len(plan.out_shape)
    n_buf = n_in + n_out

    def body(*refs):
        bufs_in, sem_refs = refs[:n_buf], refs[n_buf:n_buf + 3]
        _, wait = plan.build(bufs_in[:n_in], bufs_in[n_in:], sem_refs)
        wait()

    res = pl.pallas_call(
        body,
        in_specs=[HBM] * n_buf + [SEM] * 3 + [ANY],
        out_specs=[HBM] * n_buf,
        out_shape=[pltpu.HBM(b.shape, b.dtype) for b in bufs],
        input_output_aliases={i: i for i in range(n_buf)},
        compiler_params=pltpu.CompilerParams(has_side_effects=SIDE_EFFECT),
        name=name,
    )(*bufs, *sems, after)
    return list(res[n_in:])


def _comm_call(plan, name):
    n_in, n_out = len(plan.ins), len(plan.out_shape)

    def body(*refs):
        start, wait = plan.build(refs[:n_in], refs[n_in:n_in + n_out], refs[n_in + n_out:])
        start()
        wait()

    return list(pl.pallas_call(
        body,
        in_specs=[ANY] * n_in,
        out_specs=[ANY] * n_out,
        out_shape=plan.out_shape,
        scratch_shapes=plan.scratch(),
        name=name,
    )(*plan.ins))


def _sibling_plan(arrays):
    def build(ins, outs, sems):
        send_sems, recv_sems, _ = sems
        sibling = (lax.axis_index("x"), lax.axis_index("y"), 1 - lax.axis_index("c"))
        copies = [pltpu.make_async_remote_copy(src_ref=src, dst_ref=dst, send_sem=send_sems.at[a], recv_sem=recv_sems.at[a],
                                               device_id=sibling, device_id_type=MESH_ID)
                  for a, (src, dst) in enumerate(zip(ins, outs))]

        def start():
            for cp in copies:
                cp.start()

        def wait():
            for cp in copies:
                cp.wait()

        return start, wait

    return _CommPlan(arrays, [jax.ShapeDtypeStruct(v.shape, v.dtype) for v in arrays], build, len(arrays))


DEVICE_FLIPS = tuple((fx, fy, fc) for fx in (0, 1) for fy in (0, 1) for fc in (0, 1))[1:]


def _gather_all(v, name):
    def body(v_ref, out_ref, send_sems, recv_sems, local_sem):
        x, y, c = lax.axis_index("x"), lax.axis_index("y"), lax.axis_index("c")
        me = 4 * x + 2 * y + c
        local = pltpu.make_async_copy(v_ref, out_ref.at[me], local_sem)
        local.start()
        copies = [local]
        for k, (fx, fy, fc) in enumerate(DEVICE_FLIPS):
            cp = pltpu.make_async_remote_copy(
                src_ref=v_ref, dst_ref=out_ref.at[me], send_sem=send_sems.at[k], recv_sem=recv_sems.at[k],
                device_id=((x + fx) % 2, (y + fy) % 2, (c + fc) % 2), device_id_type=MESH_ID)
            cp.start()
            copies.append(cp)
        for cp in copies:
            cp.wait()

    return pl.pallas_call(
        body,
        in_specs=[ANY],
        out_specs=ANY,
        out_shape=jax.ShapeDtypeStruct((N_DEV,) + v.shape, v.dtype),
        scratch_shapes=[pltpu.SemaphoreType.DMA((N_DEV - 1,)), pltpu.SemaphoreType.DMA((N_DEV - 1,)), pltpu.SemaphoreType.DMA],
        name=name,
    )(v)


def _sum_slots(parts, name):
    n, rows, cols = parts.shape
    tr = _tile(rows, 256, 16)

    def body(p_ref, o_ref):
        acc = p_ref[0].astype(F32)
        for k in range(1, n):
            acc = acc + p_ref[k].astype(F32)
        o_ref[...] = acc

    return pl.pallas_call(
        body,
        grid=(rows // tr,),
        in_specs=[pl.BlockSpec((n, tr, cols), lambda i: (0, i, 0))],
        out_specs=pl.BlockSpec((tr, cols), lambda i: (i, 0)),
        out_shape=jax.ShapeDtypeStruct((rows, cols), F32),
        compiler_params=_params("parallel"),
        name=name,
    )(parts)


def _adamw_math(w, g, m, v):
    m_new = ADAM_B1 * m + (1.0 - ADAM_B1) * g
    v_new = ADAM_B2 * v + (1.0 - ADAM_B2) * jnp.square(g)
    m_hat = m_new / (1.0 - ADAM_B1 ** ADAM_STEP)
    v_hat = v_new / (1.0 - ADAM_B2 ** ADAM_STEP)
    return -ADAM_LR * (m_hat / (jnp.sqrt(v_hat) + ADAM_EPS) + ADAM_WD * w), m_new, v_new


def _adamw(w, g, m, v, name):
    rows, cols = w.shape
    tr = _tile(rows, 256, 8)

    def body(w_ref, g_ref, m_ref, v_ref, d_out, m_out, v_out):
        d_out[...], m_out[...], v_out[...] = _adamw_math(w_ref[...], g_ref[...], m_ref[...], v_ref[...])

    blk = pl.BlockSpec((tr, cols), lambda i: (i, 0))
    out = jax.ShapeDtypeStruct((rows, cols), F32)
    return pl.pallas_call(
        body,
        grid=(rows // tr,),
        in_specs=[blk] * 4,
        out_specs=[blk] * 3,
        out_shape=[out] * 3,
        compiler_params=_params("parallel"),
        name=name,
    )(w, g, m, v)


def _adamw_layer(l, w, g_a, g_b, m, v, prev, name):
    depth, rows, cols = w.shape
    tr = _tile(rows, 256, 8)

    def body(w_ref, ga_ref, gb_ref, m_ref, v_ref, *rest):
        g_out, d_out, m_out, v_out = rest[-4:]
        g = ga_ref[...] + gb_ref[...]
        g_out[...] = g
        d_out[...], m_out[...], v_out[...] = _adamw_math(w_ref[...], g, m_ref[...], v_ref[...])

    stacked = pl.BlockSpec((None, tr, cols), lambda i: (l, i, 0))
    flat = pl.BlockSpec((tr, cols), lambda i: (i, 0))
    in_specs = [stacked, flat, flat, stacked, stacked]
    args = [w, g_a, g_b, m, v]
    aliases = {}
    if prev is not None:
        in_specs += [ANY] * 4
        args += list(prev)
        aliases = {5 + k: k for k in range(4)}
    out = jax.ShapeDtypeStruct((depth, rows, cols), F32)
    return pl.pallas_call(
        body,
        grid=(rows // tr,),
        in_specs=in_specs,
        out_specs=[stacked] * 4,
        out_shape=[out] * 4,
        input_output_aliases=aliases,
        compiler_params=_params("parallel"),
        name=name,
    )(*args)


def _cols_from_shards(g):
    _, r, c = g.shape
    return jnp.transpose(g, (1, 0, 2)).reshape(r, N_CHIPS * c)


def _cols_to_shards(full):
    r, c4 = full.shape
    c = c4 // N_CHIPS
    return jnp.transpose(full.reshape(r, N_CHIPS, c), (1, 0, 2))


IN_ORDER = (Q_RANK, KV_RANK, ROPE, CONV_W, CONV_W, CONV_W, MEM_W, MIX_W)


def _w_in_to_z_layout(w_in):
    edges = [0]
    for width in IN_ORDER:
        edges.append(edges[-1] + width)
    q_lat, kv_lat, k_pe, gb, gc, xin, q_mem, gate = [w_in[..., edges[i]:edges[i + 1]] for i in range(8)]
    pad = jnp.zeros(k_pe.shape[:-1] + (LANES - ROPE,), w_in.dtype)
    return jnp.concatenate([gate, q_lat, kv_lat, k_pe, pad, gb, gc, xin, q_mem], axis=-1)


def _w_in_from_z_layout(wz):
    cut = lambda c0, width: wz[..., c0:c0 + width]
    return jnp.concatenate(
        [cut(Z_QLAT, Q_RANK), cut(Z_KVLAT, KV_RANK), cut(Z_KPE, ROPE), cut(Z_GB, CONV_W), cut(Z_GC, CONV_W),
         cut(Z_XIN, CONV_W), cut(Z_QMEM, MEM_W), cut(Z_GATE, MIX_W)], axis=-1)


def _w_uq_pad(w_uq):
    r, _ = w_uq.shape
    w = jnp.pad(w_uq.reshape(r, MLA_HEADS, QK_HEAD), ((0, 0), (0, 0), (0, QPAD - QK_HEAD)))
    return w.reshape(r, MLA_HEADS * QPAD)


def _w_uq_unpad(w):
    r, _ = w.shape
    return w.reshape(r, MLA_HEADS, QPAD)[..., :QK_HEAD].reshape(r, MLA_HEADS * QK_HEAD)


def _rope_tables(positions):
    inv_freq = 1.0 / (ROPE_THETA ** (jnp.arange(0, ROPE, 2, dtype=F32) / ROPE))
    ang = positions.astype(F32)[:, None] * inv_freq
    cos, sin = jnp.cos(ang), jnp.sin(ang)
    s = positions.shape[0]
    zero = jnp.zeros((s, HALF_ROPE), F32)
    pad = jnp.zeros((s, LANES - ROPE), F32)
    kc = jnp.concatenate([cos, cos, pad], axis=-1)
    ka = jnp.concatenate([-sin, zero, pad], axis=-1)
    kb = jnp.concatenate([zero, sin, pad], axis=-1)
    qc = jnp.concatenate([jnp.ones((s, NOPE), F32), kc], axis=-1)
    qa = jnp.concatenate([jnp.zeros((s, NOPE), F32), ka], axis=-1)
    qb = jnp.concatenate([jnp.zeros((s, NOPE), F32), kb], axis=-1)
    return (qc, qa, qb), (kc, ka, kb)


def _layer_weights(gathered):
    return (gathered[0],) + _late_weights(gathered[1:])


def _late_weights(gathered):
    g_uq, w_ukv, g_conv, w_mkv, w_o = gathered
    return (_w_uq_pad(_cols_from_shards(g_uq)), w_ukv, _cols_from_shards(g_conv), w_mkv, w_o)


def _layer_fwd(l, x, mem, wts, gains, tabs, comm, late=None):
    wt_in = wts[0]
    g_pre, g_q, g_kv, g_mem, g_post = gains
    q_tab, k_tab = tabs
    tag = f"l{l}_"
    h = _rmsnorm_fwd(x, g_pre, 0, D_MODEL, tag + "pre_norm")
    z = _matmul(h, wt_in, "nt", F32, tag + "in_proj", tn_cap=1664)
    w_uq, w_ukv, conv_w, w_mkv, w_o = wts[1:] if late is None else late(z)
    wts = (wt_in, w_uq, w_ukv, conv_w, w_mkv, w_o)
    qn = _rmsnorm_fwd(z, g_q, Z_QLAT, Q_RANK, tag + "q_norm")
    kvn = _rmsnorm_fwd(z, g_kv, Z_KVLAT, KV_RANK, tag + "kv_norm")
    q_raw = _matmul(qn, w_uq, "nn", F32, tag + "uq")
    kv = _matmul(kvn, w_ukv, "nn", BF16, tag + "ukv")
    kpe = _rope(z, *k_tab, Z_KPE, LANES, 1, tag + "k_rope")
    a_out, a_lse, arrived = _attn_fwd(q_raw, kv, kpe, kv, q_tab, MLA_HEADS, QPAD, 0, 0, 2, 1, 2, QK_HEAD ** -0.5, 512,
                                      tag + "mla_fwd", comm)
    c_out = _conv_fwd(z, conv_w, tag + "conv_fwd")
    mem_n = _rmsnorm_fwd(mem, g_mem, 0, D_MODEL, tag + "mem_norm")
    mkv = _matmul(mem_n, w_mkv, "nn", BF16, tag + "mem_kv")
    m_out, m_lse, _ = _attn_fwd(z, mkv, None, mkv, None, MEM_HEADS, LANES, Z_QMEM // LANES, 0, 1, MEM_HEADS, 1,
                                MEM_HEAD ** -0.5, 1024, tag + "mem_fwd")
    cat = jnp.concatenate([a_out, c_out, m_out], axis=-1)
    y = _gate_fwd(cat, z, tag + "gate_fwd")
    o = _matmul(y, w_o, "nn", F32, tag + "out_proj")
    x_new = _post_norm_residual(x, o, g_post, tag + "post_norm")
    saved = (x, h, z, qn, kvn, q_raw, kv, kpe, a_lse, mem_n, mkv, m_lse, cat, y, o)
    return x_new, saved, arrived


def _layer_bwd(l, g, mem, saved, wts, gains, tabs_bwd, comm, split_exchange=False):
    wt_in, w_uq, w_ukv, conv_w, w_mkv, w_o = wts
    g_pre, g_q, g_kv, g_mem, g_post = gains
    q_tab, k_tab_bwd = tabs_bwd
    x, h, z, qn, kvn, q_raw, kv, kpe, a_lse, mem_n, mkv, m_lse, cat, y, o = saved
    tag = f"l{l}_"
    do, dg_post = _rmsnorm_bwd(o, g_post, g, None, 0, D_MODEL, BF16, tag + "post_norm_bwd")
    dcat, dgate = _out_proj_dx_gate_bwd(do, w_o, cat, z, tag + "out_proj_dx")
    dw_o = _matmul(y, do, "tn", BF16, tag + "out_proj_dw")
    dq, dkv, dkpe_h, arrived = _attn_bwd(q_raw, kv, kpe, kv, cat, dcat, a_lse, q_tab, MLA_HEADS, QPAD, 0, 0, 2, 1, 2, 0,
                                         QK_HEAD ** -0.5, 512, tag + "mla_bwd", comm)
    dkpe = _kpe_grad(dkpe_h, *k_tab_bwd, MLA_HEADS, tag + "k_rope_bwd")
    dw_ukv = _matmul(kvn, dkv, "tn", BF16, tag + "ukv_dw")
    dkvn = _matmul(dkv, w_ukv, "nt", F32, tag + "ukv_dx")
    dkv_lat, dg_kv = _rmsnorm_bwd(z, g_kv, dkvn, None, Z_KVLAT, KV_RANK, BF16, tag + "kv_norm_bwd")
    dw_uq = _matmul(qn, dq, "tn", BF16, tag + "uq_dw")
    dqn = _matmul(dq, w_uq, "nt", F32, tag + "uq_dx")
    dq_lat, dg_q = _rmsnorm_bwd(z, g_q, dqn, None, Z_QLAT, Q_RANK, BF16, tag + "q_norm_bwd")
    dgb, dgc, dxin, dconv_w = _conv_bwd(z, conv_w, dcat, tag + "conv_bwd")
    dq_mem, dmk, dmv, _ = _attn_bwd(z, mkv, None, mkv, cat, dcat, m_lse, None, MEM_HEADS, LANES, Z_QMEM // LANES, 0, 1,
                                    MEM_HEADS, 1, (MLA_W + CONV_W) // LANES, MEM_HEAD ** -0.5, 1024, tag + "mem_bwd")
    dmkv = jnp.concatenate([dmk, dmv], axis=-1)
    dw_mkv = _matmul(mem_n, dmkv, "tn", BF16, tag + "mem_kv_dw")
    dmem_n = _matmul(dmkv, w_mkv, "nt", F32, tag + "mem_kv_dx")
    _, dg_mem = _rmsnorm_bwd(mem, g_mem, dmem_n, None, 0, D_MODEL, BF16, tag + "mem_norm_bwd")
    others = (_cols_to_shards(_w_uq_unpad(dw_uq)), dw_ukv, _cols_to_shards(dconv_w), dw_mkv, dw_o)
    early = None
    if split_exchange:
        early_plan = _scatter_plan(None, *others, part="rest")
        early = (early_plan, _comm_start(early_plan, dmem_n, tag + "exchange_rest_start"))
        g_pre = g_pre + early[1][2][0:1, 0:1]
    dz = jnp.concatenate([dgate, dq_lat, dkv_lat, dkpe, dgb, dgc, dxin, dq_mem], axis=-1)
    dwt_in = _matmul(dz, h, "tn", BF16, tag + "in_proj_dw", tm_cap=1664, tk_cap=1024)
    dx, dg_pre = _in_proj_dx_norm_bwd(dz, wt_in, x, g_pre, g, tag + "in_proj_dx")
    contrib = _scatter_plan(dwt_in, *others, part="in" if split_exchange else "all")
    return dx, contrib, (dg_pre, dg_q, dg_kv, dg_mem, dg_post), early


GAIN_WIDTHS = (D_MODEL, Q_RANK, KV_RANK, D_MODEL, D_MODEL)


def _pack_gains(parts):
    return jnp.concatenate([p.reshape(-1) for p in parts]).reshape(-1, LANES)


def _unpack_gains(packed, depth):
    flat = packed.reshape(-1)
    out, at = [], 0
    for width in GAIN_WIDTHS:
        out.append(flat[at:at + depth * width].reshape(depth, width))
        at += depth * width
    return out


def kernel(x, mem, positions, pre_norm_g, w_in, q_norm_g, w_uq, kv_norm_g, w_ukv, conv_w, mem_norm_g, w_mk, w_mv, w_o, post_norm_g, loss_target, m_pre_norm_g, m_w_in, m_q_norm_g, m_w_uq, m_kv_norm_g, m_w_ukv, m_conv_w, m_mem_norm_g, m_w_mk, m_w_mv, m_w_o, m_post_norm_g, v_pre_norm_g, v_w_in, v_q_norm_g, v_w_uq, v_kv_norm_g, v_w_ukv, v_conv_w, v_mem_norm_g, v_w_mk, v_w_mv, v_w_o, v_post_norm_g):
    depth = w_in.shape[0]
    x0, mem0, target = x[0], mem[0], loss_target[0]
    tabs = _rope_tables(positions[0])
    tabs_bwd = (tabs[0], (tabs[1][0], -tabs[1][1], -tabs[1][2]))

    flip = lambda t: jnp.transpose(t, (0, 2, 1))
    w_in, m_w_in, v_w_in = flip(w_in), flip(m_w_in), flip(v_w_in)
    shards = [w_in.astype(BF16), w_uq.astype(BF16), w_ukv.astype(BF16), conv_w, w_mk.astype(BF16), w_mv.astype(BF16),
              w_o.astype(BF16)]
    zero_rows = lambda: jnp.zeros((LANES - ROPE, D_MODEL), BF16)

    def layer_gains(l):
        return tuple(g[l][None, :] for g in (pre_norm_g, q_norm_g, kv_norm_g, mem_norm_g, post_norm_g))

    wts, saved = [None] * depth, [None] * depth
    first = [s[0:1] for s in shards]
    plan_in, plan_rest = _gather_plan(0, first, zero_rows(), "in"), _gather_plan(0, first, zero_rows(), "rest")
    started_in = _comm_start(plan_in, positions, "l0_gather_in_start")
    started_rest = _comm_start(plan_rest, started_in[2], "l0_gather_rest_start")
    wts[0] = tuple(_comm_finish(plan_in, started_in, started_rest[2], "l0_gather_in_wait"))

    next_gather = {}

    def start_next_gather(l, after):
        plan = _gather_plan(0, [s[l + 1:l + 2] for s in shards], zero_rows())
        next_gather[l + 1] = (plan, _comm_start(plan, after, f"l{l + 1}_gather_start"))
        return next_gather[l + 1][1][2][0:1, 0:1]

    def rest_of_layer0(z):
        got = _late_weights(_comm_finish(plan_rest, started_rest, z, "l0_gather_rest_wait"))
        wts[0] = wts[0] + got
        if depth > 1:
            got = (got[0] + start_next_gather(0, got[4]).astype(BF16),) + got[1:]
        return got

    act = x0
    for l in range(depth):
        gains = layer_gains(l)
        if 0 < l < depth - 1:
            gains = (gains[0] + start_next_gather(l, wts[l][5]),) + gains[1:]
        act, saved[l], _ = _layer_fwd(l, act, mem0, wts[l], gains, tabs, None, rest_of_layer0 if l == 0 else None)
        if l + 1 < depth:
            plan, started = next_gather[l + 1]
            wts[l + 1] = _layer_weights(_comm_finish(plan, started, act, f"l{l + 1}_gather_wait"))
    grad, loss_part = _loss_head(act, target, "loss_head")
    loss = lax.psum(loss_part[0, 0], ("x", "y", "c"))

    names = ("w_in", "w_uq", "w_ukv", "conv_w", "w_mk", "w_mv", "w_o")
    w_shards = (w_in, w_uq, w_ukv, conv_w, w_mk, w_mv, w_o)
    m_shards = (m_w_in, m_w_uq, m_w_ukv, m_conv_w, m_w_mk, m_w_mv, m_w_o)
    v_shards = (v_w_in, v_w_uq, v_w_ukv, v_conv_w, v_w_mk, v_w_mv, v_w_o)
    stacked = [None] * len(names)

    def sum_and_send(l, received):
        partial = [_sum_slots(r, f"l{l}_grad_sum_{names[i]}") for i, r in enumerate(received)]
        plan = _sibling_plan(partial)
        return l, partial, plan, _comm_start(plan, partial[0], f"l{l}_sibling_start")

    def receive_and_update(state, after):
        l, partial, plan, started = state
        other = _comm_finish(plan, started, after, f"l{l}_sibling_wait")
        for i, name in enumerate(names):
            stacked[i] = _adamw_layer(l, w_shards[i], partial[i], other[i], m_shards[i], v_shards[i], stacked[i],
                                      f"l{l}_adamw_{name}")

    dgs = [None] * depth
    pending = None
    in_flight = None
    for l in reversed(range(depth)):
        gains = layer_gains(l)
        for token in ([pending[1][2]] if pending else []) + ([in_flight[3][2]] if in_flight else []):
            gains = gains[:4] + (gains[4] + token[0:1, 0:1],)
        grad, contrib, dgs[l], early = _layer_bwd(l, grad, mem0, saved[l], wts[l], gains, tabs_bwd, None, l == 0)
        if in_flight is not None:
            receive_and_update(in_flight, grad)
            in_flight = None
        if pending is not None:
            in_flight = sum_and_send(l + 1, _comm_finish(pending[0], pending[1], grad, f"l{l + 1}_exchange_wait"))
        if l > 0:
            pending = (contrib, _comm_start(contrib, grad, f"l{l}_exchange_start"))
    got_in = _comm_call(contrib, "l0_grad_exchange_in")
    last = sum_and_send(0, got_in + _comm_finish(early[0], early[1], got_in[0], "l0_exchange_rest_wait"))
    if in_flight is not None:
        receive_and_update(in_flight, last[1][0])
    receive_and_update(last, stacked[0][0] if depth > 1 else last[1][0])
    grad_x = grad[None]
    results = {name: tuple(stacked[i]) for i, name in enumerate(names)}
    results["w_in"] = tuple(flip(t) for t in results["w_in"])

    gain_names = ("pre_norm_g", "q_norm_g", "kv_norm_g", "mem_norm_g", "post_norm_g")
    dg_packed = _pack_gains([jnp.concatenate([dgs[l][i] for l in range(depth)], axis=0) for i in range(5)])
    dg_total = _sum_slots(_gather_all(dg_packed, "gain_gather"), "gain_sum")
    gain_outs = (dg_total,) + tuple(_adamw(
        _pack_gains((pre_norm_g, q_norm_g, kv_norm_g, mem_norm_g, post_norm_g)), dg_total,
        _pack_gains((m_pre_norm_g, m_q_norm_g, m_kv_norm_g, m_mem_norm_g, m_post_norm_g)),
        _pack_gains((v_pre_norm_g, v_q_norm_g, v_kv_norm_g, v_mem_norm_g, v_post_norm_g)), "adamw_gains"))
    gain_outs = [_unpack_gains(t, depth) for t in gain_outs]
    for i, name in enumerate(gain_names):
        results[name] = tuple(gain_outs[k][i] for k in range(4))

    order = ("pre_norm_g", "w_in", "q_norm_g", "w_uq", "kv_norm_g", "w_ukv", "conv_w", "mem_norm_g", "w_mk", "w_mv", "w_o",
             "post_norm_g")
    out = [loss, grad_x]
    for k in range(4):
        out += [results[name][k] for name in order]
    return tuple(out)
```

```python
import functools

import jax
import jax.numpy as jnp
from jax import lax
from jax.experimental import pallas as pl
from jax.experimental.pallas import tpu as pltpu

F32 = jnp.float32
BF16 = jnp.bfloat16
MESH_ID = pl.DeviceIdType.MESH

D_MODEL = 2048
EPS = 1e-6
LOG2_E = 1.4426950408889634
ROPE_THETA = 10000.0
MLA_HEADS = 8
NOPE = 128
ROPE = 64
HALF_ROPE = ROPE // 2
QK_HEAD = NOPE + ROPE
V_HEAD = 128
Q_RANK = 512
KV_RANK = 256
CONV_W = 512
MEM_HEADS = 4
MEM_HEAD = 128
MEM_W = MEM_HEADS * MEM_HEAD
MLA_W = MLA_HEADS * V_HEAD
MIX_W = MLA_W + CONV_W + MEM_W
IN_COLS = Q_RANK + KV_RANK + ROPE + 3 * CONV_W + MEM_W + MIX_W
N_CHIPS = 4
N_DEV = 8

LANES = 128
VMEM_LIMIT_BYTES = 56 * 1024 * 1024

QPAD = 2 * LANES
Z_GATE = 0
Z_QLAT = Z_GATE + MIX_W
Z_KVLAT = Z_QLAT + Q_RANK
Z_KPE = Z_KVLAT + KV_RANK
Z_GB = Z_KPE + LANES
Z_GC = Z_GB + CONV_W
Z_XIN = Z_GC + CONV_W
Z_QMEM = Z_XIN + CONV_W
Z_COLS = Z_QMEM + MEM_W

ADAM_LR = 0.001
ADAM_B1 = 0.9
ADAM_B2 = 0.999
ADAM_EPS = 1e-08
ADAM_WD = 0.01
ADAM_STEP = 10


def _tile(dim, cap, unit):
    if dim <= cap:
        return dim
    t = (cap // unit) * unit
    while t >= unit:
        if dim % t == 0:
            return t
        t -= unit
    raise ValueError(f"no tile of {dim} under {cap} in units of {unit}")


def _params(*semantics):
    return pltpu.CompilerParams(dimension_semantics=semantics, vmem_limit_bytes=VMEM_LIMIT_BYTES)


def _matmul(a, b, mode, out_dtype, name, tm_cap=512, tn_cap=1024, tk_cap=2048):
    if mode == "nn":
        (m, k), (k2, n) = a.shape, b.shape
    elif mode == "nt":
        (m, k), (n, k2) = a.shape, b.shape
    else:
        (k, m), (k2, n) = a.shape, b.shape
    assert k == k2, (a.shape, b.shape, mode)
    tm = _tile(m, tm_cap, LANES if mode == "tn" else 16)
    tn = _tile(n, tn_cap, LANES)
    tk = _tile(k, tk_cap, LANES if mode != "tn" else 16)
    nk = k // tk
    if mode == "nn":
        a_spec = pl.BlockSpec((tm, tk), lambda i, j, kk: (i, kk))
        b_spec = pl.BlockSpec((tk, tn), lambda i, j, kk: (kk, j))
        dims = (((1,), (0,)), ((), ()))
    elif mode == "nt":
        a_spec = pl.BlockSpec((tm, tk), lambda i, j, kk: (i, kk))
        b_spec = pl.BlockSpec((tn, tk), lambda i, j, kk: (j, kk))
        dims = (((1,), (1,)), ((), ()))
    else:
        a_spec = pl.BlockSpec((tk, tm), lambda i, j, kk: (kk, i))
        b_spec = pl.BlockSpec((tk, tn), lambda i, j, kk: (kk, j))
        dims = (((0,), (0,)), ((), ()))

    def body(a_ref, b_ref, o_ref, *scratch):
        part = lax.dot_general(a_ref[...].astype(BF16), b_ref[...].astype(BF16), dims, preferred_element_type=F32)
        if nk == 1:
            o_ref[...] = part.astype(o_ref.dtype)
            return
        (acc_ref,) = scratch
        kk = pl.program_id(2)

        @pl.when(kk == 0)
        def _():
            acc_ref[...] = part

        @pl.when(kk > 0)
        def _():
            acc_ref[...] += part

        @pl.when(kk == nk - 1)
        def _():
            o_ref[...] = acc_ref[...].astype(o_ref.dtype)

    return pl.pallas_call(
        body,
        grid=(m // tm, n // tn, nk),
        in_specs=[a_spec, b_spec],
        out_specs=pl.BlockSpec((tm, tn), lambda i, j, kk: (i, j)),
        out_shape=jax.ShapeDtypeStruct((m, n), out_dtype),
        scratch_shapes=[] if nk == 1 else [pltpu.VMEM((tm, tn), F32)],
        compiler_params=_params("parallel", "parallel", "arbitrary"),
        name=name,
    )(a, b)


def _rmsnorm_fwd(x, gain, col0, width, name):
    rows = x.shape[0]
    tr = _tile(rows, 512, 16)
    cb = col0 // width
    assert cb * width == col0

    def body(x_ref, g_ref, o_ref):
        xv = x_ref[...]
        r = lax.rsqrt(jnp.mean(xv * xv, axis=-1, keepdims=True) + EPS)
        o_ref[...] = (xv * r * g_ref[...]).astype(o_ref.dtype)

    return pl.pallas_call(
        body,
        grid=(rows // tr,),
        in_specs=[pl.BlockSpec((tr, width), lambda i: (i, cb)), pl.BlockSpec((1, width), lambda i: (0, 0))],
        out_specs=pl.BlockSpec((tr, width), lambda i: (i, 0)),
        out_shape=jax.ShapeDtypeStruct((rows, width), BF16),
        compiler_params=_params("parallel"),
        name=name,
    )(x, gain)


def _rmsnorm_bwd(x, gain, dy, resid, col0, width, out_dtype, name):
    rows = x.shape[0]
    tr = _tile(rows, 256, 16)
    cb = col0 // width
    assert cb * width == col0
    has_resid = resid is not None

    def body(*refs):
        if has_resid:
            x_ref, g_ref, dy_ref, res_ref, dx_ref, dg_ref = refs
        else:
            x_ref, g_ref, dy_ref, dx_ref, dg_ref = refs
        i = pl.program_id(0)
        xv = x_ref[...]
        dyv = dy_ref[...].astype(F32)
        r = lax.rsqrt(jnp.mean(xv * xv, axis=-1, keepdims=True) + EPS)
        xr = xv * r
        dyg = dyv * g_ref[...]
        c = jnp.mean(dyg * xr, axis=-1, keepdims=True)
        dx = r * (dyg - xr * c)
        if has_resid:
            dx = dx + res_ref[...]
        dx_ref[...] = dx.astype(dx_ref.dtype)
        part = jnp.sum(dyv * xr, axis=0, keepdims=True)

        @pl.when(i == 0)
        def _():
            dg_ref[...] = part

        @pl.when(i > 0)
        def _():
            dg_ref[...] += part

    row_spec = pl.BlockSpec((tr, width), lambda i: (i, 0))
    in_specs = [pl.BlockSpec((tr, width), lambda i: (i, cb)), pl.BlockSpec((1, width), lambda i: (0, 0)), row_spec]
    args = [x, gain, dy]
    if has_resid:
        in_specs.append(row_spec)
        args.append(resid)
    return pl.pallas_call(
        body,
        grid=(rows // tr,),
        in_specs=in_specs,
        out_specs=[row_spec, pl.BlockSpec((1, width), lambda i: (0, 0))],
        out_shape=[jax.ShapeDtypeStruct((rows, width), out_dtype), jax.ShapeDtypeStruct((1, width), F32)],
        compiler_params=_params("arbitrary"),
        name=name,
    )(*args)


def _post_norm_residual(x, o, gain, name):
    rows, width = x.shape
    tr = _tile(rows, 256, 8)

    def body(x_ref, o_ref, g_ref, out_ref):
        ov = o_ref[...]
        r = lax.rsqrt(jnp.mean(ov * ov, axis=-1, keepdims=True) + EPS)
        out_ref[...] = x_ref[...] + ov * r * g_ref[...]

    row_spec = pl.BlockSpec((tr, width), lambda i: (i, 0))
    return pl.pallas_call(
        body,
        grid=(rows // tr,),
        in_specs=[row_spec, row_spec, pl.BlockSpec((1, width), lambda i: (0, 0))],
        out_specs=row_spec,
        out_shape=jax.ShapeDtypeStruct((rows, width), F32),
        compiler_params=_params("parallel"),
        name=name,
    )(x, o, gain)


def _rope(x, tab_c, tab_a, tab_b, col0, width, heads, name):
    rows = x.shape[0]
    tr = _tile(rows, 512, 16)
    cb = col0 // width
    assert cb * width == col0

    def body(x_ref, c_ref, a_ref, b_ref, o_ref):
        xv = x_ref[...].astype(F32)
        up = pltpu.roll(xv, width - HALF_ROPE, 1)
        down = pltpu.roll(xv, HALF_ROPE, 1)
        o_ref[...] = (xv * c_ref[...] + up * a_ref[...] + down * b_ref[...]).astype(o_ref.dtype)

    tab_spec = pl.BlockSpec((tr, width), lambda i, h: (i, 0))
    return pl.pallas_call(
        body,
        grid=(rows // tr, heads),
        in_specs=[pl.BlockSpec((tr, width), lambda i, h: (i, cb + h)), tab_spec, tab_spec, tab_spec],
        out_specs=pl.BlockSpec((tr, width), lambda i, h: (i, h)),
        out_shape=jax.ShapeDtypeStruct((rows, heads * width), BF16),
        compiler_params=_params("parallel", "parallel"),
        name=name,
    )(x, tab_c, tab_a, tab_b)


def _kpe_grad(dkb, tab_c, tab_a, tab_b, heads, name):
    rows = dkb.shape[0]
    tr = _tile(rows, 512, 16)

    def body(d_ref, c_ref, a_ref, b_ref, o_ref):
        acc = d_ref[:, 0:LANES]
        for h in range(1, heads):
            acc = acc + d_ref[:, h * LANES:(h + 1) * LANES]
        up = pltpu.roll(acc, LANES - HALF_ROPE, 1)
        down = pltpu.roll(acc, HALF_ROPE, 1)
        o_ref[...] = (acc * c_ref[...] + up * a_ref[...] + down * b_ref[...]).astype(o_ref.dtype)

    tab_spec = pl.BlockSpec((tr, LANES), lambda i: (i, 0))
    return pl.pallas_call(
        body,
        grid=(rows // tr,),
        in_specs=[pl.BlockSpec((tr, heads * LANES), lambda i: (i, 0)), tab_spec, tab_spec, tab_spec],
        out_specs=tab_spec,
        out_shape=jax.ShapeDtypeStruct((rows, LANES), BF16),
        compiler_params=_params("parallel"),
        name=name,
    )(dkb, tab_c, tab_a, tab_b)


class _CommPlan:
    def __init__(self, ins, out_shape, build, n_copies):
        self.ins, self.out_shape, self.build, self.n_copies = list(ins), list(out_shape), build, n_copies

    def scratch(self):
        n = self.n_copies
        return [pltpu.SemaphoreType.DMA((n,)), pltpu.SemaphoreType.DMA((n,)), pltpu.SemaphoreType.DMA((n,))]


def _split_comm(refs, n_in, n_out, comm):
    if comm is None:
        return refs, None
    ci, co = len(comm.ins), len(comm.out_shape)
    ins, c_ins = refs[:n_in], refs[n_in:n_in + ci]
    outs, c_outs = refs[n_in + ci:n_in + ci + n_out], refs[n_in + ci + n_out:n_in + ci + n_out + co]
    rest = refs[n_in + ci + n_out + co:]
    scratch, sems = rest[:-3], rest[-3:]
    return tuple(ins) + tuple(outs) + tuple(scratch), functools.partial(comm.build, c_ins, c_outs, sems)


def _ride_start(copies, first):
    if copies is not None:
        pl.when(first)(copies()[0])


def _ride_wait(copies, last):
    if copies is not None:
        pl.when(last)(copies()[1])


def _rope_rows(x, c, a, b, sign):
    width = x.shape[-1]
    mixed = pltpu.roll(x, width - HALF_ROPE, 1) * a + pltpu.roll(x, HALF_ROPE, 1) * b
    return x * c + mixed if sign > 0 else x * c - mixed


def _attn_fwd(q, ka, kb, v, rope, heads, q_w, q_cb, ka_cb, ka_step, v_cb, v_step, scale, tq_cap, name, comm=None, tk_cap=512):
    s_q, s_k = q.shape[0], ka.shape[0]
    tq = _tile(s_q, tq_cap, 16)
    nq = s_q // tq
    has_kb = kb is not None
    n_in = 7 if has_kb else 3
    tk = _tile(s_k, tk_cap, LANES)

    def body(*refs):
        refs, copies = _split_comm(refs, n_in, 2, comm)
        first = jnp.logical_and(pl.program_id(0) == 0, pl.program_id(1) == 0)
        last = jnp.logical_and(pl.program_id(0) == heads - 1, pl.program_id(1) == nq - 1)
        _ride_start(copies, first)
        if has_kb:
            q_ref, ka_ref, kb_ref, v_ref, c_ref, a_ref, b_ref, o_ref, lse_ref, k_scr = refs

            @pl.when(pl.program_id(1) == 0)
            def _():
                k_scr[:, 0:LANES] = ka_ref[...].astype(BF16)
                k_scr[:, LANES:2 * LANES] = kb_ref[...].astype(BF16)

            keys = k_scr
            qv = _rope_rows(q_ref[...], c_ref[...], a_ref[...], b_ref[...], 1).astype(BF16)
        else:
            q_ref, ka_ref, v_ref, o_ref, lse_ref = refs
            keys = ka_ref
            qv = q_ref[...].astype(BF16)
        c2 = scale * LOG2_E
        m = l = o = None
        nk = s_k // tk
        scores = lambda j: lax.dot_general(qv, keys[j * tk:(j + 1) * tk, :].astype(BF16), (((1,), (1,)), ((), ())),
                                           preferred_element_type=F32)
        s_next = scores(0)
        for j in range(nk):
            sj = s_next
            if j + 1 < nk:
                s_next = scores(j + 1)
            mj = jnp.max(sj, axis=-1, keepdims=True)
            m_new = mj if m is None else jnp.maximum(m, mj)
            pj = jnp.exp2((sj - m_new) * c2)
            lj = jnp.sum(pj, axis=-1, keepdims=True)
            oj = jnp.dot(pj.astype(BF16), v_ref[j * tk:(j + 1) * tk, :].astype(BF16), preferred_element_type=F32)
            if m is None:
                l, o = lj, oj
            else:
                alpha = jnp.exp2((m - m_new) * c2)
                l, o = l * alpha + lj, o * alpha + oj
            m = m_new
        o_ref[...] = (o * (1.0 / l)).astype(o_ref.dtype)
        lse_ref[...] = jnp.broadcast_to(m * c2 + jnp.log2(l), lse_ref.shape)
        _ride_wait(copies, last)

    in_specs = [pl.BlockSpec((tq, q_w), lambda h, i: (i, q_cb + h)),
                pl.BlockSpec((s_k, LANES), lambda h, i: (0, ka_cb + ka_step * h))]
    args = [q, ka]
    if has_kb:
        in_specs.append(pl.BlockSpec((s_k, LANES), lambda h, i: (0, 0)))
        args.append(kb)
    in_specs.append(pl.BlockSpec((s_k, LANES), lambda h, i: (0, v_cb + v_step * h)))
    args.append(v)
    if has_kb:
        in_specs += [pl.BlockSpec((tq, q_w), lambda h, i: (i, 0))] * 3
        args += list(rope)
    out_spec = pl.BlockSpec((tq, LANES), lambda h, i: (i, h))
    out_specs = [out_spec, out_spec]
    out_shape = [jax.ShapeDtypeStruct((s_q, heads * LANES), BF16), jax.ShapeDtypeStruct((s_q, heads * LANES), F32)]
    scratch = [pltpu.VMEM((s_k, 2 * LANES), BF16)] if has_kb else []
    if comm is not None:
        in_specs += [ANY] * len(comm.ins)
        args += comm.ins
        out_specs += [ANY] * len(comm.out_shape)
        out_shape += comm.out_shape
        scratch += comm.scratch()
    res = pl.pallas_call(
        body,
        grid=(heads, nq),
        in_specs=in_specs,
        out_specs=out_specs,
        out_shape=out_shape,
        scratch_shapes=scratch,
        compiler_params=_params("arbitrary", "arbitrary"),
        name=name,
    )(*args)
    return res[0], res[1], list(res[2:])


def _attn_bwd(q, ka, kb, v, o, do, lse, rope, heads, q_w, q_cb, ka_cb, ka_step, v_cb, v_step, o_cb, scale, tq_cap, name,
              comm=None, tk_cap=512):
    s_q, s_k = q.shape[0], ka.shape[0]
    tq = _tile(s_q, tq_cap, 16)
    nq = s_q // tq
    has_kb = kb is not None
    n_in = 10 if has_kb else 6
    n_out = 3
    tk = _tile(s_k, tk_cap, LANES)

    def body(*refs):
        refs, copies = _split_comm(refs, n_in, n_out, comm)
        first = jnp.logical_and(pl.program_id(0) == 0, pl.program_id(1) == 0)
        last = jnp.logical_and(pl.program_id(0) == heads - 1, pl.program_id(1) == nq - 1)
        _ride_start(copies, first)
        if has_kb:
            (q_ref, ka_ref, kb_ref, v_ref, o_ref, do_ref, lse_ref, c_ref, a_ref, b_ref, dq_ref, dkv_ref, dkb_ref, k_scr, dk_acc,
             dv_acc) = refs
        else:
            q_ref, ka_ref, v_ref, o_ref, do_ref, lse_ref, dq_ref, dka_ref, dv_ref, dk_acc, dv_acc = refs
        i = pl.program_id(1)

        @pl.when(i == 0)
        def _():
            dk_acc[...] = jnp.zeros_like(dk_acc)
            dv_acc[...] = jnp.zeros_like(dv_acc)
            if has_kb:
                k_scr[:, 0:LANES] = ka_ref[...].astype(BF16)
                k_scr[:, LANES:2 * LANES] = kb_ref[...].astype(BF16)

        keys = k_scr if has_kb else ka_ref
        if has_kb:
            qv = _rope_rows(q_ref[...], c_ref[...], a_ref[...], b_ref[...], 1).astype(BF16)
        else:
            qv = q_ref[...].astype(BF16)
        dov = do_ref[...].astype(BF16)
        delta = jnp.sum(dov.astype(F32) * o_ref[...].astype(F32), axis=-1, keepdims=True)
        lse2 = lse_ref[:, 0:1]
        c2 = scale * LOG2_E
        nk = s_k // tk
        rows = lambda j: slice(j * tk, (j + 1) * tk)
        nt = (((1,), (1,)), ((), ()))
        tn = (((0,), (0,)), ((), ()))

        def scores(j):
            return (lax.dot_general(qv, keys[rows(j), :].astype(BF16), nt, preferred_element_type=F32),
                    lax.dot_general(dov, v_ref[rows(j), :].astype(BF16), nt, preferred_element_type=F32))

        nxt = scores(0)
        dq = None
        for j in range(nk):
            sj, dpj = nxt
            if j + 1 < nk:
                nxt = scores(j + 1)
            pj = jnp.exp2(sj * c2 - lse2)
            dsj = (pj * (dpj - delta)).astype(BF16)
            dqj = jnp.dot(dsj, keys[rows(j), :].astype(BF16), preferred_element_type=F32)
            dq = dqj if dq is None else dq + dqj
            dk_acc[rows(j), :] += lax.dot_general(dsj, qv, tn, preferred_element_type=F32)
            dv_acc[rows(j), :] += lax.dot_general(pj.astype(BF16), dov, tn, preferred_element_type=F32)
        dq = dq * scale
        if has_kb:
            dq = _rope_rows(dq, c_ref[...], a_ref[...], b_ref[...], -1)
        dq_ref[...] = dq.astype(dq_ref.dtype)

        @pl.when(i == nq - 1)
        def _():
            if has_kb:
                dkv_ref[:, 0:LANES] = (dk_acc[:, 0:LANES] * scale).astype(dkv_ref.dtype)
                dkv_ref[:, LANES:2 * LANES] = dv_acc[...].astype(dkv_ref.dtype)
                dkb_ref[...] = dk_acc[:, LANES:2 * LANES] * scale
            else:
                dka_ref[...] = (dk_acc[...] * scale).astype(dka_ref.dtype)
                dv_ref[...] = dv_acc[...].astype(dv_ref.dtype)

        _ride_wait(copies, last)

    key_spec = lambda cb, step: pl.BlockSpec((s_k, LANES), lambda h, i: (0, cb + step * h))
    row_spec = lambda cb: pl.BlockSpec((tq, LANES), lambda h, i: (i, cb + h))
    in_specs = [pl.BlockSpec((tq, q_w), lambda h, i: (i, q_cb + h)), key_spec(ka_cb, ka_step)]
    args = [q, ka]
    if has_kb:
        in_specs.append(pl.BlockSpec((s_k, LANES), lambda h, i: (0, 0)))
        args.append(kb)
    in_specs += [key_spec(v_cb, v_step), row_spec(o_cb), row_spec(o_cb), row_spec(0)]
    args += [v, o, do, lse]
    if has_kb:
        in_specs += [pl.BlockSpec((tq, q_w), lambda h, i: (i, 0))] * 3
        args += list(rope)
    out_specs = [pl.BlockSpec((tq, q_w), lambda h, i: (i, h))]
    out_shape = [jax.ShapeDtypeStruct((s_q, heads * q_w), BF16)]
    scratch = []
    if has_kb:
        out_specs += [pl.BlockSpec((s_k, 2 * LANES), lambda h, i: (0, h)), key_spec(0, 1)]
        out_shape += [jax.ShapeDtypeStruct((s_k, heads * 2 * LANES), BF16), jax.ShapeDtypeStruct((s_k, heads * LANES), F32)]
        scratch.append(pltpu.VMEM((s_k, 2 * LANES), BF16))
    else:
        out_specs += [key_spec(0, 1), key_spec(0, 1)]
        out_shape += [jax.ShapeDtypeStruct((s_k, heads * LANES), BF16)] * 2
    scratch += [pltpu.VMEM((s_k, q_w), F32), pltpu.VMEM((s_k, LANES), F32)]
    if comm is not None:
        in_specs += [ANY] * len(comm.ins)
        args += comm.ins
        out_specs += [ANY] * len(comm.out_shape)
        out_shape += comm.out_shape
        scratch += comm.scratch()
    res = pl.pallas_call(
        body,
        grid=(heads, nq),
        in_specs=in_specs,
        out_specs=out_specs,
        out_shape=out_shape,
        scratch_shapes=scratch,
        compiler_params=_params("arbitrary", "arbitrary"),
        name=name,
    )(*args)
    return res[0], res[1], res[2], list(res[3:])


def _shift_rows(u, rows):
    t = lax.broadcasted_iota(jnp.int32, u.shape, 0)
    prev = jnp.where(t == 0, 0.0, pltpu.roll(u, 1, 0))
    nxt = jnp.where(t == rows - 1, 0.0, pltpu.roll(u, rows - 1, 0))
    return prev, nxt


def _conv_fwd(z, conv_w, name):
    rows = z.shape[0]
    nblk = CONV_W // LANES

    def body(gb_ref, gc_ref, xin_ref, w_ref, o_ref):
        u = gc_ref[...] * xin_ref[...]
        prev, nxt = _shift_rows(u, rows)
        conv = prev * w_ref[0:1, :] + u * w_ref[1:2, :] + nxt * w_ref[2:3, :]
        o_ref[...] = (gb_ref[...] * conv).astype(o_ref.dtype)

    col = lambda c0: pl.BlockSpec((rows, LANES), lambda j: (0, c0 // LANES + j))
    return pl.pallas_call(
        body,
        grid=(nblk,),
        in_specs=[col(Z_GB), col(Z_GC), col(Z_XIN), pl.BlockSpec((3, LANES), lambda j: (0, j))],
        out_specs=col(0),
        out_shape=jax.ShapeDtypeStruct((rows, CONV_W), BF16),
        compiler_params=_params("parallel"),
        name=name,
    )(z, z, z, conv_w)


def _conv_bwd(z, conv_w, dcat, name):
    rows = z.shape[0]
    nblk = CONV_W // LANES

    def body(gb_ref, gc_ref, xin_ref, w_ref, dc_ref, dgb_ref, dgc_ref, dxin_ref, dw_ref):
        gc = gc_ref[...]
        xin = xin_ref[...]
        dc = dc_ref[...].astype(F32)
        u = gc * xin
        prev, nxt = _shift_rows(u, rows)
        w0, w1, w2 = w_ref[0:1, :], w_ref[1:2, :], w_ref[2:3, :]
        conv = prev * w0 + u * w1 + nxt * w2
        dgb_ref[...] = (dc * conv).astype(dgb_ref.dtype)
        dconv = dc * gb_ref[...]
        dw_ref[0:1, :] = jnp.sum(dconv * prev, axis=0, keepdims=True)
        dw_ref[1:2, :] = jnp.sum(dconv * u, axis=0, keepdims=True)
        dw_ref[2:3, :] = jnp.sum(dconv * nxt, axis=0, keepdims=True)
        dprev, dnxt = _shift_rows(dconv, rows)
        du = dnxt * w0 + dconv * w1 + dprev * w2
        dgc_ref[...] = (du * xin).astype(dgc_ref.dtype)
        dxin_ref[...] = (du * gc).astype(dxin_ref.dtype)

    col = lambda c0: pl.BlockSpec((rows, LANES), lambda j: (0, c0 // LANES + j))
    w_spec = pl.BlockSpec((3, LANES), lambda j: (0, j))
    piece = jax.ShapeDtypeStruct((rows, CONV_W), BF16)
    return pl.pallas_call(
        body,
        grid=(nblk,),
        in_specs=[col(Z_GB), col(Z_GC), col(Z_XIN), w_spec, col(MLA_W)],
        out_specs=[col(0), col(0), col(0), w_spec],
        out_shape=[piece, piece, piece, jax.ShapeDtypeStruct((3, CONV_W), F32)],
        compiler_params=_params("parallel"),
        name=name,
    )(z, z, z, conv_w, dcat)


def _gate_fwd(cat, z, name):
    rows = cat.shape[0]
    tr = _tile(rows, 512, 16)
    tc = 512
    g0 = Z_GATE // tc

    def body(c_ref, g_ref, y_ref):
        g = g_ref[...]
        y_ref[...] = (c_ref[...].astype(F32) * (g * jax.nn.sigmoid(g))).astype(y_ref.dtype)

    blk = pl.BlockSpec((tr, tc), lambda i, j: (i, j))
    return pl.pallas_call(
        body,
        grid=(rows // tr, MIX_W // tc),
        in_specs=[blk, pl.BlockSpec((tr, tc), lambda i, j: (i, g0 + j))],
        out_specs=blk,
        out_shape=jax.ShapeDtypeStruct((rows, MIX_W), BF16),
        compiler_params=_params("parallel", "parallel"),
        name=name,
    )(cat, z)


def _out_proj_dx_gate_bwd(do, w_o, cat, z, name):
    rows, k = do.shape
    tm = _tile(rows, 512, 16)
    tn = _tile(MIX_W, 1024, LANES)
    g0 = Z_GATE // tn

    def body(do_ref, w_ref, c_ref, g_ref, dcat_ref, dgate_ref):
        dy = lax.dot_general(do_ref[...], w_ref[...], (((1,), (1,)), ((), ())), preferred_element_type=F32)
        g = g_ref[...]
        sg = jax.nn.sigmoid(g)
        dcat_ref[...] = (dy * (g * sg)).astype(dcat_ref.dtype)
        dgate_ref[...] = (dy * c_ref[...].astype(F32) * (sg * (1.0 + g * (1.0 - sg)))).astype(dgate_ref.dtype)

    blk = pl.BlockSpec((tm, tn), lambda i, j: (i, j))
    out = jax.ShapeDtypeStruct((rows, MIX_W), BF16)
    return pl.pallas_call(
        body,
        grid=(rows // tm, MIX_W // tn),
        in_specs=[pl.BlockSpec((tm, k), lambda i, j: (i, 0)), pl.BlockSpec((tn, k), lambda i, j: (j, 0)), blk,
                  pl.BlockSpec((tm, tn), lambda i, j: (i, g0 + j))],
        out_specs=[blk, blk],
        out_shape=[out, out],
        compiler_params=_params("parallel", "parallel"),
        name=name,
    )(do, w_o, cat, z)


def _loss_head(y, target, name):
    rows, width = y.shape
    tr = _tile(rows, 256, 8)

    def body(y_ref, t_ref, g_ref, loss_ref):
        i = pl.program_id(0)
        d = y_ref[...] - t_ref[...]
        g_ref[...] = d / width
        part = 0.5 * jnp.sum(jnp.mean(d * d, axis=-1, keepdims=True), axis=0, keepdims=True)
        part = jnp.broadcast_to(part, loss_ref.shape)

        @pl.when(i == 0)
        def _():
            loss_ref[...] = part

        @pl.when(i > 0)
        def _():
            loss_ref[...] += part

    row_spec = pl.BlockSpec((tr, width), lambda i: (i, 0))
    return pl.pallas_call(
        body,
        grid=(rows // tr,),
        in_specs=[row_spec, row_spec],
        out_specs=[row_spec, pl.BlockSpec((1, LANES), lambda i: (0, 0))],
        out_shape=[jax.ShapeDtypeStruct((rows, width), F32), jax.ShapeDtypeStruct((1, LANES), F32)],
        compiler_params=_params("arbitrary"),
        name=name,
    )(y, target)


CHIP_FLIPS = ((1, 0), (0, 1), (1, 1))
ANY = pl.BlockSpec(memory_space=pl.ANY)


def _chip_copies(pieces, sems, n_slot):
    send_sems, recv_sems, local_sems = sems
    x, y, c = lax.axis_index("x"), lax.axis_index("y"), lax.axis_index("c")
    me = 2 * x + y

    def remote(j, k, a, src, dst):
        fx, fy = CHIP_FLIPS[k]
        return pltpu.make_async_remote_copy(
            src_ref=src, dst_ref=dst, send_sem=send_sems.at[n_slot * k + a], recv_sem=recv_sems.at[n_slot * k + a],
            device_id=((j // 2) ^ fx, (j % 2) ^ fy, c), device_id_type=MESH_ID)

    def peer(j, k):
        fx, fy = CHIP_FLIPS[k]
        return 2 * ((j // 2) ^ fx) + ((j % 2) ^ fy)

    def start_as(j):
        def run():
            for a, (src, dst) in enumerate(pieces(j, j)):
                pltpu.make_async_copy(src, dst, local_sems.at[a]).start()
            for k in range(len(CHIP_FLIPS)):
                for a, (src, dst) in enumerate(pieces(j, peer(j, k))):
                    remote(j, k, a, src, dst).start()
        return run

    def wait_as(j):
        def run():
            for a, (src, dst) in enumerate(pieces(j, j)):
                pltpu.make_async_copy(src, dst, local_sems.at[a]).wait()
            for k in range(len(CHIP_FLIPS)):
                for a, (src, dst) in enumerate(pieces(j, peer(j, k))):
                    remote(j, k, a, src, dst).wait_send()
                for a, (src, dst) in enumerate(pieces(peer(j, k), j)):
                    remote(j, k, a, src, dst).wait_recv()
        return run

    def start():
        for j in range(N_CHIPS):
            pl.when(me == j)(start_as(j))

    def wait():
        for j in range(N_CHIPS):
            pl.when(me == j)(wait_as(j))

    return start, wait


IN_PIECES = ((0, Q_RANK, Z_QLAT), (Q_RANK, KV_RANK, Z_KVLAT), (Q_RANK + KV_RANK, ROPE, Z_KPE),
             (Q_RANK + KV_RANK + ROPE, CONV_W, Z_GB), (Q_RANK + KV_RANK + ROPE + CONV_W, CONV_W, Z_GC),
             (Q_RANK + KV_RANK + ROPE + 2 * CONV_W, CONV_W, Z_XIN), (Q_RANK + KV_RANK + ROPE + 3 * CONV_W, MEM_W, Z_QMEM),
             (Q_RANK + KV_RANK + ROPE + 3 * CONV_W + MEM_W, MIX_W, Z_GATE))
IN_SHARD = IN_COLS // N_CHIPS


def _in_segments(j):
    lo, hi = j * IN_SHARD, (j + 1) * IN_SHARD
    segs = []
    for r0, width, z0 in IN_PIECES:
        a, b = max(lo, r0), min(hi, r0 + width)
        if a < b:
            segs.append((a - lo, z0 + a - r0, b - a))
    return segs


N_SLOT = 11


def _gather_plan(l, shards, zero_rows, part="all"):
    s_in, s_uq, s_ukv, s_conv, s_mk, s_mv, s_o = shards
    ukv_c, mk_r, mk_c, o_r = s_ukv.shape[2], s_mk.shape[1], s_mk.shape[2], s_o.shape[1]
    stack = lambda s: jax.ShapeDtypeStruct((N_CHIPS,) + s.shape[1:], s.dtype)
    in_ins, in_outs = [s_in, zero_rows], [jax.ShapeDtypeStruct((Z_COLS, s_in.shape[2]), s_in.dtype)]
    rest_ins = [s_uq, s_ukv, s_conv, s_mk, s_mv, s_o]
    rest_outs = [stack(s_uq), jax.ShapeDtypeStruct((s_ukv.shape[1], N_CHIPS * ukv_c), s_ukv.dtype), stack(s_conv),
                 jax.ShapeDtypeStruct((N_CHIPS * mk_r, 2 * mk_c), s_mk.dtype),
                 jax.ShapeDtypeStruct((N_CHIPS * o_r, s_o.shape[2]), s_o.dtype)]
    with_in, with_rest = part != "rest", part != "in"

    def build(ins, outs, sems):
        ins, outs = list(ins), list(outs)
        if with_in:
            r_in, r_zero, f_in = ins.pop(0), ins.pop(0), outs.pop(0)
        if with_rest:
            r_uq, r_ukv, r_conv, r_mk, r_mv, r_o = ins
            g_uq, f_ukv, g_conv, f_mkv, f_o = outs

        def pieces(j, t):
            out = []
            if with_in:
                out += [(r_in.at[l, pl.ds(so, n), :], f_in.at[pl.ds(zo, n), :]) for so, zo, n in _in_segments(j)]
            if with_rest:
                out += [(r_uq.at[l], g_uq.at[j]), (r_ukv.at[l], f_ukv.at[:, pl.ds(j * ukv_c, ukv_c)]),
                        (r_conv.at[l], g_conv.at[j]),
                        (r_mk.at[l], f_mkv.at[pl.ds(j * mk_r, mk_r), pl.ds(0, mk_c)]),
                        (r_mv.at[l], f_mkv.at[pl.ds(j * mk_r, mk_r), pl.ds(mk_c, mk_c)]),
                        (r_o.at[l], f_o.at[pl.ds(j * o_r, o_r), :])]
            if with_in and j == t:
                out.append((r_zero, f_in.at[pl.ds(Z_KPE + ROPE, LANES - ROPE), :]))
            return out

        return _chip_copies(pieces, sems, N_SLOT)

    ins = (in_ins if with_in else []) + (rest_ins if with_rest else [])
    outs = (in_outs if with_in else []) + (rest_outs if with_rest else [])
    return _CommPlan(ins, outs, build, len(CHIP_FLIPS) * N_SLOT)


def _scatter_plan(dwt_in, c_uq, dw_ukv, c_conv, dw_mkv, dw_o, part="all"):
    ukv_c, mk_r, mk_c, o_r = dw_ukv.shape[1] // N_CHIPS, dw_mkv.shape[0] // N_CHIPS, dw_mkv.shape[1] // 2, dw_o.shape[0] // N_CHIPS
    with_in, with_rest = part != "rest", part != "in"
    in_outs = [jax.ShapeDtypeStruct((N_CHIPS, IN_SHARD, D_MODEL), BF16)]
    rest_ins = [c_uq, dw_ukv, c_conv, dw_mkv, dw_o]
    rest_outs = [jax.ShapeDtypeStruct(c_uq.shape, c_uq.dtype),
                 jax.ShapeDtypeStruct((N_CHIPS, dw_ukv.shape[0], ukv_c), dw_ukv.dtype),
                 jax.ShapeDtypeStruct(c_conv.shape, c_conv.dtype),
                 jax.ShapeDtypeStruct((N_CHIPS, mk_r, mk_c), dw_mkv.dtype), jax.ShapeDtypeStruct((N_CHIPS, mk_r, mk_c), dw_mkv.dtype),
                 jax.ShapeDtypeStruct((N_CHIPS, o_r, dw_o.shape[1]), dw_o.dtype)]

    def build(ins, outs, sems):
        ins, outs = list(ins), list(outs)
        if with_in:
            r_in, o_in = ins.pop(0), outs.pop(0)
        if with_rest:
            r_uq, r_ukv, r_conv, r_mkv, r_o = ins
            o_uq, o_ukv, o_conv, o_mk, o_mv, o_o = outs

        def pieces(j, t):
            out = []
            if with_in:
                out += [(r_in.at[pl.ds(zo, n), :], o_in.at[j, pl.ds(so, n), :]) for so, zo, n in _in_segments(t)]
            if with_rest:
                out += [(r_uq.at[t], o_uq.at[j]), (r_ukv.at[:, pl.ds(t * ukv_c, ukv_c)], o_ukv.at[j]),
                        (r_conv.at[t], o_conv.at[j]),
                        (r_mkv.at[pl.ds(t * mk_r, mk_r), pl.ds(0, mk_c)], o_mk.at[j]),
                        (r_mkv.at[pl.ds(t * mk_r, mk_r), pl.ds(mk_c, mk_c)], o_mv.at[j]),
                        (r_o.at[pl.ds(t * o_r, o_r), :], o_o.at[j])]
            return out

        return _chip_copies(pieces, sems, N_SLOT)

    ins = ([dwt_in] if with_in else []) + (rest_ins if with_rest else [])
    outs = (in_outs if with_in else []) + (rest_outs if with_rest else [])
    return _CommPlan(ins, outs, build, len(CHIP_FLIPS) * N_SLOT)


HBM = pl.BlockSpec(memory_space=pltpu.HBM)
SEM = pl.BlockSpec(memory_space=pltpu.SEMAPHORE)
SIDE_EFFECT = pltpu.SideEffectType.DATAFLOW_SIDE_EFFECTING


def _comm_start(plan, after, name):
    n_in, n_out = len(plan.ins), len(plan.out_shape)
    n_buf = n_in + n_out

    def body(*refs):
        bufs, sems, token = refs[:n_buf], refs[n_buf + 1:n_buf + 4], refs[-1]
        start, _ = plan.build(bufs[:n_in], bufs[n_in:], sems)
        start()
        token[...] = jnp.zeros_like(token)

    lands = [lax.empty(s.shape, s.dtype) for s in plan.out_shape]
    args = [pltpu.with_memory_space_constraint(a, pltpu.HBM) for a in list(plan.ins) + lands]
    res = pl.pallas_call(
        body,
        in_specs=[HBM] * n_buf + [ANY],
        out_specs=[SEM] * 3 + [HBM] * n_buf + [pl.BlockSpec(memory_space=pltpu.VMEM)],
        out_shape=plan.scratch() + [pltpu.HBM(a.shape, a.dtype) for a in args] + [jax.ShapeDtypeStruct((8, LANES), F32)],
        input_output_aliases={i: 3 + i for i in range(n_buf)},
        compiler_params=pltpu.CompilerParams(has_side_effects=SIDE_EFFECT),
        name=name,
    )(*args, after)
    return list(res[:3]), list(res[3:3 + n_buf]), res[-1]


def _comm_finish(plan, started, after, name):
    sems, bufs, _ = started
    n_in, n_out = len(plan.ins), len(plan.out_shape)
    n_buf = n_in + n_out

    def body(*refs):
        bufs_in, sem_refs = refs[:n_buf], refs[n_buf:n_buf + 3]
        _, wait = plan.build(bufs_in[:n_in], bufs_in[n_in:], sem_refs)
        wait()

    res = pl.pallas_call(
        body,
        in_specs=[HBM] * n_buf + [SEM] * 3 + [ANY],
        out_specs=[HBM] * n_buf,
        out_shape=[pltpu.HBM(b.shape, b.dtype) for b in bufs],
        input_output_aliases={i: i for i in range(n_buf)},
        compiler_params=pltpu.CompilerParams(has_side_effects=SIDE_EFFECT),
        name=name,
    )(*bufs, *sems, after)
    return list(res[n_in:])


def _comm_call(plan, name):
    n_in, n_out = len(plan.ins), len(plan.out_shape)

    def body(*refs):
        start, wait = plan.build(refs[:n_in], refs[n_in:n_in + n_out], refs[n_in + n_out:])
        start()
        wait()

    return list(pl.pallas_call(
        body,
        in_specs=[ANY] * n_in,
        out_specs=[ANY] * n_out,
        out_shape=plan.out_shape,
        scratch_shapes=plan.scratch(),
        name=name,
    )(*plan.ins))


def _sibling_plan(arrays):
    def build(ins, outs, sems):
        send_sems, recv_sems, _ = sems
        sibling = (lax.axis_index("x"), lax.axis_index("y"), 1 - lax.axis_index("c"))
        copies = [pltpu.make_async_remote_copy(src_ref=src, dst_ref=dst, send_sem=send_sems.at[a], recv_sem=recv_sems.at[a],
                                               device_id=sibling, device_id_type=MESH_ID)
                  for a, (src, dst) in enumerate(zip(ins, outs))]

        def start():
            for cp in copies:
                cp.start()

        def wait():
            for cp in copies:
                cp.wait()

        return start, wait

    return _CommPlan(arrays, [jax.ShapeDtypeStruct(v.shape, v.dtype) for v in arrays], build, len(arrays))


DEVICE_FLIPS = tuple((fx, fy, fc) for fx in (0, 1) for fy in (0, 1) for fc in (0, 1))[1:]


def _gather_all(v, name):
    def body(v_ref, out_ref, send_sems, recv_sems, local_sem):
        x, y, c = lax.axis_index("x"), lax.axis_index("y"), lax.axis_index("c")
        me = 4 * x + 2 * y + c
        local = pltpu.make_async_copy(v_ref, out_ref.at[me], local_sem)
        local.start()
        copies = [local]
        for k, (fx, fy, fc) in enumerate(DEVICE_FLIPS):
            cp = pltpu.make_async_remote_copy(
                src_ref=v_ref, dst_ref=out_ref.at[me], send_sem=send_sems.at[k], recv_sem=recv_sems.at[k],
                device_id=((x + fx) % 2, (y + fy) % 2, (c + fc) % 2), device_id_type=MESH_ID)
            cp.start()
            copies.append(cp)
        for cp in copies:
            cp.wait()

    return pl.pallas_call(
        body,
        in_specs=[ANY],
        out_specs=ANY,
        out_shape=jax.ShapeDtypeStruct((N_DEV,) + v.shape, v.dtype),
        scratch_shapes=[pltpu.SemaphoreType.DMA((N_DEV - 1,)), pltpu.SemaphoreType.DMA((N_DEV - 1,)), pltpu.SemaphoreType.DMA],
        name=name,
    )(v)


def _sum_slots(parts, name):
    n, rows, cols = parts.shape
    tr = _tile(rows, 256, 16)

    def body(p_ref, o_ref):
        acc = p_ref[0].astype(F32)
        for k in range(1, n):
            acc = acc + p_ref[k].astype(F32)
        o_ref[...] = acc

    return pl.pallas_call(
        body,
        grid=(rows // tr,),
        in_specs=[pl.BlockSpec((n, tr, cols), lambda i: (0, i, 0))],
        out_specs=pl.BlockSpec((tr, cols), lambda i: (i, 0)),
        out_shape=jax.ShapeDtypeStruct((rows, cols), F32),
        compiler_params=_params("parallel"),
        name=name,
    )(parts)


def _adamw_math(w, g, m, v):
    m_new = ADAM_B1 * m + (1.0 - ADAM_B1) * g
    v_new = ADAM_B2 * v + (1.0 - ADAM_B2) * jnp.square(g)
    m_hat = m_new / (1.0 - ADAM_B1 ** ADAM_STEP)
    v_hat = v_new / (1.0 - ADAM_B2 ** ADAM_STEP)
    return -ADAM_LR * (m_hat / (jnp.sqrt(v_hat) + ADAM_EPS) + ADAM_WD * w), m_new, v_new


def _adamw(w, g, m, v, name):
    rows, cols = w.shape
    tr = _tile(rows, 256, 8)

    def body(w_ref, g_ref, m_ref, v_ref, d_out, m_out, v_out):
        d_out[...], m_out[...], v_out[...] = _adamw_math(w_ref[...], g_ref[...], m_ref[...], v_ref[...])

    blk = pl.BlockSpec((tr, cols), lambda i: (i, 0))
    out = jax.ShapeDtypeStruct((rows, cols), F32)
    return pl.pallas_call(
        body,
        grid=(rows // tr,),
        in_specs=[blk] * 4,
        out_specs=[blk] * 3,
        out_shape=[out] * 3,
        compiler_params=_params("parallel"),
        name=name,
    )(w, g, m, v)


def _adamw_layer(l, w, g_a, g_b, m, v, prev, name):
    depth, rows, cols = w.shape
    tr = _tile(rows, 256, 8)

    def body(w_ref, ga_ref, gb_ref, m_ref, v_ref, *rest):
        g_out, d_out, m_out, v_out = rest[-4:]
        g = ga_ref[...] + gb_ref[...]
        g_out[...] = g
        d_out[...], m_out[...], v_out[...] = _adamw_math(w_ref[...], g, m_ref[...], v_ref[...])

    stacked = pl.BlockSpec((None, tr, cols), lambda i: (l, i, 0))
    flat = pl.BlockSpec((tr, cols), lambda i: (i, 0))
    in_specs = [stacked, flat, flat, stacked, stacked]
    args = [w, g_a, g_b, m, v]
    aliases = {}
    if prev is not None:
        in_specs += [ANY] * 4
        args += list(prev)
        aliases = {5 + k: k for k in range(4)}
    out = jax.ShapeDtypeStruct((depth, rows, cols), F32)
    return pl.pallas_call(
        body,
        grid=(rows // tr,),
        in_specs=in_specs,
        out_specs=[stacked] * 4,
        out_shape=[out] * 4,
        input_output_aliases=aliases,
        compiler_params=_params("parallel"),
        name=name,
    )(*args)


def _cols_from_shards(g):
    _, r, c = g.shape
    return jnp.transpose(g, (1, 0, 2)).reshape(r, N_CHIPS * c)


def _cols_to_shards(full):
    r, c4 = full.shape
    c = c4 // N_CHIPS
    return jnp.transpose(full.reshape(r, N_CHIPS, c), (1, 0, 2))


IN_ORDER = (Q_RANK, KV_RANK, ROPE, CONV_W, CONV_W, CONV_W, MEM_W, MIX_W)


def _w_in_to_z_layout(w_in):
    edges = [0]
    for width in IN_ORDER:
        edges.append(edges[-1] + width)
    q_lat, kv_lat, k_pe, gb, gc, xin, q_mem, gate = [w_in[..., edges[i]:edges[i + 1]] for i in range(8)]
    pad = jnp.zeros(k_pe.shape[:-1] + (LANES - ROPE,), w_in.dtype)
    return jnp.concatenate([gate, q_lat, kv_lat, k_pe, pad, gb, gc, xin, q_mem], axis=-1)


def _w_in_from_z_layout(wz):
    cut = lambda c0, width: wz[..., c0:c0 + width]
    return jnp.concatenate(
        [cut(Z_QLAT, Q_RANK), cut(Z_KVLAT, KV_RANK), cut(Z_KPE, ROPE), cut(Z_GB, CONV_W), cut(Z_GC, CONV_W),
         cut(Z_XIN, CONV_W), cut(Z_QMEM, MEM_W), cut(Z_GATE, MIX_W)], axis=-1)


def _w_uq_pad(w_uq):
    r, _ = w_uq.shape
    w = jnp.pad(w_uq.reshape(r, MLA_HEADS, QK_HEAD), ((0, 0), (0, 0), (0, QPAD - QK_HEAD)))
    return w.reshape(r, MLA_HEADS * QPAD)


def _w_uq_unpad(w):
    r, _ = w.shape
    return w.reshape(r, MLA_HEADS, QPAD)[..., :QK_HEAD].reshape(r, MLA_HEADS * QK_HEAD)


def _rope_tables(positions):
    inv_freq = 1.0 / (ROPE_THETA ** (jnp.arange(0, ROPE, 2, dtype=F32) / ROPE))
    ang = positions.astype(F32)[:, None] * inv_freq
    cos, sin = jnp.cos(ang), jnp.sin(ang)
    s = positions.shape[0]
    zero = jnp.zeros((s, HALF_ROPE), F32)
    pad = jnp.zeros((s, LANES - ROPE), F32)
    kc = jnp.concatenate([cos, cos, pad], axis=-1)
    ka = jnp.concatenate([-sin, zero, pad], axis=-1)
    kb = jnp.concatenate([zero, sin, pad], axis=-1)
    qc = jnp.concatenate([jnp.ones((s, NOPE), F32), kc], axis=-1)
    qa = jnp.concatenate([jnp.zeros((s, NOPE), F32), ka], axis=-1)
    qb = jnp.concatenate([jnp.zeros((s, NOPE), F32), kb], axis=-1)
    return (qc, qa, qb), (kc, ka, kb)


def _layer_weights(gathered):
    return (gathered[0],) + _late_weights(gathered[1:])


def _late_weights(gathered):
    g_uq, w_ukv, g_conv, w_mkv, w_o = gathered
    return (_w_uq_pad(_cols_from_shards(g_uq)), w_ukv, _cols_from_shards(g_conv), w_mkv, w_o)


def _layer_fwd(l, x, mem, wts, gains, tabs, comm, late=None):
    wt_in = wts[0]
    g_pre, g_q, g_kv, g_mem, g_post = gains
    q_tab, k_tab = tabs
    tag = f"l{l}_"
    h = _rmsnorm_fwd(x, g_pre, 0, D_MODEL, tag + "pre_norm")
    z = _matmul(h, wt_in, "nt", F32, tag + "in_proj", tm_cap=1024, tn_cap=1664)
    w_uq, w_ukv, conv_w, w_mkv, w_o = wts[1:] if late is None else late(z)
    wts = (wt_in, w_uq, w_ukv, conv_w, w_mkv, w_o)
    qn = _rmsnorm_fwd(z, g_q, Z_QLAT, Q_RANK, tag + "q_norm")
    kvn = _rmsnorm_fwd(z, g_kv, Z_KVLAT, KV_RANK, tag + "kv_norm")
    q_raw = _matmul(qn, w_uq, "nn", F32, tag + "uq", tm_cap=1024)
    kv = _matmul(kvn, w_ukv, "nn", BF16, tag + "ukv")
    kpe = _rope(z, *k_tab, Z_KPE, LANES, 1, tag + "k_rope")
    a_out, a_lse, arrived = _attn_fwd(q_raw, kv, kpe, kv, q_tab, MLA_HEADS, QPAD, 0, 0, 2, 1, 2, QK_HEAD ** -0.5, 512,
                                      tag + "mla_fwd", comm)
    c_out = _conv_fwd(z, conv_w, tag + "conv_fwd")
    mem_n = _rmsnorm_fwd(mem, g_mem, 0, D_MODEL, tag + "mem_norm")
    mkv = _matmul(mem_n, w_mkv, "nn", BF16, tag + "mem_kv")
    m_out, m_lse, _ = _attn_fwd(z, mkv, None, mkv, None, MEM_HEADS, LANES, Z_QMEM // LANES, 0, 1, MEM_HEADS, 1,
                                MEM_HEAD ** -0.5, 1024, tag + "mem_fwd")
    cat = jnp.concatenate([a_out, c_out, m_out], axis=-1)
    y = _gate_fwd(cat, z, tag + "gate_fwd")
    o = _matmul(y, w_o, "nn", F32, tag + "out_proj", tm_cap=1024)
    x_new = _post_norm_residual(x, o, g_post, tag + "post_norm")
    saved = (x, h, z, qn, kvn, q_raw, kv, kpe, a_lse, mem_n, mkv, m_lse, cat, y, o)
    return x_new, saved, arrived


def _layer_bwd(l, g, mem, saved, wts, gains, tabs_bwd, comm, split_exchange=False):
    wt_in, w_uq, w_ukv, conv_w, w_mkv, w_o = wts
    g_pre, g_q, g_kv, g_mem, g_post = gains
    q_tab, k_tab_bwd = tabs_bwd
    x, h, z, qn, kvn, q_raw, kv, kpe, a_lse, mem_n, mkv, m_lse, cat, y, o = saved
    tag = f"l{l}_"
    do, dg_post = _rmsnorm_bwd(o, g_post, g, None, 0, D_MODEL, BF16, tag + "post_norm_bwd")
    dcat, dgate = _out_proj_dx_gate_bwd(do, w_o, cat, z, tag + "out_proj_dx")
    dw_o = _matmul(y, do, "tn", BF16, tag + "out_proj_dw", tm_cap=1024)
    dq, dkv, dkpe_h, arrived = _attn_bwd(q_raw, kv, kpe, kv, cat, dcat, a_lse, q_tab, MLA_HEADS, QPAD, 0, 0, 2, 1, 2, 0,
                                         QK_HEAD ** -0.5, 512, tag + "mla_bwd", comm)
    dkpe = _kpe_grad(dkpe_h, *k_tab_bwd, MLA_HEADS, tag + "k_rope_bwd")
    dw_ukv = _matmul(kvn, dkv, "tn", BF16, tag + "ukv_dw")
    dkvn = _matmul(dkv, w_ukv, "nt", F32, tag + "ukv_dx")
    dkv_lat, dg_kv = _rmsnorm_bwd(z, g_kv, dkvn, None, Z_KVLAT, KV_RANK, BF16, tag + "kv_norm_bwd")
    dw_uq = _matmul(qn, dq, "tn", BF16, tag + "uq_dw")
    dqn = _matmul(dq, w_uq, "nt", F32, tag + "uq_dx")
    dq_lat, dg_q = _rmsnorm_bwd(z, g_q, dqn, None, Z_QLAT, Q_RANK, BF16, tag + "q_norm_bwd")
    dgb, dgc, dxin, dconv_w = _conv_bwd(z, conv_w, dcat, tag + "conv_bwd")
    dq_mem, dmk, dmv, _ = _attn_bwd(z, mkv, None, mkv, cat, dcat, m_lse, None, MEM_HEADS, LANES, Z_QMEM // LANES, 0, 1,
                                    MEM_HEADS, 1, (MLA_W + CONV_W) // LANES, MEM_HEAD ** -0.5, 1024, tag + "mem_bwd")
    dmkv = jnp.concatenate([dmk, dmv], axis=-1)
    dw_mkv = _matmul(mem_n, dmkv, "tn", BF16, tag + "mem_kv_dw")
    dmem_n = _matmul(dmkv, w_mkv, "nt", F32, tag + "mem_kv_dx")
    _, dg_mem = _rmsnorm_bwd(mem, g_mem, dmem_n, None, 0, D_MODEL, BF16, tag + "mem_norm_bwd")
    others = (_cols_to_shards(_w_uq_unpad(dw_uq)), dw_ukv, _cols_to_shards(dconv_w), dw_mkv, dw_o)
    early = None
    if split_exchange:
        early_plan = _scatter_plan(None, *others, part="rest")
        early = (early_plan, _comm_start(early_plan, dmem_n, tag + "exchange_rest_start"))
        g_pre = g_pre + early[1][2][0:1, 0:1]
    dz = jnp.concatenate([dgate, dq_lat, dkv_lat, dkpe, dgb, dgc, dxin, dq_mem], axis=-1)
    dwt_in = _matmul(dz, h, "tn", BF16, tag + "in_proj_dw", tm_cap=1664, tk_cap=1024)
    dh = _matmul(dz, wt_in, "nn", F32, tag + "in_proj_dx", tm_cap=1024, tk_cap=1664)
    dx, dg_pre = _rmsnorm_bwd(x, g_pre, dh, g, 0, D_MODEL, F32, tag + "pre_norm_bwd")
    contrib = _scatter_plan(dwt_in, *others, part="in" if split_exchange else "all")
    return dx, contrib, (dg_pre, dg_q, dg_kv, dg_mem, dg_post), early


GAIN_WIDTHS = (D_MODEL, Q_RANK, KV_RANK, D_MODEL, D_MODEL)


def _pack_gains(parts):
    return jnp.concatenate([p.reshape(-1) for p in parts]).reshape(-1, LANES)


def _unpack_gains(packed, depth):
    flat = packed.reshape(-1)
    out, at = [], 0
    for width in GAIN_WIDTHS:
        out.append(flat[at:at + depth * width].reshape(depth, width))
        at += depth * width
    return out


def kernel(x, mem, positions, pre_norm_g, w_in, q_norm_g, w_uq, kv_norm_g, w_ukv, conv_w, mem_norm_g, w_mk, w_mv, w_o, post_norm_g, loss_target, m_pre_norm_g, m_w_in, m_q_norm_g, m_w_uq, m_kv_norm_g, m_w_ukv, m_conv_w, m_mem_norm_g, m_w_mk, m_w_mv, m_w_o, m_post_norm_g, v_pre_norm_g, v_w_in, v_q_norm_g, v_w_uq, v_kv_norm_g, v_w_ukv, v_conv_w, v_mem_norm_g, v_w_mk, v_w_mv, v_w_o, v_post_norm_g):
    depth = w_in.shape[0]
    x0, mem0, target = x[0], mem[0], loss_target[0]
    tabs = _rope_tables(positions[0])
    tabs_bwd = (tabs[0], (tabs[1][0], -tabs[1][1], -tabs[1][2]))

    flip = lambda t: jnp.transpose(t, (0, 2, 1))
    w_in, m_w_in, v_w_in = flip(w_in), flip(m_w_in), flip(v_w_in)
    shards = [w_in.astype(BF16), w_uq.astype(BF16), w_ukv.astype(BF16), conv_w, w_mk.astype(BF16), w_mv.astype(BF16),
              w_o.astype(BF16)]
    zero_rows = lambda: jnp.zeros((LANES - ROPE, D_MODEL), BF16)

    def layer_gains(l):
        return tuple(g[l][None, :] for g in (pre_norm_g, q_norm_g, kv_norm_g, mem_norm_g, post_norm_g))

    wts, saved = [None] * depth, [None] * depth
    first = [s[0:1] for s in shards]
    plan_in, plan_rest = _gather_plan(0, first, zero_rows(), "in"), _gather_plan(0, first, zero_rows(), "rest")
    started_in = _comm_start(plan_in, positions, "l0_gather_in_start")
    started_rest = _comm_start(plan_rest, started_in[2], "l0_gather_rest_start")
    wts[0] = tuple(_comm_finish(plan_in, started_in, started_rest[2], "l0_gather_in_wait"))

    next_gather = {}

    def start_next_gather(l, after):
        plan = _gather_plan(0, [s[l + 1:l + 2] for s in shards], zero_rows())
        next_gather[l + 1] = (plan, _comm_start(plan, after, f"l{l + 1}_gather_start"))
        return next_gather[l + 1][1][2][0:1, 0:1]

    def rest_of_layer0(z):
        got = _late_weights(_comm_finish(plan_rest, started_rest, z, "l0_gather_rest_wait"))
        wts[0] = wts[0] + got
        if depth > 1:
            got = (got[0] + start_next_gather(0, got[4]).astype(BF16),) + got[1:]
        return got

    act = x0
    for l in range(depth):
        gains = layer_gains(l)
        if 0 < l < depth - 1:
            gains = (gains[0] + start_next_gather(l, wts[l][5]),) + gains[1:]
        act, saved[l], _ = _layer_fwd(l, act, mem0, wts[l], gains, tabs, None, rest_of_layer0 if l == 0 else None)
        if l + 1 < depth:
            plan, started = next_gather[l + 1]
            wts[l + 1] = _layer_weights(_comm_finish(plan, started, act, f"l{l + 1}_gather_wait"))
    grad, loss_part = _loss_head(act, target, "loss_head")
    loss = lax.psum(loss_part[0, 0], ("x", "y", "c"))

    names = ("w_in", "w_uq", "w_ukv", "conv_w", "w_mk", "w_mv", "w_o")
    w_shards = (w_in, w_uq, w_ukv, conv_w, w_mk, w_mv, w_o)
    m_shards = (m_w_in, m_w_uq, m_w_ukv, m_conv_w, m_w_mk, m_w_mv, m_w_o)
    v_shards = (v_w_in, v_w_uq, v_w_ukv, v_conv_w, v_w_mk, v_w_mv, v_w_o)
    stacked = [None] * len(names)

    def sum_and_send(l, received):
        partial = [_sum_slots(r, f"l{l}_grad_sum_{names[i]}") for i, r in enumerate(received)]
        plan = _sibling_plan(partial)
        return l, partial, plan, _comm_start(plan, partial[0], f"l{l}_sibling_start")

    def receive_and_update(state, after):
        l, partial, plan, started = state
        other = _comm_finish(plan, started, after, f"l{l}_sibling_wait")
        for i, name in enumerate(names):
            stacked[i] = _adamw_layer(l, w_shards[i], partial[i], other[i], m_shards[i], v_shards[i], stacked[i],
                                      f"l{l}_adamw_{name}")

    dgs = [None] * depth
    pending = None
    in_flight = None
    for l in reversed(range(depth)):
        gains = layer_gains(l)
        for token in ([pending[1][2]] if pending else []) + ([in_flight[3][2]] if in_flight else []):
            gains = gains[:4] + (gains[4] + token[0:1, 0:1],)
        grad, contrib, dgs[l], early = _layer_bwd(l, grad, mem0, saved[l], wts[l], gains, tabs_bwd, None, l == 0)
        if in_flight is not None:
            receive_and_update(in_flight, grad)
            in_flight = None
        if pending is not None:
            in_flight = sum_and_send(l + 1, _comm_finish(pending[0], pending[1], grad, f"l{l + 1}_exchange_wait"))
        if l > 0:
            pending = (contrib, _comm_start(contrib, grad, f"l{l}_exchange_start"))
    got_in = _comm_call(contrib, "l0_grad_exchange_in")
    last = sum_and_send(0, got_in + _comm_finish(early[0], early[1], got_in[0], "l0_exchange_rest_wait"))
    if in_flight is not None:
        receive_and_update(in_flight, last[1][0])
    receive_and_update(last, stacked[0][0] if depth > 1 else last[1][0])
    grad_x = grad[None]
    results = {name: tuple(stacked[i]) for i, name in enumerate(names)}
    results["w_in"] = tuple(flip(t) for t in results["w_in"])

    gain_names = ("pre_norm_g", "q_norm_g", "kv_norm_g", "mem_norm_g", "post_norm_g")
    dg_packed = _pack_gains([jnp.concatenate([dgs[l][i] for l in range(depth)], axis=0) for i in range(5)])
    dg_total = _sum_slots(_gather_all(dg_packed, "gain_gather"), "gain_sum")
    gain_outs = (dg_total,) + tuple(_adamw(
        _pack_gains((pre_norm_g, q_norm_g, kv_norm_g, mem_norm_g, post_norm_g)), dg_total,
        _pack_gains((m_pre_norm_g, m_q_norm_g, m_kv_norm_g, m_mem_norm_g, m_post_norm_g)),
        _pack_gains((v_pre_norm_g, v_q_norm_g, v_kv_norm_g, v_mem_norm_g, v_post_norm_g)), "adamw_gains"))
    gain_outs = [_unpack_gains(t, depth) for t in gain_outs]
    for i, name in enumerate(gain_names):
        results[name] = tuple(gain_outs[k][i] for k in range(4))

    order = ("pre_norm_g", "w_in", "q_norm_g", "w_uq", "kv_norm_g", "w_ukv", "conv_w", "mem_norm_g", "w_mk", "w_mv", "w_o",
             "post_norm_g")
    out = [loss, grad_x]
    for k in range(4):
        out += [results[name][k] for name in order]
    return tuple(out)
```

```python
import functools

import jax
import jax.numpy as jnp
from jax import lax
from jax.experimental import pallas as pl
from jax.experimental.pallas import tpu as pltpu

F32 = jnp.float32
BF16 = jnp.bfloat16
MESH_ID = pl.DeviceIdType.MESH

D_MODEL = 2048
EPS = 1e-6
LOG2_E = 1.4426950408889634
ROPE_THETA = 10000.0
MLA_HEADS = 8
NOPE = 128
ROPE = 64
HALF_ROPE = ROPE // 2
QK_HEAD = NOPE + ROPE
V_HEAD = 128
Q_RANK = 512
KV_RANK = 256
CONV_W = 512
MEM_HEADS = 4
MEM_HEAD = 128
MEM_W = MEM_HEADS * MEM_HEAD
MLA_W = MLA_HEADS * V_HEAD
MIX_W = MLA_W + CONV_W + MEM_W
IN_COLS = Q_RANK + KV_RANK + ROPE + 3 * CONV_W + MEM_W + MIX_W
N_CHIPS = 4
N_DEV = 8

LANES = 128
VMEM_LIMIT_BYTES = 56 * 1024 * 1024

QPAD = 2 * LANES
Z_GATE = 0
Z_QLAT = Z_GATE + MIX_W
Z_KVLAT = Z_QLAT + Q_RANK
Z_KPE = Z_KVLAT + KV_RANK
Z_GB = Z_KPE + LANES
Z_GC = Z_GB + CONV_W
Z_XIN = Z_GC + CONV_W
Z_QMEM = Z_XIN + CONV_W
Z_COLS = Z_QMEM + MEM_W

ADAM_LR = 0.001
ADAM_B1 = 0.9
ADAM_B2 = 0.999
ADAM_EPS = 1e-08
ADAM_WD = 0.01
ADAM_STEP = 10


def _tile(dim, cap, unit):
    if dim <= cap:
        return dim
    t = (cap // unit) * unit
    while t >= unit:
        if dim % t == 0:
            return t
        t -= unit
    raise ValueError(f"no tile of {dim} under {cap} in units of {unit}")


def _params(*semantics):
    return pltpu.CompilerParams(dimension_semantics=semantics, vmem_limit_bytes=VMEM_LIMIT_BYTES)


def _matmul(a, b, mode, out_dtype, name, tm_cap=512, tn_cap=1024, tk_cap=2048, after=None):
    if mode == "nn":
        (m, k), (k2, n) = a.shape, b.shape
    elif mode == "nt":
        (m, k), (n, k2) = a.shape, b.shape
    else:
        (k, m), (k2, n) = a.shape, b.shape
    assert k == k2, (a.shape, b.shape, mode)
    tm = _tile(m, tm_cap, LANES if mode == "tn" else 16)
    tn = _tile(n, tn_cap, LANES)
    tk = _tile(k, tk_cap, LANES if mode != "tn" else 16)
    nk = k // tk
    if mode == "nn":
        a_spec = pl.BlockSpec((tm, tk), lambda i, j, kk: (i, kk))
        b_spec = pl.BlockSpec((tk, tn), lambda i, j, kk: (kk, j))
        dims = (((1,), (0,)), ((), ()))
    elif mode == "nt":
        a_spec = pl.BlockSpec((tm, tk), lambda i, j, kk: (i, kk))
        b_spec = pl.BlockSpec((tn, tk), lambda i, j, kk: (j, kk))
        dims = (((1,), (1,)), ((), ()))
    else:
        a_spec = pl.BlockSpec((tk, tm), lambda i, j, kk: (kk, i))
        b_spec = pl.BlockSpec((tk, tn), lambda i, j, kk: (kk, j))
        dims = (((0,), (0,)), ((), ()))

    def body(a_ref, b_ref, *rest):
        o_ref, scratch = (rest[1], rest[2:]) if after is not None else (rest[0], rest[1:])
        part = lax.dot_general(a_ref[...].astype(BF16), b_ref[...].astype(BF16), dims, preferred_element_type=F32)
        if nk == 1:
            o_ref[...] = part.astype(o_ref.dtype)
            return
        (acc_ref,) = scratch
        kk = pl.program_id(2)

        @pl.when(kk == 0)
        def _():
            acc_ref[...] = part

        @pl.when(kk > 0)
        def _():
            acc_ref[...] += part

        @pl.when(kk == nk - 1)
        def _():
            o_ref[...] = acc_ref[...].astype(o_ref.dtype)

    return pl.pallas_call(
        body,
        grid=(m // tm, n // tn, nk),
        in_specs=[a_spec, b_spec] + ([] if after is None else [pl.BlockSpec(memory_space=pl.ANY)]),
        out_specs=pl.BlockSpec((tm, tn), lambda i, j, kk: (i, j)),
        out_shape=jax.ShapeDtypeStruct((m, n), out_dtype),
        scratch_shapes=[] if nk == 1 else [pltpu.VMEM((tm, tn), F32)],
        compiler_params=_params("parallel", "parallel", "arbitrary"),
        name=name,
    )(*([a, b] if after is None else [a, b, after]))


def _rmsnorm_fwd(x, gain, col0, width, name):
    rows = x.shape[0]
    tr = _tile(rows, 512, 16)
    cb = col0 // width
    assert cb * width == col0

    def body(x_ref, g_ref, o_ref):
        xv = x_ref[...]
        r = lax.rsqrt(jnp.mean(xv * xv, axis=-1, keepdims=True) + EPS)
        o_ref[...] = (xv * r * g_ref[...]).astype(o_ref.dtype)

    return pl.pallas_call(
        body,
        grid=(rows // tr,),
        in_specs=[pl.BlockSpec((tr, width), lambda i: (i, cb)), pl.BlockSpec((1, width), lambda i: (0, 0))],
        out_specs=pl.BlockSpec((tr, width), lambda i: (i, 0)),
        out_shape=jax.ShapeDtypeStruct((rows, width), BF16),
        compiler_params=_params("parallel"),
        name=name,
    )(x, gain)


def _rmsnorm_bwd(x, gain, dy, resid, col0, width, out_dtype, name):
    rows = x.shape[0]
    tr = _tile(rows, 256, 16)
    cb = col0 // width
    assert cb * width == col0
    has_resid = resid is not None

    def body(*refs):
        if has_resid:
            x_ref, g_ref, dy_ref, res_ref, dx_ref, dg_ref = refs
        else:
            x_ref, g_ref, dy_ref, dx_ref, dg_ref = refs
        i = pl.program_id(0)
        xv = x_ref[...]
        dyv = dy_ref[...].astype(F32)
        r = lax.rsqrt(jnp.mean(xv * xv, axis=-1, keepdims=True) + EPS)
        xr = xv * r
        dyg = dyv * g_ref[...]
        c = jnp.mean(dyg * xr, axis=-1, keepdims=True)
        dx = r * (dyg - xr * c)
        if has_resid:
            dx = dx + res_ref[...]
        dx_ref[...] = dx.astype(dx_ref.dtype)
        part = jnp.sum(dyv * xr, axis=0, keepdims=True)

        @pl.when(i == 0)
        def _():
            dg_ref[...] = part

        @pl.when(i > 0)
        def _():
            dg_ref[...] += part

    row_spec = pl.BlockSpec((tr, width), lambda i: (i, 0))
    in_specs = [pl.BlockSpec((tr, width), lambda i: (i, cb)), pl.BlockSpec((1, width), lambda i: (0, 0)), row_spec]
    args = [x, gain, dy]
    if has_resid:
        in_specs.append(row_spec)
        args.append(resid)
    return pl.pallas_call(
        body,
        grid=(rows // tr,),
        in_specs=in_specs,
        out_specs=[row_spec, pl.BlockSpec((1, width), lambda i: (0, 0))],
        out_shape=[jax.ShapeDtypeStruct((rows, width), out_dtype), jax.ShapeDtypeStruct((1, width), F32)],
        compiler_params=_params("arbitrary"),
        name=name,
    )(*args)


def _post_norm_residual(x, o, gain, name):
    rows, width = x.shape
    tr = _tile(rows, 256, 8)

    def body(x_ref, o_ref, g_ref, out_ref):
        ov = o_ref[...]
        r = lax.rsqrt(jnp.mean(ov * ov, axis=-1, keepdims=True) + EPS)
        out_ref[...] = x_ref[...] + ov * r * g_ref[...]

    row_spec = pl.BlockSpec((tr, width), lambda i: (i, 0))
    return pl.pallas_call(
        body,
        grid=(rows // tr,),
        in_specs=[row_spec, row_spec, pl.BlockSpec((1, width), lambda i: (0, 0))],
        out_specs=row_spec,
        out_shape=jax.ShapeDtypeStruct((rows, width), F32),
        compiler_params=_params("parallel"),
        name=name,
    )(x, o, gain)


def _rope(x, tab_c, tab_a, tab_b, col0, width, heads, name):
    rows = x.shape[0]
    tr = _tile(rows, 512, 16)
    cb = col0 // width
    assert cb * width == col0

    def body(x_ref, c_ref, a_ref, b_ref, o_ref):
        xv = x_ref[...].astype(F32)
        up = pltpu.roll(xv, width - HALF_ROPE, 1)
        down = pltpu.roll(xv, HALF_ROPE, 1)
        o_ref[...] = (xv * c_ref[...] + up * a_ref[...] + down * b_ref[...]).astype(o_ref.dtype)

    tab_spec = pl.BlockSpec((tr, width), lambda i, h: (i, 0))
    return pl.pallas_call(
        body,
        grid=(rows // tr, heads),
        in_specs=[pl.BlockSpec((tr, width), lambda i, h: (i, cb + h)), tab_spec, tab_spec, tab_spec],
        out_specs=pl.BlockSpec((tr, width), lambda i, h: (i, h)),
        out_shape=jax.ShapeDtypeStruct((rows, heads * width), BF16),
        compiler_params=_params("parallel", "parallel"),
        name=name,
    )(x, tab_c, tab_a, tab_b)


def _kpe_grad(dkb, tab_c, tab_a, tab_b, heads, name):
    rows = dkb.shape[0]
    tr = _tile(rows, 512, 16)

    def body(d_ref, c_ref, a_ref, b_ref, o_ref):
        acc = d_ref[:, 0:LANES]
        for h in range(1, heads):
            acc = acc + d_ref[:, h * LANES:(h + 1) * LANES]
        up = pltpu.roll(acc, LANES - HALF_ROPE, 1)
        down = pltpu.roll(acc, HALF_ROPE, 1)
        o_ref[...] = (acc * c_ref[...] + up * a_ref[...] + down * b_ref[...]).astype(o_ref.dtype)

    tab_spec = pl.BlockSpec((tr, LANES), lambda i: (i, 0))
    return pl.pallas_call(
        body,
        grid=(rows // tr,),
        in_specs=[pl.BlockSpec((tr, heads * LANES), lambda i: (i, 0)), tab_spec, tab_spec, tab_spec],
        out_specs=tab_spec,
        out_shape=jax.ShapeDtypeStruct((rows, LANES), BF16),
        compiler_params=_params("parallel"),
        name=name,
    )(dkb, tab_c, tab_a, tab_b)


class _CommPlan:
    def __init__(self, ins, out_shape, build, n_copies):
        self.ins, self.out_shape, self.build, self.n_copies = list(ins), list(out_shape), build, n_copies

    def scratch(self):
        n = self.n_copies
        return [pltpu.SemaphoreType.DMA((n,)), pltpu.SemaphoreType.DMA((n,)), pltpu.SemaphoreType.DMA((n,))]


def _split_comm(refs, n_in, n_out, comm):
    if comm is None:
        return refs, None
    ci, co = len(comm.ins), len(comm.out_shape)
    ins, c_ins = refs[:n_in], refs[n_in:n_in + ci]
    outs, c_outs = refs[n_in + ci:n_in + ci + n_out], refs[n_in + ci + n_out:n_in + ci + n_out + co]
    rest = refs[n_in + ci + n_out + co:]
    scratch, sems = rest[:-3], rest[-3:]
    return tuple(ins) + tuple(outs) + tuple(scratch), functools.partial(comm.build, c_ins, c_outs, sems)


def _ride_start(copies, first):
    if copies is not None:
        pl.when(first)(copies()[0])


def _ride_wait(copies, last):
    if copies is not None:
        pl.when(last)(copies()[1])


def _rope_rows(x, c, a, b, sign):
    width = x.shape[-1]
    mixed = pltpu.roll(x, width - HALF_ROPE, 1) * a + pltpu.roll(x, HALF_ROPE, 1) * b
    return x * c + mixed if sign > 0 else x * c - mixed


def _attn_fwd(q, ka, kb, v, rope, heads, q_w, q_cb, ka_cb, ka_step, v_cb, v_step, scale, tq_cap, name, comm=None, tk_cap=512):
    s_q, s_k = q.shape[0], ka.shape[0]
    tq = _tile(s_q, tq_cap, 16)
    nq = s_q // tq
    has_kb = kb is not None
    n_in = 7 if has_kb else 3
    tk = _tile(s_k, tk_cap, LANES)

    def body(*refs):
        refs, copies = _split_comm(refs, n_in, 2, comm)
        first = jnp.logical_and(pl.program_id(0) == 0, pl.program_id(1) == 0)
        last = jnp.logical_and(pl.program_id(0) == heads - 1, pl.program_id(1) == nq - 1)
        _ride_start(copies, first)
        if has_kb:
            q_ref, ka_ref, kb_ref, v_ref, c_ref, a_ref, b_ref, o_ref, lse_ref, k_scr = refs

            @pl.when(pl.program_id(1) == 0)
            def _():
                k_scr[:, 0:LANES] = ka_ref[...].astype(BF16)
                k_scr[:, LANES:2 * LANES] = kb_ref[...].astype(BF16)

            keys = k_scr
            qv = _rope_rows(q_ref[...], c_ref[...], a_ref[...], b_ref[...], 1).astype(BF16)
        else:
            q_ref, ka_ref, v_ref, o_ref, lse_ref = refs
            keys = ka_ref
            qv = q_ref[...].astype(BF16)
        c2 = scale * LOG2_E
        m = l = o = None
        nk = s_k // tk
        scores = lambda j: lax.dot_general(qv, keys[j * tk:(j + 1) * tk, :].astype(BF16), (((1,), (1,)), ((), ())),
                                           preferred_element_type=F32)
        s_next = scores(0)
        for j in range(nk):
            sj = s_next
            if j + 1 < nk:
                s_next = scores(j + 1)
            mj = jnp.max(sj, axis=-1, keepdims=True)
            m_new = mj if m is None else jnp.maximum(m, mj)
            pj = jnp.exp2((sj - m_new) * c2)
            lj = jnp.sum(pj, axis=-1, keepdims=True)
            oj = jnp.dot(pj.astype(BF16), v_ref[j * tk:(j + 1) * tk, :].astype(BF16), preferred_element_type=F32)
            if m is None:
                l, o = lj, oj
            else:
                alpha = jnp.exp2((m - m_new) * c2)
                l, o = l * alpha + lj, o * alpha + oj
            m = m_new
        o_ref[...] = (o * (1.0 / l)).astype(o_ref.dtype)
        lse_ref[...] = jnp.broadcast_to(m * c2 + jnp.log2(l), lse_ref.shape)
        _ride_wait(copies, last)

    in_specs = [pl.BlockSpec((tq, q_w), lambda h, i: (i, q_cb + h)),
                pl.BlockSpec((s_k, LANES), lambda h, i: (0, ka_cb + ka_step * h))]
    args = [q, ka]
    if has_kb:
        in_specs.append(pl.BlockSpec((s_k, LANES), lambda h, i: (0, 0)))
        args.append(kb)
    in_specs.append(pl.BlockSpec((s_k, LANES), lambda h, i: (0, v_cb + v_step * h)))
    args.append(v)
    if has_kb:
        in_specs += [pl.BlockSpec((tq, q_w), lambda h, i: (i, 0))] * 3
        args += list(rope)
    out_spec = pl.BlockSpec((tq, LANES), lambda h, i: (i, h))
    out_specs = [out_spec, out_spec]
    out_shape = [jax.ShapeDtypeStruct((s_q, heads * LANES), BF16), jax.ShapeDtypeStruct((s_q, heads * LANES), F32)]
    scratch = [pltpu.VMEM((s_k, 2 * LANES), BF16)] if has_kb else []
    if comm is not None:
        in_specs += [ANY] * len(comm.ins)
        args += comm.ins
        out_specs += [ANY] * len(comm.out_shape)
        out_shape += comm.out_shape
        scratch += comm.scratch()
    res = pl.pallas_call(
        body,
        grid=(heads, nq),
        in_specs=in_specs,
        out_specs=out_specs,
        out_shape=out_shape,
        scratch_shapes=scratch,
        compiler_params=_params("arbitrary", "arbitrary"),
        name=name,
    )(*args)
    return res[0], res[1], list(res[2:])


def _attn_bwd(q, ka, kb, v, o, do, lse, rope, heads, q_w, q_cb, ka_cb, ka_step, v_cb, v_step, o_cb, scale, tq_cap, name,
              comm=None, tk_cap=512):
    s_q, s_k = q.shape[0], ka.shape[0]
    tq = _tile(s_q, tq_cap, 16)
    nq = s_q // tq
    has_kb = kb is not None
    n_in = 10 if has_kb else 6
    n_out = 3
    tk = _tile(s_k, tk_cap, LANES)

    def body(*refs):
        refs, copies = _split_comm(refs, n_in, n_out, comm)
        first = jnp.logical_and(pl.program_id(0) == 0, pl.program_id(1) == 0)
        last = jnp.logical_and(pl.program_id(0) == heads - 1, pl.program_id(1) == nq - 1)
        _ride_start(copies, first)
        if has_kb:
            (q_ref, ka_ref, kb_ref, v_ref, o_ref, do_ref, lse_ref, c_ref, a_ref, b_ref, dq_ref, dkv_ref, dkb_ref, k_scr, dk_acc,
             dv_acc) = refs
        else:
            q_ref, ka_ref, v_ref, o_ref, do_ref, lse_ref, dq_ref, dka_ref, dv_ref, dk_acc, dv_acc = refs
        i = pl.program_id(1)

        @pl.when(i == 0)
        def _():
            dk_acc[...] = jnp.zeros_like(dk_acc)
            dv_acc[...] = jnp.zeros_like(dv_acc)
            if has_kb:
                k_scr[:, 0:LANES] = ka_ref[...].astype(BF16)
                k_scr[:, LANES:2 * LANES] = kb_ref[...].astype(BF16)

        keys = k_scr if has_kb else ka_ref
        if has_kb:
            qv = _rope_rows(q_ref[...], c_ref[...], a_ref[...], b_ref[...], 1).astype(BF16)
        else:
            qv = q_ref[...].astype(BF16)
        dov = do_ref[...].astype(BF16)
        delta = jnp.sum(dov.astype(F32) * o_ref[...].astype(F32), axis=-1, keepdims=True)
        lse2 = lse_ref[:, 0:1]
        c2 = scale * LOG2_E
        nk = s_k // tk
        rows = lambda j: slice(j * tk, (j + 1) * tk)
        nt = (((1,), (1,)), ((), ()))
        tn = (((0,), (0,)), ((), ()))

        def scores(j):
            return (lax.dot_general(qv, keys[rows(j), :].astype(BF16), nt, preferred_element_type=F32),
                    lax.dot_general(dov, v_ref[rows(j), :].astype(BF16), nt, preferred_element_type=F32))

        nxt = scores(0)
        dq = None
        for j in range(nk):
            sj, dpj = nxt
            if j + 1 < nk:
                nxt = scores(j + 1)
            pj = jnp.exp2(sj * c2 - lse2)
            dsj = (pj * (dpj - delta)).astype(BF16)
            dqj = jnp.dot(dsj, keys[rows(j), :].astype(BF16), preferred_element_type=F32)
            dq = dqj if dq is None else dq + dqj
            dk_acc[rows(j), :] += lax.dot_general(dsj, qv, tn, preferred_element_type=F32)
            dv_acc[rows(j), :] += lax.dot_general(pj.astype(BF16), dov, tn, preferred_element_type=F32)
        dq = dq * scale
        if has_kb:
            dq = _rope_rows(dq, c_ref[...], a_ref[...], b_ref[...], -1)
        dq_ref[...] = dq.astype(dq_ref.dtype)

        @pl.when(i == nq - 1)
        def _():
            if has_kb:
                dkv_ref[:, 0:LANES] = (dk_acc[:, 0:LANES] * scale).astype(dkv_ref.dtype)
                dkv_ref[:, LANES:2 * LANES] = dv_acc[...].astype(dkv_ref.dtype)
                dkb_ref[...] = dk_acc[:, LANES:2 * LANES] * scale
            else:
                dka_ref[...] = (dk_acc[...] * scale).astype(dka_ref.dtype)
                dv_ref[...] = dv_acc[...].astype(dv_ref.dtype)

        _ride_wait(copies, last)

    key_spec = lambda cb, step: pl.BlockSpec((s_k, LANES), lambda h, i: (0, cb + step * h))
    row_spec = lambda cb: pl.BlockSpec((tq, LANES), lambda h, i: (i, cb + h))
    in_specs = [pl.BlockSpec((tq, q_w), lambda h, i: (i, q_cb + h)), key_spec(ka_cb, ka_step)]
    args = [q, ka]
    if has_kb:
        in_specs.append(pl.BlockSpec((s_k, LANES), lambda h, i: (0, 0)))
        args.append(kb)
    in_specs += [key_spec(v_cb, v_step), row_spec(o_cb), row_spec(o_cb), row_spec(0)]
    args += [v, o, do, lse]
    if has_kb:
        in_specs += [pl.BlockSpec((tq, q_w), lambda h, i: (i, 0))] * 3
        args += list(rope)
    out_specs = [pl.BlockSpec((tq, q_w), lambda h, i: (i, h))]
    out_shape = [jax.ShapeDtypeStruct((s_q, heads * q_w), BF16)]
    scratch = []
    if has_kb:
        out_specs += [pl.BlockSpec((s_k, 2 * LANES), lambda h, i: (0, h)), key_spec(0, 1)]
        out_shape += [jax.ShapeDtypeStruct((s_k, heads * 2 * LANES), BF16), jax.ShapeDtypeStruct((s_k, heads * LANES), F32)]
        scratch.append(pltpu.VMEM((s_k, 2 * LANES), BF16))
    else:
        out_specs += [key_spec(0, 1), key_spec(0, 1)]
        out_shape += [jax.ShapeDtypeStruct((s_k, heads * LANES), BF16)] * 2
    scratch += [pltpu.VMEM((s_k, q_w), F32), pltpu.VMEM((s_k, LANES), F32)]
    if comm is not None:
        in_specs += [ANY] * len(comm.ins)
        args += comm.ins
        out_specs += [ANY] * len(comm.out_shape)
        out_shape += comm.out_shape
        scratch += comm.scratch()
    res = pl.pallas_call(
        body,
        grid=(heads, nq),
        in_specs=in_specs,
        out_specs=out_specs,
        out_shape=out_shape,
        scratch_shapes=scratch,
        compiler_params=_params("arbitrary", "arbitrary"),
        name=name,
    )(*args)
    return res[0], res[1], res[2], list(res[3:])


def _shift_rows(u, rows):
    t = lax.broadcasted_iota(jnp.int32, u.shape, 0)
    prev = jnp.where(t == 0, 0.0, pltpu.roll(u, 1, 0))
    nxt = jnp.where(t == rows - 1, 0.0, pltpu.roll(u, rows - 1, 0))
    return prev, nxt


def _conv_fwd(z, conv_w, name):
    rows = z.shape[0]
    nblk = CONV_W // LANES

    def body(gb_ref, gc_ref, xin_ref, w_ref, o_ref):
        u = gc_ref[...] * xin_ref[...]
        prev, nxt = _shift_rows(u, rows)
        conv = prev * w_ref[0:1, :] + u * w_ref[1:2, :] + nxt * w_ref[2:3, :]
        o_ref[...] = (gb_ref[...] * conv).astype(o_ref.dtype)

    col = lambda c0: pl.BlockSpec((rows, LANES), lambda j: (0, c0 // LANES + j))
    return pl.pallas_call(
        body,
        grid=(nblk,),
        in_specs=[col(Z_GB), col(Z_GC), col(Z_XIN), pl.BlockSpec((3, LANES), lambda j: (0, j))],
        out_specs=col(0),
        out_shape=jax.ShapeDtypeStruct((rows, CONV_W), BF16),
        compiler_params=_params("parallel"),
        name=name,
    )(z, z, z, conv_w)


def _conv_bwd(z, conv_w, dcat, name):
    rows = z.shape[0]
    nblk = CONV_W // LANES

    def body(gb_ref, gc_ref, xin_ref, w_ref, dc_ref, dgb_ref, dgc_ref, dxin_ref, dw_ref):
        gc = gc_ref[...]
        xin = xin_ref[...]
        dc = dc_ref[...].astype(F32)
        u = gc * xin
        prev, nxt = _shift_rows(u, rows)
        w0, w1, w2 = w_ref[0:1, :], w_ref[1:2, :], w_ref[2:3, :]
        conv = prev * w0 + u * w1 + nxt * w2
        dgb_ref[...] = (dc * conv).astype(dgb_ref.dtype)
        dconv = dc * gb_ref[...]
        dw_ref[0:1, :] = jnp.sum(dconv * prev, axis=0, keepdims=True)
        dw_ref[1:2, :] = jnp.sum(dconv * u, axis=0, keepdims=True)
        dw_ref[2:3, :] = jnp.sum(dconv * nxt, axis=0, keepdims=True)
        dprev, dnxt = _shift_rows(dconv, rows)
        du = dnxt * w0 + dconv * w1 + dprev * w2
        dgc_ref[...] = (du * xin).astype(dgc_ref.dtype)
        dxin_ref[...] = (du * gc).astype(dxin_ref.dtype)

    col = lambda c0: pl.BlockSpec((rows, LANES), lambda j: (0, c0 // LANES + j))
    w_spec = pl.BlockSpec((3, LANES), lambda j: (0, j))
    piece = jax.ShapeDtypeStruct((rows, CONV_W), BF16)
    return pl.pallas_call(
        body,
        grid=(nblk,),
        in_specs=[col(Z_GB), col(Z_GC), col(Z_XIN), w_spec, col(MLA_W)],
        out_specs=[col(0), col(0), col(0), w_spec],
        out_shape=[piece, piece, piece, jax.ShapeDtypeStruct((3, CONV_W), F32)],
        compiler_params=_params("parallel"),
        name=name,
    )(z, z, z, conv_w, dcat)


def _gate_fwd(cat, z, name):
    rows = cat.shape[0]
    tr = _tile(rows, 256, 16)
    tc = MIX_W
    g0 = Z_GATE // tc

    def body(c_ref, g_ref, y_ref):
        g = g_ref[...]
        y_ref[...] = (c_ref[...].astype(F32) * (g * jax.nn.sigmoid(g))).astype(y_ref.dtype)

    blk = pl.BlockSpec((tr, tc), lambda i, j: (i, j))
    return pl.pallas_call(
        body,
        grid=(rows // tr, MIX_W // tc),
        in_specs=[blk, pl.BlockSpec((tr, tc), lambda i, j: (i, g0 + j))],
        out_specs=blk,
        out_shape=jax.ShapeDtypeStruct((rows, MIX_W), BF16),
        compiler_params=_params("parallel", "parallel"),
        name=name,
    )(cat, z)


def _out_proj_dx_gate_bwd(do, w_o, cat, z, name):
    rows, k = do.shape
    tm = _tile(rows, 512, 16)
    tn = _tile(MIX_W, 1024, LANES)
    g0 = Z_GATE // tn

    def body(do_ref, w_ref, c_ref, g_ref, dcat_ref, dgate_ref):
        dy = lax.dot_general(do_ref[...], w_ref[...], (((1,), (1,)), ((), ())), preferred_element_type=F32)
        g = g_ref[...]
        sg = jax.nn.sigmoid(g)
        dcat_ref[...] = (dy * (g * sg)).astype(dcat_ref.dtype)
        dgate_ref[...] = (dy * c_ref[...].astype(F32) * (sg * (1.0 + g * (1.0 - sg)))).astype(dgate_ref.dtype)

    blk = pl.BlockSpec((tm, tn), lambda i, j: (i, j))
    out = jax.ShapeDtypeStruct((rows, MIX_W), BF16)
    return pl.pallas_call(
        body,
        grid=(rows // tm, MIX_W // tn),
        in_specs=[pl.BlockSpec((tm, k), lambda i, j: (i, 0)), pl.BlockSpec((tn, k), lambda i, j: (j, 0)), blk,
                  pl.BlockSpec((tm, tn), lambda i, j: (i, g0 + j))],
        out_specs=[blk, blk],
        out_shape=[out, out],
        compiler_params=_params("parallel", "parallel"),
        name=name,
    )(do, w_o, cat, z)


def _loss_head(y, target, name):
    rows, width = y.shape
    tr = _tile(rows, 256, 8)

    def body(y_ref, t_ref, g_ref, loss_ref):
        i = pl.program_id(0)
        d = y_ref[...] - t_ref[...]
        g_ref[...] = d / width
        part = 0.5 * jnp.sum(jnp.mean(d * d, axis=-1, keepdims=True), axis=0, keepdims=True)
        part = jnp.broadcast_to(part, loss_ref.shape)

        @pl.when(i == 0)
        def _():
            loss_ref[...] = part

        @pl.when(i > 0)
        def _():
            loss_ref[...] += part

    row_spec = pl.BlockSpec((tr, width), lambda i: (i, 0))
    return pl.pallas_call(
        body,
        grid=(rows // tr,),
        in_specs=[row_spec, row_spec],
        out_specs=[row_spec, pl.BlockSpec((1, LANES), lambda i: (0, 0))],
        out_shape=[jax.ShapeDtypeStruct((rows, width), F32), jax.ShapeDtypeStruct((1, LANES), F32)],
        compiler_params=_params("arbitrary"),
        name=name,
    )(y, target)


CHIP_FLIPS = ((1, 0), (0, 1), (1, 1))
ANY = pl.BlockSpec(memory_space=pl.ANY)


def _chip_copies(pieces, sems, n_slot):
    send_sems, recv_sems, local_sems = sems
    x, y, c = lax.axis_index("x"), lax.axis_index("y"), lax.axis_index("c")
    me = 2 * x + y

    def remote(j, k, a, src, dst):
        fx, fy = CHIP_FLIPS[k]
        return pltpu.make_async_remote_copy(
            src_ref=src, dst_ref=dst, send_sem=send_sems.at[n_slot * k + a], recv_sem=recv_sems.at[n_slot * k + a],
            device_id=((j // 2) ^ fx, (j % 2) ^ fy, c), device_id_type=MESH_ID)

    def peer(j, k):
        fx, fy = CHIP_FLIPS[k]
        return 2 * ((j // 2) ^ fx) + ((j % 2) ^ fy)

    def start_as(j):
        def run():
            for a, (src, dst) in enumerate(pieces(j, j)):
                pltpu.make_async_copy(src, dst, local_sems.at[a]).start()
            for k in range(len(CHIP_FLIPS)):
                for a, (src, dst) in enumerate(pieces(j, peer(j, k))):
                    remote(j, k, a, src, dst).start()
        return run

    def wait_as(j):
        def run():
            for a, (src, dst) in enumerate(pieces(j, j)):
                pltpu.make_async_copy(src, dst, local_sems.at[a]).wait()
            for k in range(len(CHIP_FLIPS)):
                for a, (src, dst) in enumerate(pieces(j, peer(j, k))):
                    remote(j, k, a, src, dst).wait_send()
                for a, (src, dst) in enumerate(pieces(peer(j, k), j)):
                    remote(j, k, a, src, dst).wait_recv()
        return run

    def start():
        for j in range(N_CHIPS):
            pl.when(me == j)(start_as(j))

    def wait():
        for j in range(N_CHIPS):
            pl.when(me == j)(wait_as(j))

    return start, wait


IN_PIECES = ((0, Q_RANK, Z_QLAT), (Q_RANK, KV_RANK, Z_KVLAT), (Q_RANK + KV_RANK, ROPE, Z_KPE),
             (Q_RANK + KV_RANK + ROPE, CONV_W, Z_GB), (Q_RANK + KV_RANK + ROPE + CONV_W, CONV_W, Z_GC),
             (Q_RANK + KV_RANK + ROPE + 2 * CONV_W, CONV_W, Z_XIN), (Q_RANK + KV_RANK + ROPE + 3 * CONV_W, MEM_W, Z_QMEM),
             (Q_RANK + KV_RANK + ROPE + 3 * CONV_W + MEM_W, MIX_W, Z_GATE))
IN_SHARD = IN_COLS // N_CHIPS


def _in_segments(j):
    lo, hi = j * IN_SHARD, (j + 1) * IN_SHARD
    segs = []
    for r0, width, z0 in IN_PIECES:
        a, b = max(lo, r0), min(hi, r0 + width)
        if a < b:
            segs.append((a - lo, z0 + a - r0, b - a))
    return segs


N_SLOT = 11


def _gather_plan(l, shards, zero_rows, part="all"):
    s_in, s_uq, s_ukv, s_conv, s_mk, s_mv, s_o = shards
    ukv_c, mk_r, mk_c, o_r = s_ukv.shape[2], s_mk.shape[1], s_mk.shape[2], s_o.shape[1]
    stack = lambda s: jax.ShapeDtypeStruct((N_CHIPS,) + s.shape[1:], s.dtype)
    in_ins, in_outs = [s_in, zero_rows], [jax.ShapeDtypeStruct((Z_COLS, s_in.shape[2]), s_in.dtype)]
    rest_ins = [s_uq, s_ukv, s_conv, s_mk, s_mv, s_o]
    rest_outs = [stack(s_uq), jax.ShapeDtypeStruct((s_ukv.shape[1], N_CHIPS * ukv_c), s_ukv.dtype), stack(s_conv),
                 jax.ShapeDtypeStruct((N_CHIPS * mk_r, 2 * mk_c), s_mk.dtype),
                 jax.ShapeDtypeStruct((N_CHIPS * o_r, s_o.shape[2]), s_o.dtype)]
    with_in, with_rest = part != "rest", part != "in"

    def build(ins, outs, sems):
        ins, outs = list(ins), list(outs)
        if with_in:
            r_in, r_zero, f_in = ins.pop(0), ins.pop(0), outs.pop(0)
        if with_rest:
            r_uq, r_ukv, r_conv, r_mk, r_mv, r_o = ins
            g_uq, f_ukv, g_conv, f_mkv, f_o = outs

        def pieces(j, t):
            out = []
            if with_in:
                out += [(r_in.at[l, pl.ds(so, n), :], f_in.at[pl.ds(zo, n), :]) for so, zo, n in _in_segments(j)]
            if with_rest:
                out += [(r_uq.at[l], g_uq.at[j]), (r_ukv.at[l], f_ukv.at[:, pl.ds(j * ukv_c, ukv_c)]),
                        (r_conv.at[l], g_conv.at[j]),
                        (r_mk.at[l], f_mkv.at[pl.ds(j * mk_r, mk_r), pl.ds(0, mk_c)]),
                        (r_mv.at[l], f_mkv.at[pl.ds(j * mk_r, mk_r), pl.ds(mk_c, mk_c)]),
                        (r_o.at[l], f_o.at[pl.ds(j * o_r, o_r), :])]
            if with_in and j == t:
                out.append((r_zero, f_in.at[pl.ds(Z_KPE + ROPE, LANES - ROPE), :]))
            return out

        return _chip_copies(pieces, sems, N_SLOT)

    ins = (in_ins if with_in else []) + (rest_ins if with_rest else [])
    outs = (in_outs if with_in else []) + (rest_outs if with_rest else [])
    return _CommPlan(ins, outs, build, len(CHIP_FLIPS) * N_SLOT)


def _scatter_plan(dwt_in, c_uq, dw_ukv, c_conv, dw_mkv, dw_o, part="all"):
    ukv_c, mk_r, mk_c, o_r = dw_ukv.shape[1] // N_CHIPS, dw_mkv.shape[0] // N_CHIPS, dw_mkv.shape[1] // 2, dw_o.shape[0] // N_CHIPS
    with_in, with_rest = part != "rest", part != "in"
    in_outs = [jax.ShapeDtypeStruct((N_CHIPS, IN_SHARD, D_MODEL), BF16)]
    rest_ins = [c_uq, dw_ukv, c_conv, dw_mkv, dw_o]
    rest_outs = [jax.ShapeDtypeStruct(c_uq.shape, c_uq.dtype),
                 jax.ShapeDtypeStruct((N_CHIPS, dw_ukv.shape[0], ukv_c), dw_ukv.dtype),
                 jax.ShapeDtypeStruct(c_conv.shape, c_conv.dtype),
                 jax.ShapeDtypeStruct((N_CHIPS, mk_r, mk_c), dw_mkv.dtype), jax.ShapeDtypeStruct((N_CHIPS, mk_r, mk_c), dw_mkv.dtype),
                 jax.ShapeDtypeStruct((N_CHIPS, o_r, dw_o.shape[1]), dw_o.dtype)]

    def build(ins, outs, sems):
        ins, outs = list(ins), list(outs)
        if with_in:
            r_in, o_in = ins.pop(0), outs.pop(0)
        if with_rest:
            r_uq, r_ukv, r_conv, r_mkv, r_o = ins
            o_uq, o_ukv, o_conv, o_mk, o_mv, o_o = outs

        def pieces(j, t):
            out = []
            if with_in:
                out += [(r_in.at[pl.ds(zo, n), :], o_in.at[j, pl.ds(so, n), :]) for so, zo, n in _in_segments(t)]
            if with_rest:
                out += [(r_uq.at[t], o_uq.at[j]), (r_ukv.at[:, pl.ds(t * ukv_c, ukv_c)], o_ukv.at[j]),
                        (r_conv.at[t], o_conv.at[j]),
                        (r_mkv.at[pl.ds(t * mk_r, mk_r), pl.ds(0, mk_c)], o_mk.at[j]),
                        (r_mkv.at[pl.ds(t * mk_r, mk_r), pl.ds(mk_c, mk_c)], o_mv.at[j]),
                        (r_o.at[pl.ds(t * o_r, o_r), :], o_o.at[j])]
            return out

        return _chip_copies(pieces, sems, N_SLOT)

    ins = ([dwt_in] if with_in else []) + (rest_ins if with_rest else [])
    outs = (in_outs if with_in else []) + (rest_outs if with_rest else [])
    return _CommPlan(ins, outs, build, len(CHIP_FLIPS) * N_SLOT)


HBM = pl.BlockSpec(memory_space=pltpu.HBM)
SEM = pl.BlockSpec(memory_space=pltpu.SEMAPHORE)
SIDE_EFFECT = pltpu.SideEffectType.DATAFLOW_SIDE_EFFECTING


def _comm_start(plan, after, name):
    n_in, n_out = len(plan.ins), len(plan.out_shape)
    n_buf = n_in + n_out

    def body(*refs):
        bufs, sems, token = refs[:n_buf], refs[n_buf + 1:n_buf + 4], refs[-1]
        start, _ = plan.build(bufs[:n_in], bufs[n_in:], sems)
        start()
        token[...] = jnp.zeros_like(token)

    lands = [lax.empty(s.shape, s.dtype) for s in plan.out_shape]
    args = [pltpu.with_memory_space_constraint(a, pltpu.HBM) for a in list(plan.ins) + lands]
    res = pl.pallas_call(
        body,
        in_specs=[HBM] * n_buf + [ANY],
        out_specs=[SEM] * 3 + [HBM] * n_buf + [pl.BlockSpec(memory_space=pltpu.VMEM)],
        out_shape=plan.scratch() + [pltpu.HBM(a.shape, a.dtype) for a in args] + [jax.ShapeDtypeStruct((8, LANES), F32)],
        input_output_aliases={i: 3 + i for i in range(n_buf)},
        compiler_params=pltpu.CompilerParams(has_side_effects=SIDE_EFFECT),
        name=name,
    )(*args, after)
    return list(res[:3]), list(res[3:3 + n_buf]), res[-1]


def _comm_finish(plan, started, after, name):
    sems, bufs, _ = started
    n_in, n_out = len(plan.ins), len(plan.out_shape)
    n_buf = n_in + n_out

    def body(*refs):
        bufs_in, sem_refs = refs[:n_buf], refs[n_buf:n_buf + 3]
        _, wait = plan.build(bufs_in[:n_in], bufs_in[n_in:], sem_refs)
        wait()

    res = pl.pallas_call(
        body,
        in_specs=[HBM] * n_buf + [SEM] * 3 + [ANY],
        out_specs=[HBM] * n_buf,
        out_shape=[pltpu.HBM(b.shape, b.dtype) for b in bufs],
        input_output_aliases={i: i for i in range(n_buf)},
        compiler_params=pltpu.CompilerParams(has_side_effects=SIDE_EFFECT),
        name=name,
    )(*bufs, *sems, after)
    return list(res[n_in:])


def _comm_call(plan, name):
    n_in, n_out = len(plan.ins), len(plan.out_shape)

    def body(*refs):
        start, wait = plan.build(refs[:n_in], refs[n_in:n_in + n_out], refs[n_in + n_out:])
        start()
        wait()

    return list(pl.pallas_call(
        body,
        in_specs=[ANY] * n_in,
        out_specs=[ANY] * n_out,
        out_shape=plan.out_shape,
        scratch_shapes=plan.scratch(),
        name=name,
    )(*plan.ins))


def _sibling_plan(arrays):
    def build(ins, outs, sems):
        send_sems, recv_sems, _ = sems
        sibling = (lax.axis_index("x"), lax.axis_index("y"), 1 - lax.axis_index("c"))
        copies = [pltpu.make_async_remote_copy(src_ref=src, dst_ref=dst, send_sem=send_sems.at[a], recv_sem=recv_sems.at[a],
                                               device_id=sibling, device_id_type=MESH_ID)
                  for a, (src, dst) in enumerate(zip(ins, outs))]

        def start():
            for cp in copies:
                cp.start()

        def wait():
            for cp in copies:
                cp.wait()

        return start, wait

    return _CommPlan(arrays, [jax.ShapeDtypeStruct(v.shape, v.dtype) for v in arrays], build, len(arrays))


DEVICE_FLIPS = tuple((fx, fy, fc) for fx in (0, 1) for fy in (0, 1) for fc in (0, 1))[1:]


def _gather_all(v, name):
    def body(v_ref, out_ref, send_sems, recv_sems, local_sem):
        x, y, c = lax.axis_index("x"), lax.axis_index("y"), lax.axis_index("c")
        me = 4 * x + 2 * y + c
        local = pltpu.make_async_copy(v_ref, out_ref.at[me], local_sem)
        local.start()
        copies = [local]
        for k, (fx, fy, fc) in enumerate(DEVICE_FLIPS):
            cp = pltpu.make_async_remote_copy(
                src_ref=v_ref, dst_ref=out_ref.at[me], send_sem=send_sems.at[k], recv_sem=recv_sems.at[k],
                device_id=((x + fx) % 2, (y + fy) % 2, (c + fc) % 2), device_id_type=MESH_ID)
            cp.start()
            copies.append(cp)
        for cp in copies:
            cp.wait()

    return pl.pallas_call(
        body,
        in_specs=[ANY],
        out_specs=ANY,
        out_shape=jax.ShapeDtypeStruct((N_DEV,) + v.shape, v.dtype),
        scratch_shapes=[pltpu.SemaphoreType.DMA((N_DEV - 1,)), pltpu.SemaphoreType.DMA((N_DEV - 1,)), pltpu.SemaphoreType.DMA],
        name=name,
    )(v)


def _sum_slots(parts, name):
    n, rows, cols = parts.shape
    tr = _tile(rows, 256, 16)

    def body(p_ref, o_ref):
        acc = p_ref[0].astype(F32)
        for k in range(1, n):
            acc = acc + p_ref[k].astype(F32)
        o_ref[...] = acc

    return pl.pallas_call(
        body,
        grid=(rows // tr,),
        in_specs=[pl.BlockSpec((n, tr, cols), lambda i: (0, i, 0))],
        out_specs=pl.BlockSpec((tr, cols), lambda i: (i, 0)),
        out_shape=jax.ShapeDtypeStruct((rows, cols), F32),
        compiler_params=_params("parallel"),
        name=name,
    )(parts)


def _adamw_math(w, g, m, v):
    m_new = ADAM_B1 * m + (1.0 - ADAM_B1) * g
    v_new = ADAM_B2 * v + (1.0 - ADAM_B2) * jnp.square(g)
    m_hat = m_new / (1.0 - ADAM_B1 ** ADAM_STEP)
    v_hat = v_new / (1.0 - ADAM_B2 ** ADAM_STEP)
    return -ADAM_LR * (m_hat / (jnp.sqrt(v_hat) + ADAM_EPS) + ADAM_WD * w), m_new, v_new


def _adamw(w, g, m, v, name):
    rows, cols = w.shape
    tr = _tile(rows, 256, 8)

    def body(w_ref, g_ref, m_ref, v_ref, d_out, m_out, v_out):
        d_out[...], m_out[...], v_out[...] = _adamw_math(w_ref[...], g_ref[...], m_ref[...], v_ref[...])

    blk = pl.BlockSpec((tr, cols), lambda i: (i, 0))
    out = jax.ShapeDtypeStruct((rows, cols), F32)
    return pl.pallas_call(
        body,
        grid=(rows // tr,),
        in_specs=[blk] * 4,
        out_specs=[blk] * 3,
        out_shape=[out] * 3,
        compiler_params=_params("parallel"),
        name=name,
    )(w, g, m, v)


def _adamw_layer(l, w, g_a, g_b, m, v, prev, name):
    depth, rows, cols = w.shape
    tr = _tile(rows, 256, 8)

    def body(w_ref, ga_ref, gb_ref, m_ref, v_ref, *rest):
        g_out, d_out, m_out, v_out = rest[-4:]
        g = ga_ref[...] + gb_ref[...]
        g_out[...] = g
        d_out[...], m_out[...], v_out[...] = _adamw_math(w_ref[...], g, m_ref[...], v_ref[...])

    stacked = pl.BlockSpec((None, tr, cols), lambda i: (l, i, 0))
    flat = pl.BlockSpec((tr, cols), lambda i: (i, 0))
    in_specs = [stacked, flat, flat, stacked, stacked]
    args = [w, g_a, g_b, m, v]
    aliases = {}
    if prev is not None:
        in_specs += [ANY] * 4
        args += list(prev)
        aliases = {5 + k: k for k in range(4)}
    out = jax.ShapeDtypeStruct((depth, rows, cols), F32)
    return pl.pallas_call(
        body,
        grid=(rows // tr,),
        in_specs=in_specs,
        out_specs=[stacked] * 4,
        out_shape=[out] * 4,
        input_output_aliases=aliases,
        compiler_params=_params("parallel"),
        name=name,
    )(*args)


def _cols_from_shards(g):
    _, r, c = g.shape
    return jnp.transpose(g, (1, 0, 2)).reshape(r, N_CHIPS * c)


def _cols_to_shards(full):
    r, c4 = full.shape
    c = c4 // N_CHIPS
    return jnp.transpose(full.reshape(r, N_CHIPS, c), (1, 0, 2))


IN_ORDER = (Q_RANK, KV_RANK, ROPE, CONV_W, CONV_W, CONV_W, MEM_W, MIX_W)


def _w_in_to_z_layout(w_in):
    edges = [0]
    for width in IN_ORDER:
        edges.append(edges[-1] + width)
    q_lat, kv_lat, k_pe, gb, gc, xin, q_mem, gate = [w_in[..., edges[i]:edges[i + 1]] for i in range(8)]
    pad = jnp.zeros(k_pe.shape[:-1] + (LANES - ROPE,), w_in.dtype)
    return jnp.concatenate([gate, q_lat, kv_lat, k_pe, pad, gb, gc, xin, q_mem], axis=-1)


def _w_in_from_z_layout(wz):
    cut = lambda c0, width: wz[..., c0:c0 + width]
    return jnp.concatenate(
        [cut(Z_QLAT, Q_RANK), cut(Z_KVLAT, KV_RANK), cut(Z_KPE, ROPE), cut(Z_GB, CONV_W), cut(Z_GC, CONV_W),
         cut(Z_XIN, CONV_W), cut(Z_QMEM, MEM_W), cut(Z_GATE, MIX_W)], axis=-1)


def _w_uq_pad(w_uq):
    r, _ = w_uq.shape
    w = jnp.pad(w_uq.reshape(r, MLA_HEADS, QK_HEAD), ((0, 0), (0, 0), (0, QPAD - QK_HEAD)))
    return w.reshape(r, MLA_HEADS * QPAD)


def _w_uq_unpad(w):
    r, _ = w.shape
    return w.reshape(r, MLA_HEADS, QPAD)[..., :QK_HEAD].reshape(r, MLA_HEADS * QK_HEAD)


def _rope_tables(positions):
    inv_freq = 1.0 / (ROPE_THETA ** (jnp.arange(0, ROPE, 2, dtype=F32) / ROPE))
    ang = positions.astype(F32)[:, None] * inv_freq
    cos, sin = jnp.cos(ang), jnp.sin(ang)
    s = positions.shape[0]
    zero = jnp.zeros((s, HALF_ROPE), F32)
    pad = jnp.zeros((s, LANES - ROPE), F32)
    kc = jnp.concatenate([cos, cos, pad], axis=-1)
    ka = jnp.concatenate([-sin, zero, pad], axis=-1)
    kb = jnp.concatenate([zero, sin, pad], axis=-1)
    qc = jnp.concatenate([jnp.ones((s, NOPE), F32), kc], axis=-1)
    qa = jnp.concatenate([jnp.zeros((s, NOPE), F32), ka], axis=-1)
    qb = jnp.concatenate([jnp.zeros((s, NOPE), F32), kb], axis=-1)
    return (qc, qa, qb), (kc, ka, kb)


def _layer_weights(gathered):
    return (gathered[0],) + _late_weights(gathered[1:])


def _late_weights(gathered):
    g_uq, w_ukv, g_conv, w_mkv, w_o = gathered
    return (_w_uq_pad(_cols_from_shards(g_uq)), w_ukv, _cols_from_shards(g_conv), w_mkv, w_o)


def _layer_fwd(l, x, mem, wts, gains, tabs, comm, late=None):
    wt_in = wts[0]
    g_pre, g_q, g_kv, g_mem, g_post = gains
    q_tab, k_tab = tabs
    tag = f"l{l}_"
    h = _rmsnorm_fwd(x, g_pre, 0, D_MODEL, tag + "pre_norm")
    z = _matmul(h, wt_in, "nt", F32, tag + "in_proj", tm_cap=1024, tn_cap=1664)
    w_uq, w_ukv, conv_w, w_mkv, w_o = wts[1:] if late is None else late(z)
    wts = (wt_in, w_uq, w_ukv, conv_w, w_mkv, w_o)
    qn = _rmsnorm_fwd(z, g_q, Z_QLAT, Q_RANK, tag + "q_norm")
    kvn = _rmsnorm_fwd(z, g_kv, Z_KVLAT, KV_RANK, tag + "kv_norm")
    q_raw = _matmul(qn, w_uq, "nn", F32, tag + "uq", tm_cap=1024)
    kv = _matmul(kvn, w_ukv, "nn", BF16, tag + "ukv")
    kpe = _rope(z, *k_tab, Z_KPE, LANES, 1, tag + "k_rope")
    a_out, a_lse, arrived = _attn_fwd(q_raw, kv, kpe, kv, q_tab, MLA_HEADS, QPAD, 0, 0, 2, 1, 2, QK_HEAD ** -0.5, 512,
                                      tag + "mla_fwd", comm)
    c_out = _conv_fwd(z, conv_w, tag + "conv_fwd")
    mem_n = _rmsnorm_fwd(mem, g_mem, 0, D_MODEL, tag + "mem_norm")
    mkv = _matmul(mem_n, w_mkv, "nn", BF16, tag + "mem_kv")
    m_out, m_lse, _ = _attn_fwd(z, mkv, None, mkv, None, MEM_HEADS, LANES, Z_QMEM // LANES, 0, 1, MEM_HEADS, 1,
                                MEM_HEAD ** -0.5, 1024, tag + "mem_fwd")
    cat = jnp.concatenate([a_out, c_out, m_out], axis=-1)
    y = _gate_fwd(cat, z, tag + "gate_fwd")
    o = _matmul(y, w_o, "nn", F32, tag + "out_proj", tm_cap=1024)
    x_new = _post_norm_residual(x, o, g_post, tag + "post_norm")
    saved = (x, h, z, qn, kvn, q_raw, kv, kpe, a_lse, mem_n, mkv, m_lse, cat, y, o)
    return x_new, saved, arrived


def _layer_bwd(l, g, mem, saved, wts, gains, tabs_bwd, comm, split_exchange=False):
    wt_in, w_uq, w_ukv, conv_w, w_mkv, w_o = wts
    g_pre, g_q, g_kv, g_mem, g_post = gains
    q_tab, k_tab_bwd = tabs_bwd
    x, h, z, qn, kvn, q_raw, kv, kpe, a_lse, mem_n, mkv, m_lse, cat, y, o = saved
    tag = f"l{l}_"
    do, dg_post = _rmsnorm_bwd(o, g_post, g, None, 0, D_MODEL, BF16, tag + "post_norm_bwd")
    dcat, dgate = _out_proj_dx_gate_bwd(do, w_o, cat, z, tag + "out_proj_dx")
    dw_o = _matmul(y, do, "tn", BF16, tag + "out_proj_dw", tm_cap=1024)
    dq, dkv, dkpe_h, arrived = _attn_bwd(q_raw, kv, kpe, kv, cat, dcat, a_lse, q_tab, MLA_HEADS, QPAD, 0, 0, 2, 1, 2, 0,
                                         QK_HEAD ** -0.5, 512, tag + "mla_bwd", comm)
    dkpe = _kpe_grad(dkpe_h, *k_tab_bwd, MLA_HEADS, tag + "k_rope_bwd")
    dw_ukv = _matmul(kvn, dkv, "tn", BF16, tag + "ukv_dw")
    dkvn = _matmul(dkv, w_ukv, "nt", F32, tag + "ukv_dx")
    dkv_lat, dg_kv = _rmsnorm_bwd(z, g_kv, dkvn, None, Z_KVLAT, KV_RANK, BF16, tag + "kv_norm_bwd")
    dw_uq = _matmul(qn, dq, "tn", BF16, tag + "uq_dw")
    dqn = _matmul(dq, w_uq, "nt", F32, tag + "uq_dx")
    dq_lat, dg_q = _rmsnorm_bwd(z, g_q, dqn, None, Z_QLAT, Q_RANK, BF16, tag + "q_norm_bwd")
    dgb, dgc, dxin, dconv_w = _conv_bwd(z, conv_w, dcat, tag + "conv_bwd")
    dq_mem, dmk, dmv, _ = _attn_bwd(z, mkv, None, mkv, cat, dcat, m_lse, None, MEM_HEADS, LANES, Z_QMEM // LANES, 0, 1,
                                    MEM_HEADS, 1, (MLA_W + CONV_W) // LANES, MEM_HEAD ** -0.5, 1024, tag + "mem_bwd")
    dmkv = jnp.concatenate([dmk, dmv], axis=-1)
    dw_mkv = _matmul(mem_n, dmkv, "tn", BF16, tag + "mem_kv_dw")
    dmem_n = _matmul(dmkv, w_mkv, "nt", F32, tag + "mem_kv_dx")
    _, dg_mem = _rmsnorm_bwd(mem, g_mem, dmem_n, None, 0, D_MODEL, BF16, tag + "mem_norm_bwd")
    others = (_cols_to_shards(_w_uq_unpad(dw_uq)), dw_ukv, _cols_to_shards(dconv_w), dw_mkv, dw_o)
    early = None
    if split_exchange:
        early_plan = _scatter_plan(None, *others, part="rest")
        early = (early_plan, _comm_start(early_plan, dmem_n, tag + "exchange_rest_start"))
        g_pre = g_pre + early[1][2][0:1, 0:1]
    dz = jnp.concatenate([dgate, dq_lat, dkv_lat, dkpe, dgb, dgc, dxin, dq_mem], axis=-1)
    dwt_in = _matmul(dz, h, "tn", BF16, tag + "in_proj_dw", tm_cap=1664, tk_cap=2048)
    contrib = _scatter_plan(dwt_in, *others, part="in" if split_exchange else "all")
    late = None
    if split_exchange:
        late = (contrib, _comm_start(contrib, dwt_in, tag + "exchange_in_start"))
    dh = _matmul(dz, wt_in, "nn", F32, tag + "in_proj_dx", tm_cap=1024, tk_cap=1664, after=late[1][2] if late else None)
    dx, dg_pre = _rmsnorm_bwd(x, g_pre, dh, g, 0, D_MODEL, F32, tag + "pre_norm_bwd")
    return dx, contrib, (dg_pre, dg_q, dg_kv, dg_mem, dg_post), (early, late)


GAIN_WIDTHS = (D_MODEL, Q_RANK, KV_RANK, D_MODEL, D_MODEL)


def _pack_gains(parts):
    return jnp.concatenate([p.reshape(-1) for p in parts]).reshape(-1, LANES)


def _unpack_gains(packed, depth):
    flat = packed.reshape(-1)
    out, at = [], 0
    for width in GAIN_WIDTHS:
        out.append(flat[at:at + depth * width].reshape(depth, width))
        at += depth * width
    return out


def kernel(x, mem, positions, pre_norm_g, w_in, q_norm_g, w_uq, kv_norm_g, w_ukv, conv_w, mem_norm_g, w_mk, w_mv, w_o, post_norm_g, loss_target, m_pre_norm_g, m_w_in, m_q_norm_g, m_w_uq, m_kv_norm_g, m_w_ukv, m_conv_w, m_mem_norm_g, m_w_mk, m_w_mv, m_w_o, m_post_norm_g, v_pre_norm_g, v_w_in, v_q_norm_g, v_w_uq, v_kv_norm_g, v_w_ukv, v_conv_w, v_mem_norm_g, v_w_mk, v_w_mv, v_w_o, v_post_norm_g):
    depth = w_in.shape[0]
    x0, mem0, target = x[0], mem[0], loss_target[0]
    tabs = _rope_tables(positions[0])
    tabs_bwd = (tabs[0], (tabs[1][0], -tabs[1][1], -tabs[1][2]))

    flip = lambda t: jnp.transpose(t, (0, 2, 1))
    w_in, m_w_in, v_w_in = flip(w_in), flip(m_w_in), flip(v_w_in)
    shards = [w_in.astype(BF16), w_uq.astype(BF16), w_ukv.astype(BF16), conv_w, w_mk.astype(BF16), w_mv.astype(BF16),
              w_o.astype(BF16)]
    zero_rows = lambda: jnp.zeros((LANES - ROPE, D_MODEL), BF16)

    def layer_gains(l):
        return tuple(g[l][None, :] for g in (pre_norm_g, q_norm_g, kv_norm_g, mem_norm_g, post_norm_g))

    wts, saved = [None] * depth, [None] * depth
    first = [s[0:1] for s in shards]
    plan_in, plan_rest = _gather_plan(0, first, zero_rows(), "in"), _gather_plan(0, first, zero_rows(), "rest")
    started_in = _comm_start(plan_in, positions, "l0_gather_in_start")
    started_rest = _comm_start(plan_rest, started_in[2], "l0_gather_rest_start")
    wts[0] = tuple(_comm_finish(plan_in, started_in, started_rest[2], "l0_gather_in_wait"))

    next_gather = {}

    def start_next_gather(l, after):
        plan = _gather_plan(0, [s[l + 1:l + 2] for s in shards], zero_rows())
        next_gather[l + 1] = (plan, _comm_start(plan, after, f"l{l + 1}_gather_start"))
        return next_gather[l + 1][1][2][0:1, 0:1]

    def rest_of_layer0(z):
        got = _late_weights(_comm_finish(plan_rest, started_rest, z, "l0_gather_rest_wait"))
        wts[0] = wts[0] + got
        if depth > 1:
            got = (got[0] + start_next_gather(0, got[4]).astype(BF16),) + got[1:]
        return got

    act = x0
    for l in range(depth):
        gains = layer_gains(l)
        if 0 < l < depth - 1:
            gains = (gains[0] + start_next_gather(l, wts[l][5]),) + gains[1:]
        act, saved[l], _ = _layer_fwd(l, act, mem0, wts[l], gains, tabs, None, rest_of_layer0 if l == 0 else None)
        if l + 1 < depth:
            plan, started = next_gather[l + 1]
            wts[l + 1] = _layer_weights(_comm_finish(plan, started, act, f"l{l + 1}_gather_wait"))
    grad, loss_part = _loss_head(act, target, "loss_head")
    loss = lax.psum(loss_part[0, 0], ("x", "y", "c"))

    names = ("w_in", "w_uq", "w_ukv", "conv_w", "w_mk", "w_mv", "w_o")
    w_shards = (w_in, w_uq, w_ukv, conv_w, w_mk, w_mv, w_o)
    m_shards = (m_w_in, m_w_uq, m_w_ukv, m_conv_w, m_w_mk, m_w_mv, m_w_o)
    v_shards = (v_w_in, v_w_uq, v_w_ukv, v_conv_w, v_w_mk, v_w_mv, v_w_o)
    stacked = [None] * len(names)

    def sum_and_send(l, received):
        partial = [_sum_slots(r, f"l{l}_grad_sum_{names[i]}") for i, r in enumerate(received)]
        plan = _sibling_plan(partial)
        return l, partial, plan, _comm_start(plan, partial[0], f"l{l}_sibling_start")

    def receive_and_update(state, after):
        l, partial, plan, started = state
        other = _comm_finish(plan, started, after, f"l{l}_sibling_wait")
        for i, name in enumerate(names):
            stacked[i] = _adamw_layer(l, w_shards[i], partial[i], other[i], m_shards[i], v_shards[i], stacked[i],
                                      f"l{l}_adamw_{name}")

    dgs = [None] * depth
    pending = None
    in_flight = None
    for l in reversed(range(depth)):
        gains = layer_gains(l)
        for token in ([pending[1][2]] if pending else []) + ([in_flight[3][2]] if in_flight else []):
            gains = gains[:4] + (gains[4] + token[0:1, 0:1],)
        grad, contrib, dgs[l], early = _layer_bwd(l, grad, mem0, saved[l], wts[l], gains, tabs_bwd, None, l == 0)
        if in_flight is not None:
            receive_and_update(in_flight, grad)
            in_flight = None
        if pending is not None:
            in_flight = sum_and_send(l + 1, _comm_finish(pending[0], pending[1], grad, f"l{l + 1}_exchange_wait"))
        if l > 0:
            pending = (contrib, _comm_start(contrib, grad, f"l{l}_exchange_start"))
    early, late = early
    got_in = _comm_finish(late[0], late[1], grad, "l0_exchange_in_wait")
    last = sum_and_send(0, got_in + _comm_finish(early[0], early[1], got_in[0], "l0_exchange_rest_wait"))
    if in_flight is not None:
        receive_and_update(in_flight, last[1][0])
    receive_and_update(last, stacked[0][0] if depth > 1 else last[1][0])
    grad_x = grad[None]
    results = {name: tuple(stacked[i]) for i, name in enumerate(names)}
    results["w_in"] = tuple(flip(t) for t in results["w_in"])

    gain_names = ("pre_norm_g", "q_norm_g", "kv_norm_g", "mem_norm_g", "post_norm_g")
    dg_packed = _pack_gains([jnp.concatenate([dgs[l][i] for l in range(depth)], axis=0) for i in range(5)])
    dg_total = _sum_slots(_gather_all(dg_packed, "gain_gather"), "gain_sum")
    gain_outs = (dg_total,) + tuple(_adamw(
        _pack_gains((pre_norm_g, q_norm_g, kv_norm_g, mem_norm_g, post_norm_g)), dg_total,
        _pack_gains((m_pre_norm_g, m_q_norm_g, m_kv_norm_g, m_mem_norm_g, m_post_norm_g)),
        _pack_gains((v_pre_norm_g, v_q_norm_g, v_kv_norm_g, v_mem_norm_g, v_post_norm_g)), "adamw_gains"))
    gain_outs = [_unpack_gains(t, depth) for t in gain_outs]
    for i, name in enumerate(gain_names):
        results[name] = tuple(gain_outs[k][i] for k in range(4))

    order = ("pre_norm_g", "w_in", "q_norm_g", "w_uq", "kv_norm_g", "w_ukv", "conv_w", "mem_norm_g", "w_mk", "w_mv", "w_o",
             "post_norm_g")
    out = [loss, grad_x]
    for k in range(4):
        out += [results[name][k] for name in order]
    return tuple(out)
```

```python
import functools

import jax
import jax.numpy as jnp
from jax import lax
from jax.experimental import pallas as pl
from jax.experimental.pallas import tpu as pltpu

F32 = jnp.float32
BF16 = jnp.bfloat16
MESH_ID = pl.DeviceIdType.MESH

D_MODEL = 2048
EPS = 1e-6
LOG2_E = 1.4426950408889634
ROPE_THETA = 10000.0
MLA_HEADS = 8
NOPE = 128
ROPE = 64
HALF_ROPE = ROPE // 2
QK_HEAD = NOPE + ROPE
V_HEAD = 128
Q_RANK = 512
KV_RANK = 256
CONV_W = 512
MEM_HEADS = 4
MEM_HEAD = 128
MEM_W = MEM_HEADS * MEM_HEAD
MLA_W = MLA_HEADS * V_HEAD
MIX_W = MLA_W + CONV_W + MEM_W
IN_COLS = Q_RANK + KV_RANK + ROPE + 3 * CONV_W + MEM_W + MIX_W
N_CHIPS = 4
N_DEV = 8

LANES = 128
VMEM_LIMIT_BYTES = 56 * 1024 * 1024

QPAD = 2 * LANES
Z_GATE = 0
Z_QLAT = Z_GATE + MIX_W
Z_KVLAT = Z_QLAT + Q_RANK
Z_KPE = Z_KVLAT + KV_RANK
Z_GB = Z_KPE + LANES
Z_GC = Z_GB + CONV_W
Z_XIN = Z_GC + CONV_W
Z_QMEM = Z_XIN + CONV_W
Z_COLS = Z_QMEM + MEM_W

ADAM_LR = 0.001
ADAM_B1 = 0.9
ADAM_B2 = 0.999
ADAM_EPS = 1e-08
ADAM_WD = 0.01
ADAM_STEP = 10


def _tile(dim, cap, unit):
    if dim <= cap:
        return dim
    t = (cap // unit) * unit
    while t >= unit:
        if dim % t == 0:
            return t
        t -= unit
    raise ValueError(f"no tile of {dim} under {cap} in units of {unit}")


def _params(*semantics):
    return pltpu.CompilerParams(dimension_semantics=semantics, vmem_limit_bytes=VMEM_LIMIT_BYTES)


def _matmul(a, b, mode, out_dtype, name, tm_cap=512, tn_cap=1024, tk_cap=2048, after=None):
    if mode == "nn":
        (m, k), (k2, n) = a.shape, b.shape
    elif mode == "nt":
        (m, k), (n, k2) = a.shape, b.shape
    else:
        (k, m), (k2, n) = a.shape, b.shape
    assert k == k2, (a.shape, b.shape, mode)
    tm = _tile(m, tm_cap, LANES if mode == "tn" else 16)
    tn = _tile(n, tn_cap, LANES)
    tk = _tile(k, tk_cap, LANES if mode != "tn" else 16)
    nk = k // tk
    if mode == "nn":
        a_spec = pl.BlockSpec((tm, tk), lambda i, j, kk: (i, kk))
        b_spec = pl.BlockSpec((tk, tn), lambda i, j, kk: (kk, j))
        dims = (((1,), (0,)), ((), ()))
    elif mode == "nt":
        a_spec = pl.BlockSpec((tm, tk), lambda i, j, kk: (i, kk))
        b_spec = pl.BlockSpec((tn, tk), lambda i, j, kk: (j, kk))
        dims = (((1,), (1,)), ((), ()))
    else:
        a_spec = pl.BlockSpec((tk, tm), lambda i, j, kk: (kk, i))
        b_spec = pl.BlockSpec((tk, tn), lambda i, j, kk: (kk, j))
        dims = (((0,), (0,)), ((), ()))

    def body(a_ref, b_ref, *rest):
        o_ref, scratch = (rest[1], rest[2:]) if after is not None else (rest[0], rest[1:])
        part = lax.dot_general(a_ref[...].astype(BF16), b_ref[...].astype(BF16), dims, preferred_element_type=F32)
        if nk == 1:
            o_ref[...] = part.astype(o_ref.dtype)
            return
        (acc_ref,) = scratch
        kk = pl.program_id(2)

        @pl.when(kk == 0)
        def _():
            acc_ref[...] = part

        @pl.when(kk > 0)
        def _():
            acc_ref[...] += part

        @pl.when(kk == nk - 1)
        def _():
            o_ref[...] = acc_ref[...].astype(o_ref.dtype)

    return pl.pallas_call(
        body,
        grid=(m // tm, n // tn, nk),
        in_specs=[a_spec, b_spec] + ([] if after is None else [pl.BlockSpec(memory_space=pl.ANY)]),
        out_specs=pl.BlockSpec((tm, tn), lambda i, j, kk: (i, j)),
        out_shape=jax.ShapeDtypeStruct((m, n), out_dtype),
        scratch_shapes=[] if nk == 1 else [pltpu.VMEM((tm, tn), F32)],
        compiler_params=_params("parallel", "parallel", "arbitrary"),
        name=name,
    )(*([a, b] if after is None else [a, b, after]))


def _rmsnorm_fwd(x, gain, col0, width, name):
    rows = x.shape[0]
    tr = _tile(rows, 512, 16)
    cb = col0 // width
    assert cb * width == col0

    def body(x_ref, g_ref, o_ref):
        xv = x_ref[...].astype(F32)
        r = lax.rsqrt(jnp.mean(xv * xv, axis=-1, keepdims=True) + EPS)
        o_ref[...] = (xv * r * g_ref[...]).astype(o_ref.dtype)

    return pl.pallas_call(
        body,
        grid=(rows // tr,),
        in_specs=[pl.BlockSpec((tr, width), lambda i: (i, cb)), pl.BlockSpec((1, width), lambda i: (0, 0))],
        out_specs=pl.BlockSpec((tr, width), lambda i: (i, 0)),
        out_shape=jax.ShapeDtypeStruct((rows, width), BF16),
        compiler_params=_params("parallel"),
        name=name,
    )(x, gain)


def _rmsnorm_bwd(x, gain, dy, resid, col0, width, out_dtype, name):
    rows = x.shape[0]
    tr = _tile(rows, 256, 16)
    cb = col0 // width
    assert cb * width == col0
    has_resid = resid is not None

    def body(*refs):
        if has_resid:
            x_ref, g_ref, dy_ref, res_ref, dx_ref, dg_ref = refs
        else:
            x_ref, g_ref, dy_ref, dx_ref, dg_ref = refs
        i = pl.program_id(0)
        xv = x_ref[...].astype(F32)
        dyv = dy_ref[...].astype(F32)
        r = lax.rsqrt(jnp.mean(xv * xv, axis=-1, keepdims=True) + EPS)
        xr = xv * r
        dyg = dyv * g_ref[...]
        c = jnp.mean(dyg * xr, axis=-1, keepdims=True)
        dx = r * (dyg - xr * c)
        if has_resid:
            dx = dx + res_ref[...]
        dx_ref[...] = dx.astype(dx_ref.dtype)
        part = jnp.sum(dyv * xr, axis=0, keepdims=True)

        @pl.when(i == 0)
        def _():
            dg_ref[...] = part

        @pl.when(i > 0)
        def _():
            dg_ref[...] += part

    row_spec = pl.BlockSpec((tr, width), lambda i: (i, 0))
    in_specs = [pl.BlockSpec((tr, width), lambda i: (i, cb)), pl.BlockSpec((1, width), lambda i: (0, 0)), row_spec]
    args = [x, gain, dy]
    if has_resid:
        in_specs.append(row_spec)
        args.append(resid)
    return pl.pallas_call(
        body,
        grid=(rows // tr,),
        in_specs=in_specs,
        out_specs=[row_spec, pl.BlockSpec((1, width), lambda i: (0, 0))],
        out_shape=[jax.ShapeDtypeStruct((rows, width), out_dtype), jax.ShapeDtypeStruct((1, width), F32)],
        compiler_params=_params("arbitrary"),
        name=name,
    )(*args)


def _post_norm_residual(x, o, gain, name):
    rows, width = x.shape
    tr = _tile(rows, 256, 8)

    def body(x_ref, o_ref, g_ref, out_ref):
        ov = o_ref[...]
        r = lax.rsqrt(jnp.mean(ov * ov, axis=-1, keepdims=True) + EPS)
        out_ref[...] = x_ref[...] + ov * r * g_ref[...]

    row_spec = pl.BlockSpec((tr, width), lambda i: (i, 0))
    return pl.pallas_call(
        body,
        grid=(rows // tr,),
        in_specs=[row_spec, row_spec, pl.BlockSpec((1, width), lambda i: (0, 0))],
        out_specs=row_spec,
        out_shape=jax.ShapeDtypeStruct((rows, width), F32),
        compiler_params=_params("parallel"),
        name=name,
    )(x, o, gain)


def _rope(x, tab_c, tab_a, tab_b, col0, width, heads, name):
    rows = x.shape[0]
    tr = _tile(rows, 512, 16)
    cb = col0 // width
    assert cb * width == col0

    def body(x_ref, c_ref, a_ref, b_ref, o_ref):
        xv = x_ref[...].astype(F32)
        up = pltpu.roll(xv, width - HALF_ROPE, 1)
        down = pltpu.roll(xv, HALF_ROPE, 1)
        o_ref[...] = (xv * c_ref[...] + up * a_ref[...] + down * b_ref[...]).astype(o_ref.dtype)

    tab_spec = pl.BlockSpec((tr, width), lambda i, h: (i, 0))
    return pl.pallas_call(
        body,
        grid=(rows // tr, heads),
        in_specs=[pl.BlockSpec((tr, width), lambda i, h: (i, cb + h)), tab_spec, tab_spec, tab_spec],
        out_specs=pl.BlockSpec((tr, width), lambda i, h: (i, h)),
        out_shape=jax.ShapeDtypeStruct((rows, heads * width), BF16),
        compiler_params=_params("parallel", "parallel"),
        name=name,
    )(x, tab_c, tab_a, tab_b)


def _kpe_grad(dkb, tab_c, tab_a, tab_b, heads, name):
    rows = dkb.shape[0]
    tr = _tile(rows, 512, 16)

    def body(d_ref, c_ref, a_ref, b_ref, o_ref):
        acc = d_ref[:, 0:LANES]
        for h in range(1, heads):
            acc = acc + d_ref[:, h * LANES:(h + 1) * LANES]
        up = pltpu.roll(acc, LANES - HALF_ROPE, 1)
        down = pltpu.roll(acc, HALF_ROPE, 1)
        o_ref[...] = (acc * c_ref[...] + up * a_ref[...] + down * b_ref[...]).astype(o_ref.dtype)

    tab_spec = pl.BlockSpec((tr, LANES), lambda i: (i, 0))
    return pl.pallas_call(
        body,
        grid=(rows // tr,),
        in_specs=[pl.BlockSpec((tr, heads * LANES), lambda i: (i, 0)), tab_spec, tab_spec, tab_spec],
        out_specs=tab_spec,
        out_shape=jax.ShapeDtypeStruct((rows, LANES), BF16),
        compiler_params=_params("parallel"),
        name=name,
    )(dkb, tab_c, tab_a, tab_b)


class _CommPlan:
    def __init__(self, ins, out_shape, build, n_copies):
        self.ins, self.out_shape, self.build, self.n_copies = list(ins), list(out_shape), build, n_copies

    def scratch(self):
        n = self.n_copies
        return [pltpu.SemaphoreType.DMA((n,)), pltpu.SemaphoreType.DMA((n,)), pltpu.SemaphoreType.DMA((n,))]


def _split_comm(refs, n_in, n_out, comm):
    if comm is None:
        return refs, None
    ci, co = len(comm.ins), len(comm.out_shape)
    ins, c_ins = refs[:n_in], refs[n_in:n_in + ci]
    outs, c_outs = refs[n_in + ci:n_in + ci + n_out], refs[n_in + ci + n_out:n_in + ci + n_out + co]
    rest = refs[n_in + ci + n_out + co:]
    scratch, sems = rest[:-3], rest[-3:]
    return tuple(ins) + tuple(outs) + tuple(scratch), functools.partial(comm.build, c_ins, c_outs, sems)


def _ride_start(copies, first):
    if copies is not None:
        pl.when(first)(copies()[0])


def _ride_wait(copies, last):
    if copies is not None:
        pl.when(last)(copies()[1])


def _rope_rows(x, c, a, b, sign):
    width = x.shape[-1]
    mixed = pltpu.roll(x, width - HALF_ROPE, 1) * a + pltpu.roll(x, HALF_ROPE, 1) * b
    return x * c + mixed if sign > 0 else x * c - mixed


def _attn_fwd(q, ka, kb, v, rope, heads, q_w, q_cb, ka_cb, ka_step, v_cb, v_step, scale, tq_cap, name, comm=None, tk_cap=512,
              o_into=None):
    s_q, s_k = q.shape[0], ka.shape[0]
    tq = _tile(s_q, tq_cap, 16)
    nq = s_q // tq
    has_kb = kb is not None
    n_in = 7 if has_kb else 3
    tk = _tile(s_k, tk_cap, LANES)
    o_cols, o_cb, o_old = o_into if o_into is not None else (heads * LANES, 0, None)
    assert comm is None or o_old is None

    def body(*refs):
        if o_old is not None:
            refs = refs[:n_in] + refs[n_in + 1:]
        refs, copies = _split_comm(refs, n_in, 2, comm)
        first = jnp.logical_and(pl.program_id(0) == 0, pl.program_id(1) == 0)
        last = jnp.logical_and(pl.program_id(0) == heads - 1, pl.program_id(1) == nq - 1)
        _ride_start(copies, first)
        if has_kb:
            q_ref, ka_ref, kb_ref, v_ref, c_ref, a_ref, b_ref, o_ref, lse_ref, k_scr = refs

            @pl.when(pl.program_id(1) == 0)
            def _():
                k_scr[:, 0:LANES] = ka_ref[...].astype(BF16)
                k_scr[:, LANES:2 * LANES] = kb_ref[...].astype(BF16)

            keys = k_scr
            qv = _rope_rows(q_ref[...], c_ref[...], a_ref[...], b_ref[...], 1).astype(BF16)
        else:
            q_ref, ka_ref, v_ref, o_ref, lse_ref = refs
            keys = ka_ref
            qv = q_ref[...].astype(BF16)
        c2 = scale * LOG2_E
        m = l = o = None
        nk = s_k // tk
        scores = lambda j: lax.dot_general(qv, keys[j * tk:(j + 1) * tk, :].astype(BF16), (((1,), (1,)), ((), ())),
                                           preferred_element_type=F32)
        s_next = scores(0)
        for j in range(nk):
            sj = s_next
            if j + 1 < nk:
                s_next = scores(j + 1)
            mj = jnp.max(sj, axis=-1, keepdims=True)
            m_new = mj if m is None else jnp.maximum(m, mj)
            pj = jnp.exp2((sj - m_new) * c2)
            lj = jnp.sum(pj, axis=-1, keepdims=True)
            oj = jnp.dot(pj.astype(BF16), v_ref[j * tk:(j + 1) * tk, :].astype(BF16), preferred_element_type=F32)
            if m is None:
                l, o = lj, oj
            else:
                alpha = jnp.exp2((m - m_new) * c2)
                l, o = l * alpha + lj, o * alpha + oj
            m = m_new
        o_ref[...] = (o * (1.0 / l)).astype(o_ref.dtype)
        lse_ref[...] = jnp.broadcast_to(m * c2 + jnp.log2(l), lse_ref.shape)
        _ride_wait(copies, last)

    in_specs = [pl.BlockSpec((tq, q_w), lambda h, i: (i, q_cb + h)),
                pl.BlockSpec((s_k, LANES), lambda h, i: (0, ka_cb + ka_step * h))]
    args = [q, ka]
    if has_kb:
        in_specs.append(pl.BlockSpec((s_k, LANES), lambda h, i: (0, 0)))
        args.append(kb)
    in_specs.append(pl.BlockSpec((s_k, LANES), lambda h, i: (0, v_cb + v_step * h)))
    args.append(v)
    if has_kb:
        in_specs += [pl.BlockSpec((tq, q_w), lambda h, i: (i, 0))] * 3
        args += list(rope)
    aliases = {}
    if o_old is not None:
        aliases = {len(args): 0}
        in_specs.append(ANY)
        args.append(o_old)
    out_specs = [pl.BlockSpec((tq, LANES), lambda h, i: (i, o_cb + h)), pl.BlockSpec((tq, LANES), lambda h, i: (i, h))]
    out_shape = [jax.ShapeDtypeStruct((s_q, o_cols), BF16), jax.ShapeDtypeStruct((s_q, heads * LANES), F32)]
    scratch = [pltpu.VMEM((s_k, 2 * LANES), BF16)] if has_kb else []
    if comm is not None:
        in_specs += [ANY] * len(comm.ins)
        args += comm.ins
        out_specs += [ANY] * len(comm.out_shape)
        out_shape += comm.out_shape
        scratch += comm.scratch()
    res = pl.pallas_call(
        body,
        grid=(heads, nq),
        in_specs=in_specs,
        out_specs=out_specs,
        out_shape=out_shape,
        scratch_shapes=scratch,
        input_output_aliases=aliases,
        compiler_params=_params("arbitrary", "arbitrary"),
        name=name,
    )(*args)
    return res[0], res[1], list(res[2:])


def _attn_bwd(q, ka, kb, v, o, do, lse, rope, heads, q_w, q_cb, ka_cb, ka_step, v_cb, v_step, o_cb, scale, tq_cap, name,
              comm=None, tk_cap=512):
    s_q, s_k = q.shape[0], ka.shape[0]
    tq = _tile(s_q, tq_cap, 16)
    nq = s_q // tq
    has_kb = kb is not None
    n_in = 10 if has_kb else 6
    n_out = 3
    tk = _tile(s_k, tk_cap, LANES)

    def body(*refs):
        refs, copies = _split_comm(refs, n_in, n_out, comm)
        first = jnp.logical_and(pl.program_id(0) == 0, pl.program_id(1) == 0)
        last = jnp.logical_and(pl.program_id(0) == heads - 1, pl.program_id(1) == nq - 1)
        _ride_start(copies, first)
        if has_kb:
            (q_ref, ka_ref, kb_ref, v_ref, o_ref, do_ref, lse_ref, c_ref, a_ref, b_ref, dq_ref, dkv_ref, dkb_ref, k_scr, dk_acc,
             dv_acc) = refs
        else:
            q_ref, ka_ref, v_ref, o_ref, do_ref, lse_ref, dq_ref, dka_ref, dv_ref, dk_acc, dv_acc = refs
        i = pl.program_id(1)

        @pl.when(i == 0)
        def _():
            dk_acc[...] = jnp.zeros_like(dk_acc)
            dv_acc[...] = jnp.zeros_like(dv_acc)
            if has_kb:
                k_scr[:, 0:LANES] = ka_ref[...].astype(BF16)
                k_scr[:, LANES:2 * LANES] = kb_ref[...].astype(BF16)

        keys = k_scr if has_kb else ka_ref
        if has_kb:
            qv = _rope_rows(q_ref[...], c_ref[...], a_ref[...], b_ref[...], 1).astype(BF16)
        else:
            qv = q_ref[...].astype(BF16)
        dov = do_ref[...].astype(BF16)
        delta = jnp.sum(dov.astype(F32) * o_ref[...].astype(F32), axis=-1, keepdims=True)
        lse2 = lse_ref[:, 0:1]
        c2 = scale * LOG2_E
        nk = s_k // tk
        rows = lambda j: slice(j * tk, (j + 1) * tk)
        nt = (((1,), (1,)), ((), ()))
        tn = (((0,), (0,)), ((), ()))

        def scores(j):
            return (lax.dot_general(qv, keys[rows(j), :].astype(BF16), nt, preferred_element_type=F32),
                    lax.dot_general(dov, v_ref[rows(j), :].astype(BF16), nt, preferred_element_type=F32))

        nxt = scores(0)
        dq = None
        for j in range(nk):
            sj, dpj = nxt
            if j + 1 < nk:
                nxt = scores(j + 1)
            pj = jnp.exp2(sj * c2 - lse2)
            dsj = (pj * (dpj - delta)).astype(BF16)
            dqj = jnp.dot(dsj, keys[rows(j), :].astype(BF16), preferred_element_type=F32)
            dq = dqj if dq is None else dq + dqj
            dk_acc[rows(j), :] += lax.dot_general(dsj, qv, tn, preferred_element_type=F32)
            dv_acc[rows(j), :] += lax.dot_general(pj.astype(BF16), dov, tn, preferred_element_type=F32)
        dq = dq * scale
        if has_kb:
            dq = _rope_rows(dq, c_ref[...], a_ref[...], b_ref[...], -1)
        dq_ref[...] = dq.astype(dq_ref.dtype)

        @pl.when(i == nq - 1)
        def _():
            if has_kb:
                dkv_ref[:, 0:LANES] = (dk_acc[:, 0:LANES] * scale).astype(dkv_ref.dtype)
                dkv_ref[:, LANES:2 * LANES] = dv_acc[...].astype(dkv_ref.dtype)
                dkb_ref[...] = dk_acc[:, LANES:2 * LANES] * scale
            else:
                dka_ref[...] = (dk_acc[...] * scale).astype(dka_ref.dtype)
                dv_ref[...] = dv_acc[...].astype(dv_ref.dtype)

        _ride_wait(copies, last)

    key_spec = lambda cb, step: pl.BlockSpec((s_k, LANES), lambda h, i: (0, cb + step * h))
    row_spec = lambda cb: pl.BlockSpec((tq, LANES), lambda h, i: (i, cb + h))
    in_specs = [pl.BlockSpec((tq, q_w), lambda h, i: (i, q_cb + h)), key_spec(ka_cb, ka_step)]
    args = [q, ka]
    if has_kb:
        in_specs.append(pl.BlockSpec((s_k, LANES), lambda h, i: (0, 0)))
        args.append(kb)
    in_specs += [key_spec(v_cb, v_step), row_spec(o_cb), row_spec(o_cb), row_spec(0)]
    args += [v, o, do, lse]
    if has_kb:
        in_specs += [pl.BlockSpec((tq, q_w), lambda h, i: (i, 0))] * 3
        args += list(rope)
    out_specs = [pl.BlockSpec((tq, q_w), lambda h, i: (i, h))]
    out_shape = [jax.ShapeDtypeStruct((s_q, heads * q_w), BF16)]
    scratch = []
    if has_kb:
        out_specs += [pl.BlockSpec((s_k, 2 * LANES), lambda h, i: (0, h)), key_spec(0, 1)]
        out_shape += [jax.ShapeDtypeStruct((s_k, heads * 2 * LANES), BF16), jax.ShapeDtypeStruct((s_k, heads * LANES), F32)]
        scratch.append(pltpu.VMEM((s_k, 2 * LANES), BF16))
    else:
        out_specs += [key_spec(0, 1), key_spec(0, 1)]
        out_shape += [jax.ShapeDtypeStruct((s_k, heads * LANES), BF16)] * 2
    scratch += [pltpu.VMEM((s_k, q_w), F32), pltpu.VMEM((s_k, LANES), F32)]
    if comm is not None:
        in_specs += [ANY] * len(comm.ins)
        args += comm.ins
        out_specs += [ANY] * len(comm.out_shape)
        out_shape += comm.out_shape
        scratch += comm.scratch()
    res = pl.pallas_call(
        body,
        grid=(heads, nq),
        in_specs=in_specs,
        out_specs=out_specs,
        out_shape=out_shape,
        scratch_shapes=scratch,
        compiler_params=_params("arbitrary", "arbitrary"),
        name=name,
    )(*args)
    return res[0], res[1], res[2], list(res[3:])


def _shift_rows(u, rows):
    t = lax.broadcasted_iota(jnp.int32, u.shape, 0)
    prev = jnp.where(t == 0, 0.0, pltpu.roll(u, 1, 0))
    nxt = jnp.where(t == rows - 1, 0.0, pltpu.roll(u, rows - 1, 0))
    return prev, nxt


def _conv_fwd(z, conv_w, cat, name):
    rows = z.shape[0]
    nblk = CONV_W // LANES

    def body(gb_ref, gc_ref, xin_ref, w_ref, cat_ref, o_ref):
        del cat_ref
        u = gc_ref[...].astype(F32) * xin_ref[...].astype(F32)
        prev, nxt = _shift_rows(u, rows)
        conv = prev * w_ref[0:1, :] + u * w_ref[1:2, :] + nxt * w_ref[2:3, :]
        o_ref[...] = (gb_ref[...].astype(F32) * conv).astype(o_ref.dtype)

    col = lambda c0: pl.BlockSpec((rows, LANES), lambda j: (0, c0 // LANES + j))
    return pl.pallas_call(
        body,
        grid=(nblk,),
        in_specs=[col(Z_GB), col(Z_GC), col(Z_XIN), pl.BlockSpec((3, LANES), lambda j: (0, j)), ANY],
        out_specs=col(MLA_W),
        out_shape=jax.ShapeDtypeStruct(cat.shape, cat.dtype),
        input_output_aliases={4: 0},
        compiler_params=_params("parallel"),
        name=name,
    )(z, z, z, conv_w, cat)


def _conv_bwd(z, conv_w, dcat, name):
    rows = z.shape[0]
    nblk = CONV_W // LANES

    def body(gb_ref, gc_ref, xin_ref, w_ref, dc_ref, dgb_ref, dgc_ref, dxin_ref, dw_ref):
        gc = gc_ref[...].astype(F32)
        xin = xin_ref[...].astype(F32)
        dc = dc_ref[...].astype(F32)
        u = gc * xin
        prev, nxt = _shift_rows(u, rows)
        w0, w1, w2 = w_ref[0:1, :], w_ref[1:2, :], w_ref[2:3, :]
        conv = prev * w0 + u * w1 + nxt * w2
        dgb_ref[...] = (dc * conv).astype(dgb_ref.dtype)
        dconv = dc * gb_ref[...].astype(F32)
        dw_ref[0:1, :] = jnp.sum(dconv * prev, axis=0, keepdims=True)
        dw_ref[1:2, :] = jnp.sum(dconv * u, axis=0, keepdims=True)
        dw_ref[2:3, :] = jnp.sum(dconv * nxt, axis=0, keepdims=True)
        dprev, dnxt = _shift_rows(dconv, rows)
        du = dnxt * w0 + dconv * w1 + dprev * w2
        dgc_ref[...] = (du * xin).astype(dgc_ref.dtype)
        dxin_ref[...] = (du * gc).astype(dxin_ref.dtype)

    col = lambda c0: pl.BlockSpec((rows, LANES), lambda j: (0, c0 // LANES + j))
    w_spec = pl.BlockSpec((3, LANES), lambda j: (0, j))
    piece = jax.ShapeDtypeStruct((rows, CONV_W), BF16)
    return pl.pallas_call(
        body,
        grid=(nblk,),
        in_specs=[col(Z_GB), col(Z_GC), col(Z_XIN), w_spec, col(MLA_W)],
        out_specs=[col(0), col(0), col(0), w_spec],
        out_shape=[piece, piece, piece, jax.ShapeDtypeStruct((3, CONV_W), F32)],
        compiler_params=_params("parallel"),
        name=name,
    )(z, z, z, conv_w, dcat)


def _gate_fwd(cat, z, name):
    rows = cat.shape[0]
    tr = _tile(rows, 256, 16)
    tc = MIX_W
    g0 = Z_GATE // tc

    def body(c_ref, g_ref, y_ref):
        g = g_ref[...].astype(F32)
        y_ref[...] = (c_ref[...].astype(F32) * (g * jax.nn.sigmoid(g))).astype(y_ref.dtype)

    blk = pl.BlockSpec((tr, tc), lambda i, j: (i, j))
    return pl.pallas_call(
        body,
        grid=(rows // tr, MIX_W // tc),
        in_specs=[blk, pl.BlockSpec((tr, tc), lambda i, j: (i, g0 + j))],
        out_specs=blk,
        out_shape=jax.ShapeDtypeStruct((rows, MIX_W), BF16),
        compiler_params=_params("parallel", "parallel"),
        name=name,
    )(cat, z)


def _out_proj_dx_gate_bwd(do, w_o, cat, z, name):
    rows, k = do.shape
    tm = _tile(rows, 512, 16)
    tn = _tile(MIX_W, 1024, LANES)
    g0 = Z_GATE // tn

    def body(do_ref, w_ref, c_ref, g_ref, dcat_ref, dgate_ref):
        dy = lax.dot_general(do_ref[...], w_ref[...], (((1,), (1,)), ((), ())), preferred_element_type=F32)
        g = g_ref[...].astype(F32)
        sg = jax.nn.sigmoid(g)
        dcat_ref[...] = (dy * (g * sg)).astype(dcat_ref.dtype)
        dgate_ref[...] = (dy * c_ref[...].astype(F32) * (sg * (1.0 + g * (1.0 - sg)))).astype(dgate_ref.dtype)

    blk = pl.BlockSpec((tm, tn), lambda i, j: (i, j))
    out = jax.ShapeDtypeStruct((rows, MIX_W), BF16)
    return pl.pallas_call(
        body,
        grid=(rows // tm, MIX_W // tn),
        in_specs=[pl.BlockSpec((tm, k), lambda i, j: (i, 0)), pl.BlockSpec((tn, k), lambda i, j: (j, 0)), blk,
                  pl.BlockSpec((tm, tn), lambda i, j: (i, g0 + j))],
        out_specs=[blk, blk],
        out_shape=[out, out],
        compiler_params=_params("parallel", "parallel"),
        name=name,
    )(do, w_o, cat, z)


def _loss_head(y, target, name):
    rows, width = y.shape
    tr = _tile(rows, 256, 8)

    def body(y_ref, t_ref, g_ref, loss_ref):
        i = pl.program_id(0)
        d = y_ref[...] - t_ref[...]
        g_ref[...] = d / width
        part = 0.5 * jnp.sum(jnp.mean(d * d, axis=-1, keepdims=True), axis=0, keepdims=True)
        part = jnp.broadcast_to(part, loss_ref.shape)

        @pl.when(i == 0)
        def _():
            loss_ref[...] = part

        @pl.when(i > 0)
        def _():
            loss_ref[...] += part

    row_spec = pl.BlockSpec((tr, width), lambda i: (i, 0))
    return pl.pallas_call(
        body,
        grid=(rows // tr,),
        in_specs=[row_spec, row_spec],
        out_specs=[row_spec, pl.BlockSpec((1, LANES), lambda i: (0, 0))],
        out_shape=[jax.ShapeDtypeStruct((rows, width), F32), jax.ShapeDtypeStruct((1, LANES), F32)],
        compiler_params=_params("arbitrary"),
        name=name,
    )(y, target)


CHIP_FLIPS = ((1, 0), (0, 1), (1, 1))
ANY = pl.BlockSpec(memory_space=pl.ANY)


def _chip_copies(pieces, sems, n_slot):
    send_sems, recv_sems, local_sems = sems
    x, y, c = lax.axis_index("x"), lax.axis_index("y"), lax.axis_index("c")
    me = 2 * x + y

    def remote(j, k, a, src, dst):
        fx, fy = CHIP_FLIPS[k]
        return pltpu.make_async_remote_copy(
            src_ref=src, dst_ref=dst, send_sem=send_sems.at[n_slot * k + a], recv_sem=recv_sems.at[n_slot * k + a],
            device_id=((j // 2) ^ fx, (j % 2) ^ fy, c), device_id_type=MESH_ID)

    def peer(j, k):
        fx, fy = CHIP_FLIPS[k]
        return 2 * ((j // 2) ^ fx) + ((j % 2) ^ fy)

    def start_as(j):
        def run():
            for a, (src, dst) in enumerate(pieces(j, j)):
                pltpu.make_async_copy(src, dst, local_sems.at[a]).start()
            for k in range(len(CHIP_FLIPS)):
                for a, (src, dst) in enumerate(pieces(j, peer(j, k))):
                    remote(j, k, a, src, dst).start()
        return run

    def wait_as(j):
        def run():
            for a, (src, dst) in enumerate(pieces(j, j)):
                pltpu.make_async_copy(src, dst, local_sems.at[a]).wait()
            for k in range(len(CHIP_FLIPS)):
                for a, (src, dst) in enumerate(pieces(j, peer(j, k))):
                    remote(j, k, a, src, dst).wait_send()
                for a, (src, dst) in enumerate(pieces(peer(j, k), j)):
                    remote(j, k, a, src, dst).wait_recv()
        return run

    def start():
        for j in range(N_CHIPS):
            pl.when(me == j)(start_as(j))

    def wait():
        for j in range(N_CHIPS):
            pl.when(me == j)(wait_as(j))

    return start, wait


IN_PIECES = ((0, Q_RANK, Z_QLAT), (Q_RANK, KV_RANK, Z_KVLAT), (Q_RANK + KV_RANK, ROPE, Z_KPE),
             (Q_RANK + KV_RANK + ROPE, CONV_W, Z_GB), (Q_RANK + KV_RANK + ROPE + CONV_W, CONV_W, Z_GC),
             (Q_RANK + KV_RANK + ROPE + 2 * CONV_W, CONV_W, Z_XIN), (Q_RANK + KV_RANK + ROPE + 3 * CONV_W, MEM_W, Z_QMEM),
             (Q_RANK + KV_RANK + ROPE + 3 * CONV_W + MEM_W, MIX_W, Z_GATE))
IN_SHARD = IN_COLS // N_CHIPS


def _in_segments(j):
    lo, hi = j * IN_SHARD, (j + 1) * IN_SHARD
    segs = []
    for r0, width, z0 in IN_PIECES:
        a, b = max(lo, r0), min(hi, r0 + width)
        if a < b:
            segs.append((a - lo, z0 + a - r0, b - a))
    return segs


N_SLOT = 11


def _gather_plan(l, shards, zero_rows, part="all"):
    s_in, s_uq, s_ukv, s_conv, s_mk, s_mv, s_o = shards
    ukv_c, mk_r, mk_c, o_r = s_ukv.shape[2], s_mk.shape[1], s_mk.shape[2], s_o.shape[1]
    stack = lambda s: jax.ShapeDtypeStruct((N_CHIPS,) + s.shape[1:], s.dtype)
    in_ins, in_outs = [s_in, zero_rows], [jax.ShapeDtypeStruct((Z_COLS, s_in.shape[2]), s_in.dtype)]
    rest_ins = [s_uq, s_ukv, s_conv, s_mk, s_mv, s_o]
    rest_outs = [stack(s_uq), jax.ShapeDtypeStruct((s_ukv.shape[1], N_CHIPS * ukv_c), s_ukv.dtype), stack(s_conv),
                 jax.ShapeDtypeStruct((N_CHIPS * mk_r, 2 * mk_c), s_mk.dtype),
                 jax.ShapeDtypeStruct((N_CHIPS * o_r, s_o.shape[2]), s_o.dtype)]
    with_in, with_rest = part != "rest", part != "in"

    def build(ins, outs, sems):
        ins, outs = list(ins), list(outs)
        if with_in:
            r_in, r_zero, f_in = ins.pop(0), ins.pop(0), outs.pop(0)
        if with_rest:
            r_uq, r_ukv, r_conv, r_mk, r_mv, r_o = ins
            g_uq, f_ukv, g_conv, f_mkv, f_o = outs

        def pieces(j, t):
            out = []
            if with_in:
                out += [(r_in.at[l, pl.ds(so, n), :], f_in.at[pl.ds(zo, n), :]) for so, zo, n in _in_segments(j)]
            if with_rest:
                out += [(r_uq.at[l], g_uq.at[j]), (r_ukv.at[l], f_ukv.at[:, pl.ds(j * ukv_c, ukv_c)]),
                        (r_conv.at[l], g_conv.at[j]),
                        (r_mk.at[l], f_mkv.at[pl.ds(j * mk_r, mk_r), pl.ds(0, mk_c)]),
                        (r_mv.at[l], f_mkv.at[pl.ds(j * mk_r, mk_r), pl.ds(mk_c, mk_c)]),
                        (r_o.at[l], f_o.at[pl.ds(j * o_r, o_r), :])]
            if with_in and j == t:
                out.append((r_zero, f_in.at[pl.ds(Z_KPE + ROPE, LANES - ROPE), :]))
            return out

        return _chip_copies(pieces, sems, N_SLOT)

    ins = (in_ins if with_in else []) + (rest_ins if with_rest else [])
    outs = (in_outs if with_in else []) + (rest_outs if with_rest else [])
    return _CommPlan(ins, outs, build, len(CHIP_FLIPS) * N_SLOT)


def _scatter_plan(dwt_in, c_uq, dw_ukv, c_conv, dw_mkv, dw_o, part="all"):
    ukv_c, mk_r, mk_c, o_r = dw_ukv.shape[1] // N_CHIPS, dw_mkv.shape[0] // N_CHIPS, dw_mkv.shape[1] // 2, dw_o.shape[0] // N_CHIPS
    with_in, with_rest = part != "rest", part != "in"
    in_outs = [jax.ShapeDtypeStruct((N_CHIPS, IN_SHARD, D_MODEL), BF16)]
    rest_ins = [c_uq, dw_ukv, c_conv, dw_mkv, dw_o]
    rest_outs = [jax.ShapeDtypeStruct(c_uq.shape, c_uq.dtype),
                 jax.ShapeDtypeStruct((N_CHIPS, dw_ukv.shape[0], ukv_c), dw_ukv.dtype),
                 jax.ShapeDtypeStruct(c_conv.shape, c_conv.dtype),
                 jax.ShapeDtypeStruct((N_CHIPS, mk_r, mk_c), dw_mkv.dtype), jax.ShapeDtypeStruct((N_CHIPS, mk_r, mk_c), dw_mkv.dtype),
                 jax.ShapeDtypeStruct((N_CHIPS, o_r, dw_o.shape[1]), dw_o.dtype)]

    def build(ins, outs, sems):
        ins, outs = list(ins), list(outs)
        if with_in:
            r_in, o_in = ins.pop(0), outs.pop(0)
        if with_rest:
            r_uq, r_ukv, r_conv, r_mkv, r_o = ins
            o_uq, o_ukv, o_conv, o_mk, o_mv, o_o = outs

        def pieces(j, t):
            out = []
            if with_in:
                out += [(r_in.at[pl.ds(zo, n), :], o_in.at[j, pl.ds(so, n), :]) for so, zo, n in _in_segments(t)]
            if with_rest:
                out += [(r_uq.at[t], o_uq.at[j]), (r_ukv.at[:, pl.ds(t * ukv_c, ukv_c)], o_ukv.at[j]),
                        (r_conv.at[t], o_conv.at[j]),
                        (r_mkv.at[pl.ds(t * mk_r, mk_r), pl.ds(0, mk_c)], o_mk.at[j]),
                        (r_mkv.at[pl.ds(t * mk_r, mk_r), pl.ds(mk_c, mk_c)], o_mv.at[j]),
                        (r_o.at[pl.ds(t * o_r, o_r), :], o_o.at[j])]
            return out

        return _chip_copies(pieces, sems, N_SLOT)

    ins = ([dwt_in] if with_in else []) + (rest_ins if with_rest else [])
    outs = (in_outs if with_in else []) + (rest_outs if with_rest else [])
    return _CommPlan(ins, outs, build, len(CHIP_FLIPS) * N_SLOT)


HBM = pl.BlockSpec(memory_space=pltpu.HBM)
SEM = pl.BlockSpec(memory_space=pltpu.SEMAPHORE)
SIDE_EFFECT = pltpu.SideEffectType.DATAFLOW_SIDE_EFFECTING


def _comm_start(plan, after, name):
    n_in, n_out = len(plan.ins), len(plan.out_shape)
    n_buf = n_in + n_out

    def body(*refs):
        bufs, sems, token = refs[:n_buf], refs[n_buf + 1:n_buf + 4], refs[-1]
        start, _ = plan.build(bufs[:n_in], bufs[n_in:], sems)
        start()
        token[...] = jnp.zeros_like(token)

    lands = [lax.empty(s.shape, s.dtype) for s in plan.out_shape]
    args = [pltpu.with_memory_space_constraint(a, pltpu.HBM) for a in list(plan.ins) + lands]
    res = pl.pallas_call(
        body,
        in_specs=[HBM] * n_buf + [ANY],
        out_specs=[SEM] * 3 + [HBM] * n_buf + [pl.BlockSpec(memory_space=pltpu.VMEM)],
        out_shape=plan.scratch() + [pltpu.HBM(a.shape, a.dtype) for a in args] + [jax.ShapeDtypeStruct((8, LANES), F32)],
        input_output_aliases={i: 3 + i for i in range(n_buf)},
        compiler_params=pltpu.CompilerParams(has_side_effects=SIDE_EFFECT),
        name=name,
    )(*args, after)
    return list(res[:3]), list(res[3:3 + n_buf]), res[-1]


def _comm_finish(plan, started, after, name):
    sems, bufs, _ = started
    n_in, n_out = len(plan.ins), len(plan.out_shape)
    n_buf = n_in + n_out

    def body(*refs):
        bufs_in, sem_refs = refs[:n_buf], refs[n_buf:n_buf + 3]
        _, wait = plan.build(bufs_in[:n_in], bufs_in[n_in:], sem_refs)
        wait()

    res = pl.pallas_call(
        body,
        in_specs=[HBM] * n_buf + [SEM] * 3 + [ANY],
        out_specs=[HBM] * n_buf,
        out_shape=[pltpu.HBM(b.shape, b.dtype) for b in bufs],
        input_output_aliases={i: i for i in range(n_buf)},
        compiler_params=pltpu.CompilerParams(has_side_effects=SIDE_EFFECT),
        name=name,
    )(*bufs, *sems, after)
    return list(res[n_in:])


def _comm_call(plan, name):
    n_in, n_out = len(plan.ins), len(plan.out_shape)

    def body(*refs):
        start, wait = plan.build(refs[:n_in], refs[n_in:n_in + n_out], refs[n_in + n_out:])
        start()
        wait()

    return list(pl.pallas_call(
        body,
        in_specs=[ANY] * n_in,
        out_specs=[ANY] * n_out,
        out_shape=plan.out_shape,
        scratch_shapes=plan.scratch(),
        name=name,
    )(*plan.ins))


def _sibling_plan(arrays):
    def build(ins, outs, sems):
        send_sems, recv_sems, _ = sems
        sibling = (lax.axis_index("x"), lax.axis_index("y"), 1 - lax.axis_index("c"))
        copies = [pltpu.make_async_remote_copy(src_ref=src, dst_ref=dst, send_sem=send_sems.at[a], recv_sem=recv_sems.at[a],
                                               device_id=sibling, device_id_type=MESH_ID)
                  for a, (src, dst) in enumerate(zip(ins, outs))]

        def start():
            for cp in copies:
                cp.start()

        def wait():
            for cp in copies:
                cp.wait()

        return start, wait

    return _CommPlan(arrays, [jax.ShapeDtypeStruct(v.shape, v.dtype) for v in arrays], build, len(arrays))


DEVICE_FLIPS = tuple((fx, fy, fc) for fx in (0, 1) for fy in (0, 1) for fc in (0, 1))[1:]


def _gather_all(v, name):
    def body(v_ref, out_ref, send_sems, recv_sems, local_sem):
        x, y, c = lax.axis_index("x"), lax.axis_index("y"), lax.axis_index("c")
        me = 4 * x + 2 * y + c
        local = pltpu.make_async_copy(v_ref, out_ref.at[me], local_sem)
        local.start()
        copies = [local]
        for k, (fx, fy, fc) in enumerate(DEVICE_FLIPS):
            cp = pltpu.make_async_remote_copy(
                src_ref=v_ref, dst_ref=out_ref.at[me], send_sem=send_sems.at[k], recv_sem=recv_sems.at[k],
                device_id=((x + fx) % 2, (y + fy) % 2, (c + fc) % 2), device_id_type=MESH_ID)
            cp.start()
            copies.append(cp)
        for cp in copies:
            cp.wait()

    return pl.pallas_call(
        body,
        in_specs=[ANY],
        out_specs=ANY,
        out_shape=jax.ShapeDtypeStruct((N_DEV,) + v.shape, v.dtype),
        scratch_shapes=[pltpu.SemaphoreType.DMA((N_DEV - 1,)), pltpu.SemaphoreType.DMA((N_DEV - 1,)), pltpu.SemaphoreType.DMA],
        name=name,
    )(v)


def _sum_slots(parts, name):
    n, rows, cols = parts.shape
    tr = _tile(rows, 256, 16)

    def body(p_ref, o_ref):
        acc = p_ref[0].astype(F32)
        for k in range(1, n):
            acc = acc + p_ref[k].astype(F32)
        o_ref[...] = acc

    return pl.pallas_call(
        body,
        grid=(rows // tr,),
        in_specs=[pl.BlockSpec((n, tr, cols), lambda i: (0, i, 0))],
        out_specs=pl.BlockSpec((tr, cols), lambda i: (i, 0)),
        out_shape=jax.ShapeDtypeStruct((rows, cols), F32),
        compiler_params=_params("parallel"),
        name=name,
    )(parts)


def _adamw_math(w, g, m, v):
    m_new = ADAM_B1 * m + (1.0 - ADAM_B1) * g
    v_new = ADAM_B2 * v + (1.0 - ADAM_B2) * jnp.square(g)
    m_hat = m_new / (1.0 - ADAM_B1 ** ADAM_STEP)
    v_hat = v_new / (1.0 - ADAM_B2 ** ADAM_STEP)
    return -ADAM_LR * (m_hat / (jnp.sqrt(v_hat) + ADAM_EPS) + ADAM_WD * w), m_new, v_new


def _adamw(w, g, m, v, name):
    rows, cols = w.shape
    tr = _tile(rows, 256, 8)

    def body(w_ref, g_ref, m_ref, v_ref, d_out, m_out, v_out):
        d_out[...], m_out[...], v_out[...] = _adamw_math(w_ref[...], g_ref[...], m_ref[...], v_ref[...])

    blk = pl.BlockSpec((tr, cols), lambda i: (i, 0))
    out = jax.ShapeDtypeStruct((rows, cols), F32)
    return pl.pallas_call(
        body,
        grid=(rows // tr,),
        in_specs=[blk] * 4,
        out_specs=[blk] * 3,
        out_shape=[out] * 3,
        compiler_params=_params("parallel"),
        name=name,
    )(w, g, m, v)


def _adamw_layer(l, w, g_a, g_b, m, v, prev, name):
    depth, rows, cols = w.shape
    tr = _tile(rows, 256, 8)

    def body(w_ref, ga_ref, gb_ref, m_ref, v_ref, *rest):
        g_out, d_out, m_out, v_out = rest[-4:]
        g = ga_ref[...] + gb_ref[...]
        g_out[...] = g
        d_out[...], m_out[...], v_out[...] = _adamw_math(w_ref[...], g, m_ref[...], v_ref[...])

    stacked = pl.BlockSpec((None, tr, cols), lambda i: (l, i, 0))
    flat = pl.BlockSpec((tr, cols), lambda i: (i, 0))
    in_specs = [stacked, flat, flat, stacked, stacked]
    args = [w, g_a, g_b, m, v]
    aliases = {}
    if prev is not None:
        in_specs += [ANY] * 4
        args += list(prev)
        aliases = {5 + k: k for k in range(4)}
    out = jax.ShapeDtypeStruct((depth, rows, cols), F32)
    return pl.pallas_call(
        body,
        grid=(rows // tr,),
        in_specs=in_specs,
        out_specs=[stacked] * 4,
        out_shape=[out] * 4,
        input_output_aliases=aliases,
        compiler_params=_params("parallel"),
        name=name,
    )(*args)


def _cols_from_shards(g):
    _, r, c = g.shape
    return jnp.transpose(g, (1, 0, 2)).reshape(r, N_CHIPS * c)


def _cols_to_shards(full):
    r, c4 = full.shape
    c = c4 // N_CHIPS
    return jnp.transpose(full.reshape(r, N_CHIPS, c), (1, 0, 2))


IN_ORDER = (Q_RANK, KV_RANK, ROPE, CONV_W, CONV_W, CONV_W, MEM_W, MIX_W)


def _w_in_to_z_layout(w_in):
    edges = [0]
    for width in IN_ORDER:
        edges.append(edges[-1] + width)
    q_lat, kv_lat, k_pe, gb, gc, xin, q_mem, gate = [w_in[..., edges[i]:edges[i + 1]] for i in range(8)]
    pad = jnp.zeros(k_pe.shape[:-1] + (LANES - ROPE,), w_in.dtype)
    return jnp.concatenate([gate, q_lat, kv_lat, k_pe, pad, gb, gc, xin, q_mem], axis=-1)


def _w_in_from_z_layout(wz):
    cut = lambda c0, width: wz[..., c0:c0 + width]
    return jnp.concatenate(
        [cut(Z_QLAT, Q_RANK), cut(Z_KVLAT, KV_RANK), cut(Z_KPE, ROPE), cut(Z_GB, CONV_W), cut(Z_GC, CONV_W),
         cut(Z_XIN, CONV_W), cut(Z_QMEM, MEM_W), cut(Z_GATE, MIX_W)], axis=-1)


def _w_uq_pad(w_uq):
    r, _ = w_uq.shape
    w = jnp.pad(w_uq.reshape(r, MLA_HEADS, QK_HEAD), ((0, 0), (0, 0), (0, QPAD - QK_HEAD)))
    return w.reshape(r, MLA_HEADS * QPAD)


def _w_uq_unpad(w):
    r, _ = w.shape
    return w.reshape(r, MLA_HEADS, QPAD)[..., :QK_HEAD].reshape(r, MLA_HEADS * QK_HEAD)


def _rope_tables(positions):
    inv_freq = 1.0 / (ROPE_THETA ** (jnp.arange(0, ROPE, 2, dtype=F32) / ROPE))
    ang = positions.astype(F32)[:, None] * inv_freq
    cos, sin = jnp.cos(ang), jnp.sin(ang)
    s = positions.shape[0]
    zero = jnp.zeros((s, HALF_ROPE), F32)
    pad = jnp.zeros((s, LANES - ROPE), F32)
    kc = jnp.concatenate([cos, cos, pad], axis=-1)
    ka = jnp.concatenate([-sin, zero, pad], axis=-1)
    kb = jnp.concatenate([zero, sin, pad], axis=-1)
    qc = jnp.concatenate([jnp.ones((s, NOPE), F32), kc], axis=-1)
    qa = jnp.concatenate([jnp.zeros((s, NOPE), F32), ka], axis=-1)
    qb = jnp.concatenate([jnp.zeros((s, NOPE), F32), kb], axis=-1)
    return (qc, qa, qb), (kc, ka, kb)


def _layer_weights(gathered):
    return (gathered[0],) + _late_weights(gathered[1:])


def _late_weights(gathered):
    g_uq, w_ukv, g_conv, w_mkv, w_o = gathered
    return (_w_uq_pad(_cols_from_shards(g_uq)), w_ukv, _cols_from_shards(g_conv), w_mkv, w_o)


def _layer_fwd(l, x, mem, wts, gains, tabs, comm, late=None):
    wt_in = wts[0]
    g_pre, g_q, g_kv, g_mem, g_post = gains
    q_tab, k_tab = tabs
    tag = f"l{l}_"
    h = _rmsnorm_fwd(x, g_pre, 0, D_MODEL, tag + "pre_norm")
    z = _matmul(h, wt_in, "nt", BF16, tag + "in_proj", tm_cap=1024, tn_cap=1664)
    w_uq, w_ukv, conv_w, w_mkv, w_o = wts[1:] if late is None else late(z)
    wts = (wt_in, w_uq, w_ukv, conv_w, w_mkv, w_o)
    qn = _rmsnorm_fwd(z, g_q, Z_QLAT, Q_RANK, tag + "q_norm")
    kvn = _rmsnorm_fwd(z, g_kv, Z_KVLAT, KV_RANK, tag + "kv_norm")
    q_raw = _matmul(qn, w_uq, "nn", F32, tag + "uq", tm_cap=1024)
    kv = _matmul(kvn, w_ukv, "nn", BF16, tag + "ukv")
    kpe = _rope(z, *k_tab, Z_KPE, LANES, 1, tag + "k_rope")
    cat, a_lse, arrived = _attn_fwd(q_raw, kv, kpe, kv, q_tab, MLA_HEADS, QPAD, 0, 0, 2, 1, 2, QK_HEAD ** -0.5, 512,
                                    tag + "mla_fwd", comm, o_into=(MIX_W, 0, None))
    cat = _conv_fwd(z, conv_w, cat, tag + "conv_fwd")
    mem_n = _rmsnorm_fwd(mem, g_mem, 0, D_MODEL, tag + "mem_norm")
    mkv = _matmul(mem_n, w_mkv, "nn", BF16, tag + "mem_kv")
    cat, m_lse, _ = _attn_fwd(z, mkv, None, mkv, None, MEM_HEADS, LANES, Z_QMEM // LANES, 0, 1, MEM_HEADS, 1,
                              MEM_HEAD ** -0.5, 1024, tag + "mem_fwd", o_into=(MIX_W, (MLA_W + CONV_W) // LANES, cat))
    y = _gate_fwd(cat, z, tag + "gate_fwd")
    o = _matmul(y, w_o, "nn", F32, tag + "out_proj", tm_cap=1024)
    x_new = _post_norm_residual(x, o, g_post, tag + "post_norm")
    saved = (x, h, z, qn, kvn, q_raw, kv, kpe, a_lse, mem_n, mkv, m_lse, cat, y, o)
    return x_new, saved, arrived


def _layer_bwd(l, g, mem, saved, wts, gains, tabs_bwd, comm, split_exchange=False):
    wt_in, w_uq, w_ukv, conv_w, w_mkv, w_o = wts
    g_pre, g_q, g_kv, g_mem, g_post = gains
    q_tab, k_tab_bwd = tabs_bwd
    x, h, z, qn, kvn, q_raw, kv, kpe, a_lse, mem_n, mkv, m_lse, cat, y, o = saved
    tag = f"l{l}_"
    do, dg_post = _rmsnorm_bwd(o, g_post, g, None, 0, D_MODEL, BF16, tag + "post_norm_bwd")
    dcat, dgate = _out_proj_dx_gate_bwd(do, w_o, cat, z, tag + "out_proj_dx")
    dw_o = _matmul(y, do, "tn", BF16, tag + "out_proj_dw", tm_cap=1024)
    dq, dkv, dkpe_h, arrived = _attn_bwd(q_raw, kv, kpe, kv, cat, dcat, a_lse, q_tab, MLA_HEADS, QPAD, 0, 0, 2, 1, 2, 0,
                                         QK_HEAD ** -0.5, 512, tag + "mla_bwd", comm)
    dkpe = _kpe_grad(dkpe_h, *k_tab_bwd, MLA_HEADS, tag + "k_rope_bwd")
    dw_ukv = _matmul(kvn, dkv, "tn", BF16, tag + "ukv_dw")
    dkvn = _matmul(dkv, w_ukv, "nt", F32, tag + "ukv_dx")
    dkv_lat, dg_kv = _rmsnorm_bwd(z, g_kv, dkvn, None, Z_KVLAT, KV_RANK, BF16, tag + "kv_norm_bwd")
    dw_uq = _matmul(qn, dq, "tn", BF16, tag + "uq_dw")
    dqn = _matmul(dq, w_uq, "nt", F32, tag + "uq_dx")
    dq_lat, dg_q = _rmsnorm_bwd(z, g_q, dqn, None, Z_QLAT, Q_RANK, BF16, tag + "q_norm_bwd")
    dgb, dgc, dxin, dconv_w = _conv_bwd(z, conv_w, dcat, tag + "conv_bwd")
    dq_mem, dmk, dmv, _ = _attn_bwd(z, mkv, None, mkv, cat, dcat, m_lse, None, MEM_HEADS, LANES, Z_QMEM // LANES, 0, 1,
                                    MEM_HEADS, 1, (MLA_W + CONV_W) // LANES, MEM_HEAD ** -0.5, 1024, tag + "mem_bwd")
    dmkv = jnp.concatenate([dmk, dmv], axis=-1)
    dw_mkv = _matmul(mem_n, dmkv, "tn", BF16, tag + "mem_kv_dw")
    dmem_n = _matmul(dmkv, w_mkv, "nt", F32, tag + "mem_kv_dx")
    _, dg_mem = _rmsnorm_bwd(mem, g_mem, dmem_n, None, 0, D_MODEL, BF16, tag + "mem_norm_bwd")
    others = (_cols_to_shards(_w_uq_unpad(dw_uq)), dw_ukv, _cols_to_shards(dconv_w), dw_mkv, dw_o)
    early = None
    if split_exchange:
        early_plan = _scatter_plan(None, *others, part="rest")
        early = (early_plan, _comm_start(early_plan, dmem_n, tag + "exchange_rest_start"))
        g_pre = g_pre + early[1][2][0:1, 0:1]
    dz = jnp.concatenate([dgate, dq_lat, dkv_lat, dkpe, dgb, dgc, dxin, dq_mem], axis=-1)
    dwt_in = _matmul(dz, h, "tn", BF16, tag + "in_proj_dw", tm_cap=1664, tk_cap=2048)
    contrib = _scatter_plan(dwt_in, *others, part="in" if split_exchange else "all")
    late = None
    if split_exchange:
        late = (contrib, _comm_start(contrib, dwt_in, tag + "exchange_in_start"))
    dh = _matmul(dz, wt_in, "nn", F32, tag + "in_proj_dx", tm_cap=1024, tk_cap=1664, after=late[1][2] if late else None)
    dx, dg_pre = _rmsnorm_bwd(x, g_pre, dh, g, 0, D_MODEL, F32, tag + "pre_norm_bwd")
    return dx, contrib, (dg_pre, dg_q, dg_kv, dg_mem, dg_post), (early, late)


GAIN_WIDTHS = (D_MODEL, Q_RANK, KV_RANK, D_MODEL, D_MODEL)


def _pack_gains(parts):
    return jnp.concatenate([p.reshape(-1) for p in parts]).reshape(-1, LANES)


def _unpack_gains(packed, depth):
    flat = packed.reshape(-1)
    out, at = [], 0
    for width in GAIN_WIDTHS:
        out.append(flat[at:at + depth * width].reshape(depth, width))
        at += depth * width
    return out


def kernel(x, mem, positions, pre_norm_g, w_in, q_norm_g, w_uq, kv_norm_g, w_ukv, conv_w, mem_norm_g, w_mk, w_mv, w_o, post_norm_g, loss_target, m_pre_norm_g, m_w_in, m_q_norm_g, m_w_uq, m_kv_norm_g, m_w_ukv, m_conv_w, m_mem_norm_g, m_w_mk, m_w_mv, m_w_o, m_post_norm_g, v_pre_norm_g, v_w_in, v_q_norm_g, v_w_uq, v_kv_norm_g, v_w_ukv, v_conv_w, v_mem_norm_g, v_w_mk, v_w_mv, v_w_o, v_post_norm_g):
    depth = w_in.shape[0]
    x0, mem0, target = x[0], mem[0], loss_target[0]
    tabs = _rope_tables(positions[0])
    tabs_bwd = (tabs[0], (tabs[1][0], -tabs[1][1], -tabs[1][2]))

    flip = lambda t: jnp.transpose(t, (0, 2, 1))
    w_in, m_w_in, v_w_in = flip(w_in), flip(m_w_in), flip(v_w_in)
    shards = [w_in.astype(BF16), w_uq.astype(BF16), w_ukv.astype(BF16), conv_w, w_mk.astype(BF16), w_mv.astype(BF16),
              w_o.astype(BF16)]
    zero_rows = lambda: jnp.zeros((LANES - ROPE, D_MODEL), BF16)

    def layer_gains(l):
        return tuple(g[l][None, :] for g in (pre_norm_g, q_norm_g, kv_norm_g, mem_norm_g, post_norm_g))

    wts, saved = [None] * depth, [None] * depth
    first = [s[0:1] for s in shards]
    plan_in, plan_rest = _gather_plan(0, first, zero_rows(), "in"), _gather_plan(0, first, zero_rows(), "rest")
    started_in = _comm_start(plan_in, positions, "l0_gather_in_start")
    started_rest = _comm_start(plan_rest, started_in[2], "l0_gather_rest_start")
    wts[0] = tuple(_comm_finish(plan_in, started_in, started_rest[2], "l0_gather_in_wait"))

    next_gather = {}

    def start_next_gather(l, after):
        plan = _gather_plan(0, [s[l + 1:l + 2] for s in shards], zero_rows())
        next_gather[l + 1] = (plan, _comm_start(plan, after, f"l{l + 1}_gather_start"))
        return next_gather[l + 1][1][2][0:1, 0:1]

    def rest_of_layer0(z):
        got = _late_weights(_comm_finish(plan_rest, started_rest, z, "l0_gather_rest_wait"))
        wts[0] = wts[0] + got
        if depth > 1:
            got = (got[0] + start_next_gather(0, got[4]).astype(BF16),) + got[1:]
        return got

    act = x0
    for l in range(depth):
        gains = layer_gains(l)
        if 0 < l < depth - 1:
            gains = (gains[0] + start_next_gather(l, wts[l][5]),) + gains[1:]
        act, saved[l], _ = _layer_fwd(l, act, mem0, wts[l], gains, tabs, None, rest_of_layer0 if l == 0 else None)
        if l + 1 < depth:
            plan, started = next_gather[l + 1]
            wts[l + 1] = _layer_weights(_comm_finish(plan, started, act, f"l{l + 1}_gather_wait"))
    grad, loss_part = _loss_head(act, target, "loss_head")
    loss = lax.psum(loss_part[0, 0], ("x", "y", "c"))

    names = ("w_in", "w_uq", "w_ukv", "conv_w", "w_mk", "w_mv", "w_o")
    w_shards = (w_in, w_uq, w_ukv, conv_w, w_mk, w_mv, w_o)
    m_shards = (m_w_in, m_w_uq, m_w_ukv, m_conv_w, m_w_mk, m_w_mv, m_w_o)
    v_shards = (v_w_in, v_w_uq, v_w_ukv, v_conv_w, v_w_mk, v_w_mv, v_w_o)
    stacked = [None] * len(names)

    def sum_and_send(l, received):
        partial = [_sum_slots(r, f"l{l}_grad_sum_{names[i]}") for i, r in enumerate(received)]
        plan = _sibling_plan(partial)
        return l, partial, plan, _comm_start(plan, partial[0], f"l{l}_sibling_start")

    def receive_and_update(state, after):
        l, partial, plan, started = state
        other = _comm_finish(plan, started, after, f"l{l}_sibling_wait")
        for i, name in enumerate(names):
            stacked[i] = _adamw_layer(l, w_shards[i], partial[i], other[i], m_shards[i], v_shards[i], stacked[i],
                                      f"l{l}_adamw_{name}")

    dgs = [None] * depth
    pending = None
    in_flight = None
    for l in reversed(range(depth)):
        gains = layer_gains(l)
        for token in ([pending[1][2]] if pending else []) + ([in_flight[3][2]] if in_flight else []):
            gains = gains[:4] + (gains[4] + token[0:1, 0:1],)
        grad, contrib, dgs[l], early = _layer_bwd(l, grad, mem0, saved[l], wts[l], gains, tabs_bwd, None, l == 0)
        if in_flight is not None:
            receive_and_update(in_flight, grad)
            in_flight = None
        if pending is not None:
            in_flight = sum_and_send(l + 1, _comm_finish(pending[0], pending[1], grad, f"l{l + 1}_exchange_wait"))
        if l > 0:
            pending = (contrib, _comm_start(contrib, grad, f"l{l}_exchange_start"))
    early, late = early
    got_in = _comm_finish(late[0], late[1], grad, "l0_exchange_in_wait")
    last = sum_and_send(0, got_in + _comm_finish(early[0], early[1], got_in[0], "l0_exchange_rest_wait"))
    if in_flight is not None:
        receive_and_update(in_flight, last[1][0])
    receive_and_update(last, stacked[0][0] if depth > 1 else last[1][0])
    grad_x = grad[None]
    results = {name: tuple(stacked[i]) for i, name in enumerate(names)}
    results["w_in"] = tuple(flip(t) for t in results["w_in"])

    gain_names = ("pre_norm_g", "q_norm_g", "kv_norm_g", "mem_norm_g", "post_norm_g")
    dg_packed = _pack_gains([jnp.concatenate([dgs[l][i] for l in range(depth)], axis=0) for i in range(5)])
    dg_total = _sum_slots(_gather_all(dg_packed, "gain_gather"), "gain_sum")
    gain_outs = (dg_total,) + tuple(_adamw(
        _pack_gains((pre_norm_g, q_norm_g, kv_norm_g, mem_norm_g, post_norm_g)), dg_total,
        _pack_gains((m_pre_norm_g, m_q_norm_g, m_kv_norm_g, m_mem_norm_g, m_post_norm_g)),
        _pack_gains((v_pre_norm_g, v_q_norm_g, v_kv_norm_g, v_mem_norm_g, v_post_norm_g)), "adamw_gains"))
    gain_outs = [_unpack_gains(t, depth) for t in gain_outs]
    for i, name in enumerate(gain_names):
        results[name] = tuple(gain_outs[k][i] for k in range(4))

    order = ("pre_norm_g", "w_in", "q_norm_g", "w_uq", "kv_norm_g", "w_ukv", "conv_w", "mem_norm_g", "w_mk", "w_mv", "w_o",
             "post_norm_g")
    out = [loss, grad_x]
    for k in range(4):
        out += [results[name][k] for name in order]
    return tuple(out)
```

```python
import functools

import jax
import jax.numpy as jnp
from jax import lax
from jax.experimental import pallas as pl
from jax.experimental.pallas import tpu as pltpu

F32 = jnp.float32
BF16 = jnp.bfloat16
MESH_ID = pl.DeviceIdType.MESH

D_MODEL = 2048
EPS = 1e-6
LOG2_E = 1.4426950408889634
ROPE_THETA = 10000.0
MLA_HEADS = 8
NOPE = 128
ROPE = 64
HALF_ROPE = ROPE // 2
QK_HEAD = NOPE + ROPE
V_HEAD = 128
Q_RANK = 512
KV_RANK = 256
CONV_W = 512
MEM_HEADS = 4
MEM_HEAD = 128
MEM_W = MEM_HEADS * MEM_HEAD
MLA_W = MLA_HEADS * V_HEAD
MIX_W = MLA_W + CONV_W + MEM_W
IN_COLS = Q_RANK + KV_RANK + ROPE + 3 * CONV_W + MEM_W + MIX_W
N_CHIPS = 4
N_DEV = 8

LANES = 128
VMEM_LIMIT_BYTES = 56 * 1024 * 1024

QPAD = 2 * LANES
Z_GATE = 0
Z_QLAT = Z_GATE + MIX_W
Z_KVLAT = Z_QLAT + Q_RANK
Z_KPE = Z_KVLAT + KV_RANK
Z_GB = Z_KPE + LANES
Z_GC = Z_GB + CONV_W
Z_XIN = Z_GC + CONV_W
Z_QMEM = Z_XIN + CONV_W
Z_COLS = Z_QMEM + MEM_W

ADAM_LR = 0.001
ADAM_B1 = 0.9
ADAM_B2 = 0.999
ADAM_EPS = 1e-08
ADAM_WD = 0.01
ADAM_STEP = 10


def _tile(dim, cap, unit):
    if dim <= cap:
        return dim
    t = (cap // unit) * unit
    while t >= unit:
        if dim % t == 0:
            return t
        t -= unit
    raise ValueError(f"no tile of {dim} under {cap} in units of {unit}")


def _params(*semantics):
    return pltpu.CompilerParams(dimension_semantics=semantics, vmem_limit_bytes=VMEM_LIMIT_BYTES)


def _matmul(a, b, mode, out_dtype, name, tm_cap=512, tn_cap=1024, tk_cap=2048, after=None):
    if mode == "nn":
        (m, k), (k2, n) = a.shape, b.shape
    elif mode == "nt":
        (m, k), (n, k2) = a.shape, b.shape
    else:
        (k, m), (k2, n) = a.shape, b.shape
    assert k == k2, (a.shape, b.shape, mode)
    tm = _tile(m, tm_cap, LANES if mode == "tn" else 16)
    tn = _tile(n, tn_cap, LANES)
    tk = _tile(k, tk_cap, LANES if mode != "tn" else 16)
    nk = k // tk
    if mode == "nn":
        a_spec = pl.BlockSpec((tm, tk), lambda i, j, kk: (i, kk))
        b_spec = pl.BlockSpec((tk, tn), lambda i, j, kk: (kk, j))
        dims = (((1,), (0,)), ((), ()))
    elif mode == "nt":
        a_spec = pl.BlockSpec((tm, tk), lambda i, j, kk: (i, kk))
        b_spec = pl.BlockSpec((tn, tk), lambda i, j, kk: (j, kk))
        dims = (((1,), (1,)), ((), ()))
    else:
        a_spec = pl.BlockSpec((tk, tm), lambda i, j, kk: (kk, i))
        b_spec = pl.BlockSpec((tk, tn), lambda i, j, kk: (kk, j))
        dims = (((0,), (0,)), ((), ()))

    def body(a_ref, b_ref, *rest):
        o_ref, scratch = (rest[1], rest[2:]) if after is not None else (rest[0], rest[1:])
        part = lax.dot_general(a_ref[...].astype(BF16), b_ref[...].astype(BF16), dims, preferred_element_type=F32)
        if nk == 1:
            o_ref[...] = part.astype(o_ref.dtype)
            return
        (acc_ref,) = scratch
        kk = pl.program_id(2)

        @pl.when(kk == 0)
        def _():
            acc_ref[...] = part

        @pl.when(kk > 0)
        def _():
            acc_ref[...] += part

        @pl.when(kk == nk - 1)
        def _():
            o_ref[...] = acc_ref[...].astype(o_ref.dtype)

    return pl.pallas_call(
        body,
        grid=(m // tm, n // tn, nk),
        in_specs=[a_spec, b_spec] + ([] if after is None else [pl.BlockSpec(memory_space=pl.ANY)]),
        out_specs=pl.BlockSpec((tm, tn), lambda i, j, kk: (i, j)),
        out_shape=jax.ShapeDtypeStruct((m, n), out_dtype),
        scratch_shapes=[] if nk == 1 else [pltpu.VMEM((tm, tn), F32)],
        compiler_params=_params("parallel", "parallel", "arbitrary"),
        name=name,
    )(*([a, b] if after is None else [a, b, after]))


def _rmsnorm_fwd(x, gain, col0, width, name):
    rows = x.shape[0]
    tr = _tile(rows, 512, 16)
    cb = col0 // width
    assert cb * width == col0

    def body(x_ref, g_ref, o_ref):
        xv = x_ref[...].astype(F32)
        r = lax.rsqrt(jnp.mean(xv * xv, axis=-1, keepdims=True) + EPS)
        o_ref[...] = (xv * r * g_ref[...]).astype(o_ref.dtype)

    return pl.pallas_call(
        body,
        grid=(rows // tr,),
        in_specs=[pl.BlockSpec((tr, width), lambda i: (i, cb)), pl.BlockSpec((1, width), lambda i: (0, 0))],
        out_specs=pl.BlockSpec((tr, width), lambda i: (i, 0)),
        out_shape=jax.ShapeDtypeStruct((rows, width), BF16),
        compiler_params=_params("parallel"),
        name=name,
    )(x, gain)


def _rmsnorm_bwd(x, gain, dy, resid, col0, width, out_dtype, name):
    rows = x.shape[0]
    tr = _tile(rows, 256, 16)
    cb = col0 // width
    assert cb * width == col0
    has_resid = resid is not None

    def body(*refs):
        if has_resid:
            x_ref, g_ref, dy_ref, res_ref, dx_ref, dg_ref = refs
        else:
            x_ref, g_ref, dy_ref, dx_ref, dg_ref = refs
        i = pl.program_id(0)
        xv = x_ref[...].astype(F32)
        dyv = dy_ref[...].astype(F32)
        r = lax.rsqrt(jnp.mean(xv * xv, axis=-1, keepdims=True) + EPS)
        xr = xv * r
        dyg = dyv * g_ref[...]
        c = jnp.mean(dyg * xr, axis=-1, keepdims=True)
        dx = r * (dyg - xr * c)
        if has_resid:
            dx = dx + res_ref[...]
        dx_ref[...] = dx.astype(dx_ref.dtype)
        part = jnp.sum(dyv * xr, axis=0, keepdims=True)

        @pl.when(i == 0)
        def _():
            dg_ref[...] = part

        @pl.when(i > 0)
        def _():
            dg_ref[...] += part

    row_spec = pl.BlockSpec((tr, width), lambda i: (i, 0))
    in_specs = [pl.BlockSpec((tr, width), lambda i: (i, cb)), pl.BlockSpec((1, width), lambda i: (0, 0)), row_spec]
    args = [x, gain, dy]
    if has_resid:
        in_specs.append(row_spec)
        args.append(resid)
    return pl.pallas_call(
        body,
        grid=(rows // tr,),
        in_specs=in_specs,
        out_specs=[row_spec, pl.BlockSpec((1, width), lambda i: (0, 0))],
        out_shape=[jax.ShapeDtypeStruct((rows, width), out_dtype), jax.ShapeDtypeStruct((1, width), F32)],
        compiler_params=_params("arbitrary"),
        name=name,
    )(*args)


def _post_norm_residual(x, o, gain, name):
    rows, width = x.shape
    tr = _tile(rows, 256, 8)

    def body(x_ref, o_ref, g_ref, out_ref):
        ov = o_ref[...]
        r = lax.rsqrt(jnp.mean(ov * ov, axis=-1, keepdims=True) + EPS)
        out_ref[...] = x_ref[...] + ov * r * g_ref[...]

    row_spec = pl.BlockSpec((tr, width), lambda i: (i, 0))
    return pl.pallas_call(
        body,
        grid=(rows // tr,),
        in_specs=[row_spec, row_spec, pl.BlockSpec((1, width), lambda i: (0, 0))],
        out_specs=row_spec,
        out_shape=jax.ShapeDtypeStruct((rows, width), F32),
        compiler_params=_params("parallel"),
        name=name,
    )(x, o, gain)


def _rope(x, tab_c, tab_a, tab_b, col0, width, heads, name):
    rows = x.shape[0]
    tr = _tile(rows, 512, 16)
    cb = col0 // width
    assert cb * width == col0

    def body(x_ref, c_ref, a_ref, b_ref, o_ref):
        xv = x_ref[...].astype(F32)
        up = pltpu.roll(xv, width - HALF_ROPE, 1)
        down = pltpu.roll(xv, HALF_ROPE, 1)
        o_ref[...] = (xv * c_ref[...] + up * a_ref[...] + down * b_ref[...]).astype(o_ref.dtype)

    tab_spec = pl.BlockSpec((tr, width), lambda i, h: (i, 0))
    return pl.pallas_call(
        body,
        grid=(rows // tr, heads),
        in_specs=[pl.BlockSpec((tr, width), lambda i, h: (i, cb + h)), tab_spec, tab_spec, tab_spec],
        out_specs=pl.BlockSpec((tr, width), lambda i, h: (i, h)),
        out_shape=jax.ShapeDtypeStruct((rows, heads * width), BF16),
        compiler_params=_params("parallel", "parallel"),
        name=name,
    )(x, tab_c, tab_a, tab_b)


def _kpe_grad(dkb, tab_c, tab_a, tab_b, heads, name):
    rows = dkb.shape[0]
    tr = _tile(rows, 512, 16)

    def body(d_ref, c_ref, a_ref, b_ref, o_ref):
        acc = d_ref[:, 0:LANES]
        for h in range(1, heads):
            acc = acc + d_ref[:, h * LANES:(h + 1) * LANES]
        up = pltpu.roll(acc, LANES - HALF_ROPE, 1)
        down = pltpu.roll(acc, HALF_ROPE, 1)
        o_ref[...] = (acc * c_ref[...] + up * a_ref[...] + down * b_ref[...]).astype(o_ref.dtype)

    tab_spec = pl.BlockSpec((tr, LANES), lambda i: (i, 0))
    return pl.pallas_call(
        body,
        grid=(rows // tr,),
        in_specs=[pl.BlockSpec((tr, heads * LANES), lambda i: (i, 0)), tab_spec, tab_spec, tab_spec],
        out_specs=tab_spec,
        out_shape=jax.ShapeDtypeStruct((rows, LANES), BF16),
        compiler_params=_params("parallel"),
        name=name,
    )(dkb, tab_c, tab_a, tab_b)


class _CommPlan:
    def __init__(self, ins, out_shape, build, n_copies):
        self.ins, self.out_shape, self.build, self.n_copies = list(ins), list(out_shape), build, n_copies

    def scratch(self):
        n = self.n_copies
        return [pltpu.SemaphoreType.DMA((n,)), pltpu.SemaphoreType.DMA((n,)), pltpu.SemaphoreType.DMA((n,))]


def _split_comm(refs, n_in, n_out, comm):
    if comm is None:
        return refs, None
    ci, co = len(comm.ins), len(comm.out_shape)
    ins, c_ins = refs[:n_in], refs[n_in:n_in + ci]
    outs, c_outs = refs[n_in + ci:n_in + ci + n_out], refs[n_in + ci + n_out:n_in + ci + n_out + co]
    rest = refs[n_in + ci + n_out + co:]
    scratch, sems = rest[:-3], rest[-3:]
    return tuple(ins) + tuple(outs) + tuple(scratch), functools.partial(comm.build, c_ins, c_outs, sems)


def _ride_start(copies, first):
    if copies is not None:
        pl.when(first)(copies()[0])


def _ride_wait(copies, last):
    if copies is not None:
        pl.when(last)(copies()[1])


def _rope_rows(x, c, a, b, sign):
    width = x.shape[-1]
    mixed = pltpu.roll(x, width - HALF_ROPE, 1) * a + pltpu.roll(x, HALF_ROPE, 1) * b
    return x * c + mixed if sign > 0 else x * c - mixed


def _attn_fwd(q, ka, kb, v, rope, heads, q_w, q_cb, ka_cb, ka_step, v_cb, v_step, scale, tq_cap, name, comm=None, tk_cap=512,
              o_into=None):
    s_q, s_k = q.shape[0], ka.shape[0]
    tq = _tile(s_q, tq_cap, 16)
    nq = s_q // tq
    has_kb = kb is not None
    n_in = 7 if has_kb else 3
    tk = _tile(s_k, tk_cap, LANES)
    o_cols, o_cb, o_old = o_into if o_into is not None else (heads * LANES, 0, None)
    assert comm is None or o_old is None

    def body(*refs):
        if o_old is not None:
            refs = refs[:n_in] + refs[n_in + 1:]
        refs, copies = _split_comm(refs, n_in, 2, comm)
        first = jnp.logical_and(pl.program_id(0) == 0, pl.program_id(1) == 0)
        last = jnp.logical_and(pl.program_id(0) == heads - 1, pl.program_id(1) == nq - 1)
        _ride_start(copies, first)
        if has_kb:
            q_ref, ka_ref, kb_ref, v_ref, c_ref, a_ref, b_ref, o_ref, lse_ref, k_scr = refs

            @pl.when(pl.program_id(1) == 0)
            def _():
                k_scr[:, 0:LANES] = ka_ref[...].astype(BF16)
                k_scr[:, LANES:2 * LANES] = kb_ref[...].astype(BF16)

            keys = k_scr
            qv = _rope_rows(q_ref[...], c_ref[...], a_ref[...], b_ref[...], 1).astype(BF16)
        else:
            q_ref, ka_ref, v_ref, o_ref, lse_ref = refs
            keys = ka_ref
            qv = q_ref[...].astype(BF16)
        c2 = scale * LOG2_E
        m = l = o = None
        nk = s_k // tk
        scores = lambda j: lax.dot_general(qv, keys[j * tk:(j + 1) * tk, :].astype(BF16), (((1,), (1,)), ((), ())),
                                           preferred_element_type=F32)
        s_next = scores(0)
        for j in range(nk):
            sj = s_next
            if j + 1 < nk:
                s_next = scores(j + 1)
            mj = jnp.max(sj, axis=-1, keepdims=True)
            m_new = mj if m is None else jnp.maximum(m, mj)
            pj = jnp.exp2((sj - m_new) * c2)
            lj = jnp.sum(pj, axis=-1, keepdims=True)
            oj = jnp.dot(pj.astype(BF16), v_ref[j * tk:(j + 1) * tk, :].astype(BF16), preferred_element_type=F32)
            if m is None:
                l, o = lj, oj
            else:
                alpha = jnp.exp2((m - m_new) * c2)
                l, o = l * alpha + lj, o * alpha + oj
            m = m_new
        o_ref[...] = (o * (1.0 / l)).astype(o_ref.dtype)
        lse_ref[...] = jnp.broadcast_to(m * c2 + jnp.log2(l), lse_ref.shape)
        _ride_wait(copies, last)

    in_specs = [pl.BlockSpec((tq, q_w), lambda h, i: (i, q_cb + h)),
                pl.BlockSpec((s_k, LANES), lambda h, i: (0, ka_cb + ka_step * h))]
    args = [q, ka]
    if has_kb:
        in_specs.append(pl.BlockSpec((s_k, LANES), lambda h, i: (0, 0)))
        args.append(kb)
    in_specs.append(pl.BlockSpec((s_k, LANES), lambda h, i: (0, v_cb + v_step * h)))
    args.append(v)
    if has_kb:
        in_specs += [pl.BlockSpec((tq, q_w), lambda h, i: (i, 0))] * 3
        args += list(rope)
    aliases = {}
    if o_old is not None:
        aliases = {len(args): 0}
        in_specs.append(ANY)
        args.append(o_old)
    out_specs = [pl.BlockSpec((tq, LANES), lambda h, i: (i, o_cb + h)), pl.BlockSpec((tq, LANES), lambda h, i: (i, h))]
    out_shape = [jax.ShapeDtypeStruct((s_q, o_cols), BF16), jax.ShapeDtypeStruct((s_q, heads * LANES), F32)]
    scratch = [pltpu.VMEM((s_k, 2 * LANES), BF16)] if has_kb else []
    if comm is not None:
        in_specs += [ANY] * len(comm.ins)
        args += comm.ins
        out_specs += [ANY] * len(comm.out_shape)
        out_shape += comm.out_shape
        scratch += comm.scratch()
    res = pl.pallas_call(
        body,
        grid=(heads, nq),
        in_specs=in_specs,
        out_specs=out_specs,
        out_shape=out_shape,
        scratch_shapes=scratch,
        input_output_aliases=aliases,
        compiler_params=_params("arbitrary", "arbitrary"),
        name=name,
    )(*args)
    return res[0], res[1], list(res[2:])


def _attn_bwd(q, ka, kb, v, o, do, lse, rope, heads, q_w, q_cb, ka_cb, ka_step, v_cb, v_step, o_cb, scale, tq_cap, name,
              comm=None, tk_cap=512):
    s_q, s_k = q.shape[0], ka.shape[0]
    tq = _tile(s_q, tq_cap, 16)
    nq = s_q // tq
    has_kb = kb is not None
    n_in = 10 if has_kb else 6
    n_out = 3
    tk = _tile(s_k, tk_cap, LANES)

    def body(*refs):
        refs, copies = _split_comm(refs, n_in, n_out, comm)
        first = jnp.logical_and(pl.program_id(0) == 0, pl.program_id(1) == 0)
        last = jnp.logical_and(pl.program_id(0) == heads - 1, pl.program_id(1) == nq - 1)
        _ride_start(copies, first)
        if has_kb:
            (q_ref, ka_ref, kb_ref, v_ref, o_ref, do_ref, lse_ref, c_ref, a_ref, b_ref, dq_ref, dkv_ref, dkb_ref, k_scr, dk_acc,
             dv_acc) = refs
        else:
            q_ref, ka_ref, v_ref, o_ref, do_ref, lse_ref, dq_ref, dka_ref, dv_ref, dk_acc, dv_acc = refs
        i = pl.program_id(1)

        @pl.when(i == 0)
        def _():
            dk_acc[...] = jnp.zeros_like(dk_acc)
            dv_acc[...] = jnp.zeros_like(dv_acc)
            if has_kb:
                k_scr[:, 0:LANES] = ka_ref[...].astype(BF16)
                k_scr[:, LANES:2 * LANES] = kb_ref[...].astype(BF16)

        keys = k_scr if has_kb else ka_ref
        if has_kb:
            qv = _rope_rows(q_ref[...], c_ref[...], a_ref[...], b_ref[...], 1).astype(BF16)
        else:
            qv = q_ref[...].astype(BF16)
        dov = do_ref[...].astype(BF16)
        delta = jnp.sum(dov.astype(F32) * o_ref[...].astype(F32), axis=-1, keepdims=True)
        lse2 = lse_ref[:, 0:1]
        c2 = scale * LOG2_E
        nk = s_k // tk
        rows = lambda j: slice(j * tk, (j + 1) * tk)
        nt = (((1,), (1,)), ((), ()))
        tn = (((0,), (0,)), ((), ()))

        def scores(j):
            return (lax.dot_general(qv, keys[rows(j), :].astype(BF16), nt, preferred_element_type=F32),
                    lax.dot_general(dov, v_ref[rows(j), :].astype(BF16), nt, preferred_element_type=F32))

        nxt = scores(0)
        dq = None
        for j in range(nk):
            sj, dpj = nxt
            if j + 1 < nk:
                nxt = scores(j + 1)
            pj = jnp.exp2(sj * c2 - lse2)
            dsj = (pj * (dpj - delta)).astype(BF16)
            dqj = jnp.dot(dsj, keys[rows(j), :].astype(BF16), preferred_element_type=F32)
            dq = dqj if dq is None else dq + dqj
            dk_acc[rows(j), :] += lax.dot_general(dsj, qv, tn, preferred_element_type=F32)
            dv_acc[rows(j), :] += lax.dot_general(pj.astype(BF16), dov, tn, preferred_element_type=F32)
        dq = dq * scale
        if has_kb:
            dq = _rope_rows(dq, c_ref[...], a_ref[...], b_ref[...], -1)
        dq_ref[...] = dq.astype(dq_ref.dtype)

        @pl.when(i == nq - 1)
        def _():
            if has_kb:
                dkv_ref[:, 0:LANES] = (dk_acc[:, 0:LANES] * scale).astype(dkv_ref.dtype)
                dkv_ref[:, LANES:2 * LANES] = dv_acc[...].astype(dkv_ref.dtype)
                dkb_ref[...] = dk_acc[:, LANES:2 * LANES] * scale
            else:
                dka_ref[...] = (dk_acc[...] * scale).astype(dka_ref.dtype)
                dv_ref[...] = dv_acc[...].astype(dv_ref.dtype)

        _ride_wait(copies, last)

    key_spec = lambda cb, step: pl.BlockSpec((s_k, LANES), lambda h, i: (0, cb + step * h))
    row_spec = lambda cb: pl.BlockSpec((tq, LANES), lambda h, i: (i, cb + h))
    in_specs = [pl.BlockSpec((tq, q_w), lambda h, i: (i, q_cb + h)), key_spec(ka_cb, ka_step)]
    args = [q, ka]
    if has_kb:
        in_specs.append(pl.BlockSpec((s_k, LANES), lambda h, i: (0, 0)))
        args.append(kb)
    in_specs += [key_spec(v_cb, v_step), row_spec(o_cb), row_spec(o_cb), row_spec(0)]
    args += [v, o, do, lse]
    if has_kb:
        in_specs += [pl.BlockSpec((tq, q_w), lambda h, i: (i, 0))] * 3
        args += list(rope)
    out_specs = [pl.BlockSpec((tq, q_w), lambda h, i: (i, h))]
    out_shape = [jax.ShapeDtypeStruct((s_q, heads * q_w), BF16)]
    scratch = []
    if has_kb:
        out_specs += [pl.BlockSpec((s_k, 2 * LANES), lambda h, i: (0, h)), key_spec(0, 1)]
        out_shape += [jax.ShapeDtypeStruct((s_k, heads * 2 * LANES), BF16), jax.ShapeDtypeStruct((s_k, heads * LANES), F32)]
        scratch.append(pltpu.VMEM((s_k, 2 * LANES), BF16))
    else:
        out_specs += [key_spec(0, 1), key_spec(0, 1)]
        out_shape += [jax.ShapeDtypeStruct((s_k, heads * LANES), BF16)] * 2
    scratch += [pltpu.VMEM((s_k, q_w), F32), pltpu.VMEM((s_k, LANES), F32)]
    if comm is not None:
        in_specs += [ANY] * len(comm.ins)
        args += comm.ins
        out_specs += [ANY] * len(comm.out_shape)
        out_shape += comm.out_shape
        scratch += comm.scratch()
    res = pl.pallas_call(
        body,
        grid=(heads, nq),
        in_specs=in_specs,
        out_specs=out_specs,
        out_shape=out_shape,
        scratch_shapes=scratch,
        compiler_params=_params("arbitrary", "arbitrary"),
        name=name,
    )(*args)
    return res[0], res[1], res[2], list(res[3:])


def _shift_rows(u, rows):
    t = lax.broadcasted_iota(jnp.int32, u.shape, 0)
    prev = jnp.where(t == 0, 0.0, pltpu.roll(u, 1, 0))
    nxt = jnp.where(t == rows - 1, 0.0, pltpu.roll(u, rows - 1, 0))
    return prev, nxt


def _conv_fwd(z, conv_w, cat, name):
    rows = z.shape[0]
    nblk = CONV_W // LANES

    def body(gb_ref, gc_ref, xin_ref, w_ref, cat_ref, o_ref):
        del cat_ref
        u = gc_ref[...].astype(F32) * xin_ref[...].astype(F32)
        prev, nxt = _shift_rows(u, rows)
        conv = prev * w_ref[0:1, :] + u * w_ref[1:2, :] + nxt * w_ref[2:3, :]
        o_ref[...] = (gb_ref[...].astype(F32) * conv).astype(o_ref.dtype)

    col = lambda c0: pl.BlockSpec((rows, LANES), lambda j: (0, c0 // LANES + j))
    return pl.pallas_call(
        body,
        grid=(nblk,),
        in_specs=[col(Z_GB), col(Z_GC), col(Z_XIN), pl.BlockSpec((3, LANES), lambda j: (0, j)), ANY],
        out_specs=col(MLA_W),
        out_shape=jax.ShapeDtypeStruct(cat.shape, cat.dtype),
        input_output_aliases={4: 0},
        compiler_params=_params("parallel"),
        name=name,
    )(z, z, z, conv_w, cat)


def _conv_bwd(z, conv_w, dcat, name):
    rows = z.shape[0]
    nblk = CONV_W // LANES

    def body(gb_ref, gc_ref, xin_ref, w_ref, dc_ref, dgb_ref, dgc_ref, dxin_ref, dw_ref):
        gc = gc_ref[...].astype(F32)
        xin = xin_ref[...].astype(F32)
        dc = dc_ref[...].astype(F32)
        u = gc * xin
        prev, nxt = _shift_rows(u, rows)
        w0, w1, w2 = w_ref[0:1, :], w_ref[1:2, :], w_ref[2:3, :]
        conv = prev * w0 + u * w1 + nxt * w2
        dgb_ref[...] = (dc * conv).astype(dgb_ref.dtype)
        dconv = dc * gb_ref[...].astype(F32)
        dw_ref[0:1, :] = jnp.sum(dconv * prev, axis=0, keepdims=True)
        dw_ref[1:2, :] = jnp.sum(dconv * u, axis=0, keepdims=True)
        dw_ref[2:3, :] = jnp.sum(dconv * nxt, axis=0, keepdims=True)
        dprev, dnxt = _shift_rows(dconv, rows)
        du = dnxt * w0 + dconv * w1 + dprev * w2
        dgc_ref[...] = (du * xin).astype(dgc_ref.dtype)
        dxin_ref[...] = (du * gc).astype(dxin_ref.dtype)

    col = lambda c0: pl.BlockSpec((rows, LANES), lambda j: (0, c0 // LANES + j))
    w_spec = pl.BlockSpec((3, LANES), lambda j: (0, j))
    piece = jax.ShapeDtypeStruct((rows, CONV_W), BF16)
    return pl.pallas_call(
        body,
        grid=(nblk,),
        in_specs=[col(Z_GB), col(Z_GC), col(Z_XIN), w_spec, col(MLA_W)],
        out_specs=[col(0), col(0), col(0), w_spec],
        out_shape=[piece, piece, piece, jax.ShapeDtypeStruct((3, CONV_W), F32)],
        compiler_params=_params("parallel"),
        name=name,
    )(z, z, z, conv_w, dcat)


def _gate_fwd(cat, z, name):
    rows = cat.shape[0]
    tr = _tile(rows, 256, 16)
    tc = MIX_W
    g0 = Z_GATE // tc

    def body(c_ref, g_ref, y_ref):
        g = g_ref[...].astype(F32)
        y_ref[...] = (c_ref[...].astype(F32) * (g * jax.nn.sigmoid(g))).astype(y_ref.dtype)

    blk = pl.BlockSpec((tr, tc), lambda i, j: (i, j))
    return pl.pallas_call(
        body,
        grid=(rows // tr, MIX_W // tc),
        in_specs=[blk, pl.BlockSpec((tr, tc), lambda i, j: (i, g0 + j))],
        out_specs=blk,
        out_shape=jax.ShapeDtypeStruct((rows, MIX_W), BF16),
        compiler_params=_params("parallel", "parallel"),
        name=name,
    )(cat, z)


def _out_proj_dx_gate_bwd(do, w_o, cat, z, name):
    rows, k = do.shape
    tm = _tile(rows, 512, 16)
    tn = _tile(MIX_W, 1024, LANES)
    g0 = Z_GATE // tn

    def body(do_ref, w_ref, c_ref, g_ref, dcat_ref, dgate_ref):
        dy = lax.dot_general(do_ref[...], w_ref[...], (((1,), (1,)), ((), ())), preferred_element_type=F32)
        g = g_ref[...].astype(F32)
        sg = jax.nn.sigmoid(g)
        dcat_ref[...] = (dy * (g * sg)).astype(dcat_ref.dtype)
        dgate_ref[...] = (dy * c_ref[...].astype(F32) * (sg * (1.0 + g * (1.0 - sg)))).astype(dgate_ref.dtype)

    blk = pl.BlockSpec((tm, tn), lambda i, j: (i, j))
    out = jax.ShapeDtypeStruct((rows, MIX_W), BF16)
    return pl.pallas_call(
        body,
        grid=(rows // tm, MIX_W // tn),
        in_specs=[pl.BlockSpec((tm, k), lambda i, j: (i, 0)), pl.BlockSpec((tn, k), lambda i, j: (j, 0)), blk,
                  pl.BlockSpec((tm, tn), lambda i, j: (i, g0 + j))],
        out_specs=[blk, blk],
        out_shape=[out, out],
        compiler_params=_params("parallel", "parallel"),
        name=name,
    )(do, w_o, cat, z)


def _loss_head(y, target, name):
    rows, width = y.shape
    tr = _tile(rows, 256, 8)

    def body(y_ref, t_ref, g_ref, loss_ref):
        i = pl.program_id(0)
        d = y_ref[...] - t_ref[...]
        g_ref[...] = d / width
        part = 0.5 * jnp.sum(jnp.mean(d * d, axis=-1, keepdims=True), axis=0, keepdims=True)
        part = jnp.broadcast_to(part, loss_ref.shape)

        @pl.when(i == 0)
        def _():
            loss_ref[...] = part

        @pl.when(i > 0)
        def _():
            loss_ref[...] += part

    row_spec = pl.BlockSpec((tr, width), lambda i: (i, 0))
    return pl.pallas_call(
        body,
        grid=(rows // tr,),
        in_specs=[row_spec, row_spec],
        out_specs=[row_spec, pl.BlockSpec((1, LANES), lambda i: (0, 0))],
        out_shape=[jax.ShapeDtypeStruct((rows, width), F32), jax.ShapeDtypeStruct((1, LANES), F32)],
        compiler_params=_params("arbitrary"),
        name=name,
    )(y, target)


CHIP_FLIPS = ((1, 0), (0, 1), (1, 1))
ANY = pl.BlockSpec(memory_space=pl.ANY)


def _chip_copies(pieces, sems, n_slot):
    send_sems, recv_sems, local_sems = sems
    x, y, c = lax.axis_index("x"), lax.axis_index("y"), lax.axis_index("c")
    me = 2 * x + y

    def remote(j, k, a, src, dst):
        fx, fy = CHIP_FLIPS[k]
        return pltpu.make_async_remote_copy(
            src_ref=src, dst_ref=dst, send_sem=send_sems.at[n_slot * k + a], recv_sem=recv_sems.at[n_slot * k + a],
            device_id=((j // 2) ^ fx, (j % 2) ^ fy, c), device_id_type=MESH_ID)

    def peer(j, k):
        fx, fy = CHIP_FLIPS[k]
        return 2 * ((j // 2) ^ fx) + ((j % 2) ^ fy)

    def start_as(j):
        def run():
            for a, (src, dst) in enumerate(pieces(j, j)):
                pltpu.make_async_copy(src, dst, local_sems.at[a]).start()
            for k in range(len(CHIP_FLIPS)):
                for a, (src, dst) in enumerate(pieces(j, peer(j, k))):
                    remote(j, k, a, src, dst).start()
        return run

    def wait_as(j):
        def run():
            for a, (src, dst) in enumerate(pieces(j, j)):
                pltpu.make_async_copy(src, dst, local_sems.at[a]).wait()
            for k in range(len(CHIP_FLIPS)):
                for a, (src, dst) in enumerate(pieces(j, peer(j, k))):
                    remote(j, k, a, src, dst).wait_send()
                for a, (src, dst) in enumerate(pieces(peer(j, k), j)):
                    remote(j, k, a, src, dst).wait_recv()
        return run

    def start():
        for j in range(N_CHIPS):
            pl.when(me == j)(start_as(j))

    def wait():
        for j in range(N_CHIPS):
            pl.when(me == j)(wait_as(j))

    return start, wait


IN_PIECES = ((0, Q_RANK, Z_QLAT), (Q_RANK, KV_RANK, Z_KVLAT), (Q_RANK + KV_RANK, ROPE, Z_KPE),
             (Q_RANK + KV_RANK + ROPE, CONV_W, Z_GB), (Q_RANK + KV_RANK + ROPE + CONV_W, CONV_W, Z_GC),
             (Q_RANK + KV_RANK + ROPE + 2 * CONV_W, CONV_W, Z_XIN), (Q_RANK + KV_RANK + ROPE + 3 * CONV_W, MEM_W, Z_QMEM),
             (Q_RANK + KV_RANK + ROPE + 3 * CONV_W + MEM_W, MIX_W, Z_GATE))
IN_SHARD = IN_COLS // N_CHIPS


def _in_segments(j):
    lo, hi = j * IN_SHARD, (j + 1) * IN_SHARD
    segs = []
    for r0, width, z0 in IN_PIECES:
        a, b = max(lo, r0), min(hi, r0 + width)
        if a < b:
            segs.append((a - lo, z0 + a - r0, b - a))
    return segs


N_SLOT = 11


def _gather_plan(l, shards, zero_rows, part="all"):
    s_in, s_uq, s_ukv, s_conv, s_mk, s_mv, s_o = shards
    ukv_c, mk_r, mk_c, o_r = s_ukv.shape[2], s_mk.shape[1], s_mk.shape[2], s_o.shape[1]
    stack = lambda s: jax.ShapeDtypeStruct((N_CHIPS,) + s.shape[1:], s.dtype)
    in_ins, in_outs = [s_in, zero_rows], [jax.ShapeDtypeStruct((Z_COLS, s_in.shape[2]), s_in.dtype)]
    rest_ins = [s_uq, s_ukv, s_conv, s_mk, s_mv, s_o]
    rest_outs = [stack(s_uq), jax.ShapeDtypeStruct((s_ukv.shape[1], N_CHIPS * ukv_c), s_ukv.dtype), stack(s_conv),
                 jax.ShapeDtypeStruct((N_CHIPS * mk_r, 2 * mk_c), s_mk.dtype),
                 jax.ShapeDtypeStruct((N_CHIPS * o_r, s_o.shape[2]), s_o.dtype)]
    with_in, with_rest = part != "rest", part != "in"

    def build(ins, outs, sems):
        ins, outs = list(ins), list(outs)
        if with_in:
            r_in, r_zero, f_in = ins.pop(0), ins.pop(0), outs.pop(0)
        if with_rest:
            r_uq, r_ukv, r_conv, r_mk, r_mv, r_o = ins
            g_uq, f_ukv, g_conv, f_mkv, f_o = outs

        def pieces(j, t):
            out = []
            if with_in:
                out += [(r_in.at[l, pl.ds(so, n), :], f_in.at[pl.ds(zo, n), :]) for so, zo, n in _in_segments(j)]
            if with_rest:
                out += [(r_uq.at[l], g_uq.at[j]), (r_ukv.at[l], f_ukv.at[:, pl.ds(j * ukv_c, ukv_c)]),
                        (r_conv.at[l], g_conv.at[j]),
                        (r_mk.at[l], f_mkv.at[pl.ds(j * mk_r, mk_r), pl.ds(0, mk_c)]),
                        (r_mv.at[l], f_mkv.at[pl.ds(j * mk_r, mk_r), pl.ds(mk_c, mk_c)]),
                        (r_o.at[l], f_o.at[pl.ds(j * o_r, o_r), :])]
            if with_in and j == t:
                out.append((r_zero, f_in.at[pl.ds(Z_KPE + ROPE, LANES - ROPE), :]))
            return out

        return _chip_copies(pieces, sems, N_SLOT)

    ins = (in_ins if with_in else []) + (rest_ins if with_rest else [])
    outs = (in_outs if with_in else []) + (rest_outs if with_rest else [])
    return _CommPlan(ins, outs, build, len(CHIP_FLIPS) * N_SLOT)


def _scatter_plan(dwt_in, c_uq, dw_ukv, c_conv, dw_mkv, dw_o, part="all"):
    ukv_c, mk_r, mk_c, o_r = dw_ukv.shape[1] // N_CHIPS, dw_mkv.shape[0] // N_CHIPS, dw_mkv.shape[1] // 2, dw_o.shape[0] // N_CHIPS
    with_in, with_rest = part != "rest", part != "in"
    in_outs = [jax.ShapeDtypeStruct((N_CHIPS, IN_SHARD, D_MODEL), BF16)]
    rest_ins = [c_uq, dw_ukv, c_conv, dw_mkv, dw_o]
    rest_outs = [jax.ShapeDtypeStruct(c_uq.shape, c_uq.dtype),
                 jax.ShapeDtypeStruct((N_CHIPS, dw_ukv.shape[0], ukv_c), dw_ukv.dtype),
                 jax.ShapeDtypeStruct(c_conv.shape, c_conv.dtype),
                 jax.ShapeDtypeStruct((N_CHIPS, mk_r, mk_c), dw_mkv.dtype), jax.ShapeDtypeStruct((N_CHIPS, mk_r, mk_c), dw_mkv.dtype),
                 jax.ShapeDtypeStruct((N_CHIPS, o_r, dw_o.shape[1]), dw_o.dtype)]

    def build(ins, outs, sems):
        ins, outs = list(ins), list(outs)
        if with_in:
            r_in, o_in = ins.pop(0), outs.pop(0)
        if with_rest:
            r_uq, r_ukv, r_conv, r_mkv, r_o = ins
            o_uq, o_ukv, o_conv, o_mk, o_mv, o_o = outs

        def pieces(j, t):
            out = []
            if with_in:
                out += [(r_in.at[pl.ds(zo, n), :], o_in.at[j, pl.ds(so, n), :]) for so, zo, n in _in_segments(t)]
            if with_rest:
                out += [(r_uq.at[t], o_uq.at[j]), (r_ukv.at[:, pl.ds(t * ukv_c, ukv_c)], o_ukv.at[j]),
                        (r_conv.at[t], o_conv.at[j]),
                        (r_mkv.at[pl.ds(t * mk_r, mk_r), pl.ds(0, mk_c)], o_mk.at[j]),
                        (r_mkv.at[pl.ds(t * mk_r, mk_r), pl.ds(mk_c, mk_c)], o_mv.at[j]),
                        (r_o.at[pl.ds(t * o_r, o_r), :], o_o.at[j])]
            return out

        return _chip_copies(pieces, sems, N_SLOT)

    ins = ([dwt_in] if with_in else []) + (rest_ins if with_rest else [])
    outs = (in_outs if with_in else []) + (rest_outs if with_rest else [])
    return _CommPlan(ins, outs, build, len(CHIP_FLIPS) * N_SLOT)


HBM = pl.BlockSpec(memory_space=pltpu.HBM)
SEM = pl.BlockSpec(memory_space=pltpu.SEMAPHORE)
SIDE_EFFECT = pltpu.SideEffectType.DATAFLOW_SIDE_EFFECTING


def _comm_start(plan, after, name):
    n_in, n_out, n_after = len(plan.ins), len(plan.out_shape), len(after)
    n_buf = n_in + n_out

    def body(*refs):
        bufs, sems, token = refs[:n_buf], refs[n_buf + n_after:n_buf + n_after + 3], refs[-1]
        start, _ = plan.build(bufs[:n_in], bufs[n_in:], sems)
        start()
        token[...] = jnp.zeros_like(token)

    lands = [lax.empty(s.shape, s.dtype) for s in plan.out_shape]
    args = [pltpu.with_memory_space_constraint(a, pltpu.HBM) for a in list(plan.ins) + lands]
    res = pl.pallas_call(
        body,
        in_specs=[HBM] * n_buf + [ANY] * n_after,
        out_specs=[SEM] * 3 + [HBM] * n_out + [pl.BlockSpec(memory_space=pltpu.VMEM)],
        out_shape=plan.scratch() + [pltpu.HBM(a.shape, a.dtype) for a in lands] + [jax.ShapeDtypeStruct((8, LANES), F32)],
        input_output_aliases={n_in + i: 3 + i for i in range(n_out)},
        compiler_params=pltpu.CompilerParams(has_side_effects=SIDE_EFFECT),
        name=name,
    )(*args, *after)
    return list(res[:3]), list(res[3:3 + n_out]), res[-1]


def _comm_finish(plan, started, after, name):
    sems, lands, _ = started
    n_in, n_out = len(plan.ins), len(plan.out_shape)
    n_buf = n_in + n_out

    def body(*refs):
        bufs_in, sem_refs = refs[:n_buf], refs[n_buf:n_buf + 3]
        _, wait = plan.build(bufs_in[:n_in], bufs_in[n_in:], sem_refs)
        wait()

    sources = [pltpu.with_memory_space_constraint(a, pltpu.HBM) for a in plan.ins]
    res = pl.pallas_call(
        body,
        in_specs=[HBM] * n_buf + [SEM] * 3 + [ANY] * len(after),
        out_specs=[HBM] * n_out,
        out_shape=[pltpu.HBM(b.shape, b.dtype) for b in lands],
        input_output_aliases={n_in + i: i for i in range(n_out)},
        compiler_params=pltpu.CompilerParams(has_side_effects=SIDE_EFFECT),
        name=name,
    )(*sources, *lands, *sems, *after)
    return list(res)


def _comm_call(plan, name):
    n_in, n_out = len(plan.ins), len(plan.out_shape)

    def body(*refs):
        start, wait = plan.build(refs[:n_in], refs[n_in:n_in + n_out], refs[n_in + n_out:])
        start()
        wait()

    return list(pl.pallas_call(
        body,
        in_specs=[ANY] * n_in,
        out_specs=[ANY] * n_out,
        out_shape=plan.out_shape,
        scratch_shapes=plan.scratch(),
        name=name,
    )(*plan.ins))


def _sibling_plan(arrays):
    def build(ins, outs, sems):
        send_sems, recv_sems, _ = sems
        sibling = (lax.axis_index("x"), lax.axis_index("y"), 1 - lax.axis_index("c"))
        copies = [pltpu.make_async_remote_copy(src_ref=src, dst_ref=dst, send_sem=send_sems.at[a], recv_sem=recv_sems.at[a],
                                               device_id=sibling, device_id_type=MESH_ID)
                  for a, (src, dst) in enumerate(zip(ins, outs))]

        def start():
            for cp in copies:
                cp.start()

        def wait():
            for cp in copies:
                cp.wait()

        return start, wait

    return _CommPlan(arrays, [jax.ShapeDtypeStruct(v.shape, v.dtype) for v in arrays], build, len(arrays))


DEVICE_FLIPS = tuple((fx, fy, fc) for fx in (0, 1) for fy in (0, 1) for fc in (0, 1))[1:]


def _gather_all(v, name):
    def body(v_ref, out_ref, send_sems, recv_sems, local_sem):
        x, y, c = lax.axis_index("x"), lax.axis_index("y"), lax.axis_index("c")
        me = 4 * x + 2 * y + c
        local = pltpu.make_async_copy(v_ref, out_ref.at[me], local_sem)
        local.start()
        copies = [local]
        for k, (fx, fy, fc) in enumerate(DEVICE_FLIPS):
            cp = pltpu.make_async_remote_copy(
                src_ref=v_ref, dst_ref=out_ref.at[me], send_sem=send_sems.at[k], recv_sem=recv_sems.at[k],
                device_id=((x + fx) % 2, (y + fy) % 2, (c + fc) % 2), device_id_type=MESH_ID)
            cp.start()
            copies.append(cp)
        for cp in copies:
            cp.wait()

    return pl.pallas_call(
        body,
        in_specs=[ANY],
        out_specs=ANY,
        out_shape=jax.ShapeDtypeStruct((N_DEV,) + v.shape, v.dtype),
        scratch_shapes=[pltpu.SemaphoreType.DMA((N_DEV - 1,)), pltpu.SemaphoreType.DMA((N_DEV - 1,)), pltpu.SemaphoreType.DMA],
        name=name,
    )(v)


def _sum_slots(parts, name):
    n, rows, cols = parts.shape
    tr = _tile(rows, 256, 16)

    def body(p_ref, o_ref):
        acc = p_ref[0].astype(F32)
        for k in range(1, n):
            acc = acc + p_ref[k].astype(F32)
        o_ref[...] = acc

    return pl.pallas_call(
        body,
        grid=(rows // tr,),
        in_specs=[pl.BlockSpec((n, tr, cols), lambda i: (0, i, 0))],
        out_specs=pl.BlockSpec((tr, cols), lambda i: (i, 0)),
        out_shape=jax.ShapeDtypeStruct((rows, cols), F32),
        compiler_params=_params("parallel"),
        name=name,
    )(parts)


def _adamw_math(w, g, m, v):
    m_new = ADAM_B1 * m + (1.0 - ADAM_B1) * g
    v_new = ADAM_B2 * v + (1.0 - ADAM_B2) * jnp.square(g)
    m_hat = m_new / (1.0 - ADAM_B1 ** ADAM_STEP)
    v_hat = v_new / (1.0 - ADAM_B2 ** ADAM_STEP)
    return -ADAM_LR * (m_hat / (jnp.sqrt(v_hat) + ADAM_EPS) + ADAM_WD * w), m_new, v_new


def _adamw(w, g, m, v, name):
    rows, cols = w.shape
    tr = _tile(rows, 256, 8)

    def body(w_ref, g_ref, m_ref, v_ref, d_out, m_out, v_out):
        d_out[...], m_out[...], v_out[...] = _adamw_math(w_ref[...], g_ref[...], m_ref[...], v_ref[...])

    blk = pl.BlockSpec((tr, cols), lambda i: (i, 0))
    out = jax.ShapeDtypeStruct((rows, cols), F32)
    return pl.pallas_call(
        body,
        grid=(rows // tr,),
        in_specs=[blk] * 4,
        out_specs=[blk] * 3,
        out_shape=[out] * 3,
        compiler_params=_params("parallel"),
        name=name,
    )(w, g, m, v)


def _adamw_layer(l, w, g_a, g_b, m, v, prev, name):
    depth, rows, cols = w.shape
    tr = _tile(rows, 256, 8)

    def body(w_ref, ga_ref, gb_ref, m_ref, v_ref, *rest):
        g_out, d_out, m_out, v_out = rest[-4:]
        g = ga_ref[...] + gb_ref[...]
        g_out[...] = g
        d_out[...], m_out[...], v_out[...] = _adamw_math(w_ref[...], g, m_ref[...], v_ref[...])

    stacked = pl.BlockSpec((None, tr, cols), lambda i: (l, i, 0))
    flat = pl.BlockSpec((tr, cols), lambda i: (i, 0))
    in_specs = [stacked, flat, flat, stacked, stacked]
    args = [w, g_a, g_b, m, v]
    aliases = {}
    if prev is not None:
        in_specs += [ANY] * 4
        args += list(prev)
        aliases = {5 + k: k for k in range(4)}
    out = jax.ShapeDtypeStruct((depth, rows, cols), F32)
    return pl.pallas_call(
        body,
        grid=(rows // tr,),
        in_specs=in_specs,
        out_specs=[stacked] * 4,
        out_shape=[out] * 4,
        input_output_aliases=aliases,
        compiler_params=_params("parallel"),
        name=name,
    )(*args)


def _cols_from_shards(g):
    _, r, c = g.shape
    return jnp.transpose(g, (1, 0, 2)).reshape(r, N_CHIPS * c)


def _cols_to_shards(full):
    r, c4 = full.shape
    c = c4 // N_CHIPS
    return jnp.transpose(full.reshape(r, N_CHIPS, c), (1, 0, 2))


IN_ORDER = (Q_RANK, KV_RANK, ROPE, CONV_W, CONV_W, CONV_W, MEM_W, MIX_W)


def _w_in_to_z_layout(w_in):
    edges = [0]
    for width in IN_ORDER:
        edges.append(edges[-1] + width)
    q_lat, kv_lat, k_pe, gb, gc, xin, q_mem, gate = [w_in[..., edges[i]:edges[i + 1]] for i in range(8)]
    pad = jnp.zeros(k_pe.shape[:-1] + (LANES - ROPE,), w_in.dtype)
    return jnp.concatenate([gate, q_lat, kv_lat, k_pe, pad, gb, gc, xin, q_mem], axis=-1)


def _w_in_from_z_layout(wz):
    cut = lambda c0, width: wz[..., c0:c0 + width]
    return jnp.concatenate(
        [cut(Z_QLAT, Q_RANK), cut(Z_KVLAT, KV_RANK), cut(Z_KPE, ROPE), cut(Z_GB, CONV_W), cut(Z_GC, CONV_W),
         cut(Z_XIN, CONV_W), cut(Z_QMEM, MEM_W), cut(Z_GATE, MIX_W)], axis=-1)


def _w_uq_pad(w_uq):
    r, _ = w_uq.shape
    w = jnp.pad(w_uq.reshape(r, MLA_HEADS, QK_HEAD), ((0, 0), (0, 0), (0, QPAD - QK_HEAD)))
    return w.reshape(r, MLA_HEADS * QPAD)


def _w_uq_unpad(w):
    r, _ = w.shape
    return w.reshape(r, MLA_HEADS, QPAD)[..., :QK_HEAD].reshape(r, MLA_HEADS * QK_HEAD)


def _rope_tables(positions):
    inv_freq = 1.0 / (ROPE_THETA ** (jnp.arange(0, ROPE, 2, dtype=F32) / ROPE))
    ang = positions.astype(F32)[:, None] * inv_freq
    cos, sin = jnp.cos(ang), jnp.sin(ang)
    s = positions.shape[0]
    zero = jnp.zeros((s, HALF_ROPE), F32)
    pad = jnp.zeros((s, LANES - ROPE), F32)
    kc = jnp.concatenate([cos, cos, pad], axis=-1)
    ka = jnp.concatenate([-sin, zero, pad], axis=-1)
    kb = jnp.concatenate([zero, sin, pad], axis=-1)
    qc = jnp.concatenate([jnp.ones((s, NOPE), F32), kc], axis=-1)
    qa = jnp.concatenate([jnp.zeros((s, NOPE), F32), ka], axis=-1)
    qb = jnp.concatenate([jnp.zeros((s, NOPE), F32), kb], axis=-1)
    return (qc, qa, qb), (kc, ka, kb)


def _layer_weights(gathered):
    return (gathered[0],) + _late_weights(gathered[1:])


def _late_weights(gathered):
    g_uq, w_ukv, g_conv, w_mkv, w_o = gathered
    return (_w_uq_pad(_cols_from_shards(g_uq)), w_ukv, _cols_from_shards(g_conv), w_mkv, w_o)


def _layer_fwd(l, x, mem, wts, gains, tabs, comm, late=None, h=None):
    wt_in = wts[0]
    g_pre, g_q, g_kv, g_mem, g_post = gains
    q_tab, k_tab = tabs
    tag = f"l{l}_"
    if h is None:
        h = _rmsnorm_fwd(x, g_pre, 0, D_MODEL, tag + "pre_norm")
    z = _matmul(h, wt_in, "nt", BF16, tag + "in_proj", tm_cap=1024, tn_cap=1664)
    w_uq, w_ukv, conv_w, w_mkv, w_o = wts[1:] if late is None else late(z)
    wts = (wt_in, w_uq, w_ukv, conv_w, w_mkv, w_o)
    qn = _rmsnorm_fwd(z, g_q, Z_QLAT, Q_RANK, tag + "q_norm")
    kvn = _rmsnorm_fwd(z, g_kv, Z_KVLAT, KV_RANK, tag + "kv_norm")
    q_raw = _matmul(qn, w_uq, "nn", F32, tag + "uq", tm_cap=1024)
    kv = _matmul(kvn, w_ukv, "nn", BF16, tag + "ukv")
    kpe = _rope(z, *k_tab, Z_KPE, LANES, 1, tag + "k_rope")
    cat, a_lse, arrived = _attn_fwd(q_raw, kv, kpe, kv, q_tab, MLA_HEADS, QPAD, 0, 0, 2, 1, 2, QK_HEAD ** -0.5, 512,
                                    tag + "mla_fwd", comm, o_into=(MIX_W, 0, None))
    cat = _conv_fwd(z, conv_w, cat, tag + "conv_fwd")
    mem_n = _rmsnorm_fwd(mem, g_mem, 0, D_MODEL, tag + "mem_norm")
    mkv = _matmul(mem_n, w_mkv, "nn", BF16, tag + "mem_kv")
    cat, m_lse, _ = _attn_fwd(z, mkv, None, mkv, None, MEM_HEADS, LANES, Z_QMEM // LANES, 0, 1, MEM_HEADS, 1,
                              MEM_HEAD ** -0.5, 1024, tag + "mem_fwd", o_into=(MIX_W, (MLA_W + CONV_W) // LANES, cat))
    y = _gate_fwd(cat, z, tag + "gate_fwd")
    o = _matmul(y, w_o, "nn", F32, tag + "out_proj", tm_cap=1024)
    x_new = _post_norm_residual(x, o, g_post, tag + "post_norm")
    saved = (x, h, z, qn, kvn, q_raw, kv, kpe, a_lse, mem_n, mkv, m_lse, cat, y, o)
    return x_new, saved, arrived


def _layer_bwd(l, g, mem, saved, wts, gains, tabs_bwd, comm, split_exchange=False):
    wt_in, w_uq, w_ukv, conv_w, w_mkv, w_o = wts
    g_pre, g_q, g_kv, g_mem, g_post = gains
    q_tab, k_tab_bwd = tabs_bwd
    x, h, z, qn, kvn, q_raw, kv, kpe, a_lse, mem_n, mkv, m_lse, cat, y, o = saved
    tag = f"l{l}_"
    do, dg_post = _rmsnorm_bwd(o, g_post, g, None, 0, D_MODEL, BF16, tag + "post_norm_bwd")
    dcat, dgate = _out_proj_dx_gate_bwd(do, w_o, cat, z, tag + "out_proj_dx")
    dw_o = _matmul(y, do, "tn", BF16, tag + "out_proj_dw", tm_cap=1024)
    dq, dkv, dkpe_h, arrived = _attn_bwd(q_raw, kv, kpe, kv, cat, dcat, a_lse, q_tab, MLA_HEADS, QPAD, 0, 0, 2, 1, 2, 0,
                                         QK_HEAD ** -0.5, 512, tag + "mla_bwd", comm)
    dkpe = _kpe_grad(dkpe_h, *k_tab_bwd, MLA_HEADS, tag + "k_rope_bwd")
    dw_ukv = _matmul(kvn, dkv, "tn", BF16, tag + "ukv_dw")
    dkvn = _matmul(dkv, w_ukv, "nt", F32, tag + "ukv_dx")
    dkv_lat, dg_kv = _rmsnorm_bwd(z, g_kv, dkvn, None, Z_KVLAT, KV_RANK, BF16, tag + "kv_norm_bwd")
    dw_uq = _matmul(qn, dq, "tn", BF16, tag + "uq_dw")
    dqn = _matmul(dq, w_uq, "nt", F32, tag + "uq_dx")
    dq_lat, dg_q = _rmsnorm_bwd(z, g_q, dqn, None, Z_QLAT, Q_RANK, BF16, tag + "q_norm_bwd")
    dgb, dgc, dxin, dconv_w = _conv_bwd(z, conv_w, dcat, tag + "conv_bwd")
    dq_mem, dmk, dmv, _ = _attn_bwd(z, mkv, None, mkv, cat, dcat, m_lse, None, MEM_HEADS, LANES, Z_QMEM // LANES, 0, 1,
                                    MEM_HEADS, 1, (MLA_W + CONV_W) // LANES, MEM_HEAD ** -0.5, 1024, tag + "mem_bwd")
    dmkv = jnp.concatenate([dmk, dmv], axis=-1)
    dw_mkv = _matmul(mem_n, dmkv, "tn", BF16, tag + "mem_kv_dw")
    dmem_n = _matmul(dmkv, w_mkv, "nt", F32, tag + "mem_kv_dx")
    _, dg_mem = _rmsnorm_bwd(mem, g_mem, dmem_n, None, 0, D_MODEL, BF16, tag + "mem_norm_bwd")
    others = (_cols_to_shards(_w_uq_unpad(dw_uq)), dw_ukv, _cols_to_shards(dconv_w), dw_mkv, dw_o)
    early = None
    if split_exchange:
        early_plan = _scatter_plan(None, *others, part="rest")
        early = (early_plan, _comm_start(early_plan, [dmem_n], tag + "exchange_rest_start"))
        g_pre = g_pre + early[1][2][0:1, 0:1]
    dz = jnp.concatenate([dgate, dq_lat, dkv_lat, dkpe, dgb, dgc, dxin, dq_mem], axis=-1)
    dwt_in = _matmul(dz, h, "tn", BF16, tag + "in_proj_dw", tm_cap=1664, tk_cap=2048)
    contrib = _scatter_plan(dwt_in, *others, part="in" if split_exchange else "all")
    late = None
    if split_exchange:
        late = (contrib, _comm_start(contrib, [dwt_in], tag + "exchange_in_start"))
    dh = _matmul(dz, wt_in, "nn", F32, tag + "in_proj_dx", tm_cap=1024, tk_cap=1664, after=late[1][2] if late else None)
    dx, dg_pre = _rmsnorm_bwd(x, g_pre, dh, g, 0, D_MODEL, F32, tag + "pre_norm_bwd")
    return dx, contrib, (dg_pre, dg_q, dg_kv, dg_mem, dg_post), (early, late)


GAIN_WIDTHS = (D_MODEL, Q_RANK, KV_RANK, D_MODEL, D_MODEL)


def _pack_gains(parts):
    return jnp.concatenate([p.reshape(-1) for p in parts]).reshape(-1, LANES)


def _unpack_gains(packed, depth):
    flat = packed.reshape(-1)
    out, at = [], 0
    for width in GAIN_WIDTHS:
        out.append(flat[at:at + depth * width].reshape(depth, width))
        at += depth * width
    return out


def kernel(x, mem, positions, pre_norm_g, w_in, q_norm_g, w_uq, kv_norm_g, w_ukv, conv_w, mem_norm_g, w_mk, w_mv, w_o, post_norm_g, loss_target, m_pre_norm_g, m_w_in, m_q_norm_g, m_w_uq, m_kv_norm_g, m_w_ukv, m_conv_w, m_mem_norm_g, m_w_mk, m_w_mv, m_w_o, m_post_norm_g, v_pre_norm_g, v_w_in, v_q_norm_g, v_w_uq, v_kv_norm_g, v_w_ukv, v_conv_w, v_mem_norm_g, v_w_mk, v_w_mv, v_w_o, v_post_norm_g):
    depth = w_in.shape[0]
    x0, mem0, target = x[0], mem[0], loss_target[0]
    tabs = _rope_tables(positions[0])
    tabs_bwd = (tabs[0], (tabs[1][0], -tabs[1][1], -tabs[1][2]))

    flip = lambda t: jnp.transpose(t, (0, 2, 1))
    w_in, m_w_in, v_w_in = flip(w_in), flip(m_w_in), flip(v_w_in)
    shards = [w_in.astype(BF16), w_uq.astype(BF16), w_ukv.astype(BF16), conv_w, w_mk.astype(BF16), w_mv.astype(BF16),
              w_o.astype(BF16)]
    zero_rows = lambda: jnp.zeros((LANES - ROPE, D_MODEL), BF16)

    def layer_gains(l):
        return tuple(g[l][None, :] for g in (pre_norm_g, q_norm_g, kv_norm_g, mem_norm_g, post_norm_g))

    wts, saved = [None] * depth, [None] * depth
    plan_in, plan_rest = _gather_plan(0, shards, zero_rows(), "in"), _gather_plan(0, shards, zero_rows(), "rest")
    started_in = _comm_start(plan_in, [positions], "l0_gather_in_start")
    started_rest = _comm_start(plan_rest, [started_in[2]], "l0_gather_rest_start")
    h0 = _rmsnorm_fwd(x0, layer_gains(0)[0], 0, D_MODEL, "l0_pre_norm")
    wts[0] = tuple(_comm_finish(plan_in, started_in, [started_rest[2], h0, tabs[0][0]], "l0_gather_in_wait"))

    next_gather = {}

    def start_next_gather(l, after):
        plan = _gather_plan(l + 1, shards, zero_rows())
        next_gather[l + 1] = (plan, _comm_start(plan, [after], f"l{l + 1}_gather_start"))
        return next_gather[l + 1][1][2][0:1, 0:1]

    def rest_of_layer0(z):
        got = _late_weights(_comm_finish(plan_rest, started_rest, [z], "l0_gather_rest_wait"))
        wts[0] = wts[0] + got
        if depth > 1:
            got = (got[0] + start_next_gather(0, got[4]).astype(BF16),) + got[1:]
        return got

    act = x0
    for l in range(depth):
        gains = layer_gains(l)
        if 0 < l < depth - 1:
            gains = (gains[0] + start_next_gather(l, wts[l][5]),) + gains[1:]
        act, saved[l], _ = _layer_fwd(l, act, mem0, wts[l], gains, tabs, None, rest_of_layer0 if l == 0 else None,
                                      h0 if l == 0 else None)
        if l + 1 < depth:
            plan, started = next_gather[l + 1]
            wts[l + 1] = _layer_weights(_comm_finish(plan, started, [act], f"l{l + 1}_gather_wait"))
    grad, loss_part = _loss_head(act, target, "loss_head")
    loss = lax.psum(loss_part[0, 0], ("x", "y", "c"))

    names = ("w_in", "w_uq", "w_ukv", "conv_w", "w_mk", "w_mv", "w_o")
    w_shards = (w_in, w_uq, w_ukv, conv_w, w_mk, w_mv, w_o)
    m_shards = (m_w_in, m_w_uq, m_w_ukv, m_conv_w, m_w_mk, m_w_mv, m_w_o)
    v_shards = (v_w_in, v_w_uq, v_w_ukv, v_conv_w, v_w_mk, v_w_mv, v_w_o)
    stacked = [None] * len(names)

    def sum_and_send(l, received):
        partial = [_sum_slots(r, f"l{l}_grad_sum_{names[i]}") for i, r in enumerate(received)]
        plan = _sibling_plan(partial)
        return l, partial, plan, _comm_start(plan, [partial[0]], f"l{l}_sibling_start")

    def receive_and_update(state, after):
        l, partial, plan, started = state
        other = _comm_finish(plan, started, [after], f"l{l}_sibling_wait")
        for i, name in enumerate(names):
            stacked[i] = _adamw_layer(l, w_shards[i], partial[i], other[i], m_shards[i], v_shards[i], stacked[i],
                                      f"l{l}_adamw_{name}")

    dgs = [None] * depth
    pending = None
    in_flight = None
    for l in reversed(range(depth)):
        gains = layer_gains(l)
        for token in ([pending[1][2]] if pending else []) + ([in_flight[3][2]] if in_flight else []):
            gains = gains[:4] + (gains[4] + token[0:1, 0:1],)
        grad, contrib, dgs[l], early = _layer_bwd(l, grad, mem0, saved[l], wts[l], gains, tabs_bwd, None, l == 0)
        if in_flight is not None:
            receive_and_update(in_flight, grad)
            in_flight = None
        if pending is not None:
            in_flight = sum_and_send(l + 1, _comm_finish(pending[0], pending[1], [grad], f"l{l + 1}_exchange_wait"))
        if l > 0:
            pending = (contrib, _comm_start(contrib, [grad], f"l{l}_exchange_start"))
    early, late = early
    got_in = _comm_finish(late[0], late[1], [grad], "l0_exchange_in_wait")
    last = sum_and_send(0, got_in + _comm_finish(early[0], early[1], [got_in[0]], "l0_exchange_rest_wait"))
    if in_flight is not None:
        receive_and_update(in_flight, last[1][0])
    receive_and_update(last, stacked[0][0] if depth > 1 else last[1][0])
    grad_x = grad[None]
    results = {name: tuple(stacked[i]) for i, name in enumerate(names)}
    results["w_in"] = tuple(flip(t) for t in results["w_in"])

    gain_names = ("pre_norm_g", "q_norm_g", "kv_norm_g", "mem_norm_g", "post_norm_g")
    dg_packed = _pack_gains([jnp.concatenate([dgs[l][i] for l in range(depth)], axis=0) for i in range(5)])
    dg_total = _sum_slots(_gather_all(dg_packed, "gain_gather"), "gain_sum")
    gain_outs = (dg_total,) + tuple(_adamw(
        _pack_gains((pre_norm_g, q_norm_g, kv_norm_g, mem_norm_g, post_norm_g)), dg_total,
        _pack_gains((m_pre_norm_g, m_q_norm_g, m_kv_norm_g, m_mem_norm_g, m_post_norm_g)),
        _pack_gains((v_pre_norm_g, v_q_norm_g, v_kv_norm_g, v_mem_norm_g, v_post_norm_g)), "adamw_gains"))
    gain_outs = [_unpack_gains(t, depth) for t in gain_outs]
    for i, name in enumerate(gain_names):
        results[name] = tuple(gain_outs[k][i] for k in range(4))

    order = ("pre_norm_g", "w_in", "q_norm_g", "w_uq", "kv_norm_g", "w_ukv", "conv_w", "mem_norm_g", "w_mk", "w_mv", "w_o",
             "post_norm_g")
    out = [loss, grad_x]
    for k in range(4):
        out += [results[name][k] for name in order]
    return tuple(out)
```

```python
import functools

import jax
import jax.numpy as jnp
from jax import lax
from jax.experimental import pallas as pl
from jax.experimental.pallas import tpu as pltpu

F32 = jnp.float32
BF16 = jnp.bfloat16
MESH_ID = pl.DeviceIdType.MESH

D_MODEL = 2048
EPS = 1e-6
LOG2_E = 1.4426950408889634
ROPE_THETA = 10000.0
MLA_HEADS = 8
NOPE = 128
ROPE = 64
HALF_ROPE = ROPE // 2
QK_HEAD = NOPE + ROPE
V_HEAD = 128
Q_RANK = 512
KV_RANK = 256
CONV_W = 512
MEM_HEADS = 4
MEM_HEAD = 128
MEM_W = MEM_HEADS * MEM_HEAD
MLA_W = MLA_HEADS * V_HEAD
MIX_W = MLA_W + CONV_W + MEM_W
IN_COLS = Q_RANK + KV_RANK + ROPE + 3 * CONV_W + MEM_W + MIX_W
N_CHIPS = 4
N_DEV = 8

LANES = 128
VMEM_LIMIT_BYTES = 56 * 1024 * 1024

QPAD = 2 * LANES
Z_GATE = 0
Z_QLAT = Z_GATE + MIX_W
Z_KVLAT = Z_QLAT + Q_RANK
Z_KPE = Z_KVLAT + KV_RANK
Z_GB = Z_KPE + LANES
Z_GC = Z_GB + CONV_W
Z_XIN = Z_GC + CONV_W
Z_QMEM = Z_XIN + CONV_W
Z_COLS = Z_QMEM + MEM_W

ADAM_LR = 0.001
ADAM_B1 = 0.9
ADAM_B2 = 0.999
ADAM_EPS = 1e-08
ADAM_WD = 0.01
ADAM_STEP = 10


def _tile(dim, cap, unit):
    if dim <= cap:
        return dim
    t = (cap // unit) * unit
    while t >= unit:
        if dim % t == 0:
            return t
        t -= unit
    raise ValueError(f"no tile of {dim} under {cap} in units of {unit}")


def _params(*semantics):
    return pltpu.CompilerParams(dimension_semantics=semantics, vmem_limit_bytes=VMEM_LIMIT_BYTES)


def _matmul(a, b, mode, out_dtype, name, tm_cap=512, tn_cap=1024, tk_cap=2048, after=None):
    if mode == "nn":
        (m, k), (k2, n) = a.shape, b.shape
    elif mode == "nt":
        (m, k), (n, k2) = a.shape, b.shape
    else:
        (k, m), (k2, n) = a.shape, b.shape
    assert k == k2, (a.shape, b.shape, mode)
    tm = _tile(m, tm_cap, LANES if mode == "tn" else 16)
    tn = _tile(n, tn_cap, LANES)
    tk = _tile(k, tk_cap, LANES if mode != "tn" else 16)
    nk = k // tk
    if mode == "nn":
        a_spec = pl.BlockSpec((tm, tk), lambda i, j, kk: (i, kk))
        b_spec = pl.BlockSpec((tk, tn), lambda i, j, kk: (kk, j))
        dims = (((1,), (0,)), ((), ()))
    elif mode == "nt":
        a_spec = pl.BlockSpec((tm, tk), lambda i, j, kk: (i, kk))
        b_spec = pl.BlockSpec((tn, tk), lambda i, j, kk: (j, kk))
        dims = (((1,), (1,)), ((), ()))
    else:
        a_spec = pl.BlockSpec((tk, tm), lambda i, j, kk: (kk, i))
        b_spec = pl.BlockSpec((tk, tn), lambda i, j, kk: (kk, j))
        dims = (((0,), (0,)), ((), ()))

    def body(a_ref, b_ref, *rest):
        o_ref, scratch = (rest[1], rest[2:]) if after is not None else (rest[0], rest[1:])
        part = lax.dot_general(a_ref[...].astype(BF16), b_ref[...].astype(BF16), dims, preferred_element_type=F32)
        if nk == 1:
            o_ref[...] = part.astype(o_ref.dtype)
            return
        (acc_ref,) = scratch
        kk = pl.program_id(2)

        @pl.when(kk == 0)
        def _():
            acc_ref[...] = part

        @pl.when(kk > 0)
        def _():
            acc_ref[...] += part

        @pl.when(kk == nk - 1)
        def _():
            o_ref[...] = acc_ref[...].astype(o_ref.dtype)

    return pl.pallas_call(
        body,
        grid=(m // tm, n // tn, nk),
        in_specs=[a_spec, b_spec] + ([] if after is None else [pl.BlockSpec(memory_space=pl.ANY)]),
        out_specs=pl.BlockSpec((tm, tn), lambda i, j, kk: (i, j)),
        out_shape=jax.ShapeDtypeStruct((m, n), out_dtype),
        scratch_shapes=[] if nk == 1 else [pltpu.VMEM((tm, tn), F32)],
        compiler_params=_params("parallel", "parallel", "arbitrary"),
        name=name,
    )(*([a, b] if after is None else [a, b, after]))


def _rmsnorm_fwd(x, gain, col0, width, name):
    rows = x.shape[0]
    tr = _tile(rows, 512, 16)
    cb = col0 // width
    assert cb * width == col0

    def body(x_ref, g_ref, o_ref):
        xv = x_ref[...].astype(F32)
        r = lax.rsqrt(jnp.mean(xv * xv, axis=-1, keepdims=True) + EPS)
        o_ref[...] = (xv * r * g_ref[...]).astype(o_ref.dtype)

    return pl.pallas_call(
        body,
        grid=(rows // tr,),
        in_specs=[pl.BlockSpec((tr, width), lambda i: (i, cb)), pl.BlockSpec((1, width), lambda i: (0, 0))],
        out_specs=pl.BlockSpec((tr, width), lambda i: (i, 0)),
        out_shape=jax.ShapeDtypeStruct((rows, width), BF16),
        compiler_params=_params("parallel"),
        name=name,
    )(x, gain)


def _rmsnorm_bwd(x, gain, dy, resid, col0, width, out_dtype, name):
    rows = x.shape[0]
    tr = _tile(rows, 256, 16)
    cb = col0 // width
    assert cb * width == col0
    has_resid = resid is not None

    def body(*refs):
        if has_resid:
            x_ref, g_ref, dy_ref, res_ref, dx_ref, dg_ref = refs
        else:
            x_ref, g_ref, dy_ref, dx_ref, dg_ref = refs
        i = pl.program_id(0)
        xv = x_ref[...].astype(F32)
        dyv = dy_ref[...].astype(F32)
        r = lax.rsqrt(jnp.mean(xv * xv, axis=-1, keepdims=True) + EPS)
        xr = xv * r
        dyg = dyv * g_ref[...]
        c = jnp.mean(dyg * xr, axis=-1, keepdims=True)
        dx = r * (dyg - xr * c)
        if has_resid:
            dx = dx + res_ref[...]
        dx_ref[...] = dx.astype(dx_ref.dtype)
        part = jnp.sum(dyv * xr, axis=0, keepdims=True)

        @pl.when(i == 0)
        def _():
            dg_ref[...] = part

        @pl.when(i > 0)
        def _():
            dg_ref[...] += part

    row_spec = pl.BlockSpec((tr, width), lambda i: (i, 0))
    in_specs = [pl.BlockSpec((tr, width), lambda i: (i, cb)), pl.BlockSpec((1, width), lambda i: (0, 0)), row_spec]
    args = [x, gain, dy]
    if has_resid:
        in_specs.append(row_spec)
        args.append(resid)
    return pl.pallas_call(
        body,
        grid=(rows // tr,),
        in_specs=in_specs,
        out_specs=[row_spec, pl.BlockSpec((1, width), lambda i: (0, 0))],
        out_shape=[jax.ShapeDtypeStruct((rows, width), out_dtype), jax.ShapeDtypeStruct((1, width), F32)],
        compiler_params=_params("arbitrary"),
        name=name,
    )(*args)


def _post_norm_residual(x, o, gain, name):
    rows, width = x.shape
    tr = _tile(rows, 256, 8)

    def body(x_ref, o_ref, g_ref, out_ref):
        ov = o_ref[...]
        r = lax.rsqrt(jnp.mean(ov * ov, axis=-1, keepdims=True) + EPS)
        out_ref[...] = x_ref[...] + ov * r * g_ref[...]

    row_spec = pl.BlockSpec((tr, width), lambda i: (i, 0))
    return pl.pallas_call(
        body,
        grid=(rows // tr,),
        in_specs=[row_spec, row_spec, pl.BlockSpec((1, width), lambda i: (0, 0))],
        out_specs=row_spec,
        out_shape=jax.ShapeDtypeStruct((rows, width), F32),
        compiler_params=_params("parallel"),
        name=name,
    )(x, o, gain)


def _rope(x, tab_c, tab_a, tab_b, col0, width, heads, name):
    rows = x.shape[0]
    tr = _tile(rows, 512, 16)
    cb = col0 // width
    assert cb * width == col0

    def body(x_ref, c_ref, a_ref, b_ref, o_ref):
        xv = x_ref[...].astype(F32)
        up = pltpu.roll(xv, width - HALF_ROPE, 1)
        down = pltpu.roll(xv, HALF_ROPE, 1)
        o_ref[...] = (xv * c_ref[...] + up * a_ref[...] + down * b_ref[...]).astype(o_ref.dtype)

    tab_spec = pl.BlockSpec((tr, width), lambda i, h: (i, 0))
    return pl.pallas_call(
        body,
        grid=(rows // tr, heads),
        in_specs=[pl.BlockSpec((tr, width), lambda i, h: (i, cb + h)), tab_spec, tab_spec, tab_spec],
        out_specs=pl.BlockSpec((tr, width), lambda i, h: (i, h)),
        out_shape=jax.ShapeDtypeStruct((rows, heads * width), BF16),
        compiler_params=_params("parallel", "parallel"),
        name=name,
    )(x, tab_c, tab_a, tab_b)


def _kpe_grad(dkb, tab_c, tab_a, tab_b, heads, name):
    rows = dkb.shape[0]
    tr = _tile(rows, 512, 16)

    def body(d_ref, c_ref, a_ref, b_ref, o_ref):
        acc = d_ref[:, 0:LANES]
        for h in range(1, heads):
            acc = acc + d_ref[:, h * LANES:(h + 1) * LANES]
        up = pltpu.roll(acc, LANES - HALF_ROPE, 1)
        down = pltpu.roll(acc, HALF_ROPE, 1)
        o_ref[...] = (acc * c_ref[...] + up * a_ref[...] + down * b_ref[...]).astype(o_ref.dtype)

    tab_spec = pl.BlockSpec((tr, LANES), lambda i: (i, 0))
    return pl.pallas_call(
        body,
        grid=(rows // tr,),
        in_specs=[pl.BlockSpec((tr, heads * LANES), lambda i: (i, 0)), tab_spec, tab_spec, tab_spec],
        out_specs=tab_spec,
        out_shape=jax.ShapeDtypeStruct((rows, LANES), BF16),
        compiler_params=_params("parallel"),
        name=name,
    )(dkb, tab_c, tab_a, tab_b)


class _CommPlan:
    def __init__(self, ins, out_shape, build, n_copies):
        self.ins, self.out_shape, self.build, self.n_copies = list(ins), list(out_shape), build, n_copies

    def scratch(self):
        n = self.n_copies
        return [pltpu.SemaphoreType.DMA((n,)), pltpu.SemaphoreType.DMA((n,)), pltpu.SemaphoreType.DMA((n,))]


def _split_comm(refs, n_in, n_out, comm):
    if comm is None:
        return refs, None
    ci, co = len(comm.ins), len(comm.out_shape)
    ins, c_ins = refs[:n_in], refs[n_in:n_in + ci]
    outs, c_outs = refs[n_in + ci:n_in + ci + n_out], refs[n_in + ci + n_out:n_in + ci + n_out + co]
    rest = refs[n_in + ci + n_out + co:]
    scratch, sems = rest[:-3], rest[-3:]
    return tuple(ins) + tuple(outs) + tuple(scratch), functools.partial(comm.build, c_ins, c_outs, sems)


def _ride_start(copies, first):
    if copies is not None:
        pl.when(first)(copies()[0])


def _ride_wait(copies, last):
    if copies is not None:
        pl.when(last)(copies()[1])


def _rope_rows(x, c, a, b, sign):
    width = x.shape[-1]
    mixed = pltpu.roll(x, width - HALF_ROPE, 1) * a + pltpu.roll(x, HALF_ROPE, 1) * b
    return x * c + mixed if sign > 0 else x * c - mixed


def _attn_fwd(q, ka, kb, v, rope, heads, q_w, q_cb, ka_cb, ka_step, v_cb, v_step, scale, tq_cap, name, comm=None, tk_cap=512,
              o_into=None):
    s_q, s_k = q.shape[0], ka.shape[0]
    tq = _tile(s_q, tq_cap, 16)
    nq = s_q // tq
    has_kb = kb is not None
    n_in = 7 if has_kb else 3
    tk = _tile(s_k, tk_cap, LANES)
    o_cols, o_cb, o_old = o_into if o_into is not None else (heads * LANES, 0, None)
    assert comm is None or o_old is None

    def body(*refs):
        if o_old is not None:
            refs = refs[:n_in] + refs[n_in + 1:]
        refs, copies = _split_comm(refs, n_in, 2, comm)
        first = jnp.logical_and(pl.program_id(0) == 0, pl.program_id(1) == 0)
        last = jnp.logical_and(pl.program_id(0) == heads - 1, pl.program_id(1) == nq - 1)
        _ride_start(copies, first)
        if has_kb:
            q_ref, ka_ref, kb_ref, v_ref, c_ref, a_ref, b_ref, o_ref, lse_ref, k_scr = refs

            @pl.when(pl.program_id(1) == 0)
            def _():
                k_scr[:, 0:LANES] = ka_ref[...].astype(BF16)
                k_scr[:, LANES:2 * LANES] = kb_ref[...].astype(BF16)

            keys = k_scr
            qv = _rope_rows(q_ref[...], c_ref[...], a_ref[...], b_ref[...], 1).astype(BF16)
        else:
            q_ref, ka_ref, v_ref, o_ref, lse_ref = refs
            keys = ka_ref
            qv = q_ref[...].astype(BF16)
        c2 = scale * LOG2_E
        m = l = o = None
        nk = s_k // tk
        scores = lambda j: lax.dot_general(qv, keys[j * tk:(j + 1) * tk, :].astype(BF16), (((1,), (1,)), ((), ())),
                                           preferred_element_type=F32)
        s_next = scores(0)
        for j in range(nk):
            sj = s_next
            if j + 1 < nk:
                s_next = scores(j + 1)
            mj = jnp.max(sj, axis=-1, keepdims=True)
            m_new = mj if m is None else jnp.maximum(m, mj)
            pj = jnp.exp2((sj - m_new) * c2)
            lj = jnp.sum(pj, axis=-1, keepdims=True)
            oj = jnp.dot(pj.astype(BF16), v_ref[j * tk:(j + 1) * tk, :].astype(BF16), preferred_element_type=F32)
            if m is None:
                l, o = lj, oj
            else:
                alpha = jnp.exp2((m - m_new) * c2)
                l, o = l * alpha + lj, o * alpha + oj
            m = m_new
        o_ref[...] = (o * (1.0 / l)).astype(o_ref.dtype)
        lse_ref[...] = jnp.broadcast_to(m * c2 + jnp.log2(l), lse_ref.shape)
        _ride_wait(copies, last)

    in_specs = [pl.BlockSpec((tq, q_w), lambda h, i: (i, q_cb + h)),
                pl.BlockSpec((s_k, LANES), lambda h, i: (0, ka_cb + ka_step * h))]
    args = [q, ka]
    if has_kb:
        in_specs.append(pl.BlockSpec((s_k, LANES), lambda h, i: (0, 0)))
        args.append(kb)
    in_specs.append(pl.BlockSpec((s_k, LANES), lambda h, i: (0, v_cb + v_step * h)))
    args.append(v)
    if has_kb:
        in_specs += [pl.BlockSpec((tq, q_w), lambda h, i: (i, 0))] * 3
        args += list(rope)
    aliases = {}
    if o_old is not None:
        aliases = {len(args): 0}
        in_specs.append(ANY)
        args.append(o_old)
    out_specs = [pl.BlockSpec((tq, LANES), lambda h, i: (i, o_cb + h)), pl.BlockSpec((tq, LANES), lambda h, i: (i, h))]
    out_shape = [jax.ShapeDtypeStruct((s_q, o_cols), BF16), jax.ShapeDtypeStruct((s_q, heads * LANES), F32)]
    scratch = [pltpu.VMEM((s_k, 2 * LANES), BF16)] if has_kb else []
    if comm is not None:
        in_specs += [ANY] * len(comm.ins)
        args += comm.ins
        out_specs += [ANY] * len(comm.out_shape)
        out_shape += comm.out_shape
        scratch += comm.scratch()
    res = pl.pallas_call(
        body,
        grid=(heads, nq),
        in_specs=in_specs,
        out_specs=out_specs,
        out_shape=out_shape,
        scratch_shapes=scratch,
        input_output_aliases=aliases,
        compiler_params=_params("arbitrary", "arbitrary"),
        name=name,
    )(*args)
    return res[0], res[1], list(res[2:])


def _attn_bwd(q, ka, kb, v, o, do, lse, rope, heads, q_w, q_cb, ka_cb, ka_step, v_cb, v_step, o_cb, scale, tq_cap, name,
              comm=None, tk_cap=512):
    s_q, s_k = q.shape[0], ka.shape[0]
    tq = _tile(s_q, tq_cap, 16)
    nq = s_q // tq
    has_kb = kb is not None
    n_in = 10 if has_kb else 6
    n_out = 3
    tk = _tile(s_k, tk_cap, LANES)

    def body(*refs):
        refs, copies = _split_comm(refs, n_in, n_out, comm)
        first = jnp.logical_and(pl.program_id(0) == 0, pl.program_id(1) == 0)
        last = jnp.logical_and(pl.program_id(0) == heads - 1, pl.program_id(1) == nq - 1)
        _ride_start(copies, first)
        if has_kb:
            (q_ref, ka_ref, kb_ref, v_ref, o_ref, do_ref, lse_ref, c_ref, a_ref, b_ref, dq_ref, dkv_ref, dkb_ref, k_scr, dk_acc,
             dv_acc) = refs
        else:
            q_ref, ka_ref, v_ref, o_ref, do_ref, lse_ref, dq_ref, dka_ref, dv_ref, dk_acc, dv_acc = refs
        i = pl.program_id(1)

        @pl.when(i == 0)
        def _():
            dk_acc[...] = jnp.zeros_like(dk_acc)
            dv_acc[...] = jnp.zeros_like(dv_acc)
            if has_kb:
                k_scr[:, 0:LANES] = ka_ref[...].astype(BF16)
                k_scr[:, LANES:2 * LANES] = kb_ref[...].astype(BF16)

        keys = k_scr if has_kb else ka_ref
        if has_kb:
            qv = _rope_rows(q_ref[...], c_ref[...], a_ref[...], b_ref[...], 1).astype(BF16)
        else:
            qv = q_ref[...].astype(BF16)
        dov = do_ref[...].astype(BF16)
        delta = jnp.sum(dov.astype(F32) * o_ref[...].astype(F32), axis=-1, keepdims=True)
        lse2 = lse_ref[:, 0:1]
        c2 = scale * LOG2_E
        nk = s_k // tk
        rows = lambda j: slice(j * tk, (j + 1) * tk)
        nt = (((1,), (1,)), ((), ()))
        tn = (((0,), (0,)), ((), ()))

        def scores(j):
            return (lax.dot_general(qv, keys[rows(j), :].astype(BF16), nt, preferred_element_type=F32),
                    lax.dot_general(dov, v_ref[rows(j), :].astype(BF16), nt, preferred_element_type=F32))

        nxt = scores(0)
        dq = None
        for j in range(nk):
            sj, dpj = nxt
            if j + 1 < nk:
                nxt = scores(j + 1)
            pj = jnp.exp2(sj * c2 - lse2)
            dsj = (pj * (dpj - delta)).astype(BF16)
            dqj = jnp.dot(dsj, keys[rows(j), :].astype(BF16), preferred_element_type=F32)
            dq = dqj if dq is None else dq + dqj
            dk_acc[rows(j), :] += lax.dot_general(dsj, qv, tn, preferred_element_type=F32)
            dv_acc[rows(j), :] += lax.dot_general(pj.astype(BF16), dov, tn, preferred_element_type=F32)
        dq = dq * scale
        if has_kb:
            dq = _rope_rows(dq, c_ref[...], a_ref[...], b_ref[...], -1)
        dq_ref[...] = dq.astype(dq_ref.dtype)

        @pl.when(i == nq - 1)
        def _():
            if has_kb:
                dkv_ref[:, 0:LANES] = (dk_acc[:, 0:LANES] * scale).astype(dkv_ref.dtype)
                dkv_ref[:, LANES:2 * LANES] = dv_acc[...].astype(dkv_ref.dtype)
                dkb_ref[...] = dk_acc[:, LANES:2 * LANES] * scale
            else:
                dka_ref[...] = (dk_acc[...] * scale).astype(dka_ref.dtype)
                dv_ref[...] = dv_acc[...].astype(dv_ref.dtype)

        _ride_wait(copies, last)

    key_spec = lambda cb, step: pl.BlockSpec((s_k, LANES), lambda h, i: (0, cb + step * h))
    row_spec = lambda cb: pl.BlockSpec((tq, LANES), lambda h, i: (i, cb + h))
    in_specs = [pl.BlockSpec((tq, q_w), lambda h, i: (i, q_cb + h)), key_spec(ka_cb, ka_step)]
    args = [q, ka]
    if has_kb:
        in_specs.append(pl.BlockSpec((s_k, LANES), lambda h, i: (0, 0)))
        args.append(kb)
    in_specs += [key_spec(v_cb, v_step), row_spec(o_cb), row_spec(o_cb), row_spec(0)]
    args += [v, o, do, lse]
    if has_kb:
        in_specs += [pl.BlockSpec((tq, q_w), lambda h, i: (i, 0))] * 3
        args += list(rope)
    out_specs = [pl.BlockSpec((tq, q_w), lambda h, i: (i, h))]
    out_shape = [jax.ShapeDtypeStruct((s_q, heads * q_w), BF16)]
    scratch = []
    if has_kb:
        out_specs += [pl.BlockSpec((s_k, 2 * LANES), lambda h, i: (0, h)), key_spec(0, 1)]
        out_shape += [jax.ShapeDtypeStruct((s_k, heads * 2 * LANES), BF16), jax.ShapeDtypeStruct((s_k, heads * LANES), F32)]
        scratch.append(pltpu.VMEM((s_k, 2 * LANES), BF16))
    else:
        out_specs += [key_spec(0, 1), key_spec(0, 1)]
        out_shape += [jax.ShapeDtypeStruct((s_k, heads * LANES), BF16)] * 2
    scratch += [pltpu.VMEM((s_k, q_w), F32), pltpu.VMEM((s_k, LANES), F32)]
    if comm is not None:
        in_specs += [ANY] * len(comm.ins)
        args += comm.ins
        out_specs += [ANY] * len(comm.out_shape)
        out_shape += comm.out_shape
        scratch += comm.scratch()
    res = pl.pallas_call(
        body,
        grid=(heads, nq),
        in_specs=in_specs,
        out_specs=out_specs,
        out_shape=out_shape,
        scratch_shapes=scratch,
        compiler_params=_params("arbitrary", "arbitrary"),
        name=name,
    )(*args)
    return res[0], res[1], res[2], list(res[3:])


def _shift_rows(u, rows):
    t = lax.broadcasted_iota(jnp.int32, u.shape, 0)
    prev = jnp.where(t == 0, 0.0, pltpu.roll(u, 1, 0))
    nxt = jnp.where(t == rows - 1, 0.0, pltpu.roll(u, rows - 1, 0))
    return prev, nxt


def _conv_fwd(z, conv_w, cat, name):
    rows = z.shape[0]
    nblk = CONV_W // LANES

    def body(gb_ref, gc_ref, xin_ref, w_ref, cat_ref, o_ref):
        del cat_ref
        u = gc_ref[...].astype(F32) * xin_ref[...].astype(F32)
        prev, nxt = _shift_rows(u, rows)
        conv = prev * w_ref[0:1, :] + u * w_ref[1:2, :] + nxt * w_ref[2:3, :]
        o_ref[...] = (gb_ref[...].astype(F32) * conv).astype(o_ref.dtype)

    col = lambda c0: pl.BlockSpec((rows, LANES), lambda j: (0, c0 // LANES + j))
    return pl.pallas_call(
        body,
        grid=(nblk,),
        in_specs=[col(Z_GB), col(Z_GC), col(Z_XIN), pl.BlockSpec((3, LANES), lambda j: (0, j)), ANY],
        out_specs=col(MLA_W),
        out_shape=jax.ShapeDtypeStruct(cat.shape, cat.dtype),
        input_output_aliases={4: 0},
        compiler_params=_params("parallel"),
        name=name,
    )(z, z, z, conv_w, cat)


def _conv_bwd(z, conv_w, dcat, name):
    rows = z.shape[0]
    nblk = CONV_W // LANES

    def body(gb_ref, gc_ref, xin_ref, w_ref, dc_ref, dgb_ref, dgc_ref, dxin_ref, dw_ref):
        gc = gc_ref[...].astype(F32)
        xin = xin_ref[...].astype(F32)
        dc = dc_ref[...].astype(F32)
        u = gc * xin
        prev, nxt = _shift_rows(u, rows)
        w0, w1, w2 = w_ref[0:1, :], w_ref[1:2, :], w_ref[2:3, :]
        conv = prev * w0 + u * w1 + nxt * w2
        dgb_ref[...] = (dc * conv).astype(dgb_ref.dtype)
        dconv = dc * gb_ref[...].astype(F32)
        dw_ref[0:1, :] = jnp.sum(dconv * prev, axis=0, keepdims=True)
        dw_ref[1:2, :] = jnp.sum(dconv * u, axis=0, keepdims=True)
        dw_ref[2:3, :] = jnp.sum(dconv * nxt, axis=0, keepdims=True)
        dprev, dnxt = _shift_rows(dconv, rows)
        du = dnxt * w0 + dconv * w1 + dprev * w2
        dgc_ref[...] = (du * xin).astype(dgc_ref.dtype)
        dxin_ref[...] = (du * gc).astype(dxin_ref.dtype)

    col = lambda c0: pl.BlockSpec((rows, LANES), lambda j: (0, c0 // LANES + j))
    w_spec = pl.BlockSpec((3, LANES), lambda j: (0, j))
    piece = jax.ShapeDtypeStruct((rows, CONV_W), BF16)
    return pl.pallas_call(
        body,
        grid=(nblk,),
        in_specs=[col(Z_GB), col(Z_GC), col(Z_XIN), w_spec, col(MLA_W)],
        out_specs=[col(0), col(0), col(0), w_spec],
        out_shape=[piece, piece, piece, jax.ShapeDtypeStruct((3, CONV_W), F32)],
        compiler_params=_params("parallel"),
        name=name,
    )(z, z, z, conv_w, dcat)


def _gate_fwd(cat, z, name):
    rows = cat.shape[0]
    tr = _tile(rows, 256, 16)
    tc = MIX_W
    g0 = Z_GATE // tc

    def body(c_ref, g_ref, y_ref):
        g = g_ref[...].astype(F32)
        y_ref[...] = (c_ref[...].astype(F32) * (g * jax.nn.sigmoid(g))).astype(y_ref.dtype)

    blk = pl.BlockSpec((tr, tc), lambda i, j: (i, j))
    return pl.pallas_call(
        body,
        grid=(rows // tr, MIX_W // tc),
        in_specs=[blk, pl.BlockSpec((tr, tc), lambda i, j: (i, g0 + j))],
        out_specs=blk,
        out_shape=jax.ShapeDtypeStruct((rows, MIX_W), BF16),
        compiler_params=_params("parallel", "parallel"),
        name=name,
    )(cat, z)


def _out_proj_dx_gate_bwd(do, w_o, cat, z, name):
    rows, k = do.shape
    tm = _tile(rows, 512, 16)
    tn = _tile(MIX_W, 1024, LANES)
    g0 = Z_GATE // tn

    def body(do_ref, w_ref, c_ref, g_ref, dcat_ref, dgate_ref):
        dy = lax.dot_general(do_ref[...], w_ref[...], (((1,), (1,)), ((), ())), preferred_element_type=F32)
        g = g_ref[...].astype(F32)
        sg = jax.nn.sigmoid(g)
        dcat_ref[...] = (dy * (g * sg)).astype(dcat_ref.dtype)
        dgate_ref[...] = (dy * c_ref[...].astype(F32) * (sg * (1.0 + g * (1.0 - sg)))).astype(dgate_ref.dtype)

    blk = pl.BlockSpec((tm, tn), lambda i, j: (i, j))
    out = jax.ShapeDtypeStruct((rows, MIX_W), BF16)
    return pl.pallas_call(
        body,
        grid=(rows // tm, MIX_W // tn),
        in_specs=[pl.BlockSpec((tm, k), lambda i, j: (i, 0)), pl.BlockSpec((tn, k), lambda i, j: (j, 0)), blk,
                  pl.BlockSpec((tm, tn), lambda i, j: (i, g0 + j))],
        out_specs=[blk, blk],
        out_shape=[out, out],
        compiler_params=_params("parallel", "parallel"),
        name=name,
    )(do, w_o, cat, z)


def _loss_head(y, target, name):
    rows, width = y.shape
    tr = _tile(rows, 256, 8)

    def body(y_ref, t_ref, g_ref, loss_ref):
        i = pl.program_id(0)
        d = y_ref[...] - t_ref[...]
        g_ref[...] = d / width
        part = 0.5 * jnp.sum(jnp.mean(d * d, axis=-1, keepdims=True), axis=0, keepdims=True)
        part = jnp.broadcast_to(part, loss_ref.shape)

        @pl.when(i == 0)
        def _():
            loss_ref[...] = part

        @pl.when(i > 0)
        def _():
            loss_ref[...] += part

    row_spec = pl.BlockSpec((tr, width), lambda i: (i, 0))
    return pl.pallas_call(
        body,
        grid=(rows // tr,),
        in_specs=[row_spec, row_spec],
        out_specs=[row_spec, pl.BlockSpec((1, LANES), lambda i: (0, 0))],
        out_shape=[jax.ShapeDtypeStruct((rows, width), F32), jax.ShapeDtypeStruct((1, LANES), F32)],
        compiler_params=_params("arbitrary"),
        name=name,
    )(y, target)


CHIP_FLIPS = ((1, 0), (0, 1), (1, 1))
ANY = pl.BlockSpec(memory_space=pl.ANY)


def _chip_copies(pieces, sems, n_slot):
    send_sems, recv_sems, local_sems = sems
    x, y, c = lax.axis_index("x"), lax.axis_index("y"), lax.axis_index("c")
    me = 2 * x + y

    def remote(j, k, a, src, dst):
        fx, fy = CHIP_FLIPS[k]
        return pltpu.make_async_remote_copy(
            src_ref=src, dst_ref=dst, send_sem=send_sems.at[n_slot * k + a], recv_sem=recv_sems.at[n_slot * k + a],
            device_id=((j // 2) ^ fx, (j % 2) ^ fy, c), device_id_type=MESH_ID)

    def peer(j, k):
        fx, fy = CHIP_FLIPS[k]
        return 2 * ((j // 2) ^ fx) + ((j % 2) ^ fy)

    def start_as(j):
        def run():
            for a, (src, dst) in enumerate(pieces(j, j)):
                pltpu.make_async_copy(src, dst, local_sems.at[a]).start()
            for k in range(len(CHIP_FLIPS)):
                for a, (src, dst) in enumerate(pieces(j, peer(j, k))):
                    remote(j, k, a, src, dst).start()
        return run

    def wait_as(j):
        def run():
            for a, (src, dst) in enumerate(pieces(j, j)):
                pltpu.make_async_copy(src, dst, local_sems.at[a]).wait()
            for k in range(len(CHIP_FLIPS)):
                for a, (src, dst) in enumerate(pieces(j, peer(j, k))):
                    remote(j, k, a, src, dst).wait_send()
                for a, (src, dst) in enumerate(pieces(peer(j, k), j)):
                    remote(j, k, a, src, dst).wait_recv()
        return run

    def start():
        for j in range(N_CHIPS):
            pl.when(me == j)(start_as(j))

    def wait():
        for j in range(N_CHIPS):
            pl.when(me == j)(wait_as(j))

    return start, wait


IN_PIECES = ((0, Q_RANK, Z_QLAT), (Q_RANK, KV_RANK, Z_KVLAT), (Q_RANK + KV_RANK, ROPE, Z_KPE),
             (Q_RANK + KV_RANK + ROPE, CONV_W, Z_GB), (Q_RANK + KV_RANK + ROPE + CONV_W, CONV_W, Z_GC),
             (Q_RANK + KV_RANK + ROPE + 2 * CONV_W, CONV_W, Z_XIN), (Q_RANK + KV_RANK + ROPE + 3 * CONV_W, MEM_W, Z_QMEM),
             (Q_RANK + KV_RANK + ROPE + 3 * CONV_W + MEM_W, MIX_W, Z_GATE))
IN_SHARD = IN_COLS // N_CHIPS


def _in_segments(j):
    lo, hi = j * IN_SHARD, (j + 1) * IN_SHARD
    segs = []
    for r0, width, z0 in IN_PIECES:
        a, b = max(lo, r0), min(hi, r0 + width)
        if a < b:
            segs.append((a - lo, z0 + a - r0, b - a))
    return segs


N_SLOT = 11


def _gather_plan(l, shards, zero_rows, part="all"):
    s_in, s_uq, s_ukv, s_conv, s_mk, s_mv, s_o = shards
    ukv_c, mk_r, mk_c, o_r = s_ukv.shape[2], s_mk.shape[1], s_mk.shape[2], s_o.shape[1]
    stack = lambda s: jax.ShapeDtypeStruct((N_CHIPS,) + s.shape[1:], s.dtype)
    in_ins, in_outs = [s_in, zero_rows], [jax.ShapeDtypeStruct((Z_COLS, s_in.shape[2]), s_in.dtype)]
    rest_ins = [s_uq, s_ukv, s_conv, s_mk, s_mv, s_o]
    rest_outs = [stack(s_uq), jax.ShapeDtypeStruct((s_ukv.shape[1], N_CHIPS * ukv_c), s_ukv.dtype), stack(s_conv),
                 jax.ShapeDtypeStruct((N_CHIPS * mk_r, 2 * mk_c), s_mk.dtype),
                 jax.ShapeDtypeStruct((N_CHIPS * o_r, s_o.shape[2]), s_o.dtype)]
    with_in, with_rest = part != "rest", part != "in"

    def build(ins, outs, sems):
        ins, outs = list(ins), list(outs)
        if with_in:
            r_in, r_zero, f_in = ins.pop(0), ins.pop(0), outs.pop(0)
        if with_rest:
            r_uq, r_ukv, r_conv, r_mk, r_mv, r_o = ins
            g_uq, f_ukv, g_conv, f_mkv, f_o = outs

        def pieces(j, t):
            out = []
            if with_in:
                out += [(r_in.at[l, pl.ds(so, n), :], f_in.at[pl.ds(zo, n), :]) for so, zo, n in _in_segments(j)]
            if with_rest:
                out += [(r_uq.at[l], g_uq.at[j]), (r_ukv.at[l], f_ukv.at[:, pl.ds(j * ukv_c, ukv_c)]),
                        (r_conv.at[l], g_conv.at[j]),
                        (r_mk.at[l], f_mkv.at[pl.ds(j * mk_r, mk_r), pl.ds(0, mk_c)]),
                        (r_mv.at[l], f_mkv.at[pl.ds(j * mk_r, mk_r), pl.ds(mk_c, mk_c)]),
                        (r_o.at[l], f_o.at[pl.ds(j * o_r, o_r), :])]
            if with_in and j == t:
                out.append((r_zero, f_in.at[pl.ds(Z_KPE + ROPE, LANES - ROPE), :]))
            return out

        return _chip_copies(pieces, sems, N_SLOT)

    ins = (in_ins if with_in else []) + (rest_ins if with_rest else [])
    outs = (in_outs if with_in else []) + (rest_outs if with_rest else [])
    return _CommPlan(ins, outs, build, len(CHIP_FLIPS) * N_SLOT)


def _scatter_plan(dwt_in, c_uq, dw_ukv, c_conv, dw_mkv, dw_o, part="all"):
    ukv_c, mk_r, mk_c, o_r = dw_ukv.shape[1] // N_CHIPS, dw_mkv.shape[0] // N_CHIPS, dw_mkv.shape[1] // 2, dw_o.shape[0] // N_CHIPS
    with_in, with_rest = part != "rest", part != "in"
    in_outs = [jax.ShapeDtypeStruct((N_CHIPS, IN_SHARD, D_MODEL), BF16)]
    rest_ins = [c_uq, dw_ukv, c_conv, dw_mkv, dw_o]
    rest_outs = [jax.ShapeDtypeStruct(c_uq.shape, c_uq.dtype),
                 jax.ShapeDtypeStruct((N_CHIPS, dw_ukv.shape[0], ukv_c), dw_ukv.dtype),
                 jax.ShapeDtypeStruct(c_conv.shape, c_conv.dtype),
                 jax.ShapeDtypeStruct((N_CHIPS, mk_r, mk_c), dw_mkv.dtype), jax.ShapeDtypeStruct((N_CHIPS, mk_r, mk_c), dw_mkv.dtype),
                 jax.ShapeDtypeStruct((N_CHIPS, o_r, dw_o.shape[1]), dw_o.dtype)]

    def build(ins, outs, sems):
        ins, outs = list(ins), list(outs)
        if with_in:
            r_in, o_in = ins.pop(0), outs.pop(0)
        if with_rest:
            r_uq, r_ukv, r_conv, r_mkv, r_o = ins
            o_uq, o_ukv, o_conv, o_mk, o_mv, o_o = outs

        def pieces(j, t):
            out = []
            if with_in:
                out += [(r_in.at[pl.ds(zo, n), :], o_in.at[j, pl.ds(so, n), :]) for so, zo, n in _in_segments(t)]
            if with_rest:
                out += [(r_uq.at[t], o_uq.at[j]), (r_ukv.at[:, pl.ds(t * ukv_c, ukv_c)], o_ukv.at[j]),
                        (r_conv.at[t], o_conv.at[j]),
                        (r_mkv.at[pl.ds(t * mk_r, mk_r), pl.ds(0, mk_c)], o_mk.at[j]),
                        (r_mkv.at[pl.ds(t * mk_r, mk_r), pl.ds(mk_c, mk_c)], o_mv.at[j]),
                        (r_o.at[pl.ds(t * o_r, o_r), :], o_o.at[j])]
            return out

        return _chip_copies(pieces, sems, N_SLOT)

    ins = ([dwt_in] if with_in else []) + (rest_ins if with_rest else [])
    outs = (in_outs if with_in else []) + (rest_outs if with_rest else [])
    return _CommPlan(ins, outs, build, len(CHIP_FLIPS) * N_SLOT)


HBM = pl.BlockSpec(memory_space=pltpu.HBM)
SEM = pl.BlockSpec(memory_space=pltpu.SEMAPHORE)
SIDE_EFFECT = pltpu.SideEffectType.DATAFLOW_SIDE_EFFECTING


def _comm_start(plan, after, name):
    n_in, n_out, n_after = len(plan.ins), len(plan.out_shape), len(after)
    n_buf = n_in + n_out

    def body(*refs):
        bufs, sems, token = refs[:n_buf], refs[n_buf + n_after:n_buf + n_after + 3], refs[-1]
        start, _ = plan.build(bufs[:n_in], bufs[n_in:], sems)
        start()
        token[...] = jnp.zeros_like(token)

    lands = [lax.empty(s.shape, s.dtype) for s in plan.out_shape]
    args = [pltpu.with_memory_space_constraint(a, pltpu.HBM) for a in list(plan.ins) + lands]
    res = pl.pallas_call(
        body,
        in_specs=[HBM] * n_buf + [ANY] * n_after,
        out_specs=[SEM] * 3 + [HBM] * n_out + [pl.BlockSpec(memory_space=pltpu.VMEM)],
        out_shape=plan.scratch() + [pltpu.HBM(a.shape, a.dtype) for a in lands] + [jax.ShapeDtypeStruct((8, LANES), F32)],
        input_output_aliases={n_in + i: 3 + i for i in range(n_out)},
        compiler_params=pltpu.CompilerParams(has_side_effects=SIDE_EFFECT),
        name=name,
    )(*args, *after)
    return list(res[:3]), list(res[3:3 + n_out]), res[-1]


def _comm_finish(plan, started, after, name):
    sems, lands, _ = started
    n_in, n_out = len(plan.ins), len(plan.out_shape)
    n_buf = n_in + n_out

    def body(*refs):
        bufs_in, sem_refs = refs[:n_buf], refs[n_buf:n_buf + 3]
        _, wait = plan.build(bufs_in[:n_in], bufs_in[n_in:], sem_refs)
        wait()

    sources = [pltpu.with_memory_space_constraint(a, pltpu.HBM) for a in plan.ins]
    res = pl.pallas_call(
        body,
        in_specs=[HBM] * n_buf + [SEM] * 3 + [ANY] * len(after),
        out_specs=[HBM] * n_out,
        out_shape=[pltpu.HBM(b.shape, b.dtype) for b in lands],
        input_output_aliases={n_in + i: i for i in range(n_out)},
        compiler_params=pltpu.CompilerParams(has_side_effects=SIDE_EFFECT),
        name=name,
    )(*sources, *lands, *sems, *after)
    return list(res)


def _comm_call(plan, name):
    n_in, n_out = len(plan.ins), len(plan.out_shape)

    def body(*refs):
        start, wait = plan.build(refs[:n_in], refs[n_in:n_in + n_out], refs[n_in + n_out:])
        start()
        wait()

    return list(pl.pallas_call(
        body,
        in_specs=[ANY] * n_in,
        out_specs=[ANY] * n_out,
        out_shape=plan.out_shape,
        scratch_shapes=plan.scratch(),
        name=name,
    )(*plan.ins))


def _sibling_plan(arrays):
    def build(ins, outs, sems):
        send_sems, recv_sems, _ = sems
        sibling = (lax.axis_index("x"), lax.axis_index("y"), 1 - lax.axis_index("c"))
        copies = [pltpu.make_async_remote_copy(src_ref=src, dst_ref=dst, send_sem=send_sems.at[a], recv_sem=recv_sems.at[a],
                                               device_id=sibling, device_id_type=MESH_ID)
                  for a, (src, dst) in enumerate(zip(ins, outs))]

        def start():
            for cp in copies:
                cp.start()

        def wait():
            for cp in copies:
                cp.wait()

        return start, wait

    return _CommPlan(arrays, [jax.ShapeDtypeStruct(v.shape, v.dtype) for v in arrays], build, len(arrays))


DEVICE_FLIPS = tuple((fx, fy, fc) for fx in (0, 1) for fy in (0, 1) for fc in (0, 1))[1:]


def _gather_all(v, after, name):
    def body(v_ref, after_ref, out_ref, send_sems, recv_sems, local_sem):
        del after_ref
        x, y, c = lax.axis_index("x"), lax.axis_index("y"), lax.axis_index("c")
        me = 4 * x + 2 * y + c
        local = pltpu.make_async_copy(v_ref, out_ref.at[me], local_sem)
        local.start()
        copies = [local]
        for k, (fx, fy, fc) in enumerate(DEVICE_FLIPS):
            cp = pltpu.make_async_remote_copy(
                src_ref=v_ref, dst_ref=out_ref.at[me], send_sem=send_sems.at[k], recv_sem=recv_sems.at[k],
                device_id=((x + fx) % 2, (y + fy) % 2, (c + fc) % 2), device_id_type=MESH_ID)
            cp.start()
            copies.append(cp)
        for cp in copies:
            cp.wait()

    return pl.pallas_call(
        body,
        in_specs=[ANY, ANY],
        out_specs=ANY,
        out_shape=jax.ShapeDtypeStruct((N_DEV,) + v.shape, v.dtype),
        scratch_shapes=[pltpu.SemaphoreType.DMA((N_DEV - 1,)), pltpu.SemaphoreType.DMA((N_DEV - 1,)), pltpu.SemaphoreType.DMA],
        name=name,
    )(v, after)


def _sum_slots(parts, name):
    n, rows, cols = parts.shape
    tr = _tile(rows, 256, 16)

    def body(p_ref, o_ref):
        acc = p_ref[0].astype(F32)
        for k in range(1, n):
            acc = acc + p_ref[k].astype(F32)
        o_ref[...] = acc

    return pl.pallas_call(
        body,
        grid=(rows // tr,),
        in_specs=[pl.BlockSpec((n, tr, cols), lambda i: (0, i, 0))],
        out_specs=pl.BlockSpec((tr, cols), lambda i: (i, 0)),
        out_shape=jax.ShapeDtypeStruct((rows, cols), F32),
        compiler_params=_params("parallel"),
        name=name,
    )(parts)


def _adamw_math(w, g, m, v):
    m_new = ADAM_B1 * m + (1.0 - ADAM_B1) * g
    v_new = ADAM_B2 * v + (1.0 - ADAM_B2) * jnp.square(g)
    m_hat = m_new / (1.0 - ADAM_B1 ** ADAM_STEP)
    v_hat = v_new / (1.0 - ADAM_B2 ** ADAM_STEP)
    return -ADAM_LR * (m_hat / (jnp.sqrt(v_hat) + ADAM_EPS) + ADAM_WD * w), m_new, v_new


def _adamw(w, g, m, v, name):
    rows, cols = w.shape
    tr = _tile(rows, 256, 8)

    def body(w_ref, g_ref, m_ref, v_ref, d_out, m_out, v_out):
        d_out[...], m_out[...], v_out[...] = _adamw_math(w_ref[...], g_ref[...], m_ref[...], v_ref[...])

    blk = pl.BlockSpec((tr, cols), lambda i: (i, 0))
    out = jax.ShapeDtypeStruct((rows, cols), F32)
    return pl.pallas_call(
        body,
        grid=(rows // tr,),
        in_specs=[blk] * 4,
        out_specs=[blk] * 3,
        out_shape=[out] * 3,
        compiler_params=_params("parallel"),
        name=name,
    )(w, g, m, v)


def _adamw_layer(l, w, g_a, g_b, m, v, prev, name):
    depth, rows, cols = w.shape
    tr = _tile(rows, 256, 8)

    def body(w_ref, ga_ref, gb_ref, m_ref, v_ref, *rest):
        g_out, d_out, m_out, v_out = rest[-4:]
        g = ga_ref[...] + gb_ref[...]
        g_out[...] = g
        d_out[...], m_out[...], v_out[...] = _adamw_math(w_ref[...], g, m_ref[...], v_ref[...])

    stacked = pl.BlockSpec((None, tr, cols), lambda i: (l, i, 0))
    flat = pl.BlockSpec((tr, cols), lambda i: (i, 0))
    in_specs = [stacked, flat, flat, stacked, stacked]
    args = [w, g_a, g_b, m, v]
    aliases = {}
    if prev is not None:
        in_specs += [ANY] * 4
        args += list(prev)
        aliases = {5 + k: k for k in range(4)}
    out = jax.ShapeDtypeStruct((depth, rows, cols), F32)
    return pl.pallas_call(
        body,
        grid=(rows // tr,),
        in_specs=in_specs,
        out_specs=[stacked] * 4,
        out_shape=[out] * 4,
        input_output_aliases=aliases,
        compiler_params=_params("parallel"),
        name=name,
    )(*args)


def _cols_from_shards(g):
    _, r, c = g.shape
    return jnp.transpose(g, (1, 0, 2)).reshape(r, N_CHIPS * c)


def _cols_to_shards(full):
    r, c4 = full.shape
    c = c4 // N_CHIPS
    return jnp.transpose(full.reshape(r, N_CHIPS, c), (1, 0, 2))


IN_ORDER = (Q_RANK, KV_RANK, ROPE, CONV_W, CONV_W, CONV_W, MEM_W, MIX_W)


def _w_in_to_z_layout(w_in):
    edges = [0]
    for width in IN_ORDER:
        edges.append(edges[-1] + width)
    q_lat, kv_lat, k_pe, gb, gc, xin, q_mem, gate = [w_in[..., edges[i]:edges[i + 1]] for i in range(8)]
    pad = jnp.zeros(k_pe.shape[:-1] + (LANES - ROPE,), w_in.dtype)
    return jnp.concatenate([gate, q_lat, kv_lat, k_pe, pad, gb, gc, xin, q_mem], axis=-1)


def _w_in_from_z_layout(wz):
    cut = lambda c0, width: wz[..., c0:c0 + width]
    return jnp.concatenate(
        [cut(Z_QLAT, Q_RANK), cut(Z_KVLAT, KV_RANK), cut(Z_KPE, ROPE), cut(Z_GB, CONV_W), cut(Z_GC, CONV_W),
         cut(Z_XIN, CONV_W), cut(Z_QMEM, MEM_W), cut(Z_GATE, MIX_W)], axis=-1)


def _w_uq_pad(w_uq):
    r, _ = w_uq.shape
    w = jnp.pad(w_uq.reshape(r, MLA_HEADS, QK_HEAD), ((0, 0), (0, 0), (0, QPAD - QK_HEAD)))
    return w.reshape(r, MLA_HEADS * QPAD)


def _w_uq_unpad(w):
    r, _ = w.shape
    return w.reshape(r, MLA_HEADS, QPAD)[..., :QK_HEAD].reshape(r, MLA_HEADS * QK_HEAD)


def _rope_tables(positions):
    inv_freq = 1.0 / (ROPE_THETA ** (jnp.arange(0, ROPE, 2, dtype=F32) / ROPE))
    ang = positions.astype(F32)[:, None] * inv_freq
    cos, sin = jnp.cos(ang), jnp.sin(ang)
    s = positions.shape[0]
    zero = jnp.zeros((s, HALF_ROPE), F32)
    pad = jnp.zeros((s, LANES - ROPE), F32)
    kc = jnp.concatenate([cos, cos, pad], axis=-1)
    ka = jnp.concatenate([-sin, zero, pad], axis=-1)
    kb = jnp.concatenate([zero, sin, pad], axis=-1)
    qc = jnp.concatenate([jnp.ones((s, NOPE), F32), kc], axis=-1)
    qa = jnp.concatenate([jnp.zeros((s, NOPE), F32), ka], axis=-1)
    qb = jnp.concatenate([jnp.zeros((s, NOPE), F32), kb], axis=-1)
    return (qc, qa, qb), (kc, ka, kb)


def _layer_weights(gathered):
    return (gathered[0],) + _late_weights(gathered[1:])


def _late_weights(gathered):
    g_uq, w_ukv, g_conv, w_mkv, w_o = gathered
    return (_w_uq_pad(_cols_from_shards(g_uq)), w_ukv, _cols_from_shards(g_conv), w_mkv, w_o)


def _layer_fwd(l, x, mem, wts, gains, tabs, comm, late=None, h=None):
    wt_in = wts[0]
    g_pre, g_q, g_kv, g_mem, g_post = gains
    q_tab, k_tab = tabs
    tag = f"l{l}_"
    if h is None:
        h = _rmsnorm_fwd(x, g_pre, 0, D_MODEL, tag + "pre_norm")
    z = _matmul(h, wt_in, "nt", BF16, tag + "in_proj", tm_cap=1024, tn_cap=1664)
    w_uq, w_ukv, conv_w, w_mkv, w_o = wts[1:] if late is None else late(z)
    wts = (wt_in, w_uq, w_ukv, conv_w, w_mkv, w_o)
    qn = _rmsnorm_fwd(z, g_q, Z_QLAT, Q_RANK, tag + "q_norm")
    kvn = _rmsnorm_fwd(z, g_kv, Z_KVLAT, KV_RANK, tag + "kv_norm")
    q_raw = _matmul(qn, w_uq, "nn", F32, tag + "uq", tm_cap=1024)
    kv = _matmul(kvn, w_ukv, "nn", BF16, tag + "ukv")
    kpe = _rope(z, *k_tab, Z_KPE, LANES, 1, tag + "k_rope")
    cat, a_lse, arrived = _attn_fwd(q_raw, kv, kpe, kv, q_tab, MLA_HEADS, QPAD, 0, 0, 2, 1, 2, QK_HEAD ** -0.5, 512,
                                    tag + "mla_fwd", comm, o_into=(MIX_W, 0, None))
    cat = _conv_fwd(z, conv_w, cat, tag + "conv_fwd")
    mem_n = _rmsnorm_fwd(mem, g_mem, 0, D_MODEL, tag + "mem_norm")
    mkv = _matmul(mem_n, w_mkv, "nn", BF16, tag + "mem_kv")
    cat, m_lse, _ = _attn_fwd(z, mkv, None, mkv, None, MEM_HEADS, LANES, Z_QMEM // LANES, 0, 1, MEM_HEADS, 1,
                              MEM_HEAD ** -0.5, 1024, tag + "mem_fwd", o_into=(MIX_W, (MLA_W + CONV_W) // LANES, cat))
    y = _gate_fwd(cat, z, tag + "gate_fwd")
    o = _matmul(y, w_o, "nn", F32, tag + "out_proj", tm_cap=1024)
    x_new = _post_norm_residual(x, o, g_post, tag + "post_norm")
    saved = (x, h, z, qn, kvn, q_raw, kv, kpe, a_lse, mem_n, mkv, m_lse, cat, y, o)
    return x_new, saved, arrived


def _layer_bwd(l, g, mem, saved, wts, gains, tabs_bwd, comm, split_exchange=False):
    wt_in, w_uq, w_ukv, conv_w, w_mkv, w_o = wts
    g_pre, g_q, g_kv, g_mem, g_post = gains
    q_tab, k_tab_bwd = tabs_bwd
    x, h, z, qn, kvn, q_raw, kv, kpe, a_lse, mem_n, mkv, m_lse, cat, y, o = saved
    tag = f"l{l}_"
    do, dg_post = _rmsnorm_bwd(o, g_post, g, None, 0, D_MODEL, BF16, tag + "post_norm_bwd")
    dcat, dgate = _out_proj_dx_gate_bwd(do, w_o, cat, z, tag + "out_proj_dx")
    dw_o = _matmul(y, do, "tn", BF16, tag + "out_proj_dw", tm_cap=1024)
    dq, dkv, dkpe_h, arrived = _attn_bwd(q_raw, kv, kpe, kv, cat, dcat, a_lse, q_tab, MLA_HEADS, QPAD, 0, 0, 2, 1, 2, 0,
                                         QK_HEAD ** -0.5, 512, tag + "mla_bwd", comm)
    dkpe = _kpe_grad(dkpe_h, *k_tab_bwd, MLA_HEADS, tag + "k_rope_bwd")
    dw_ukv = _matmul(kvn, dkv, "tn", BF16, tag + "ukv_dw")
    dkvn = _matmul(dkv, w_ukv, "nt", F32, tag + "ukv_dx")
    dkv_lat, dg_kv = _rmsnorm_bwd(z, g_kv, dkvn, None, Z_KVLAT, KV_RANK, BF16, tag + "kv_norm_bwd")
    dw_uq = _matmul(qn, dq, "tn", BF16, tag + "uq_dw")
    dqn = _matmul(dq, w_uq, "nt", F32, tag + "uq_dx")
    dq_lat, dg_q = _rmsnorm_bwd(z, g_q, dqn, None, Z_QLAT, Q_RANK, BF16, tag + "q_norm_bwd")
    dgb, dgc, dxin, dconv_w = _conv_bwd(z, conv_w, dcat, tag + "conv_bwd")
    dq_mem, dmk, dmv, _ = _attn_bwd(z, mkv, None, mkv, cat, dcat, m_lse, None, MEM_HEADS, LANES, Z_QMEM // LANES, 0, 1,
                                    MEM_HEADS, 1, (MLA_W + CONV_W) // LANES, MEM_HEAD ** -0.5, 1024, tag + "mem_bwd")
    dmkv = jnp.concatenate([dmk, dmv], axis=-1)
    dw_mkv = _matmul(mem_n, dmkv, "tn", BF16, tag + "mem_kv_dw")
    dmem_n = _matmul(dmkv, w_mkv, "nt", F32, tag + "mem_kv_dx")
    _, dg_mem = _rmsnorm_bwd(mem, g_mem, dmem_n, None, 0, D_MODEL, BF16, tag + "mem_norm_bwd")
    others = (_cols_to_shards(_w_uq_unpad(dw_uq)), dw_ukv, _cols_to_shards(dconv_w), dw_mkv, dw_o)
    early = None
    if split_exchange:
        early_plan = _scatter_plan(None, *others, part="rest")
        early = (early_plan, _comm_start(early_plan, [dmem_n], tag + "exchange_rest_start"))
        g_pre = g_pre + early[1][2][0:1, 0:1]
    dz = jnp.concatenate([dgate, dq_lat, dkv_lat, dkpe, dgb, dgc, dxin, dq_mem], axis=-1)
    dwt_in = _matmul(dz, h, "tn", BF16, tag + "in_proj_dw", tm_cap=1664, tk_cap=2048)
    contrib = _scatter_plan(dwt_in, *others, part="in" if split_exchange else "all")
    late = None
    if split_exchange:
        late = (contrib, _comm_start(contrib, [dwt_in], tag + "exchange_in_start"))
    dh = _matmul(dz, wt_in, "nn", BF16, tag + "in_proj_dx", tm_cap=1024, tk_cap=1664, after=late[1][2] if late else None)
    dx, dg_pre = _rmsnorm_bwd(x, g_pre, dh, g, 0, D_MODEL, F32, tag + "pre_norm_bwd")
    return dx, contrib, (dg_pre, dg_q, dg_kv, dg_mem, dg_post), (early, late)


GAIN_WIDTHS = (D_MODEL, Q_RANK, KV_RANK, D_MODEL, D_MODEL)


def _pack_gains(parts):
    return jnp.concatenate([p.reshape(-1) for p in parts]).reshape(-1, LANES)


def _unpack_gains(packed, depth):
    flat = packed.reshape(-1)
    out, at = [], 0
    for width in GAIN_WIDTHS:
        out.append(flat[at:at + depth * width].reshape(depth, width))
        at += depth * width
    return out


def kernel(x, mem, positions, pre_norm_g, w_in, q_norm_g, w_uq, kv_norm_g, w_ukv, conv_w, mem_norm_g, w_mk, w_mv, w_o, post_norm_g, loss_target, m_pre_norm_g, m_w_in, m_q_norm_g, m_w_uq, m_kv_norm_g, m_w_ukv, m_conv_w, m_mem_norm_g, m_w_mk, m_w_mv, m_w_o, m_post_norm_g, v_pre_norm_g, v_w_in, v_q_norm_g, v_w_uq, v_kv_norm_g, v_w_ukv, v_conv_w, v_mem_norm_g, v_w_mk, v_w_mv, v_w_o, v_post_norm_g):
    depth = w_in.shape[0]
    x0, mem0, target = x[0], mem[0], loss_target[0]
    tabs = _rope_tables(positions[0])
    tabs_bwd = (tabs[0], (tabs[1][0], -tabs[1][1], -tabs[1][2]))

    flip = lambda t: jnp.transpose(t, (0, 2, 1))
    w_in, m_w_in, v_w_in = flip(w_in), flip(m_w_in), flip(v_w_in)
    shards = [w_in.astype(BF16), w_uq.astype(BF16), w_ukv.astype(BF16), conv_w, w_mk.astype(BF16), w_mv.astype(BF16),
              w_o.astype(BF16)]
    zero_rows = lambda: jnp.zeros((LANES - ROPE, D_MODEL), BF16)

    def layer_gains(l):
        return tuple(g[l][None, :] for g in (pre_norm_g, q_norm_g, kv_norm_g, mem_norm_g, post_norm_g))

    wts, saved = [None] * depth, [None] * depth
    plan_in, plan_rest = _gather_plan(0, shards, zero_rows(), "in"), _gather_plan(0, shards, zero_rows(), "rest")
    started_in = _comm_start(plan_in, [positions], "l0_gather_in_start")
    started_rest = _comm_start(plan_rest, [started_in[2]], "l0_gather_rest_start")
    h0 = _rmsnorm_fwd(x0, layer_gains(0)[0], 0, D_MODEL, "l0_pre_norm")
    wts[0] = tuple(_comm_finish(plan_in, started_in, [started_rest[2], h0, tabs[0][0]], "l0_gather_in_wait"))

    next_gather = {}

    def start_next_gather(l, after):
        plan = _gather_plan(l + 1, shards, zero_rows())
        next_gather[l + 1] = (plan, _comm_start(plan, [after], f"l{l + 1}_gather_start"))
        return next_gather[l + 1][1][2][0:1, 0:1]

    def rest_of_layer0(z):
        got = _late_weights(_comm_finish(plan_rest, started_rest, [z], "l0_gather_rest_wait"))
        wts[0] = wts[0] + got
        if depth > 1:
            got = (got[0] + start_next_gather(0, got[4]).astype(BF16),) + got[1:]
        return got

    act = x0
    for l in range(depth):
        gains = layer_gains(l)
        if 0 < l < depth - 1:
            gains = (gains[0] + start_next_gather(l, wts[l][5]),) + gains[1:]
        act, saved[l], _ = _layer_fwd(l, act, mem0, wts[l], gains, tabs, None, rest_of_layer0 if l == 0 else None,
                                      h0 if l == 0 else None)
        if l + 1 < depth:
            plan, started = next_gather[l + 1]
            wts[l + 1] = _layer_weights(_comm_finish(plan, started, [act], f"l{l + 1}_gather_wait"))
    grad, loss_part = _loss_head(act, target, "loss_head")
    loss = lax.psum(loss_part[0, 0], ("x", "y", "c"))

    names = ("w_in", "w_uq", "w_ukv", "conv_w", "w_mk", "w_mv", "w_o")
    w_shards = (w_in, w_uq, w_ukv, conv_w, w_mk, w_mv, w_o)
    m_shards = (m_w_in, m_w_uq, m_w_ukv, m_conv_w, m_w_mk, m_w_mv, m_w_o)
    v_shards = (v_w_in, v_w_uq, v_w_ukv, v_conv_w, v_w_mk, v_w_mv, v_w_o)
    stacked = [None] * len(names)

    def sum_and_send(l, received):
        partial = [_sum_slots(r, f"l{l}_grad_sum_{names[i]}") for i, r in enumerate(received)]
        plan = _sibling_plan(partial)
        return l, partial, plan, _comm_start(plan, [partial[0]], f"l{l}_sibling_start")

    def receive_and_update(state, after):
        l, partial, plan, started = state
        other = _comm_finish(plan, started, [after], f"l{l}_sibling_wait")
        for i, name in enumerate(names):
            stacked[i] = _adamw_layer(l, w_shards[i], partial[i], other[i], m_shards[i], v_shards[i], stacked[i],
                                      f"l{l}_adamw_{name}")

    dgs = [None] * depth
    pending = None
    in_flight = None
    for l in reversed(range(depth)):
        gains = layer_gains(l)
        for token in ([pending[1][2]] if pending else []) + ([in_flight[3][2]] if in_flight else []):
            gains = gains[:4] + (gains[4] + token[0:1, 0:1],)
        grad, contrib, dgs[l], early = _layer_bwd(l, grad, mem0, saved[l], wts[l], gains, tabs_bwd, None, l == 0)
        if in_flight is not None:
            receive_and_update(in_flight, grad)
            in_flight = None
        if pending is not None:
            in_flight = sum_and_send(l + 1, _comm_finish(pending[0], pending[1], [grad], f"l{l + 1}_exchange_wait"))
        if l > 0:
            pending = (contrib, _comm_start(contrib, [grad], f"l{l}_exchange_start"))
    early, late = early
    got_in = _comm_finish(late[0], late[1], [grad], "l0_exchange_in_wait")
    last = sum_and_send(0, got_in + _comm_finish(early[0], early[1], [got_in[0]], "l0_exchange_rest_wait"))
    if in_flight is not None:
        receive_and_update(in_flight, last[1][0])
    receive_and_update(last, stacked[0][0] if depth > 1 else last[1][0])
    grad_x = grad[None]
    results = {name: tuple(stacked[i]) for i, name in enumerate(names)}
    results["w_in"] = tuple(flip(t) for t in results["w_in"])

    gain_names = ("pre_norm_g", "q_norm_g", "kv_norm_g", "mem_norm_g", "post_norm_g")
    dg_packed = _pack_gains([jnp.concatenate([dgs[l][i] for l in range(depth)], axis=0) for i in range(5)])
    dg_total = _sum_slots(_gather_all(dg_packed, last[1][0], "gain_gather"), "gain_sum")
    gain_outs = (dg_total,) + tuple(_adamw(
        _pack_gains((pre_norm_g, q_norm_g, kv_norm_g, mem_norm_g, post_norm_g)), dg_total,
        _pack_gains((m_pre_norm_g, m_q_norm_g, m_kv_norm_g, m_mem_norm_g, m_post_norm_g)),
        _pack_gains((v_pre_norm_g, v_q_norm_g, v_kv_norm_g, v_mem_norm_g, v_post_norm_g)), "adamw_gains"))
    gain_outs = [_unpack_gains(t, depth) for t in gain_outs]
    for i, name in enumerate(gain_names):
        results[name] = tuple(gain_outs[k][i] for k in range(4))

    order = ("pre_norm_g", "w_in", "q_norm_g", "w_uq", "kv_norm_g", "w_ukv", "conv_w", "mem_norm_g", "w_mk", "w_mv", "w_o",
             "post_norm_g")
    out = [loss, grad_x]
    for k in range(4):
        out += [results[name][k] for name in order]
    return tuple(out)
```

```python
import functools

import jax
import jax.numpy as jnp
from jax import lax
from jax.experimental import pallas as pl
from jax.experimental.pallas import tpu as pltpu

F32 = jnp.float32
BF16 = jnp.bfloat16
MESH_ID = pl.DeviceIdType.MESH

D_MODEL = 2048
EPS = 1e-6
LOG2_E = 1.4426950408889634
ROPE_THETA = 10000.0
MLA_HEADS = 8
NOPE = 128
ROPE = 64
HALF_ROPE = ROPE // 2
QK_HEAD = NOPE + ROPE
V_HEAD = 128
Q_RANK = 512
KV_RANK = 256
CONV_W = 512
MEM_HEADS = 4
MEM_HEAD = 128
MEM_W = MEM_HEADS * MEM_HEAD
MLA_W = MLA_HEADS * V_HEAD
MIX_W = MLA_W + CONV_W + MEM_W
IN_COLS = Q_RANK + KV_RANK + ROPE + 3 * CONV_W + MEM_W + MIX_W
N_CHIPS = 4
N_DEV = 8

LANES = 128
VMEM_LIMIT_BYTES = 56 * 1024 * 1024

QPAD = 2 * LANES
Z_GATE = 0
Z_QLAT = Z_GATE + MIX_W
Z_KVLAT = Z_QLAT + Q_RANK
Z_KPE = Z_KVLAT + KV_RANK
Z_GB = Z_KPE + LANES
Z_GC = Z_GB + CONV_W
Z_XIN = Z_GC + CONV_W
Z_QMEM = Z_XIN + CONV_W
Z_COLS = Z_QMEM + MEM_W

ADAM_LR = 0.001
ADAM_B1 = 0.9
ADAM_B2 = 0.999
ADAM_EPS = 1e-08
ADAM_WD = 0.01
ADAM_STEP = 10


def _tile(dim, cap, unit):
    if dim <= cap:
        return dim
    t = (cap // unit) * unit
    while t >= unit:
        if dim % t == 0:
            return t
        t -= unit
    raise ValueError(f"no tile of {dim} under {cap} in units of {unit}")


def _params(*semantics):
    return pltpu.CompilerParams(dimension_semantics=semantics, vmem_limit_bytes=VMEM_LIMIT_BYTES)


def _matmul(a, b, mode, out_dtype, name, tm_cap=512, tn_cap=1024, tk_cap=2048, after=None):
    if mode == "nn":
        (m, k), (k2, n) = a.shape, b.shape
    elif mode == "nt":
        (m, k), (n, k2) = a.shape, b.shape
    else:
        (k, m), (k2, n) = a.shape, b.shape
    assert k == k2, (a.shape, b.shape, mode)
    tm = _tile(m, tm_cap, LANES if mode == "tn" else 16)
    tn = _tile(n, tn_cap, LANES)
    tk = _tile(k, tk_cap, LANES if mode != "tn" else 16)
    nk = k // tk
    if mode == "nn":
        a_spec = pl.BlockSpec((tm, tk), lambda i, j, kk: (i, kk))
        b_spec = pl.BlockSpec((tk, tn), lambda i, j, kk: (kk, j))
        dims = (((1,), (0,)), ((), ()))
    elif mode == "nt":
        a_spec = pl.BlockSpec((tm, tk), lambda i, j, kk: (i, kk))
        b_spec = pl.BlockSpec((tn, tk), lambda i, j, kk: (j, kk))
        dims = (((1,), (1,)), ((), ()))
    else:
        a_spec = pl.BlockSpec((tk, tm), lambda i, j, kk: (kk, i))
        b_spec = pl.BlockSpec((tk, tn), lambda i, j, kk: (kk, j))
        dims = (((0,), (0,)), ((), ()))

    def body(a_ref, b_ref, *rest):
        o_ref, scratch = (rest[1], rest[2:]) if after is not None else (rest[0], rest[1:])
        part = lax.dot_general(a_ref[...].astype(BF16), b_ref[...].astype(BF16), dims, preferred_element_type=F32)
        if nk == 1:
            o_ref[...] = part.astype(o_ref.dtype)
            return
        (acc_ref,) = scratch
        kk = pl.program_id(2)

        @pl.when(kk == 0)
        def _():
            acc_ref[...] = part

        @pl.when(kk > 0)
        def _():
            acc_ref[...] += part

        @pl.when(kk == nk - 1)
        def _():
            o_ref[...] = acc_ref[...].astype(o_ref.dtype)

    return pl.pallas_call(
        body,
        grid=(m // tm, n // tn, nk),
        in_specs=[a_spec, b_spec] + ([] if after is None else [pl.BlockSpec(memory_space=pl.ANY)]),
        out_specs=pl.BlockSpec((tm, tn), lambda i, j, kk: (i, j)),
        out_shape=jax.ShapeDtypeStruct((m, n), out_dtype),
        scratch_shapes=[] if nk == 1 else [pltpu.VMEM((tm, tn), F32)],
        compiler_params=_params("parallel", "parallel", "arbitrary"),
        name=name,
    )(*([a, b] if after is None else [a, b, after]))


def _rmsnorm_fwd(x, gain, col0, width, name):
    rows = x.shape[0]
    tr = _tile(rows, 512, 16)
    cb = col0 // width
    assert cb * width == col0

    def body(x_ref, g_ref, o_ref):
        xv = x_ref[...].astype(F32)
        r = lax.rsqrt(jnp.mean(xv * xv, axis=-1, keepdims=True) + EPS)
        o_ref[...] = (xv * r * g_ref[...]).astype(o_ref.dtype)

    return pl.pallas_call(
        body,
        grid=(rows // tr,),
        in_specs=[pl.BlockSpec((tr, width), lambda i: (i, cb)), pl.BlockSpec((1, width), lambda i: (0, 0))],
        out_specs=pl.BlockSpec((tr, width), lambda i: (i, 0)),
        out_shape=jax.ShapeDtypeStruct((rows, width), BF16),
        compiler_params=_params("parallel"),
        name=name,
    )(x, gain)


def _rmsnorm_bwd(x, gain, dy, resid, col0, width, out_dtype, name):
    rows = x.shape[0]
    tr = _tile(rows, 256, 16)
    cb = col0 // width
    assert cb * width == col0
    has_resid = resid is not None

    def body(*refs):
        if has_resid:
            x_ref, g_ref, dy_ref, res_ref, dx_ref, dg_ref = refs
        else:
            x_ref, g_ref, dy_ref, dx_ref, dg_ref = refs
        i = pl.program_id(0)
        xv = x_ref[...].astype(F32)
        dyv = dy_ref[...].astype(F32)
        r = lax.rsqrt(jnp.mean(xv * xv, axis=-1, keepdims=True) + EPS)
        xr = xv * r
        dyg = dyv * g_ref[...]
        c = jnp.mean(dyg * xr, axis=-1, keepdims=True)
        dx = r * (dyg - xr * c)
        if has_resid:
            dx = dx + res_ref[...]
        dx_ref[...] = dx.astype(dx_ref.dtype)
        part = jnp.sum(dyv * xr, axis=0, keepdims=True)

        @pl.when(i == 0)
        def _():
            dg_ref[...] = part

        @pl.when(i > 0)
        def _():
            dg_ref[...] += part

    row_spec = pl.BlockSpec((tr, width), lambda i: (i, 0))
    in_specs = [pl.BlockSpec((tr, width), lambda i: (i, cb)), pl.BlockSpec((1, width), lambda i: (0, 0)), row_spec]
    args = [x, gain, dy]
    if has_resid:
        in_specs.append(row_spec)
        args.append(resid)
    return pl.pallas_call(
        body,
        grid=(rows // tr,),
        in_specs=in_specs,
        out_specs=[row_spec, pl.BlockSpec((1, width), lambda i: (0, 0))],
        out_shape=[jax.ShapeDtypeStruct((rows, width), out_dtype), jax.ShapeDtypeStruct((1, width), F32)],
        compiler_params=_params("arbitrary"),
        name=name,
    )(*args)


def _post_norm_residual(x, o, gain, name):
    rows, width = x.shape
    tr = _tile(rows, 256, 8)

    def body(x_ref, o_ref, g_ref, out_ref):
        ov = o_ref[...]
        r = lax.rsqrt(jnp.mean(ov * ov, axis=-1, keepdims=True) + EPS)
        out_ref[...] = x_ref[...] + ov * r * g_ref[...]

    row_spec = pl.BlockSpec((tr, width), lambda i: (i, 0))
    return pl.pallas_call(
        body,
        grid=(rows // tr,),
        in_specs=[row_spec, row_spec, pl.BlockSpec((1, width), lambda i: (0, 0))],
        out_specs=row_spec,
        out_shape=jax.ShapeDtypeStruct((rows, width), F32),
        compiler_params=_params("parallel"),
        name=name,
    )(x, o, gain)


def _latent_prep(z, g_q, g_kv, tab_c, tab_a, tab_b, name):
    rows = z.shape[0]
    tr = _tile(rows, 512, 16)

    def norm(x_ref, g_ref, o_ref):
        xv = x_ref[...].astype(F32)
        r = lax.rsqrt(jnp.mean(xv * xv, axis=-1, keepdims=True) + EPS)
        o_ref[...] = (xv * r * g_ref[...]).astype(o_ref.dtype)

    def body(q_ref, kv_ref, k_ref, gq_ref, gkv_ref, c_ref, a_ref, b_ref, qn_ref, kvn_ref, kpe_ref):
        norm(q_ref, gq_ref, qn_ref)
        norm(kv_ref, gkv_ref, kvn_ref)
        kpe_ref[...] = _rope_rows(k_ref[...].astype(F32), c_ref[...], a_ref[...], b_ref[...], 1).astype(kpe_ref.dtype)

    window = lambda c0, width: pl.BlockSpec((tr, width), lambda i: (i, c0 // width))
    gain = lambda width: pl.BlockSpec((1, width), lambda i: (0, 0))
    tab = pl.BlockSpec((tr, LANES), lambda i: (i, 0))
    out = lambda width: pl.BlockSpec((tr, width), lambda i: (i, 0))
    return pl.pallas_call(
        body,
        grid=(rows // tr,),
        in_specs=[window(Z_QLAT, Q_RANK), window(Z_KVLAT, KV_RANK), window(Z_KPE, LANES), gain(Q_RANK), gain(KV_RANK), tab, tab, tab],
        out_specs=[out(Q_RANK), out(KV_RANK), out(LANES)],
        out_shape=[jax.ShapeDtypeStruct((rows, Q_RANK), BF16), jax.ShapeDtypeStruct((rows, KV_RANK), BF16),
                   jax.ShapeDtypeStruct((rows, LANES), BF16)],
        compiler_params=_params("parallel"),
        name=name,
    )(z, z, z, g_q, g_kv, tab_c, tab_a, tab_b)


def _kpe_grad(dkb, tab_c, tab_a, tab_b, heads, name):
    rows = dkb.shape[0]
    tr = _tile(rows, 512, 16)

    def body(d_ref, c_ref, a_ref, b_ref, o_ref):
        acc = d_ref[:, 0:LANES]
        for h in range(1, heads):
            acc = acc + d_ref[:, h * LANES:(h + 1) * LANES]
        up = pltpu.roll(acc, LANES - HALF_ROPE, 1)
        down = pltpu.roll(acc, HALF_ROPE, 1)
        o_ref[...] = (acc * c_ref[...] + up * a_ref[...] + down * b_ref[...]).astype(o_ref.dtype)

    tab_spec = pl.BlockSpec((tr, LANES), lambda i: (i, 0))
    return pl.pallas_call(
        body,
        grid=(rows // tr,),
        in_specs=[pl.BlockSpec((tr, heads * LANES), lambda i: (i, 0)), tab_spec, tab_spec, tab_spec],
        out_specs=tab_spec,
        out_shape=jax.ShapeDtypeStruct((rows, LANES), BF16),
        compiler_params=_params("parallel"),
        name=name,
    )(dkb, tab_c, tab_a, tab_b)


class _CommPlan:
    def __init__(self, ins, out_shape, build, n_copies):
        self.ins, self.out_shape, self.build, self.n_copies = list(ins), list(out_shape), build, n_copies

    def scratch(self):
        n = self.n_copies
        return [pltpu.SemaphoreType.DMA((n,)), pltpu.SemaphoreType.DMA((n,)), pltpu.SemaphoreType.DMA((n,))]


def _split_comm(refs, n_in, n_out, comm):
    if comm is None:
        return refs, None
    ci, co = len(comm.ins), len(comm.out_shape)
    ins, c_ins = refs[:n_in], refs[n_in:n_in + ci]
    outs, c_outs = refs[n_in + ci:n_in + ci + n_out], refs[n_in + ci + n_out:n_in + ci + n_out + co]
    rest = refs[n_in + ci + n_out + co:]
    scratch, sems = rest[:-3], rest[-3:]
    return tuple(ins) + tuple(outs) + tuple(scratch), functools.partial(comm.build, c_ins, c_outs, sems)


def _ride_start(copies, first):
    if copies is not None:
        pl.when(first)(copies()[0])


def _ride_wait(copies, last):
    if copies is not None:
        pl.when(last)(copies()[1])


def _rope_rows(x, c, a, b, sign):
    width = x.shape[-1]
    mixed = pltpu.roll(x, width - HALF_ROPE, 1) * a + pltpu.roll(x, HALF_ROPE, 1) * b
    return x * c + mixed if sign > 0 else x * c - mixed


def _attn_fwd(q, ka, kb, v, rope, heads, q_w, q_cb, ka_cb, ka_step, v_cb, v_step, scale, tq_cap, name, comm=None, tk_cap=512,
              o_into=None):
    s_q, s_k = q.shape[0], ka.shape[0]
    tq = _tile(s_q, tq_cap, 16)
    nq = s_q // tq
    has_kb = kb is not None
    n_in = 7 if has_kb else 3
    tk = _tile(s_k, tk_cap, LANES)
    o_cols, o_cb, o_old = o_into if o_into is not None else (heads * LANES, 0, None)
    assert comm is None or o_old is None

    def body(*refs):
        if o_old is not None:
            refs = refs[:n_in] + refs[n_in + 1:]
        refs, copies = _split_comm(refs, n_in, 2, comm)
        first = jnp.logical_and(pl.program_id(0) == 0, pl.program_id(1) == 0)
        last = jnp.logical_and(pl.program_id(0) == heads - 1, pl.program_id(1) == nq - 1)
        _ride_start(copies, first)
        if has_kb:
            q_ref, ka_ref, kb_ref, v_ref, c_ref, a_ref, b_ref, o_ref, lse_ref, k_scr = refs

            @pl.when(pl.program_id(1) == 0)
            def _():
                k_scr[:, 0:LANES] = ka_ref[...].astype(BF16)
                k_scr[:, LANES:2 * LANES] = kb_ref[...].astype(BF16)

            keys = k_scr
            qv = _rope_rows(q_ref[...], c_ref[...], a_ref[...], b_ref[...], 1).astype(BF16)
        else:
            q_ref, ka_ref, v_ref, o_ref, lse_ref = refs
            keys = ka_ref
            qv = q_ref[...].astype(BF16)
        c2 = scale * LOG2_E
        m = l = o = None
        nk = s_k // tk
        scores = lambda j: lax.dot_general(qv, keys[j * tk:(j + 1) * tk, :].astype(BF16), (((1,), (1,)), ((), ())),
                                           preferred_element_type=F32)
        s_next = scores(0)
        for j in range(nk):
            sj = s_next
            if j + 1 < nk:
                s_next = scores(j + 1)
            mj = jnp.max(sj, axis=-1, keepdims=True)
            m_new = mj if m is None else jnp.maximum(m, mj)
            pj = jnp.exp2((sj - m_new) * c2)
            lj = jnp.sum(pj, axis=-1, keepdims=True)
            oj = jnp.dot(pj.astype(BF16), v_ref[j * tk:(j + 1) * tk, :].astype(BF16), preferred_element_type=F32)
            if m is None:
                l, o = lj, oj
            else:
                alpha = jnp.exp2((m - m_new) * c2)
                l, o = l * alpha + lj, o * alpha + oj
            m = m_new
        o_ref[...] = (o * (1.0 / l)).astype(o_ref.dtype)
        lse_ref[...] = jnp.broadcast_to(m * c2 + jnp.log2(l), lse_ref.shape)
        _ride_wait(copies, last)

    in_specs = [pl.BlockSpec((tq, q_w), lambda h, i: (i, q_cb + h)),
                pl.BlockSpec((s_k, LANES), lambda h, i: (0, ka_cb + ka_step * h))]
    args = [q, ka]
    if has_kb:
        in_specs.append(pl.BlockSpec((s_k, LANES), lambda h, i: (0, 0)))
        args.append(kb)
    in_specs.append(pl.BlockSpec((s_k, LANES), lambda h, i: (0, v_cb + v_step * h)))
    args.append(v)
    if has_kb:
        in_specs += [pl.BlockSpec((tq, q_w), lambda h, i: (i, 0))] * 3
        args += list(rope)
    aliases = {}
    if o_old is not None:
        aliases = {len(args): 0}
        in_specs.append(ANY)
        args.append(o_old)
    out_specs = [pl.BlockSpec((tq, LANES), lambda h, i: (i, o_cb + h)), pl.BlockSpec((tq, LANES), lambda h, i: (i, h))]
    out_shape = [jax.ShapeDtypeStruct((s_q, o_cols), BF16), jax.ShapeDtypeStruct((s_q, heads * LANES), F32)]
    scratch = [pltpu.VMEM((s_k, 2 * LANES), BF16)] if has_kb else []
    if comm is not None:
        in_specs += [ANY] * len(comm.ins)
        args += comm.ins
        out_specs += [ANY] * len(comm.out_shape)
        out_shape += comm.out_shape
        scratch += comm.scratch()
    res = pl.pallas_call(
        body,
        grid=(heads, nq),
        in_specs=in_specs,
        out_specs=out_specs,
        out_shape=out_shape,
        scratch_shapes=scratch,
        input_output_aliases=aliases,
        compiler_params=_params("arbitrary", "arbitrary"),
        name=name,
    )(*args)
    return res[0], res[1], list(res[2:])


def _attn_bwd(q, ka, kb, v, o, do, lse, rope, heads, q_w, q_cb, ka_cb, ka_step, v_cb, v_step, o_cb, scale, tq_cap, name,
              comm=None, tk_cap=512):
    s_q, s_k = q.shape[0], ka.shape[0]
    tq = _tile(s_q, tq_cap, 16)
    nq = s_q // tq
    has_kb = kb is not None
    n_in = 10 if has_kb else 6
    n_out = 3
    tk = _tile(s_k, tk_cap, LANES)

    def body(*refs):
        refs, copies = _split_comm(refs, n_in, n_out, comm)
        first = jnp.logical_and(pl.program_id(0) == 0, pl.program_id(1) == 0)
        last = jnp.logical_and(pl.program_id(0) == heads - 1, pl.program_id(1) == nq - 1)
        _ride_start(copies, first)
        if has_kb:
            (q_ref, ka_ref, kb_ref, v_ref, o_ref, do_ref, lse_ref, c_ref, a_ref, b_ref, dq_ref, dkv_ref, dkb_ref, k_scr, dk_acc,
             dv_acc) = refs
        else:
            q_ref, ka_ref, v_ref, o_ref, do_ref, lse_ref, dq_ref, dka_ref, dv_ref, dk_acc, dv_acc = refs
        i = pl.program_id(1)

        @pl.when(i == 0)
        def _():
            dk_acc[...] = jnp.zeros_like(dk_acc)
            dv_acc[...] = jnp.zeros_like(dv_acc)
            if has_kb:
                k_scr[:, 0:LANES] = ka_ref[...].astype(BF16)
                k_scr[:, LANES:2 * LANES] = kb_ref[...].astype(BF16)

        keys = k_scr if has_kb else ka_ref
        if has_kb:
            qv = _rope_rows(q_ref[...], c_ref[...], a_ref[...], b_ref[...], 1).astype(BF16)
        else:
            qv = q_ref[...].astype(BF16)
        dov = do_ref[...].astype(BF16)
        delta = jnp.sum(dov.astype(F32) * o_ref[...].astype(F32), axis=-1, keepdims=True)
        lse2 = lse_ref[:, 0:1]
        c2 = scale * LOG2_E
        nk = s_k // tk
        rows = lambda j: slice(j * tk, (j + 1) * tk)
        nt = (((1,), (1,)), ((), ()))
        tn = (((0,), (0,)), ((), ()))

        def scores(j):
            return (lax.dot_general(qv, keys[rows(j), :].astype(BF16), nt, preferred_element_type=F32),
                    lax.dot_general(dov, v_ref[rows(j), :].astype(BF16), nt, preferred_element_type=F32))

        nxt = scores(0)
        dq = None
        for j in range(nk):
            sj, dpj = nxt
            if j + 1 < nk:
                nxt = scores(j + 1)
            pj = jnp.exp2(sj * c2 - lse2)
            dsj = (pj * (dpj - delta)).astype(BF16)
            dqj = jnp.dot(dsj, keys[rows(j), :].astype(BF16), preferred_element_type=F32)
            dq = dqj if dq is None else dq + dqj
            dk_acc[rows(j), :] += lax.dot_general(dsj, qv, tn, preferred_element_type=F32)
            dv_acc[rows(j), :] += lax.dot_general(pj.astype(BF16), dov, tn, preferred_element_type=F32)
        dq = dq * scale
        if has_kb:
            dq = _rope_rows(dq, c_ref[...], a_ref[...], b_ref[...], -1)
        dq_ref[...] = dq.astype(dq_ref.dtype)

        @pl.when(i == nq - 1)
        def _():
            if has_kb:
                dkv_ref[:, 0:LANES] = (dk_acc[:, 0:LANES] * scale).astype(dkv_ref.dtype)
                dkv_ref[:, LANES:2 * LANES] = dv_acc[...].astype(dkv_ref.dtype)
                dkb_ref[...] = dk_acc[:, LANES:2 * LANES] * scale
            else:
                dka_ref[...] = (dk_acc[...] * scale).astype(dka_ref.dtype)
                dv_ref[...] = dv_acc[...].astype(dv_ref.dtype)

        _ride_wait(copies, last)

    key_spec = lambda cb, step: pl.BlockSpec((s_k, LANES), lambda h, i: (0, cb + step * h))
    row_spec = lambda cb: pl.BlockSpec((tq, LANES), lambda h, i: (i, cb + h))
    in_specs = [pl.BlockSpec((tq, q_w), lambda h, i: (i, q_cb + h)), key_spec(ka_cb, ka_step)]
    args = [q, ka]
    if has_kb:
        in_specs.append(pl.BlockSpec((s_k, LANES), lambda h, i: (0, 0)))
        args.append(kb)
    in_specs += [key_spec(v_cb, v_step), row_spec(o_cb), row_spec(o_cb), row_spec(0)]
    args += [v, o, do, lse]
    if has_kb:
        in_specs += [pl.BlockSpec((tq, q_w), lambda h, i: (i, 0))] * 3
        args += list(rope)
    out_specs = [pl.BlockSpec((tq, q_w), lambda h, i: (i, h))]
    out_shape = [jax.ShapeDtypeStruct((s_q, heads * q_w), BF16)]
    scratch = []
    if has_kb:
        out_specs += [pl.BlockSpec((s_k, 2 * LANES), lambda h, i: (0, h)), key_spec(0, 1)]
        out_shape += [jax.ShapeDtypeStruct((s_k, heads * 2 * LANES), BF16), jax.ShapeDtypeStruct((s_k, heads * LANES), F32)]
        scratch.append(pltpu.VMEM((s_k, 2 * LANES), BF16))
    else:
        out_specs += [key_spec(0, 1), key_spec(0, 1)]
        out_shape += [jax.ShapeDtypeStruct((s_k, heads * LANES), BF16)] * 2
    scratch += [pltpu.VMEM((s_k, q_w), F32), pltpu.VMEM((s_k, LANES), F32)]
    if comm is not None:
        in_specs += [ANY] * len(comm.ins)
        args += comm.ins
        out_specs += [ANY] * len(comm.out_shape)
        out_shape += comm.out_shape
        scratch += comm.scratch()
    res = pl.pallas_call(
        body,
        grid=(heads, nq),
        in_specs=in_specs,
        out_specs=out_specs,
        out_shape=out_shape,
        scratch_shapes=scratch,
        compiler_params=_params("arbitrary", "arbitrary"),
        name=name,
    )(*args)
    return res[0], res[1], res[2], list(res[3:])


def _shift_rows(u, rows):
    t = lax.broadcasted_iota(jnp.int32, u.shape, 0)
    prev = jnp.where(t == 0, 0.0, pltpu.roll(u, 1, 0))
    nxt = jnp.where(t == rows - 1, 0.0, pltpu.roll(u, rows - 1, 0))
    return prev, nxt


def _conv_fwd(z, conv_w, cat, name):
    rows = z.shape[0]
    nblk = CONV_W // LANES

    def body(gb_ref, gc_ref, xin_ref, w_ref, cat_ref, o_ref):
        del cat_ref
        u = gc_ref[...].astype(F32) * xin_ref[...].astype(F32)
        prev, nxt = _shift_rows(u, rows)
        conv = prev * w_ref[0:1, :] + u * w_ref[1:2, :] + nxt * w_ref[2:3, :]
        o_ref[...] = (gb_ref[...].astype(F32) * conv).astype(o_ref.dtype)

    col = lambda c0: pl.BlockSpec((rows, LANES), lambda j: (0, c0 // LANES + j))
    return pl.pallas_call(
        body,
        grid=(nblk,),
        in_specs=[col(Z_GB), col(Z_GC), col(Z_XIN), pl.BlockSpec((3, LANES), lambda j: (0, j)), ANY],
        out_specs=col(MLA_W),
        out_shape=jax.ShapeDtypeStruct(cat.shape, cat.dtype),
        input_output_aliases={4: 0},
        compiler_params=_params("parallel"),
        name=name,
    )(z, z, z, conv_w, cat)


def _conv_bwd(z, conv_w, dcat, name):
    rows = z.shape[0]
    nblk = CONV_W // LANES

    def body(gb_ref, gc_ref, xin_ref, w_ref, dc_ref, dgb_ref, dgc_ref, dxin_ref, dw_ref):
        gc = gc_ref[...].astype(F32)
        xin = xin_ref[...].astype(F32)
        dc = dc_ref[...].astype(F32)
        u = gc * xin
        prev, nxt = _shift_rows(u, rows)
        w0, w1, w2 = w_ref[0:1, :], w_ref[1:2, :], w_ref[2:3, :]
        conv = prev * w0 + u * w1 + nxt * w2
        dgb_ref[...] = (dc * conv).astype(dgb_ref.dtype)
        dconv = dc * gb_ref[...].astype(F32)
        dw_ref[0:1, :] = jnp.sum(dconv * prev, axis=0, keepdims=True)
        dw_ref[1:2, :] = jnp.sum(dconv * u, axis=0, keepdims=True)
        dw_ref[2:3, :] = jnp.sum(dconv * nxt, axis=0, keepdims=True)
        dprev, dnxt = _shift_rows(dconv, rows)
        du = dnxt * w0 + dconv * w1 + dprev * w2
        dgc_ref[...] = (du * xin).astype(dgc_ref.dtype)
        dxin_ref[...] = (du * gc).astype(dxin_ref.dtype)

    col = lambda c0: pl.BlockSpec((rows, LANES), lambda j: (0, c0 // LANES + j))
    w_spec = pl.BlockSpec((3, LANES), lambda j: (0, j))
    piece = jax.ShapeDtypeStruct((rows, CONV_W), BF16)
    return pl.pallas_call(
        body,
        grid=(nblk,),
        in_specs=[col(Z_GB), col(Z_GC), col(Z_XIN), w_spec, col(MLA_W)],
        out_specs=[col(0), col(0), col(0), w_spec],
        out_shape=[piece, piece, piece, jax.ShapeDtypeStruct((3, CONV_W), F32)],
        compiler_params=_params("parallel"),
        name=name,
    )(z, z, z, conv_w, dcat)


def _gate_fwd(cat, z, name):
    rows = cat.shape[0]
    tr = _tile(rows, 256, 16)
    tc = MIX_W
    g0 = Z_GATE // tc

    def body(c_ref, g_ref, y_ref):
        g = g_ref[...].astype(F32)
        y_ref[...] = (c_ref[...].astype(F32) * (g * jax.nn.sigmoid(g))).astype(y_ref.dtype)

    blk = pl.BlockSpec((tr, tc), lambda i, j: (i, j))
    return pl.pallas_call(
        body,
        grid=(rows // tr, MIX_W // tc),
        in_specs=[blk, pl.BlockSpec((tr, tc), lambda i, j: (i, g0 + j))],
        out_specs=blk,
        out_shape=jax.ShapeDtypeStruct((rows, MIX_W), BF16),
        compiler_params=_params("parallel", "parallel"),
        name=name,
    )(cat, z)


def _out_proj_dx_gate_bwd(do, w_o, cat, z, name):
    rows, k = do.shape
    tm = _tile(rows, 512, 16)
    tn = _tile(MIX_W, 1024, LANES)
    g0 = Z_GATE // tn

    def body(do_ref, w_ref, c_ref, g_ref, dcat_ref, dgate_ref):
        dy = lax.dot_general(do_ref[...], w_ref[...], (((1,), (1,)), ((), ())), preferred_element_type=F32)
        g = g_ref[...].astype(F32)
        sg = jax.nn.sigmoid(g)
        dcat_ref[...] = (dy * (g * sg)).astype(dcat_ref.dtype)
        dgate_ref[...] = (dy * c_ref[...].astype(F32) * (sg * (1.0 + g * (1.0 - sg)))).astype(dgate_ref.dtype)

    blk = pl.BlockSpec((tm, tn), lambda i, j: (i, j))
    out = jax.ShapeDtypeStruct((rows, MIX_W), BF16)
    return pl.pallas_call(
        body,
        grid=(rows // tm, MIX_W // tn),
        in_specs=[pl.BlockSpec((tm, k), lambda i, j: (i, 0)), pl.BlockSpec((tn, k), lambda i, j: (j, 0)), blk,
                  pl.BlockSpec((tm, tn), lambda i, j: (i, g0 + j))],
        out_specs=[blk, pl.BlockSpec((tm, tn), lambda i, j: (i, g0 + j))],
        out_shape=[out, jax.ShapeDtypeStruct((rows, Z_COLS), BF16)],
        compiler_params=_params("parallel", "parallel"),
        name=name,
    )(do, w_o, cat, z)


def _loss_head(y, target, name):
    rows, width = y.shape
    tr = _tile(rows, 256, 8)

    def body(y_ref, t_ref, g_ref, loss_ref):
        i = pl.program_id(0)
        d = y_ref[...] - t_ref[...]
        g_ref[...] = d / width
        part = 0.5 * jnp.sum(jnp.mean(d * d, axis=-1, keepdims=True), axis=0, keepdims=True)
        part = jnp.broadcast_to(part, loss_ref.shape)

        @pl.when(i == 0)
        def _():
            loss_ref[...] = part

        @pl.when(i > 0)
        def _():
            loss_ref[...] += part

    row_spec = pl.BlockSpec((tr, width), lambda i: (i, 0))
    return pl.pallas_call(
        body,
        grid=(rows // tr,),
        in_specs=[row_spec, row_spec],
        out_specs=[row_spec, pl.BlockSpec((1, LANES), lambda i: (0, 0))],
        out_shape=[jax.ShapeDtypeStruct((rows, width), F32), jax.ShapeDtypeStruct((1, LANES), F32)],
        compiler_params=_params("arbitrary"),
        name=name,
    )(y, target)


CHIP_FLIPS = ((1, 0), (0, 1), (1, 1))
ANY = pl.BlockSpec(memory_space=pl.ANY)


def _chip_copies(pieces, sems, n_slot):
    send_sems, recv_sems, local_sems = sems
    x, y, c = lax.axis_index("x"), lax.axis_index("y"), lax.axis_index("c")
    me = 2 * x + y

    def remote(j, k, a, src, dst):
        fx, fy = CHIP_FLIPS[k]
        return pltpu.make_async_remote_copy(
            src_ref=src, dst_ref=dst, send_sem=send_sems.at[n_slot * k + a], recv_sem=recv_sems.at[n_slot * k + a],
            device_id=((j // 2) ^ fx, (j % 2) ^ fy, c), device_id_type=MESH_ID)

    def peer(j, k):
        fx, fy = CHIP_FLIPS[k]
        return 2 * ((j // 2) ^ fx) + ((j % 2) ^ fy)

    def start_as(j):
        def run():
            for a, (src, dst) in enumerate(pieces(j, j)):
                pltpu.make_async_copy(src, dst, local_sems.at[a]).start()
            for k in range(len(CHIP_FLIPS)):
                for a, (src, dst) in enumerate(pieces(j, peer(j, k))):
                    remote(j, k, a, src, dst).start()
        return run

    def wait_as(j):
        def run():
            for a, (src, dst) in enumerate(pieces(j, j)):
                pltpu.make_async_copy(src, dst, local_sems.at[a]).wait()
            for k in range(len(CHIP_FLIPS)):
                for a, (src, dst) in enumerate(pieces(j, peer(j, k))):
                    remote(j, k, a, src, dst).wait_send()
                for a, (src, dst) in enumerate(pieces(peer(j, k), j)):
                    remote(j, k, a, src, dst).wait_recv()
        return run

    def start():
        for j in range(N_CHIPS):
            pl.when(me == j)(start_as(j))

    def wait():
        for j in range(N_CHIPS):
            pl.when(me == j)(wait_as(j))

    return start, wait


IN_PIECES = ((0, Q_RANK, Z_QLAT), (Q_RANK, KV_RANK, Z_KVLAT), (Q_RANK + KV_RANK, ROPE, Z_KPE),
             (Q_RANK + KV_RANK + ROPE, CONV_W, Z_GB), (Q_RANK + KV_RANK + ROPE + CONV_W, CONV_W, Z_GC),
             (Q_RANK + KV_RANK + ROPE + 2 * CONV_W, CONV_W, Z_XIN), (Q_RANK + KV_RANK + ROPE + 3 * CONV_W, MEM_W, Z_QMEM),
             (Q_RANK + KV_RANK + ROPE + 3 * CONV_W + MEM_W, MIX_W, Z_GATE))
IN_SHARD = IN_COLS // N_CHIPS


def _in_segments(j):
    lo, hi = j * IN_SHARD, (j + 1) * IN_SHARD
    segs = []
    for r0, width, z0 in IN_PIECES:
        a, b = max(lo, r0), min(hi, r0 + width)
        if a < b:
            segs.append((a - lo, z0 + a - r0, b - a))
    return segs


N_SLOT = 11


def _gather_plan(l, shards, zero_rows, part="all"):
    s_in, s_uq, s_ukv, s_conv, s_mk, s_mv, s_o = shards
    ukv_c, mk_r, mk_c, o_r = s_ukv.shape[2], s_mk.shape[1], s_mk.shape[2], s_o.shape[1]
    stack = lambda s: jax.ShapeDtypeStruct((N_CHIPS,) + s.shape[1:], s.dtype)
    in_ins, in_outs = [s_in, zero_rows], [jax.ShapeDtypeStruct((Z_COLS, s_in.shape[2]), s_in.dtype)]
    rest_ins = [s_uq, s_ukv, s_conv, s_mk, s_mv, s_o]
    rest_outs = [stack(s_uq), jax.ShapeDtypeStruct((s_ukv.shape[1], N_CHIPS * ukv_c), s_ukv.dtype), stack(s_conv),
                 jax.ShapeDtypeStruct((N_CHIPS * mk_r, 2 * mk_c), s_mk.dtype),
                 jax.ShapeDtypeStruct((N_CHIPS * o_r, s_o.shape[2]), s_o.dtype)]
    with_in, with_rest = part != "rest", part != "in"

    def build(ins, outs, sems):
        ins, outs = list(ins), list(outs)
        if with_in:
            r_in, r_zero, f_in = ins.pop(0), ins.pop(0), outs.pop(0)
        if with_rest:
            r_uq, r_ukv, r_conv, r_mk, r_mv, r_o = ins
            g_uq, f_ukv, g_conv, f_mkv, f_o = outs

        def pieces(j, t):
            out = []
            if with_in:
                out += [(r_in.at[l, pl.ds(so, n), :], f_in.at[pl.ds(zo, n), :]) for so, zo, n in _in_segments(j)]
            if with_rest:
                out += [(r_uq.at[l], g_uq.at[j]), (r_ukv.at[l], f_ukv.at[:, pl.ds(j * ukv_c, ukv_c)]),
                        (r_conv.at[l], g_conv.at[j]),
                        (r_mk.at[l], f_mkv.at[pl.ds(j * mk_r, mk_r), pl.ds(0, mk_c)]),
                        (r_mv.at[l], f_mkv.at[pl.ds(j * mk_r, mk_r), pl.ds(mk_c, mk_c)]),
                        (r_o.at[l], f_o.at[pl.ds(j * o_r, o_r), :])]
            if with_in and j == t:
                out.append((r_zero, f_in.at[pl.ds(Z_KPE + ROPE, LANES - ROPE), :]))
            return out

        return _chip_copies(pieces, sems, N_SLOT)

    ins = (in_ins if with_in else []) + (rest_ins if with_rest else [])
    outs = (in_outs if with_in else []) + (rest_outs if with_rest else [])
    return _CommPlan(ins, outs, build, len(CHIP_FLIPS) * N_SLOT)


def _scatter_plan(dwt_in, c_uq, dw_ukv, c_conv, dw_mkv, dw_o, part="all"):
    ukv_c, mk_r, mk_c, o_r = dw_ukv.shape[1] // N_CHIPS, dw_mkv.shape[0] // N_CHIPS, dw_mkv.shape[1] // 2, dw_o.shape[0] // N_CHIPS
    with_in, with_rest = part != "rest", part != "in"
    in_outs = [jax.ShapeDtypeStruct((N_CHIPS, IN_SHARD, D_MODEL), BF16)]
    rest_ins = [c_uq, dw_ukv, c_conv, dw_mkv, dw_o]
    rest_outs = [jax.ShapeDtypeStruct(c_uq.shape, c_uq.dtype),
                 jax.ShapeDtypeStruct((N_CHIPS, dw_ukv.shape[0], ukv_c), dw_ukv.dtype),
                 jax.ShapeDtypeStruct(c_conv.shape, c_conv.dtype),
                 jax.ShapeDtypeStruct((N_CHIPS, mk_r, mk_c), dw_mkv.dtype), jax.ShapeDtypeStruct((N_CHIPS, mk_r, mk_c), dw_mkv.dtype),
                 jax.ShapeDtypeStruct((N_CHIPS, o_r, dw_o.shape[1]), dw_o.dtype)]

    def build(ins, outs, sems):
        ins, outs = list(ins), list(outs)
        if with_in:
            r_in, o_in = ins.pop(0), outs.pop(0)
        if with_rest:
            r_uq, r_ukv, r_conv, r_mkv, r_o = ins
            o_uq, o_ukv, o_conv, o_mk, o_mv, o_o = outs

        def pieces(j, t):
            out = []
            if with_in:
                out += [(r_in.at[pl.ds(zo, n), :], o_in.at[j, pl.ds(so, n), :]) for so, zo, n in _in_segments(t)]
            if with_rest:
                out += [(r_uq.at[t], o_uq.at[j]), (r_ukv.at[:, pl.ds(t * ukv_c, ukv_c)], o_ukv.at[j]),
                        (r_conv.at[t], o_conv.at[j]),
                        (r_mkv.at[pl.ds(t * mk_r, mk_r), pl.ds(0, mk_c)], o_mk.at[j]),
                        (r_mkv.at[pl.ds(t * mk_r, mk_r), pl.ds(mk_c, mk_c)], o_mv.at[j]),
                        (r_o.at[pl.ds(t * o_r, o_r), :], o_o.at[j])]
            return out

        return _chip_copies(pieces, sems, N_SLOT)

    ins = ([dwt_in] if with_in else []) + (rest_ins if with_rest else [])
    outs = (in_outs if with_in else []) + (rest_outs if with_rest else [])
    return _CommPlan(ins, outs, build, len(CHIP_FLIPS) * N_SLOT)


HBM = pl.BlockSpec(memory_space=pltpu.HBM)
SEM = pl.BlockSpec(memory_space=pltpu.SEMAPHORE)
SIDE_EFFECT = pltpu.SideEffectType.DATAFLOW_SIDE_EFFECTING


def _comm_start(plan, after, name):
    n_in, n_out, n_after = len(plan.ins), len(plan.out_shape), len(after)
    n_buf = n_in + n_out

    def body(*refs):
        bufs, sems, token = refs[:n_buf], refs[n_buf + n_after:n_buf + n_after + 3], refs[-1]
        start, _ = plan.build(bufs[:n_in], bufs[n_in:], sems)
        start()
        token[...] = jnp.zeros_like(token)

    lands = [lax.empty(s.shape, s.dtype) for s in plan.out_shape]
    args = [pltpu.with_memory_space_constraint(a, pltpu.HBM) for a in list(plan.ins) + lands]
    res = pl.pallas_call(
        body,
        in_specs=[HBM] * n_buf + [ANY] * n_after,
        out_specs=[SEM] * 3 + [HBM] * n_out + [pl.BlockSpec(memory_space=pltpu.VMEM)],
        out_shape=plan.scratch() + [pltpu.HBM(a.shape, a.dtype) for a in lands] + [jax.ShapeDtypeStruct((8, LANES), F32)],
        input_output_aliases={n_in + i: 3 + i for i in range(n_out)},
        compiler_params=pltpu.CompilerParams(has_side_effects=SIDE_EFFECT),
        name=name,
    )(*args, *after)
    return list(res[:3]), list(res[3:3 + n_out]), res[-1]


def _comm_finish(plan, started, after, name):
    sems, lands, _ = started
    n_in, n_out = len(plan.ins), len(plan.out_shape)
    n_buf = n_in + n_out

    def body(*refs):
        bufs_in, sem_refs = refs[:n_buf], refs[n_buf:n_buf + 3]
        _, wait = plan.build(bufs_in[:n_in], bufs_in[n_in:], sem_refs)
        wait()

    sources = [pltpu.with_memory_space_constraint(a, pltpu.HBM) for a in plan.ins]
    res = pl.pallas_call(
        body,
        in_specs=[HBM] * n_buf + [SEM] * 3 + [ANY] * len(after),
        out_specs=[HBM] * n_out,
        out_shape=[pltpu.HBM(b.shape, b.dtype) for b in lands],
        input_output_aliases={n_in + i: i for i in range(n_out)},
        compiler_params=pltpu.CompilerParams(has_side_effects=SIDE_EFFECT),
        name=name,
    )(*sources, *lands, *sems, *after)
    return list(res)


def _comm_call(plan, name):
    n_in, n_out = len(plan.ins), len(plan.out_shape)

    def body(*refs):
        start, wait = plan.build(refs[:n_in], refs[n_in:n_in + n_out], refs[n_in + n_out:])
        start()
        wait()

    return list(pl.pallas_call(
        body,
        in_specs=[ANY] * n_in,
        out_specs=[ANY] * n_out,
        out_shape=plan.out_shape,
        scratch_shapes=plan.scratch(),
        name=name,
    )(*plan.ins))


def _sibling_plan(arrays):
    def build(ins, outs, sems):
        send_sems, recv_sems, _ = sems
        sibling = (lax.axis_index("x"), lax.axis_index("y"), 1 - lax.axis_index("c"))
        copies = [pltpu.make_async_remote_copy(src_ref=src, dst_ref=dst, send_sem=send_sems.at[a], recv_sem=recv_sems.at[a],
                                               device_id=sibling, device_id_type=MESH_ID)
                  for a, (src, dst) in enumerate(zip(ins, outs))]

        def start():
            for cp in copies:
                cp.start()

        def wait():
            for cp in copies:
                cp.wait()

        return start, wait

    return _CommPlan(arrays, [jax.ShapeDtypeStruct(v.shape, v.dtype) for v in arrays], build, len(arrays))


DEVICE_FLIPS = tuple((fx, fy, fc) for fx in (0, 1) for fy in (0, 1) for fc in (0, 1))[1:]


def _gather_all(v, after, name):
    def body(v_ref, after_ref, out_ref, send_sems, recv_sems, local_sem):
        del after_ref
        x, y, c = lax.axis_index("x"), lax.axis_index("y"), lax.axis_index("c")
        me = 4 * x + 2 * y + c
        local = pltpu.make_async_copy(v_ref, out_ref.at[me], local_sem)
        local.start()
        copies = [local]
        for k, (fx, fy, fc) in enumerate(DEVICE_FLIPS):
            cp = pltpu.make_async_remote_copy(
                src_ref=v_ref, dst_ref=out_ref.at[me], send_sem=send_sems.at[k], recv_sem=recv_sems.at[k],
                device_id=((x + fx) % 2, (y + fy) % 2, (c + fc) % 2), device_id_type=MESH_ID)
            cp.start()
            copies.append(cp)
        for cp in copies:
            cp.wait()

    return pl.pallas_call(
        body,
        in_specs=[ANY, ANY],
        out_specs=ANY,
        out_shape=jax.ShapeDtypeStruct((N_DEV,) + v.shape, v.dtype),
        scratch_shapes=[pltpu.SemaphoreType.DMA((N_DEV - 1,)), pltpu.SemaphoreType.DMA((N_DEV - 1,)), pltpu.SemaphoreType.DMA],
        name=name,
    )(v, after)


def _sum_slots(parts, name):
    n, rows, cols = parts.shape
    tr = _tile(rows, 256, 16)

    def body(p_ref, o_ref):
        acc = p_ref[0].astype(F32)
        for k in range(1, n):
            acc = acc + p_ref[k].astype(F32)
        o_ref[...] = acc

    return pl.pallas_call(
        body,
        grid=(rows // tr,),
        in_specs=[pl.BlockSpec((n, tr, cols), lambda i: (0, i, 0))],
        out_specs=pl.BlockSpec((tr, cols), lambda i: (i, 0)),
        out_shape=jax.ShapeDtypeStruct((rows, cols), F32),
        compiler_params=_params("parallel"),
        name=name,
    )(parts)


def _adamw_math(w, g, m, v):
    m_new = ADAM_B1 * m + (1.0 - ADAM_B1) * g
    v_new = ADAM_B2 * v + (1.0 - ADAM_B2) * jnp.square(g)
    m_hat = m_new / (1.0 - ADAM_B1 ** ADAM_STEP)
    v_hat = v_new / (1.0 - ADAM_B2 ** ADAM_STEP)
    return -ADAM_LR * (m_hat / (jnp.sqrt(v_hat) + ADAM_EPS) + ADAM_WD * w), m_new, v_new


def _adamw(w, g, m, v, name):
    rows, cols = w.shape
    tr = _tile(rows, 256, 8)

    def body(w_ref, g_ref, m_ref, v_ref, d_out, m_out, v_out):
        d_out[...], m_out[...], v_out[...] = _adamw_math(w_ref[...], g_ref[...], m_ref[...], v_ref[...])

    blk = pl.BlockSpec((tr, cols), lambda i: (i, 0))
    out = jax.ShapeDtypeStruct((rows, cols), F32)
    return pl.pallas_call(
        body,
        grid=(rows // tr,),
        in_specs=[blk] * 4,
        out_specs=[blk] * 3,
        out_shape=[out] * 3,
        compiler_params=_params("parallel"),
        name=name,
    )(w, g, m, v)


def _adamw_layer(l, w, g_a, g_b, m, v, prev, name):
    depth, rows, cols = w.shape
    tr = _tile(rows, 256, 8)

    def body(w_ref, ga_ref, gb_ref, m_ref, v_ref, *rest):
        g_out, d_out, m_out, v_out = rest[-4:]
        g = ga_ref[...] + gb_ref[...]
        g_out[...] = g
        d_out[...], m_out[...], v_out[...] = _adamw_math(w_ref[...], g, m_ref[...], v_ref[...])

    stacked = pl.BlockSpec((None, tr, cols), lambda i: (l, i, 0))
    flat = pl.BlockSpec((tr, cols), lambda i: (i, 0))
    in_specs = [stacked, flat, flat, stacked, stacked]
    args = [w, g_a, g_b, m, v]
    aliases = {}
    if prev is not None:
        in_specs += [ANY] * 4
        args += list(prev)
        aliases = {5 + k: k for k in range(4)}
    out = jax.ShapeDtypeStruct((depth, rows, cols), F32)
    return pl.pallas_call(
        body,
        grid=(rows // tr,),
        in_specs=in_specs,
        out_specs=[stacked] * 4,
        out_shape=[out] * 4,
        input_output_aliases=aliases,
        compiler_params=_params("parallel"),
        name=name,
    )(*args)


def _cols_from_shards(g):
    _, r, c = g.shape
    return jnp.transpose(g, (1, 0, 2)).reshape(r, N_CHIPS * c)


def _cols_to_shards(full):
    r, c4 = full.shape
    c = c4 // N_CHIPS
    return jnp.transpose(full.reshape(r, N_CHIPS, c), (1, 0, 2))


IN_ORDER = (Q_RANK, KV_RANK, ROPE, CONV_W, CONV_W, CONV_W, MEM_W, MIX_W)


def _w_in_to_z_layout(w_in):
    edges = [0]
    for width in IN_ORDER:
        edges.append(edges[-1] + width)
    q_lat, kv_lat, k_pe, gb, gc, xin, q_mem, gate = [w_in[..., edges[i]:edges[i + 1]] for i in range(8)]
    pad = jnp.zeros(k_pe.shape[:-1] + (LANES - ROPE,), w_in.dtype)
    return jnp.concatenate([gate, q_lat, kv_lat, k_pe, pad, gb, gc, xin, q_mem], axis=-1)


def _w_in_from_z_layout(wz):
    cut = lambda c0, width: wz[..., c0:c0 + width]
    return jnp.concatenate(
        [cut(Z_QLAT, Q_RANK), cut(Z_KVLAT, KV_RANK), cut(Z_KPE, ROPE), cut(Z_GB, CONV_W), cut(Z_GC, CONV_W),
         cut(Z_XIN, CONV_W), cut(Z_QMEM, MEM_W), cut(Z_GATE, MIX_W)], axis=-1)


def _w_uq_pad(w_uq):
    r, _ = w_uq.shape
    w = jnp.pad(w_uq.reshape(r, MLA_HEADS, QK_HEAD), ((0, 0), (0, 0), (0, QPAD - QK_HEAD)))
    return w.reshape(r, MLA_HEADS * QPAD)


def _w_uq_unpad(w):
    r, _ = w.shape
    return w.reshape(r, MLA_HEADS, QPAD)[..., :QK_HEAD].reshape(r, MLA_HEADS * QK_HEAD)


def _rope_tables(positions):
    inv_freq = 1.0 / (ROPE_THETA ** (jnp.arange(0, ROPE, 2, dtype=F32) / ROPE))
    ang = positions.astype(F32)[:, None] * inv_freq
    cos, sin = jnp.cos(ang), jnp.sin(ang)
    s = positions.shape[0]
    zero = jnp.zeros((s, HALF_ROPE), F32)
    pad = jnp.zeros((s, LANES - ROPE), F32)
    kc = jnp.concatenate([cos, cos, pad], axis=-1)
    ka = jnp.concatenate([-sin, zero, pad], axis=-1)
    kb = jnp.concatenate([zero, sin, pad], axis=-1)
    qc = jnp.concatenate([jnp.ones((s, NOPE), F32), kc], axis=-1)
    qa = jnp.concatenate([jnp.zeros((s, NOPE), F32), ka], axis=-1)
    qb = jnp.concatenate([jnp.zeros((s, NOPE), F32), kb], axis=-1)
    return (qc, qa, qb), (kc, ka, kb)


def _layer_weights(gathered):
    return (gathered[0],) + _late_weights(gathered[1:])


def _late_weights(gathered):
    g_uq, w_ukv, g_conv, w_mkv, w_o = gathered
    return (_w_uq_pad(_cols_from_shards(g_uq)), w_ukv, _cols_from_shards(g_conv), w_mkv, w_o)


def _layer_fwd(l, x, mem, wts, gains, tabs, comm, late=None, h=None):
    wt_in = wts[0]
    g_pre, g_q, g_kv, g_mem, g_post = gains
    q_tab, k_tab = tabs
    tag = f"l{l}_"
    if h is None:
        h = _rmsnorm_fwd(x, g_pre, 0, D_MODEL, tag + "pre_norm")
    z = _matmul(h, wt_in, "nt", BF16, tag + "in_proj", tm_cap=1024, tn_cap=1664)
    w_uq, w_ukv, conv_w, w_mkv, w_o = wts[1:] if late is None else late(z)
    wts = (wt_in, w_uq, w_ukv, conv_w, w_mkv, w_o)
    qn, kvn, kpe = _latent_prep(z, g_q, g_kv, *k_tab, tag + "latent_prep")
    q_raw = _matmul(qn, w_uq, "nn", F32, tag + "uq", tm_cap=1024)
    kv = _matmul(kvn, w_ukv, "nn", BF16, tag + "ukv")
    cat, a_lse, arrived = _attn_fwd(q_raw, kv, kpe, kv, q_tab, MLA_HEADS, QPAD, 0, 0, 2, 1, 2, QK_HEAD ** -0.5, 512,
                                    tag + "mla_fwd", comm, o_into=(MIX_W, 0, None))
    cat = _conv_fwd(z, conv_w, cat, tag + "conv_fwd")
    mem_n = _rmsnorm_fwd(mem, g_mem, 0, D_MODEL, tag + "mem_norm")
    mkv = _matmul(mem_n, w_mkv, "nn", BF16, tag + "mem_kv")
    cat, m_lse, _ = _attn_fwd(z, mkv, None, mkv, None, MEM_HEADS, LANES, Z_QMEM // LANES, 0, 1, MEM_HEADS, 1,
                              MEM_HEAD ** -0.5, 1024, tag + "mem_fwd", o_into=(MIX_W, (MLA_W + CONV_W) // LANES, cat))
    y = _gate_fwd(cat, z, tag + "gate_fwd")
    o = _matmul(y, w_o, "nn", F32, tag + "out_proj", tm_cap=1024)
    x_new = _post_norm_residual(x, o, g_post, tag + "post_norm")
    saved = (x, h, z, qn, kvn, q_raw, kv, kpe, a_lse, mem_n, mkv, m_lse, cat, y, o)
    return x_new, saved, arrived


def _layer_bwd(l, g, mem, saved, wts, gains, tabs_bwd, comm, split_exchange=False):
    wt_in, w_uq, w_ukv, conv_w, w_mkv, w_o = wts
    g_pre, g_q, g_kv, g_mem, g_post = gains
    q_tab, k_tab_bwd = tabs_bwd
    x, h, z, qn, kvn, q_raw, kv, kpe, a_lse, mem_n, mkv, m_lse, cat, y, o = saved
    tag = f"l{l}_"
    do, dg_post = _rmsnorm_bwd(o, g_post, g, None, 0, D_MODEL, BF16, tag + "post_norm_bwd")
    dcat, dgate = _out_proj_dx_gate_bwd(do, w_o, cat, z, tag + "out_proj_dx")
    dw_o = _matmul(y, do, "tn", BF16, tag + "out_proj_dw", tm_cap=1024)
    dq, dkv, dkpe_h, arrived = _attn_bwd(q_raw, kv, kpe, kv, cat, dcat, a_lse, q_tab, MLA_HEADS, QPAD, 0, 0, 2, 1, 2, 0,
                                         QK_HEAD ** -0.5, 512, tag + "mla_bwd", comm)
    dkpe = _kpe_grad(dkpe_h, *k_tab_bwd, MLA_HEADS, tag + "k_rope_bwd")
    dw_ukv = _matmul(kvn, dkv, "tn", BF16, tag + "ukv_dw")
    dkvn = _matmul(dkv, w_ukv, "nt", F32, tag + "ukv_dx")
    dkv_lat, dg_kv = _rmsnorm_bwd(z, g_kv, dkvn, None, Z_KVLAT, KV_RANK, BF16, tag + "kv_norm_bwd")
    dw_uq = _matmul(qn, dq, "tn", BF16, tag + "uq_dw")
    dqn = _matmul(dq, w_uq, "nt", F32, tag + "uq_dx")
    dq_lat, dg_q = _rmsnorm_bwd(z, g_q, dqn, None, Z_QLAT, Q_RANK, BF16, tag + "q_norm_bwd")
    dgb, dgc, dxin, dconv_w = _conv_bwd(z, conv_w, dcat, tag + "conv_bwd")
    dq_mem, dmk, dmv, _ = _attn_bwd(z, mkv, None, mkv, cat, dcat, m_lse, None, MEM_HEADS, LANES, Z_QMEM // LANES, 0, 1,
                                    MEM_HEADS, 1, (MLA_W + CONV_W) // LANES, MEM_HEAD ** -0.5, 1024, tag + "mem_bwd")
    dmkv = jnp.concatenate([dmk, dmv], axis=-1)
    dw_mkv = _matmul(mem_n, dmkv, "tn", BF16, tag + "mem_kv_dw")
    dmem_n = _matmul(dmkv, w_mkv, "nt", F32, tag + "mem_kv_dx")
    _, dg_mem = _rmsnorm_bwd(mem, g_mem, dmem_n, None, 0, D_MODEL, BF16, tag + "mem_norm_bwd")
    others = (_cols_to_shards(_w_uq_unpad(dw_uq)), dw_ukv, _cols_to_shards(dconv_w), dw_mkv, dw_o)
    early = None
    if split_exchange:
        early_plan = _scatter_plan(None, *others, part="rest")
        early = (early_plan, _comm_start(early_plan, [dmem_n], tag + "exchange_rest_start"))
        g_pre = g_pre + early[1][2][0:1, 0:1]
    dz = lax.dynamic_update_slice(dgate, jnp.concatenate([dq_lat, dkv_lat, dkpe, dgb, dgc, dxin, dq_mem], axis=-1), (0, Z_QLAT))
    dwt_in = _matmul(dz, h, "tn", BF16, tag + "in_proj_dw", tm_cap=1664, tk_cap=2048)
    contrib = _scatter_plan(dwt_in, *others, part="in" if split_exchange else "all")
    late = None
    if split_exchange:
        late = (contrib, _comm_start(contrib, [dwt_in], tag + "exchange_in_start"))
    dh = _matmul(dz, wt_in, "nn", BF16, tag + "in_proj_dx", tm_cap=1024, tk_cap=1664, after=late[1][2] if late else None)
    dx, dg_pre = _rmsnorm_bwd(x, g_pre, dh, g, 0, D_MODEL, F32, tag + "pre_norm_bwd")
    return dx, contrib, (dg_pre, dg_q, dg_kv, dg_mem, dg_post), (early, late)


GAIN_WIDTHS = (D_MODEL, Q_RANK, KV_RANK, D_MODEL, D_MODEL)


def _pack_gains(parts):
    return jnp.concatenate([p.reshape(-1) for p in parts]).reshape(-1, LANES)


def _unpack_gains(packed, depth):
    flat = packed.reshape(-1)
    out, at = [], 0
    for width in GAIN_WIDTHS:
        out.append(flat[at:at + depth * width].reshape(depth, width))
        at += depth * width
    return out


def kernel(x, mem, positions, pre_norm_g, w_in, q_norm_g, w_uq, kv_norm_g, w_ukv, conv_w, mem_norm_g, w_mk, w_mv, w_o, post_norm_g, loss_target, m_pre_norm_g, m_w_in, m_q_norm_g, m_w_uq, m_kv_norm_g, m_w_ukv, m_conv_w, m_mem_norm_g, m_w_mk, m_w_mv, m_w_o, m_post_norm_g, v_pre_norm_g, v_w_in, v_q_norm_g, v_w_uq, v_kv_norm_g, v_w_ukv, v_conv_w, v_mem_norm_g, v_w_mk, v_w_mv, v_w_o, v_post_norm_g):
    depth = w_in.shape[0]
    x0, mem0, target = x[0], mem[0], loss_target[0]
    tabs = _rope_tables(positions[0])
    tabs_bwd = (tabs[0], (tabs[1][0], -tabs[1][1], -tabs[1][2]))

    flip = lambda t: jnp.transpose(t, (0, 2, 1))
    w_in, m_w_in, v_w_in = flip(w_in), flip(m_w_in), flip(v_w_in)
    shards = [w_in.astype(BF16), w_uq.astype(BF16), w_ukv.astype(BF16), conv_w, w_mk.astype(BF16), w_mv.astype(BF16),
              w_o.astype(BF16)]
    zero_rows = lambda: jnp.zeros((LANES - ROPE, D_MODEL), BF16)

    def layer_gains(l):
        return tuple(g[l][None, :] for g in (pre_norm_g, q_norm_g, kv_norm_g, mem_norm_g, post_norm_g))

    wts, saved = [None] * depth, [None] * depth
    plan_in, plan_rest = _gather_plan(0, shards, zero_rows(), "in"), _gather_plan(0, shards, zero_rows(), "rest")
    started_in = _comm_start(plan_in, [positions], "l0_gather_in_start")
    started_rest = _comm_start(plan_rest, [started_in[2]], "l0_gather_rest_start")
    h0 = _rmsnorm_fwd(x0, layer_gains(0)[0], 0, D_MODEL, "l0_pre_norm")
    wts[0] = tuple(_comm_finish(plan_in, started_in, [started_rest[2], h0, tabs[0][0]], "l0_gather_in_wait"))

    next_gather = {}

    def start_next_gather(l, after):
        plan = _gather_plan(l + 1, shards, zero_rows())
        next_gather[l + 1] = (plan, _comm_start(plan, [after], f"l{l + 1}_gather_start"))
        return next_gather[l + 1][1][2][0:1, 0:1]

    def rest_of_layer0(z):
        got = _late_weights(_comm_finish(plan_rest, started_rest, [z], "l0_gather_rest_wait"))
        wts[0] = wts[0] + got
        if depth > 1:
            got = (got[0] + start_next_gather(0, got[4]).astype(BF16),) + got[1:]
        return got

    act = x0
    for l in range(depth):
        gains = layer_gains(l)
        if 0 < l < depth - 1:
            gains = (gains[0] + start_next_gather(l, wts[l][5]),) + gains[1:]
        act, saved[l], _ = _layer_fwd(l, act, mem0, wts[l], gains, tabs, None, rest_of_layer0 if l == 0 else None,
                                      h0 if l == 0 else None)
        if l + 1 < depth:
            plan, started = next_gather[l + 1]
            wts[l + 1] = _layer_weights(_comm_finish(plan, started, [act], f"l{l + 1}_gather_wait"))
    grad, loss_part = _loss_head(act, target, "loss_head")
    loss = lax.psum(loss_part[0, 0], ("x", "y", "c"))

    names = ("w_in", "w_uq", "w_ukv", "conv_w", "w_mk", "w_mv", "w_o")
    w_shards = (w_in, w_uq, w_ukv, conv_w, w_mk, w_mv, w_o)
    m_shards = (m_w_in, m_w_uq, m_w_ukv, m_conv_w, m_w_mk, m_w_mv, m_w_o)
    v_shards = (v_w_in, v_w_uq, v_w_ukv, v_conv_w, v_w_mk, v_w_mv, v_w_o)
    stacked = [None] * len(names)

    def sum_and_send(l, received):
        partial = [_sum_slots(r, f"l{l}_grad_sum_{names[i]}") for i, r in enumerate(received)]
        plan = _sibling_plan(partial)
        return l, partial, plan, _comm_start(plan, [partial[0]], f"l{l}_sibling_start")

    def receive_and_update(state, after):
        l, partial, plan, started = state
        other = _comm_finish(plan, started, [after], f"l{l}_sibling_wait")
        for i, name in enumerate(names):
            stacked[i] = _adamw_layer(l, w_shards[i], partial[i], other[i], m_shards[i], v_shards[i], stacked[i],
                                      f"l{l}_adamw_{name}")

    dgs = [None] * depth
    pending = None
    in_flight = None
    for l in reversed(range(depth)):
        gains = layer_gains(l)
        for token in ([pending[1][2]] if pending else []) + ([in_flight[3][2]] if in_flight else []):
            gains = gains[:4] + (gains[4] + token[0:1, 0:1],)
        grad, contrib, dgs[l], early = _layer_bwd(l, grad, mem0, saved[l], wts[l], gains, tabs_bwd, None, l == 0)
        if in_flight is not None:
            receive_and_update(in_flight, grad)
            in_flight = None
        if pending is not None:
            in_flight = sum_and_send(l + 1, _comm_finish(pending[0], pending[1], [grad], f"l{l + 1}_exchange_wait"))
        if l > 0:
            pending = (contrib, _comm_start(contrib, [grad], f"l{l}_exchange_start"))
    early, late = early
    got_in = _comm_finish(late[0], late[1], [grad], "l0_exchange_in_wait")
    last = sum_and_send(0, got_in + _comm_finish(early[0], early[1], [got_in[0]], "l0_exchange_rest_wait"))
    if in_flight is not None:
        receive_and_update(in_flight, last[1][0])
    receive_and_update(last, stacked[0][0] if depth > 1 else last[1][0])
    grad_x = grad[None]
    results = {name: tuple(stacked[i]) for i, name in enumerate(names)}
    results["w_in"] = tuple(flip(t) for t in results["w_in"])

    gain_names = ("pre_norm_g", "q_norm_g", "kv_norm_g", "mem_norm_g", "post_norm_g")
    dg_packed = _pack_gains([jnp.concatenate([dgs[l][i] for l in range(depth)], axis=0) for i in range(5)])
    dg_total = _sum_slots(_gather_all(dg_packed, stacked[0][0], "gain_gather"), "gain_sum")
    gain_outs = (dg_total,) + tuple(_adamw(
        _pack_gains((pre_norm_g, q_norm_g, kv_norm_g, mem_norm_g, post_norm_g)), dg_total,
        _pack_gains((m_pre_norm_g, m_q_norm_g, m_kv_norm_g, m_mem_norm_g, m_post_norm_g)),
        _pack_gains((v_pre_norm_g, v_q_norm_g, v_kv_norm_g, v_mem_norm_g, v_post_norm_g)), "adamw_gains"))
    gain_outs = [_unpack_gains(t, depth) for t in gain_outs]
    for i, name in enumerate(gain_names):
        results[name] = tuple(gain_outs[k][i] for k in range(4))

    order = ("pre_norm_g", "w_in", "q_norm_g", "w_uq", "kv_norm_g", "w_ukv", "conv_w", "mem_norm_g", "w_mk", "w_mv", "w_o",
             "post_norm_g")
    out = [loss, grad_x]
    for k in range(4):
        out += [results[name][k] for name in order]
    return tuple(out)
```

```python
import functools

import jax
import jax.numpy as jnp
from jax import lax
from jax.experimental import pallas as pl
from jax.experimental.pallas import tpu as pltpu

F32 = jnp.float32
BF16 = jnp.bfloat16
MESH_ID = pl.DeviceIdType.MESH

D_MODEL = 2048
EPS = 1e-6
LOG2_E = 1.4426950408889634
ROPE_THETA = 10000.0
MLA_HEADS = 8
NOPE = 128
ROPE = 64
HALF_ROPE = ROPE // 2
QK_HEAD = NOPE + ROPE
V_HEAD = 128
Q_RANK = 512
KV_RANK = 256
CONV_W = 512
MEM_HEADS = 4
MEM_HEAD = 128
MEM_W = MEM_HEADS * MEM_HEAD
MLA_W = MLA_HEADS * V_HEAD
MIX_W = MLA_W + CONV_W + MEM_W
IN_COLS = Q_RANK + KV_RANK + ROPE + 3 * CONV_W + MEM_W + MIX_W
N_CHIPS = 4
N_DEV = 8

LANES = 128
VMEM_LIMIT_BYTES = 56 * 1024 * 1024

QPAD = 2 * LANES
Z_GATE = 0
Z_QLAT = Z_GATE + MIX_W
Z_KVLAT = Z_QLAT + Q_RANK
Z_KPE = Z_KVLAT + KV_RANK
Z_GB = Z_KPE + LANES
Z_GC = Z_GB + CONV_W
Z_XIN = Z_GC + CONV_W
Z_QMEM = Z_XIN + CONV_W
Z_COLS = Z_QMEM + MEM_W

ADAM_LR = 0.001
ADAM_B1 = 0.9
ADAM_B2 = 0.999
ADAM_EPS = 1e-08
ADAM_WD = 0.01
ADAM_STEP = 10


def _tile(dim, cap, unit):
    if dim <= cap:
        return dim
    t = (cap // unit) * unit
    while t >= unit:
        if dim % t == 0:
            return t
        t -= unit
    raise ValueError(f"no tile of {dim} under {cap} in units of {unit}")


def _params(*semantics):
    return pltpu.CompilerParams(dimension_semantics=semantics, vmem_limit_bytes=VMEM_LIMIT_BYTES)


def _matmul(a, b, mode, out_dtype, name, tm_cap=512, tn_cap=1024, tk_cap=2048, after=None):
    if mode == "nn":
        (m, k), (k2, n) = a.shape, b.shape
    elif mode == "nt":
        (m, k), (n, k2) = a.shape, b.shape
    else:
        (k, m), (k2, n) = a.shape, b.shape
    assert k == k2, (a.shape, b.shape, mode)
    tm = _tile(m, tm_cap, LANES if mode == "tn" else 16)
    tn = _tile(n, tn_cap, LANES)
    tk = _tile(k, tk_cap, LANES if mode != "tn" else 16)
    nk = k // tk
    if mode == "nn":
        a_spec = pl.BlockSpec((tm, tk), lambda i, j, kk: (i, kk))
        b_spec = pl.BlockSpec((tk, tn), lambda i, j, kk: (kk, j))
        dims = (((1,), (0,)), ((), ()))
    elif mode == "nt":
        a_spec = pl.BlockSpec((tm, tk), lambda i, j, kk: (i, kk))
        b_spec = pl.BlockSpec((tn, tk), lambda i, j, kk: (j, kk))
        dims = (((1,), (1,)), ((), ()))
    else:
        a_spec = pl.BlockSpec((tk, tm), lambda i, j, kk: (kk, i))
        b_spec = pl.BlockSpec((tk, tn), lambda i, j, kk: (kk, j))
        dims = (((0,), (0,)), ((), ()))

    def body(a_ref, b_ref, *rest):
        o_ref, scratch = (rest[1], rest[2:]) if after is not None else (rest[0], rest[1:])
        part = lax.dot_general(a_ref[...].astype(BF16), b_ref[...].astype(BF16), dims, preferred_element_type=F32)
        if nk == 1:
            o_ref[...] = part.astype(o_ref.dtype)
            return
        (acc_ref,) = scratch
        kk = pl.program_id(2)

        @pl.when(kk == 0)
        def _():
            acc_ref[...] = part

        @pl.when(kk > 0)
        def _():
            acc_ref[...] += part

        @pl.when(kk == nk - 1)
        def _():
            o_ref[...] = acc_ref[...].astype(o_ref.dtype)

    return pl.pallas_call(
        body,
        grid=(m // tm, n // tn, nk),
        in_specs=[a_spec, b_spec] + ([] if after is None else [pl.BlockSpec(memory_space=pl.ANY)]),
        out_specs=pl.BlockSpec((tm, tn), lambda i, j, kk: (i, j)),
        out_shape=jax.ShapeDtypeStruct((m, n), out_dtype),
        scratch_shapes=[] if nk == 1 else [pltpu.VMEM((tm, tn), F32)],
        compiler_params=_params("parallel", "parallel", "arbitrary"),
        name=name,
    )(*([a, b] if after is None else [a, b, after]))


def _rmsnorm_fwd(x, gain, col0, width, name):
    rows = x.shape[0]
    tr = _tile(rows, 512, 16)
    cb = col0 // width
    assert cb * width == col0

    def body(x_ref, g_ref, o_ref):
        xv = x_ref[...].astype(F32)
        r = lax.rsqrt(jnp.mean(xv * xv, axis=-1, keepdims=True) + EPS)
        o_ref[...] = (xv * r * g_ref[...]).astype(o_ref.dtype)

    return pl.pallas_call(
        body,
        grid=(rows // tr,),
        in_specs=[pl.BlockSpec((tr, width), lambda i: (i, cb)), pl.BlockSpec((1, width), lambda i: (0, 0))],
        out_specs=pl.BlockSpec((tr, width), lambda i: (i, 0)),
        out_shape=jax.ShapeDtypeStruct((rows, width), BF16),
        compiler_params=_params("parallel"),
        name=name,
    )(x, gain)


def _rmsnorm_bwd(x, gain, dy, resid, col0, width, out_dtype, name):
    rows = x.shape[0]
    tr = _tile(rows, 256, 16)
    cb = col0 // width
    assert cb * width == col0
    has_resid = resid is not None

    def body(*refs):
        if has_resid:
            x_ref, g_ref, dy_ref, res_ref, dx_ref, dg_ref = refs
        else:
            x_ref, g_ref, dy_ref, dx_ref, dg_ref = refs
        i = pl.program_id(0)
        xv = x_ref[...].astype(F32)
        dyv = dy_ref[...].astype(F32)
        r = lax.rsqrt(jnp.mean(xv * xv, axis=-1, keepdims=True) + EPS)
        xr = xv * r
        dyg = dyv * g_ref[...]
        c = jnp.mean(dyg * xr, axis=-1, keepdims=True)
        dx = r * (dyg - xr * c)
        if has_resid:
            dx = dx + res_ref[...]
        dx_ref[...] = dx.astype(dx_ref.dtype)
        part = jnp.sum(dyv * xr, axis=0, keepdims=True)

        @pl.when(i == 0)
        def _():
            dg_ref[...] = part

        @pl.when(i > 0)
        def _():
            dg_ref[...] += part

    row_spec = pl.BlockSpec((tr, width), lambda i: (i, 0))
    in_specs = [pl.BlockSpec((tr, width), lambda i: (i, cb)), pl.BlockSpec((1, width), lambda i: (0, 0)), row_spec]
    args = [x, gain, dy]
    if has_resid:
        in_specs.append(row_spec)
        args.append(resid)
    return pl.pallas_call(
        body,
        grid=(rows // tr,),
        in_specs=in_specs,
        out_specs=[row_spec, pl.BlockSpec((1, width), lambda i: (0, 0))],
        out_shape=[jax.ShapeDtypeStruct((rows, width), out_dtype), jax.ShapeDtypeStruct((1, width), F32)],
        compiler_params=_params("arbitrary"),
        name=name,
    )(*args)


def _post_norm_residual(x, o, gain, name):
    rows, width = x.shape
    tr = _tile(rows, 256, 8)

    def body(x_ref, o_ref, g_ref, out_ref):
        ov = o_ref[...].astype(F32)
        r = lax.rsqrt(jnp.mean(ov * ov, axis=-1, keepdims=True) + EPS)
        out_ref[...] = x_ref[...] + ov * r * g_ref[...]

    row_spec = pl.BlockSpec((tr, width), lambda i: (i, 0))
    return pl.pallas_call(
        body,
        grid=(rows // tr,),
        in_specs=[row_spec, row_spec, pl.BlockSpec((1, width), lambda i: (0, 0))],
        out_specs=row_spec,
        out_shape=jax.ShapeDtypeStruct((rows, width), F32),
        compiler_params=_params("parallel"),
        name=name,
    )(x, o, gain)


def _latent_prep(z, g_q, g_kv, tab_c, tab_a, tab_b, name):
    rows = z.shape[0]
    tr = _tile(rows, 512, 16)

    def norm(x_ref, g_ref, o_ref):
        xv = x_ref[...].astype(F32)
        r = lax.rsqrt(jnp.mean(xv * xv, axis=-1, keepdims=True) + EPS)
        o_ref[...] = (xv * r * g_ref[...]).astype(o_ref.dtype)

    def body(q_ref, kv_ref, k_ref, gq_ref, gkv_ref, c_ref, a_ref, b_ref, qn_ref, kvn_ref, kpe_ref):
        norm(q_ref, gq_ref, qn_ref)
        norm(kv_ref, gkv_ref, kvn_ref)
        kpe_ref[...] = _rope_rows(k_ref[...].astype(F32), c_ref[...], a_ref[...], b_ref[...], 1).astype(kpe_ref.dtype)

    window = lambda c0, width: pl.BlockSpec((tr, width), lambda i: (i, c0 // width))
    gain = lambda width: pl.BlockSpec((1, width), lambda i: (0, 0))
    tab = pl.BlockSpec((tr, LANES), lambda i: (i, 0))
    out = lambda width: pl.BlockSpec((tr, width), lambda i: (i, 0))
    return pl.pallas_call(
        body,
        grid=(rows // tr,),
        in_specs=[window(Z_QLAT, Q_RANK), window(Z_KVLAT, KV_RANK), window(Z_KPE, LANES), gain(Q_RANK), gain(KV_RANK), tab, tab, tab],
        out_specs=[out(Q_RANK), out(KV_RANK), out(LANES)],
        out_shape=[jax.ShapeDtypeStruct((rows, Q_RANK), BF16), jax.ShapeDtypeStruct((rows, KV_RANK), BF16),
                   jax.ShapeDtypeStruct((rows, LANES), BF16)],
        compiler_params=_params("parallel"),
        name=name,
    )(z, z, z, g_q, g_kv, tab_c, tab_a, tab_b)


def _latent_prep_bwd(z, g_q, g_kv, dqn, dkvn, dkb, tab_c, tab_a, tab_b, heads, name):
    rows = z.shape[0]
    tr = _tile(rows, 256, 16)

    def norm_bwd(x_ref, g_ref, dy_ref, dx_ref, dg_ref, i):
        xv = x_ref[...].astype(F32)
        dyv = dy_ref[...].astype(F32)
        r = lax.rsqrt(jnp.mean(xv * xv, axis=-1, keepdims=True) + EPS)
        xr = xv * r
        dyg = dyv * g_ref[...]
        c = jnp.mean(dyg * xr, axis=-1, keepdims=True)
        dx_ref[...] = (r * (dyg - xr * c)).astype(dx_ref.dtype)
        part = jnp.sum(dyv * xr, axis=0, keepdims=True)

        @pl.when(i == 0)
        def _():
            dg_ref[...] = part

        @pl.when(i > 0)
        def _():
            dg_ref[...] += part

    def body(q_ref, kv_ref, gq_ref, gkv_ref, dqn_ref, dkvn_ref, d_ref, c_ref, a_ref, b_ref,
             dq_ref, dkv_ref, dkpe_ref, dgq_ref, dgkv_ref):
        i = pl.program_id(0)
        norm_bwd(q_ref, gq_ref, dqn_ref, dq_ref, dgq_ref, i)
        norm_bwd(kv_ref, gkv_ref, dkvn_ref, dkv_ref, dgkv_ref, i)
        acc = d_ref[:, 0:LANES]
        for h in range(1, heads):
            acc = acc + d_ref[:, h * LANES:(h + 1) * LANES]
        dkpe_ref[...] = _rope_rows(acc, c_ref[...], a_ref[...], b_ref[...], -1).astype(dkpe_ref.dtype)

    window = lambda c0, width: pl.BlockSpec((tr, width), lambda i: (i, c0 // width))
    gain = lambda width: pl.BlockSpec((1, width), lambda i: (0, 0))
    rows_of = lambda width: pl.BlockSpec((tr, width), lambda i: (i, 0))
    return pl.pallas_call(
        body,
        grid=(rows // tr,),
        in_specs=[window(Z_QLAT, Q_RANK), window(Z_KVLAT, KV_RANK), gain(Q_RANK), gain(KV_RANK), rows_of(Q_RANK), rows_of(KV_RANK),
                  rows_of(heads * LANES), rows_of(LANES), rows_of(LANES), rows_of(LANES)],
        out_specs=[rows_of(Q_RANK), rows_of(KV_RANK), rows_of(LANES), gain(Q_RANK), gain(KV_RANK)],
        out_shape=[jax.ShapeDtypeStruct((rows, Q_RANK), BF16), jax.ShapeDtypeStruct((rows, KV_RANK), BF16),
                   jax.ShapeDtypeStruct((rows, LANES), BF16), jax.ShapeDtypeStruct((1, Q_RANK), F32),
                   jax.ShapeDtypeStruct((1, KV_RANK), F32)],
        compiler_params=_params("arbitrary"),
        name=name,
    )(z, z, g_q, g_kv, dqn, dkvn, dkb, tab_c, tab_a, tab_b)


class _CommPlan:
    def __init__(self, ins, out_shape, build, n_copies):
        self.ins, self.out_shape, self.build, self.n_copies = list(ins), list(out_shape), build, n_copies

    def scratch(self):
        n = self.n_copies
        return [pltpu.SemaphoreType.DMA((n,)), pltpu.SemaphoreType.DMA((n,)), pltpu.SemaphoreType.DMA((n,))]


def _split_comm(refs, n_in, n_out, comm):
    if comm is None:
        return refs, None
    ci, co = len(comm.ins), len(comm.out_shape)
    ins, c_ins = refs[:n_in], refs[n_in:n_in + ci]
    outs, c_outs = refs[n_in + ci:n_in + ci + n_out], refs[n_in + ci + n_out:n_in + ci + n_out + co]
    rest = refs[n_in + ci + n_out + co:]
    scratch, sems = rest[:-3], rest[-3:]
    return tuple(ins) + tuple(outs) + tuple(scratch), functools.partial(comm.build, c_ins, c_outs, sems)


def _ride_start(copies, first):
    if copies is not None:
        pl.when(first)(copies()[0])


def _ride_wait(copies, last):
    if copies is not None:
        pl.when(last)(copies()[1])


def _rope_rows(x, c, a, b, sign):
    width = x.shape[-1]
    mixed = pltpu.roll(x, width - HALF_ROPE, 1) * a + pltpu.roll(x, HALF_ROPE, 1) * b
    return x * c + mixed if sign > 0 else x * c - mixed


def _attn_fwd(q, ka, kb, v, rope, heads, q_w, q_cb, ka_cb, ka_step, v_cb, v_step, scale, tq_cap, name, comm=None, tk_cap=512,
              o_into=None):
    s_q, s_k = q.shape[0], ka.shape[0]
    tq = _tile(s_q, tq_cap, 16)
    nq = s_q // tq
    has_kb = kb is not None
    n_in = 7 if has_kb else 3
    tk = _tile(s_k, tk_cap, LANES)
    o_cols, o_cb, o_old = o_into if o_into is not None else (heads * LANES, 0, None)
    assert comm is None or o_old is None

    def body(*refs):
        if o_old is not None:
            refs = refs[:n_in] + refs[n_in + 1:]
        refs, copies = _split_comm(refs, n_in, 2, comm)
        first = jnp.logical_and(pl.program_id(0) == 0, pl.program_id(1) == 0)
        last = jnp.logical_and(pl.program_id(0) == heads - 1, pl.program_id(1) == nq - 1)
        _ride_start(copies, first)
        if has_kb:
            q_ref, ka_ref, kb_ref, v_ref, c_ref, a_ref, b_ref, o_ref, lse_ref, k_scr = refs

            @pl.when(pl.program_id(1) == 0)
            def _():
                k_scr[:, 0:LANES] = ka_ref[...].astype(BF16)
                k_scr[:, LANES:2 * LANES] = kb_ref[...].astype(BF16)

            keys = k_scr
            qv = _rope_rows(q_ref[...].astype(F32), c_ref[...], a_ref[...], b_ref[...], 1).astype(BF16)
        else:
            q_ref, ka_ref, v_ref, o_ref, lse_ref = refs
            keys = ka_ref
            qv = q_ref[...].astype(BF16)
        c2 = scale * LOG2_E
        m = l = o = None
        nk = s_k // tk
        scores = lambda j: lax.dot_general(qv, keys[j * tk:(j + 1) * tk, :].astype(BF16), (((1,), (1,)), ((), ())),
                                           preferred_element_type=F32)
        s_next = scores(0)
        for j in range(nk):
            sj = s_next
            if j + 1 < nk:
                s_next = scores(j + 1)
            mj = jnp.max(sj, axis=-1, keepdims=True)
            m_new = mj if m is None else jnp.maximum(m, mj)
            pj = jnp.exp2((sj - m_new) * c2)
            lj = jnp.sum(pj, axis=-1, keepdims=True)
            oj = jnp.dot(pj.astype(BF16), v_ref[j * tk:(j + 1) * tk, :].astype(BF16), preferred_element_type=F32)
            if m is None:
                l, o = lj, oj
            else:
                alpha = jnp.exp2((m - m_new) * c2)
                l, o = l * alpha + lj, o * alpha + oj
            m = m_new
        o_ref[...] = (o * (1.0 / l)).astype(o_ref.dtype)
        lse_ref[...] = jnp.broadcast_to(m * c2 + jnp.log2(l), lse_ref.shape)
        _ride_wait(copies, last)

    in_specs = [pl.BlockSpec((tq, q_w), lambda h, i: (i, q_cb + h)),
                pl.BlockSpec((s_k, LANES), lambda h, i: (0, ka_cb + ka_step * h))]
    args = [q, ka]
    if has_kb:
        in_specs.append(pl.BlockSpec((s_k, LANES), lambda h, i: (0, 0)))
        args.append(kb)
    in_specs.append(pl.BlockSpec((s_k, LANES), lambda h, i: (0, v_cb + v_step * h)))
    args.append(v)
    if has_kb:
        in_specs += [pl.BlockSpec((tq, q_w), lambda h, i: (i, 0))] * 3
        args += list(rope)
    aliases = {}
    if o_old is not None:
        aliases = {len(args): 0}
        in_specs.append(ANY)
        args.append(o_old)
    out_specs = [pl.BlockSpec((tq, LANES), lambda h, i: (i, o_cb + h)), pl.BlockSpec((tq, LANES), lambda h, i: (i, h))]
    out_shape = [jax.ShapeDtypeStruct((s_q, o_cols), BF16), jax.ShapeDtypeStruct((s_q, heads * LANES), F32)]
    scratch = [pltpu.VMEM((s_k, 2 * LANES), BF16)] if has_kb else []
    if comm is not None:
        in_specs += [ANY] * len(comm.ins)
        args += comm.ins
        out_specs += [ANY] * len(comm.out_shape)
        out_shape += comm.out_shape
        scratch += comm.scratch()
    res = pl.pallas_call(
        body,
        grid=(heads, nq),
        in_specs=in_specs,
        out_specs=out_specs,
        out_shape=out_shape,
        scratch_shapes=scratch,
        input_output_aliases=aliases,
        compiler_params=_params("arbitrary", "arbitrary"),
        name=name,
    )(*args)
    return res[0], res[1], list(res[2:])


def _attn_bwd(q, ka, kb, v, o, do, lse, rope, heads, q_w, q_cb, ka_cb, ka_step, v_cb, v_step, o_cb, scale, tq_cap, name,
              comm=None, tk_cap=512):
    s_q, s_k = q.shape[0], ka.shape[0]
    tq = _tile(s_q, tq_cap, 16)
    nq = s_q // tq
    has_kb = kb is not None
    n_in = 10 if has_kb else 6
    n_out = 3
    tk = _tile(s_k, tk_cap, LANES)

    def body(*refs):
        refs, copies = _split_comm(refs, n_in, n_out, comm)
        first = jnp.logical_and(pl.program_id(0) == 0, pl.program_id(1) == 0)
        last = jnp.logical_and(pl.program_id(0) == heads - 1, pl.program_id(1) == nq - 1)
        _ride_start(copies, first)
        if has_kb:
            (q_ref, ka_ref, kb_ref, v_ref, o_ref, do_ref, lse_ref, c_ref, a_ref, b_ref, dq_ref, dkv_ref, dkb_ref, k_scr, dk_acc,
             dv_acc) = refs
        else:
            q_ref, ka_ref, v_ref, o_ref, do_ref, lse_ref, dq_ref, dka_ref, dv_ref, dk_acc, dv_acc = refs
        i = pl.program_id(1)

        @pl.when(i == 0)
        def _():
            dk_acc[...] = jnp.zeros_like(dk_acc)
            dv_acc[...] = jnp.zeros_like(dv_acc)
            if has_kb:
                k_scr[:, 0:LANES] = ka_ref[...].astype(BF16)
                k_scr[:, LANES:2 * LANES] = kb_ref[...].astype(BF16)

        keys = k_scr if has_kb else ka_ref
        if has_kb:
            qv = _rope_rows(q_ref[...].astype(F32), c_ref[...], a_ref[...], b_ref[...], 1).astype(BF16)
        else:
            qv = q_ref[...].astype(BF16)
        dov = do_ref[...].astype(BF16)
        delta = jnp.sum(dov.astype(F32) * o_ref[...].astype(F32), axis=-1, keepdims=True)
        lse2 = lse_ref[:, 0:1]
        c2 = scale * LOG2_E
        nk = s_k // tk
        rows = lambda j: slice(j * tk, (j + 1) * tk)
        nt = (((1,), (1,)), ((), ()))
        tn = (((0,), (0,)), ((), ()))

        def scores(j):
            return (lax.dot_general(qv, keys[rows(j), :].astype(BF16), nt, preferred_element_type=F32),
                    lax.dot_general(dov, v_ref[rows(j), :].astype(BF16), nt, preferred_element_type=F32))

        nxt = scores(0)
        dq = None
        for j in range(nk):
            sj, dpj = nxt
            if j + 1 < nk:
                nxt = scores(j + 1)
            pj = jnp.exp2(sj * c2 - lse2)
            dsj = (pj * (dpj - delta)).astype(BF16)
            dqj = jnp.dot(dsj, keys[rows(j), :].astype(BF16), preferred_element_type=F32)
            dq = dqj if dq is None else dq + dqj
            dk_acc[rows(j), :] += lax.dot_general(dsj, qv, tn, preferred_element_type=F32)
            dv_acc[rows(j), :] += lax.dot_general(pj.astype(BF16), dov, tn, preferred_element_type=F32)
        dq = dq * scale
        if has_kb:
            dq = _rope_rows(dq, c_ref[...], a_ref[...], b_ref[...], -1)
        dq_ref[...] = dq.astype(dq_ref.dtype)

        @pl.when(i == nq - 1)
        def _():
            if has_kb:
                dkv_ref[:, 0:LANES] = (dk_acc[:, 0:LANES] * scale).astype(dkv_ref.dtype)
                dkv_ref[:, LANES:2 * LANES] = dv_acc[...].astype(dkv_ref.dtype)
                dkb_ref[...] = dk_acc[:, LANES:2 * LANES] * scale
            else:
                dka_ref[...] = (dk_acc[...] * scale).astype(dka_ref.dtype)
                dv_ref[...] = dv_acc[...].astype(dv_ref.dtype)

        _ride_wait(copies, last)

    key_spec = lambda cb, step: pl.BlockSpec((s_k, LANES), lambda h, i: (0, cb + step * h))
    row_spec = lambda cb: pl.BlockSpec((tq, LANES), lambda h, i: (i, cb + h))
    in_specs = [pl.BlockSpec((tq, q_w), lambda h, i: (i, q_cb + h)), key_spec(ka_cb, ka_step)]
    args = [q, ka]
    if has_kb:
        in_specs.append(pl.BlockSpec((s_k, LANES), lambda h, i: (0, 0)))
        args.append(kb)
    in_specs += [key_spec(v_cb, v_step), row_spec(o_cb), row_spec(o_cb), row_spec(0)]
    args += [v, o, do, lse]
    if has_kb:
        in_specs += [pl.BlockSpec((tq, q_w), lambda h, i: (i, 0))] * 3
        args += list(rope)
    out_specs = [pl.BlockSpec((tq, q_w), lambda h, i: (i, h))]
    out_shape = [jax.ShapeDtypeStruct((s_q, heads * q_w), BF16)]
    scratch = []
    if has_kb:
        out_specs += [pl.BlockSpec((s_k, 2 * LANES), lambda h, i: (0, h)), key_spec(0, 1)]
        out_shape += [jax.ShapeDtypeStruct((s_k, heads * 2 * LANES), BF16), jax.ShapeDtypeStruct((s_k, heads * LANES), F32)]
        scratch.append(pltpu.VMEM((s_k, 2 * LANES), BF16))
    else:
        out_specs += [key_spec(0, 1), key_spec(0, 1)]
        out_shape += [jax.ShapeDtypeStruct((s_k, heads * LANES), BF16)] * 2
    scratch += [pltpu.VMEM((s_k, q_w), F32), pltpu.VMEM((s_k, LANES), F32)]
    if comm is not None:
        in_specs += [ANY] * len(comm.ins)
        args += comm.ins
        out_specs += [ANY] * len(comm.out_shape)
        out_shape += comm.out_shape
        scratch += comm.scratch()
    res = pl.pallas_call(
        body,
        grid=(heads, nq),
        in_specs=in_specs,
        out_specs=out_specs,
        out_shape=out_shape,
        scratch_shapes=scratch,
        compiler_params=_params("arbitrary", "arbitrary"),
        name=name,
    )(*args)
    return res[0], res[1], res[2], list(res[3:])


def _shift_rows(u, rows):
    t = lax.broadcasted_iota(jnp.int32, u.shape, 0)
    prev = jnp.where(t == 0, 0.0, pltpu.roll(u, 1, 0))
    nxt = jnp.where(t == rows - 1, 0.0, pltpu.roll(u, rows - 1, 0))
    return prev, nxt


def _conv_fwd(z, conv_w, cat, name):
    rows = z.shape[0]
    nblk = CONV_W // LANES

    def body(gb_ref, gc_ref, xin_ref, w_ref, cat_ref, o_ref):
        del cat_ref
        u = gc_ref[...].astype(F32) * xin_ref[...].astype(F32)
        prev, nxt = _shift_rows(u, rows)
        conv = prev * w_ref[0:1, :] + u * w_ref[1:2, :] + nxt * w_ref[2:3, :]
        o_ref[...] = (gb_ref[...].astype(F32) * conv).astype(o_ref.dtype)

    col = lambda c0: pl.BlockSpec((rows, LANES), lambda j: (0, c0 // LANES + j))
    return pl.pallas_call(
        body,
        grid=(nblk,),
        in_specs=[col(Z_GB), col(Z_GC), col(Z_XIN), pl.BlockSpec((3, LANES), lambda j: (0, j)), ANY],
        out_specs=col(MLA_W),
        out_shape=jax.ShapeDtypeStruct(cat.shape, cat.dtype),
        input_output_aliases={4: 0},
        compiler_params=_params("parallel"),
        name=name,
    )(z, z, z, conv_w, cat)


def _conv_bwd(z, conv_w, dcat, name):
    rows = z.shape[0]
    nblk = CONV_W // LANES

    def body(gb_ref, gc_ref, xin_ref, w_ref, dc_ref, dgb_ref, dgc_ref, dxin_ref, dw_ref):
        gc = gc_ref[...].astype(F32)
        xin = xin_ref[...].astype(F32)
        dc = dc_ref[...].astype(F32)
        u = gc * xin
        prev, nxt = _shift_rows(u, rows)
        w0, w1, w2 = w_ref[0:1, :], w_ref[1:2, :], w_ref[2:3, :]
        conv = prev * w0 + u * w1 + nxt * w2
        dgb_ref[...] = (dc * conv).astype(dgb_ref.dtype)
        dconv = dc * gb_ref[...].astype(F32)
        dw_ref[0:1, :] = jnp.sum(dconv * prev, axis=0, keepdims=True)
        dw_ref[1:2, :] = jnp.sum(dconv * u, axis=0, keepdims=True)
        dw_ref[2:3, :] = jnp.sum(dconv * nxt, axis=0, keepdims=True)
        dprev, dnxt = _shift_rows(dconv, rows)
        du = dnxt * w0 + dconv * w1 + dprev * w2
        dgc_ref[...] = (du * xin).astype(dgc_ref.dtype)
        dxin_ref[...] = (du * gc).astype(dxin_ref.dtype)

    col = lambda c0: pl.BlockSpec((rows, LANES), lambda j: (0, c0 // LANES + j))
    w_spec = pl.BlockSpec((3, LANES), lambda j: (0, j))
    piece = jax.ShapeDtypeStruct((rows, CONV_W), BF16)
    return pl.pallas_call(
        body,
        grid=(nblk,),
        in_specs=[col(Z_GB), col(Z_GC), col(Z_XIN), w_spec, col(MLA_W)],
        out_specs=[col(0), col(0), col(0), w_spec],
        out_shape=[piece, piece, piece, jax.ShapeDtypeStruct((3, CONV_W), F32)],
        compiler_params=_params("parallel"),
        name=name,
    )(z, z, z, conv_w, dcat)


def _gate_fwd(cat, z, name):
    rows = cat.shape[0]
    tr = _tile(rows, 256, 16)
    tc = MIX_W
    g0 = Z_GATE // tc

    def body(c_ref, g_ref, y_ref):
        g = g_ref[...].astype(F32)
        y_ref[...] = (c_ref[...].astype(F32) * (g * jax.nn.sigmoid(g))).astype(y_ref.dtype)

    blk = pl.BlockSpec((tr, tc), lambda i, j: (i, j))
    return pl.pallas_call(
        body,
        grid=(rows // tr, MIX_W // tc),
        in_specs=[blk, pl.BlockSpec((tr, tc), lambda i, j: (i, g0 + j))],
        out_specs=blk,
        out_shape=jax.ShapeDtypeStruct((rows, MIX_W), BF16),
        compiler_params=_params("parallel", "parallel"),
        name=name,
    )(cat, z)


def _out_proj_dx_gate_bwd(do, w_o, cat, z, name):
    rows, k = do.shape
    tm = _tile(rows, 512, 16)
    tn = _tile(MIX_W, 1024, LANES)
    g0 = Z_GATE // tn

    def body(do_ref, w_ref, c_ref, g_ref, dcat_ref, dgate_ref):
        dy = lax.dot_general(do_ref[...], w_ref[...], (((1,), (1,)), ((), ())), preferred_element_type=F32)
        g = g_ref[...].astype(F32)
        sg = jax.nn.sigmoid(g)
        dcat_ref[...] = (dy * (g * sg)).astype(dcat_ref.dtype)
        dgate_ref[...] = (dy * c_ref[...].astype(F32) * (sg * (1.0 + g * (1.0 - sg)))).astype(dgate_ref.dtype)

    blk = pl.BlockSpec((tm, tn), lambda i, j: (i, j))
    out = jax.ShapeDtypeStruct((rows, MIX_W), BF16)
    return pl.pallas_call(
        body,
        grid=(rows // tm, MIX_W // tn),
        in_specs=[pl.BlockSpec((tm, k), lambda i, j: (i, 0)), pl.BlockSpec((tn, k), lambda i, j: (j, 0)), blk,
                  pl.BlockSpec((tm, tn), lambda i, j: (i, g0 + j))],
        out_specs=[blk, blk],
        out_shape=[out, out],
        compiler_params=_params("parallel", "parallel"),
        name=name,
    )(do, w_o, cat, z)


def _loss_head(y, target, name):
    rows, width = y.shape
    tr = _tile(rows, 256, 8)

    def body(y_ref, t_ref, g_ref, loss_ref):
        i = pl.program_id(0)
        d = y_ref[...] - t_ref[...]
        g_ref[...] = d / width
        part = 0.5 * jnp.sum(jnp.mean(d * d, axis=-1, keepdims=True), axis=0, keepdims=True)
        part = jnp.broadcast_to(part, loss_ref.shape)

        @pl.when(i == 0)
        def _():
            loss_ref[...] = part

        @pl.when(i > 0)
        def _():
            loss_ref[...] += part

    row_spec = pl.BlockSpec((tr, width), lambda i: (i, 0))
    return pl.pallas_call(
        body,
        grid=(rows // tr,),
        in_specs=[row_spec, row_spec],
        out_specs=[row_spec, pl.BlockSpec((1, LANES), lambda i: (0, 0))],
        out_shape=[jax.ShapeDtypeStruct((rows, width), F32), jax.ShapeDtypeStruct((1, LANES), F32)],
        compiler_params=_params("arbitrary"),
        name=name,
    )(y, target)


CHIP_FLIPS = ((1, 0), (0, 1), (1, 1))
ANY = pl.BlockSpec(memory_space=pl.ANY)


def _chip_copies(pieces, sems, n_slot):
    send_sems, recv_sems, local_sems = sems
    x, y, c = lax.axis_index("x"), lax.axis_index("y"), lax.axis_index("c")
    me = 2 * x + y

    def remote(j, k, a, src, dst):
        fx, fy = CHIP_FLIPS[k]
        return pltpu.make_async_remote_copy(
            src_ref=src, dst_ref=dst, send_sem=send_sems.at[n_slot * k + a], recv_sem=recv_sems.at[n_slot * k + a],
            device_id=((j // 2) ^ fx, (j % 2) ^ fy, c), device_id_type=MESH_ID)

    def peer(j, k):
        fx, fy = CHIP_FLIPS[k]
        return 2 * ((j // 2) ^ fx) + ((j % 2) ^ fy)

    def start_as(j):
        def run():
            for a, (src, dst) in enumerate(pieces(j, j)):
                pltpu.make_async_copy(src, dst, local_sems.at[a]).start()
            for k in range(len(CHIP_FLIPS)):
                for a, (src, dst) in enumerate(pieces(j, peer(j, k))):
                    remote(j, k, a, src, dst).start()
        return run

    def wait_as(j):
        def run():
            for a, (src, dst) in enumerate(pieces(j, j)):
                pltpu.make_async_copy(src, dst, local_sems.at[a]).wait()
            for k in range(len(CHIP_FLIPS)):
                for a, (src, dst) in enumerate(pieces(j, peer(j, k))):
                    remote(j, k, a, src, dst).wait_send()
                for a, (src, dst) in enumerate(pieces(peer(j, k), j)):
                    remote(j, k, a, src, dst).wait_recv()
        return run

    def start():
        for j in range(N_CHIPS):
            pl.when(me == j)(start_as(j))

    def wait():
        for j in range(N_CHIPS):
            pl.when(me == j)(wait_as(j))

    return start, wait


IN_PIECES = ((0, Q_RANK, Z_QLAT), (Q_RANK, KV_RANK, Z_KVLAT), (Q_RANK + KV_RANK, ROPE, Z_KPE),
             (Q_RANK + KV_RANK + ROPE, CONV_W, Z_GB), (Q_RANK + KV_RANK + ROPE + CONV_W, CONV_W, Z_GC),
             (Q_RANK + KV_RANK + ROPE + 2 * CONV_W, CONV_W, Z_XIN), (Q_RANK + KV_RANK + ROPE + 3 * CONV_W, MEM_W, Z_QMEM),
             (Q_RANK + KV_RANK + ROPE + 3 * CONV_W + MEM_W, MIX_W, Z_GATE))
IN_SHARD = IN_COLS // N_CHIPS


def _in_segments(j):
    lo, hi = j * IN_SHARD, (j + 1) * IN_SHARD
    segs = []
    for r0, width, z0 in IN_PIECES:
        a, b = max(lo, r0), min(hi, r0 + width)
        if a < b:
            segs.append((a - lo, z0 + a - r0, b - a))
    return segs


N_SLOT = 11


def _gather_plan(l, shards, zero_rows, part="all"):
    s_in, s_uq, s_ukv, s_conv, s_mk, s_mv, s_o = shards
    ukv_c, mk_r, mk_c, o_r = s_ukv.shape[2], s_mk.shape[1], s_mk.shape[2], s_o.shape[1]
    stack = lambda s: jax.ShapeDtypeStruct((N_CHIPS,) + s.shape[1:], s.dtype)
    in_ins, in_outs = [s_in, zero_rows], [jax.ShapeDtypeStruct((Z_COLS, s_in.shape[2]), s_in.dtype)]
    rest_ins = [s_uq, s_ukv, s_conv, s_mk, s_mv, s_o]
    rest_outs = [stack(s_uq), jax.ShapeDtypeStruct((s_ukv.shape[1], N_CHIPS * ukv_c), s_ukv.dtype), stack(s_conv),
                 jax.ShapeDtypeStruct((N_CHIPS * mk_r, 2 * mk_c), s_mk.dtype),
                 jax.ShapeDtypeStruct((N_CHIPS * o_r, s_o.shape[2]), s_o.dtype)]
    with_in, with_rest = part != "rest", part != "in"

    def build(ins, outs, sems):
        ins, outs = list(ins), list(outs)
        if with_in:
            r_in, r_zero, f_in = ins.pop(0), ins.pop(0), outs.pop(0)
        if with_rest:
            r_uq, r_ukv, r_conv, r_mk, r_mv, r_o = ins
            g_uq, f_ukv, g_conv, f_mkv, f_o = outs

        def pieces(j, t):
            out = []
            if with_in:
                out += [(r_in.at[l, pl.ds(so, n), :], f_in.at[pl.ds(zo, n), :]) for so, zo, n in _in_segments(j)]
            if with_rest:
                out += [(r_uq.at[l], g_uq.at[j]), (r_ukv.at[l], f_ukv.at[:, pl.ds(j * ukv_c, ukv_c)]),
                        (r_conv.at[l], g_conv.at[j]),
                        (r_mk.at[l], f_mkv.at[pl.ds(j * mk_r, mk_r), pl.ds(0, mk_c)]),
                        (r_mv.at[l], f_mkv.at[pl.ds(j * mk_r, mk_r), pl.ds(mk_c, mk_c)]),
                        (r_o.at[l], f_o.at[pl.ds(j * o_r, o_r), :])]
            if with_in and j == t:
                out.append((r_zero, f_in.at[pl.ds(Z_KPE + ROPE, LANES - ROPE), :]))
            return out

        return _chip_copies(pieces, sems, N_SLOT)

    ins = (in_ins if with_in else []) + (rest_ins if with_rest else [])
    outs = (in_outs if with_in else []) + (rest_outs if with_rest else [])
    return _CommPlan(ins, outs, build, len(CHIP_FLIPS) * N_SLOT)


def _scatter_plan(dwt_in, c_uq, dw_ukv, c_conv, dw_mkv, dw_o, part="all"):
    ukv_c, mk_r, mk_c, o_r = dw_ukv.shape[1] // N_CHIPS, dw_mkv.shape[0] // N_CHIPS, dw_mkv.shape[1] // 2, dw_o.shape[0] // N_CHIPS
    with_in, with_rest = part != "rest", part != "in"
    in_outs = [jax.ShapeDtypeStruct((N_CHIPS, IN_SHARD, D_MODEL), BF16)]
    rest_ins = [c_uq, dw_ukv, c_conv, dw_mkv, dw_o]
    rest_outs = [jax.ShapeDtypeStruct(c_uq.shape, c_uq.dtype),
                 jax.ShapeDtypeStruct((N_CHIPS, dw_ukv.shape[0], ukv_c), dw_ukv.dtype),
                 jax.ShapeDtypeStruct(c_conv.shape, c_conv.dtype),
                 jax.ShapeDtypeStruct((N_CHIPS, mk_r, mk_c), dw_mkv.dtype), jax.ShapeDtypeStruct((N_CHIPS, mk_r, mk_c), dw_mkv.dtype),
                 jax.ShapeDtypeStruct((N_CHIPS, o_r, dw_o.shape[1]), dw_o.dtype)]

    def build(ins, outs, sems):
        ins, outs = list(ins), list(outs)
        if with_in:
            r_in, o_in = ins.pop(0), outs.pop(0)
        if with_rest:
            r_uq, r_ukv, r_conv, r_mkv, r_o = ins
            o_uq, o_ukv, o_conv, o_mk, o_mv, o_o = outs

        def pieces(j, t):
            out = []
            if with_in:
                out += [(r_in.at[pl.ds(zo, n), :], o_in.at[j, pl.ds(so, n), :]) for so, zo, n in _in_segments(t)]
            if with_rest:
                out += [(r_uq.at[t], o_uq.at[j]), (r_ukv.at[:, pl.ds(t * ukv_c, ukv_c)], o_ukv.at[j]),
                        (r_conv.at[t], o_conv.at[j]),
                        (r_mkv.at[pl.ds(t * mk_r, mk_r), pl.ds(0, mk_c)], o_mk.at[j]),
                        (r_mkv.at[pl.ds(t * mk_r, mk_r), pl.ds(mk_c, mk_c)], o_mv.at[j]),
                        (r_o.at[pl.ds(t * o_r, o_r), :], o_o.at[j])]
            return out

        return _chip_copies(pieces, sems, N_SLOT)

    ins = ([dwt_in] if with_in else []) + (rest_ins if with_rest else [])
    outs = (in_outs if with_in else []) + (rest_outs if with_rest else [])
    return _CommPlan(ins, outs, build, len(CHIP_FLIPS) * N_SLOT)


HBM = pl.BlockSpec(memory_space=pltpu.HBM)
SEM = pl.BlockSpec(memory_space=pltpu.SEMAPHORE)
SIDE_EFFECT = pltpu.SideEffectType.DATAFLOW_SIDE_EFFECTING


def _comm_start(plan, after, name):
    n_in, n_out, n_after = len(plan.ins), len(plan.out_shape), len(after)
    n_buf = n_in + n_out

    def body(*refs):
        bufs, sems, token = refs[:n_buf], refs[n_buf + n_after:n_buf + n_after + 3], refs[-1]
        start, _ = plan.build(bufs[:n_in], bufs[n_in:], sems)
        start()
        token[...] = jnp.zeros_like(token)

    lands = [lax.empty(s.shape, s.dtype) for s in plan.out_shape]
    args = [pltpu.with_memory_space_constraint(a, pltpu.HBM) for a in list(plan.ins) + lands]
    res = pl.pallas_call(
        body,
        in_specs=[HBM] * n_buf + [ANY] * n_after,
        out_specs=[SEM] * 3 + [HBM] * n_out + [pl.BlockSpec(memory_space=pltpu.VMEM)],
        out_shape=plan.scratch() + [pltpu.HBM(a.shape, a.dtype) for a in lands] + [jax.ShapeDtypeStruct((8, LANES), F32)],
        input_output_aliases={n_in + i: 3 + i for i in range(n_out)},
        compiler_params=pltpu.CompilerParams(has_side_effects=SIDE_EFFECT),
        name=name,
    )(*args, *after)
    return list(res[:3]), list(res[3:3 + n_out]), res[-1]


def _comm_finish(plan, started, after, name):
    sems, lands, _ = started
    n_in, n_out = len(plan.ins), len(plan.out_shape)
    n_buf = n_in + n_out

    def body(*refs):
        bufs_in, sem_refs = refs[:n_buf], refs[n_buf:n_buf + 3]
        _, wait = plan.build(bufs_in[:n_in], bufs_in[n_in:], sem_refs)
        wait()

    sources = [pltpu.with_memory_space_constraint(a, pltpu.HBM) for a in plan.ins]
    res = pl.pallas_call(
        body,
        in_specs=[HBM] * n_buf + [SEM] * 3 + [ANY] * len(after),
        out_specs=[HBM] * n_out,
        out_shape=[pltpu.HBM(b.shape, b.dtype) for b in lands],
        input_output_aliases={n_in + i: i for i in range(n_out)},
        compiler_params=pltpu.CompilerParams(has_side_effects=SIDE_EFFECT),
        name=name,
    )(*sources, *lands, *sems, *after)
    return list(res)


def _comm_call(plan, name):
    n_in, n_out = len(plan.ins), len(plan.out_shape)

    def body(*refs):
        start, wait = plan.build(refs[:n_in], refs[n_in:n_in + n_out], refs[n_in + n_out:])
        start()
        wait()

    return list(pl.pallas_call(
        body,
        in_specs=[ANY] * n_in,
        out_specs=[ANY] * n_out,
        out_shape=plan.out_shape,
        scratch_shapes=plan.scratch(),
        name=name,
    )(*plan.ins))


def _sibling_plan(arrays):
    def build(ins, outs, sems):
        send_sems, recv_sems, _ = sems
        sibling = (lax.axis_index("x"), lax.axis_index("y"), 1 - lax.axis_index("c"))
        copies = [pltpu.make_async_remote_copy(src_ref=src, dst_ref=dst, send_sem=send_sems.at[a], recv_sem=recv_sems.at[a],
                                               device_id=sibling, device_id_type=MESH_ID)
                  for a, (src, dst) in enumerate(zip(ins, outs))]

        def start():
            for cp in copies:
                cp.start()

        def wait():
            for cp in copies:
                cp.wait()

        return start, wait

    return _CommPlan(arrays, [jax.ShapeDtypeStruct(v.shape, v.dtype) for v in arrays], build, len(arrays))


DEVICE_FLIPS = tuple((fx, fy, fc) for fx in (0, 1) for fy in (0, 1) for fc in (0, 1))[1:]


def _gather_all(v, after, name):
    def body(v_ref, after_ref, out_ref, send_sems, recv_sems, local_sem):
        del after_ref
        x, y, c = lax.axis_index("x"), lax.axis_index("y"), lax.axis_index("c")
        me = 4 * x + 2 * y + c
        local = pltpu.make_async_copy(v_ref, out_ref.at[me], local_sem)
        local.start()
        copies = [local]
        for k, (fx, fy, fc) in enumerate(DEVICE_FLIPS):
            cp = pltpu.make_async_remote_copy(
                src_ref=v_ref, dst_ref=out_ref.at[me], send_sem=send_sems.at[k], recv_sem=recv_sems.at[k],
                device_id=((x + fx) % 2, (y + fy) % 2, (c + fc) % 2), device_id_type=MESH_ID)
            cp.start()
            copies.append(cp)
        for cp in copies:
            cp.wait()

    return pl.pallas_call(
        body,
        in_specs=[ANY, ANY],
        out_specs=ANY,
        out_shape=jax.ShapeDtypeStruct((N_DEV,) + v.shape, v.dtype),
        scratch_shapes=[pltpu.SemaphoreType.DMA((N_DEV - 1,)), pltpu.SemaphoreType.DMA((N_DEV - 1,)), pltpu.SemaphoreType.DMA],
        name=name,
    )(v, after)


def _sum_slots(parts, name):
    n, rows, cols = parts.shape
    tr = _tile(rows, 256, 16)

    def body(p_ref, o_ref):
        acc = p_ref[0].astype(F32)
        for k in range(1, n):
            acc = acc + p_ref[k].astype(F32)
        o_ref[...] = acc

    return pl.pallas_call(
        body,
        grid=(rows // tr,),
        in_specs=[pl.BlockSpec((n, tr, cols), lambda i: (0, i, 0))],
        out_specs=pl.BlockSpec((tr, cols), lambda i: (i, 0)),
        out_shape=jax.ShapeDtypeStruct((rows, cols), F32),
        compiler_params=_params("parallel"),
        name=name,
    )(parts)


def _adamw_math(w, g, m, v):
    m_new = ADAM_B1 * m + (1.0 - ADAM_B1) * g
    v_new = ADAM_B2 * v + (1.0 - ADAM_B2) * jnp.square(g)
    m_hat = m_new / (1.0 - ADAM_B1 ** ADAM_STEP)
    v_hat = v_new / (1.0 - ADAM_B2 ** ADAM_STEP)
    return -ADAM_LR * (m_hat / (jnp.sqrt(v_hat) + ADAM_EPS) + ADAM_WD * w), m_new, v_new


def _adamw(w, g, m, v, name):
    rows, cols = w.shape
    tr = _tile(rows, 256, 8)

    def body(w_ref, g_ref, m_ref, v_ref, d_out, m_out, v_out):
        d_out[...], m_out[...], v_out[...] = _adamw_math(w_ref[...], g_ref[...], m_ref[...], v_ref[...])

    blk = pl.BlockSpec((tr, cols), lambda i: (i, 0))
    out = jax.ShapeDtypeStruct((rows, cols), F32)
    return pl.pallas_call(
        body,
        grid=(rows // tr,),
        in_specs=[blk] * 4,
        out_specs=[blk] * 3,
        out_shape=[out] * 3,
        compiler_params=_params("parallel"),
        name=name,
    )(w, g, m, v)


def _adamw_layer(l, w, g_a, g_b, m, v, prev, name):
    depth, rows, cols = w.shape
    tr = _tile(rows, 256, 8)

    def body(w_ref, ga_ref, gb_ref, m_ref, v_ref, *rest):
        g_out, d_out, m_out, v_out = rest[-4:]
        g = ga_ref[...] + gb_ref[...]
        g_out[...] = g
        d_out[...], m_out[...], v_out[...] = _adamw_math(w_ref[...], g, m_ref[...], v_ref[...])

    stacked = pl.BlockSpec((None, tr, cols), lambda i: (l, i, 0))
    flat = pl.BlockSpec((tr, cols), lambda i: (i, 0))
    in_specs = [stacked, flat, flat, stacked, stacked]
    args = [w, g_a, g_b, m, v]
    aliases = {}
    if prev is not None:
        in_specs += [ANY] * 4
        args += list(prev)
        aliases = {5 + k: k for k in range(4)}
    out = jax.ShapeDtypeStruct((depth, rows, cols), F32)
    return pl.pallas_call(
        body,
        grid=(rows // tr,),
        in_specs=in_specs,
        out_specs=[stacked] * 4,
        out_shape=[out] * 4,
        input_output_aliases=aliases,
        compiler_params=_params("parallel"),
        name=name,
    )(*args)


def _cols_from_shards(g):
    _, r, c = g.shape
    return jnp.transpose(g, (1, 0, 2)).reshape(r, N_CHIPS * c)


def _cols_to_shards(full):
    r, c4 = full.shape
    c = c4 // N_CHIPS
    return jnp.transpose(full.reshape(r, N_CHIPS, c), (1, 0, 2))


IN_ORDER = (Q_RANK, KV_RANK, ROPE, CONV_W, CONV_W, CONV_W, MEM_W, MIX_W)


def _w_in_to_z_layout(w_in):
    edges = [0]
    for width in IN_ORDER:
        edges.append(edges[-1] + width)
    q_lat, kv_lat, k_pe, gb, gc, xin, q_mem, gate = [w_in[..., edges[i]:edges[i + 1]] for i in range(8)]
    pad = jnp.zeros(k_pe.shape[:-1] + (LANES - ROPE,), w_in.dtype)
    return jnp.concatenate([gate, q_lat, kv_lat, k_pe, pad, gb, gc, xin, q_mem], axis=-1)


def _w_in_from_z_layout(wz):
    cut = lambda c0, width: wz[..., c0:c0 + width]
    return jnp.concatenate(
        [cut(Z_QLAT, Q_RANK), cut(Z_KVLAT, KV_RANK), cut(Z_KPE, ROPE), cut(Z_GB, CONV_W), cut(Z_GC, CONV_W),
         cut(Z_XIN, CONV_W), cut(Z_QMEM, MEM_W), cut(Z_GATE, MIX_W)], axis=-1)


def _w_uq_pad(w_uq):
    r, _ = w_uq.shape
    w = jnp.pad(w_uq.reshape(r, MLA_HEADS, QK_HEAD), ((0, 0), (0, 0), (0, QPAD - QK_HEAD)))
    return w.reshape(r, MLA_HEADS * QPAD)


def _w_uq_unpad(w):
    r, _ = w.shape
    return w.reshape(r, MLA_HEADS, QPAD)[..., :QK_HEAD].reshape(r, MLA_HEADS * QK_HEAD)


def _rope_tables(positions):
    inv_freq = 1.0 / (ROPE_THETA ** (jnp.arange(0, ROPE, 2, dtype=F32) / ROPE))
    ang = positions.astype(F32)[:, None] * inv_freq
    cos, sin = jnp.cos(ang), jnp.sin(ang)
    s = positions.shape[0]
    zero = jnp.zeros((s, HALF_ROPE), F32)
    pad = jnp.zeros((s, LANES - ROPE), F32)
    kc = jnp.concatenate([cos, cos, pad], axis=-1)
    ka = jnp.concatenate([-sin, zero, pad], axis=-1)
    kb = jnp.concatenate([zero, sin, pad], axis=-1)
    qc = jnp.concatenate([jnp.ones((s, NOPE), F32), kc], axis=-1)
    qa = jnp.concatenate([jnp.zeros((s, NOPE), F32), ka], axis=-1)
    qb = jnp.concatenate([jnp.zeros((s, NOPE), F32), kb], axis=-1)
    return (qc, qa, qb), (kc, ka, kb)


def _layer_weights(gathered):
    return (gathered[0],) + _late_weights(gathered[1:])


def _late_weights(gathered):
    g_uq, w_ukv, g_conv, w_mkv, w_o = gathered
    return (_w_uq_pad(_cols_from_shards(g_uq)), w_ukv, _cols_from_shards(g_conv), w_mkv, w_o)


def _layer_fwd(l, x, mem, wts, gains, tabs, comm, late=None, h=None):
    wt_in = wts[0]
    g_pre, g_q, g_kv, g_mem, g_post = gains
    q_tab, k_tab = tabs
    tag = f"l{l}_"
    if h is None:
        h = _rmsnorm_fwd(x, g_pre, 0, D_MODEL, tag + "pre_norm")
    z = _matmul(h, wt_in, "nt", BF16, tag + "in_proj", tm_cap=1024, tn_cap=1664)
    w_uq, w_ukv, conv_w, w_mkv, w_o = wts[1:] if late is None else late(z)
    wts = (wt_in, w_uq, w_ukv, conv_w, w_mkv, w_o)
    qn, kvn, kpe = _latent_prep(z, g_q, g_kv, *k_tab, tag + "latent_prep")
    q_raw = _matmul(qn, w_uq, "nn", BF16, tag + "uq", tm_cap=1024)
    kv = _matmul(kvn, w_ukv, "nn", BF16, tag + "ukv")
    cat, a_lse, arrived = _attn_fwd(q_raw, kv, kpe, kv, q_tab, MLA_HEADS, QPAD, 0, 0, 2, 1, 2, QK_HEAD ** -0.5, 512,
                                    tag + "mla_fwd", comm, o_into=(MIX_W, 0, None))
    cat = _conv_fwd(z, conv_w, cat, tag + "conv_fwd")
    mem_n = _rmsnorm_fwd(mem, g_mem, 0, D_MODEL, tag + "mem_norm")
    mkv = _matmul(mem_n, w_mkv, "nn", BF16, tag + "mem_kv")
    cat, m_lse, _ = _attn_fwd(z, mkv, None, mkv, None, MEM_HEADS, LANES, Z_QMEM // LANES, 0, 1, MEM_HEADS, 1,
                              MEM_HEAD ** -0.5, 1024, tag + "mem_fwd", o_into=(MIX_W, (MLA_W + CONV_W) // LANES, cat))
    y = _gate_fwd(cat, z, tag + "gate_fwd")
    o = _matmul(y, w_o, "nn", BF16, tag + "out_proj", tm_cap=1024)
    x_new = _post_norm_residual(x, o, g_post, tag + "post_norm")
    saved = (x, h, z, qn, kvn, q_raw, kv, kpe, a_lse, mem_n, mkv, m_lse, cat, y, o)
    return x_new, saved, arrived


def _layer_bwd(l, g, mem, saved, wts, gains, tabs_bwd, comm, split_exchange=False):
    wt_in, w_uq, w_ukv, conv_w, w_mkv, w_o = wts
    g_pre, g_q, g_kv, g_mem, g_post = gains
    q_tab, k_tab = tabs_bwd
    x, h, z, qn, kvn, q_raw, kv, kpe, a_lse, mem_n, mkv, m_lse, cat, y, o = saved
    tag = f"l{l}_"
    do, dg_post = _rmsnorm_bwd(o, g_post, g, None, 0, D_MODEL, BF16, tag + "post_norm_bwd")
    dcat, dgate = _out_proj_dx_gate_bwd(do, w_o, cat, z, tag + "out_proj_dx")
    dw_o = _matmul(y, do, "tn", BF16, tag + "out_proj_dw", tm_cap=1024)
    dq, dkv, dkpe_h, arrived = _attn_bwd(q_raw, kv, kpe, kv, cat, dcat, a_lse, q_tab, MLA_HEADS, QPAD, 0, 0, 2, 1, 2, 0,
                                         QK_HEAD ** -0.5, 512, tag + "mla_bwd", comm)
    dw_ukv = _matmul(kvn, dkv, "tn", BF16, tag + "ukv_dw")
    dkvn = _matmul(dkv, w_ukv, "nt", BF16, tag + "ukv_dx")
    dw_uq = _matmul(qn, dq, "tn", BF16, tag + "uq_dw")
    dqn = _matmul(dq, w_uq, "nt", BF16, tag + "uq_dx")
    dq_lat, dkv_lat, dkpe, dg_q, dg_kv = _latent_prep_bwd(z, g_q, g_kv, dqn, dkvn, dkpe_h, *k_tab, MLA_HEADS,
                                                          tag + "latent_prep_bwd")
    dgb, dgc, dxin, dconv_w = _conv_bwd(z, conv_w, dcat, tag + "conv_bwd")
    dq_mem, dmk, dmv, _ = _attn_bwd(z, mkv, None, mkv, cat, dcat, m_lse, None, MEM_HEADS, LANES, Z_QMEM // LANES, 0, 1,
                                    MEM_HEADS, 1, (MLA_W + CONV_W) // LANES, MEM_HEAD ** -0.5, 1024, tag + "mem_bwd")
    dmkv = jnp.concatenate([dmk, dmv], axis=-1)
    dw_mkv = _matmul(mem_n, dmkv, "tn", BF16, tag + "mem_kv_dw")
    dmem_n = _matmul(dmkv, w_mkv, "nt", F32, tag + "mem_kv_dx")
    _, dg_mem = _rmsnorm_bwd(mem, g_mem, dmem_n, None, 0, D_MODEL, BF16, tag + "mem_norm_bwd")
    others = (_cols_to_shards(_w_uq_unpad(dw_uq)), dw_ukv, _cols_to_shards(dconv_w), dw_mkv, dw_o)
    early = None
    if split_exchange:
        early_plan = _scatter_plan(None, *others, part="rest")
        early = (early_plan, _comm_start(early_plan, [dmem_n], tag + "exchange_rest_start"))
        g_pre = g_pre + early[1][2][0:1, 0:1]
    dz = jnp.concatenate([dgate, dq_lat, dkv_lat, dkpe, dgb, dgc, dxin, dq_mem], axis=-1)
    dwt_in = _matmul(dz, h, "tn", BF16, tag + "in_proj_dw", tm_cap=1664, tk_cap=2048)
    contrib = _scatter_plan(dwt_in, *others, part="in" if split_exchange else "all")
    late = None
    if split_exchange:
        late = (contrib, _comm_start(contrib, [dwt_in], tag + "exchange_in_start"))
    dh = _matmul(dz, wt_in, "nn", BF16, tag + "in_proj_dx", tm_cap=1024, tk_cap=1664, after=late[1][2] if late else None)
    dx, dg_pre = _rmsnorm_bwd(x, g_pre, dh, g, 0, D_MODEL, F32, tag + "pre_norm_bwd")
    return dx, contrib, (dg_pre, dg_q, dg_kv, dg_mem, dg_post), (early, late)


GAIN_WIDTHS = (D_MODEL, Q_RANK, KV_RANK, D_MODEL, D_MODEL)


def _pack_gains(parts):
    return jnp.concatenate([p.reshape(-1) for p in parts]).reshape(-1, LANES)


def _unpack_gains(packed, depth):
    flat = packed.reshape(-1)
    out, at = [], 0
    for width in GAIN_WIDTHS:
        out.append(flat[at:at + depth * width].reshape(depth, width))
        at += depth * width
    return out


def kernel(x, mem, positions, pre_norm_g, w_in, q_norm_g, w_uq, kv_norm_g, w_ukv, conv_w, mem_norm_g, w_mk, w_mv, w_o, post_norm_g, loss_target, m_pre_norm_g, m_w_in, m_q_norm_g, m_w_uq, m_kv_norm_g, m_w_ukv, m_conv_w, m_mem_norm_g, m_w_mk, m_w_mv, m_w_o, m_post_norm_g, v_pre_norm_g, v_w_in, v_q_norm_g, v_w_uq, v_kv_norm_g, v_w_ukv, v_conv_w, v_mem_norm_g, v_w_mk, v_w_mv, v_w_o, v_post_norm_g):
    depth = w_in.shape[0]
    x0, mem0, target = x[0], mem[0], loss_target[0]
    tabs = _rope_tables(positions[0])
    tabs_bwd = tabs

    flip = lambda t: jnp.transpose(t, (0, 2, 1))
    w_in, m_w_in, v_w_in = flip(w_in), flip(m_w_in), flip(v_w_in)
    shards = [w_in.astype(BF16), w_uq.astype(BF16), w_ukv.astype(BF16), conv_w, w_mk.astype(BF16), w_mv.astype(BF16),
              w_o.astype(BF16)]
    zero_rows = lambda: jnp.zeros((LANES - ROPE, D_MODEL), BF16)

    def layer_gains(l):
        return tuple(g[l][None, :] for g in (pre_norm_g, q_norm_g, kv_norm_g, mem_norm_g, post_norm_g))

    wts, saved = [None] * depth, [None] * depth
    plan_in, plan_rest = _gather_plan(0, shards, zero_rows(), "in"), _gather_plan(0, shards, zero_rows(), "rest")
    started_in = _comm_start(plan_in, [positions], "l0_gather_in_start")
    started_rest = _comm_start(plan_rest, [started_in[2]], "l0_gather_rest_start")
    h0 = _rmsnorm_fwd(x0, layer_gains(0)[0], 0, D_MODEL, "l0_pre_norm")
    wts[0] = tuple(_comm_finish(plan_in, started_in, [started_rest[2], h0, tabs[0][0]], "l0_gather_in_wait"))

    next_gather = {}

    def start_next_gather(l, after):
        plan = _gather_plan(l + 1, shards, zero_rows())
        next_gather[l + 1] = (plan, _comm_start(plan, [after], f"l{l + 1}_gather_start"))
        return next_gather[l + 1][1][2][0:1, 0:1]

    def rest_of_layer0(z):
        got = _late_weights(_comm_finish(plan_rest, started_rest, [z], "l0_gather_rest_wait"))
        wts[0] = wts[0] + got
        if depth > 1:
            got = (got[0] + start_next_gather(0, got[4]).astype(BF16),) + got[1:]
        return got

    act = x0
    for l in range(depth):
        gains = layer_gains(l)
        if 0 < l < depth - 1:
            gains = (gains[0] + start_next_gather(l, wts[l][5]),) + gains[1:]
        act, saved[l], _ = _layer_fwd(l, act, mem0, wts[l], gains, tabs, None, rest_of_layer0 if l == 0 else None,
                                      h0 if l == 0 else None)
        if l + 1 < depth:
            plan, started = next_gather[l + 1]
            wts[l + 1] = _layer_weights(_comm_finish(plan, started, [act], f"l{l + 1}_gather_wait"))
    grad, loss_part = _loss_head(act, target, "loss_head")
    loss = lax.psum(loss_part[0, 0], ("x", "y", "c"))

    names = ("w_in", "w_uq", "w_ukv", "conv_w", "w_mk", "w_mv", "w_o")
    w_shards = (w_in, w_uq, w_ukv, conv_w, w_mk, w_mv, w_o)
    m_shards = (m_w_in, m_w_uq, m_w_ukv, m_conv_w, m_w_mk, m_w_mv, m_w_o)
    v_shards = (v_w_in, v_w_uq, v_w_ukv, v_conv_w, v_w_mk, v_w_mv, v_w_o)
    stacked = [None] * len(names)

    def sum_and_send(l, received):
        partial = [_sum_slots(r, f"l{l}_grad_sum_{names[i]}") for i, r in enumerate(received)]
        plan = _sibling_plan(partial)
        return l, partial, plan, _comm_start(plan, [partial[0]], f"l{l}_sibling_start")

    def receive_and_update(state, after):
        l, partial, plan, started = state
        other = _comm_finish(plan, started, [after], f"l{l}_sibling_wait")
        for i, name in enumerate(names):
            stacked[i] = _adamw_layer(l, w_shards[i], partial[i], other[i], m_shards[i], v_shards[i], stacked[i],
                                      f"l{l}_adamw_{name}")

    dgs = [None] * depth
    pending = None
    in_flight = None
    for l in reversed(range(depth)):
        gains = layer_gains(l)
        for token in ([pending[1][2]] if pending else []) + ([in_flight[3][2]] if in_flight else []):
            gains = gains[:4] + (gains[4] + token[0:1, 0:1],)
        grad, contrib, dgs[l], early = _layer_bwd(l, grad, mem0, saved[l], wts[l], gains, tabs_bwd, None, l == 0)
        if in_flight is not None:
            receive_and_update(in_flight, grad)
            in_flight = None
        if pending is not None:
            in_flight = sum_and_send(l + 1, _comm_finish(pending[0], pending[1], [grad], f"l{l + 1}_exchange_wait"))
        if l > 0:
            pending = (contrib, _comm_start(contrib, [grad], f"l{l}_exchange_start"))
    early, late = early
    got_in = _comm_finish(late[0], late[1], [grad], "l0_exchange_in_wait")
    last = sum_and_send(0, got_in + _comm_finish(early[0], early[1], [got_in[0]], "l0_exchange_rest_wait"))
    if in_flight is not None:
        receive_and_update(in_flight, last[1][0])
    receive_and_update(last, stacked[0][0] if depth > 1 else last[1][0])
    grad_x = grad[None]
    results = {name: tuple(stacked[i]) for i, name in enumerate(names)}
    results["w_in"] = tuple(flip(t) for t in results["w_in"])

    gain_names = ("pre_norm_g", "q_norm_g", "kv_norm_g", "mem_norm_g", "post_norm_g")
    dg_packed = _pack_gains([jnp.concatenate([dgs[l][i] for l in range(depth)], axis=0) for i in range(5)])
    dg_total = _sum_slots(_gather_all(dg_packed, stacked[0][0], "gain_gather"), "gain_sum")
    gain_outs = (dg_total,) + tuple(_adamw(
        _pack_gains((pre_norm_g, q_norm_g, kv_norm_g, mem_norm_g, post_norm_g)), dg_total,
        _pack_gains((m_pre_norm_g, m_q_norm_g, m_kv_norm_g, m_mem_norm_g, m_post_norm_g)),
        _pack_gains((v_pre_norm_g, v_q_norm_g, v_kv_norm_g, v_mem_norm_g, v_post_norm_g)), "adamw_gains"))
    gain_outs = [_unpack_gains(t, depth) for t in gain_outs]
    for i, name in enumerate(gain_names):
        results[name] = tuple(gain_outs[k][i] for k in range(4))

    order = ("pre_norm_g", "w_in", "q_norm_g", "w_uq", "kv_norm_g", "w_ukv", "conv_w", "mem_norm_g", "w_mk", "w_mv", "w_o",
             "post_norm_g")
    out = [loss, grad_x]
    for k in range(4):
        out += [results[name][k] for name in order]
    return tuple(out)
```

```python
import functools

import jax
import jax.numpy as jnp
from jax import lax
from jax.experimental import pallas as pl
from jax.experimental.pallas import tpu as pltpu

F32 = jnp.float32
BF16 = jnp.bfloat16
MESH_ID = pl.DeviceIdType.MESH

D_MODEL = 2048
EPS = 1e-6
LOG2_E = 1.4426950408889634
ROPE_THETA = 10000.0
MLA_HEADS = 8
NOPE = 128
ROPE = 64
HALF_ROPE = ROPE // 2
QK_HEAD = NOPE + ROPE
V_HEAD = 128
Q_RANK = 512
KV_RANK = 256
CONV_W = 512
MEM_HEADS = 4
MEM_HEAD = 128
MEM_W = MEM_HEADS * MEM_HEAD
MLA_W = MLA_HEADS * V_HEAD
MIX_W = MLA_W + CONV_W + MEM_W
IN_COLS = Q_RANK + KV_RANK + ROPE + 3 * CONV_W + MEM_W + MIX_W
N_CHIPS = 4
N_DEV = 8

LANES = 128
VMEM_LIMIT_BYTES = 56 * 1024 * 1024

QPAD = 2 * LANES
Z_GATE = 0
Z_QLAT = Z_GATE + MIX_W
Z_KVLAT = Z_QLAT + Q_RANK
Z_KPE = Z_KVLAT + KV_RANK
Z_GB = Z_KPE + LANES
Z_GC = Z_GB + CONV_W
Z_XIN = Z_GC + CONV_W
Z_QMEM = Z_XIN + CONV_W
Z_COLS = Z_QMEM + MEM_W

ADAM_LR = 0.001
ADAM_B1 = 0.9
ADAM_B2 = 0.999
ADAM_EPS = 1e-08
ADAM_WD = 0.01
ADAM_STEP = 10


def _tile(dim, cap, unit):
    if dim <= cap:
        return dim
    t = (cap // unit) * unit
    while t >= unit:
        if dim % t == 0:
            return t
        t -= unit
    raise ValueError(f"no tile of {dim} under {cap} in units of {unit}")


def _params(*semantics):
    return pltpu.CompilerParams(dimension_semantics=semantics, vmem_limit_bytes=VMEM_LIMIT_BYTES)


def _matmul(a, b, mode, out_dtype, name, tm_cap=512, tn_cap=1024, tk_cap=2048, after=None):
    if mode == "nn":
        (m, k), (k2, n) = a.shape, b.shape
    elif mode == "nt":
        (m, k), (n, k2) = a.shape, b.shape
    else:
        (k, m), (k2, n) = a.shape, b.shape
    assert k == k2, (a.shape, b.shape, mode)
    tm = _tile(m, tm_cap, LANES if mode == "tn" else 16)
    tn = _tile(n, tn_cap, LANES)
    tk = _tile(k, tk_cap, LANES if mode != "tn" else 16)
    nk = k // tk
    if mode == "nn":
        a_spec = pl.BlockSpec((tm, tk), lambda i, j, kk: (i, kk))
        b_spec = pl.BlockSpec((tk, tn), lambda i, j, kk: (kk, j))
        dims = (((1,), (0,)), ((), ()))
    elif mode == "nt":
        a_spec = pl.BlockSpec((tm, tk), lambda i, j, kk: (i, kk))
        b_spec = pl.BlockSpec((tn, tk), lambda i, j, kk: (j, kk))
        dims = (((1,), (1,)), ((), ()))
    else:
        a_spec = pl.BlockSpec((tk, tm), lambda i, j, kk: (kk, i))
        b_spec = pl.BlockSpec((tk, tn), lambda i, j, kk: (kk, j))
        dims = (((0,), (0,)), ((), ()))

    def body(a_ref, b_ref, *rest):
        o_ref, scratch = (rest[1], rest[2:]) if after is not None else (rest[0], rest[1:])
        part = lax.dot_general(a_ref[...].astype(BF16), b_ref[...].astype(BF16), dims, preferred_element_type=F32)
        if nk == 1:
            o_ref[...] = part.astype(o_ref.dtype)
            return
        (acc_ref,) = scratch
        kk = pl.program_id(2)

        @pl.when(kk == 0)
        def _():
            acc_ref[...] = part

        @pl.when(kk > 0)
        def _():
            acc_ref[...] += part

        @pl.when(kk == nk - 1)
        def _():
            o_ref[...] = acc_ref[...].astype(o_ref.dtype)

    return pl.pallas_call(
        body,
        grid=(m // tm, n // tn, nk),
        in_specs=[a_spec, b_spec] + ([] if after is None else [pl.BlockSpec(memory_space=pl.ANY)]),
        out_specs=pl.BlockSpec((tm, tn), lambda i, j, kk: (i, j)),
        out_shape=jax.ShapeDtypeStruct((m, n), out_dtype),
        scratch_shapes=[] if nk == 1 else [pltpu.VMEM((tm, tn), F32)],
        compiler_params=_params("parallel", "parallel", "arbitrary"),
        name=name,
    )(*([a, b] if after is None else [a, b, after]))


def _rmsnorm_fwd(x, gain, col0, width, name):
    rows = x.shape[0]
    tr = _tile(rows, 512, 16)
    cb = col0 // width
    assert cb * width == col0

    def body(x_ref, g_ref, o_ref):
        xv = x_ref[...].astype(F32)
        r = lax.rsqrt(jnp.mean(xv * xv, axis=-1, keepdims=True) + EPS)
        o_ref[...] = (xv * r * g_ref[...]).astype(o_ref.dtype)

    return pl.pallas_call(
        body,
        grid=(rows // tr,),
        in_specs=[pl.BlockSpec((tr, width), lambda i: (i, cb)), pl.BlockSpec((1, width), lambda i: (0, 0))],
        out_specs=pl.BlockSpec((tr, width), lambda i: (i, 0)),
        out_shape=jax.ShapeDtypeStruct((rows, width), BF16),
        compiler_params=_params("parallel"),
        name=name,
    )(x, gain)


def _rmsnorm_bwd(x, gain, dy, resid, col0, width, out_dtype, name):
    rows = x.shape[0]
    tr = _tile(rows, 256, 16)
    cb = col0 // width
    assert cb * width == col0
    has_resid = resid is not None

    def body(*refs):
        if has_resid:
            x_ref, g_ref, dy_ref, res_ref, dx_ref, dg_ref = refs
        else:
            x_ref, g_ref, dy_ref, dx_ref, dg_ref = refs
        i = pl.program_id(0)
        xv = x_ref[...].astype(F32)
        dyv = dy_ref[...].astype(F32)
        r = lax.rsqrt(jnp.mean(xv * xv, axis=-1, keepdims=True) + EPS)
        xr = xv * r
        dyg = dyv * g_ref[...]
        c = jnp.mean(dyg * xr, axis=-1, keepdims=True)
        dx = r * (dyg - xr * c)
        if has_resid:
            dx = dx + res_ref[...]
        dx_ref[...] = dx.astype(dx_ref.dtype)
        part = jnp.sum(dyv * xr, axis=0, keepdims=True)

        @pl.when(i == 0)
        def _():
            dg_ref[...] = part

        @pl.when(i > 0)
        def _():
            dg_ref[...] += part

    row_spec = pl.BlockSpec((tr, width), lambda i: (i, 0))
    in_specs = [pl.BlockSpec((tr, width), lambda i: (i, cb)), pl.BlockSpec((1, width), lambda i: (0, 0)), row_spec]
    args = [x, gain, dy]
    if has_resid:
        in_specs.append(row_spec)
        args.append(resid)
    return pl.pallas_call(
        body,
        grid=(rows // tr,),
        in_specs=in_specs,
        out_specs=[row_spec, pl.BlockSpec((1, width), lambda i: (0, 0))],
        out_shape=[jax.ShapeDtypeStruct((rows, width), out_dtype), jax.ShapeDtypeStruct((1, width), F32)],
        compiler_params=_params("arbitrary"),
        name=name,
    )(*args)


def _post_norm_residual(x, o, gain, next_gain, name):
    rows, width = x.shape
    tr = _tile(rows, 256, 16)
    with_next = next_gain is not None

    def body(*refs):
        x_ref, o_ref, g_ref = refs[:3]
        ov = o_ref[...].astype(F32)
        r = lax.rsqrt(jnp.mean(ov * ov, axis=-1, keepdims=True) + EPS)
        x_new = x_ref[...] + ov * r * g_ref[...]
        if with_next:
            gn_ref, out_ref, h_ref = refs[3:]
            rn = lax.rsqrt(jnp.mean(x_new * x_new, axis=-1, keepdims=True) + EPS)
            h_ref[...] = (x_new * rn * gn_ref[...]).astype(h_ref.dtype)
        else:
            (out_ref,) = refs[3:]
        out_ref[...] = x_new

    row_spec = pl.BlockSpec((tr, width), lambda i: (i, 0))
    gain_spec = pl.BlockSpec((1, width), lambda i: (0, 0))
    res = pl.pallas_call(
        body,
        grid=(rows // tr,),
        in_specs=[row_spec, row_spec, gain_spec] + ([gain_spec] if with_next else []),
        out_specs=[row_spec] + ([row_spec] if with_next else []),
        out_shape=[jax.ShapeDtypeStruct((rows, width), F32)] + ([jax.ShapeDtypeStruct((rows, width), BF16)] if with_next else []),
        compiler_params=_params("parallel"),
        name=name,
    )(*([x, o, gain] + ([next_gain] if with_next else [])))
    return (res[0], res[1]) if with_next else (res[0], None)


def _latent_prep(z, g_q, g_kv, tab_c, tab_a, tab_b, name):
    rows = z.shape[0]
    tr = _tile(rows, 512, 16)

    def norm(x_ref, g_ref, o_ref):
        xv = x_ref[...].astype(F32)
        r = lax.rsqrt(jnp.mean(xv * xv, axis=-1, keepdims=True) + EPS)
        o_ref[...] = (xv * r * g_ref[...]).astype(o_ref.dtype)

    def body(q_ref, kv_ref, k_ref, gq_ref, gkv_ref, c_ref, a_ref, b_ref, qn_ref, kvn_ref, kpe_ref):
        norm(q_ref, gq_ref, qn_ref)
        norm(kv_ref, gkv_ref, kvn_ref)
        kpe_ref[...] = _rope_rows(k_ref[...].astype(F32), c_ref[...], a_ref[...], b_ref[...], 1).astype(kpe_ref.dtype)

    window = lambda c0, width: pl.BlockSpec((tr, width), lambda i: (i, c0 // width))
    gain = lambda width: pl.BlockSpec((1, width), lambda i: (0, 0))
    tab = pl.BlockSpec((tr, LANES), lambda i: (i, 0))
    out = lambda width: pl.BlockSpec((tr, width), lambda i: (i, 0))
    return pl.pallas_call(
        body,
        grid=(rows // tr,),
        in_specs=[window(Z_QLAT, Q_RANK), window(Z_KVLAT, KV_RANK), window(Z_KPE, LANES), gain(Q_RANK), gain(KV_RANK), tab, tab, tab],
        out_specs=[out(Q_RANK), out(KV_RANK), out(LANES)],
        out_shape=[jax.ShapeDtypeStruct((rows, Q_RANK), BF16), jax.ShapeDtypeStruct((rows, KV_RANK), BF16),
                   jax.ShapeDtypeStruct((rows, LANES), BF16)],
        compiler_params=_params("parallel"),
        name=name,
    )(z, z, z, g_q, g_kv, tab_c, tab_a, tab_b)


def _latent_prep_bwd(z, g_q, g_kv, dqn, dkvn, dkb, tab_c, tab_a, tab_b, heads, name):
    rows = z.shape[0]
    tr = _tile(rows, 256, 16)

    def norm_bwd(x_ref, g_ref, dy_ref, dx_ref, dg_ref, i):
        xv = x_ref[...].astype(F32)
        dyv = dy_ref[...].astype(F32)
        r = lax.rsqrt(jnp.mean(xv * xv, axis=-1, keepdims=True) + EPS)
        xr = xv * r
        dyg = dyv * g_ref[...]
        c = jnp.mean(dyg * xr, axis=-1, keepdims=True)
        dx_ref[...] = (r * (dyg - xr * c)).astype(dx_ref.dtype)
        part = jnp.sum(dyv * xr, axis=0, keepdims=True)

        @pl.when(i == 0)
        def _():
            dg_ref[...] = part

        @pl.when(i > 0)
        def _():
            dg_ref[...] += part

    def body(q_ref, kv_ref, gq_ref, gkv_ref, dqn_ref, dkvn_ref, d_ref, c_ref, a_ref, b_ref,
             dq_ref, dkv_ref, dkpe_ref, dgq_ref, dgkv_ref):
        i = pl.program_id(0)
        norm_bwd(q_ref, gq_ref, dqn_ref, dq_ref, dgq_ref, i)
        norm_bwd(kv_ref, gkv_ref, dkvn_ref, dkv_ref, dgkv_ref, i)
        acc = d_ref[:, 0:LANES]
        for h in range(1, heads):
            acc = acc + d_ref[:, h * LANES:(h + 1) * LANES]
        dkpe_ref[...] = _rope_rows(acc, c_ref[...], a_ref[...], b_ref[...], -1).astype(dkpe_ref.dtype)

    window = lambda c0, width: pl.BlockSpec((tr, width), lambda i: (i, c0 // width))
    gain = lambda width: pl.BlockSpec((1, width), lambda i: (0, 0))
    rows_of = lambda width: pl.BlockSpec((tr, width), lambda i: (i, 0))
    return pl.pallas_call(
        body,
        grid=(rows // tr,),
        in_specs=[window(Z_QLAT, Q_RANK), window(Z_KVLAT, KV_RANK), gain(Q_RANK), gain(KV_RANK), rows_of(Q_RANK), rows_of(KV_RANK),
                  rows_of(heads * LANES), rows_of(LANES), rows_of(LANES), rows_of(LANES)],
        out_specs=[rows_of(Q_RANK), rows_of(KV_RANK), rows_of(LANES), gain(Q_RANK), gain(KV_RANK)],
        out_shape=[jax.ShapeDtypeStruct((rows, Q_RANK), BF16), jax.ShapeDtypeStruct((rows, KV_RANK), BF16),
                   jax.ShapeDtypeStruct((rows, LANES), BF16), jax.ShapeDtypeStruct((1, Q_RANK), F32),
                   jax.ShapeDtypeStruct((1, KV_RANK), F32)],
        compiler_params=_params("arbitrary"),
        name=name,
    )(z, z, g_q, g_kv, dqn, dkvn, dkb, tab_c, tab_a, tab_b)


class _CommPlan:
    def __init__(self, ins, out_shape, build, n_copies):
        self.ins, self.out_shape, self.build, self.n_copies = list(ins), list(out_shape), build, n_copies

    def scratch(self):
        n = self.n_copies
        return [pltpu.SemaphoreType.DMA((n,)), pltpu.SemaphoreType.DMA((n,)), pltpu.SemaphoreType.DMA((n,))]


def _split_comm(refs, n_in, n_out, comm):
    if comm is None:
        return refs, None
    ci, co = len(comm.ins), len(comm.out_shape)
    ins, c_ins = refs[:n_in], refs[n_in:n_in + ci]
    outs, c_outs = refs[n_in + ci:n_in + ci + n_out], refs[n_in + ci + n_out:n_in + ci + n_out + co]
    rest = refs[n_in + ci + n_out + co:]
    scratch, sems = rest[:-3], rest[-3:]
    return tuple(ins) + tuple(outs) + tuple(scratch), functools.partial(comm.build, c_ins, c_outs, sems)


def _ride_start(copies, first):
    if copies is not None:
        pl.when(first)(copies()[0])


def _ride_wait(copies, last):
    if copies is not None:
        pl.when(last)(copies()[1])


def _rope_rows(x, c, a, b, sign):
    width = x.shape[-1]
    mixed = pltpu.roll(x, width - HALF_ROPE, 1) * a + pltpu.roll(x, HALF_ROPE, 1) * b
    return x * c + mixed if sign > 0 else x * c - mixed


def _attn_fwd(q, ka, kb, v, rope, heads, q_w, q_cb, ka_cb, ka_step, v_cb, v_step, scale, tq_cap, name, comm=None, tk_cap=512,
              o_into=None):
    s_q, s_k = q.shape[0], ka.shape[0]
    tq = _tile(s_q, tq_cap, 16)
    nq = s_q // tq
    has_kb = kb is not None
    n_in = 7 if has_kb else 3
    tk = _tile(s_k, tk_cap, LANES)
    o_cols, o_cb, o_old = o_into if o_into is not None else (heads * LANES, 0, None)
    assert comm is None or o_old is None

    def body(*refs):
        if o_old is not None:
            refs = refs[:n_in] + refs[n_in + 1:]
        refs, copies = _split_comm(refs, n_in, 2, comm)
        first = jnp.logical_and(pl.program_id(0) == 0, pl.program_id(1) == 0)
        last = jnp.logical_and(pl.program_id(0) == heads - 1, pl.program_id(1) == nq - 1)
        _ride_start(copies, first)
        if has_kb:
            q_ref, ka_ref, kb_ref, v_ref, c_ref, a_ref, b_ref, o_ref, lse_ref, k_scr = refs

            @pl.when(pl.program_id(1) == 0)
            def _():
                k_scr[:, 0:LANES] = ka_ref[...].astype(BF16)
                k_scr[:, LANES:2 * LANES] = kb_ref[...].astype(BF16)

            keys = k_scr
            qv = _rope_rows(q_ref[...].astype(F32), c_ref[...], a_ref[...], b_ref[...], 1).astype(BF16)
        else:
            q_ref, ka_ref, v_ref, o_ref, lse_ref = refs
            keys = ka_ref
            qv = q_ref[...].astype(BF16)
        c2 = scale * LOG2_E
        m = l = o = None
        nk = s_k // tk
        scores = lambda j: lax.dot_general(qv, keys[j * tk:(j + 1) * tk, :].astype(BF16), (((1,), (1,)), ((), ())),
                                           preferred_element_type=F32)
        s_next = scores(0)
        for j in range(nk):
            sj = s_next
            if j + 1 < nk:
                s_next = scores(j + 1)
            mj = jnp.max(sj, axis=-1, keepdims=True)
            m_new = mj if m is None else jnp.maximum(m, mj)
            pj = jnp.exp2((sj - m_new) * c2)
            lj = jnp.sum(pj, axis=-1, keepdims=True)
            oj = jnp.dot(pj.astype(BF16), v_ref[j * tk:(j + 1) * tk, :].astype(BF16), preferred_element_type=F32)
            if m is None:
                l, o = lj, oj
            else:
                alpha = jnp.exp2((m - m_new) * c2)
                l, o = l * alpha + lj, o * alpha + oj
            m = m_new
        o_ref[...] = (o * (1.0 / l)).astype(o_ref.dtype)
        lse_ref[...] = jnp.broadcast_to(m * c2 + jnp.log2(l), lse_ref.shape)
        _ride_wait(copies, last)

    in_specs = [pl.BlockSpec((tq, q_w), lambda h, i: (i, q_cb + h)),
                pl.BlockSpec((s_k, LANES), lambda h, i: (0, ka_cb + ka_step * h))]
    args = [q, ka]
    if has_kb:
        in_specs.append(pl.BlockSpec((s_k, LANES), lambda h, i: (0, 0)))
        args.append(kb)
    in_specs.append(pl.BlockSpec((s_k, LANES), lambda h, i: (0, v_cb + v_step * h)))
    args.append(v)
    if has_kb:
        in_specs += [pl.BlockSpec((tq, q_w), lambda h, i: (i, 0))] * 3
        args += list(rope)
    aliases = {}
    if o_old is not None:
        aliases = {len(args): 0}
        in_specs.append(ANY)
        args.append(o_old)
    out_specs = [pl.BlockSpec((tq, LANES), lambda h, i: (i, o_cb + h)), pl.BlockSpec((tq, LANES), lambda h, i: (i, h))]
    out_shape = [jax.ShapeDtypeStruct((s_q, o_cols), BF16), jax.ShapeDtypeStruct((s_q, heads * LANES), F32)]
    scratch = [pltpu.VMEM((s_k, 2 * LANES), BF16)] if has_kb else []
    if comm is not None:
        in_specs += [ANY] * len(comm.ins)
        args += comm.ins
        out_specs += [ANY] * len(comm.out_shape)
        out_shape += comm.out_shape
        scratch += comm.scratch()
    res = pl.pallas_call(
        body,
        grid=(heads, nq),
        in_specs=in_specs,
        out_specs=out_specs,
        out_shape=out_shape,
        scratch_shapes=scratch,
        input_output_aliases=aliases,
        compiler_params=_params("arbitrary", "arbitrary"),
        name=name,
    )(*args)
    return res[0], res[1], list(res[2:])


def _attn_bwd(q, ka, kb, v, o, do, lse, rope, heads, q_w, q_cb, ka_cb, ka_step, v_cb, v_step, o_cb, scale, tq_cap, name,
              comm=None, tk_cap=512):
    s_q, s_k = q.shape[0], ka.shape[0]
    tq = _tile(s_q, tq_cap, 16)
    nq = s_q // tq
    has_kb = kb is not None
    n_in = 10 if has_kb else 6
    n_out = 3
    tk = _tile(s_k, tk_cap, LANES)

    def body(*refs):
        refs, copies = _split_comm(refs, n_in, n_out, comm)
        first = jnp.logical_and(pl.program_id(0) == 0, pl.program_id(1) == 0)
        last = jnp.logical_and(pl.program_id(0) == heads - 1, pl.program_id(1) == nq - 1)
        _ride_start(copies, first)
        if has_kb:
            (q_ref, ka_ref, kb_ref, v_ref, o_ref, do_ref, lse_ref, c_ref, a_ref, b_ref, dq_ref, dkv_ref, dkb_ref, k_scr, dk_acc,
             dv_acc) = refs
        else:
            q_ref, ka_ref, v_ref, o_ref, do_ref, lse_ref, dq_ref, dka_ref, dv_ref, dk_acc, dv_acc = refs
        i = pl.program_id(1)

        @pl.when(i == 0)
        def _():
            dk_acc[...] = jnp.zeros_like(dk_acc)
            dv_acc[...] = jnp.zeros_like(dv_acc)
            if has_kb:
                k_scr[:, 0:LANES] = ka_ref[...].astype(BF16)
                k_scr[:, LANES:2 * LANES] = kb_ref[...].astype(BF16)

        keys = k_scr if has_kb else ka_ref
        if has_kb:
            qv = _rope_rows(q_ref[...].astype(F32), c_ref[...], a_ref[...], b_ref[...], 1).astype(BF16)
        else:
            qv = q_ref[...].astype(BF16)
        dov = do_ref[...].astype(BF16)
        delta = jnp.sum(dov.astype(F32) * o_ref[...].astype(F32), axis=-1, keepdims=True)
        lse2 = lse_ref[:, 0:1]
        c2 = scale * LOG2_E
        nk = s_k // tk
        rows = lambda j: slice(j * tk, (j + 1) * tk)
        nt = (((1,), (1,)), ((), ()))
        tn = (((0,), (0,)), ((), ()))

        def scores(j):
            return (lax.dot_general(qv, keys[rows(j), :].astype(BF16), nt, preferred_element_type=F32),
                    lax.dot_general(dov, v_ref[rows(j), :].astype(BF16), nt, preferred_element_type=F32))

        nxt = scores(0)
        dq = None
        for j in range(nk):
            sj, dpj = nxt
            if j + 1 < nk:
                nxt = scores(j + 1)
            pj = jnp.exp2(sj * c2 - lse2)
            dsj = (pj * (dpj - delta)).astype(BF16)
            dqj = jnp.dot(dsj, keys[rows(j), :].astype(BF16), preferred_element_type=F32)
            dq = dqj if dq is None else dq + dqj
            dk_acc[rows(j), :] += lax.dot_general(dsj, qv, tn, preferred_element_type=F32)
            dv_acc[rows(j), :] += lax.dot_general(pj.astype(BF16), dov, tn, preferred_element_type=F32)
        dq = dq * scale
        if has_kb:
            dq = _rope_rows(dq, c_ref[...], a_ref[...], b_ref[...], -1)
        dq_ref[...] = dq.astype(dq_ref.dtype)

        @pl.when(i == nq - 1)
        def _():
            if has_kb:
                dkv_ref[:, 0:LANES] = (dk_acc[:, 0:LANES] * scale).astype(dkv_ref.dtype)
                dkv_ref[:, LANES:2 * LANES] = dv_acc[...].astype(dkv_ref.dtype)
                dkb_ref[...] = dk_acc[:, LANES:2 * LANES] * scale
            else:
                dka_ref[...] = (dk_acc[...] * scale).astype(dka_ref.dtype)
                dv_ref[...] = dv_acc[...].astype(dv_ref.dtype)

        _ride_wait(copies, last)

    key_spec = lambda cb, step: pl.BlockSpec((s_k, LANES), lambda h, i: (0, cb + step * h))
    row_spec = lambda cb: pl.BlockSpec((tq, LANES), lambda h, i: (i, cb + h))
    in_specs = [pl.BlockSpec((tq, q_w), lambda h, i: (i, q_cb + h)), key_spec(ka_cb, ka_step)]
    args = [q, ka]
    if has_kb:
        in_specs.append(pl.BlockSpec((s_k, LANES), lambda h, i: (0, 0)))
        args.append(kb)
    in_specs += [key_spec(v_cb, v_step), row_spec(o_cb), row_spec(o_cb), row_spec(0)]
    args += [v, o, do, lse]
    if has_kb:
        in_specs += [pl.BlockSpec((tq, q_w), lambda h, i: (i, 0))] * 3
        args += list(rope)
    out_specs = [pl.BlockSpec((tq, q_w), lambda h, i: (i, h))]
    out_shape = [jax.ShapeDtypeStruct((s_q, heads * q_w), BF16)]
    scratch = []
    if has_kb:
        out_specs += [pl.BlockSpec((s_k, 2 * LANES), lambda h, i: (0, h)), key_spec(0, 1)]
        out_shape += [jax.ShapeDtypeStruct((s_k, heads * 2 * LANES), BF16), jax.ShapeDtypeStruct((s_k, heads * LANES), F32)]
        scratch.append(pltpu.VMEM((s_k, 2 * LANES), BF16))
    else:
        out_specs += [key_spec(0, 1), key_spec(0, 1)]
        out_shape += [jax.ShapeDtypeStruct((s_k, heads * LANES), BF16)] * 2
    scratch += [pltpu.VMEM((s_k, q_w), F32), pltpu.VMEM((s_k, LANES), F32)]
    if comm is not None:
        in_specs += [ANY] * len(comm.ins)
        args += comm.ins
        out_specs += [ANY] * len(comm.out_shape)
        out_shape += comm.out_shape
        scratch += comm.scratch()
    res = pl.pallas_call(
        body,
        grid=(heads, nq),
        in_specs=in_specs,
        out_specs=out_specs,
        out_shape=out_shape,
        scratch_shapes=scratch,
        compiler_params=_params("arbitrary", "arbitrary"),
        name=name,
    )(*args)
    return res[0], res[1], res[2], list(res[3:])


def _shift_rows(u, rows):
    t = lax.broadcasted_iota(jnp.int32, u.shape, 0)
    prev = jnp.where(t == 0, 0.0, pltpu.roll(u, 1, 0))
    nxt = jnp.where(t == rows - 1, 0.0, pltpu.roll(u, rows - 1, 0))
    return prev, nxt


def _conv_fwd(z, conv_w, cat, name):
    rows = z.shape[0]
    nblk = CONV_W // LANES

    def body(gb_ref, gc_ref, xin_ref, w_ref, cat_ref, o_ref):
        del cat_ref
        u = gc_ref[...].astype(F32) * xin_ref[...].astype(F32)
        prev, nxt = _shift_rows(u, rows)
        conv = prev * w_ref[0:1, :] + u * w_ref[1:2, :] + nxt * w_ref[2:3, :]
        o_ref[...] = (gb_ref[...].astype(F32) * conv).astype(o_ref.dtype)

    col = lambda c0: pl.BlockSpec((rows, LANES), lambda j: (0, c0 // LANES + j))
    return pl.pallas_call(
        body,
        grid=(nblk,),
        in_specs=[col(Z_GB), col(Z_GC), col(Z_XIN), pl.BlockSpec((3, LANES), lambda j: (0, j)), ANY],
        out_specs=col(MLA_W),
        out_shape=jax.ShapeDtypeStruct(cat.shape, cat.dtype),
        input_output_aliases={4: 0},
        compiler_params=_params("parallel"),
        name=name,
    )(z, z, z, conv_w, cat)


def _conv_bwd(z, conv_w, dcat, name):
    rows = z.shape[0]
    nblk = CONV_W // LANES

    def body(gb_ref, gc_ref, xin_ref, w_ref, dc_ref, dgb_ref, dgc_ref, dxin_ref, dw_ref):
        gc = gc_ref[...].astype(F32)
        xin = xin_ref[...].astype(F32)
        dc = dc_ref[...].astype(F32)
        u = gc * xin
        prev, nxt = _shift_rows(u, rows)
        w0, w1, w2 = w_ref[0:1, :], w_ref[1:2, :], w_ref[2:3, :]
        conv = prev * w0 + u * w1 + nxt * w2
        dgb_ref[...] = (dc * conv).astype(dgb_ref.dtype)
        dconv = dc * gb_ref[...].astype(F32)
        dw_ref[0:1, :] = jnp.sum(dconv * prev, axis=0, keepdims=True)
        dw_ref[1:2, :] = jnp.sum(dconv * u, axis=0, keepdims=True)
        dw_ref[2:3, :] = jnp.sum(dconv * nxt, axis=0, keepdims=True)
        dprev, dnxt = _shift_rows(dconv, rows)
        du = dnxt * w0 + dconv * w1 + dprev * w2
        dgc_ref[...] = (du * xin).astype(dgc_ref.dtype)
        dxin_ref[...] = (du * gc).astype(dxin_ref.dtype)

    col = lambda c0: pl.BlockSpec((rows, LANES), lambda j: (0, c0 // LANES + j))
    w_spec = pl.BlockSpec((3, LANES), lambda j: (0, j))
    piece = jax.ShapeDtypeStruct((rows, CONV_W), BF16)
    return pl.pallas_call(
        body,
        grid=(nblk,),
        in_specs=[col(Z_GB), col(Z_GC), col(Z_XIN), w_spec, col(MLA_W)],
        out_specs=[col(0), col(0), col(0), w_spec],
        out_shape=[piece, piece, piece, jax.ShapeDtypeStruct((3, CONV_W), F32)],
        compiler_params=_params("parallel"),
        name=name,
    )(z, z, z, conv_w, dcat)


def _gate_fwd(cat, z, name):
    rows = cat.shape[0]
    tr = _tile(rows, 256, 16)
    tc = MIX_W
    g0 = Z_GATE // tc

    def body(c_ref, g_ref, y_ref):
        g = g_ref[...].astype(F32)
        y_ref[...] = (c_ref[...].astype(F32) * (g * jax.nn.sigmoid(g))).astype(y_ref.dtype)

    blk = pl.BlockSpec((tr, tc), lambda i, j: (i, j))
    return pl.pallas_call(
        body,
        grid=(rows // tr, MIX_W // tc),
        in_specs=[blk, pl.BlockSpec((tr, tc), lambda i, j: (i, g0 + j))],
        out_specs=blk,
        out_shape=jax.ShapeDtypeStruct((rows, MIX_W), BF16),
        compiler_params=_params("parallel", "parallel"),
        name=name,
    )(cat, z)


def _out_proj_dx_gate_bwd(do, w_o, cat, z, name):
    rows, k = do.shape
    tm = _tile(rows, 1024, 16)
    tn = _tile(MIX_W, 1024, LANES)
    g0 = Z_GATE // tn

    def body(do_ref, w_ref, c_ref, g_ref, dcat_ref, dgate_ref):
        dy = lax.dot_general(do_ref[...], w_ref[...], (((1,), (1,)), ((), ())), preferred_element_type=F32)
        g = g_ref[...].astype(F32)
        sg = jax.nn.sigmoid(g)
        dcat_ref[...] = (dy * (g * sg)).astype(dcat_ref.dtype)
        dgate_ref[...] = (dy * c_ref[...].astype(F32) * (sg * (1.0 + g * (1.0 - sg)))).astype(dgate_ref.dtype)

    blk = pl.BlockSpec((tm, tn), lambda i, j: (i, j))
    out = jax.ShapeDtypeStruct((rows, MIX_W), BF16)
    return pl.pallas_call(
        body,
        grid=(rows // tm, MIX_W // tn),
        in_specs=[pl.BlockSpec((tm, k), lambda i, j: (i, 0)), pl.BlockSpec((tn, k), lambda i, j: (j, 0)), blk,
                  pl.BlockSpec((tm, tn), lambda i, j: (i, g0 + j))],
        out_specs=[blk, blk],
        out_shape=[out, out],
        compiler_params=_params("parallel", "parallel"),
        name=name,
    )(do, w_o, cat, z)


def _loss_head(y, target, name):
    rows, width = y.shape
    tr = _tile(rows, 256, 8)

    def body(y_ref, t_ref, g_ref, loss_ref):
        i = pl.program_id(0)
        d = y_ref[...] - t_ref[...]
        g_ref[...] = d / width
        part = 0.5 * jnp.sum(jnp.mean(d * d, axis=-1, keepdims=True), axis=0, keepdims=True)
        part = jnp.broadcast_to(part, loss_ref.shape)

        @pl.when(i == 0)
        def _():
            loss_ref[...] = part

        @pl.when(i > 0)
        def _():
            loss_ref[...] += part

    row_spec = pl.BlockSpec((tr, width), lambda i: (i, 0))
    return pl.pallas_call(
        body,
        grid=(rows // tr,),
        in_specs=[row_spec, row_spec],
        out_specs=[row_spec, pl.BlockSpec((1, LANES), lambda i: (0, 0))],
        out_shape=[jax.ShapeDtypeStruct((rows, width), F32), jax.ShapeDtypeStruct((1, LANES), F32)],
        compiler_params=_params("arbitrary"),
        name=name,
    )(y, target)


CHIP_FLIPS = ((1, 0), (0, 1), (1, 1))
ANY = pl.BlockSpec(memory_space=pl.ANY)


def _chip_copies(pieces, sems, n_slot):
    send_sems, recv_sems, local_sems = sems
    x, y, c = lax.axis_index("x"), lax.axis_index("y"), lax.axis_index("c")
    me = 2 * x + y

    def remote(j, k, a, src, dst):
        fx, fy = CHIP_FLIPS[k]
        return pltpu.make_async_remote_copy(
            src_ref=src, dst_ref=dst, send_sem=send_sems.at[n_slot * k + a], recv_sem=recv_sems.at[n_slot * k + a],
            device_id=((j // 2) ^ fx, (j % 2) ^ fy, c), device_id_type=MESH_ID)

    def peer(j, k):
        fx, fy = CHIP_FLIPS[k]
        return 2 * ((j // 2) ^ fx) + ((j % 2) ^ fy)

    def start_as(j):
        def run():
            for a, (src, dst) in enumerate(pieces(j, j)):
                pltpu.make_async_copy(src, dst, local_sems.at[a]).start()
            for k in range(len(CHIP_FLIPS)):
                for a, (src, dst) in enumerate(pieces(j, peer(j, k))):
                    remote(j, k, a, src, dst).start()
        return run

    def wait_as(j):
        def run():
            for a, (src, dst) in enumerate(pieces(j, j)):
                pltpu.make_async_copy(src, dst, local_sems.at[a]).wait()
            for k in range(len(CHIP_FLIPS)):
                for a, (src, dst) in enumerate(pieces(j, peer(j, k))):
                    remote(j, k, a, src, dst).wait_send()
                for a, (src, dst) in enumerate(pieces(peer(j, k), j)):
                    remote(j, k, a, src, dst).wait_recv()
        return run

    def start():
        for j in range(N_CHIPS):
            pl.when(me == j)(start_as(j))

    def wait():
        for j in range(N_CHIPS):
            pl.when(me == j)(wait_as(j))

    return start, wait


IN_PIECES = ((0, Q_RANK, Z_QLAT), (Q_RANK, KV_RANK, Z_KVLAT), (Q_RANK + KV_RANK, ROPE, Z_KPE),
             (Q_RANK + KV_RANK + ROPE, CONV_W, Z_GB), (Q_RANK + KV_RANK + ROPE + CONV_W, CONV_W, Z_GC),
             (Q_RANK + KV_RANK + ROPE + 2 * CONV_W, CONV_W, Z_XIN), (Q_RANK + KV_RANK + ROPE + 3 * CONV_W, MEM_W, Z_QMEM),
             (Q_RANK + KV_RANK + ROPE + 3 * CONV_W + MEM_W, MIX_W, Z_GATE))
IN_SHARD = IN_COLS // N_CHIPS


def _in_segments(j):
    lo, hi = j * IN_SHARD, (j + 1) * IN_SHARD
    segs = []
    for r0, width, z0 in IN_PIECES:
        a, b = max(lo, r0), min(hi, r0 + width)
        if a < b:
            segs.append((a - lo, z0 + a - r0, b - a))
    return segs


N_SLOT = 11


def _gather_plan(l, shards, zero_rows, part="all"):
    s_in, s_uq, s_ukv, s_conv, s_mk, s_mv, s_o = shards
    ukv_c, mk_r, mk_c, o_r = s_ukv.shape[2], s_mk.shape[1], s_mk.shape[2], s_o.shape[1]
    stack = lambda s: jax.ShapeDtypeStruct((N_CHIPS,) + s.shape[1:], s.dtype)
    in_ins, in_outs = [s_in, zero_rows], [jax.ShapeDtypeStruct((Z_COLS, s_in.shape[2]), s_in.dtype)]
    rest_ins = [s_uq, s_ukv, s_conv, s_mk, s_mv, s_o]
    rest_outs = [stack(s_uq), jax.ShapeDtypeStruct((s_ukv.shape[1], N_CHIPS * ukv_c), s_ukv.dtype), stack(s_conv),
                 jax.ShapeDtypeStruct((N_CHIPS * mk_r, 2 * mk_c), s_mk.dtype),
                 jax.ShapeDtypeStruct((N_CHIPS * o_r, s_o.shape[2]), s_o.dtype)]
    with_in, with_rest = part != "rest", part != "in"

    def build(ins, outs, sems):
        ins, outs = list(ins), list(outs)
        if with_in:
            r_in, r_zero, f_in = ins.pop(0), ins.pop(0), outs.pop(0)
        if with_rest:
            r_uq, r_ukv, r_conv, r_mk, r_mv, r_o = ins
            g_uq, f_ukv, g_conv, f_mkv, f_o = outs

        def pieces(j, t):
            out = []
            if with_in:
                out += [(r_in.at[l, pl.ds(so, n), :], f_in.at[pl.ds(zo, n), :]) for so, zo, n in _in_segments(j)]
            if with_rest:
                out += [(r_uq.at[l], g_uq.at[j]), (r_ukv.at[l], f_ukv.at[:, pl.ds(j * ukv_c, ukv_c)]),
                        (r_conv.at[l], g_conv.at[j]),
                        (r_mk.at[l], f_mkv.at[pl.ds(j * mk_r, mk_r), pl.ds(0, mk_c)]),
                        (r_mv.at[l], f_mkv.at[pl.ds(j * mk_r, mk_r), pl.ds(mk_c, mk_c)]),
                        (r_o.at[l], f_o.at[pl.ds(j * o_r, o_r), :])]
            if with_in and j == t:
                out.append((r_zero, f_in.at[pl.ds(Z_KPE + ROPE, LANES - ROPE), :]))
            return out

        return _chip_copies(pieces, sems, N_SLOT)

    ins = (in_ins if with_in else []) + (rest_ins if with_rest else [])
    outs = (in_outs if with_in else []) + (rest_outs if with_rest else [])
    return _CommPlan(ins, outs, build, len(CHIP_FLIPS) * N_SLOT)


def _scatter_plan(dwt_in, c_uq, dw_ukv, c_conv, dw_mkv, dw_o, part="all"):
    ukv_c, mk_r, mk_c, o_r = dw_ukv.shape[1] // N_CHIPS, dw_mkv.shape[0] // N_CHIPS, dw_mkv.shape[1] // 2, dw_o.shape[0] // N_CHIPS
    with_in, with_rest = part != "rest", part != "in"
    in_outs = [jax.ShapeDtypeStruct((N_CHIPS, IN_SHARD, D_MODEL), BF16)]
    rest_ins = [c_uq, dw_ukv, c_conv, dw_mkv, dw_o]
    rest_outs = [jax.ShapeDtypeStruct(c_uq.shape, c_uq.dtype),
                 jax.ShapeDtypeStruct((N_CHIPS, dw_ukv.shape[0], ukv_c), dw_ukv.dtype),
                 jax.ShapeDtypeStruct(c_conv.shape, c_conv.dtype),
                 jax.ShapeDtypeStruct((N_CHIPS, mk_r, mk_c), dw_mkv.dtype), jax.ShapeDtypeStruct((N_CHIPS, mk_r, mk_c), dw_mkv.dtype),
                 jax.ShapeDtypeStruct((N_CHIPS, o_r, dw_o.shape[1]), dw_o.dtype)]

    def build(ins, outs, sems):
        ins, outs = list(ins), list(outs)
        if with_in:
            r_in, o_in = ins.pop(0), outs.pop(0)
        if with_rest:
            r_uq, r_ukv, r_conv, r_mkv, r_o = ins
            o_uq, o_ukv, o_conv, o_mk, o_mv, o_o = outs

        def pieces(j, t):
            out = []
            if with_in:
                out += [(r_in.at[pl.ds(zo, n), :], o_in.at[j, pl.ds(so, n), :]) for so, zo, n in _in_segments(t)]
            if with_rest:
                out += [(r_uq.at[t], o_uq.at[j]), (r_ukv.at[:, pl.ds(t * ukv_c, ukv_c)], o_ukv.at[j]),
                        (r_conv.at[t], o_conv.at[j]),
                        (r_mkv.at[pl.ds(t * mk_r, mk_r), pl.ds(0, mk_c)], o_mk.at[j]),
                        (r_mkv.at[pl.ds(t * mk_r, mk_r), pl.ds(mk_c, mk_c)], o_mv.at[j]),
                        (r_o.at[pl.ds(t * o_r, o_r), :], o_o.at[j])]
            return out

        return _chip_copies(pieces, sems, N_SLOT)

    ins = ([dwt_in] if with_in else []) + (rest_ins if with_rest else [])
    outs = (in_outs if with_in else []) + (rest_outs if with_rest else [])
    return _CommPlan(ins, outs, build, len(CHIP_FLIPS) * N_SLOT)


HBM = pl.BlockSpec(memory_space=pltpu.HBM)
SEM = pl.BlockSpec(memory_space=pltpu.SEMAPHORE)
SIDE_EFFECT = pltpu.SideEffectType.DATAFLOW_SIDE_EFFECTING


def _comm_start(plan, after, name):
    n_in, n_out, n_after = len(plan.ins), len(plan.out_shape), len(after)
    n_buf = n_in + n_out

    def body(*refs):
        bufs, sems, token = refs[:n_buf], refs[n_buf + n_after:n_buf + n_after + 3], refs[-1]
        start, _ = plan.build(bufs[:n_in], bufs[n_in:], sems)
        start()
        token[...] = jnp.zeros_like(token)

    lands = [lax.empty(s.shape, s.dtype) for s in plan.out_shape]
    args = [pltpu.with_memory_space_constraint(a, pltpu.HBM) for a in list(plan.ins) + lands]
    res = pl.pallas_call(
        body,
        in_specs=[HBM] * n_buf + [ANY] * n_after,
        out_specs=[SEM] * 3 + [HBM] * n_out + [pl.BlockSpec(memory_space=pltpu.VMEM)],
        out_shape=plan.scratch() + [pltpu.HBM(a.shape, a.dtype) for a in lands] + [jax.ShapeDtypeStruct((8, LANES), F32)],
        input_output_aliases={n_in + i: 3 + i for i in range(n_out)},
        compiler_params=pltpu.CompilerParams(has_side_effects=SIDE_EFFECT),
        name=name,
    )(*args, *after)
    return list(res[:3]), list(res[3:3 + n_out]), res[-1]


def _comm_finish(plan, started, after, name):
    sems, lands, _ = started
    n_in, n_out = len(plan.ins), len(plan.out_shape)
    n_buf = n_in + n_out

    def body(*refs):
        bufs_in, sem_refs = refs[:n_buf], refs[n_buf:n_buf + 3]
        _, wait = plan.build(bufs_in[:n_in], bufs_in[n_in:], sem_refs)
        wait()

    sources = [pltpu.with_memory_space_constraint(a, pltpu.HBM) for a in plan.ins]
    res = pl.pallas_call(
        body,
        in_specs=[HBM] * n_buf + [SEM] * 3 + [ANY] * len(after),
        out_specs=[HBM] * n_out,
        out_shape=[pltpu.HBM(b.shape, b.dtype) for b in lands],
        input_output_aliases={n_in + i: i for i in range(n_out)},
        compiler_params=pltpu.CompilerParams(has_side_effects=SIDE_EFFECT),
        name=name,
    )(*sources, *lands, *sems, *after)
    return list(res)


def _comm_call(plan, name):
    n_in, n_out = len(plan.ins), len(plan.out_shape)

    def body(*refs):
        start, wait = plan.build(refs[:n_in], refs[n_in:n_in + n_out], refs[n_in + n_out:])
        start()
        wait()

    return list(pl.pallas_call(
        body,
        in_specs=[ANY] * n_in,
        out_specs=[ANY] * n_out,
        out_shape=plan.out_shape,
        scratch_shapes=plan.scratch(),
        name=name,
    )(*plan.ins))


def _sibling_plan(arrays):
    def build(ins, outs, sems):
        send_sems, recv_sems, _ = sems
        sibling = (lax.axis_index("x"), lax.axis_index("y"), 1 - lax.axis_index("c"))
        copies = [pltpu.make_async_remote_copy(src_ref=src, dst_ref=dst, send_sem=send_sems.at[a], recv_sem=recv_sems.at[a],
                                               device_id=sibling, device_id_type=MESH_ID)
                  for a, (src, dst) in enumerate(zip(ins, outs))]

        def start():
            for cp in copies:
                cp.start()

        def wait():
            for cp in copies:
                cp.wait()

        return start, wait

    return _CommPlan(arrays, [jax.ShapeDtypeStruct(v.shape, v.dtype) for v in arrays], build, len(arrays))


DEVICE_FLIPS = tuple((fx, fy, fc) for fx in (0, 1) for fy in (0, 1) for fc in (0, 1))[1:]


def _gather_all(v, after, name):
    def body(v_ref, after_ref, out_ref, send_sems, recv_sems, local_sem):
        del after_ref
        x, y, c = lax.axis_index("x"), lax.axis_index("y"), lax.axis_index("c")
        me = 4 * x + 2 * y + c
        local = pltpu.make_async_copy(v_ref, out_ref.at[me], local_sem)
        local.start()
        copies = [local]
        for k, (fx, fy, fc) in enumerate(DEVICE_FLIPS):
            cp = pltpu.make_async_remote_copy(
                src_ref=v_ref, dst_ref=out_ref.at[me], send_sem=send_sems.at[k], recv_sem=recv_sems.at[k],
                device_id=((x + fx) % 2, (y + fy) % 2, (c + fc) % 2), device_id_type=MESH_ID)
            cp.start()
            copies.append(cp)
        for cp in copies:
            cp.wait()

    return pl.pallas_call(
        body,
        in_specs=[ANY, ANY],
        out_specs=ANY,
        out_shape=jax.ShapeDtypeStruct((N_DEV,) + v.shape, v.dtype),
        scratch_shapes=[pltpu.SemaphoreType.DMA((N_DEV - 1,)), pltpu.SemaphoreType.DMA((N_DEV - 1,)), pltpu.SemaphoreType.DMA],
        name=name,
    )(v, after)


def _sum_slots(parts, name):
    n, rows, cols = parts.shape
    tr = _tile(rows, 256, 16)

    def body(p_ref, o_ref):
        acc = p_ref[0].astype(F32)
        for k in range(1, n):
            acc = acc + p_ref[k].astype(F32)
        o_ref[...] = acc

    return pl.pallas_call(
        body,
        grid=(rows // tr,),
        in_specs=[pl.BlockSpec((n, tr, cols), lambda i: (0, i, 0))],
        out_specs=pl.BlockSpec((tr, cols), lambda i: (i, 0)),
        out_shape=jax.ShapeDtypeStruct((rows, cols), F32),
        compiler_params=_params("parallel"),
        name=name,
    )(parts)


def _adamw_math(w, g, m, v):
    m_new = ADAM_B1 * m + (1.0 - ADAM_B1) * g
    v_new = ADAM_B2 * v + (1.0 - ADAM_B2) * jnp.square(g)
    m_hat = m_new / (1.0 - ADAM_B1 ** ADAM_STEP)
    v_hat = v_new / (1.0 - ADAM_B2 ** ADAM_STEP)
    return -ADAM_LR * (m_hat / (jnp.sqrt(v_hat) + ADAM_EPS) + ADAM_WD * w), m_new, v_new


def _adamw(w, g, m, v, name):
    rows, cols = w.shape
    tr = _tile(rows, 256, 8)

    def body(w_ref, g_ref, m_ref, v_ref, d_out, m_out, v_out):
        d_out[...], m_out[...], v_out[...] = _adamw_math(w_ref[...], g_ref[...], m_ref[...], v_ref[...])

    blk = pl.BlockSpec((tr, cols), lambda i: (i, 0))
    out = jax.ShapeDtypeStruct((rows, cols), F32)
    return pl.pallas_call(
        body,
        grid=(rows // tr,),
        in_specs=[blk] * 4,
        out_specs=[blk] * 3,
        out_shape=[out] * 3,
        compiler_params=_params("parallel"),
        name=name,
    )(w, g, m, v)


def _adamw_layer(l, w, g_a, g_b, m, v, prev, name):
    depth, rows, cols = w.shape
    tr = _tile(rows, 256, 8)

    def body(w_ref, ga_ref, gb_ref, m_ref, v_ref, *rest):
        g_out, d_out, m_out, v_out = rest[-4:]
        g = ga_ref[...] + gb_ref[...]
        g_out[...] = g
        d_out[...], m_out[...], v_out[...] = _adamw_math(w_ref[...], g, m_ref[...], v_ref[...])

    stacked = pl.BlockSpec((None, tr, cols), lambda i: (l, i, 0))
    flat = pl.BlockSpec((tr, cols), lambda i: (i, 0))
    in_specs = [stacked, flat, flat, stacked, stacked]
    args = [w, g_a, g_b, m, v]
    aliases = {}
    if prev is not None:
        in_specs += [ANY] * 4
        args += list(prev)
        aliases = {5 + k: k for k in range(4)}
    out = jax.ShapeDtypeStruct((depth, rows, cols), F32)
    return pl.pallas_call(
        body,
        grid=(rows // tr,),
        in_specs=in_specs,
        out_specs=[stacked] * 4,
        out_shape=[out] * 4,
        input_output_aliases=aliases,
        compiler_params=_params("parallel"),
        name=name,
    )(*args)


def _cols_from_shards(g):
    _, r, c = g.shape
    return jnp.transpose(g, (1, 0, 2)).reshape(r, N_CHIPS * c)


def _cols_to_shards(full):
    r, c4 = full.shape
    c = c4 // N_CHIPS
    return jnp.transpose(full.reshape(r, N_CHIPS, c), (1, 0, 2))


IN_ORDER = (Q_RANK, KV_RANK, ROPE, CONV_W, CONV_W, CONV_W, MEM_W, MIX_W)


def _w_in_to_z_layout(w_in):
    edges = [0]
    for width in IN_ORDER:
        edges.append(edges[-1] + width)
    q_lat, kv_lat, k_pe, gb, gc, xin, q_mem, gate = [w_in[..., edges[i]:edges[i + 1]] for i in range(8)]
    pad = jnp.zeros(k_pe.shape[:-1] + (LANES - ROPE,), w_in.dtype)
    return jnp.concatenate([gate, q_lat, kv_lat, k_pe, pad, gb, gc, xin, q_mem], axis=-1)


def _w_in_from_z_layout(wz):
    cut = lambda c0, width: wz[..., c0:c0 + width]
    return jnp.concatenate(
        [cut(Z_QLAT, Q_RANK), cut(Z_KVLAT, KV_RANK), cut(Z_KPE, ROPE), cut(Z_GB, CONV_W), cut(Z_GC, CONV_W),
         cut(Z_XIN, CONV_W), cut(Z_QMEM, MEM_W), cut(Z_GATE, MIX_W)], axis=-1)


def _w_uq_pad(w_uq):
    r, _ = w_uq.shape
    w = jnp.pad(w_uq.reshape(r, MLA_HEADS, QK_HEAD), ((0, 0), (0, 0), (0, QPAD - QK_HEAD)))
    return w.reshape(r, MLA_HEADS * QPAD)


def _w_uq_unpad(w):
    r, _ = w.shape
    return w.reshape(r, MLA_HEADS, QPAD)[..., :QK_HEAD].reshape(r, MLA_HEADS * QK_HEAD)


def _rope_tables(positions):
    inv_freq = 1.0 / (ROPE_THETA ** (jnp.arange(0, ROPE, 2, dtype=F32) / ROPE))
    ang = positions.astype(F32)[:, None] * inv_freq
    cos, sin = jnp.cos(ang), jnp.sin(ang)
    s = positions.shape[0]
    zero = jnp.zeros((s, HALF_ROPE), F32)
    pad = jnp.zeros((s, LANES - ROPE), F32)
    kc = jnp.concatenate([cos, cos, pad], axis=-1)
    ka = jnp.concatenate([-sin, zero, pad], axis=-1)
    kb = jnp.concatenate([zero, sin, pad], axis=-1)
    qc = jnp.concatenate([jnp.ones((s, NOPE), F32), kc], axis=-1)
    qa = jnp.concatenate([jnp.zeros((s, NOPE), F32), ka], axis=-1)
    qb = jnp.concatenate([jnp.zeros((s, NOPE), F32), kb], axis=-1)
    return (qc, qa, qb), (kc, ka, kb)


def _layer_weights(gathered):
    return (gathered[0],) + _late_weights(gathered[1:])


def _late_weights(gathered):
    g_uq, w_ukv, g_conv, w_mkv, w_o = gathered
    return (_w_uq_pad(_cols_from_shards(g_uq)), w_ukv, _cols_from_shards(g_conv), w_mkv, w_o)


def _layer_fwd(l, x, mem, wts, gains, tabs, comm, late=None, h=None, next_g_pre=None):
    wt_in = wts[0]
    g_pre, g_q, g_kv, g_mem, g_post = gains
    q_tab, k_tab = tabs
    tag = f"l{l}_"
    if h is None:
        h = _rmsnorm_fwd(x, g_pre, 0, D_MODEL, tag + "pre_norm")
    z = _matmul(h, wt_in, "nt", BF16, tag + "in_proj", tm_cap=1024, tn_cap=1664)
    w_uq, w_ukv, conv_w, w_mkv, w_o = wts[1:] if late is None else late(z)
    wts = (wt_in, w_uq, w_ukv, conv_w, w_mkv, w_o)
    qn, kvn, kpe = _latent_prep(z, g_q, g_kv, *k_tab, tag + "latent_prep")
    q_raw = _matmul(qn, w_uq, "nn", BF16, tag + "uq", tm_cap=1024)
    kv = _matmul(kvn, w_ukv, "nn", BF16, tag + "ukv")
    cat, a_lse, arrived = _attn_fwd(q_raw, kv, kpe, kv, q_tab, MLA_HEADS, QPAD, 0, 0, 2, 1, 2, QK_HEAD ** -0.5, 512,
                                    tag + "mla_fwd", comm, o_into=(MIX_W, 0, None))
    cat = _conv_fwd(z, conv_w, cat, tag + "conv_fwd")
    mem_n = _rmsnorm_fwd(mem, g_mem, 0, D_MODEL, tag + "mem_norm")
    mkv = _matmul(mem_n, w_mkv, "nn", BF16, tag + "mem_kv")
    cat, m_lse, _ = _attn_fwd(z, mkv, None, mkv, None, MEM_HEADS, LANES, Z_QMEM // LANES, 0, 1, MEM_HEADS, 1,
                              MEM_HEAD ** -0.5, 4096, tag + "mem_fwd", o_into=(MIX_W, (MLA_W + CONV_W) // LANES, cat))
    y = _gate_fwd(cat, z, tag + "gate_fwd")
    o = _matmul(y, w_o, "nn", BF16, tag + "out_proj", tm_cap=1024)
    x_new, h_next = _post_norm_residual(x, o, g_post, next_g_pre, tag + "post_norm")
    saved = (x, h, z, qn, kvn, q_raw, kv, kpe, a_lse, mem_n, mkv, m_lse, cat, y, o)
    return x_new, saved, h_next


def _layer_bwd(l, g, mem, saved, wts, gains, tabs_bwd, comm, split_exchange=False):
    wt_in, w_uq, w_ukv, conv_w, w_mkv, w_o = wts
    g_pre, g_q, g_kv, g_mem, g_post = gains
    q_tab, k_tab = tabs_bwd
    x, h, z, qn, kvn, q_raw, kv, kpe, a_lse, mem_n, mkv, m_lse, cat, y, o = saved
    tag = f"l{l}_"
    do, dg_post = _rmsnorm_bwd(o, g_post, g, None, 0, D_MODEL, BF16, tag + "post_norm_bwd")
    dcat, dgate = _out_proj_dx_gate_bwd(do, w_o, cat, z, tag + "out_proj_dx")
    dw_o = _matmul(y, do, "tn", BF16, tag + "out_proj_dw", tm_cap=1024)
    dq, dkv, dkpe_h, arrived = _attn_bwd(q_raw, kv, kpe, kv, cat, dcat, a_lse, q_tab, MLA_HEADS, QPAD, 0, 0, 2, 1, 2, 0,
                                         QK_HEAD ** -0.5, 512, tag + "mla_bwd", comm)
    dw_ukv = _matmul(kvn, dkv, "tn", BF16, tag + "ukv_dw")
    dkvn = _matmul(dkv, w_ukv, "nt", BF16, tag + "ukv_dx")
    dw_uq = _matmul(qn, dq, "tn", BF16, tag + "uq_dw")
    dqn = _matmul(dq, w_uq, "nt", BF16, tag + "uq_dx")
    dq_lat, dkv_lat, dkpe, dg_q, dg_kv = _latent_prep_bwd(z, g_q, g_kv, dqn, dkvn, dkpe_h, *k_tab, MLA_HEADS,
                                                          tag + "latent_prep_bwd")
    dgb, dgc, dxin, dconv_w = _conv_bwd(z, conv_w, dcat, tag + "conv_bwd")
    dq_mem, dmk, dmv, _ = _attn_bwd(z, mkv, None, mkv, cat, dcat, m_lse, None, MEM_HEADS, LANES, Z_QMEM // LANES, 0, 1,
                                    MEM_HEADS, 1, (MLA_W + CONV_W) // LANES, MEM_HEAD ** -0.5, 2048, tag + "mem_bwd")
    dmkv = jnp.concatenate([dmk, dmv], axis=-1)
    dw_mkv = _matmul(mem_n, dmkv, "tn", BF16, tag + "mem_kv_dw")
    dmem_n = _matmul(dmkv, w_mkv, "nt", F32, tag + "mem_kv_dx")
    _, dg_mem = _rmsnorm_bwd(mem, g_mem, dmem_n, None, 0, D_MODEL, BF16, tag + "mem_norm_bwd")
    others = (_cols_to_shards(_w_uq_unpad(dw_uq)), dw_ukv, _cols_to_shards(dconv_w), dw_mkv, dw_o)
    early = None
    if split_exchange:
        early_plan = _scatter_plan(None, *others, part="rest")
        early = (early_plan, _comm_start(early_plan, [dmem_n], tag + "exchange_rest_start"))
        g_pre = g_pre + early[1][2][0:1, 0:1]
    dz = jnp.concatenate([dgate, dq_lat, dkv_lat, dkpe, dgb, dgc, dxin, dq_mem], axis=-1)
    dwt_in = _matmul(dz, h, "tn", BF16, tag + "in_proj_dw", tm_cap=1664, tk_cap=2048)
    contrib = _scatter_plan(dwt_in, *others, part="in" if split_exchange else "all")
    late = None
    if split_exchange:
        late = (contrib, _comm_start(contrib, [dwt_in], tag + "exchange_in_start"))
    dh = _matmul(dz, wt_in, "nn", BF16, tag + "in_proj_dx", tm_cap=1024, tk_cap=1664, after=late[1][2] if late else None)
    dx, dg_pre = _rmsnorm_bwd(x, g_pre, dh, g, 0, D_MODEL, F32, tag + "pre_norm_bwd")
    return dx, contrib, (dg_pre, dg_q, dg_kv, dg_mem, dg_post), (early, late)


GAIN_WIDTHS = (D_MODEL, Q_RANK, KV_RANK, D_MODEL, D_MODEL)


def _pack_gains(parts):
    return jnp.concatenate([p.reshape(-1) for p in parts]).reshape(-1, LANES)


def _unpack_gains(packed, depth):
    flat = packed.reshape(-1)
    out, at = [], 0
    for width in GAIN_WIDTHS:
        out.append(flat[at:at + depth * width].reshape(depth, width))
        at += depth * width
    return out


def kernel(x, mem, positions, pre_norm_g, w_in, q_norm_g, w_uq, kv_norm_g, w_ukv, conv_w, mem_norm_g, w_mk, w_mv, w_o, post_norm_g, loss_target, m_pre_norm_g, m_w_in, m_q_norm_g, m_w_uq, m_kv_norm_g, m_w_ukv, m_conv_w, m_mem_norm_g, m_w_mk, m_w_mv, m_w_o, m_post_norm_g, v_pre_norm_g, v_w_in, v_q_norm_g, v_w_uq, v_kv_norm_g, v_w_ukv, v_conv_w, v_mem_norm_g, v_w_mk, v_w_mv, v_w_o, v_post_norm_g):
    depth = w_in.shape[0]
    x0, mem0, target = x[0], mem[0], loss_target[0]
    tabs = _rope_tables(positions[0])
    tabs_bwd = tabs

    flip = lambda t: jnp.transpose(t, (0, 2, 1))
    w_in, m_w_in, v_w_in = flip(w_in), flip(m_w_in), flip(v_w_in)
    shards = [w_in.astype(BF16), w_uq.astype(BF16), w_ukv.astype(BF16), conv_w, w_mk.astype(BF16), w_mv.astype(BF16),
              w_o.astype(BF16)]
    zero_rows = lambda: jnp.zeros((LANES - ROPE, D_MODEL), BF16)

    def layer_gains(l):
        return tuple(g[l][None, :] for g in (pre_norm_g, q_norm_g, kv_norm_g, mem_norm_g, post_norm_g))

    wts, saved = [None] * depth, [None] * depth
    plan_in, plan_rest = _gather_plan(0, shards, zero_rows(), "in"), _gather_plan(0, shards, zero_rows(), "rest")
    started_in = _comm_start(plan_in, [positions], "l0_gather_in_start")
    started_rest = _comm_start(plan_rest, [started_in[2]], "l0_gather_rest_start")
    h0 = _rmsnorm_fwd(x0, layer_gains(0)[0], 0, D_MODEL, "l0_pre_norm")
    wts[0] = tuple(_comm_finish(plan_in, started_in, [started_rest[2], h0, tabs[0][0]], "l0_gather_in_wait"))

    next_gather = {}

    def start_next_gather(l, after):
        plan = _gather_plan(l + 1, shards, zero_rows())
        next_gather[l + 1] = (plan, _comm_start(plan, [after], f"l{l + 1}_gather_start"))
        return next_gather[l + 1][1][2][0:1, 0:1]

    def rest_of_layer0(z):
        got = _late_weights(_comm_finish(plan_rest, started_rest, [z], "l0_gather_rest_wait"))
        wts[0] = wts[0] + got
        if depth > 1:
            got = (got[0] + start_next_gather(0, got[4]).astype(BF16),) + got[1:]
        return got

    act, h_act = x0, h0
    for l in range(depth):
        gains = layer_gains(l)
        if 0 < l < depth - 1:
            gains = gains[:1] + (gains[1] + start_next_gather(l, wts[l][5]),) + gains[2:]
        act, saved[l], h_act = _layer_fwd(l, act, mem0, wts[l], gains, tabs, None, rest_of_layer0 if l == 0 else None, h_act,
                                          layer_gains(l + 1)[0] if l + 1 < depth else None)
        if l + 1 < depth:
            plan, started = next_gather[l + 1]
            wts[l + 1] = _layer_weights(_comm_finish(plan, started, [act], f"l{l + 1}_gather_wait"))
    grad, loss_part = _loss_head(act, target, "loss_head")
    loss = lax.psum(loss_part[0, 0], ("x", "y", "c"))

    names = ("w_in", "w_uq", "w_ukv", "conv_w", "w_mk", "w_mv", "w_o")
    w_shards = (w_in, w_uq, w_ukv, conv_w, w_mk, w_mv, w_o)
    m_shards = (m_w_in, m_w_uq, m_w_ukv, m_conv_w, m_w_mk, m_w_mv, m_w_o)
    v_shards = (v_w_in, v_w_uq, v_w_ukv, v_conv_w, v_w_mk, v_w_mv, v_w_o)
    stacked = [None] * len(names)

    def sum_and_send(l, received):
        partial = [_sum_slots(r, f"l{l}_grad_sum_{names[i]}") for i, r in enumerate(received)]
        plan = _sibling_plan(partial)
        return l, partial, plan, _comm_start(plan, [partial[0]], f"l{l}_sibling_start")

    def receive_and_update(state, after):
        l, partial, plan, started = state
        other = _comm_finish(plan, started, [after], f"l{l}_sibling_wait")
        for i, name in enumerate(names):
            stacked[i] = _adamw_layer(l, w_shards[i], partial[i], other[i], m_shards[i], v_shards[i], stacked[i],
                                      f"l{l}_adamw_{name}")

    dgs = [None] * depth
    pending = None
    in_flight = None
    for l in reversed(range(depth)):
        gains = layer_gains(l)
        for token in ([pending[1][2]] if pending else []) + ([in_flight[3][2]] if in_flight else []):
            gains = gains[:4] + (gains[4] + token[0:1, 0:1],)
        grad, contrib, dgs[l], early = _layer_bwd(l, grad, mem0, saved[l], wts[l], gains, tabs_bwd, None, l == 0)
        if in_flight is not None:
            receive_and_update(in_flight, grad)
            in_flight = None
        if pending is not None:
            in_flight = sum_and_send(l + 1, _comm_finish(pending[0], pending[1], [grad], f"l{l + 1}_exchange_wait"))
        if l > 0:
            pending = (contrib, _comm_start(contrib, [grad], f"l{l}_exchange_start"))
    early, late = early
    got_in = _comm_finish(late[0], late[1], [grad], "l0_exchange_in_wait")
    last = sum_and_send(0, got_in + _comm_finish(early[0], early[1], [got_in[0]], "l0_exchange_rest_wait"))
    if in_flight is not None:
        receive_and_update(in_flight, last[1][0])
    receive_and_update(last, stacked[0][0] if depth > 1 else last[1][0])
    grad_x = grad[None]
    results = {name: tuple(stacked[i]) for i, name in enumerate(names)}
    results["w_in"] = tuple(flip(t) for t in results["w_in"])

    gain_names = ("pre_norm_g", "q_norm_g", "kv_norm_g", "mem_norm_g", "post_norm_g")
    dg_packed = _pack_gains([jnp.concatenate([dgs[l][i] for l in range(depth)], axis=0) for i in range(5)])
    dg_total = _sum_slots(_gather_all(dg_packed, stacked[0][0], "gain_gather"), "gain_sum")
    gain_outs = (dg_total,) + tuple(_adamw(
        _pack_gains((pre_norm_g, q_norm_g, kv_norm_g, mem_norm_g, post_norm_g)), dg_total,
        _pack_gains((m_pre_norm_g, m_q_norm_g, m_kv_norm_g, m_mem_norm_g, m_post_norm_g)),
        _pack_gains((v_pre_norm_g, v_q_norm_g, v_kv_norm_g, v_mem_norm_g, v_post_norm_g)), "adamw_gains"))
    gain_outs = [_unpack_gains(t, depth) for t in gain_outs]
    for i, name in enumerate(gain_names):
        results[name] = tuple(gain_outs[k][i] for k in range(4))

    order = ("pre_norm_g", "w_in", "q_norm_g", "w_uq", "kv_norm_g", "w_ukv", "conv_w", "mem_norm_g", "w_mk", "w_mv", "w_o",
             "post_norm_g")
    out = [loss, grad_x]
    for k in range(4):
        out += [results[name][k] for name in order]
    return tuple(out)
```

```python
import jax
import jax.numpy as jnp
from jax import lax
from jax.experimental import pallas as pl
from jax.experimental.pallas import tpu as pltpu

F32 = jnp.float32
BF16 = jnp.bfloat16
MESH_ID = pl.DeviceIdType.MESH

D_MODEL = 2048
EPS = 1e-6
LOG2_E = 1.4426950408889634
ROPE_THETA = 10000.0
MLA_HEADS = 8
NOPE = 128
ROPE = 64
HALF_ROPE = ROPE // 2
QK_HEAD = NOPE + ROPE
V_HEAD = 128
Q_RANK = 512
KV_RANK = 256
CONV_W = 512
MEM_HEADS = 4
MEM_HEAD = 128
MEM_W = MEM_HEADS * MEM_HEAD
MLA_W = MLA_HEADS * V_HEAD
MIX_W = MLA_W + CONV_W + MEM_W
IN_COLS = Q_RANK + KV_RANK + ROPE + 3 * CONV_W + MEM_W + MIX_W
N_CHIPS = 4
N_DEV = 8

LANES = 128
VMEM_LIMIT_BYTES = 56 * 1024 * 1024

QPAD = 2 * LANES
Z_GATE = 0
Z_QLAT = Z_GATE + MIX_W
Z_KVLAT = Z_QLAT + Q_RANK
Z_KPE = Z_KVLAT + KV_RANK
Z_GB = Z_KPE + LANES
Z_GC = Z_GB + CONV_W
Z_XIN = Z_GC + CONV_W
Z_QMEM = Z_XIN + CONV_W
Z_COLS = Z_QMEM + MEM_W

ADAM_LR = 0.001
ADAM_B1 = 0.9
ADAM_B2 = 0.999
ADAM_EPS = 1e-08
ADAM_WD = 0.01
ADAM_STEP = 10


def _tile(dim, cap, unit):
    if dim <= cap:
        return dim
    t = (cap // unit) * unit
    while t >= unit:
        if dim % t == 0:
            return t
        t -= unit
    raise ValueError(f"no tile of {dim} under {cap} in units of {unit}")


def _params(*semantics):
    return pltpu.CompilerParams(dimension_semantics=semantics, vmem_limit_bytes=VMEM_LIMIT_BYTES)


def _matmul(a, b, mode, out_dtype, name, tm_cap=512, tn_cap=1024, tk_cap=2048, after=None):
    if mode == "nn":
        (m, k), (k2, n) = a.shape, b.shape
    elif mode == "nt":
        (m, k), (n, k2) = a.shape, b.shape
    else:
        (k, m), (k2, n) = a.shape, b.shape
    assert k == k2, (a.shape, b.shape, mode)
    tm = _tile(m, tm_cap, LANES if mode == "tn" else 16)
    tn = _tile(n, tn_cap, LANES)
    tk = _tile(k, tk_cap, LANES if mode != "tn" else 16)
    nk = k // tk
    if mode == "nn":
        a_spec = pl.BlockSpec((tm, tk), lambda i, j, kk: (i, kk))
        b_spec = pl.BlockSpec((tk, tn), lambda i, j, kk: (kk, j))
        dims = (((1,), (0,)), ((), ()))
    elif mode == "nt":
        a_spec = pl.BlockSpec((tm, tk), lambda i, j, kk: (i, kk))
        b_spec = pl.BlockSpec((tn, tk), lambda i, j, kk: (j, kk))
        dims = (((1,), (1,)), ((), ()))
    else:
        a_spec = pl.BlockSpec((tk, tm), lambda i, j, kk: (kk, i))
        b_spec = pl.BlockSpec((tk, tn), lambda i, j, kk: (kk, j))
        dims = (((0,), (0,)), ((), ()))

    def body(a_ref, b_ref, *rest):
        o_ref, scratch = (rest[1], rest[2:]) if after is not None else (rest[0], rest[1:])
        part = lax.dot_general(a_ref[...].astype(BF16), b_ref[...].astype(BF16), dims, preferred_element_type=F32)
        if nk == 1:
            o_ref[...] = part.astype(o_ref.dtype)
            return
        (acc_ref,) = scratch
        kk = pl.program_id(2)

        @pl.when(kk == 0)
        def _():
            acc_ref[...] = part

        @pl.when(kk > 0)
        def _():
            acc_ref[...] += part

        @pl.when(kk == nk - 1)
        def _():
            o_ref[...] = acc_ref[...].astype(o_ref.dtype)

    return pl.pallas_call(
        body,
        grid=(m // tm, n // tn, nk),
        in_specs=[a_spec, b_spec] + ([] if after is None else [pl.BlockSpec(memory_space=pl.ANY)]),
        out_specs=pl.BlockSpec((tm, tn), lambda i, j, kk: (i, j)),
        out_shape=jax.ShapeDtypeStruct((m, n), out_dtype),
        scratch_shapes=[] if nk == 1 else [pltpu.VMEM((tm, tn), F32)],
        compiler_params=_params("parallel", "parallel", "arbitrary"),
        name=name,
    )(*([a, b] if after is None else [a, b, after]))


def _rmsnorm_fwd(x, gain, col0, width, name):
    rows = x.shape[0]
    tr = _tile(rows, 512, 16)
    cb = col0 // width
    assert cb * width == col0

    def body(x_ref, g_ref, o_ref):
        xv = x_ref[...].astype(F32)
        r = lax.rsqrt(jnp.mean(xv * xv, axis=-1, keepdims=True) + EPS)
        o_ref[...] = (xv * r * g_ref[...]).astype(o_ref.dtype)

    return pl.pallas_call(
        body,
        grid=(rows // tr,),
        in_specs=[pl.BlockSpec((tr, width), lambda i: (i, cb)), pl.BlockSpec((1, width), lambda i: (0, 0))],
        out_specs=pl.BlockSpec((tr, width), lambda i: (i, 0)),
        out_shape=jax.ShapeDtypeStruct((rows, width), BF16),
        compiler_params=_params("parallel"),
        name=name,
    )(x, gain)


def _rmsnorm_bwd(x, gain, dy, resid, col0, width, out_dtype, name):
    rows = x.shape[0]
    tr = _tile(rows, 256, 16)
    cb = col0 // width
    assert cb * width == col0
    has_resid = resid is not None

    def body(*refs):
        if has_resid:
            x_ref, g_ref, dy_ref, res_ref, dx_ref, dg_ref = refs
        else:
            x_ref, g_ref, dy_ref, dx_ref, dg_ref = refs
        i = pl.program_id(0)
        xv = x_ref[...].astype(F32)
        dyv = dy_ref[...].astype(F32)
        r = lax.rsqrt(jnp.mean(xv * xv, axis=-1, keepdims=True) + EPS)
        xr = xv * r
        dyg = dyv * g_ref[...]
        c = jnp.mean(dyg * xr, axis=-1, keepdims=True)
        dx = r * (dyg - xr * c)
        if has_resid:
            dx = dx + res_ref[...]
        dx_ref[...] = dx.astype(dx_ref.dtype)
        part = jnp.sum(dyv * xr, axis=0, keepdims=True)

        @pl.when(i == 0)
        def _():
            dg_ref[...] = part

        @pl.when(i > 0)
        def _():
            dg_ref[...] += part

    row_spec = pl.BlockSpec((tr, width), lambda i: (i, 0))
    in_specs = [pl.BlockSpec((tr, width), lambda i: (i, cb)), pl.BlockSpec((1, width), lambda i: (0, 0)), row_spec]
    args = [x, gain, dy]
    if has_resid:
        in_specs.append(row_spec)
        args.append(resid)
    return pl.pallas_call(
        body,
        grid=(rows // tr,),
        in_specs=in_specs,
        out_specs=[row_spec, pl.BlockSpec((1, width), lambda i: (0, 0))],
        out_shape=[jax.ShapeDtypeStruct((rows, width), out_dtype), jax.ShapeDtypeStruct((1, width), F32)],
        compiler_params=_params("arbitrary"),
        name=name,
    )(*args)


def _post_norm_residual(x, o, gain, next_gain, name):
    rows, width = x.shape
    tr = _tile(rows, 256, 16)
    with_next = next_gain is not None

    def body(*refs):
        x_ref, o_ref, g_ref = refs[:3]
        ov = o_ref[...].astype(F32)
        r = lax.rsqrt(jnp.mean(ov * ov, axis=-1, keepdims=True) + EPS)
        x_new = x_ref[...] + ov * r * g_ref[...]
        if with_next:
            gn_ref, out_ref, h_ref = refs[3:]
            rn = lax.rsqrt(jnp.mean(x_new * x_new, axis=-1, keepdims=True) + EPS)
            h_ref[...] = (x_new * rn * gn_ref[...]).astype(h_ref.dtype)
        else:
            (out_ref,) = refs[3:]
        out_ref[...] = x_new

    row_spec = pl.BlockSpec((tr, width), lambda i: (i, 0))
    gain_spec = pl.BlockSpec((1, width), lambda i: (0, 0))
    res = pl.pallas_call(
        body,
        grid=(rows // tr,),
        in_specs=[row_spec, row_spec, gain_spec] + ([gain_spec] if with_next else []),
        out_specs=[row_spec] + ([row_spec] if with_next else []),
        out_shape=[jax.ShapeDtypeStruct((rows, width), F32)] + ([jax.ShapeDtypeStruct((rows, width), BF16)] if with_next else []),
        compiler_params=_params("parallel"),
        name=name,
    )(*([x, o, gain] + ([next_gain] if with_next else [])))
    return (res[0], res[1]) if with_next else (res[0], None)


def _latent_prep(z, g_q, g_kv, tab_c, tab_a, tab_b, name):
    rows = z.shape[0]
    tr = _tile(rows, 512, 16)

    def norm(x_ref, g_ref, o_ref):
        xv = x_ref[...].astype(F32)
        r = lax.rsqrt(jnp.mean(xv * xv, axis=-1, keepdims=True) + EPS)
        o_ref[...] = (xv * r * g_ref[...]).astype(o_ref.dtype)

    def body(q_ref, kv_ref, k_ref, gq_ref, gkv_ref, c_ref, a_ref, b_ref, qn_ref, kvn_ref, kpe_ref):
        norm(q_ref, gq_ref, qn_ref)
        norm(kv_ref, gkv_ref, kvn_ref)
        kpe_ref[...] = _rope_rows(k_ref[...].astype(F32), c_ref[...], a_ref[...], b_ref[...], 1).astype(kpe_ref.dtype)

    window = lambda c0, width: pl.BlockSpec((tr, width), lambda i: (i, c0 // width))
    gain = lambda width: pl.BlockSpec((1, width), lambda i: (0, 0))
    tab = pl.BlockSpec((tr, LANES), lambda i: (i, 0))
    out = lambda width: pl.BlockSpec((tr, width), lambda i: (i, 0))
    return pl.pallas_call(
        body,
        grid=(rows // tr,),
        in_specs=[window(Z_QLAT, Q_RANK), window(Z_KVLAT, KV_RANK), window(Z_KPE, LANES), gain(Q_RANK), gain(KV_RANK), tab, tab, tab],
        out_specs=[out(Q_RANK), out(KV_RANK), out(LANES)],
        out_shape=[jax.ShapeDtypeStruct((rows, Q_RANK), BF16), jax.ShapeDtypeStruct((rows, KV_RANK), BF16),
                   jax.ShapeDtypeStruct((rows, LANES), BF16)],
        compiler_params=_params("parallel"),
        name=name,
    )(z, z, z, g_q, g_kv, tab_c, tab_a, tab_b)


def _latent_prep_bwd(z, g_q, g_kv, dqn, dkvn, dkb, tab_c, tab_a, tab_b, heads, name):
    rows = z.shape[0]
    tr = _tile(rows, 256, 16)

    def norm_bwd(x_ref, g_ref, dy_ref, dx_ref, dg_ref, i):
        xv = x_ref[...].astype(F32)
        dyv = dy_ref[...].astype(F32)
        r = lax.rsqrt(jnp.mean(xv * xv, axis=-1, keepdims=True) + EPS)
        xr = xv * r
        dyg = dyv * g_ref[...]
        c = jnp.mean(dyg * xr, axis=-1, keepdims=True)
        dx_ref[...] = (r * (dyg - xr * c)).astype(dx_ref.dtype)
        part = jnp.sum(dyv * xr, axis=0, keepdims=True)

        @pl.when(i == 0)
        def _():
            dg_ref[...] = part

        @pl.when(i > 0)
        def _():
            dg_ref[...] += part

    def body(q_ref, kv_ref, gq_ref, gkv_ref, dqn_ref, dkvn_ref, d_ref, c_ref, a_ref, b_ref,
             dq_ref, dkv_ref, dkpe_ref, dgq_ref, dgkv_ref):
        i = pl.program_id(0)
        norm_bwd(q_ref, gq_ref, dqn_ref, dq_ref, dgq_ref, i)
        norm_bwd(kv_ref, gkv_ref, dkvn_ref, dkv_ref, dgkv_ref, i)
        acc = d_ref[:, 0:LANES]
        for h in range(1, heads):
            acc = acc + d_ref[:, h * LANES:(h + 1) * LANES]
        dkpe_ref[...] = _rope_rows(acc, c_ref[...], a_ref[...], b_ref[...], -1).astype(dkpe_ref.dtype)

    window = lambda c0, width: pl.BlockSpec((tr, width), lambda i: (i, c0 // width))
    gain = lambda width: pl.BlockSpec((1, width), lambda i: (0, 0))
    rows_of = lambda width: pl.BlockSpec((tr, width), lambda i: (i, 0))
    return pl.pallas_call(
        body,
        grid=(rows // tr,),
        in_specs=[window(Z_QLAT, Q_RANK), window(Z_KVLAT, KV_RANK), gain(Q_RANK), gain(KV_RANK), rows_of(Q_RANK), rows_of(KV_RANK),
                  rows_of(heads * LANES), rows_of(LANES), rows_of(LANES), rows_of(LANES)],
        out_specs=[rows_of(Q_RANK), rows_of(KV_RANK), rows_of(LANES), gain(Q_RANK), gain(KV_RANK)],
        out_shape=[jax.ShapeDtypeStruct((rows, Q_RANK), BF16), jax.ShapeDtypeStruct((rows, KV_RANK), BF16),
                   jax.ShapeDtypeStruct((rows, LANES), BF16), jax.ShapeDtypeStruct((1, Q_RANK), F32),
                   jax.ShapeDtypeStruct((1, KV_RANK), F32)],
        compiler_params=_params("arbitrary"),
        name=name,
    )(z, z, g_q, g_kv, dqn, dkvn, dkb, tab_c, tab_a, tab_b)


class _CommPlan:
    def __init__(self, ins, out_shape, build, n_copies):
        self.ins, self.out_shape, self.build, self.n_copies = list(ins), list(out_shape), build, n_copies

    def scratch(self):
        n = self.n_copies
        return [pltpu.SemaphoreType.DMA((n,)), pltpu.SemaphoreType.DMA((n,)), pltpu.SemaphoreType.DMA((n,))]


def _rope_rows(x, c, a, b, sign):
    width = x.shape[-1]
    mixed = pltpu.roll(x, width - HALF_ROPE, 1) * a + pltpu.roll(x, HALF_ROPE, 1) * b
    return x * c + mixed if sign > 0 else x * c - mixed


def _attn_fwd(q, ka, kb, v, rope, heads, q_w, q_cb, ka_cb, ka_step, v_cb, v_step, scale, tq_cap, name, tk_cap=512, o_into=None):
    s_q, s_k = q.shape[0], ka.shape[0]
    tq = _tile(s_q, tq_cap, 16)
    nq = s_q // tq
    has_kb = kb is not None
    n_in = 7 if has_kb else 3
    tk = _tile(s_k, tk_cap, LANES)
    o_cols, o_cb, o_old = o_into if o_into is not None else (heads * LANES, 0, None)

    def body(*refs):
        if o_old is not None:
            refs = refs[:n_in] + refs[n_in + 1:]
        if has_kb:
            q_ref, ka_ref, kb_ref, v_ref, c_ref, a_ref, b_ref, o_ref, lse_ref, k_scr = refs

            @pl.when(pl.program_id(1) == 0)
            def _():
                k_scr[:, 0:LANES] = ka_ref[...].astype(BF16)
                k_scr[:, LANES:2 * LANES] = kb_ref[...].astype(BF16)

            keys = k_scr
            qv = _rope_rows(q_ref[...].astype(F32), c_ref[...], a_ref[...], b_ref[...], 1).astype(BF16)
        else:
            q_ref, ka_ref, v_ref, o_ref, lse_ref = refs
            keys = ka_ref
            qv = q_ref[...].astype(BF16)
        c2 = scale * LOG2_E
        m = l = o = None
        nk = s_k // tk
        scores = lambda j: lax.dot_general(qv, keys[j * tk:(j + 1) * tk, :].astype(BF16), (((1,), (1,)), ((), ())),
                                           preferred_element_type=F32)
        s_next = scores(0)
        for j in range(nk):
            sj = s_next
            if j + 1 < nk:
                s_next = scores(j + 1)
            mj = jnp.max(sj, axis=-1, keepdims=True)
            m_new = mj if m is None else jnp.maximum(m, mj)
            pj = jnp.exp2((sj - m_new) * c2)
            lj = jnp.sum(pj, axis=-1, keepdims=True)
            oj = jnp.dot(pj.astype(BF16), v_ref[j * tk:(j + 1) * tk, :].astype(BF16), preferred_element_type=F32)
            if m is None:
                l, o = lj, oj
            else:
                alpha = jnp.exp2((m - m_new) * c2)
                l, o = l * alpha + lj, o * alpha + oj
            m = m_new
        o_ref[...] = (o * (1.0 / l)).astype(o_ref.dtype)
        lse_ref[...] = jnp.broadcast_to(m * c2 + jnp.log2(l), lse_ref.shape)

    in_specs = [pl.BlockSpec((tq, q_w), lambda h, i: (i, q_cb + h)),
                pl.BlockSpec((s_k, LANES), lambda h, i: (0, ka_cb + ka_step * h))]
    args = [q, ka]
    if has_kb:
        in_specs.append(pl.BlockSpec((s_k, LANES), lambda h, i: (0, 0)))
        args.append(kb)
    in_specs.append(pl.BlockSpec((s_k, LANES), lambda h, i: (0, v_cb + v_step * h)))
    args.append(v)
    if has_kb:
        in_specs += [pl.BlockSpec((tq, q_w), lambda h, i: (i, 0))] * 3
        args += list(rope)
    aliases = {}
    if o_old is not None:
        aliases = {len(args): 0}
        in_specs.append(ANY)
        args.append(o_old)
    out_specs = [pl.BlockSpec((tq, LANES), lambda h, i: (i, o_cb + h)), pl.BlockSpec((tq, LANES), lambda h, i: (i, h))]
    out_shape = [jax.ShapeDtypeStruct((s_q, o_cols), BF16), jax.ShapeDtypeStruct((s_q, heads * LANES), F32)]
    scratch = [pltpu.VMEM((s_k, 2 * LANES), BF16)] if has_kb else []
    return pl.pallas_call(
        body,
        grid=(heads, nq),
        in_specs=in_specs,
        out_specs=out_specs,
        out_shape=out_shape,
        scratch_shapes=scratch,
        input_output_aliases=aliases,
        compiler_params=_params("arbitrary", "arbitrary"),
        name=name,
    )(*args)


def _attn_bwd(q, ka, kb, v, o, do, lse, rope, heads, q_w, q_cb, ka_cb, ka_step, v_cb, v_step, o_cb, scale, tq_cap, name,
              tk_cap=512):
    s_q, s_k = q.shape[0], ka.shape[0]
    tq = _tile(s_q, tq_cap, 16)
    nq = s_q // tq
    has_kb = kb is not None
    tk = _tile(s_k, tk_cap, LANES)

    def body(*refs):
        if has_kb:
            (q_ref, ka_ref, kb_ref, v_ref, o_ref, do_ref, lse_ref, c_ref, a_ref, b_ref, dq_ref, dkv_ref, dkb_ref, k_scr, dk_acc,
             dv_acc) = refs
        else:
            q_ref, ka_ref, v_ref, o_ref, do_ref, lse_ref, dq_ref, dka_ref, dv_ref, dk_acc, dv_acc = refs
        i = pl.program_id(1)

        @pl.when(i == 0)
        def _():
            dk_acc[...] = jnp.zeros_like(dk_acc)
            dv_acc[...] = jnp.zeros_like(dv_acc)
            if has_kb:
                k_scr[:, 0:LANES] = ka_ref[...].astype(BF16)
                k_scr[:, LANES:2 * LANES] = kb_ref[...].astype(BF16)

        keys = k_scr if has_kb else ka_ref
        if has_kb:
            qv = _rope_rows(q_ref[...].astype(F32), c_ref[...], a_ref[...], b_ref[...], 1).astype(BF16)
        else:
            qv = q_ref[...].astype(BF16)
        dov = do_ref[...].astype(BF16)
        delta = jnp.sum(dov.astype(F32) * o_ref[...].astype(F32), axis=-1, keepdims=True)
        lse2 = lse_ref[:, 0:1]
        c2 = scale * LOG2_E
        nk = s_k // tk
        rows = lambda j: slice(j * tk, (j + 1) * tk)
        nt = (((1,), (1,)), ((), ()))
        tn = (((0,), (0,)), ((), ()))

        def scores(j):
            return (lax.dot_general(qv, keys[rows(j), :].astype(BF16), nt, preferred_element_type=F32),
                    lax.dot_general(dov, v_ref[rows(j), :].astype(BF16), nt, preferred_element_type=F32))

        nxt = scores(0)
        dq = None
        for j in range(nk):
            sj, dpj = nxt
            if j + 1 < nk:
                nxt = scores(j + 1)
            pj = jnp.exp2(sj * c2 - lse2)
            dsj = (pj * (dpj - delta)).astype(BF16)
            dqj = jnp.dot(dsj, keys[rows(j), :].astype(BF16), preferred_element_type=F32)
            dq = dqj if dq is None else dq + dqj
            dk_acc[rows(j), :] += lax.dot_general(dsj, qv, tn, preferred_element_type=F32)
            dv_acc[rows(j), :] += lax.dot_general(pj.astype(BF16), dov, tn, preferred_element_type=F32)
        dq = dq * scale
        if has_kb:
            dq = _rope_rows(dq, c_ref[...], a_ref[...], b_ref[...], -1)
        dq_ref[...] = dq.astype(dq_ref.dtype)

        @pl.when(i == nq - 1)
        def _():
            if has_kb:
                dkv_ref[:, 0:LANES] = (dk_acc[:, 0:LANES] * scale).astype(dkv_ref.dtype)
                dkv_ref[:, LANES:2 * LANES] = dv_acc[...].astype(dkv_ref.dtype)
                dkb_ref[...] = dk_acc[:, LANES:2 * LANES] * scale
            else:
                dka_ref[...] = (dk_acc[...] * scale).astype(dka_ref.dtype)
                dv_ref[...] = dv_acc[...].astype(dv_ref.dtype)

    key_spec = lambda cb, step: pl.BlockSpec((s_k, LANES), lambda h, i: (0, cb + step * h))
    row_spec = lambda cb: pl.BlockSpec((tq, LANES), lambda h, i: (i, cb + h))
    in_specs = [pl.BlockSpec((tq, q_w), lambda h, i: (i, q_cb + h)), key_spec(ka_cb, ka_step)]
    args = [q, ka]
    if has_kb:
        in_specs.append(pl.BlockSpec((s_k, LANES), lambda h, i: (0, 0)))
        args.append(kb)
    in_specs += [key_spec(v_cb, v_step), row_spec(o_cb), row_spec(o_cb), row_spec(0)]
    args += [v, o, do, lse]
    if has_kb:
        in_specs += [pl.BlockSpec((tq, q_w), lambda h, i: (i, 0))] * 3
        args += list(rope)
    out_specs = [pl.BlockSpec((tq, q_w), lambda h, i: (i, h))]
    out_shape = [jax.ShapeDtypeStruct((s_q, heads * q_w), BF16)]
    scratch = []
    if has_kb:
        out_specs += [pl.BlockSpec((s_k, 2 * LANES), lambda h, i: (0, h)), key_spec(0, 1)]
        out_shape += [jax.ShapeDtypeStruct((s_k, heads * 2 * LANES), BF16), jax.ShapeDtypeStruct((s_k, heads * LANES), F32)]
        scratch.append(pltpu.VMEM((s_k, 2 * LANES), BF16))
    else:
        out_specs += [key_spec(0, 1), key_spec(0, 1)]
        out_shape += [jax.ShapeDtypeStruct((s_k, heads * LANES), BF16)] * 2
    scratch += [pltpu.VMEM((s_k, q_w), F32), pltpu.VMEM((s_k, LANES), F32)]
    return pl.pallas_call(
        body,
        grid=(heads, nq),
        in_specs=in_specs,
        out_specs=out_specs,
        out_shape=out_shape,
        scratch_shapes=scratch,
        compiler_params=_params("arbitrary", "arbitrary"),
        name=name,
    )(*args)


def _shift_rows(u, rows):
    t = lax.broadcasted_iota(jnp.int32, u.shape, 0)
    prev = jnp.where(t == 0, 0.0, pltpu.roll(u, 1, 0))
    nxt = jnp.where(t == rows - 1, 0.0, pltpu.roll(u, rows - 1, 0))
    return prev, nxt


def _conv_fwd(z, conv_w, cat, name):
    rows = z.shape[0]
    nblk = CONV_W // LANES

    def body(gb_ref, gc_ref, xin_ref, w_ref, cat_ref, o_ref):
        del cat_ref
        u = gc_ref[...].astype(F32) * xin_ref[...].astype(F32)
        prev, nxt = _shift_rows(u, rows)
        conv = prev * w_ref[0:1, :] + u * w_ref[1:2, :] + nxt * w_ref[2:3, :]
        o_ref[...] = (gb_ref[...].astype(F32) * conv).astype(o_ref.dtype)

    col = lambda c0: pl.BlockSpec((rows, LANES), lambda j: (0, c0 // LANES + j))
    return pl.pallas_call(
        body,
        grid=(nblk,),
        in_specs=[col(Z_GB), col(Z_GC), col(Z_XIN), pl.BlockSpec((3, LANES), lambda j: (0, j)), ANY],
        out_specs=col(MLA_W),
        out_shape=jax.ShapeDtypeStruct(cat.shape, cat.dtype),
        input_output_aliases={4: 0},
        compiler_params=_params("parallel"),
        name=name,
    )(z, z, z, conv_w, cat)


def _conv_bwd(z, conv_w, dcat, name):
    rows = z.shape[0]
    nblk = CONV_W // LANES

    def body(gb_ref, gc_ref, xin_ref, w_ref, dc_ref, dgb_ref, dgc_ref, dxin_ref, dw_ref):
        gc = gc_ref[...].astype(F32)
        xin = xin_ref[...].astype(F32)
        dc = dc_ref[...].astype(F32)
        u = gc * xin
        prev, nxt = _shift_rows(u, rows)
        w0, w1, w2 = w_ref[0:1, :], w_ref[1:2, :], w_ref[2:3, :]
        conv = prev * w0 + u * w1 + nxt * w2
        dgb_ref[...] = (dc * conv).astype(dgb_ref.dtype)
        dconv = dc * gb_ref[...].astype(F32)
        dw_ref[0:1, :] = jnp.sum(dconv * prev, axis=0, keepdims=True)
        dw_ref[1:2, :] = jnp.sum(dconv * u, axis=0, keepdims=True)
        dw_ref[2:3, :] = jnp.sum(dconv * nxt, axis=0, keepdims=True)
        dprev, dnxt = _shift_rows(dconv, rows)
        du = dnxt * w0 + dconv * w1 + dprev * w2
        dgc_ref[...] = (du * xin).astype(dgc_ref.dtype)
        dxin_ref[...] = (du * gc).astype(dxin_ref.dtype)

    col = lambda c0: pl.BlockSpec((rows, LANES), lambda j: (0, c0 // LANES + j))
    w_spec = pl.BlockSpec((3, LANES), lambda j: (0, j))
    piece = jax.ShapeDtypeStruct((rows, CONV_W), BF16)
    return pl.pallas_call(
        body,
        grid=(nblk,),
        in_specs=[col(Z_GB), col(Z_GC), col(Z_XIN), w_spec, col(MLA_W)],
        out_specs=[col(0), col(0), col(0), w_spec],
        out_shape=[piece, piece, piece, jax.ShapeDtypeStruct((3, CONV_W), F32)],
        compiler_params=_params("parallel"),
        name=name,
    )(z, z, z, conv_w, dcat)


def _gate_fwd(cat, z, name):
    rows = cat.shape[0]
    tr = _tile(rows, 256, 16)
    tc = MIX_W
    g0 = Z_GATE // tc

    def body(c_ref, g_ref, y_ref):
        g = g_ref[...].astype(F32)
        y_ref[...] = (c_ref[...].astype(F32) * (g * jax.nn.sigmoid(g))).astype(y_ref.dtype)

    blk = pl.BlockSpec((tr, tc), lambda i, j: (i, j))
    return pl.pallas_call(
        body,
        grid=(rows // tr, MIX_W // tc),
        in_specs=[blk, pl.BlockSpec((tr, tc), lambda i, j: (i, g0 + j))],
        out_specs=blk,
        out_shape=jax.ShapeDtypeStruct((rows, MIX_W), BF16),
        compiler_params=_params("parallel", "parallel"),
        name=name,
    )(cat, z)


def _out_proj_dx_gate_bwd(do, w_o, cat, z, name):
    rows, k = do.shape
    tm = _tile(rows, 1024, 16)
    tn = _tile(MIX_W, 1024, LANES)
    g0 = Z_GATE // tn

    def body(do_ref, w_ref, c_ref, g_ref, dcat_ref, dgate_ref):
        dy = lax.dot_general(do_ref[...], w_ref[...], (((1,), (1,)), ((), ())), preferred_element_type=F32)
        g = g_ref[...].astype(F32)
        sg = jax.nn.sigmoid(g)
        dcat_ref[...] = (dy * (g * sg)).astype(dcat_ref.dtype)
        dgate_ref[...] = (dy * c_ref[...].astype(F32) * (sg * (1.0 + g * (1.0 - sg)))).astype(dgate_ref.dtype)

    blk = pl.BlockSpec((tm, tn), lambda i, j: (i, j))
    out = jax.ShapeDtypeStruct((rows, MIX_W), BF16)
    return pl.pallas_call(
        body,
        grid=(rows // tm, MIX_W // tn),
        in_specs=[pl.BlockSpec((tm, k), lambda i, j: (i, 0)), pl.BlockSpec((tn, k), lambda i, j: (j, 0)), blk,
                  pl.BlockSpec((tm, tn), lambda i, j: (i, g0 + j))],
        out_specs=[blk, blk],
        out_shape=[out, out],
        compiler_params=_params("parallel", "parallel"),
        name=name,
    )(do, w_o, cat, z)


def _loss_head(y, target, name):
    rows, width = y.shape
    tr = _tile(rows, 256, 8)

    def body(y_ref, t_ref, g_ref, loss_ref):
        i = pl.program_id(0)
        d = y_ref[...] - t_ref[...]
        g_ref[...] = d / width
        part = 0.5 * jnp.sum(jnp.mean(d * d, axis=-1, keepdims=True), axis=0, keepdims=True)
        part = jnp.broadcast_to(part, loss_ref.shape)

        @pl.when(i == 0)
        def _():
            loss_ref[...] = part

        @pl.when(i > 0)
        def _():
            loss_ref[...] += part

    row_spec = pl.BlockSpec((tr, width), lambda i: (i, 0))
    return pl.pallas_call(
        body,
        grid=(rows // tr,),
        in_specs=[row_spec, row_spec],
        out_specs=[row_spec, pl.BlockSpec((1, LANES), lambda i: (0, 0))],
        out_shape=[jax.ShapeDtypeStruct((rows, width), F32), jax.ShapeDtypeStruct((1, LANES), F32)],
        compiler_params=_params("arbitrary"),
        name=name,
    )(y, target)


CHIP_FLIPS = ((1, 0), (0, 1), (1, 1))
ANY = pl.BlockSpec(memory_space=pl.ANY)


def _chip_copies(pieces, sems, n_slot):
    send_sems, recv_sems, local_sems = sems
    x, y, c = lax.axis_index("x"), lax.axis_index("y"), lax.axis_index("c")
    me = 2 * x + y

    def remote(j, k, a, src, dst):
        fx, fy = CHIP_FLIPS[k]
        return pltpu.make_async_remote_copy(
            src_ref=src, dst_ref=dst, send_sem=send_sems.at[n_slot * k + a], recv_sem=recv_sems.at[n_slot * k + a],
            device_id=((j // 2) ^ fx, (j % 2) ^ fy, c), device_id_type=MESH_ID)

    def peer(j, k):
        fx, fy = CHIP_FLIPS[k]
        return 2 * ((j // 2) ^ fx) + ((j % 2) ^ fy)

    def start_as(j):
        def run():
            for a, (src, dst) in enumerate(pieces(j, j)):
                pltpu.make_async_copy(src, dst, local_sems.at[a]).start()
            for k in range(len(CHIP_FLIPS)):
                for a, (src, dst) in enumerate(pieces(j, peer(j, k))):
                    remote(j, k, a, src, dst).start()
        return run

    def wait_as(j):
        def run():
            for a, (src, dst) in enumerate(pieces(j, j)):
                pltpu.make_async_copy(src, dst, local_sems.at[a]).wait()
            for k in range(len(CHIP_FLIPS)):
                for a, (src, dst) in enumerate(pieces(j, peer(j, k))):
                    remote(j, k, a, src, dst).wait_send()
                for a, (src, dst) in enumerate(pieces(peer(j, k), j)):
                    remote(j, k, a, src, dst).wait_recv()
        return run

    def start():
        for j in range(N_CHIPS):
            pl.when(me == j)(start_as(j))

    def wait():
        for j in range(N_CHIPS):
            pl.when(me == j)(wait_as(j))

    return start, wait


IN_PIECES = ((0, Q_RANK, Z_QLAT), (Q_RANK, KV_RANK, Z_KVLAT), (Q_RANK + KV_RANK, ROPE, Z_KPE),
             (Q_RANK + KV_RANK + ROPE, CONV_W, Z_GB), (Q_RANK + KV_RANK + ROPE + CONV_W, CONV_W, Z_GC),
             (Q_RANK + KV_RANK + ROPE + 2 * CONV_W, CONV_W, Z_XIN), (Q_RANK + KV_RANK + ROPE + 3 * CONV_W, MEM_W, Z_QMEM),
             (Q_RANK + KV_RANK + ROPE + 3 * CONV_W + MEM_W, MIX_W, Z_GATE))
IN_SHARD = IN_COLS // N_CHIPS


def _in_segments(j):
    lo, hi = j * IN_SHARD, (j + 1) * IN_SHARD
    segs = []
    for r0, width, z0 in IN_PIECES:
        a, b = max(lo, r0), min(hi, r0 + width)
        if a < b:
            segs.append((a - lo, z0 + a - r0, b - a))
    return segs


N_SLOT = 11


def _gather_plan(l, shards, zero_rows, part="all"):
    s_in, s_uq, s_ukv, s_conv, s_mk, s_mv, s_o = shards
    ukv_c, mk_r, mk_c, o_r = s_ukv.shape[2], s_mk.shape[1], s_mk.shape[2], s_o.shape[1]
    stack = lambda s: jax.ShapeDtypeStruct((N_CHIPS,) + s.shape[1:], s.dtype)
    in_ins, in_outs = [s_in, zero_rows], [jax.ShapeDtypeStruct((Z_COLS, s_in.shape[2]), s_in.dtype)]
    early_ins = [s_uq, s_ukv, s_conv]
    early_outs = [stack(s_uq), jax.ShapeDtypeStruct((s_ukv.shape[1], N_CHIPS * ukv_c), s_ukv.dtype), stack(s_conv)]
    late_ins = [s_mk, s_mv, s_o]
    late_outs = [jax.ShapeDtypeStruct((N_CHIPS * mk_r, 2 * mk_c), s_mk.dtype),
                 jax.ShapeDtypeStruct((N_CHIPS * o_r, s_o.shape[2]), s_o.dtype)]
    with_in, with_early, with_late = part in ("all", "in"), part in ("all", "early"), part in ("all", "late")

    def build(ins, outs, sems):
        ins, outs = list(ins), list(outs)
        if with_in:
            r_in, r_zero, f_in = ins.pop(0), ins.pop(0), outs.pop(0)
        if with_early:
            r_uq, r_ukv, r_conv = ins.pop(0), ins.pop(0), ins.pop(0)
            g_uq, f_ukv, g_conv = outs.pop(0), outs.pop(0), outs.pop(0)
        if with_late:
            r_mk, r_mv, r_o = ins
            f_mkv, f_o = outs

        def pieces(j, t):
            out = []
            if with_in:
                out += [(r_in.at[l, pl.ds(so, n), :], f_in.at[pl.ds(zo, n), :]) for so, zo, n in _in_segments(j)]
            if with_early:
                out += [(r_uq.at[l], g_uq.at[j]), (r_ukv.at[l], f_ukv.at[:, pl.ds(j * ukv_c, ukv_c)]),
                        (r_conv.at[l], g_conv.at[j])]
            if with_late:
                out += [(r_mk.at[l], f_mkv.at[pl.ds(j * mk_r, mk_r), pl.ds(0, mk_c)]),
                        (r_mv.at[l], f_mkv.at[pl.ds(j * mk_r, mk_r), pl.ds(mk_c, mk_c)]),
                        (r_o.at[l], f_o.at[pl.ds(j * o_r, o_r), :])]
            if with_in and j == t:
                out.append((r_zero, f_in.at[pl.ds(Z_KPE + ROPE, LANES - ROPE), :]))
            return out

        return _chip_copies(pieces, sems, N_SLOT)

    ins = (in_ins if with_in else []) + (early_ins if with_early else []) + (late_ins if with_late else [])
    outs = (in_outs if with_in else []) + (early_outs if with_early else []) + (late_outs if with_late else [])
    return _CommPlan(ins, outs, build, len(CHIP_FLIPS) * N_SLOT)


def _scatter_plan(dwt_in, c_uq, dw_ukv, c_conv, dw_mkv, dw_o, part="all"):
    ukv_c, mk_r, mk_c, o_r = dw_ukv.shape[1] // N_CHIPS, dw_mkv.shape[0] // N_CHIPS, dw_mkv.shape[1] // 2, dw_o.shape[0] // N_CHIPS
    with_in, with_rest = part != "rest", part != "in"
    in_outs = [jax.ShapeDtypeStruct((N_CHIPS, IN_SHARD, D_MODEL), BF16)]
    rest_ins = [c_uq, dw_ukv, c_conv, dw_mkv, dw_o]
    rest_outs = [jax.ShapeDtypeStruct(c_uq.shape, c_uq.dtype),
                 jax.ShapeDtypeStruct((N_CHIPS, dw_ukv.shape[0], ukv_c), dw_ukv.dtype),
                 jax.ShapeDtypeStruct(c_conv.shape, c_conv.dtype),
                 jax.ShapeDtypeStruct((N_CHIPS, mk_r, mk_c), dw_mkv.dtype), jax.ShapeDtypeStruct((N_CHIPS, mk_r, mk_c), dw_mkv.dtype),
                 jax.ShapeDtypeStruct((N_CHIPS, o_r, dw_o.shape[1]), dw_o.dtype)]

    def build(ins, outs, sems):
        ins, outs = list(ins), list(outs)
        if with_in:
            r_in, o_in = ins.pop(0), outs.pop(0)
        if with_rest:
            r_uq, r_ukv, r_conv, r_mkv, r_o = ins
            o_uq, o_ukv, o_conv, o_mk, o_mv, o_o = outs

        def pieces(j, t):
            out = []
            if with_in:
                out += [(r_in.at[pl.ds(zo, n), :], o_in.at[j, pl.ds(so, n), :]) for so, zo, n in _in_segments(t)]
            if with_rest:
                out += [(r_uq.at[t], o_uq.at[j]), (r_ukv.at[:, pl.ds(t * ukv_c, ukv_c)], o_ukv.at[j]),
                        (r_conv.at[t], o_conv.at[j]),
                        (r_mkv.at[pl.ds(t * mk_r, mk_r), pl.ds(0, mk_c)], o_mk.at[j]),
                        (r_mkv.at[pl.ds(t * mk_r, mk_r), pl.ds(mk_c, mk_c)], o_mv.at[j]),
                        (r_o.at[pl.ds(t * o_r, o_r), :], o_o.at[j])]
            return out

        return _chip_copies(pieces, sems, N_SLOT)

    ins = ([dwt_in] if with_in else []) + (rest_ins if with_rest else [])
    outs = (in_outs if with_in else []) + (rest_outs if with_rest else [])
    return _CommPlan(ins, outs, build, len(CHIP_FLIPS) * N_SLOT)


HBM = pl.BlockSpec(memory_space=pltpu.HBM)
SEM = pl.BlockSpec(memory_space=pltpu.SEMAPHORE)
SIDE_EFFECT = pltpu.SideEffectType.DATAFLOW_SIDE_EFFECTING


def _comm_start(plan, after, name):
    n_in, n_out, n_after = len(plan.ins), len(plan.out_shape), len(after)
    n_buf = n_in + n_out

    def body(*refs):
        bufs, sems, token = refs[:n_buf], refs[n_buf + n_after:n_buf + n_after + 3], refs[-1]
        start, _ = plan.build(bufs[:n_in], bufs[n_in:], sems)
        start()
        token[...] = jnp.zeros_like(token)

    lands = [lax.empty(s.shape, s.dtype) for s in plan.out_shape]
    args = [pltpu.with_memory_space_constraint(a, pltpu.HBM) for a in list(plan.ins) + lands]
    res = pl.pallas_call(
        body,
        in_specs=[HBM] * n_buf + [ANY] * n_after,
        out_specs=[SEM] * 3 + [HBM] * n_out + [pl.BlockSpec(memory_space=pltpu.VMEM)],
        out_shape=plan.scratch() + [pltpu.HBM(a.shape, a.dtype) for a in lands] + [jax.ShapeDtypeStruct((8, LANES), F32)],
        input_output_aliases={n_in + i: 3 + i for i in range(n_out)},
        compiler_params=pltpu.CompilerParams(has_side_effects=SIDE_EFFECT),
        name=name,
    )(*args, *after)
    return list(res[:3]), list(res[3:3 + n_out]), res[-1]


def _comm_finish(plan, started, after, name):
    sems, lands, _ = started
    n_in, n_out = len(plan.ins), len(plan.out_shape)
    n_buf = n_in + n_out

    def body(*refs):
        bufs_in, sem_refs = refs[:n_buf], refs[n_buf:n_buf + 3]
        _, wait = plan.build(bufs_in[:n_in], bufs_in[n_in:], sem_refs)
        wait()

    sources = [pltpu.with_memory_space_constraint(a, pltpu.HBM) for a in plan.ins]
    res = pl.pallas_call(
        body,
        in_specs=[HBM] * n_buf + [SEM] * 3 + [ANY] * len(after),
        out_specs=[HBM] * n_out,
        out_shape=[pltpu.HBM(b.shape, b.dtype) for b in lands],
        input_output_aliases={n_in + i: i for i in range(n_out)},
        compiler_params=pltpu.CompilerParams(has_side_effects=SIDE_EFFECT),
        name=name,
    )(*sources, *lands, *sems, *after)
    return list(res)


def _sibling_plan(arrays):
    def build(ins, outs, sems):
        send_sems, recv_sems, _ = sems
        sibling = (lax.axis_index("x"), lax.axis_index("y"), 1 - lax.axis_index("c"))
        copies = [pltpu.make_async_remote_copy(src_ref=src, dst_ref=dst, send_sem=send_sems.at[a], recv_sem=recv_sems.at[a],
                                               device_id=sibling, device_id_type=MESH_ID)
                  for a, (src, dst) in enumerate(zip(ins, outs))]

        def start():
            for cp in copies:
                cp.start()

        def wait():
            for cp in copies:
                cp.wait()

        return start, wait

    return _CommPlan(arrays, [jax.ShapeDtypeStruct(v.shape, v.dtype) for v in arrays], build, len(arrays))


DEVICE_FLIPS = tuple((fx, fy, fc) for fx in (0, 1) for fy in (0, 1) for fc in (0, 1))[1:]


def _gather_all(v, after, name):
    def body(v_ref, after_ref, out_ref, send_sems, recv_sems, local_sem):
        del after_ref
        x, y, c = lax.axis_index("x"), lax.axis_index("y"), lax.axis_index("c")
        me = 4 * x + 2 * y + c
        local = pltpu.make_async_copy(v_ref, out_ref.at[me], local_sem)
        local.start()
        copies = [local]
        for k, (fx, fy, fc) in enumerate(DEVICE_FLIPS):
            cp = pltpu.make_async_remote_copy(
                src_ref=v_ref, dst_ref=out_ref.at[me], send_sem=send_sems.at[k], recv_sem=recv_sems.at[k],
                device_id=((x + fx) % 2, (y + fy) % 2, (c + fc) % 2), device_id_type=MESH_ID)
            cp.start()
            copies.append(cp)
        for cp in copies:
            cp.wait()

    return pl.pallas_call(
        body,
        in_specs=[ANY, ANY],
        out_specs=ANY,
        out_shape=jax.ShapeDtypeStruct((N_DEV,) + v.shape, v.dtype),
        scratch_shapes=[pltpu.SemaphoreType.DMA((N_DEV - 1,)), pltpu.SemaphoreType.DMA((N_DEV - 1,)), pltpu.SemaphoreType.DMA],
        name=name,
    )(v, after)


def _sum_slots(parts, name):
    n, rows, cols = parts.shape
    tr = _tile(rows, 256, 16)

    def body(p_ref, o_ref):
        acc = p_ref[0].astype(F32)
        for k in range(1, n):
            acc = acc + p_ref[k].astype(F32)
        o_ref[...] = acc

    return pl.pallas_call(
        body,
        grid=(rows // tr,),
        in_specs=[pl.BlockSpec((n, tr, cols), lambda i: (0, i, 0))],
        out_specs=pl.BlockSpec((tr, cols), lambda i: (i, 0)),
        out_shape=jax.ShapeDtypeStruct((rows, cols), F32),
        compiler_params=_params("parallel"),
        name=name,
    )(parts)


def _adamw_math(w, g, m, v):
    m_new = ADAM_B1 * m + (1.0 - ADAM_B1) * g
    v_new = ADAM_B2 * v + (1.0 - ADAM_B2) * jnp.square(g)
    m_hat = m_new / (1.0 - ADAM_B1 ** ADAM_STEP)
    v_hat = v_new / (1.0 - ADAM_B2 ** ADAM_STEP)
    return -ADAM_LR * (m_hat / (jnp.sqrt(v_hat) + ADAM_EPS) + ADAM_WD * w), m_new, v_new


def _adamw(w, g, m, v, name):
    rows, cols = w.shape
    tr = _tile(rows, 256, 8)

    def body(w_ref, g_ref, m_ref, v_ref, d_out, m_out, v_out):
        d_out[...], m_out[...], v_out[...] = _adamw_math(w_ref[...], g_ref[...], m_ref[...], v_ref[...])

    blk = pl.BlockSpec((tr, cols), lambda i: (i, 0))
    out = jax.ShapeDtypeStruct((rows, cols), F32)
    return pl.pallas_call(
        body,
        grid=(rows // tr,),
        in_specs=[blk] * 4,
        out_specs=[blk] * 3,
        out_shape=[out] * 3,
        compiler_params=_params("parallel"),
        name=name,
    )(w, g, m, v)


def _adamw_layer(l, w, g_a, g_b, m, v, prev, name):
    depth, rows, cols = w.shape
    tr = _tile(rows, 256, 8)

    def body(w_ref, ga_ref, gb_ref, m_ref, v_ref, *rest):
        g_out, d_out, m_out, v_out = rest[-4:]
        g = ga_ref[...] + gb_ref[...]
        g_out[...] = g
        d_out[...], m_out[...], v_out[...] = _adamw_math(w_ref[...], g, m_ref[...], v_ref[...])

    stacked = pl.BlockSpec((None, tr, cols), lambda i: (l, i, 0))
    flat = pl.BlockSpec((tr, cols), lambda i: (i, 0))
    in_specs = [stacked, flat, flat, stacked, stacked]
    args = [w, g_a, g_b, m, v]
    aliases = {}
    if prev is not None:
        in_specs += [ANY] * 4
        args += list(prev)
        aliases = {5 + k: k for k in range(4)}
    out = jax.ShapeDtypeStruct((depth, rows, cols), F32)
    return pl.pallas_call(
        body,
        grid=(rows // tr,),
        in_specs=in_specs,
        out_specs=[stacked] * 4,
        out_shape=[out] * 4,
        input_output_aliases=aliases,
        compiler_params=_params("parallel"),
        name=name,
    )(*args)


def _cols_from_shards(g):
    _, r, c = g.shape
    return jnp.transpose(g, (1, 0, 2)).reshape(r, N_CHIPS * c)


def _cols_to_shards(full):
    r, c4 = full.shape
    c = c4 // N_CHIPS
    return jnp.transpose(full.reshape(r, N_CHIPS, c), (1, 0, 2))


def _w_uq_pad(w_uq):
    r, _ = w_uq.shape
    w = jnp.pad(w_uq.reshape(r, MLA_HEADS, QK_HEAD), ((0, 0), (0, 0), (0, QPAD - QK_HEAD)))
    return w.reshape(r, MLA_HEADS * QPAD)


def _w_uq_unpad(w):
    r, _ = w.shape
    return w.reshape(r, MLA_HEADS, QPAD)[..., :QK_HEAD].reshape(r, MLA_HEADS * QK_HEAD)


def _rope_tables(positions):
    inv_freq = 1.0 / (ROPE_THETA ** (jnp.arange(0, ROPE, 2, dtype=F32) / ROPE))
    ang = positions.astype(F32)[:, None] * inv_freq
    cos, sin = jnp.cos(ang), jnp.sin(ang)
    s = positions.shape[0]
    zero = jnp.zeros((s, HALF_ROPE), F32)
    pad = jnp.zeros((s, LANES - ROPE), F32)
    kc = jnp.concatenate([cos, cos, pad], axis=-1)
    ka = jnp.concatenate([-sin, zero, pad], axis=-1)
    kb = jnp.concatenate([zero, sin, pad], axis=-1)
    qc = jnp.concatenate([jnp.ones((s, NOPE), F32), kc], axis=-1)
    qa = jnp.concatenate([jnp.zeros((s, NOPE), F32), ka], axis=-1)
    qb = jnp.concatenate([jnp.zeros((s, NOPE), F32), kb], axis=-1)
    return (qc, qa, qb), (kc, ka, kb)


def _layer_weights(gathered):
    return (gathered[0],) + _late_weights(gathered[1:])


def _late_weights(gathered):
    g_uq, w_ukv, g_conv, w_mkv, w_o = gathered
    return (_w_uq_pad(_cols_from_shards(g_uq)), w_ukv, _cols_from_shards(g_conv), w_mkv, w_o)


def _layer_fwd(l, x, mem, wts, gains, tabs, late=None, h=None, next_g_pre=None):
    wt_in = wts[0]
    g_pre, g_q, g_kv, g_mem, g_post = gains
    q_tab, k_tab = tabs
    tag = f"l{l}_"
    if h is None:
        h = _rmsnorm_fwd(x, g_pre, 0, D_MODEL, tag + "pre_norm")
    z = _matmul(h, wt_in, "nt", BF16, tag + "in_proj", tm_cap=1024, tn_cap=1664)
    w_uq, w_ukv, conv_w, w_mkv, w_o = wts[1:] if late is None else late(z)
    wts = (wt_in, w_uq, w_ukv, conv_w, w_mkv, w_o)
    qn, kvn, kpe = _latent_prep(z, g_q, g_kv, *k_tab, tag + "latent_prep")
    q_raw = _matmul(qn, w_uq, "nn", BF16, tag + "uq", tm_cap=1024)
    kv = _matmul(kvn, w_ukv, "nn", BF16, tag + "ukv")
    cat, a_lse = _attn_fwd(q_raw, kv, kpe, kv, q_tab, MLA_HEADS, QPAD, 0, 0, 2, 1, 2, QK_HEAD ** -0.5, 512,
                           tag + "mla_fwd", o_into=(MIX_W, 0, None))
    cat = _conv_fwd(z, conv_w, cat, tag + "conv_fwd")
    mem_n = _rmsnorm_fwd(mem, g_mem, 0, D_MODEL, tag + "mem_norm")
    mkv = _matmul(mem_n, w_mkv, "nn", BF16, tag + "mem_kv")
    cat, m_lse = _attn_fwd(z, mkv, None, mkv, None, MEM_HEADS, LANES, Z_QMEM // LANES, 0, 1, MEM_HEADS, 1,
                           MEM_HEAD ** -0.5, 4096, tag + "mem_fwd", o_into=(MIX_W, (MLA_W + CONV_W) // LANES, cat))
    y = _gate_fwd(cat, z, tag + "gate_fwd")
    o = _matmul(y, w_o, "nn", BF16, tag + "out_proj", tm_cap=1024)
    x_new, h_next = _post_norm_residual(x, o, g_post, next_g_pre, tag + "post_norm")
    saved = (x, h, z, qn, kvn, q_raw, kv, kpe, a_lse, mem_n, mkv, m_lse, cat, y, o)
    return x_new, saved, h_next


def _layer_bwd(l, g, mem, saved, wts, gains, tabs, split_exchange=False):
    wt_in, w_uq, w_ukv, conv_w, w_mkv, w_o = wts
    g_pre, g_q, g_kv, g_mem, g_post = gains
    q_tab, k_tab = tabs
    x, h, z, qn, kvn, q_raw, kv, kpe, a_lse, mem_n, mkv, m_lse, cat, y, o = saved
    tag = f"l{l}_"
    do, dg_post = _rmsnorm_bwd(o, g_post, g, None, 0, D_MODEL, BF16, tag + "post_norm_bwd")
    dcat, dgate = _out_proj_dx_gate_bwd(do, w_o, cat, z, tag + "out_proj_dx")
    dw_o = _matmul(y, do, "tn", BF16, tag + "out_proj_dw", tm_cap=1024)
    dq, dkv, dkpe_h = _attn_bwd(q_raw, kv, kpe, kv, cat, dcat, a_lse, q_tab, MLA_HEADS, QPAD, 0, 0, 2, 1, 2, 0,
                                QK_HEAD ** -0.5, 512, tag + "mla_bwd")
    dw_ukv = _matmul(kvn, dkv, "tn", BF16, tag + "ukv_dw")
    dkvn = _matmul(dkv, w_ukv, "nt", BF16, tag + "ukv_dx")
    dw_uq = _matmul(qn, dq, "tn", BF16, tag + "uq_dw")
    dqn = _matmul(dq, w_uq, "nt", BF16, tag + "uq_dx")
    dq_lat, dkv_lat, dkpe, dg_q, dg_kv = _latent_prep_bwd(z, g_q, g_kv, dqn, dkvn, dkpe_h, *k_tab, MLA_HEADS,
                                                          tag + "latent_prep_bwd")
    dgb, dgc, dxin, dconv_w = _conv_bwd(z, conv_w, dcat, tag + "conv_bwd")
    dq_mem, dmk, dmv = _attn_bwd(z, mkv, None, mkv, cat, dcat, m_lse, None, MEM_HEADS, LANES, Z_QMEM // LANES, 0, 1,
                                 MEM_HEADS, 1, (MLA_W + CONV_W) // LANES, MEM_HEAD ** -0.5, 2048, tag + "mem_bwd")
    dmkv = jnp.concatenate([dmk, dmv], axis=-1)
    dw_mkv = _matmul(mem_n, dmkv, "tn", BF16, tag + "mem_kv_dw")
    dmem_n = _matmul(dmkv, w_mkv, "nt", F32, tag + "mem_kv_dx")
    _, dg_mem = _rmsnorm_bwd(mem, g_mem, dmem_n, None, 0, D_MODEL, BF16, tag + "mem_norm_bwd")
    others = (_cols_to_shards(_w_uq_unpad(dw_uq)), dw_ukv, _cols_to_shards(dconv_w), dw_mkv, dw_o)
    early = None
    if split_exchange:
        early_plan = _scatter_plan(None, *others, part="rest")
        early = (early_plan, _comm_start(early_plan, [dmem_n], tag + "exchange_rest_start"))
        g_pre = g_pre + early[1][2][0:1, 0:1]
    dz = jnp.concatenate([dgate, dq_lat, dkv_lat, dkpe, dgb, dgc, dxin, dq_mem], axis=-1)
    dwt_in = _matmul(dz, h, "tn", BF16, tag + "in_proj_dw", tm_cap=1664, tk_cap=2048)
    contrib = _scatter_plan(dwt_in, *others, part="in" if split_exchange else "all")
    late = None
    if split_exchange:
        late = (contrib, _comm_start(contrib, [dwt_in], tag + "exchange_in_start"))
    dh = _matmul(dz, wt_in, "nn", BF16, tag + "in_proj_dx", tm_cap=1024, tk_cap=1664, after=late[1][2] if late else None)
    dx, dg_pre = _rmsnorm_bwd(x, g_pre, dh, g, 0, D_MODEL, F32, tag + "pre_norm_bwd")
    return dx, contrib, (dg_pre, dg_q, dg_kv, dg_mem, dg_post), (early, late)


GAIN_WIDTHS = (D_MODEL, Q_RANK, KV_RANK, D_MODEL, D_MODEL)


def _pack_gains(parts):
    return jnp.concatenate([p.reshape(-1) for p in parts]).reshape(-1, LANES)


def _unpack_gains(packed, depth):
    flat = packed.reshape(-1)
    out, at = [], 0
    for width in GAIN_WIDTHS:
        out.append(flat[at:at + depth * width].reshape(depth, width))
        at += depth * width
    return out


def kernel(x, mem, positions, pre_norm_g, w_in, q_norm_g, w_uq, kv_norm_g, w_ukv, conv_w, mem_norm_g, w_mk, w_mv, w_o, post_norm_g, loss_target, m_pre_norm_g, m_w_in, m_q_norm_g, m_w_uq, m_kv_norm_g, m_w_ukv, m_conv_w, m_mem_norm_g, m_w_mk, m_w_mv, m_w_o, m_post_norm_g, v_pre_norm_g, v_w_in, v_q_norm_g, v_w_uq, v_kv_norm_g, v_w_ukv, v_conv_w, v_mem_norm_g, v_w_mk, v_w_mv, v_w_o, v_post_norm_g):
    depth = w_in.shape[0]
    x0, mem0, target = x[0], mem[0], loss_target[0]
    tabs = _rope_tables(positions[0])

    flip = lambda t: jnp.transpose(t, (0, 2, 1))
    w_in, m_w_in, v_w_in = flip(w_in), flip(m_w_in), flip(v_w_in)
    shards = [w_in.astype(BF16), w_uq.astype(BF16), w_ukv.astype(BF16), conv_w, w_mk.astype(BF16), w_mv.astype(BF16),
              w_o.astype(BF16)]
    zero_rows = lambda: jnp.zeros((LANES - ROPE, D_MODEL), BF16)

    def layer_gains(l):
        return tuple(g[l][None, :] for g in (pre_norm_g, q_norm_g, kv_norm_g, mem_norm_g, post_norm_g))

    wts, saved = [None] * depth, [None] * depth
    plan_in, plan_early, plan_late = (_gather_plan(0, shards, zero_rows(), part) for part in ("in", "early", "late"))
    started_in = _comm_start(plan_in, [positions], "l0_gather_in_start")
    started_early = _comm_start(plan_early, [started_in[2]], "l0_gather_early_start")
    started_late = _comm_start(plan_late, [started_early[2]], "l0_gather_late_start")
    h0 = _rmsnorm_fwd(x0, layer_gains(0)[0], 0, D_MODEL, "l0_pre_norm")
    wts[0] = tuple(_comm_finish(plan_in, started_in, [started_late[2], h0, tabs[0][0]], "l0_gather_in_wait"))

    next_gather = {}

    def start_next_gather(l, after):
        plan = _gather_plan(l + 1, shards, zero_rows())
        next_gather[l + 1] = (plan, _comm_start(plan, [after], f"l{l + 1}_gather_start"))
        return next_gather[l + 1][1][2][0:1, 0:1]

    def rest_of_layer0(z):
        early = _comm_finish(plan_early, started_early, [z], "l0_gather_early_wait")
        late = _comm_finish(plan_late, started_late, [early[1]], "l0_gather_late_wait")
        got = _late_weights(early + late)
        wts[0] = wts[0] + got
        if depth > 1:
            got = (got[0] + start_next_gather(0, early[1]).astype(BF16),) + got[1:]
        return got

    act, h_act = x0, h0
    for l in range(depth):
        gains = layer_gains(l)
        if 0 < l < depth - 1:
            gains = gains[:1] + (gains[1] + start_next_gather(l, wts[l][5]),) + gains[2:]
        act, saved[l], h_act = _layer_fwd(l, act, mem0, wts[l], gains, tabs, rest_of_layer0 if l == 0 else None, h_act,
                                          layer_gains(l + 1)[0] if l + 1 < depth else None)
        if l + 1 < depth:
            plan, started = next_gather[l + 1]
            wts[l + 1] = _layer_weights(_comm_finish(plan, started, [act], f"l{l + 1}_gather_wait"))
    grad, loss_part = _loss_head(act, target, "loss_head")
    loss = lax.psum(loss_part[0, 0], ("x", "y", "c"))

    names = ("w_in", "w_uq", "w_ukv", "conv_w", "w_mk", "w_mv", "w_o")
    w_shards = (w_in, w_uq, w_ukv, conv_w, w_mk, w_mv, w_o)
    m_shards = (m_w_in, m_w_uq, m_w_ukv, m_conv_w, m_w_mk, m_w_mv, m_w_o)
    v_shards = (v_w_in, v_w_uq, v_w_ukv, v_conv_w, v_w_mk, v_w_mv, v_w_o)
    stacked = [None] * len(names)

    def sum_and_send(l, received):
        partial = [_sum_slots(r, f"l{l}_grad_sum_{names[i]}") for i, r in enumerate(received)]
        plan = _sibling_plan(partial)
        return l, partial, plan, _comm_start(plan, [partial[0]], f"l{l}_sibling_start")

    def receive_and_update(state, after):
        l, partial, plan, started = state
        other = _comm_finish(plan, started, [after], f"l{l}_sibling_wait")
        for i, name in enumerate(names):
            stacked[i] = _adamw_layer(l, w_shards[i], partial[i], other[i], m_shards[i], v_shards[i], stacked[i],
                                      f"l{l}_adamw_{name}")

    dgs = [None] * depth
    pending = None
    in_flight = None
    for l in reversed(range(depth)):
        gains = layer_gains(l)
        for token in ([pending[1][2]] if pending else []) + ([in_flight[3][2]] if in_flight else []):
            gains = gains[:4] + (gains[4] + token[0:1, 0:1],)
        grad, contrib, dgs[l], early = _layer_bwd(l, grad, mem0, saved[l], wts[l], gains, tabs, l == 0)
        if in_flight is not None:
            receive_and_update(in_flight, grad)
            in_flight = None
        if pending is not None:
            in_flight = sum_and_send(l + 1, _comm_finish(pending[0], pending[1], [grad], f"l{l + 1}_exchange_wait"))
        if l > 0:
            pending = (contrib, _comm_start(contrib, [grad], f"l{l}_exchange_start"))
    early, late = early
    got_in = _comm_finish(late[0], late[1], [grad], "l0_exchange_in_wait")
    last = sum_and_send(0, got_in + _comm_finish(early[0], early[1], [got_in[0]], "l0_exchange_rest_wait"))
    if in_flight is not None:
        receive_and_update(in_flight, last[1][0])
    receive_and_update(last, stacked[0][0] if depth > 1 else last[1][0])
    grad_x = grad[None]
    results = {name: tuple(stacked[i]) for i, name in enumerate(names)}
    results["w_in"] = tuple(flip(t) for t in results["w_in"])

    gain_names = ("pre_norm_g", "q_norm_g", "kv_norm_g", "mem_norm_g", "post_norm_g")
    dg_packed = _pack_gains([jnp.concatenate([dgs[l][i] for l in range(depth)], axis=0) for i in range(5)])
    dg_total = _sum_slots(_gather_all(dg_packed, stacked[0][0], "gain_gather"), "gain_sum")
    gain_outs = (dg_total,) + tuple(_adamw(
        _pack_gains((pre_norm_g, q_norm_g, kv_norm_g, mem_norm_g, post_norm_g)), dg_total,
        _pack_gains((m_pre_norm_g, m_q_norm_g, m_kv_norm_g, m_mem_norm_g, m_post_norm_g)),
        _pack_gains((v_pre_norm_g, v_q_norm_g, v_kv_norm_g, v_mem_norm_g, v_post_norm_g)), "adamw_gains"))
    gain_outs = [_unpack_gains(t, depth) for t in gain_outs]
    for i, name in enumerate(gain_names):
        results[name] = tuple(gain_outs[k][i] for k in range(4))

    order = ("pre_norm_g", "w_in", "q_norm_g", "w_uq", "kv_norm_g", "w_ukv", "conv_w", "mem_norm_g", "w_mk", "w_mv", "w_o",
             "post_norm_g")
    out = [loss, grad_x]
    for k in range(4):
        out += [results[name][k] for name in order]
    return tuple(out)
```

```python
import jax
import jax.numpy as jnp
from jax import lax
from jax.experimental import pallas as pl
from jax.experimental.pallas import tpu as pltpu

F32 = jnp.float32
BF16 = jnp.bfloat16
MESH_ID = pl.DeviceIdType.MESH

D_MODEL = 2048
EPS = 1e-6
LOG2_E = 1.4426950408889634
ROPE_THETA = 10000.0
MLA_HEADS = 8
NOPE = 128
ROPE = 64
HALF_ROPE = ROPE // 2
QK_HEAD = NOPE + ROPE
V_HEAD = 128
Q_RANK = 512
KV_RANK = 256
CONV_W = 512
MEM_HEADS = 4
MEM_HEAD = 128
MEM_W = MEM_HEADS * MEM_HEAD
MLA_W = MLA_HEADS * V_HEAD
MIX_W = MLA_W + CONV_W + MEM_W
IN_COLS = Q_RANK + KV_RANK + ROPE + 3 * CONV_W + MEM_W + MIX_W
N_CHIPS = 4
N_DEV = 8

LANES = 128
VMEM_LIMIT_BYTES = 56 * 1024 * 1024

QPAD = 2 * LANES
Z_GATE = 0
Z_QLAT = Z_GATE + MIX_W
Z_KVLAT = Z_QLAT + Q_RANK
Z_KPE = Z_KVLAT + KV_RANK
Z_GB = Z_KPE + LANES
Z_GC = Z_GB + CONV_W
Z_XIN = Z_GC + CONV_W
Z_QMEM = Z_XIN + CONV_W
Z_COLS = Z_QMEM + MEM_W

ADAM_LR = 0.001
ADAM_B1 = 0.9
ADAM_B2 = 0.999
ADAM_EPS = 1e-08
ADAM_WD = 0.01
ADAM_STEP = 10


def _tile(dim, cap, unit):
    if dim <= cap:
        return dim
    t = (cap // unit) * unit
    while t >= unit:
        if dim % t == 0:
            return t
        t -= unit
    raise ValueError(f"no tile of {dim} under {cap} in units of {unit}")


def _params(*semantics):
    return pltpu.CompilerParams(dimension_semantics=semantics, vmem_limit_bytes=VMEM_LIMIT_BYTES)


def _matmul(a, b, mode, out_dtype, name, tm_cap=512, tn_cap=1024, tk_cap=2048, after=None):
    if mode == "nn":
        (m, k), (k2, n) = a.shape, b.shape
    elif mode == "nt":
        (m, k), (n, k2) = a.shape, b.shape
    else:
        (k, m), (k2, n) = a.shape, b.shape
    assert k == k2, (a.shape, b.shape, mode)
    tm = _tile(m, tm_cap, LANES if mode == "tn" else 16)
    tn = _tile(n, tn_cap, LANES)
    tk = _tile(k, tk_cap, LANES if mode != "tn" else 16)
    nk = k // tk
    if mode == "nn":
        a_spec = pl.BlockSpec((tm, tk), lambda i, j, kk: (i, kk))
        b_spec = pl.BlockSpec((tk, tn), lambda i, j, kk: (kk, j))
        dims = (((1,), (0,)), ((), ()))
    elif mode == "nt":
        a_spec = pl.BlockSpec((tm, tk), lambda i, j, kk: (i, kk))
        b_spec = pl.BlockSpec((tn, tk), lambda i, j, kk: (j, kk))
        dims = (((1,), (1,)), ((), ()))
    else:
        a_spec = pl.BlockSpec((tk, tm), lambda i, j, kk: (kk, i))
        b_spec = pl.BlockSpec((tk, tn), lambda i, j, kk: (kk, j))
        dims = (((0,), (0,)), ((), ()))

    def body(a_ref, b_ref, *rest):
        o_ref, scratch = (rest[1], rest[2:]) if after is not None else (rest[0], rest[1:])
        part = lax.dot_general(a_ref[...].astype(BF16), b_ref[...].astype(BF16), dims, preferred_element_type=F32)
        if nk == 1:
            o_ref[...] = part.astype(o_ref.dtype)
            return
        (acc_ref,) = scratch
        kk = pl.program_id(2)

        @pl.when(kk == 0)
        def _():
            acc_ref[...] = part

        @pl.when(kk > 0)
        def _():
            acc_ref[...] += part

        @pl.when(kk == nk - 1)
        def _():
            o_ref[...] = acc_ref[...].astype(o_ref.dtype)

    return pl.pallas_call(
        body,
        grid=(m // tm, n // tn, nk),
        in_specs=[a_spec, b_spec] + ([] if after is None else [pl.BlockSpec(memory_space=pl.ANY)]),
        out_specs=pl.BlockSpec((tm, tn), lambda i, j, kk: (i, j)),
        out_shape=jax.ShapeDtypeStruct((m, n), out_dtype),
        scratch_shapes=[] if nk == 1 else [pltpu.VMEM((tm, tn), F32)],
        compiler_params=_params("parallel", "parallel", "arbitrary"),
        name=name,
    )(*([a, b] if after is None else [a, b, after]))


def _rmsnorm_fwd(x, gain, col0, width, name):
    rows = x.shape[0]
    tr = _tile(rows, 512, 16)
    cb = col0 // width
    assert cb * width == col0

    def body(x_ref, g_ref, o_ref):
        xv = x_ref[...].astype(F32)
        r = lax.rsqrt(jnp.mean(xv * xv, axis=-1, keepdims=True) + EPS)
        o_ref[...] = (xv * r * g_ref[...]).astype(o_ref.dtype)

    return pl.pallas_call(
        body,
        grid=(rows // tr,),
        in_specs=[pl.BlockSpec((tr, width), lambda i: (i, cb)), pl.BlockSpec((1, width), lambda i: (0, 0))],
        out_specs=pl.BlockSpec((tr, width), lambda i: (i, 0)),
        out_shape=jax.ShapeDtypeStruct((rows, width), BF16),
        compiler_params=_params("parallel"),
        name=name,
    )(x, gain)


def _rmsnorm_bwd(x, gain, dy, resid, col0, width, out_dtype, name):
    rows = x.shape[0]
    tr = _tile(rows, 256, 16)
    cb = col0 // width
    assert cb * width == col0
    has_resid = resid is not None

    def body(*refs):
        if has_resid:
            x_ref, g_ref, dy_ref, res_ref, dx_ref, dg_ref = refs
        else:
            x_ref, g_ref, dy_ref, dx_ref, dg_ref = refs
        i = pl.program_id(0)
        xv = x_ref[...].astype(F32)
        dyv = dy_ref[...].astype(F32)
        r = lax.rsqrt(jnp.mean(xv * xv, axis=-1, keepdims=True) + EPS)
        xr = xv * r
        dyg = dyv * g_ref[...]
        c = jnp.mean(dyg * xr, axis=-1, keepdims=True)
        dx = r * (dyg - xr * c)
        if has_resid:
            dx = dx + res_ref[...]
        dx_ref[...] = dx.astype(dx_ref.dtype)
        part = jnp.sum(dyv * xr, axis=0, keepdims=True)

        @pl.when(i == 0)
        def _():
            dg_ref[...] = part

        @pl.when(i > 0)
        def _():
            dg_ref[...] += part

    row_spec = pl.BlockSpec((tr, width), lambda i: (i, 0))
    in_specs = [pl.BlockSpec((tr, width), lambda i: (i, cb)), pl.BlockSpec((1, width), lambda i: (0, 0)), row_spec]
    args = [x, gain, dy]
    if has_resid:
        in_specs.append(row_spec)
        args.append(resid)
    return pl.pallas_call(
        body,
        grid=(rows // tr,),
        in_specs=in_specs,
        out_specs=[row_spec, pl.BlockSpec((1, width), lambda i: (0, 0))],
        out_shape=[jax.ShapeDtypeStruct((rows, width), out_dtype), jax.ShapeDtypeStruct((1, width), F32)],
        compiler_params=_params("arbitrary"),
        name=name,
    )(*args)


def _post_norm_residual(x, o, gain, next_gain, name):
    rows, width = x.shape
    tr = _tile(rows, 256, 16)
    with_next = next_gain is not None

    def body(*refs):
        x_ref, o_ref, g_ref = refs[:3]
        ov = o_ref[...].astype(F32)
        r = lax.rsqrt(jnp.mean(ov * ov, axis=-1, keepdims=True) + EPS)
        x_new = x_ref[...] + ov * r * g_ref[...]
        if with_next:
            gn_ref, out_ref, h_ref = refs[3:]
            rn = lax.rsqrt(jnp.mean(x_new * x_new, axis=-1, keepdims=True) + EPS)
            h_ref[...] = (x_new * rn * gn_ref[...]).astype(h_ref.dtype)
        else:
            (out_ref,) = refs[3:]
        out_ref[...] = x_new

    row_spec = pl.BlockSpec((tr, width), lambda i: (i, 0))
    gain_spec = pl.BlockSpec((1, width), lambda i: (0, 0))
    res = pl.pallas_call(
        body,
        grid=(rows // tr,),
        in_specs=[row_spec, row_spec, gain_spec] + ([gain_spec] if with_next else []),
        out_specs=[row_spec] + ([row_spec] if with_next else []),
        out_shape=[jax.ShapeDtypeStruct((rows, width), F32)] + ([jax.ShapeDtypeStruct((rows, width), BF16)] if with_next else []),
        compiler_params=_params("parallel"),
        name=name,
    )(*([x, o, gain] + ([next_gain] if with_next else [])))
    return (res[0], res[1]) if with_next else (res[0], None)


def _latent_prep(z, g_q, g_kv, tab_c, tab_a, tab_b, name):
    rows = z.shape[0]
    tr = _tile(rows, 512, 16)

    def norm(x_ref, g_ref, o_ref):
        xv = x_ref[...].astype(F32)
        r = lax.rsqrt(jnp.mean(xv * xv, axis=-1, keepdims=True) + EPS)
        o_ref[...] = (xv * r * g_ref[...]).astype(o_ref.dtype)

    def body(q_ref, kv_ref, k_ref, gq_ref, gkv_ref, c_ref, a_ref, b_ref, qn_ref, kvn_ref, kpe_ref):
        norm(q_ref, gq_ref, qn_ref)
        norm(kv_ref, gkv_ref, kvn_ref)
        kpe_ref[...] = _rope_rows(k_ref[...].astype(F32), c_ref[...], a_ref[...], b_ref[...], 1).astype(kpe_ref.dtype)

    window = lambda c0, width: pl.BlockSpec((tr, width), lambda i: (i, c0 // width))
    gain = lambda width: pl.BlockSpec((1, width), lambda i: (0, 0))
    tab = pl.BlockSpec((tr, LANES), lambda i: (i, 0))
    out = lambda width: pl.BlockSpec((tr, width), lambda i: (i, 0))
    return pl.pallas_call(
        body,
        grid=(rows // tr,),
        in_specs=[window(Z_QLAT, Q_RANK), window(Z_KVLAT, KV_RANK), window(Z_KPE, LANES), gain(Q_RANK), gain(KV_RANK), tab, tab, tab],
        out_specs=[out(Q_RANK), out(KV_RANK), out(LANES)],
        out_shape=[jax.ShapeDtypeStruct((rows, Q_RANK), BF16), jax.ShapeDtypeStruct((rows, KV_RANK), BF16),
                   jax.ShapeDtypeStruct((rows, LANES), BF16)],
        compiler_params=_params("parallel"),
        name=name,
    )(z, z, z, g_q, g_kv, tab_c, tab_a, tab_b)


def _latent_prep_bwd(z, g_q, g_kv, dqn, dkvn, dkb, tab_c, tab_a, tab_b, heads, name):
    rows = z.shape[0]
    tr = _tile(rows, 256, 16)

    def norm_bwd(x_ref, g_ref, dy_ref, dx_ref, dg_ref, i):
        xv = x_ref[...].astype(F32)
        dyv = dy_ref[...].astype(F32)
        r = lax.rsqrt(jnp.mean(xv * xv, axis=-1, keepdims=True) + EPS)
        xr = xv * r
        dyg = dyv * g_ref[...]
        c = jnp.mean(dyg * xr, axis=-1, keepdims=True)
        dx_ref[...] = (r * (dyg - xr * c)).astype(dx_ref.dtype)
        part = jnp.sum(dyv * xr, axis=0, keepdims=True)

        @pl.when(i == 0)
        def _():
            dg_ref[...] = part

        @pl.when(i > 0)
        def _():
            dg_ref[...] += part

    def body(q_ref, kv_ref, gq_ref, gkv_ref, dqn_ref, dkvn_ref, d_ref, c_ref, a_ref, b_ref,
             dq_ref, dkv_ref, dkpe_ref, dgq_ref, dgkv_ref):
        i = pl.program_id(0)
        norm_bwd(q_ref, gq_ref, dqn_ref, dq_ref, dgq_ref, i)
        norm_bwd(kv_ref, gkv_ref, dkvn_ref, dkv_ref, dgkv_ref, i)
        acc = d_ref[:, 0:LANES]
        for h in range(1, heads):
            acc = acc + d_ref[:, h * LANES:(h + 1) * LANES]
        dkpe_ref[...] = _rope_rows(acc, c_ref[...], a_ref[...], b_ref[...], -1).astype(dkpe_ref.dtype)

    window = lambda c0, width: pl.BlockSpec((tr, width), lambda i: (i, c0 // width))
    gain = lambda width: pl.BlockSpec((1, width), lambda i: (0, 0))
    rows_of = lambda width: pl.BlockSpec((tr, width), lambda i: (i, 0))
    return pl.pallas_call(
        body,
        grid=(rows // tr,),
        in_specs=[window(Z_QLAT, Q_RANK), window(Z_KVLAT, KV_RANK), gain(Q_RANK), gain(KV_RANK), rows_of(Q_RANK), rows_of(KV_RANK),
                  rows_of(heads * LANES), rows_of(LANES), rows_of(LANES), rows_of(LANES)],
        out_specs=[rows_of(Q_RANK), rows_of(KV_RANK), rows_of(LANES), gain(Q_RANK), gain(KV_RANK)],
        out_shape=[jax.ShapeDtypeStruct((rows, Q_RANK), BF16), jax.ShapeDtypeStruct((rows, KV_RANK), BF16),
                   jax.ShapeDtypeStruct((rows, LANES), BF16), jax.ShapeDtypeStruct((1, Q_RANK), F32),
                   jax.ShapeDtypeStruct((1, KV_RANK), F32)],
        compiler_params=_params("arbitrary"),
        name=name,
    )(z, z, g_q, g_kv, dqn, dkvn, dkb, tab_c, tab_a, tab_b)


class _CommPlan:
    def __init__(self, ins, out_shape, build, n_copies):
        self.ins, self.out_shape, self.build, self.n_copies = list(ins), list(out_shape), build, n_copies

    def scratch(self):
        n = self.n_copies
        return [pltpu.SemaphoreType.DMA((n,)), pltpu.SemaphoreType.DMA((n,)), pltpu.SemaphoreType.DMA((n,))]


def _rope_rows(x, c, a, b, sign):
    width = x.shape[-1]
    mixed = pltpu.roll(x, width - HALF_ROPE, 1) * a + pltpu.roll(x, HALF_ROPE, 1) * b
    return x * c + mixed if sign > 0 else x * c - mixed


def _attn_fwd(q, ka, kb, v, rope, heads, q_w, q_cb, ka_cb, ka_step, v_cb, v_step, scale, tq_cap, name, tk_cap=512, o_into=None):
    s_q, s_k = q.shape[0], ka.shape[0]
    tq = _tile(s_q, tq_cap, 16)
    nq = s_q // tq
    has_kb = kb is not None
    n_in = 7 if has_kb else 3
    tk = _tile(s_k, tk_cap, LANES)
    o_cols, o_cb, o_old = o_into if o_into is not None else (heads * LANES, 0, None)

    def body(*refs):
        if o_old is not None:
            refs = refs[:n_in] + refs[n_in + 1:]
        if has_kb:
            q_ref, ka_ref, kb_ref, v_ref, c_ref, a_ref, b_ref, o_ref, lse_ref, k_scr = refs

            @pl.when(pl.program_id(1) == 0)
            def _():
                k_scr[:, 0:LANES] = ka_ref[...].astype(BF16)
                k_scr[:, LANES:2 * LANES] = kb_ref[...].astype(BF16)

            keys = k_scr
            qv = _rope_rows(q_ref[...].astype(F32), c_ref[...], a_ref[...], b_ref[...], 1).astype(BF16)
        else:
            q_ref, ka_ref, v_ref, o_ref, lse_ref = refs
            keys = ka_ref
            qv = q_ref[...].astype(BF16)
        c2 = scale * LOG2_E
        m = l = o = None
        nk = s_k // tk
        scores = lambda j: lax.dot_general(qv, keys[j * tk:(j + 1) * tk, :].astype(BF16), (((1,), (1,)), ((), ())),
                                           preferred_element_type=F32)
        s_next = scores(0)
        for j in range(nk):
            sj = s_next
            if j + 1 < nk:
                s_next = scores(j + 1)
            mj = jnp.max(sj, axis=-1, keepdims=True)
            m_new = mj if m is None else jnp.maximum(m, mj)
            pj = jnp.exp2((sj - m_new) * c2)
            lj = jnp.sum(pj, axis=-1, keepdims=True)
            oj = jnp.dot(pj.astype(BF16), v_ref[j * tk:(j + 1) * tk, :].astype(BF16), preferred_element_type=F32)
            if m is None:
                l, o = lj, oj
            else:
                alpha = jnp.exp2((m - m_new) * c2)
                l, o = l * alpha + lj, o * alpha + oj
            m = m_new
        o_ref[...] = (o * (1.0 / l)).astype(o_ref.dtype)
        lse_ref[...] = jnp.broadcast_to(m * c2 + jnp.log2(l), lse_ref.shape)

    in_specs = [pl.BlockSpec((tq, q_w), lambda h, i: (i, q_cb + h)),
                pl.BlockSpec((s_k, LANES), lambda h, i: (0, ka_cb + ka_step * h))]
    args = [q, ka]
    if has_kb:
        in_specs.append(pl.BlockSpec((s_k, LANES), lambda h, i: (0, 0)))
        args.append(kb)
    in_specs.append(pl.BlockSpec((s_k, LANES), lambda h, i: (0, v_cb + v_step * h)))
    args.append(v)
    if has_kb:
        in_specs += [pl.BlockSpec((tq, q_w), lambda h, i: (i, 0))] * 3
        args += list(rope)
    aliases = {}
    if o_old is not None:
        aliases = {len(args): 0}
        in_specs.append(ANY)
        args.append(o_old)
    out_specs = [pl.BlockSpec((tq, LANES), lambda h, i: (i, o_cb + h)), pl.BlockSpec((tq, LANES), lambda h, i: (i, h))]
    out_shape = [jax.ShapeDtypeStruct((s_q, o_cols), BF16), jax.ShapeDtypeStruct((s_q, heads * LANES), F32)]
    scratch = [pltpu.VMEM((s_k, 2 * LANES), BF16)] if has_kb else []
    return pl.pallas_call(
        body,
        grid=(heads, nq),
        in_specs=in_specs,
        out_specs=out_specs,
        out_shape=out_shape,
        scratch_shapes=scratch,
        input_output_aliases=aliases,
        compiler_params=_params("arbitrary", "arbitrary"),
        name=name,
    )(*args)


def _attn_bwd(q, ka, kb, v, o, do, lse, rope, heads, q_w, q_cb, ka_cb, ka_step, v_cb, v_step, o_cb, scale, tq_cap, name,
              tk_cap=512):
    s_q, s_k = q.shape[0], ka.shape[0]
    tq = _tile(s_q, tq_cap, 16)
    nq = s_q // tq
    has_kb = kb is not None
    tk = _tile(s_k, tk_cap, LANES)

    def body(*refs):
        if has_kb:
            (q_ref, ka_ref, kb_ref, v_ref, o_ref, do_ref, lse_ref, c_ref, a_ref, b_ref, dq_ref, dkv_ref, dkb_ref, k_scr, dk_acc,
             dv_acc) = refs
        else:
            q_ref, ka_ref, v_ref, o_ref, do_ref, lse_ref, dq_ref, dka_ref, dv_ref, dk_acc, dv_acc = refs
        i = pl.program_id(1)

        @pl.when(i == 0)
        def _():
            dk_acc[...] = jnp.zeros_like(dk_acc)
            dv_acc[...] = jnp.zeros_like(dv_acc)
            if has_kb:
                k_scr[:, 0:LANES] = ka_ref[...].astype(BF16)
                k_scr[:, LANES:2 * LANES] = kb_ref[...].astype(BF16)

        keys = k_scr if has_kb else ka_ref
        if has_kb:
            qv = _rope_rows(q_ref[...].astype(F32), c_ref[...], a_ref[...], b_ref[...], 1).astype(BF16)
        else:
            qv = q_ref[...].astype(BF16)
        dov = do_ref[...].astype(BF16)
        delta = jnp.sum(dov.astype(F32) * o_ref[...].astype(F32), axis=-1, keepdims=True)
        lse2 = lse_ref[:, 0:1]
        c2 = scale * LOG2_E
        nk = s_k // tk
        rows = lambda j: slice(j * tk, (j + 1) * tk)
        nt = (((1,), (1,)), ((), ()))
        tn = (((0,), (0,)), ((), ()))

        def scores(j):
            return (lax.dot_general(qv, keys[rows(j), :].astype(BF16), nt, preferred_element_type=F32),
                    lax.dot_general(dov, v_ref[rows(j), :].astype(BF16), nt, preferred_element_type=F32))

        nxt = scores(0)
        dq = None
        for j in range(nk):
            sj, dpj = nxt
            if j + 1 < nk:
                nxt = scores(j + 1)
            pj = jnp.exp2(sj * c2 - lse2)
            dsj = (pj * (dpj - delta)).astype(BF16)
            dqj = jnp.dot(dsj, keys[rows(j), :].astype(BF16), preferred_element_type=F32)
            dq = dqj if dq is None else dq + dqj
            dk_acc[rows(j), :] += lax.dot_general(dsj, qv, tn, preferred_element_type=F32)
            dv_acc[rows(j), :] += lax.dot_general(pj.astype(BF16), dov, tn, preferred_element_type=F32)
        dq = dq * scale
        if has_kb:
            dq = _rope_rows(dq, c_ref[...], a_ref[...], b_ref[...], -1)
        dq_ref[...] = dq.astype(dq_ref.dtype)

        @pl.when(i == nq - 1)
        def _():
            if has_kb:
                dkv_ref[:, 0:LANES] = (dk_acc[:, 0:LANES] * scale).astype(dkv_ref.dtype)
                dkv_ref[:, LANES:2 * LANES] = dv_acc[...].astype(dkv_ref.dtype)
                dkb_ref[...] = dk_acc[:, LANES:2 * LANES] * scale
            else:
                dka_ref[...] = (dk_acc[...] * scale).astype(dka_ref.dtype)
                dv_ref[...] = dv_acc[...].astype(dv_ref.dtype)

    key_spec = lambda cb, step: pl.BlockSpec((s_k, LANES), lambda h, i: (0, cb + step * h))
    row_spec = lambda cb: pl.BlockSpec((tq, LANES), lambda h, i: (i, cb + h))
    in_specs = [pl.BlockSpec((tq, q_w), lambda h, i: (i, q_cb + h)), key_spec(ka_cb, ka_step)]
    args = [q, ka]
    if has_kb:
        in_specs.append(pl.BlockSpec((s_k, LANES), lambda h, i: (0, 0)))
        args.append(kb)
    in_specs += [key_spec(v_cb, v_step), row_spec(o_cb), row_spec(o_cb), row_spec(0)]
    args += [v, o, do, lse]
    if has_kb:
        in_specs += [pl.BlockSpec((tq, q_w), lambda h, i: (i, 0))] * 3
        args += list(rope)
    out_specs = [pl.BlockSpec((tq, q_w), lambda h, i: (i, h))]
    out_shape = [jax.ShapeDtypeStruct((s_q, heads * q_w), BF16)]
    scratch = []
    if has_kb:
        out_specs += [pl.BlockSpec((s_k, 2 * LANES), lambda h, i: (0, h)), key_spec(0, 1)]
        out_shape += [jax.ShapeDtypeStruct((s_k, heads * 2 * LANES), BF16), jax.ShapeDtypeStruct((s_k, heads * LANES), F32)]
        scratch.append(pltpu.VMEM((s_k, 2 * LANES), BF16))
    else:
        out_specs += [key_spec(0, 1), key_spec(0, 1)]
        out_shape += [jax.ShapeDtypeStruct((s_k, heads * LANES), BF16)] * 2
    scratch += [pltpu.VMEM((s_k, q_w), F32), pltpu.VMEM((s_k, LANES), F32)]
    return pl.pallas_call(
        body,
        grid=(heads, nq),
        in_specs=in_specs,
        out_specs=out_specs,
        out_shape=out_shape,
        scratch_shapes=scratch,
        compiler_params=_params("arbitrary", "arbitrary"),
        name=name,
    )(*args)


def _shift_rows(u, rows):
    t = lax.broadcasted_iota(jnp.int32, u.shape, 0)
    prev = jnp.where(t == 0, 0.0, pltpu.roll(u, 1, 0))
    nxt = jnp.where(t == rows - 1, 0.0, pltpu.roll(u, rows - 1, 0))
    return prev, nxt


def _conv_fwd(z, conv_w, cat, name):
    rows = z.shape[0]
    nblk = CONV_W // LANES

    def body(gb_ref, gc_ref, xin_ref, w_ref, cat_ref, o_ref):
        del cat_ref
        u = gc_ref[...].astype(F32) * xin_ref[...].astype(F32)
        prev, nxt = _shift_rows(u, rows)
        conv = prev * w_ref[0:1, :] + u * w_ref[1:2, :] + nxt * w_ref[2:3, :]
        o_ref[...] = (gb_ref[...].astype(F32) * conv).astype(o_ref.dtype)

    col = lambda c0: pl.BlockSpec((rows, LANES), lambda j: (0, c0 // LANES + j))
    return pl.pallas_call(
        body,
        grid=(nblk,),
        in_specs=[col(Z_GB), col(Z_GC), col(Z_XIN), pl.BlockSpec((3, LANES), lambda j: (0, j)), ANY],
        out_specs=col(MLA_W),
        out_shape=jax.ShapeDtypeStruct(cat.shape, cat.dtype),
        input_output_aliases={4: 0},
        compiler_params=_params("parallel"),
        name=name,
    )(z, z, z, conv_w, cat)


def _conv_bwd(z, conv_w, dcat, name):
    rows = z.shape[0]
    nblk = CONV_W // LANES

    def body(gb_ref, gc_ref, xin_ref, w_ref, dc_ref, dgb_ref, dgc_ref, dxin_ref, dw_ref):
        gc = gc_ref[...].astype(F32)
        xin = xin_ref[...].astype(F32)
        dc = dc_ref[...].astype(F32)
        u = gc * xin
        prev, nxt = _shift_rows(u, rows)
        w0, w1, w2 = w_ref[0:1, :], w_ref[1:2, :], w_ref[2:3, :]
        conv = prev * w0 + u * w1 + nxt * w2
        dgb_ref[...] = (dc * conv).astype(dgb_ref.dtype)
        dconv = dc * gb_ref[...].astype(F32)
        dw_ref[0:1, :] = jnp.sum(dconv * prev, axis=0, keepdims=True)
        dw_ref[1:2, :] = jnp.sum(dconv * u, axis=0, keepdims=True)
        dw_ref[2:3, :] = jnp.sum(dconv * nxt, axis=0, keepdims=True)
        dprev, dnxt = _shift_rows(dconv, rows)
        du = dnxt * w0 + dconv * w1 + dprev * w2
        dgc_ref[...] = (du * xin).astype(dgc_ref.dtype)
        dxin_ref[...] = (du * gc).astype(dxin_ref.dtype)

    col = lambda c0: pl.BlockSpec((rows, LANES), lambda j: (0, c0 // LANES + j))
    w_spec = pl.BlockSpec((3, LANES), lambda j: (0, j))
    piece = jax.ShapeDtypeStruct((rows, CONV_W), BF16)
    return pl.pallas_call(
        body,
        grid=(nblk,),
        in_specs=[col(Z_GB), col(Z_GC), col(Z_XIN), w_spec, col(MLA_W)],
        out_specs=[col(0), col(0), col(0), w_spec],
        out_shape=[piece, piece, piece, jax.ShapeDtypeStruct((3, CONV_W), F32)],
        compiler_params=_params("parallel"),
        name=name,
    )(z, z, z, conv_w, dcat)


def _gate_fwd(cat, z, name):
    rows = cat.shape[0]
    tr = _tile(rows, 256, 16)
    tc = MIX_W
    g0 = Z_GATE // tc

    def body(c_ref, g_ref, y_ref):
        g = g_ref[...].astype(F32)
        y_ref[...] = (c_ref[...].astype(F32) * (g * jax.nn.sigmoid(g))).astype(y_ref.dtype)

    blk = pl.BlockSpec((tr, tc), lambda i, j: (i, j))
    return pl.pallas_call(
        body,
        grid=(rows // tr, MIX_W // tc),
        in_specs=[blk, pl.BlockSpec((tr, tc), lambda i, j: (i, g0 + j))],
        out_specs=blk,
        out_shape=jax.ShapeDtypeStruct((rows, MIX_W), BF16),
        compiler_params=_params("parallel", "parallel"),
        name=name,
    )(cat, z)


def _out_proj_dx_gate_bwd(do, w_o, cat, z, name):
    rows, k = do.shape
    tm = _tile(rows, 1024, 16)
    tn = _tile(MIX_W, 1024, LANES)
    g0 = Z_GATE // tn

    def body(do_ref, w_ref, c_ref, g_ref, dcat_ref, dgate_ref):
        dy = lax.dot_general(do_ref[...], w_ref[...], (((1,), (1,)), ((), ())), preferred_element_type=F32)
        g = g_ref[...].astype(F32)
        sg = jax.nn.sigmoid(g)
        dcat_ref[...] = (dy * (g * sg)).astype(dcat_ref.dtype)
        dgate_ref[...] = (dy * c_ref[...].astype(F32) * (sg * (1.0 + g * (1.0 - sg)))).astype(dgate_ref.dtype)

    blk = pl.BlockSpec((tm, tn), lambda i, j: (i, j))
    out = jax.ShapeDtypeStruct((rows, MIX_W), BF16)
    return pl.pallas_call(
        body,
        grid=(rows // tm, MIX_W // tn),
        in_specs=[pl.BlockSpec((tm, k), lambda i, j: (i, 0)), pl.BlockSpec((tn, k), lambda i, j: (j, 0)), blk,
                  pl.BlockSpec((tm, tn), lambda i, j: (i, g0 + j))],
        out_specs=[blk, blk],
        out_shape=[out, out],
        compiler_params=_params("parallel", "parallel"),
        name=name,
    )(do, w_o, cat, z)


def _loss_head(y, target, name):
    rows, width = y.shape
    tr = _tile(rows, 256, 8)

    def body(y_ref, t_ref, g_ref, loss_ref):
        i = pl.program_id(0)
        d = y_ref[...] - t_ref[...]
        g_ref[...] = d / width
        part = 0.5 * jnp.sum(jnp.mean(d * d, axis=-1, keepdims=True), axis=0, keepdims=True)
        part = jnp.broadcast_to(part, loss_ref.shape)

        @pl.when(i == 0)
        def _():
            loss_ref[...] = part

        @pl.when(i > 0)
        def _():
            loss_ref[...] += part

    row_spec = pl.BlockSpec((tr, width), lambda i: (i, 0))
    return pl.pallas_call(
        body,
        grid=(rows // tr,),
        in_specs=[row_spec, row_spec],
        out_specs=[row_spec, pl.BlockSpec((1, LANES), lambda i: (0, 0))],
        out_shape=[jax.ShapeDtypeStruct((rows, width), F32), jax.ShapeDtypeStruct((1, LANES), F32)],
        compiler_params=_params("arbitrary"),
        name=name,
    )(y, target)


CHIP_FLIPS = ((1, 0), (0, 1), (1, 1))
ANY = pl.BlockSpec(memory_space=pl.ANY)


def _chip_copies(pieces, sems, n_slot):
    send_sems, recv_sems, local_sems = sems
    x, y, c = lax.axis_index("x"), lax.axis_index("y"), lax.axis_index("c")
    me = 2 * x + y

    def remote(j, k, a, src, dst):
        fx, fy = CHIP_FLIPS[k]
        return pltpu.make_async_remote_copy(
            src_ref=src, dst_ref=dst, send_sem=send_sems.at[n_slot * k + a], recv_sem=recv_sems.at[n_slot * k + a],
            device_id=((j // 2) ^ fx, (j % 2) ^ fy, c), device_id_type=MESH_ID)

    def peer(j, k):
        fx, fy = CHIP_FLIPS[k]
        return 2 * ((j // 2) ^ fx) + ((j % 2) ^ fy)

    def start_as(j):
        def run():
            for a, (src, dst) in enumerate(pieces(j, j)):
                pltpu.make_async_copy(src, dst, local_sems.at[a]).start()
            for k in range(len(CHIP_FLIPS)):
                for a, (src, dst) in enumerate(pieces(j, peer(j, k))):
                    remote(j, k, a, src, dst).start()
        return run

    def wait_as(j):
        def run():
            for a, (src, dst) in enumerate(pieces(j, j)):
                pltpu.make_async_copy(src, dst, local_sems.at[a]).wait()
            for k in range(len(CHIP_FLIPS)):
                for a, (src, dst) in enumerate(pieces(j, peer(j, k))):
                    remote(j, k, a, src, dst).wait_send()
                for a, (src, dst) in enumerate(pieces(peer(j, k), j)):
                    remote(j, k, a, src, dst).wait_recv()
        return run

    def start():
        for j in range(N_CHIPS):
            pl.when(me == j)(start_as(j))

    def wait():
        for j in range(N_CHIPS):
            pl.when(me == j)(wait_as(j))

    return start, wait


IN_PIECES = ((0, Q_RANK, Z_QLAT), (Q_RANK, KV_RANK, Z_KVLAT), (Q_RANK + KV_RANK, ROPE, Z_KPE),
             (Q_RANK + KV_RANK + ROPE, CONV_W, Z_GB), (Q_RANK + KV_RANK + ROPE + CONV_W, CONV_W, Z_GC),
             (Q_RANK + KV_RANK + ROPE + 2 * CONV_W, CONV_W, Z_XIN), (Q_RANK + KV_RANK + ROPE + 3 * CONV_W, MEM_W, Z_QMEM),
             (Q_RANK + KV_RANK + ROPE + 3 * CONV_W + MEM_W, MIX_W, Z_GATE))
IN_SHARD = IN_COLS // N_CHIPS


def _in_segments(j):
    lo, hi = j * IN_SHARD, (j + 1) * IN_SHARD
    segs = []
    for r0, width, z0 in IN_PIECES:
        a, b = max(lo, r0), min(hi, r0 + width)
        if a < b:
            segs.append((a - lo, z0 + a - r0, b - a))
    return segs


N_SLOT = 11


def _gather_plan(l, shards, zero_rows, part="all"):
    s_in, s_uq, s_ukv, s_conv, s_mk, s_mv, s_o = shards
    ukv_c, mk_r, mk_c, o_r = s_ukv.shape[2], s_mk.shape[1], s_mk.shape[2], s_o.shape[1]
    stack = lambda s: jax.ShapeDtypeStruct((N_CHIPS,) + s.shape[1:], s.dtype)
    in_ins, in_outs = [s_in, zero_rows], [jax.ShapeDtypeStruct((Z_COLS, s_in.shape[2]), s_in.dtype)]
    early_ins = [s_uq, s_ukv, s_conv]
    early_outs = [stack(s_uq), jax.ShapeDtypeStruct((s_ukv.shape[1], N_CHIPS * ukv_c), s_ukv.dtype), stack(s_conv)]
    late_ins = [s_mk, s_mv, s_o]
    late_outs = [jax.ShapeDtypeStruct((N_CHIPS * mk_r, 2 * mk_c), s_mk.dtype),
                 jax.ShapeDtypeStruct((N_CHIPS * o_r, s_o.shape[2]), s_o.dtype)]
    with_in, with_early, with_late = part in ("all", "in"), part in ("all", "early"), part in ("all", "late")

    def build(ins, outs, sems):
        ins, outs = list(ins), list(outs)
        if with_in:
            r_in, r_zero, f_in = ins.pop(0), ins.pop(0), outs.pop(0)
        if with_early:
            r_uq, r_ukv, r_conv = ins.pop(0), ins.pop(0), ins.pop(0)
            g_uq, f_ukv, g_conv = outs.pop(0), outs.pop(0), outs.pop(0)
        if with_late:
            r_mk, r_mv, r_o = ins
            f_mkv, f_o = outs

        def pieces(j, t):
            out = []
            if with_in:
                out += [(r_in.at[l, pl.ds(so, n), :], f_in.at[pl.ds(zo, n), :]) for so, zo, n in _in_segments(j)]
            if with_early:
                out += [(r_uq.at[l], g_uq.at[j]), (r_ukv.at[l], f_ukv.at[:, pl.ds(j * ukv_c, ukv_c)]),
                        (r_conv.at[l], g_conv.at[j])]
            if with_late:
                out += [(r_mk.at[l], f_mkv.at[pl.ds(j * mk_r, mk_r), pl.ds(0, mk_c)]),
                        (r_mv.at[l], f_mkv.at[pl.ds(j * mk_r, mk_r), pl.ds(mk_c, mk_c)]),
                        (r_o.at[l], f_o.at[pl.ds(j * o_r, o_r), :])]
            if with_in and j == t:
                out.append((r_zero, f_in.at[pl.ds(Z_KPE + ROPE, LANES - ROPE), :]))
            return out

        return _chip_copies(pieces, sems, N_SLOT)

    ins = (in_ins if with_in else []) + (early_ins if with_early else []) + (late_ins if with_late else [])
    outs = (in_outs if with_in else []) + (early_outs if with_early else []) + (late_outs if with_late else [])
    return _CommPlan(ins, outs, build, len(CHIP_FLIPS) * N_SLOT)


def _scatter_plan(dwt_in, c_uq, dw_ukv, c_conv, dw_mkv, dw_o, part="all"):
    ukv_c, mk_r, mk_c, o_r = dw_ukv.shape[1] // N_CHIPS, dw_mkv.shape[0] // N_CHIPS, dw_mkv.shape[1] // 2, dw_o.shape[0] // N_CHIPS
    with_in, with_rest = part != "rest", part != "in"
    in_outs = [jax.ShapeDtypeStruct((N_CHIPS, IN_SHARD, D_MODEL), BF16)]
    rest_ins = [c_uq, dw_ukv, c_conv, dw_mkv, dw_o]
    rest_outs = [jax.ShapeDtypeStruct(c_uq.shape, c_uq.dtype),
                 jax.ShapeDtypeStruct((N_CHIPS, dw_ukv.shape[0], ukv_c), dw_ukv.dtype),
                 jax.ShapeDtypeStruct(c_conv.shape, c_conv.dtype),
                 jax.ShapeDtypeStruct((N_CHIPS, mk_r, mk_c), dw_mkv.dtype), jax.ShapeDtypeStruct((N_CHIPS, mk_r, mk_c), dw_mkv.dtype),
                 jax.ShapeDtypeStruct((N_CHIPS, o_r, dw_o.shape[1]), dw_o.dtype)]

    def build(ins, outs, sems):
        ins, outs = list(ins), list(outs)
        if with_in:
            r_in, o_in = ins.pop(0), outs.pop(0)
        if with_rest:
            r_uq, r_ukv, r_conv, r_mkv, r_o = ins
            o_uq, o_ukv, o_conv, o_mk, o_mv, o_o = outs

        def pieces(j, t):
            out = []
            if with_in:
                out += [(r_in.at[pl.ds(zo, n), :], o_in.at[j, pl.ds(so, n), :]) for so, zo, n in _in_segments(t)]
            if with_rest:
                out += [(r_uq.at[t], o_uq.at[j]), (r_ukv.at[:, pl.ds(t * ukv_c, ukv_c)], o_ukv.at[j]),
                        (r_conv.at[t], o_conv.at[j]),
                        (r_mkv.at[pl.ds(t * mk_r, mk_r), pl.ds(0, mk_c)], o_mk.at[j]),
                        (r_mkv.at[pl.ds(t * mk_r, mk_r), pl.ds(mk_c, mk_c)], o_mv.at[j]),
                        (r_o.at[pl.ds(t * o_r, o_r), :], o_o.at[j])]
            return out

        return _chip_copies(pieces, sems, N_SLOT)

    ins = ([dwt_in] if with_in else []) + (rest_ins if with_rest else [])
    outs = (in_outs if with_in else []) + (rest_outs if with_rest else [])
    return _CommPlan(ins, outs, build, len(CHIP_FLIPS) * N_SLOT)


HBM = pl.BlockSpec(memory_space=pltpu.HBM)
SEM = pl.BlockSpec(memory_space=pltpu.SEMAPHORE)
SIDE_EFFECT = pltpu.SideEffectType.DATAFLOW_SIDE_EFFECTING


def _comm_start(plan, after, name):
    n_in, n_out, n_after = len(plan.ins), len(plan.out_shape), len(after)
    n_buf = n_in + n_out

    def body(*refs):
        bufs, sems, token = refs[:n_buf], refs[n_buf + n_after:n_buf + n_after + 3], refs[-1]
        start, _ = plan.build(bufs[:n_in], bufs[n_in:], sems)
        start()
        token[...] = jnp.zeros_like(token)

    lands = [lax.empty(s.shape, s.dtype) for s in plan.out_shape]
    args = [pltpu.with_memory_space_constraint(a, pltpu.HBM) for a in list(plan.ins) + lands]
    res = pl.pallas_call(
        body,
        in_specs=[HBM] * n_buf + [ANY] * n_after,
        out_specs=[SEM] * 3 + [HBM] * n_out + [pl.BlockSpec(memory_space=pltpu.VMEM)],
        out_shape=plan.scratch() + [pltpu.HBM(a.shape, a.dtype) for a in lands] + [jax.ShapeDtypeStruct((8, LANES), F32)],
        input_output_aliases={n_in + i: 3 + i for i in range(n_out)},
        compiler_params=pltpu.CompilerParams(has_side_effects=SIDE_EFFECT),
        name=name,
    )(*args, *after)
    return list(res[:3]), list(res[3:3 + n_out]), res[-1]


def _comm_finish(plan, started, after, name):
    sems, lands, _ = started
    n_in, n_out = len(plan.ins), len(plan.out_shape)
    n_buf = n_in + n_out

    def body(*refs):
        bufs_in, sem_refs = refs[:n_buf], refs[n_buf:n_buf + 3]
        _, wait = plan.build(bufs_in[:n_in], bufs_in[n_in:], sem_refs)
        wait()

    sources = [pltpu.with_memory_space_constraint(a, pltpu.HBM) for a in plan.ins]
    res = pl.pallas_call(
        body,
        in_specs=[HBM] * n_buf + [SEM] * 3 + [ANY] * len(after),
        out_specs=[HBM] * n_out,
        out_shape=[pltpu.HBM(b.shape, b.dtype) for b in lands],
        input_output_aliases={n_in + i: i for i in range(n_out)},
        compiler_params=pltpu.CompilerParams(has_side_effects=SIDE_EFFECT),
        name=name,
    )(*sources, *lands, *sems, *after)
    return list(res)


def _sibling_plan(arrays):
    def build(ins, outs, sems):
        send_sems, recv_sems, _ = sems
        sibling = (lax.axis_index("x"), lax.axis_index("y"), 1 - lax.axis_index("c"))
        copies = [pltpu.make_async_remote_copy(src_ref=src, dst_ref=dst, send_sem=send_sems.at[a], recv_sem=recv_sems.at[a],
                                               device_id=sibling, device_id_type=MESH_ID)
                  for a, (src, dst) in enumerate(zip(ins, outs))]

        def start():
            for cp in copies:
                cp.start()

        def wait():
            for cp in copies:
                cp.wait()

        return start, wait

    return _CommPlan(arrays, [jax.ShapeDtypeStruct(v.shape, v.dtype) for v in arrays], build, len(arrays))


DEVICE_FLIPS = tuple((fx, fy, fc) for fx in (0, 1) for fy in (0, 1) for fc in (0, 1))[1:]


def _gather_all(v, after, name):
    def body(v_ref, after_ref, out_ref, send_sems, recv_sems, local_sem):
        del after_ref
        x, y, c = lax.axis_index("x"), lax.axis_index("y"), lax.axis_index("c")
        me = 4 * x + 2 * y + c
        local = pltpu.make_async_copy(v_ref, out_ref.at[me], local_sem)
        local.start()
        copies = [local]
        for k, (fx, fy, fc) in enumerate(DEVICE_FLIPS):
            cp = pltpu.make_async_remote_copy(
                src_ref=v_ref, dst_ref=out_ref.at[me], send_sem=send_sems.at[k], recv_sem=recv_sems.at[k],
                device_id=((x + fx) % 2, (y + fy) % 2, (c + fc) % 2), device_id_type=MESH_ID)
            cp.start()
            copies.append(cp)
        for cp in copies:
            cp.wait()

    return pl.pallas_call(
        body,
        in_specs=[ANY, ANY],
        out_specs=ANY,
        out_shape=jax.ShapeDtypeStruct((N_DEV,) + v.shape, v.dtype),
        scratch_shapes=[pltpu.SemaphoreType.DMA((N_DEV - 1,)), pltpu.SemaphoreType.DMA((N_DEV - 1,)), pltpu.SemaphoreType.DMA],
        name=name,
    )(v, after)


def _sum_slots(parts, name):
    n, rows, cols = parts.shape
    tr = _tile(rows, 256, 16)

    def body(p_ref, o_ref):
        acc = p_ref[0].astype(F32)
        for k in range(1, n):
            acc = acc + p_ref[k].astype(F32)
        o_ref[...] = acc

    return pl.pallas_call(
        body,
        grid=(rows // tr,),
        in_specs=[pl.BlockSpec((n, tr, cols), lambda i: (0, i, 0))],
        out_specs=pl.BlockSpec((tr, cols), lambda i: (i, 0)),
        out_shape=jax.ShapeDtypeStruct((rows, cols), F32),
        compiler_params=_params("parallel"),
        name=name,
    )(parts)


def _adamw_math(w, g, m, v):
    m_new = ADAM_B1 * m + (1.0 - ADAM_B1) * g
    v_new = ADAM_B2 * v + (1.0 - ADAM_B2) * jnp.square(g)
    m_hat = m_new / (1.0 - ADAM_B1 ** ADAM_STEP)
    v_hat = v_new / (1.0 - ADAM_B2 ** ADAM_STEP)
    return -ADAM_LR * (m_hat / (jnp.sqrt(v_hat) + ADAM_EPS) + ADAM_WD * w), m_new, v_new


def _adamw(w, g, m, v, name):
    rows, cols = w.shape
    tr = _tile(rows, 256, 8)

    def body(w_ref, g_ref, m_ref, v_ref, d_out, m_out, v_out):
        d_out[...], m_out[...], v_out[...] = _adamw_math(w_ref[...], g_ref[...], m_ref[...], v_ref[...])

    blk = pl.BlockSpec((tr, cols), lambda i: (i, 0))
    out = jax.ShapeDtypeStruct((rows, cols), F32)
    return pl.pallas_call(
        body,
        grid=(rows // tr,),
        in_specs=[blk] * 4,
        out_specs=[blk] * 3,
        out_shape=[out] * 3,
        compiler_params=_params("parallel"),
        name=name,
    )(w, g, m, v)


def _adamw_layer(l, w, g_a, g_b, m, v, prev, name):
    depth, rows, cols = w.shape
    tr = _tile(rows, 256, 8)

    def body(w_ref, ga_ref, gb_ref, m_ref, v_ref, *rest):
        g_out, d_out, m_out, v_out = rest[-4:]
        g = ga_ref[...] + gb_ref[...]
        g_out[...] = g
        d_out[...], m_out[...], v_out[...] = _adamw_math(w_ref[...], g, m_ref[...], v_ref[...])

    stacked = pl.BlockSpec((None, tr, cols), lambda i: (l, i, 0))
    flat = pl.BlockSpec((tr, cols), lambda i: (i, 0))
    in_specs = [stacked, flat, flat, stacked, stacked]
    args = [w, g_a, g_b, m, v]
    aliases = {}
    if prev is not None:
        in_specs += [ANY] * 4
        args += list(prev)
        aliases = {5 + k: k for k in range(4)}
    out = jax.ShapeDtypeStruct((depth, rows, cols), F32)
    return pl.pallas_call(
        body,
        grid=(rows // tr,),
        in_specs=in_specs,
        out_specs=[stacked] * 4,
        out_shape=[out] * 4,
        input_output_aliases=aliases,
        compiler_params=_params("parallel"),
        name=name,
    )(*args)


def _cols_from_shards(g):
    _, r, c = g.shape
    return jnp.transpose(g, (1, 0, 2)).reshape(r, N_CHIPS * c)


def _cols_to_shards(full):
    r, c4 = full.shape
    c = c4 // N_CHIPS
    return jnp.transpose(full.reshape(r, N_CHIPS, c), (1, 0, 2))


def _w_uq_pad(w_uq):
    r, _ = w_uq.shape
    w = jnp.pad(w_uq.reshape(r, MLA_HEADS, QK_HEAD), ((0, 0), (0, 0), (0, QPAD - QK_HEAD)))
    return w.reshape(r, MLA_HEADS * QPAD)


def _w_uq_unpad(w):
    r, _ = w.shape
    return w.reshape(r, MLA_HEADS, QPAD)[..., :QK_HEAD].reshape(r, MLA_HEADS * QK_HEAD)


def _rope_tables(positions):
    inv_freq = 1.0 / (ROPE_THETA ** (jnp.arange(0, ROPE, 2, dtype=F32) / ROPE))
    ang = positions.astype(F32)[:, None] * inv_freq
    cos, sin = jnp.cos(ang), jnp.sin(ang)
    s = positions.shape[0]
    zero = jnp.zeros((s, HALF_ROPE), F32)
    pad = jnp.zeros((s, LANES - ROPE), F32)
    kc = jnp.concatenate([cos, cos, pad], axis=-1)
    ka = jnp.concatenate([-sin, zero, pad], axis=-1)
    kb = jnp.concatenate([zero, sin, pad], axis=-1)
    qc = jnp.concatenate([jnp.ones((s, NOPE), F32), kc], axis=-1)
    qa = jnp.concatenate([jnp.zeros((s, NOPE), F32), ka], axis=-1)
    qb = jnp.concatenate([jnp.zeros((s, NOPE), F32), kb], axis=-1)
    return (qc, qa, qb), (kc, ka, kb)


def _layer_weights(gathered):
    return (gathered[0],) + _late_weights(gathered[1:])


def _late_weights(gathered):
    g_uq, w_ukv, g_conv, w_mkv, w_o = gathered
    return (_w_uq_pad(_cols_from_shards(g_uq)), w_ukv, _cols_from_shards(g_conv), w_mkv, w_o)


def _layer_fwd(l, x, mem, wts, gains, tabs, late=None, h=None, next_g_pre=None):
    wt_in = wts[0]
    g_pre, g_q, g_kv, g_mem, g_post = gains
    q_tab, k_tab = tabs
    tag = f"l{l}_"
    if h is None:
        h = _rmsnorm_fwd(x, g_pre, 0, D_MODEL, tag + "pre_norm")
    z = _matmul(h, wt_in, "nt", BF16, tag + "in_proj", tm_cap=1024, tn_cap=1664)
    w_uq, w_ukv, conv_w, w_mkv, w_o = wts[1:] if late is None else late(z)
    wts = (wt_in, w_uq, w_ukv, conv_w, w_mkv, w_o)
    qn, kvn, kpe = _latent_prep(z, g_q, g_kv, *k_tab, tag + "latent_prep")
    q_raw = _matmul(qn, w_uq, "nn", BF16, tag + "uq", tm_cap=1024)
    kv = _matmul(kvn, w_ukv, "nn", BF16, tag + "ukv")
    cat, a_lse = _attn_fwd(q_raw, kv, kpe, kv, q_tab, MLA_HEADS, QPAD, 0, 0, 2, 1, 2, QK_HEAD ** -0.5, 512,
                           tag + "mla_fwd", tk_cap=1024, o_into=(MIX_W, 0, None))
    cat = _conv_fwd(z, conv_w, cat, tag + "conv_fwd")
    mem_n = _rmsnorm_fwd(mem, g_mem, 0, D_MODEL, tag + "mem_norm")
    mkv = _matmul(mem_n, w_mkv, "nn", BF16, tag + "mem_kv")
    cat, m_lse = _attn_fwd(z, mkv, None, mkv, None, MEM_HEADS, LANES, Z_QMEM // LANES, 0, 1, MEM_HEADS, 1,
                           MEM_HEAD ** -0.5, 4096, tag + "mem_fwd", o_into=(MIX_W, (MLA_W + CONV_W) // LANES, cat))
    y = _gate_fwd(cat, z, tag + "gate_fwd")
    o = _matmul(y, w_o, "nn", BF16, tag + "out_proj", tm_cap=1024)
    x_new, h_next = _post_norm_residual(x, o, g_post, next_g_pre, tag + "post_norm")
    saved = (x, h, z, qn, kvn, q_raw, kv, kpe, a_lse, mem_n, mkv, m_lse, cat, y, o)
    return x_new, saved, h_next


def _layer_bwd(l, g, mem, saved, wts, gains, tabs, split_exchange=False):
    wt_in, w_uq, w_ukv, conv_w, w_mkv, w_o = wts
    g_pre, g_q, g_kv, g_mem, g_post = gains
    q_tab, k_tab = tabs
    x, h, z, qn, kvn, q_raw, kv, kpe, a_lse, mem_n, mkv, m_lse, cat, y, o = saved
    tag = f"l{l}_"
    do, dg_post = _rmsnorm_bwd(o, g_post, g, None, 0, D_MODEL, BF16, tag + "post_norm_bwd")
    dcat, dgate = _out_proj_dx_gate_bwd(do, w_o, cat, z, tag + "out_proj_dx")
    dw_o = _matmul(y, do, "tn", BF16, tag + "out_proj_dw", tm_cap=1024)
    dq, dkv, dkpe_h = _attn_bwd(q_raw, kv, kpe, kv, cat, dcat, a_lse, q_tab, MLA_HEADS, QPAD, 0, 0, 2, 1, 2, 0,
                                QK_HEAD ** -0.5, 512, tag + "mla_bwd")
    dw_ukv = _matmul(kvn, dkv, "tn", BF16, tag + "ukv_dw")
    dkvn = _matmul(dkv, w_ukv, "nt", BF16, tag + "ukv_dx")
    dw_uq = _matmul(qn, dq, "tn", BF16, tag + "uq_dw")
    dqn = _matmul(dq, w_uq, "nt", BF16, tag + "uq_dx")
    dq_lat, dkv_lat, dkpe, dg_q, dg_kv = _latent_prep_bwd(z, g_q, g_kv, dqn, dkvn, dkpe_h, *k_tab, MLA_HEADS,
                                                          tag + "latent_prep_bwd")
    dgb, dgc, dxin, dconv_w = _conv_bwd(z, conv_w, dcat, tag + "conv_bwd")
    dq_mem, dmk, dmv = _attn_bwd(z, mkv, None, mkv, cat, dcat, m_lse, None, MEM_HEADS, LANES, Z_QMEM // LANES, 0, 1,
                                 MEM_HEADS, 1, (MLA_W + CONV_W) // LANES, MEM_HEAD ** -0.5, 2048, tag + "mem_bwd")
    dmkv = jnp.concatenate([dmk, dmv], axis=-1)
    dw_mkv = _matmul(mem_n, dmkv, "tn", BF16, tag + "mem_kv_dw")
    dmem_n = _matmul(dmkv, w_mkv, "nt", F32, tag + "mem_kv_dx")
    _, dg_mem = _rmsnorm_bwd(mem, g_mem, dmem_n, None, 0, D_MODEL, BF16, tag + "mem_norm_bwd")
    others = (_cols_to_shards(_w_uq_unpad(dw_uq)), dw_ukv, _cols_to_shards(dconv_w), dw_mkv, dw_o)
    early = None
    if split_exchange:
        early_plan = _scatter_plan(None, *others, part="rest")
        early = (early_plan, _comm_start(early_plan, [dmem_n], tag + "exchange_rest_start"))
        g_pre = g_pre + early[1][2][0:1, 0:1]
    dz = jnp.concatenate([dgate, dq_lat, dkv_lat, dkpe, dgb, dgc, dxin, dq_mem], axis=-1)
    dwt_in = _matmul(dz, h, "tn", BF16, tag + "in_proj_dw", tm_cap=1664, tk_cap=2048)
    contrib = _scatter_plan(dwt_in, *others, part="in" if split_exchange else "all")
    late = None
    if split_exchange:
        late = (contrib, _comm_start(contrib, [dwt_in], tag + "exchange_in_start"))
    dh = _matmul(dz, wt_in, "nn", BF16, tag + "in_proj_dx", tm_cap=1024, tk_cap=1664, after=late[1][2] if late else None)
    dx, dg_pre = _rmsnorm_bwd(x, g_pre, dh, g, 0, D_MODEL, F32, tag + "pre_norm_bwd")
    return dx, contrib, (dg_pre, dg_q, dg_kv, dg_mem, dg_post), (early, late)


GAIN_WIDTHS = (D_MODEL, Q_RANK, KV_RANK, D_MODEL, D_MODEL)


def _pack_gains(parts):
    return jnp.concatenate([p.reshape(-1) for p in parts]).reshape(-1, LANES)


def _unpack_gains(packed, depth):
    flat = packed.reshape(-1)
    out, at = [], 0
    for width in GAIN_WIDTHS:
        out.append(flat[at:at + depth * width].reshape(depth, width))
        at += depth * width
    return out


def kernel(x, mem, positions, pre_norm_g, w_in, q_norm_g, w_uq, kv_norm_g, w_ukv, conv_w, mem_norm_g, w_mk, w_mv, w_o, post_norm_g, loss_target, m_pre_norm_g, m_w_in, m_q_norm_g, m_w_uq, m_kv_norm_g, m_w_ukv, m_conv_w, m_mem_norm_g, m_w_mk, m_w_mv, m_w_o, m_post_norm_g, v_pre_norm_g, v_w_in, v_q_norm_g, v_w_uq, v_kv_norm_g, v_w_ukv, v_conv_w, v_mem_norm_g, v_w_mk, v_w_mv, v_w_o, v_post_norm_g):
    depth = w_in.shape[0]
    x0, mem0, target = x[0], mem[0], loss_target[0]
    tabs = _rope_tables(positions[0])

    flip = lambda t: jnp.transpose(t, (0, 2, 1))
    w_in, m_w_in, v_w_in = flip(w_in), flip(m_w_in), flip(v_w_in)
    shards = [w_in.astype(BF16), w_uq.astype(BF16), w_ukv.astype(BF16), conv_w, w_mk.astype(BF16), w_mv.astype(BF16),
              w_o.astype(BF16)]
    zero_rows = lambda: jnp.zeros((LANES - ROPE, D_MODEL), BF16)

    def layer_gains(l):
        return tuple(g[l][None, :] for g in (pre_norm_g, q_norm_g, kv_norm_g, mem_norm_g, post_norm_g))

    wts, saved = [None] * depth, [None] * depth
    plan_in, plan_early, plan_late = (_gather_plan(0, shards, zero_rows(), part) for part in ("in", "early", "late"))
    started_in = _comm_start(plan_in, [positions], "l0_gather_in_start")
    started_early = _comm_start(plan_early, [started_in[2]], "l0_gather_early_start")
    started_late = _comm_start(plan_late, [started_early[2]], "l0_gather_late_start")
    h0 = _rmsnorm_fwd(x0, layer_gains(0)[0], 0, D_MODEL, "l0_pre_norm")
    wts[0] = tuple(_comm_finish(plan_in, started_in, [started_late[2], h0, tabs[0][0]], "l0_gather_in_wait"))

    next_gather = {}

    def start_next_gather(l, after):
        plan = _gather_plan(l + 1, shards, zero_rows())
        next_gather[l + 1] = (plan, _comm_start(plan, [after], f"l{l + 1}_gather_start"))
        return next_gather[l + 1][1][2][0:1, 0:1]

    def rest_of_layer0(z):
        early = _comm_finish(plan_early, started_early, [z], "l0_gather_early_wait")
        late = _comm_finish(plan_late, started_late, [early[1]], "l0_gather_late_wait")
        got = _late_weights(early + late)
        wts[0] = wts[0] + got
        if depth > 1:
            got = (got[0] + start_next_gather(0, early[1]).astype(BF16),) + got[1:]
        return got

    act, h_act = x0, h0
    for l in range(depth):
        gains = layer_gains(l)
        if 0 < l < depth - 1:
            gains = gains[:1] + (gains[1] + start_next_gather(l, wts[l][5]),) + gains[2:]
        act, saved[l], h_act = _layer_fwd(l, act, mem0, wts[l], gains, tabs, rest_of_layer0 if l == 0 else None, h_act,
                                          layer_gains(l + 1)[0] if l + 1 < depth else None)
        if l + 1 < depth:
            plan, started = next_gather[l + 1]
            wts[l + 1] = _layer_weights(_comm_finish(plan, started, [act], f"l{l + 1}_gather_wait"))
    grad, loss_part = _loss_head(act, target, "loss_head")
    loss = lax.psum(loss_part[0, 0], ("x", "y", "c"))

    names = ("w_in", "w_uq", "w_ukv", "conv_w", "w_mk", "w_mv", "w_o")
    w_shards = (w_in, w_uq, w_ukv, conv_w, w_mk, w_mv, w_o)
    m_shards = (m_w_in, m_w_uq, m_w_ukv, m_conv_w, m_w_mk, m_w_mv, m_w_o)
    v_shards = (v_w_in, v_w_uq, v_w_ukv, v_conv_w, v_w_mk, v_w_mv, v_w_o)
    stacked = [None] * len(names)

    def sum_and_send(l, received):
        partial = [_sum_slots(r, f"l{l}_grad_sum_{names[i]}") for i, r in enumerate(received)]
        plan = _sibling_plan(partial)
        return l, partial, plan, _comm_start(plan, [partial[0]], f"l{l}_sibling_start")

    def receive_and_update(state, after):
        l, partial, plan, started = state
        other = _comm_finish(plan, started, [after], f"l{l}_sibling_wait")
        for i, name in enumerate(names):
            stacked[i] = _adamw_layer(l, w_shards[i], partial[i], other[i], m_shards[i], v_shards[i], stacked[i],
                                      f"l{l}_adamw_{name}")

    dgs = [None] * depth
    pending = None
    in_flight = None
    for l in reversed(range(depth)):
        gains = layer_gains(l)
        for token in ([pending[1][2]] if pending else []) + ([in_flight[3][2]] if in_flight else []):
            gains = gains[:4] + (gains[4] + token[0:1, 0:1],)
        grad, contrib, dgs[l], early = _layer_bwd(l, grad, mem0, saved[l], wts[l], gains, tabs, l == 0)
        if in_flight is not None:
            receive_and_update(in_flight, grad)
            in_flight = None
        if pending is not None:
            in_flight = sum_and_send(l + 1, _comm_finish(pending[0], pending[1], [grad], f"l{l + 1}_exchange_wait"))
        if l > 0:
            pending = (contrib, _comm_start(contrib, [grad], f"l{l}_exchange_start"))
    early, late = early
    got_in = _comm_finish(late[0], late[1], [grad], "l0_exchange_in_wait")
    last = sum_and_send(0, got_in + _comm_finish(early[0], early[1], [got_in[0]], "l0_exchange_rest_wait"))
    if in_flight is not None:
        receive_and_update(in_flight, last[1][0])
    receive_and_update(last, stacked[0][0] if depth > 1 else last[1][0])
    grad_x = grad[None]
    results = {name: tuple(stacked[i]) for i, name in enumerate(names)}
    results["w_in"] = tuple(flip(t) for t in results["w_in"])

    gain_names = ("pre_norm_g", "q_norm_g", "kv_norm_g", "mem_norm_g", "post_norm_g")
    dg_packed = _pack_gains([jnp.concatenate([dgs[l][i] for l in range(depth)], axis=0) for i in range(5)])
    dg_total = _sum_slots(_gather_all(dg_packed, stacked[0][0], "gain_gather"), "gain_sum")
    gain_outs = (dg_total,) + tuple(_adamw(
        _pack_gains((pre_norm_g, q_norm_g, kv_norm_g, mem_norm_g, post_norm_g)), dg_total,
        _pack_gains((m_pre_norm_g, m_q_norm_g, m_kv_norm_g, m_mem_norm_g, m_post_norm_g)),
        _pack_gains((v_pre_norm_g, v_q_norm_g, v_kv_norm_g, v_mem_norm_g, v_post_norm_g)), "adamw_gains"))
    gain_outs = [_unpack_gains(t, depth) for t in gain_outs]
    for i, name in enumerate(gain_names):
        results[name] = tuple(gain_outs[k][i] for k in range(4))

    order = ("pre_norm_g", "w_in", "q_norm_g", "w_uq", "kv_norm_g", "w_ukv", "conv_w", "mem_norm_g", "w_mk", "w_mv", "w_o",
             "post_norm_g")
    out = [loss, grad_x]
    for k in range(4):
        out += [results[name][k] for name in order]
    return tuple(out)
```

```python
import jax
import jax.numpy as jnp
from jax import lax
from jax.experimental import pallas as pl
from jax.experimental.pallas import tpu as pltpu

F32 = jnp.float32
BF16 = jnp.bfloat16
MESH_ID = pl.DeviceIdType.MESH

D_MODEL = 2048
EPS = 1e-6
LOG2_E = 1.4426950408889634
ROPE_THETA = 10000.0
MLA_HEADS = 8
NOPE = 128
ROPE = 64
HALF_ROPE = ROPE // 2
QK_HEAD = NOPE + ROPE
V_HEAD = 128
Q_RANK = 512
KV_RANK = 256
CONV_W = 512
MEM_HEADS = 4
MEM_HEAD = 128
MEM_W = MEM_HEADS * MEM_HEAD
MLA_W = MLA_HEADS * V_HEAD
MIX_W = MLA_W + CONV_W + MEM_W
IN_COLS = Q_RANK + KV_RANK + ROPE + 3 * CONV_W + MEM_W + MIX_W
N_CHIPS = 4
N_DEV = 8

LANES = 128
VMEM_LIMIT_BYTES = 56 * 1024 * 1024

QPAD = 2 * LANES
Z_GATE = 0
Z_QLAT = Z_GATE + MIX_W
Z_KVLAT = Z_QLAT + Q_RANK
Z_KPE = Z_KVLAT + KV_RANK
Z_GB = Z_KPE + LANES
Z_GC = Z_GB + CONV_W
Z_XIN = Z_GC + CONV_W
Z_QMEM = Z_XIN + CONV_W
Z_COLS = Z_QMEM + MEM_W

ADAM_LR = 0.001
ADAM_B1 = 0.9
ADAM_B2 = 0.999
ADAM_EPS = 1e-08
ADAM_WD = 0.01
ADAM_STEP = 10


def _tile(dim, cap, unit):
    if dim <= cap:
        return dim
    t = (cap // unit) * unit
    while t >= unit:
        if dim % t == 0:
            return t
        t -= unit
    raise ValueError(f"no tile of {dim} under {cap} in units of {unit}")


def _params(*semantics):
    return pltpu.CompilerParams(dimension_semantics=semantics, vmem_limit_bytes=VMEM_LIMIT_BYTES)


def _matmul(a, b, mode, out_dtype, name, tm_cap=512, tn_cap=1024, tk_cap=2048, after=None):
    if mode == "nn":
        (m, k), (k2, n) = a.shape, b.shape
    elif mode == "nt":
        (m, k), (n, k2) = a.shape, b.shape
    else:
        (k, m), (k2, n) = a.shape, b.shape
    assert k == k2, (a.shape, b.shape, mode)
    tm = _tile(m, tm_cap, LANES if mode == "tn" else 16)
    tn = _tile(n, tn_cap, LANES)
    tk = _tile(k, tk_cap, LANES if mode != "tn" else 16)
    nk = k // tk
    if mode == "nn":
        a_spec = pl.BlockSpec((tm, tk), lambda i, j, kk: (i, kk))
        b_spec = pl.BlockSpec((tk, tn), lambda i, j, kk: (kk, j))
        dims = (((1,), (0,)), ((), ()))
    elif mode == "nt":
        a_spec = pl.BlockSpec((tm, tk), lambda i, j, kk: (i, kk))
        b_spec = pl.BlockSpec((tn, tk), lambda i, j, kk: (j, kk))
        dims = (((1,), (1,)), ((), ()))
    else:
        a_spec = pl.BlockSpec((tk, tm), lambda i, j, kk: (kk, i))
        b_spec = pl.BlockSpec((tk, tn), lambda i, j, kk: (kk, j))
        dims = (((0,), (0,)), ((), ()))

    def body(a_ref, b_ref, *rest):
        o_ref, scratch = (rest[1], rest[2:]) if after is not None else (rest[0], rest[1:])
        part = lax.dot_general(a_ref[...].astype(BF16), b_ref[...].astype(BF16), dims, preferred_element_type=F32)
        if nk == 1:
            o_ref[...] = part.astype(o_ref.dtype)
            return
        (acc_ref,) = scratch
        kk = pl.program_id(2)

        @pl.when(kk == 0)
        def _():
            acc_ref[...] = part

        @pl.when(kk > 0)
        def _():
            acc_ref[...] += part

        @pl.when(kk == nk - 1)
        def _():
            o_ref[...] = acc_ref[...].astype(o_ref.dtype)

    return pl.pallas_call(
        body,
        grid=(m // tm, n // tn, nk),
        in_specs=[a_spec, b_spec] + ([] if after is None else [pl.BlockSpec(memory_space=pl.ANY)]),
        out_specs=pl.BlockSpec((tm, tn), lambda i, j, kk: (i, j)),
        out_shape=jax.ShapeDtypeStruct((m, n), out_dtype),
        scratch_shapes=[] if nk == 1 else [pltpu.VMEM((tm, tn), F32)],
        compiler_params=_params("parallel", "parallel", "arbitrary"),
        name=name,
    )(*([a, b] if after is None else [a, b, after]))


def _rmsnorm_fwd(x, gain, col0, width, name):
    rows = x.shape[0]
    tr = _tile(rows, 512, 16)
    cb = col0 // width
    assert cb * width == col0

    def body(x_ref, g_ref, o_ref):
        xv = x_ref[...].astype(F32)
        r = lax.rsqrt(jnp.mean(xv * xv, axis=-1, keepdims=True) + EPS)
        o_ref[...] = (xv * r * g_ref[...]).astype(o_ref.dtype)

    return pl.pallas_call(
        body,
        grid=(rows // tr,),
        in_specs=[pl.BlockSpec((tr, width), lambda i: (i, cb)), pl.BlockSpec((1, width), lambda i: (0, 0))],
        out_specs=pl.BlockSpec((tr, width), lambda i: (i, 0)),
        out_shape=jax.ShapeDtypeStruct((rows, width), BF16),
        compiler_params=_params("parallel"),
        name=name,
    )(x, gain)


def _rmsnorm_bwd(x, gain, dy, resid, col0, width, out_dtype, name):
    rows = x.shape[0]
    tr = _tile(rows, 256, 16)
    cb = col0 // width
    assert cb * width == col0
    has_resid = resid is not None

    def body(*refs):
        if has_resid:
            x_ref, g_ref, dy_ref, res_ref, dx_ref, dg_ref = refs
        else:
            x_ref, g_ref, dy_ref, dx_ref, dg_ref = refs
        i = pl.program_id(0)
        xv = x_ref[...].astype(F32)
        dyv = dy_ref[...].astype(F32)
        r = lax.rsqrt(jnp.mean(xv * xv, axis=-1, keepdims=True) + EPS)
        xr = xv * r
        dyg = dyv * g_ref[...]
        c = jnp.mean(dyg * xr, axis=-1, keepdims=True)
        dx = r * (dyg - xr * c)
        if has_resid:
            dx = dx + res_ref[...]
        dx_ref[...] = dx.astype(dx_ref.dtype)
        part = jnp.sum(dyv * xr, axis=0, keepdims=True)

        @pl.when(i == 0)
        def _():
            dg_ref[...] = part

        @pl.when(i > 0)
        def _():
            dg_ref[...] += part

    row_spec = pl.BlockSpec((tr, width), lambda i: (i, 0))
    in_specs = [pl.BlockSpec((tr, width), lambda i: (i, cb)), pl.BlockSpec((1, width), lambda i: (0, 0)), row_spec]
    args = [x, gain, dy]
    if has_resid:
        in_specs.append(row_spec)
        args.append(resid)
    return pl.pallas_call(
        body,
        grid=(rows // tr,),
        in_specs=in_specs,
        out_specs=[row_spec, pl.BlockSpec((1, width), lambda i: (0, 0))],
        out_shape=[jax.ShapeDtypeStruct((rows, width), out_dtype), jax.ShapeDtypeStruct((1, width), F32)],
        compiler_params=_params("arbitrary"),
        name=name,
    )(*args)


def _post_norm_residual(x, o, gain, next_gain, name):
    rows, width = x.shape
    tr = _tile(rows, 256, 16)
    with_next = next_gain is not None

    def body(*refs):
        x_ref, o_ref, g_ref = refs[:3]
        ov = o_ref[...].astype(F32)
        r = lax.rsqrt(jnp.mean(ov * ov, axis=-1, keepdims=True) + EPS)
        x_new = x_ref[...] + ov * r * g_ref[...]
        if with_next:
            gn_ref, out_ref, h_ref = refs[3:]
            rn = lax.rsqrt(jnp.mean(x_new * x_new, axis=-1, keepdims=True) + EPS)
            h_ref[...] = (x_new * rn * gn_ref[...]).astype(h_ref.dtype)
        else:
            (out_ref,) = refs[3:]
        out_ref[...] = x_new

    row_spec = pl.BlockSpec((tr, width), lambda i: (i, 0))
    gain_spec = pl.BlockSpec((1, width), lambda i: (0, 0))
    res = pl.pallas_call(
        body,
        grid=(rows // tr,),
        in_specs=[row_spec, row_spec, gain_spec] + ([gain_spec] if with_next else []),
        out_specs=[row_spec] + ([row_spec] if with_next else []),
        out_shape=[jax.ShapeDtypeStruct((rows, width), F32)] + ([jax.ShapeDtypeStruct((rows, width), BF16)] if with_next else []),
        compiler_params=_params("parallel"),
        name=name,
    )(*([x, o, gain] + ([next_gain] if with_next else [])))
    return (res[0], res[1]) if with_next else (res[0], None)


def _latent_prep(z, g_q, g_kv, tab_c, tab_a, tab_b, name):
    rows = z.shape[0]
    tr = _tile(rows, 512, 16)

    def norm(x_ref, g_ref, o_ref):
        xv = x_ref[...].astype(F32)
        r = lax.rsqrt(jnp.mean(xv * xv, axis=-1, keepdims=True) + EPS)
        o_ref[...] = (xv * r * g_ref[...]).astype(o_ref.dtype)

    def body(q_ref, kv_ref, k_ref, gq_ref, gkv_ref, c_ref, a_ref, b_ref, qn_ref, kvn_ref, kpe_ref):
        norm(q_ref, gq_ref, qn_ref)
        norm(kv_ref, gkv_ref, kvn_ref)
        kpe_ref[...] = _rope_rows(k_ref[...].astype(F32), c_ref[...], a_ref[...], b_ref[...], 1).astype(kpe_ref.dtype)

    window = lambda c0, width: pl.BlockSpec((tr, width), lambda i: (i, c0 // width))
    gain = lambda width: pl.BlockSpec((1, width), lambda i: (0, 0))
    tab = pl.BlockSpec((tr, LANES), lambda i: (i, 0))
    out = lambda width: pl.BlockSpec((tr, width), lambda i: (i, 0))
    return pl.pallas_call(
        body,
        grid=(rows // tr,),
        in_specs=[window(Z_QLAT, Q_RANK), window(Z_KVLAT, KV_RANK), window(Z_KPE, LANES), gain(Q_RANK), gain(KV_RANK), tab, tab, tab],
        out_specs=[out(Q_RANK), out(KV_RANK), out(LANES)],
        out_shape=[jax.ShapeDtypeStruct((rows, Q_RANK), BF16), jax.ShapeDtypeStruct((rows, KV_RANK), BF16),
                   jax.ShapeDtypeStruct((rows, LANES), BF16)],
        compiler_params=_params("parallel"),
        name=name,
    )(z, z, z, g_q, g_kv, tab_c, tab_a, tab_b)


def _latent_prep_bwd(z, g_q, g_kv, dqn, dkvn, dkb, tab_c, tab_a, tab_b, heads, name):
    rows = z.shape[0]
    tr = _tile(rows, 256, 16)

    def norm_bwd(x_ref, g_ref, dy_ref, dx_ref, dg_ref, i):
        xv = x_ref[...].astype(F32)
        dyv = dy_ref[...].astype(F32)
        r = lax.rsqrt(jnp.mean(xv * xv, axis=-1, keepdims=True) + EPS)
        xr = xv * r
        dyg = dyv * g_ref[...]
        c = jnp.mean(dyg * xr, axis=-1, keepdims=True)
        dx_ref[...] = (r * (dyg - xr * c)).astype(dx_ref.dtype)
        part = jnp.sum(dyv * xr, axis=0, keepdims=True)

        @pl.when(i == 0)
        def _():
            dg_ref[...] = part

        @pl.when(i > 0)
        def _():
            dg_ref[...] += part

    def body(q_ref, kv_ref, gq_ref, gkv_ref, dqn_ref, dkvn_ref, d_ref, c_ref, a_ref, b_ref,
             dq_ref, dkv_ref, dkpe_ref, dgq_ref, dgkv_ref):
        i = pl.program_id(0)
        norm_bwd(q_ref, gq_ref, dqn_ref, dq_ref, dgq_ref, i)
        norm_bwd(kv_ref, gkv_ref, dkvn_ref, dkv_ref, dgkv_ref, i)
        acc = d_ref[:, 0:LANES]
        for h in range(1, heads):
            acc = acc + d_ref[:, h * LANES:(h + 1) * LANES]
        dkpe_ref[...] = _rope_rows(acc, c_ref[...], a_ref[...], b_ref[...], -1).astype(dkpe_ref.dtype)

    window = lambda c0, width: pl.BlockSpec((tr, width), lambda i: (i, c0 // width))
    gain = lambda width: pl.BlockSpec((1, width), lambda i: (0, 0))
    rows_of = lambda width: pl.BlockSpec((tr, width), lambda i: (i, 0))
    return pl.pallas_call(
        body,
        grid=(rows // tr,),
        in_specs=[window(Z_QLAT, Q_RANK), window(Z_KVLAT, KV_RANK), gain(Q_RANK), gain(KV_RANK), rows_of(Q_RANK), rows_of(KV_RANK),
                  rows_of(heads * LANES), rows_of(LANES), rows_of(LANES), rows_of(LANES)],
        out_specs=[rows_of(Q_RANK), rows_of(KV_RANK), rows_of(LANES), gain(Q_RANK), gain(KV_RANK)],
        out_shape=[jax.ShapeDtypeStruct((rows, Q_RANK), BF16), jax.ShapeDtypeStruct((rows, KV_RANK), BF16),
                   jax.ShapeDtypeStruct((rows, LANES), BF16), jax.ShapeDtypeStruct((1, Q_RANK), F32),
                   jax.ShapeDtypeStruct((1, KV_RANK), F32)],
        compiler_params=_params("arbitrary"),
        name=name,
    )(z, z, g_q, g_kv, dqn, dkvn, dkb, tab_c, tab_a, tab_b)


class _CommPlan:
    def __init__(self, ins, out_shape, build, n_copies):
        self.ins, self.out_shape, self.build, self.n_copies = list(ins), list(out_shape), build, n_copies

    def scratch(self):
        n = self.n_copies
        return [pltpu.SemaphoreType.DMA((n,)), pltpu.SemaphoreType.DMA((n,)), pltpu.SemaphoreType.DMA((n,))]


def _rope_rows(x, c, a, b, sign):
    width = x.shape[-1]
    mixed = pltpu.roll(x, width - HALF_ROPE, 1) * a + pltpu.roll(x, HALF_ROPE, 1) * b
    return x * c + mixed if sign > 0 else x * c - mixed


def _attn_fwd(q, ka, kb, v, rope, heads, q_w, q_cb, ka_cb, ka_step, v_cb, v_step, scale, tq_cap, name, tk_cap=512, o_into=None):
    s_q, s_k = q.shape[0], ka.shape[0]
    tq = _tile(s_q, tq_cap, 16)
    nq = s_q // tq
    has_kb = kb is not None
    n_in = 7 if has_kb else 3
    tk = _tile(s_k, tk_cap, LANES)
    o_cols, o_cb, o_old = o_into if o_into is not None else (heads * LANES, 0, None)

    def body(*refs):
        if o_old is not None:
            refs = refs[:n_in] + refs[n_in + 1:]
        if has_kb:
            q_ref, ka_ref, kb_ref, v_ref, c_ref, a_ref, b_ref, o_ref, lse_ref, k_scr = refs

            @pl.when(pl.program_id(1) == 0)
            def _():
                k_scr[:, 0:LANES] = ka_ref[...].astype(BF16)
                k_scr[:, LANES:2 * LANES] = kb_ref[...].astype(BF16)

            keys = k_scr
            qv = _rope_rows(q_ref[...].astype(F32), c_ref[...], a_ref[...], b_ref[...], 1).astype(BF16)
        else:
            q_ref, ka_ref, v_ref, o_ref, lse_ref = refs
            keys = ka_ref
            qv = q_ref[...].astype(BF16)
        c2 = scale * LOG2_E
        m = l = o = None
        nk = s_k // tk
        scores = lambda j: lax.dot_general(qv, keys[j * tk:(j + 1) * tk, :].astype(BF16), (((1,), (1,)), ((), ())),
                                           preferred_element_type=F32)
        s_next = scores(0)
        for j in range(nk):
            sj = s_next
            if j + 1 < nk:
                s_next = scores(j + 1)
            mj = jnp.max(sj, axis=-1, keepdims=True)
            m_new = mj if m is None else jnp.maximum(m, mj)
            pj = jnp.exp2((sj - m_new) * c2)
            lj = jnp.sum(pj, axis=-1, keepdims=True)
            oj = jnp.dot(pj.astype(BF16), v_ref[j * tk:(j + 1) * tk, :].astype(BF16), preferred_element_type=F32)
            if m is None:
                l, o = lj, oj
            else:
                alpha = jnp.exp2((m - m_new) * c2)
                l, o = l * alpha + lj, o * alpha + oj
            m = m_new
        o_ref[...] = (o * (1.0 / l)).astype(o_ref.dtype)
        lse_ref[...] = jnp.broadcast_to(m * c2 + jnp.log2(l), lse_ref.shape)

    in_specs = [pl.BlockSpec((tq, q_w), lambda h, i: (i, q_cb + h)),
                pl.BlockSpec((s_k, LANES), lambda h, i: (0, ka_cb + ka_step * h))]
    args = [q, ka]
    if has_kb:
        in_specs.append(pl.BlockSpec((s_k, LANES), lambda h, i: (0, 0)))
        args.append(kb)
    in_specs.append(pl.BlockSpec((s_k, LANES), lambda h, i: (0, v_cb + v_step * h)))
    args.append(v)
    if has_kb:
        in_specs += [pl.BlockSpec((tq, q_w), lambda h, i: (i, 0))] * 3
        args += list(rope)
    aliases = {}
    if o_old is not None:
        aliases = {len(args): 0}
        in_specs.append(ANY)
        args.append(o_old)
    out_specs = [pl.BlockSpec((tq, LANES), lambda h, i: (i, o_cb + h)), pl.BlockSpec((tq, LANES), lambda h, i: (i, h))]
    out_shape = [jax.ShapeDtypeStruct((s_q, o_cols), BF16), jax.ShapeDtypeStruct((s_q, heads * LANES), F32)]
    scratch = [pltpu.VMEM((s_k, 2 * LANES), BF16)] if has_kb else []
    return pl.pallas_call(
        body,
        grid=(heads, nq),
        in_specs=in_specs,
        out_specs=out_specs,
        out_shape=out_shape,
        scratch_shapes=scratch,
        input_output_aliases=aliases,
        compiler_params=_params("arbitrary", "arbitrary"),
        name=name,
    )(*args)


def _attn_bwd(q, ka, kb, v, o, do, lse, rope, heads, q_w, q_cb, ka_cb, ka_step, v_cb, v_step, o_cb, scale, tq_cap, name,
              tk_cap=512):
    s_q, s_k = q.shape[0], ka.shape[0]
    tq = _tile(s_q, tq_cap, 16)
    nq = s_q // tq
    has_kb = kb is not None
    tk = _tile(s_k, tk_cap, LANES)

    def body(*refs):
        if has_kb:
            (q_ref, ka_ref, kb_ref, v_ref, o_ref, do_ref, lse_ref, c_ref, a_ref, b_ref, dq_ref, dkv_ref, dkb_ref, k_scr, dk_acc,
             dv_acc) = refs
        else:
            q_ref, ka_ref, v_ref, o_ref, do_ref, lse_ref, dq_ref, dka_ref, dv_ref, dk_acc, dv_acc = refs
        i = pl.program_id(1)

        @pl.when(i == 0)
        def _():
            dk_acc[...] = jnp.zeros_like(dk_acc)
            dv_acc[...] = jnp.zeros_like(dv_acc)
            if has_kb:
                k_scr[:, 0:LANES] = ka_ref[...].astype(BF16)
                k_scr[:, LANES:2 * LANES] = kb_ref[...].astype(BF16)

        keys = k_scr if has_kb else ka_ref
        if has_kb:
            qv = _rope_rows(q_ref[...].astype(F32), c_ref[...], a_ref[...], b_ref[...], 1).astype(BF16)
        else:
            qv = q_ref[...].astype(BF16)
        dov = do_ref[...].astype(BF16)
        delta = jnp.sum(dov.astype(F32) * o_ref[...].astype(F32), axis=-1, keepdims=True)
        lse2 = lse_ref[:, 0:1]
        c2 = scale * LOG2_E
        nk = s_k // tk
        rows = lambda j: slice(j * tk, (j + 1) * tk)
        nt = (((1,), (1,)), ((), ()))
        tn = (((0,), (0,)), ((), ()))

        def scores(j):
            return (lax.dot_general(qv, keys[rows(j), :].astype(BF16), nt, preferred_element_type=F32),
                    lax.dot_general(dov, v_ref[rows(j), :].astype(BF16), nt, preferred_element_type=F32))

        nxt = scores(0)
        dq = None
        for j in range(nk):
            sj, dpj = nxt
            if j + 1 < nk:
                nxt = scores(j + 1)
            pj = jnp.exp2(sj * c2 - lse2)
            dsj = (pj * (dpj - delta)).astype(BF16)
            dqj = jnp.dot(dsj, keys[rows(j), :].astype(BF16), preferred_element_type=F32)
            dq = dqj if dq is None else dq + dqj
            dk_acc[rows(j), :] += lax.dot_general(dsj, qv, tn, preferred_element_type=F32)
            dv_acc[rows(j), :] += lax.dot_general(pj.astype(BF16), dov, tn, preferred_element_type=F32)
        dq = dq * scale
        if has_kb:
            dq = _rope_rows(dq, c_ref[...], a_ref[...], b_ref[...], -1)
        dq_ref[...] = dq.astype(dq_ref.dtype)

        @pl.when(i == nq - 1)
        def _():
            if has_kb:
                dkv_ref[:, 0:LANES] = (dk_acc[:, 0:LANES] * scale).astype(dkv_ref.dtype)
                dkv_ref[:, LANES:2 * LANES] = dv_acc[...].astype(dkv_ref.dtype)
                dkb_ref[...] = dk_acc[:, LANES:2 * LANES] * scale
            else:
                dka_ref[...] = (dk_acc[...] * scale).astype(dka_ref.dtype)
                dv_ref[...] = dv_acc[...].astype(dv_ref.dtype)

    key_spec = lambda cb, step: pl.BlockSpec((s_k, LANES), lambda h, i: (0, cb + step * h))
    row_spec = lambda cb: pl.BlockSpec((tq, LANES), lambda h, i: (i, cb + h))
    in_specs = [pl.BlockSpec((tq, q_w), lambda h, i: (i, q_cb + h)), key_spec(ka_cb, ka_step)]
    args = [q, ka]
    if has_kb:
        in_specs.append(pl.BlockSpec((s_k, LANES), lambda h, i: (0, 0)))
        args.append(kb)
    in_specs += [key_spec(v_cb, v_step), row_spec(o_cb), row_spec(o_cb), row_spec(0)]
    args += [v, o, do, lse]
    if has_kb:
        in_specs += [pl.BlockSpec((tq, q_w), lambda h, i: (i, 0))] * 3
        args += list(rope)
    out_specs = [pl.BlockSpec((tq, q_w), lambda h, i: (i, h))]
    out_shape = [jax.ShapeDtypeStruct((s_q, heads * q_w), BF16)]
    scratch = []
    if has_kb:
        out_specs += [pl.BlockSpec((s_k, 2 * LANES), lambda h, i: (0, h)), key_spec(0, 1)]
        out_shape += [jax.ShapeDtypeStruct((s_k, heads * 2 * LANES), BF16), jax.ShapeDtypeStruct((s_k, heads * LANES), F32)]
        scratch.append(pltpu.VMEM((s_k, 2 * LANES), BF16))
    else:
        out_specs += [key_spec(0, 1), key_spec(0, 1)]
        out_shape += [jax.ShapeDtypeStruct((s_k, heads * LANES), BF16)] * 2
    scratch += [pltpu.VMEM((s_k, q_w), F32), pltpu.VMEM((s_k, LANES), F32)]
    return pl.pallas_call(
        body,
        grid=(heads, nq),
        in_specs=in_specs,
        out_specs=out_specs,
        out_shape=out_shape,
        scratch_shapes=scratch,
        compiler_params=_params("arbitrary", "arbitrary"),
        name=name,
    )(*args)


def _shift_rows(u, rows):
    t = lax.broadcasted_iota(jnp.int32, u.shape, 0)
    prev = jnp.where(t == 0, 0.0, pltpu.roll(u, 1, 0))
    nxt = jnp.where(t == rows - 1, 0.0, pltpu.roll(u, rows - 1, 0))
    return prev, nxt


def _conv_fwd(z, conv_w, cat, name):
    rows = z.shape[0]
    nblk = CONV_W // LANES

    def body(gb_ref, gc_ref, xin_ref, w_ref, cat_ref, o_ref):
        del cat_ref
        u = gc_ref[...].astype(F32) * xin_ref[...].astype(F32)
        prev, nxt = _shift_rows(u, rows)
        conv = prev * w_ref[0:1, :] + u * w_ref[1:2, :] + nxt * w_ref[2:3, :]
        o_ref[...] = (gb_ref[...].astype(F32) * conv).astype(o_ref.dtype)

    col = lambda c0: pl.BlockSpec((rows, LANES), lambda j: (0, c0 // LANES + j))
    return pl.pallas_call(
        body,
        grid=(nblk,),
        in_specs=[col(Z_GB), col(Z_GC), col(Z_XIN), pl.BlockSpec((3, LANES), lambda j: (0, j)), ANY],
        out_specs=col(MLA_W),
        out_shape=jax.ShapeDtypeStruct(cat.shape, cat.dtype),
        input_output_aliases={4: 0},
        compiler_params=_params("parallel"),
        name=name,
    )(z, z, z, conv_w, cat)


def _conv_bwd(z, conv_w, dcat, name):
    rows = z.shape[0]
    nblk = CONV_W // LANES

    def body(gb_ref, gc_ref, xin_ref, w_ref, dc_ref, dgb_ref, dgc_ref, dxin_ref, dw_ref):
        gc = gc_ref[...].astype(F32)
        xin = xin_ref[...].astype(F32)
        dc = dc_ref[...].astype(F32)
        u = gc * xin
        prev, nxt = _shift_rows(u, rows)
        w0, w1, w2 = w_ref[0:1, :], w_ref[1:2, :], w_ref[2:3, :]
        conv = prev * w0 + u * w1 + nxt * w2
        dgb_ref[...] = (dc * conv).astype(dgb_ref.dtype)
        dconv = dc * gb_ref[...].astype(F32)
        dw_ref[0:1, :] = jnp.sum(dconv * prev, axis=0, keepdims=True)
        dw_ref[1:2, :] = jnp.sum(dconv * u, axis=0, keepdims=True)
        dw_ref[2:3, :] = jnp.sum(dconv * nxt, axis=0, keepdims=True)
        dprev, dnxt = _shift_rows(dconv, rows)
        du = dnxt * w0 + dconv * w1 + dprev * w2
        dgc_ref[...] = (du * xin).astype(dgc_ref.dtype)
        dxin_ref[...] = (du * gc).astype(dxin_ref.dtype)

    col = lambda c0: pl.BlockSpec((rows, LANES), lambda j: (0, c0 // LANES + j))
    w_spec = pl.BlockSpec((3, LANES), lambda j: (0, j))
    piece = jax.ShapeDtypeStruct((rows, CONV_W), BF16)
    return pl.pallas_call(
        body,
        grid=(nblk,),
        in_specs=[col(Z_GB), col(Z_GC), col(Z_XIN), w_spec, col(MLA_W)],
        out_specs=[col(0), col(0), col(0), w_spec],
        out_shape=[piece, piece, piece, jax.ShapeDtypeStruct((3, CONV_W), F32)],
        compiler_params=_params("parallel"),
        name=name,
    )(z, z, z, conv_w, dcat)


def _gate_fwd(cat, z, name):
    rows = cat.shape[0]
    tr = _tile(rows, 256, 16)
    tc = MIX_W
    g0 = Z_GATE // tc

    def body(c_ref, g_ref, y_ref):
        g = g_ref[...].astype(F32)
        y_ref[...] = (c_ref[...].astype(F32) * (g * jax.nn.sigmoid(g))).astype(y_ref.dtype)

    blk = pl.BlockSpec((tr, tc), lambda i, j: (i, j))
    return pl.pallas_call(
        body,
        grid=(rows // tr, MIX_W // tc),
        in_specs=[blk, pl.BlockSpec((tr, tc), lambda i, j: (i, g0 + j))],
        out_specs=blk,
        out_shape=jax.ShapeDtypeStruct((rows, MIX_W), BF16),
        compiler_params=_params("parallel", "parallel"),
        name=name,
    )(cat, z)


def _out_proj_dx_gate_bwd(do, w_o, cat, z, name):
    rows, k = do.shape
    tm = _tile(rows, 1024, 16)
    tn = _tile(MIX_W, 1024, LANES)
    g0 = Z_GATE // tn

    def body(do_ref, w_ref, c_ref, g_ref, dcat_ref, dgate_ref):
        dy = lax.dot_general(do_ref[...], w_ref[...], (((1,), (1,)), ((), ())), preferred_element_type=F32)
        g = g_ref[...].astype(F32)
        sg = jax.nn.sigmoid(g)
        dcat_ref[...] = (dy * (g * sg)).astype(dcat_ref.dtype)
        dgate_ref[...] = (dy * c_ref[...].astype(F32) * (sg * (1.0 + g * (1.0 - sg)))).astype(dgate_ref.dtype)

    blk = pl.BlockSpec((tm, tn), lambda i, j: (i, j))
    out = jax.ShapeDtypeStruct((rows, MIX_W), BF16)
    return pl.pallas_call(
        body,
        grid=(rows // tm, MIX_W // tn),
        in_specs=[pl.BlockSpec((tm, k), lambda i, j: (i, 0)), pl.BlockSpec((tn, k), lambda i, j: (j, 0)), blk,
                  pl.BlockSpec((tm, tn), lambda i, j: (i, g0 + j))],
        out_specs=[blk, blk],
        out_shape=[out, out],
        compiler_params=_params("parallel", "parallel"),
        name=name,
    )(do, w_o, cat, z)


def _loss_head(y, target, name):
    rows, width = y.shape
    tr = _tile(rows, 256, 8)

    def body(y_ref, t_ref, g_ref, loss_ref):
        i = pl.program_id(0)
        d = y_ref[...] - t_ref[...]
        g_ref[...] = d / width
        part = 0.5 * jnp.sum(jnp.mean(d * d, axis=-1, keepdims=True), axis=0, keepdims=True)
        part = jnp.broadcast_to(part, loss_ref.shape)

        @pl.when(i == 0)
        def _():
            loss_ref[...] = part

        @pl.when(i > 0)
        def _():
            loss_ref[...] += part

    row_spec = pl.BlockSpec((tr, width), lambda i: (i, 0))
    return pl.pallas_call(
        body,
        grid=(rows // tr,),
        in_specs=[row_spec, row_spec],
        out_specs=[row_spec, pl.BlockSpec((1, LANES), lambda i: (0, 0))],
        out_shape=[jax.ShapeDtypeStruct((rows, width), F32), jax.ShapeDtypeStruct((1, LANES), F32)],
        compiler_params=_params("arbitrary"),
        name=name,
    )(y, target)


CHIP_FLIPS = ((1, 0), (0, 1), (1, 1))
ANY = pl.BlockSpec(memory_space=pl.ANY)


def _chip_copies(pieces, sems, n_slot):
    send_sems, recv_sems, local_sems = sems
    x, y, c = lax.axis_index("x"), lax.axis_index("y"), lax.axis_index("c")
    me = 2 * x + y

    def remote(j, k, a, src, dst):
        fx, fy = CHIP_FLIPS[k]
        return pltpu.make_async_remote_copy(
            src_ref=src, dst_ref=dst, send_sem=send_sems.at[n_slot * k + a], recv_sem=recv_sems.at[n_slot * k + a],
            device_id=((j // 2) ^ fx, (j % 2) ^ fy, c), device_id_type=MESH_ID)

    def peer(j, k):
        fx, fy = CHIP_FLIPS[k]
        return 2 * ((j // 2) ^ fx) + ((j % 2) ^ fy)

    def start_as(j):
        def run():
            for a, (src, dst) in enumerate(pieces(j, j)):
                pltpu.make_async_copy(src, dst, local_sems.at[a]).start()
            for k in range(len(CHIP_FLIPS)):
                for a, (src, dst) in enumerate(pieces(j, peer(j, k))):
                    remote(j, k, a, src, dst).start()
        return run

    def wait_as(j):
        def run():
            for a, (src, dst) in enumerate(pieces(j, j)):
                pltpu.make_async_copy(src, dst, local_sems.at[a]).wait()
            for k in range(len(CHIP_FLIPS)):
                for a, (src, dst) in enumerate(pieces(j, peer(j, k))):
                    remote(j, k, a, src, dst).wait_send()
                for a, (src, dst) in enumerate(pieces(peer(j, k), j)):
                    remote(j, k, a, src, dst).wait_recv()
        return run

    def start():
        for j in range(N_CHIPS):
            pl.when(me == j)(start_as(j))

    def wait():
        for j in range(N_CHIPS):
            pl.when(me == j)(wait_as(j))

    return start, wait


IN_PIECES = ((0, Q_RANK, Z_QLAT), (Q_RANK, KV_RANK, Z_KVLAT), (Q_RANK + KV_RANK, ROPE, Z_KPE),
             (Q_RANK + KV_RANK + ROPE, CONV_W, Z_GB), (Q_RANK + KV_RANK + ROPE + CONV_W, CONV_W, Z_GC),
             (Q_RANK + KV_RANK + ROPE + 2 * CONV_W, CONV_W, Z_XIN), (Q_RANK + KV_RANK + ROPE + 3 * CONV_W, MEM_W, Z_QMEM),
             (Q_RANK + KV_RANK + ROPE + 3 * CONV_W + MEM_W, MIX_W, Z_GATE))
IN_SHARD = IN_COLS // N_CHIPS


def _in_segments(j):
    lo, hi = j * IN_SHARD, (j + 1) * IN_SHARD
    segs = []
    for r0, width, z0 in IN_PIECES:
        a, b = max(lo, r0), min(hi, r0 + width)
        if a < b:
            segs.append((a - lo, z0 + a - r0, b - a))
    return segs


N_SLOT = 11


def _gather_plan(l, shards, zero_rows, part="all"):
    s_in, s_uq, s_ukv, s_conv, s_mk, s_mv, s_o = shards
    ukv_c, mk_r, mk_c, o_r = s_ukv.shape[2], s_mk.shape[1], s_mk.shape[2], s_o.shape[1]
    stack = lambda s: jax.ShapeDtypeStruct((N_CHIPS,) + s.shape[1:], s.dtype)
    in_ins, in_outs = [s_in, zero_rows], [jax.ShapeDtypeStruct((Z_COLS, s_in.shape[2]), s_in.dtype)]
    early_ins = [s_uq, s_ukv, s_conv]
    early_outs = [stack(s_uq), jax.ShapeDtypeStruct((s_ukv.shape[1], N_CHIPS * ukv_c), s_ukv.dtype), stack(s_conv)]
    late_ins = [s_mk, s_mv, s_o]
    late_outs = [jax.ShapeDtypeStruct((N_CHIPS * mk_r, 2 * mk_c), s_mk.dtype),
                 jax.ShapeDtypeStruct((N_CHIPS * o_r, s_o.shape[2]), s_o.dtype)]
    with_in, with_early, with_late = part in ("all", "in"), part in ("all", "early"), part in ("all", "late")

    def build(ins, outs, sems):
        ins, outs = list(ins), list(outs)
        if with_in:
            r_in, r_zero, f_in = ins.pop(0), ins.pop(0), outs.pop(0)
        if with_early:
            r_uq, r_ukv, r_conv = ins.pop(0), ins.pop(0), ins.pop(0)
            g_uq, f_ukv, g_conv = outs.pop(0), outs.pop(0), outs.pop(0)
        if with_late:
            r_mk, r_mv, r_o = ins
            f_mkv, f_o = outs

        def pieces(j, t):
            out = []
            if with_in:
                out += [(r_in.at[l, pl.ds(so, n), :], f_in.at[pl.ds(zo, n), :]) for so, zo, n in _in_segments(j)]
            if with_early:
                out += [(r_uq.at[l], g_uq.at[j]), (r_ukv.at[l], f_ukv.at[:, pl.ds(j * ukv_c, ukv_c)]),
                        (r_conv.at[l], g_conv.at[j])]
            if with_late:
                out += [(r_mk.at[l], f_mkv.at[pl.ds(j * mk_r, mk_r), pl.ds(0, mk_c)]),
                        (r_mv.at[l], f_mkv.at[pl.ds(j * mk_r, mk_r), pl.ds(mk_c, mk_c)]),
                        (r_o.at[l], f_o.at[pl.ds(j * o_r, o_r), :])]
            if with_in and j == t:
                out.append((r_zero, f_in.at[pl.ds(Z_KPE + ROPE, LANES - ROPE), :]))
            return out

        return _chip_copies(pieces, sems, N_SLOT)

    ins = (in_ins if with_in else []) + (early_ins if with_early else []) + (late_ins if with_late else [])
    outs = (in_outs if with_in else []) + (early_outs if with_early else []) + (late_outs if with_late else [])
    return _CommPlan(ins, outs, build, len(CHIP_FLIPS) * N_SLOT)


def _scatter_plan(dwt_in, c_uq, dw_ukv, c_conv, dw_mkv, dw_o, part="all"):
    ukv_c, mk_r, mk_c, o_r = dw_ukv.shape[1] // N_CHIPS, dw_mkv.shape[0] // N_CHIPS, dw_mkv.shape[1] // 2, dw_o.shape[0] // N_CHIPS
    with_in, with_rest = part != "rest", part != "in"
    in_outs = [jax.ShapeDtypeStruct((N_CHIPS, IN_SHARD, D_MODEL), BF16)]
    rest_ins = [c_uq, dw_ukv, c_conv, dw_mkv, dw_o]
    rest_outs = [jax.ShapeDtypeStruct(c_uq.shape, c_uq.dtype),
                 jax.ShapeDtypeStruct((N_CHIPS, dw_ukv.shape[0], ukv_c), dw_ukv.dtype),
                 jax.ShapeDtypeStruct(c_conv.shape, c_conv.dtype),
                 jax.ShapeDtypeStruct((N_CHIPS, mk_r, mk_c), dw_mkv.dtype), jax.ShapeDtypeStruct((N_CHIPS, mk_r, mk_c), dw_mkv.dtype),
                 jax.ShapeDtypeStruct((N_CHIPS, o_r, dw_o.shape[1]), dw_o.dtype)]

    def build(ins, outs, sems):
        ins, outs = list(ins), list(outs)
        if with_in:
            r_in, o_in = ins.pop(0), outs.pop(0)
        if with_rest:
            r_uq, r_ukv, r_conv, r_mkv, r_o = ins
            o_uq, o_ukv, o_conv, o_mk, o_mv, o_o = outs

        def pieces(j, t):
            out = []
            if with_in:
                out += [(r_in.at[pl.ds(zo, n), :], o_in.at[j, pl.ds(so, n), :]) for so, zo, n in _in_segments(t)]
            if with_rest:
                out += [(r_uq.at[t], o_uq.at[j]), (r_ukv.at[:, pl.ds(t * ukv_c, ukv_c)], o_ukv.at[j]),
                        (r_conv.at[t], o_conv.at[j]),
                        (r_mkv.at[pl.ds(t * mk_r, mk_r), pl.ds(0, mk_c)], o_mk.at[j]),
                        (r_mkv.at[pl.ds(t * mk_r, mk_r), pl.ds(mk_c, mk_c)], o_mv.at[j]),
                        (r_o.at[pl.ds(t * o_r, o_r), :], o_o.at[j])]
            return out

        return _chip_copies(pieces, sems, N_SLOT)

    ins = ([dwt_in] if with_in else []) + (rest_ins if with_rest else [])
    outs = (in_outs if with_in else []) + (rest_outs if with_rest else [])
    return _CommPlan(ins, outs, build, len(CHIP_FLIPS) * N_SLOT)


HBM = pl.BlockSpec(memory_space=pltpu.HBM)
SEM = pl.BlockSpec(memory_space=pltpu.SEMAPHORE)
SIDE_EFFECT = pltpu.SideEffectType.DATAFLOW_SIDE_EFFECTING


def _comm_start(plan, after, name):
    n_in, n_out, n_after = len(plan.ins), len(plan.out_shape), len(after)
    n_buf = n_in + n_out

    def body(*refs):
        bufs, sems, token = refs[:n_buf], refs[n_buf + n_after:n_buf + n_after + 3], refs[-1]
        start, _ = plan.build(bufs[:n_in], bufs[n_in:], sems)
        start()
        token[...] = jnp.zeros_like(token)

    lands = [lax.empty(s.shape, s.dtype) for s in plan.out_shape]
    args = [pltpu.with_memory_space_constraint(a, pltpu.HBM) for a in list(plan.ins) + lands]
    res = pl.pallas_call(
        body,
        in_specs=[HBM] * n_buf + [ANY] * n_after,
        out_specs=[SEM] * 3 + [HBM] * n_out + [pl.BlockSpec(memory_space=pltpu.VMEM)],
        out_shape=plan.scratch() + [pltpu.HBM(a.shape, a.dtype) for a in lands] + [jax.ShapeDtypeStruct((8, LANES), F32)],
        input_output_aliases={n_in + i: 3 + i for i in range(n_out)},
        compiler_params=pltpu.CompilerParams(has_side_effects=SIDE_EFFECT),
        name=name,
    )(*args, *after)
    return list(res[:3]), list(res[3:3 + n_out]), res[-1]


def _comm_finish(plan, started, after, name):
    sems, lands, _ = started
    n_in, n_out = len(plan.ins), len(plan.out_shape)
    n_buf = n_in + n_out

    def body(*refs):
        bufs_in, sem_refs = refs[:n_buf], refs[n_buf:n_buf + 3]
        _, wait = plan.build(bufs_in[:n_in], bufs_in[n_in:], sem_refs)
        wait()

    sources = [pltpu.with_memory_space_constraint(a, pltpu.HBM) for a in plan.ins]
    res = pl.pallas_call(
        body,
        in_specs=[HBM] * n_buf + [SEM] * 3 + [ANY] * len(after),
        out_specs=[HBM] * n_out,
        out_shape=[pltpu.HBM(b.shape, b.dtype) for b in lands],
        input_output_aliases={n_in + i: i for i in range(n_out)},
        compiler_params=pltpu.CompilerParams(has_side_effects=SIDE_EFFECT),
        name=name,
    )(*sources, *lands, *sems, *after)
    return list(res)


def _sibling_plan(arrays):
    def build(ins, outs, sems):
        send_sems, recv_sems, _ = sems
        sibling = (lax.axis_index("x"), lax.axis_index("y"), 1 - lax.axis_index("c"))
        copies = [pltpu.make_async_remote_copy(src_ref=src, dst_ref=dst, send_sem=send_sems.at[a], recv_sem=recv_sems.at[a],
                                               device_id=sibling, device_id_type=MESH_ID)
                  for a, (src, dst) in enumerate(zip(ins, outs))]

        def start():
            for cp in copies:
                cp.start()

        def wait():
            for cp in copies:
                cp.wait()

        return start, wait

    return _CommPlan(arrays, [jax.ShapeDtypeStruct(v.shape, v.dtype) for v in arrays], build, len(arrays))


DEVICE_FLIPS = tuple((fx, fy, fc) for fx in (0, 1) for fy in (0, 1) for fc in (0, 1))[1:]


def _gather_all(v, after, name):
    def body(v_ref, after_ref, out_ref, send_sems, recv_sems, local_sem):
        del after_ref
        x, y, c = lax.axis_index("x"), lax.axis_index("y"), lax.axis_index("c")
        me = 4 * x + 2 * y + c
        local = pltpu.make_async_copy(v_ref, out_ref.at[me], local_sem)
        local.start()
        copies = [local]
        for k, (fx, fy, fc) in enumerate(DEVICE_FLIPS):
            cp = pltpu.make_async_remote_copy(
                src_ref=v_ref, dst_ref=out_ref.at[me], send_sem=send_sems.at[k], recv_sem=recv_sems.at[k],
                device_id=((x + fx) % 2, (y + fy) % 2, (c + fc) % 2), device_id_type=MESH_ID)
            cp.start()
            copies.append(cp)
        for cp in copies:
            cp.wait()

    return pl.pallas_call(
        body,
        in_specs=[ANY, ANY],
        out_specs=ANY,
        out_shape=jax.ShapeDtypeStruct((N_DEV,) + v.shape, v.dtype),
        scratch_shapes=[pltpu.SemaphoreType.DMA((N_DEV - 1,)), pltpu.SemaphoreType.DMA((N_DEV - 1,)), pltpu.SemaphoreType.DMA],
        name=name,
    )(v, after)


def _sum_slots(parts, name):
    n, rows, cols = parts.shape
    tr = _tile(rows, 256, 16)

    def body(p_ref, o_ref):
        acc = p_ref[0].astype(F32)
        for k in range(1, n):
            acc = acc + p_ref[k].astype(F32)
        o_ref[...] = acc

    return pl.pallas_call(
        body,
        grid=(rows // tr,),
        in_specs=[pl.BlockSpec((n, tr, cols), lambda i: (0, i, 0))],
        out_specs=pl.BlockSpec((tr, cols), lambda i: (i, 0)),
        out_shape=jax.ShapeDtypeStruct((rows, cols), F32),
        compiler_params=_params("parallel"),
        name=name,
    )(parts)


def _adamw_math(w, g, m, v):
    m_new = ADAM_B1 * m + (1.0 - ADAM_B1) * g
    v_new = ADAM_B2 * v + (1.0 - ADAM_B2) * jnp.square(g)
    m_hat = m_new / (1.0 - ADAM_B1 ** ADAM_STEP)
    v_hat = v_new / (1.0 - ADAM_B2 ** ADAM_STEP)
    return -ADAM_LR * (m_hat / (jnp.sqrt(v_hat) + ADAM_EPS) + ADAM_WD * w), m_new, v_new


def _adamw(w, g, m, v, name):
    rows, cols = w.shape
    tr = _tile(rows, 256, 8)

    def body(w_ref, g_ref, m_ref, v_ref, d_out, m_out, v_out):
        d_out[...], m_out[...], v_out[...] = _adamw_math(w_ref[...], g_ref[...], m_ref[...], v_ref[...])

    blk = pl.BlockSpec((tr, cols), lambda i: (i, 0))
    out = jax.ShapeDtypeStruct((rows, cols), F32)
    return pl.pallas_call(
        body,
        grid=(rows // tr,),
        in_specs=[blk] * 4,
        out_specs=[blk] * 3,
        out_shape=[out] * 3,
        compiler_params=_params("parallel"),
        name=name,
    )(w, g, m, v)


def _adamw_layer(l, w, g_a, g_b, m, v, prev, name):
    depth, rows, cols = w.shape
    tr = _tile(rows, 256, 8)

    def body(w_ref, ga_ref, gb_ref, m_ref, v_ref, *rest):
        g_out, d_out, m_out, v_out = rest[-4:]
        g = ga_ref[...] + gb_ref[...]
        g_out[...] = g
        d_out[...], m_out[...], v_out[...] = _adamw_math(w_ref[...], g, m_ref[...], v_ref[...])

    stacked = pl.BlockSpec((None, tr, cols), lambda i: (l, i, 0))
    flat = pl.BlockSpec((tr, cols), lambda i: (i, 0))
    in_specs = [stacked, flat, flat, stacked, stacked]
    args = [w, g_a, g_b, m, v]
    aliases = {}
    if prev is not None:
        in_specs += [ANY] * 4
        args += list(prev)
        aliases = {5 + k: k for k in range(4)}
    out = jax.ShapeDtypeStruct((depth, rows, cols), F32)
    return pl.pallas_call(
        body,
        grid=(rows // tr,),
        in_specs=in_specs,
        out_specs=[stacked] * 4,
        out_shape=[out] * 4,
        input_output_aliases=aliases,
        compiler_params=_params("parallel"),
        name=name,
    )(*args)


def _cols_from_shards(g):
    _, r, c = g.shape
    return jnp.transpose(g, (1, 0, 2)).reshape(r, N_CHIPS * c)


def _cols_to_shards(full):
    r, c4 = full.shape
    c = c4 // N_CHIPS
    return jnp.transpose(full.reshape(r, N_CHIPS, c), (1, 0, 2))


def _w_uq_pad(w_uq):
    r, _ = w_uq.shape
    w = jnp.pad(w_uq.reshape(r, MLA_HEADS, QK_HEAD), ((0, 0), (0, 0), (0, QPAD - QK_HEAD)))
    return w.reshape(r, MLA_HEADS * QPAD)


def _w_uq_unpad(w):
    r, _ = w.shape
    return w.reshape(r, MLA_HEADS, QPAD)[..., :QK_HEAD].reshape(r, MLA_HEADS * QK_HEAD)


def _rope_tables(positions):
    inv_freq = 1.0 / (ROPE_THETA ** (jnp.arange(0, ROPE, 2, dtype=F32) / ROPE))
    ang = positions.astype(F32)[:, None] * inv_freq
    cos, sin = jnp.cos(ang), jnp.sin(ang)
    s = positions.shape[0]
    zero = jnp.zeros((s, HALF_ROPE), F32)
    pad = jnp.zeros((s, LANES - ROPE), F32)
    kc = jnp.concatenate([cos, cos, pad], axis=-1)
    ka = jnp.concatenate([-sin, zero, pad], axis=-1)
    kb = jnp.concatenate([zero, sin, pad], axis=-1)
    qc = jnp.concatenate([jnp.ones((s, NOPE), F32), kc], axis=-1)
    qa = jnp.concatenate([jnp.zeros((s, NOPE), F32), ka], axis=-1)
    qb = jnp.concatenate([jnp.zeros((s, NOPE), F32), kb], axis=-1)
    return (qc, qa, qb), (kc, ka, kb)


def _layer_weights(gathered):
    return (gathered[0],) + _late_weights(gathered[1:])


def _late_weights(gathered):
    g_uq, w_ukv, g_conv, w_mkv, w_o = gathered
    return (_w_uq_pad(_cols_from_shards(g_uq)), w_ukv, _cols_from_shards(g_conv), w_mkv, w_o)


def _layer_fwd(l, x, mem, wts, gains, tabs, late=None, h=None, next_g_pre=None):
    wt_in = wts[0]
    g_pre, g_q, g_kv, g_mem, g_post = gains
    q_tab, k_tab = tabs
    tag = f"l{l}_"
    if h is None:
        h = _rmsnorm_fwd(x, g_pre, 0, D_MODEL, tag + "pre_norm")
    z = _matmul(h, wt_in, "nt", BF16, tag + "in_proj", tm_cap=1024, tn_cap=1664)
    w_uq, w_ukv, conv_w, w_mkv, w_o = wts[1:] if late is None else late(z)
    wts = (wt_in, w_uq, w_ukv, conv_w, w_mkv, w_o)
    qn, kvn, kpe = _latent_prep(z, g_q, g_kv, *k_tab, tag + "latent_prep")
    q_raw = _matmul(qn, w_uq, "nn", BF16, tag + "uq", tm_cap=1024)
    kv = _matmul(kvn, w_ukv, "nn", BF16, tag + "ukv")
    cat, a_lse = _attn_fwd(q_raw, kv, kpe, kv, q_tab, MLA_HEADS, QPAD, 0, 0, 2, 1, 2, QK_HEAD ** -0.5, 512,
                           tag + "mla_fwd", tk_cap=1024, o_into=(MIX_W, 0, None))
    cat = _conv_fwd(z, conv_w, cat, tag + "conv_fwd")
    mem_n = _rmsnorm_fwd(mem, g_mem, 0, D_MODEL, tag + "mem_norm")
    mkv = _matmul(mem_n, w_mkv, "nn", BF16, tag + "mem_kv")
    cat, m_lse = _attn_fwd(z, mkv, None, mkv, None, MEM_HEADS, LANES, Z_QMEM // LANES, 0, 1, MEM_HEADS, 1,
                           MEM_HEAD ** -0.5, 4096, tag + "mem_fwd", o_into=(MIX_W, (MLA_W + CONV_W) // LANES, cat))
    y = _gate_fwd(cat, z, tag + "gate_fwd")
    o = _matmul(y, w_o, "nn", BF16, tag + "out_proj", tm_cap=1024)
    x_new, h_next = _post_norm_residual(x, o, g_post, next_g_pre, tag + "post_norm")
    saved = (x, h, z, qn, kvn, q_raw, kv, kpe, a_lse, mem_n, mkv, m_lse, cat, y, o)
    return x_new, saved, h_next


def _layer_bwd(l, g, mem, saved, wts, gains, tabs, split_exchange=False):
    wt_in, w_uq, w_ukv, conv_w, w_mkv, w_o = wts
    g_pre, g_q, g_kv, g_mem, g_post = gains
    q_tab, k_tab = tabs
    x, h, z, qn, kvn, q_raw, kv, kpe, a_lse, mem_n, mkv, m_lse, cat, y, o = saved
    tag = f"l{l}_"
    do, dg_post = _rmsnorm_bwd(o, g_post, g, None, 0, D_MODEL, BF16, tag + "post_norm_bwd")
    dcat, dgate = _out_proj_dx_gate_bwd(do, w_o, cat, z, tag + "out_proj_dx")
    dw_o = _matmul(y, do, "tn", BF16, tag + "out_proj_dw", tm_cap=1024)
    dq, dkv, dkpe_h = _attn_bwd(q_raw, kv, kpe, kv, cat, dcat, a_lse, q_tab, MLA_HEADS, QPAD, 0, 0, 2, 1, 2, 0,
                                QK_HEAD ** -0.5, 1024, tag + "mla_bwd")
    dw_ukv = _matmul(kvn, dkv, "tn", BF16, tag + "ukv_dw")
    dkvn = _matmul(dkv, w_ukv, "nt", BF16, tag + "ukv_dx")
    dw_uq = _matmul(qn, dq, "tn", BF16, tag + "uq_dw")
    dqn = _matmul(dq, w_uq, "nt", BF16, tag + "uq_dx")
    dq_lat, dkv_lat, dkpe, dg_q, dg_kv = _latent_prep_bwd(z, g_q, g_kv, dqn, dkvn, dkpe_h, *k_tab, MLA_HEADS,
                                                          tag + "latent_prep_bwd")
    dgb, dgc, dxin, dconv_w = _conv_bwd(z, conv_w, dcat, tag + "conv_bwd")
    dq_mem, dmk, dmv = _attn_bwd(z, mkv, None, mkv, cat, dcat, m_lse, None, MEM_HEADS, LANES, Z_QMEM // LANES, 0, 1,
                                 MEM_HEADS, 1, (MLA_W + CONV_W) // LANES, MEM_HEAD ** -0.5, 2048, tag + "mem_bwd")
    dmkv = jnp.concatenate([dmk, dmv], axis=-1)
    dw_mkv = _matmul(mem_n, dmkv, "tn", BF16, tag + "mem_kv_dw")
    dmem_n = _matmul(dmkv, w_mkv, "nt", F32, tag + "mem_kv_dx")
    _, dg_mem = _rmsnorm_bwd(mem, g_mem, dmem_n, None, 0, D_MODEL, BF16, tag + "mem_norm_bwd")
    others = (_cols_to_shards(_w_uq_unpad(dw_uq)), dw_ukv, _cols_to_shards(dconv_w), dw_mkv, dw_o)
    early = None
    if split_exchange:
        early_plan = _scatter_plan(None, *others, part="rest")
        early = (early_plan, _comm_start(early_plan, [dmem_n], tag + "exchange_rest_start"))
        g_pre = g_pre + early[1][2][0:1, 0:1]
    dz = jnp.concatenate([dgate, dq_lat, dkv_lat, dkpe, dgb, dgc, dxin, dq_mem], axis=-1)
    dwt_in = _matmul(dz, h, "tn", BF16, tag + "in_proj_dw", tm_cap=1664, tk_cap=2048)
    contrib = _scatter_plan(dwt_in, *others, part="in" if split_exchange else "all")
    late = None
    if split_exchange:
        late = (contrib, _comm_start(contrib, [dwt_in], tag + "exchange_in_start"))
    dh = _matmul(dz, wt_in, "nn", BF16, tag + "in_proj_dx", tm_cap=1024, tk_cap=1664, after=late[1][2] if late else None)
    dx, dg_pre = _rmsnorm_bwd(x, g_pre, dh, g, 0, D_MODEL, F32, tag + "pre_norm_bwd")
    return dx, contrib, (dg_pre, dg_q, dg_kv, dg_mem, dg_post), (early, late)


GAIN_WIDTHS = (D_MODEL, Q_RANK, KV_RANK, D_MODEL, D_MODEL)


def _pack_gains(parts):
    return jnp.concatenate([p.reshape(-1) for p in parts]).reshape(-1, LANES)


def _unpack_gains(packed, depth):
    flat = packed.reshape(-1)
    out, at = [], 0
    for width in GAIN_WIDTHS:
        out.append(flat[at:at + depth * width].reshape(depth, width))
        at += depth * width
    return out


def kernel(x, mem, positions, pre_norm_g, w_in, q_norm_g, w_uq, kv_norm_g, w_ukv, conv_w, mem_norm_g, w_mk, w_mv, w_o, post_norm_g, loss_target, m_pre_norm_g, m_w_in, m_q_norm_g, m_w_uq, m_kv_norm_g, m_w_ukv, m_conv_w, m_mem_norm_g, m_w_mk, m_w_mv, m_w_o, m_post_norm_g, v_pre_norm_g, v_w_in, v_q_norm_g, v_w_uq, v_kv_norm_g, v_w_ukv, v_conv_w, v_mem_norm_g, v_w_mk, v_w_mv, v_w_o, v_post_norm_g):
    depth = w_in.shape[0]
    x0, mem0, target = x[0], mem[0], loss_target[0]
    tabs = _rope_tables(positions[0])

    flip = lambda t: jnp.transpose(t, (0, 2, 1))
    w_in, m_w_in, v_w_in = flip(w_in), flip(m_w_in), flip(v_w_in)
    shards = [w_in.astype(BF16), w_uq.astype(BF16), w_ukv.astype(BF16), conv_w, w_mk.astype(BF16), w_mv.astype(BF16),
              w_o.astype(BF16)]
    zero_rows = lambda: jnp.zeros((LANES - ROPE, D_MODEL), BF16)

    def layer_gains(l):
        return tuple(g[l][None, :] for g in (pre_norm_g, q_norm_g, kv_norm_g, mem_norm_g, post_norm_g))

    wts, saved = [None] * depth, [None] * depth
    plan_in, plan_early, plan_late = (_gather_plan(0, shards, zero_rows(), part) for part in ("in", "early", "late"))
    started_in = _comm_start(plan_in, [positions], "l0_gather_in_start")
    started_early = _comm_start(plan_early, [started_in[2]], "l0_gather_early_start")
    started_late = _comm_start(plan_late, [started_early[2]], "l0_gather_late_start")
    h0 = _rmsnorm_fwd(x0, layer_gains(0)[0], 0, D_MODEL, "l0_pre_norm")
    wts[0] = tuple(_comm_finish(plan_in, started_in, [started_late[2], h0, tabs[0][0]], "l0_gather_in_wait"))

    next_gather = {}

    def start_next_gather(l, after):
        plan = _gather_plan(l + 1, shards, zero_rows())
        next_gather[l + 1] = (plan, _comm_start(plan, [after], f"l{l + 1}_gather_start"))
        return next_gather[l + 1][1][2][0:1, 0:1]

    def rest_of_layer0(z):
        early = _comm_finish(plan_early, started_early, [z], "l0_gather_early_wait")
        late = _comm_finish(plan_late, started_late, [early[1]], "l0_gather_late_wait")
        got = _late_weights(early + late)
        wts[0] = wts[0] + got
        if depth > 1:
            got = (got[0] + start_next_gather(0, early[1]).astype(BF16),) + got[1:]
        return got

    act, h_act = x0, h0
    for l in range(depth):
        gains = layer_gains(l)
        if 0 < l < depth - 1:
            gains = gains[:1] + (gains[1] + start_next_gather(l, wts[l][5]),) + gains[2:]
        act, saved[l], h_act = _layer_fwd(l, act, mem0, wts[l], gains, tabs, rest_of_layer0 if l == 0 else None, h_act,
                                          layer_gains(l + 1)[0] if l + 1 < depth else None)
        if l + 1 < depth:
            plan, started = next_gather[l + 1]
            wts[l + 1] = _layer_weights(_comm_finish(plan, started, [act], f"l{l + 1}_gather_wait"))
    grad, loss_part = _loss_head(act, target, "loss_head")
    loss = lax.psum(loss_part[0, 0], ("x", "y", "c"))

    names = ("w_in", "w_uq", "w_ukv", "conv_w", "w_mk", "w_mv", "w_o")
    w_shards = (w_in, w_uq, w_ukv, conv_w, w_mk, w_mv, w_o)
    m_shards = (m_w_in, m_w_uq, m_w_ukv, m_conv_w, m_w_mk, m_w_mv, m_w_o)
    v_shards = (v_w_in, v_w_uq, v_w_ukv, v_conv_w, v_w_mk, v_w_mv, v_w_o)
    stacked = [None] * len(names)

    def sum_and_send(l, received):
        partial = [_sum_slots(r, f"l{l}_grad_sum_{names[i]}") for i, r in enumerate(received)]
        plan = _sibling_plan(partial)
        return l, partial, plan, _comm_start(plan, [partial[0]], f"l{l}_sibling_start")

    def receive_and_update(state, after):
        l, partial, plan, started = state
        other = _comm_finish(plan, started, [after], f"l{l}_sibling_wait")
        for i, name in enumerate(names):
            stacked[i] = _adamw_layer(l, w_shards[i], partial[i], other[i], m_shards[i], v_shards[i], stacked[i],
                                      f"l{l}_adamw_{name}")

    dgs = [None] * depth
    pending = None
    in_flight = None
    for l in reversed(range(depth)):
        gains = layer_gains(l)
        for token in ([pending[1][2]] if pending else []) + ([in_flight[3][2]] if in_flight else []):
            gains = gains[:4] + (gains[4] + token[0:1, 0:1],)
        grad, contrib, dgs[l], early = _layer_bwd(l, grad, mem0, saved[l], wts[l], gains, tabs, l == 0)
        if in_flight is not None:
            receive_and_update(in_flight, grad)
            in_flight = None
        if pending is not None:
            in_flight = sum_and_send(l + 1, _comm_finish(pending[0], pending[1], [grad], f"l{l + 1}_exchange_wait"))
        if l > 0:
            pending = (contrib, _comm_start(contrib, [grad], f"l{l}_exchange_start"))
    early, late = early
    got_in = _comm_finish(late[0], late[1], [grad], "l0_exchange_in_wait")
    last = sum_and_send(0, got_in + _comm_finish(early[0], early[1], [got_in[0]], "l0_exchange_rest_wait"))
    if in_flight is not None:
        receive_and_update(in_flight, last[1][0])
    receive_and_update(last, stacked[0][0] if depth > 1 else last[1][0])
    grad_x = grad[None]
    results = {name: tuple(stacked[i]) for i, name in enumerate(names)}
    results["w_in"] = tuple(flip(t) for t in results["w_in"])

    gain_names = ("pre_norm_g", "q_norm_g", "kv_norm_g", "mem_norm_g", "post_norm_g")
    dg_packed = _pack_gains([jnp.concatenate([dgs[l][i] for l in range(depth)], axis=0) for i in range(5)])
    dg_total = _sum_slots(_gather_all(dg_packed, stacked[0][0], "gain_gather"), "gain_sum")
    gain_outs = (dg_total,) + tuple(_adamw(
        _pack_gains((pre_norm_g, q_norm_g, kv_norm_g, mem_norm_g, post_norm_g)), dg_total,
        _pack_gains((m_pre_norm_g, m_q_norm_g, m_kv_norm_g, m_mem_norm_g, m_post_norm_g)),
        _pack_gains((v_pre_norm_g, v_q_norm_g, v_kv_norm_g, v_mem_norm_g, v_post_norm_g)), "adamw_gains"))
    gain_outs = [_unpack_gains(t, depth) for t in gain_outs]
    for i, name in enumerate(gain_names):
        results[name] = tuple(gain_outs[k][i] for k in range(4))

    order = ("pre_norm_g", "w_in", "q_norm_g", "w_uq", "kv_norm_g", "w_ukv", "conv_w", "mem_norm_g", "w_mk", "w_mv", "w_o",
             "post_norm_g")
    out = [loss, grad_x]
    for k in range(4):
        out += [results[name][k] for name in order]
    return tuple(out)
```

```python
import jax
import jax.numpy as jnp
from jax import lax
from jax.experimental import pallas as pl
from jax.experimental.pallas import tpu as pltpu

F32 = jnp.float32
BF16 = jnp.bfloat16
MESH_ID = pl.DeviceIdType.MESH

D_MODEL = 2048
EPS = 1e-6
LOG2_E = 1.4426950408889634
ROPE_THETA = 10000.0
MLA_HEADS = 8
NOPE = 128
ROPE = 64
HALF_ROPE = ROPE // 2
QK_HEAD = NOPE + ROPE
V_HEAD = 128
Q_RANK = 512
KV_RANK = 256
CONV_W = 512
MEM_HEADS = 4
MEM_HEAD = 128
MEM_W = MEM_HEADS * MEM_HEAD
MLA_W = MLA_HEADS * V_HEAD
MIX_W = MLA_W + CONV_W + MEM_W
IN_COLS = Q_RANK + KV_RANK + ROPE + 3 * CONV_W + MEM_W + MIX_W
N_CHIPS = 4
N_DEV = 8

LANES = 128
VMEM_LIMIT_BYTES = 56 * 1024 * 1024

QPAD = 2 * LANES
Z_GATE = 0
Z_QLAT = Z_GATE + MIX_W
Z_KVLAT = Z_QLAT + Q_RANK
Z_KPE = Z_KVLAT + KV_RANK
Z_GB = Z_KPE + LANES
Z_GC = Z_GB + CONV_W
Z_XIN = Z_GC + CONV_W
Z_QMEM = Z_XIN + CONV_W
Z_COLS = Z_QMEM + MEM_W

ADAM_LR = 0.001
ADAM_B1 = 0.9
ADAM_B2 = 0.999
ADAM_EPS = 1e-08
ADAM_WD = 0.01
ADAM_STEP = 10


def _tile(dim, cap, unit):
    if dim <= cap:
        return dim
    t = (cap // unit) * unit
    while t >= unit:
        if dim % t == 0:
            return t
        t -= unit
    raise ValueError(f"no tile of {dim} under {cap} in units of {unit}")


def _params(*semantics):
    return pltpu.CompilerParams(dimension_semantics=semantics, vmem_limit_bytes=VMEM_LIMIT_BYTES)


def _matmul(a, b, mode, out_dtype, name, tm_cap=512, tn_cap=1024, tk_cap=2048, after=None):
    if mode == "nn":
        (m, k), (k2, n) = a.shape, b.shape
    elif mode == "nt":
        (m, k), (n, k2) = a.shape, b.shape
    else:
        (k, m), (k2, n) = a.shape, b.shape
    assert k == k2, (a.shape, b.shape, mode)
    tm = _tile(m, tm_cap, LANES if mode == "tn" else 16)
    tn = _tile(n, tn_cap, LANES)
    tk = _tile(k, tk_cap, LANES if mode != "tn" else 16)
    nk = k // tk
    if mode == "nn":
        a_spec = pl.BlockSpec((tm, tk), lambda i, j, kk: (i, kk))
        b_spec = pl.BlockSpec((tk, tn), lambda i, j, kk: (kk, j))
        dims = (((1,), (0,)), ((), ()))
    elif mode == "nt":
        a_spec = pl.BlockSpec((tm, tk), lambda i, j, kk: (i, kk))
        b_spec = pl.BlockSpec((tn, tk), lambda i, j, kk: (j, kk))
        dims = (((1,), (1,)), ((), ()))
    else:
        a_spec = pl.BlockSpec((tk, tm), lambda i, j, kk: (kk, i))
        b_spec = pl.BlockSpec((tk, tn), lambda i, j, kk: (kk, j))
        dims = (((0,), (0,)), ((), ()))

    def body(a_ref, b_ref, *rest):
        o_ref, scratch = (rest[1], rest[2:]) if after is not None else (rest[0], rest[1:])
        part = lax.dot_general(a_ref[...].astype(BF16), b_ref[...].astype(BF16), dims, preferred_element_type=F32)
        if nk == 1:
            o_ref[...] = part.astype(o_ref.dtype)
            return
        (acc_ref,) = scratch
        kk = pl.program_id(2)

        @pl.when(kk == 0)
        def _():
            acc_ref[...] = part

        @pl.when(kk > 0)
        def _():
            acc_ref[...] += part

        @pl.when(kk == nk - 1)
        def _():
            o_ref[...] = acc_ref[...].astype(o_ref.dtype)

    return pl.pallas_call(
        body,
        grid=(m // tm, n // tn, nk),
        in_specs=[a_spec, b_spec] + ([] if after is None else [pl.BlockSpec(memory_space=pl.ANY)]),
        out_specs=pl.BlockSpec((tm, tn), lambda i, j, kk: (i, j)),
        out_shape=jax.ShapeDtypeStruct((m, n), out_dtype),
        scratch_shapes=[] if nk == 1 else [pltpu.VMEM((tm, tn), F32)],
        compiler_params=_params("parallel", "parallel", "arbitrary"),
        name=name,
    )(*([a, b] if after is None else [a, b, after]))


def _rmsnorm_fwd(x, gain, col0, width, name):
    rows = x.shape[0]
    tr = _tile(rows, 512, 16)
    cb = col0 // width
    assert cb * width == col0

    def body(x_ref, g_ref, o_ref):
        xv = x_ref[...].astype(F32)
        r = lax.rsqrt(jnp.mean(xv * xv, axis=-1, keepdims=True) + EPS)
        o_ref[...] = (xv * r * g_ref[...]).astype(o_ref.dtype)

    return pl.pallas_call(
        body,
        grid=(rows // tr,),
        in_specs=[pl.BlockSpec((tr, width), lambda i: (i, cb)), pl.BlockSpec((1, width), lambda i: (0, 0))],
        out_specs=pl.BlockSpec((tr, width), lambda i: (i, 0)),
        out_shape=jax.ShapeDtypeStruct((rows, width), BF16),
        compiler_params=_params("parallel"),
        name=name,
    )(x, gain)


def _rmsnorm_bwd(x, gain, dy, resid, col0, width, out_dtype, name):
    rows = x.shape[0]
    tr = _tile(rows, 256, 16)
    cb = col0 // width
    assert cb * width == col0
    has_resid = resid is not None

    def body(*refs):
        if has_resid:
            x_ref, g_ref, dy_ref, res_ref, dx_ref, dg_ref = refs
        else:
            x_ref, g_ref, dy_ref, dx_ref, dg_ref = refs
        i = pl.program_id(0)
        xv = x_ref[...].astype(F32)
        dyv = dy_ref[...].astype(F32)
        r = lax.rsqrt(jnp.mean(xv * xv, axis=-1, keepdims=True) + EPS)
        xr = xv * r
        dyg = dyv * g_ref[...]
        c = jnp.mean(dyg * xr, axis=-1, keepdims=True)
        dx = r * (dyg - xr * c)
        if has_resid:
            dx = dx + res_ref[...]
        dx_ref[...] = dx.astype(dx_ref.dtype)
        part = jnp.sum(dyv * xr, axis=0, keepdims=True)

        @pl.when(i == 0)
        def _():
            dg_ref[...] = part

        @pl.when(i > 0)
        def _():
            dg_ref[...] += part

    row_spec = pl.BlockSpec((tr, width), lambda i: (i, 0))
    in_specs = [pl.BlockSpec((tr, width), lambda i: (i, cb)), pl.BlockSpec((1, width), lambda i: (0, 0)), row_spec]
    args = [x, gain, dy]
    if has_resid:
        in_specs.append(row_spec)
        args.append(resid)
    return pl.pallas_call(
        body,
        grid=(rows // tr,),
        in_specs=in_specs,
        out_specs=[row_spec, pl.BlockSpec((1, width), lambda i: (0, 0))],
        out_shape=[jax.ShapeDtypeStruct((rows, width), out_dtype), jax.ShapeDtypeStruct((1, width), F32)],
        compiler_params=_params("arbitrary"),
        name=name,
    )(*args)


def _post_norm_residual(x, o, gain, next_gain, name):
    rows, width = x.shape
    tr = _tile(rows, 256, 16)
    with_next = next_gain is not None

    def body(*refs):
        x_ref, o_ref, g_ref = refs[:3]
        ov = o_ref[...].astype(F32)
        r = lax.rsqrt(jnp.mean(ov * ov, axis=-1, keepdims=True) + EPS)
        x_new = x_ref[...] + ov * r * g_ref[...]
        if with_next:
            gn_ref, out_ref, h_ref = refs[3:]
            rn = lax.rsqrt(jnp.mean(x_new * x_new, axis=-1, keepdims=True) + EPS)
            h_ref[...] = (x_new * rn * gn_ref[...]).astype(h_ref.dtype)
        else:
            (out_ref,) = refs[3:]
        out_ref[...] = x_new

    row_spec = pl.BlockSpec((tr, width), lambda i: (i, 0))
    gain_spec = pl.BlockSpec((1, width), lambda i: (0, 0))
    res = pl.pallas_call(
        body,
        grid=(rows // tr,),
        in_specs=[row_spec, row_spec, gain_spec] + ([gain_spec] if with_next else []),
        out_specs=[row_spec] + ([row_spec] if with_next else []),
        out_shape=[jax.ShapeDtypeStruct((rows, width), F32)] + ([jax.ShapeDtypeStruct((rows, width), BF16)] if with_next else []),
        compiler_params=_params("parallel"),
        name=name,
    )(*([x, o, gain] + ([next_gain] if with_next else [])))
    return (res[0], res[1]) if with_next else (res[0], None)


def _latent_prep(z, g_q, g_kv, tab_c, tab_a, tab_b, name):
    rows = z.shape[0]
    tr = _tile(rows, 512, 16)

    def norm(x_ref, g_ref, o_ref):
        xv = x_ref[...].astype(F32)
        r = lax.rsqrt(jnp.mean(xv * xv, axis=-1, keepdims=True) + EPS)
        o_ref[...] = (xv * r * g_ref[...]).astype(o_ref.dtype)

    def body(q_ref, kv_ref, k_ref, gq_ref, gkv_ref, c_ref, a_ref, b_ref, qn_ref, kvn_ref, kpe_ref):
        norm(q_ref, gq_ref, qn_ref)
        norm(kv_ref, gkv_ref, kvn_ref)
        kpe_ref[...] = _rope_rows(k_ref[...].astype(F32), c_ref[...], a_ref[...], b_ref[...], 1).astype(kpe_ref.dtype)

    window = lambda c0, width: pl.BlockSpec((tr, width), lambda i: (i, c0 // width))
    gain = lambda width: pl.BlockSpec((1, width), lambda i: (0, 0))
    tab = pl.BlockSpec((tr, LANES), lambda i: (i, 0))
    out = lambda width: pl.BlockSpec((tr, width), lambda i: (i, 0))
    return pl.pallas_call(
        body,
        grid=(rows // tr,),
        in_specs=[window(Z_QLAT, Q_RANK), window(Z_KVLAT, KV_RANK), window(Z_KPE, LANES), gain(Q_RANK), gain(KV_RANK), tab, tab, tab],
        out_specs=[out(Q_RANK), out(KV_RANK), out(LANES)],
        out_shape=[jax.ShapeDtypeStruct((rows, Q_RANK), BF16), jax.ShapeDtypeStruct((rows, KV_RANK), BF16),
                   jax.ShapeDtypeStruct((rows, LANES), BF16)],
        compiler_params=_params("parallel"),
        name=name,
    )(z, z, z, g_q, g_kv, tab_c, tab_a, tab_b)


def _latent_prep_bwd(z, g_q, g_kv, dqn, dkvn, dkb, tab_c, tab_a, tab_b, heads, name):
    rows = z.shape[0]
    tr = _tile(rows, 256, 16)

    def norm_bwd(x_ref, g_ref, dy_ref, dx_ref, dg_ref, i):
        xv = x_ref[...].astype(F32)
        dyv = dy_ref[...].astype(F32)
        r = lax.rsqrt(jnp.mean(xv * xv, axis=-1, keepdims=True) + EPS)
        xr = xv * r
        dyg = dyv * g_ref[...]
        c = jnp.mean(dyg * xr, axis=-1, keepdims=True)
        dx_ref[...] = (r * (dyg - xr * c)).astype(dx_ref.dtype)
        part = jnp.sum(dyv * xr, axis=0, keepdims=True)

        @pl.when(i == 0)
        def _():
            dg_ref[...] = part

        @pl.when(i > 0)
        def _():
            dg_ref[...] += part

    def body(q_ref, kv_ref, gq_ref, gkv_ref, dqn_ref, dkvn_ref, d_ref, c_ref, a_ref, b_ref,
             dq_ref, dkv_ref, dkpe_ref, dgq_ref, dgkv_ref):
        i = pl.program_id(0)
        norm_bwd(q_ref, gq_ref, dqn_ref, dq_ref, dgq_ref, i)
        norm_bwd(kv_ref, gkv_ref, dkvn_ref, dkv_ref, dgkv_ref, i)
        acc = d_ref[:, 0:LANES]
        for h in range(1, heads):
            acc = acc + d_ref[:, h * LANES:(h + 1) * LANES]
        dkpe_ref[...] = _rope_rows(acc, c_ref[...], a_ref[...], b_ref[...], -1).astype(dkpe_ref.dtype)

    window = lambda c0, width: pl.BlockSpec((tr, width), lambda i: (i, c0 // width))
    gain = lambda width: pl.BlockSpec((1, width), lambda i: (0, 0))
    rows_of = lambda width: pl.BlockSpec((tr, width), lambda i: (i, 0))
    return pl.pallas_call(
        body,
        grid=(rows // tr,),
        in_specs=[window(Z_QLAT, Q_RANK), window(Z_KVLAT, KV_RANK), gain(Q_RANK), gain(KV_RANK), rows_of(Q_RANK), rows_of(KV_RANK),
                  rows_of(heads * LANES), rows_of(LANES), rows_of(LANES), rows_of(LANES)],
        out_specs=[rows_of(Q_RANK), rows_of(KV_RANK), rows_of(LANES), gain(Q_RANK), gain(KV_RANK)],
        out_shape=[jax.ShapeDtypeStruct((rows, Q_RANK), BF16), jax.ShapeDtypeStruct((rows, KV_RANK), BF16),
                   jax.ShapeDtypeStruct((rows, LANES), BF16), jax.ShapeDtypeStruct((1, Q_RANK), F32),
                   jax.ShapeDtypeStruct((1, KV_RANK), F32)],
        compiler_params=_params("arbitrary"),
        name=name,
    )(z, z, g_q, g_kv, dqn, dkvn, dkb, tab_c, tab_a, tab_b)


class _CommPlan:
    def __init__(self, ins, out_shape, build, n_copies):
        self.ins, self.out_shape, self.build, self.n_copies = list(ins), list(out_shape), build, n_copies

    def scratch(self):
        n = self.n_copies
        return [pltpu.SemaphoreType.DMA((n,)), pltpu.SemaphoreType.DMA((n,)), pltpu.SemaphoreType.DMA((n,))]


def _rope_rows(x, c, a, b, sign):
    width = x.shape[-1]
    mixed = pltpu.roll(x, width - HALF_ROPE, 1) * a + pltpu.roll(x, HALF_ROPE, 1) * b
    return x * c + mixed if sign > 0 else x * c - mixed


def _attn_fwd(q, ka, kb, v, rope, heads, q_w, q_cb, ka_cb, ka_step, v_cb, v_step, scale, tq_cap, name, tk_cap=512, o_into=None):
    s_q, s_k = q.shape[0], ka.shape[0]
    tq = _tile(s_q, tq_cap, 16)
    nq = s_q // tq
    has_kb = kb is not None
    n_in = 7 if has_kb else 3
    tk = _tile(s_k, tk_cap, LANES)
    o_cols, o_cb, o_old = o_into if o_into is not None else (heads * LANES, 0, None)

    def body(*refs):
        if o_old is not None:
            refs = refs[:n_in] + refs[n_in + 1:]
        if has_kb:
            q_ref, ka_ref, kb_ref, v_ref, c_ref, a_ref, b_ref, o_ref, lse_ref, k_scr = refs

            @pl.when(pl.program_id(1) == 0)
            def _():
                k_scr[:, 0:LANES] = ka_ref[...].astype(BF16)
                k_scr[:, LANES:2 * LANES] = kb_ref[...].astype(BF16)

            keys = k_scr
            qv = _rope_rows(q_ref[...].astype(F32), c_ref[...], a_ref[...], b_ref[...], 1).astype(BF16)
        else:
            q_ref, ka_ref, v_ref, o_ref, lse_ref = refs
            keys = ka_ref
            qv = q_ref[...].astype(BF16)
        c2 = scale * LOG2_E
        m = l = o = None
        nk = s_k // tk
        scores = lambda j: lax.dot_general(qv, keys[j * tk:(j + 1) * tk, :].astype(BF16), (((1,), (1,)), ((), ())),
                                           preferred_element_type=F32)
        s_next = scores(0)
        for j in range(nk):
            sj = s_next
            if j + 1 < nk:
                s_next = scores(j + 1)
            mj = jnp.max(sj, axis=-1, keepdims=True)
            m_new = mj if m is None else jnp.maximum(m, mj)
            pj = jnp.exp2((sj - m_new) * c2)
            lj = jnp.sum(pj, axis=-1, keepdims=True)
            oj = jnp.dot(pj.astype(BF16), v_ref[j * tk:(j + 1) * tk, :].astype(BF16), preferred_element_type=F32)
            if m is None:
                l, o = lj, oj
            else:
                alpha = jnp.exp2((m - m_new) * c2)
                l, o = l * alpha + lj, o * alpha + oj
            m = m_new
        o_ref[...] = (o * (1.0 / l)).astype(o_ref.dtype)
        lse_ref[...] = jnp.broadcast_to(m * c2 + jnp.log2(l), lse_ref.shape)

    in_specs = [pl.BlockSpec((tq, q_w), lambda h, i: (i, q_cb + h)),
                pl.BlockSpec((s_k, LANES), lambda h, i: (0, ka_cb + ka_step * h))]
    args = [q, ka]
    if has_kb:
        in_specs.append(pl.BlockSpec((s_k, LANES), lambda h, i: (0, 0)))
        args.append(kb)
    in_specs.append(pl.BlockSpec((s_k, LANES), lambda h, i: (0, v_cb + v_step * h)))
    args.append(v)
    if has_kb:
        in_specs += [pl.BlockSpec((tq, q_w), lambda h, i: (i, 0))] * 3
        args += list(rope)
    aliases = {}
    if o_old is not None:
        aliases = {len(args): 0}
        in_specs.append(ANY)
        args.append(o_old)
    out_specs = [pl.BlockSpec((tq, LANES), lambda h, i: (i, o_cb + h)), pl.BlockSpec((tq, LANES), lambda h, i: (i, h))]
    out_shape = [jax.ShapeDtypeStruct((s_q, o_cols), BF16), jax.ShapeDtypeStruct((s_q, heads * LANES), F32)]
    scratch = [pltpu.VMEM((s_k, 2 * LANES), BF16)] if has_kb else []
    return pl.pallas_call(
        body,
        grid=(heads, nq),
        in_specs=in_specs,
        out_specs=out_specs,
        out_shape=out_shape,
        scratch_shapes=scratch,
        input_output_aliases=aliases,
        compiler_params=_params("arbitrary", "arbitrary"),
        name=name,
    )(*args)


def _attn_bwd(q, ka, kb, v, o, do, lse, rope, heads, q_w, q_cb, ka_cb, ka_step, v_cb, v_step, o_cb, scale, tq_cap, name,
              tk_cap=512):
    s_q, s_k = q.shape[0], ka.shape[0]
    tq = _tile(s_q, tq_cap, 16)
    nq = s_q // tq
    has_kb = kb is not None
    tk = _tile(s_k, tk_cap, LANES)

    def body(*refs):
        if has_kb:
            (q_ref, ka_ref, kb_ref, v_ref, o_ref, do_ref, lse_ref, c_ref, a_ref, b_ref, dq_ref, dkv_ref, dkb_ref, k_scr, dk_acc,
             dv_acc) = refs
        else:
            q_ref, ka_ref, v_ref, o_ref, do_ref, lse_ref, dq_ref, dka_ref, dv_ref, dk_acc, dv_acc = refs
        i = pl.program_id(1)

        @pl.when(i == 0)
        def _():
            dk_acc[...] = jnp.zeros_like(dk_acc)
            dv_acc[...] = jnp.zeros_like(dv_acc)
            if has_kb:
                k_scr[:, 0:LANES] = ka_ref[...].astype(BF16)
                k_scr[:, LANES:2 * LANES] = kb_ref[...].astype(BF16)

        keys = k_scr if has_kb else ka_ref
        if has_kb:
            qv = _rope_rows(q_ref[...].astype(F32), c_ref[...], a_ref[...], b_ref[...], 1).astype(BF16)
        else:
            qv = q_ref[...].astype(BF16)
        dov = do_ref[...].astype(BF16)
        delta = jnp.sum(dov.astype(F32) * o_ref[...].astype(F32), axis=-1, keepdims=True)
        lse2 = lse_ref[:, 0:1]
        c2 = scale * LOG2_E
        nk = s_k // tk
        rows = lambda j: slice(j * tk, (j + 1) * tk)
        nt = (((1,), (1,)), ((), ()))
        tn = (((0,), (0,)), ((), ()))

        def scores(j):
            return (lax.dot_general(qv, keys[rows(j), :].astype(BF16), nt, preferred_element_type=F32),
                    lax.dot_general(dov, v_ref[rows(j), :].astype(BF16), nt, preferred_element_type=F32))

        nxt = scores(0)
        dq = None
        for j in range(nk):
            sj, dpj = nxt
            if j + 1 < nk:
                nxt = scores(j + 1)
            pj = jnp.exp2(sj * c2 - lse2)
            dsj = (pj * (dpj - delta)).astype(BF16)
            dqj = jnp.dot(dsj, keys[rows(j), :].astype(BF16), preferred_element_type=F32)
            dq = dqj if dq is None else dq + dqj
            dk_acc[rows(j), :] += lax.dot_general(dsj, qv, tn, preferred_element_type=F32)
            dv_acc[rows(j), :] += lax.dot_general(pj.astype(BF16), dov, tn, preferred_element_type=F32)
        dq = dq * scale
        if has_kb:
            dq = _rope_rows(dq, c_ref[...], a_ref[...], b_ref[...], -1)
        dq_ref[...] = dq.astype(dq_ref.dtype)

        @pl.when(i == nq - 1)
        def _():
            if has_kb:
                dkv_ref[:, 0:LANES] = (dk_acc[:, 0:LANES] * scale).astype(dkv_ref.dtype)
                dkv_ref[:, LANES:2 * LANES] = dv_acc[...].astype(dkv_ref.dtype)
                dkb_ref[...] = dk_acc[:, LANES:2 * LANES] * scale
            else:
                dka_ref[...] = (dk_acc[...] * scale).astype(dka_ref.dtype)
                dv_ref[...] = dv_acc[...].astype(dv_ref.dtype)

    key_spec = lambda cb, step: pl.BlockSpec((s_k, LANES), lambda h, i: (0, cb + step * h))
    row_spec = lambda cb: pl.BlockSpec((tq, LANES), lambda h, i: (i, cb + h))
    in_specs = [pl.BlockSpec((tq, q_w), lambda h, i: (i, q_cb + h)), key_spec(ka_cb, ka_step)]
    args = [q, ka]
    if has_kb:
        in_specs.append(pl.BlockSpec((s_k, LANES), lambda h, i: (0, 0)))
        args.append(kb)
    in_specs += [key_spec(v_cb, v_step), row_spec(o_cb), row_spec(o_cb), row_spec(0)]
    args += [v, o, do, lse]
    if has_kb:
        in_specs += [pl.BlockSpec((tq, q_w), lambda h, i: (i, 0))] * 3
        args += list(rope)
    out_specs = [pl.BlockSpec((tq, q_w), lambda h, i: (i, h))]
    out_shape = [jax.ShapeDtypeStruct((s_q, heads * q_w), BF16)]
    scratch = []
    if has_kb:
        out_specs += [pl.BlockSpec((s_k, 2 * LANES), lambda h, i: (0, h)), key_spec(0, 1)]
        out_shape += [jax.ShapeDtypeStruct((s_k, heads * 2 * LANES), BF16), jax.ShapeDtypeStruct((s_k, heads * LANES), F32)]
        scratch.append(pltpu.VMEM((s_k, 2 * LANES), BF16))
    else:
        out_specs += [key_spec(0, 1), key_spec(0, 1)]
        out_shape += [jax.ShapeDtypeStruct((s_k, heads * LANES), BF16)] * 2
    scratch += [pltpu.VMEM((s_k, q_w), F32), pltpu.VMEM((s_k, LANES), F32)]
    return pl.pallas_call(
        body,
        grid=(heads, nq),
        in_specs=in_specs,
        out_specs=out_specs,
        out_shape=out_shape,
        scratch_shapes=scratch,
        compiler_params=_params("arbitrary", "arbitrary"),
        name=name,
    )(*args)


def _shift_rows(u, rows):
    t = lax.broadcasted_iota(jnp.int32, u.shape, 0)
    prev = jnp.where(t == 0, 0.0, pltpu.roll(u, 1, 0))
    nxt = jnp.where(t == rows - 1, 0.0, pltpu.roll(u, rows - 1, 0))
    return prev, nxt


def _conv_fwd(z, conv_w, cat, name):
    rows = z.shape[0]
    nblk = CONV_W // LANES

    def body(gb_ref, gc_ref, xin_ref, w_ref, cat_ref, o_ref):
        del cat_ref
        u = gc_ref[...].astype(F32) * xin_ref[...].astype(F32)
        prev, nxt = _shift_rows(u, rows)
        conv = prev * w_ref[0:1, :] + u * w_ref[1:2, :] + nxt * w_ref[2:3, :]
        o_ref[...] = (gb_ref[...].astype(F32) * conv).astype(o_ref.dtype)

    col = lambda c0: pl.BlockSpec((rows, LANES), lambda j: (0, c0 // LANES + j))
    return pl.pallas_call(
        body,
        grid=(nblk,),
        in_specs=[col(Z_GB), col(Z_GC), col(Z_XIN), pl.BlockSpec((3, LANES), lambda j: (0, j)), ANY],
        out_specs=col(MLA_W),
        out_shape=jax.ShapeDtypeStruct(cat.shape, cat.dtype),
        input_output_aliases={4: 0},
        compiler_params=_params("parallel"),
        name=name,
    )(z, z, z, conv_w, cat)


def _conv_bwd(z, conv_w, dcat, name):
    rows = z.shape[0]
    nblk = CONV_W // LANES

    def body(gb_ref, gc_ref, xin_ref, w_ref, dc_ref, dgb_ref, dgc_ref, dxin_ref, dw_ref):
        gc = gc_ref[...].astype(F32)
        xin = xin_ref[...].astype(F32)
        dc = dc_ref[...].astype(F32)
        u = gc * xin
        prev, nxt = _shift_rows(u, rows)
        w0, w1, w2 = w_ref[0:1, :], w_ref[1:2, :], w_ref[2:3, :]
        conv = prev * w0 + u * w1 + nxt * w2
        dgb_ref[...] = (dc * conv).astype(dgb_ref.dtype)
        dconv = dc * gb_ref[...].astype(F32)
        dw_ref[0:1, :] = jnp.sum(dconv * prev, axis=0, keepdims=True)
        dw_ref[1:2, :] = jnp.sum(dconv * u, axis=0, keepdims=True)
        dw_ref[2:3, :] = jnp.sum(dconv * nxt, axis=0, keepdims=True)
        dprev, dnxt = _shift_rows(dconv, rows)
        du = dnxt * w0 + dconv * w1 + dprev * w2
        dgc_ref[...] = (du * xin).astype(dgc_ref.dtype)
        dxin_ref[...] = (du * gc).astype(dxin_ref.dtype)

    col = lambda c0: pl.BlockSpec((rows, LANES), lambda j: (0, c0 // LANES + j))
    w_spec = pl.BlockSpec((3, LANES), lambda j: (0, j))
    piece = jax.ShapeDtypeStruct((rows, CONV_W), BF16)
    return pl.pallas_call(
        body,
        grid=(nblk,),
        in_specs=[col(Z_GB), col(Z_GC), col(Z_XIN), w_spec, col(MLA_W)],
        out_specs=[col(0), col(0), col(0), w_spec],
        out_shape=[piece, piece, piece, jax.ShapeDtypeStruct((3, CONV_W), F32)],
        compiler_params=_params("parallel"),
        name=name,
    )(z, z, z, conv_w, dcat)


def _gate_fwd(cat, z, name):
    rows = cat.shape[0]
    tr = _tile(rows, 256, 16)
    tc = MIX_W
    g0 = Z_GATE // tc

    def body(c_ref, g_ref, y_ref):
        g = g_ref[...].astype(F32)
        y_ref[...] = (c_ref[...].astype(F32) * (g * jax.nn.sigmoid(g))).astype(y_ref.dtype)

    blk = pl.BlockSpec((tr, tc), lambda i, j: (i, j))
    return pl.pallas_call(
        body,
        grid=(rows // tr, MIX_W // tc),
        in_specs=[blk, pl.BlockSpec((tr, tc), lambda i, j: (i, g0 + j))],
        out_specs=blk,
        out_shape=jax.ShapeDtypeStruct((rows, MIX_W), BF16),
        compiler_params=_params("parallel", "parallel"),
        name=name,
    )(cat, z)


def _out_proj_dx_gate_bwd(do, w_o, cat, z, name):
    rows, k = do.shape
    tm = _tile(rows, 1024, 16)
    tn = _tile(MIX_W, 1024, LANES)
    g0 = Z_GATE // tn

    def body(do_ref, w_ref, c_ref, g_ref, dcat_ref, dgate_ref):
        dy = lax.dot_general(do_ref[...], w_ref[...], (((1,), (1,)), ((), ())), preferred_element_type=F32)
        g = g_ref[...].astype(F32)
        sg = jax.nn.sigmoid(g)
        dcat_ref[...] = (dy * (g * sg)).astype(dcat_ref.dtype)
        dgate_ref[...] = (dy * c_ref[...].astype(F32) * (sg * (1.0 + g * (1.0 - sg)))).astype(dgate_ref.dtype)

    blk = pl.BlockSpec((tm, tn), lambda i, j: (i, j))
    out = jax.ShapeDtypeStruct((rows, MIX_W), BF16)
    return pl.pallas_call(
        body,
        grid=(rows // tm, MIX_W // tn),
        in_specs=[pl.BlockSpec((tm, k), lambda i, j: (i, 0)), pl.BlockSpec((tn, k), lambda i, j: (j, 0)), blk,
                  pl.BlockSpec((tm, tn), lambda i, j: (i, g0 + j))],
        out_specs=[blk, blk],
        out_shape=[out, out],
        compiler_params=_params("parallel", "parallel"),
        name=name,
    )(do, w_o, cat, z)


def _loss_head(x, o, gain, target, name):
    rows, width = x.shape
    tr = _tile(rows, 256, 16)

    def body(x_ref, o_ref, gain_ref, t_ref, g_ref, loss_ref):
        i = pl.program_id(0)
        ov = o_ref[...].astype(F32)
        r = lax.rsqrt(jnp.mean(ov * ov, axis=-1, keepdims=True) + EPS)
        d = (x_ref[...] + ov * r * gain_ref[...]) - t_ref[...]
        g_ref[...] = d / width
        part = 0.5 * jnp.sum(jnp.mean(d * d, axis=-1, keepdims=True), axis=0, keepdims=True)
        part = jnp.broadcast_to(part, loss_ref.shape)

        @pl.when(i == 0)
        def _():
            loss_ref[...] = part

        @pl.when(i > 0)
        def _():
            loss_ref[...] += part

    row_spec = pl.BlockSpec((tr, width), lambda i: (i, 0))
    return pl.pallas_call(
        body,
        grid=(rows // tr,),
        in_specs=[row_spec, row_spec, pl.BlockSpec((1, width), lambda i: (0, 0)), row_spec],
        out_specs=[row_spec, pl.BlockSpec((1, LANES), lambda i: (0, 0))],
        out_shape=[jax.ShapeDtypeStruct((rows, width), F32), jax.ShapeDtypeStruct((1, LANES), F32)],
        compiler_params=_params("arbitrary"),
        name=name,
    )(x, o, gain, target)


CHIP_FLIPS = ((1, 0), (0, 1), (1, 1))
ANY = pl.BlockSpec(memory_space=pl.ANY)


def _chip_copies(pieces, sems, n_slot):
    send_sems, recv_sems, local_sems = sems
    x, y, c = lax.axis_index("x"), lax.axis_index("y"), lax.axis_index("c")
    me = 2 * x + y

    def remote(j, k, a, src, dst):
        fx, fy = CHIP_FLIPS[k]
        return pltpu.make_async_remote_copy(
            src_ref=src, dst_ref=dst, send_sem=send_sems.at[n_slot * k + a], recv_sem=recv_sems.at[n_slot * k + a],
            device_id=((j // 2) ^ fx, (j % 2) ^ fy, c), device_id_type=MESH_ID)

    def peer(j, k):
        fx, fy = CHIP_FLIPS[k]
        return 2 * ((j // 2) ^ fx) + ((j % 2) ^ fy)

    def start_as(j):
        def run():
            for a, (src, dst) in enumerate(pieces(j, j)):
                pltpu.make_async_copy(src, dst, local_sems.at[a]).start()
            for k in range(len(CHIP_FLIPS)):
                for a, (src, dst) in enumerate(pieces(j, peer(j, k))):
                    remote(j, k, a, src, dst).start()
        return run

    def wait_as(j):
        def run():
            for a, (src, dst) in enumerate(pieces(j, j)):
                pltpu.make_async_copy(src, dst, local_sems.at[a]).wait()
            for k in range(len(CHIP_FLIPS)):
                for a, (src, dst) in enumerate(pieces(j, peer(j, k))):
                    remote(j, k, a, src, dst).wait_send()
                for a, (src, dst) in enumerate(pieces(peer(j, k), j)):
                    remote(j, k, a, src, dst).wait_recv()
        return run

    def start():
        for j in range(N_CHIPS):
            pl.when(me == j)(start_as(j))

    def wait():
        for j in range(N_CHIPS):
            pl.when(me == j)(wait_as(j))

    return start, wait


IN_PIECES = ((0, Q_RANK, Z_QLAT), (Q_RANK, KV_RANK, Z_KVLAT), (Q_RANK + KV_RANK, ROPE, Z_KPE),
             (Q_RANK + KV_RANK + ROPE, CONV_W, Z_GB), (Q_RANK + KV_RANK + ROPE + CONV_W, CONV_W, Z_GC),
             (Q_RANK + KV_RANK + ROPE + 2 * CONV_W, CONV_W, Z_XIN), (Q_RANK + KV_RANK + ROPE + 3 * CONV_W, MEM_W, Z_QMEM),
             (Q_RANK + KV_RANK + ROPE + 3 * CONV_W + MEM_W, MIX_W, Z_GATE))
IN_SHARD = IN_COLS // N_CHIPS


def _in_segments(j):
    lo, hi = j * IN_SHARD, (j + 1) * IN_SHARD
    segs = []
    for r0, width, z0 in IN_PIECES:
        a, b = max(lo, r0), min(hi, r0 + width)
        if a < b:
            segs.append((a - lo, z0 + a - r0, b - a))
    return segs


N_SLOT = 11


def _gather_plan(l, shards, zero_rows, part="all"):
    s_in, s_uq, s_ukv, s_conv, s_mk, s_mv, s_o = shards
    ukv_c, mk_r, mk_c, o_r = s_ukv.shape[2], s_mk.shape[1], s_mk.shape[2], s_o.shape[1]
    stack = lambda s: jax.ShapeDtypeStruct((N_CHIPS,) + s.shape[1:], s.dtype)
    in_ins, in_outs = [s_in, zero_rows], [jax.ShapeDtypeStruct((Z_COLS, s_in.shape[2]), s_in.dtype)]
    early_ins = [s_uq, s_ukv, s_conv]
    early_outs = [stack(s_uq), jax.ShapeDtypeStruct((s_ukv.shape[1], N_CHIPS * ukv_c), s_ukv.dtype), stack(s_conv)]
    late_ins = [s_mk, s_mv, s_o]
    late_outs = [jax.ShapeDtypeStruct((N_CHIPS * mk_r, 2 * mk_c), s_mk.dtype),
                 jax.ShapeDtypeStruct((N_CHIPS * o_r, s_o.shape[2]), s_o.dtype)]
    with_in, with_early, with_late = part in ("all", "in"), part in ("all", "early"), part in ("all", "late")

    def build(ins, outs, sems):
        ins, outs = list(ins), list(outs)
        if with_in:
            r_in, r_zero, f_in = ins.pop(0), ins.pop(0), outs.pop(0)
        if with_early:
            r_uq, r_ukv, r_conv = ins.pop(0), ins.pop(0), ins.pop(0)
            g_uq, f_ukv, g_conv = outs.pop(0), outs.pop(0), outs.pop(0)
        if with_late:
            r_mk, r_mv, r_o = ins
            f_mkv, f_o = outs

        def pieces(j, t):
            out = []
            if with_in:
                out += [(r_in.at[l, pl.ds(so, n), :], f_in.at[pl.ds(zo, n), :]) for so, zo, n in _in_segments(j)]
            if with_early:
                out += [(r_uq.at[l], g_uq.at[j]), (r_ukv.at[l], f_ukv.at[:, pl.ds(j * ukv_c, ukv_c)]),
                        (r_conv.at[l], g_conv.at[j])]
            if with_late:
                out += [(r_mk.at[l], f_mkv.at[pl.ds(j * mk_r, mk_r), pl.ds(0, mk_c)]),
                        (r_mv.at[l], f_mkv.at[pl.ds(j * mk_r, mk_r), pl.ds(mk_c, mk_c)]),
                        (r_o.at[l], f_o.at[pl.ds(j * o_r, o_r), :])]
            if with_in and j == t:
                out.append((r_zero, f_in.at[pl.ds(Z_KPE + ROPE, LANES - ROPE), :]))
            return out

        return _chip_copies(pieces, sems, N_SLOT)

    ins = (in_ins if with_in else []) + (early_ins if with_early else []) + (late_ins if with_late else [])
    outs = (in_outs if with_in else []) + (early_outs if with_early else []) + (late_outs if with_late else [])
    return _CommPlan(ins, outs, build, len(CHIP_FLIPS) * N_SLOT)


def _scatter_plan(dwt_in, c_uq, dw_ukv, c_conv, dw_mkv, dw_o, part="all"):
    ukv_c, mk_r, mk_c, o_r = dw_ukv.shape[1] // N_CHIPS, dw_mkv.shape[0] // N_CHIPS, dw_mkv.shape[1] // 2, dw_o.shape[0] // N_CHIPS
    with_in, with_rest = part != "rest", part != "in"
    in_outs = [jax.ShapeDtypeStruct((N_CHIPS, IN_SHARD, D_MODEL), BF16)]
    rest_ins = [c_uq, dw_ukv, c_conv, dw_mkv, dw_o]
    rest_outs = [jax.ShapeDtypeStruct(c_uq.shape, c_uq.dtype),
                 jax.ShapeDtypeStruct((N_CHIPS, dw_ukv.shape[0], ukv_c), dw_ukv.dtype),
                 jax.ShapeDtypeStruct(c_conv.shape, c_conv.dtype),
                 jax.ShapeDtypeStruct((N_CHIPS, mk_r, mk_c), dw_mkv.dtype), jax.ShapeDtypeStruct((N_CHIPS, mk_r, mk_c), dw_mkv.dtype),
                 jax.ShapeDtypeStruct((N_CHIPS, o_r, dw_o.shape[1]), dw_o.dtype)]

    def build(ins, outs, sems):
        ins, outs = list(ins), list(outs)
        if with_in:
            r_in, o_in = ins.pop(0), outs.pop(0)
        if with_rest:
            r_uq, r_ukv, r_conv, r_mkv, r_o = ins
            o_uq, o_ukv, o_conv, o_mk, o_mv, o_o = outs

        def pieces(j, t):
            out = []
            if with_in:
                out += [(r_in.at[pl.ds(zo, n), :], o_in.at[j, pl.ds(so, n), :]) for so, zo, n in _in_segments(t)]
            if with_rest:
                out += [(r_uq.at[t], o_uq.at[j]), (r_ukv.at[:, pl.ds(t * ukv_c, ukv_c)], o_ukv.at[j]),
                        (r_conv.at[t], o_conv.at[j]),
                        (r_mkv.at[pl.ds(t * mk_r, mk_r), pl.ds(0, mk_c)], o_mk.at[j]),
                        (r_mkv.at[pl.ds(t * mk_r, mk_r), pl.ds(mk_c, mk_c)], o_mv.at[j]),
                        (r_o.at[pl.ds(t * o_r, o_r), :], o_o.at[j])]
            return out

        return _chip_copies(pieces, sems, N_SLOT)

    ins = ([dwt_in] if with_in else []) + (rest_ins if with_rest else [])
    outs = (in_outs if with_in else []) + (rest_outs if with_rest else [])
    return _CommPlan(ins, outs, build, len(CHIP_FLIPS) * N_SLOT)


HBM = pl.BlockSpec(memory_space=pltpu.HBM)
SEM = pl.BlockSpec(memory_space=pltpu.SEMAPHORE)
SIDE_EFFECT = pltpu.SideEffectType.DATAFLOW_SIDE_EFFECTING


def _comm_start(plan, after, name):
    n_in, n_out, n_after = len(plan.ins), len(plan.out_shape), len(after)
    n_buf = n_in + n_out

    def body(*refs):
        bufs, sems, token = refs[:n_buf], refs[n_buf + n_after:n_buf + n_after + 3], refs[-1]
        start, _ = plan.build(bufs[:n_in], bufs[n_in:], sems)
        start()
        token[...] = jnp.zeros_like(token)

    lands = [lax.empty(s.shape, s.dtype) for s in plan.out_shape]
    args = [pltpu.with_memory_space_constraint(a, pltpu.HBM) for a in list(plan.ins) + lands]
    res = pl.pallas_call(
        body,
        in_specs=[HBM] * n_buf + [ANY] * n_after,
        out_specs=[SEM] * 3 + [HBM] * n_out + [pl.BlockSpec(memory_space=pltpu.VMEM)],
        out_shape=plan.scratch() + [pltpu.HBM(a.shape, a.dtype) for a in lands] + [jax.ShapeDtypeStruct((8, LANES), F32)],
        input_output_aliases={n_in + i: 3 + i for i in range(n_out)},
        compiler_params=pltpu.CompilerParams(has_side_effects=SIDE_EFFECT),
        name=name,
    )(*args, *after)
    return list(res[:3]), list(res[3:3 + n_out]), res[-1]


def _comm_finish(plan, started, after, name):
    sems, lands, _ = started
    n_in, n_out = len(plan.ins), len(plan.out_shape)
    n_buf = n_in + n_out

    def body(*refs):
        bufs_in, sem_refs = refs[:n_buf], refs[n_buf:n_buf + 3]
        _, wait = plan.build(bufs_in[:n_in], bufs_in[n_in:], sem_refs)
        wait()

    sources = [pltpu.with_memory_space_constraint(a, pltpu.HBM) for a in plan.ins]
    res = pl.pallas_call(
        body,
        in_specs=[HBM] * n_buf + [SEM] * 3 + [ANY] * len(after),
        out_specs=[HBM] * n_out,
        out_shape=[pltpu.HBM(b.shape, b.dtype) for b in lands],
        input_output_aliases={n_in + i: i for i in range(n_out)},
        compiler_params=pltpu.CompilerParams(has_side_effects=SIDE_EFFECT),
        name=name,
    )(*sources, *lands, *sems, *after)
    return list(res)


def _sibling_plan(arrays):
    def build(ins, outs, sems):
        send_sems, recv_sems, _ = sems
        sibling = (lax.axis_index("x"), lax.axis_index("y"), 1 - lax.axis_index("c"))
        copies = [pltpu.make_async_remote_copy(src_ref=src, dst_ref=dst, send_sem=send_sems.at[a], recv_sem=recv_sems.at[a],
                                               device_id=sibling, device_id_type=MESH_ID)
                  for a, (src, dst) in enumerate(zip(ins, outs))]

        def start():
            for cp in copies:
                cp.start()

        def wait():
            for cp in copies:
                cp.wait()

        return start, wait

    return _CommPlan(arrays, [jax.ShapeDtypeStruct(v.shape, v.dtype) for v in arrays], build, len(arrays))


DEVICE_FLIPS = tuple((fx, fy, fc) for fx in (0, 1) for fy in (0, 1) for fc in (0, 1))[1:]


def _gather_all(v, after, name):
    def body(v_ref, after_ref, out_ref, send_sems, recv_sems, local_sem):
        del after_ref
        x, y, c = lax.axis_index("x"), lax.axis_index("y"), lax.axis_index("c")
        me = 4 * x + 2 * y + c
        local = pltpu.make_async_copy(v_ref, out_ref.at[me], local_sem)
        local.start()
        copies = [local]
        for k, (fx, fy, fc) in enumerate(DEVICE_FLIPS):
            cp = pltpu.make_async_remote_copy(
                src_ref=v_ref, dst_ref=out_ref.at[me], send_sem=send_sems.at[k], recv_sem=recv_sems.at[k],
                device_id=((x + fx) % 2, (y + fy) % 2, (c + fc) % 2), device_id_type=MESH_ID)
            cp.start()
            copies.append(cp)
        for cp in copies:
            cp.wait()

    return pl.pallas_call(
        body,
        in_specs=[ANY, ANY],
        out_specs=ANY,
        out_shape=jax.ShapeDtypeStruct((N_DEV,) + v.shape, v.dtype),
        scratch_shapes=[pltpu.SemaphoreType.DMA((N_DEV - 1,)), pltpu.SemaphoreType.DMA((N_DEV - 1,)), pltpu.SemaphoreType.DMA],
        name=name,
    )(v, after)


def _sum_slots(parts, name):
    n, rows, cols = parts.shape
    tr = _tile(rows, 256, 16)

    def body(p_ref, o_ref):
        acc = p_ref[0].astype(F32)
        for k in range(1, n):
            acc = acc + p_ref[k].astype(F32)
        o_ref[...] = acc

    return pl.pallas_call(
        body,
        grid=(rows // tr,),
        in_specs=[pl.BlockSpec((n, tr, cols), lambda i: (0, i, 0))],
        out_specs=pl.BlockSpec((tr, cols), lambda i: (i, 0)),
        out_shape=jax.ShapeDtypeStruct((rows, cols), F32),
        compiler_params=_params("parallel"),
        name=name,
    )(parts)


def _adamw_math(w, g, m, v):
    m_new = ADAM_B1 * m + (1.0 - ADAM_B1) * g
    v_new = ADAM_B2 * v + (1.0 - ADAM_B2) * jnp.square(g)
    m_hat = m_new / (1.0 - ADAM_B1 ** ADAM_STEP)
    v_hat = v_new / (1.0 - ADAM_B2 ** ADAM_STEP)
    return -ADAM_LR * (m_hat / (jnp.sqrt(v_hat) + ADAM_EPS) + ADAM_WD * w), m_new, v_new


def _adamw(w, g, m, v, name):
    rows, cols = w.shape
    tr = _tile(rows, 256, 8)

    def body(w_ref, g_ref, m_ref, v_ref, d_out, m_out, v_out):
        d_out[...], m_out[...], v_out[...] = _adamw_math(w_ref[...], g_ref[...], m_ref[...], v_ref[...])

    blk = pl.BlockSpec((tr, cols), lambda i: (i, 0))
    out = jax.ShapeDtypeStruct((rows, cols), F32)
    return pl.pallas_call(
        body,
        grid=(rows // tr,),
        in_specs=[blk] * 4,
        out_specs=[blk] * 3,
        out_shape=[out] * 3,
        compiler_params=_params("parallel"),
        name=name,
    )(w, g, m, v)


def _adamw_layer(l, w, g_a, g_b, m, v, prev, name):
    depth, rows, cols = w.shape
    tr = _tile(rows, 256, 8)

    def body(w_ref, ga_ref, gb_ref, m_ref, v_ref, *rest):
        g_out, d_out, m_out, v_out = rest[-4:]
        g = ga_ref[...] + gb_ref[...]
        g_out[...] = g
        d_out[...], m_out[...], v_out[...] = _adamw_math(w_ref[...], g, m_ref[...], v_ref[...])

    stacked = pl.BlockSpec((None, tr, cols), lambda i: (l, i, 0))
    flat = pl.BlockSpec((tr, cols), lambda i: (i, 0))
    in_specs = [stacked, flat, flat, stacked, stacked]
    args = [w, g_a, g_b, m, v]
    aliases = {}
    if prev is not None:
        in_specs += [ANY] * 4
        args += list(prev)
        aliases = {5 + k: k for k in range(4)}
    out = jax.ShapeDtypeStruct((depth, rows, cols), F32)
    return pl.pallas_call(
        body,
        grid=(rows // tr,),
        in_specs=in_specs,
        out_specs=[stacked] * 4,
        out_shape=[out] * 4,
        input_output_aliases=aliases,
        compiler_params=_params("parallel"),
        name=name,
    )(*args)


def _cols_from_shards(g):
    _, r, c = g.shape
    return jnp.transpose(g, (1, 0, 2)).reshape(r, N_CHIPS * c)


def _cols_to_shards(full):
    r, c4 = full.shape
    c = c4 // N_CHIPS
    return jnp.transpose(full.reshape(r, N_CHIPS, c), (1, 0, 2))


def _w_uq_pad(w_uq):
    r, _ = w_uq.shape
    w = jnp.pad(w_uq.reshape(r, MLA_HEADS, QK_HEAD), ((0, 0), (0, 0), (0, QPAD - QK_HEAD)))
    return w.reshape(r, MLA_HEADS * QPAD)


def _w_uq_unpad(w):
    r, _ = w.shape
    return w.reshape(r, MLA_HEADS, QPAD)[..., :QK_HEAD].reshape(r, MLA_HEADS * QK_HEAD)


def _rope_tables(positions):
    inv_freq = 1.0 / (ROPE_THETA ** (jnp.arange(0, ROPE, 2, dtype=F32) / ROPE))
    ang = positions.astype(F32)[:, None] * inv_freq
    cos, sin = jnp.cos(ang), jnp.sin(ang)
    s = positions.shape[0]
    zero = jnp.zeros((s, HALF_ROPE), F32)
    pad = jnp.zeros((s, LANES - ROPE), F32)
    kc = jnp.concatenate([cos, cos, pad], axis=-1)
    ka = jnp.concatenate([-sin, zero, pad], axis=-1)
    kb = jnp.concatenate([zero, sin, pad], axis=-1)
    qc = jnp.concatenate([jnp.ones((s, NOPE), F32), kc], axis=-1)
    qa = jnp.concatenate([jnp.zeros((s, NOPE), F32), ka], axis=-1)
    qb = jnp.concatenate([jnp.zeros((s, NOPE), F32), kb], axis=-1)
    return (qc, qa, qb), (kc, ka, kb)


def _layer_weights(gathered):
    return (gathered[0],) + _late_weights(gathered[1:])


def _late_weights(gathered):
    g_uq, w_ukv, g_conv, w_mkv, w_o = gathered
    return (_w_uq_pad(_cols_from_shards(g_uq)), w_ukv, _cols_from_shards(g_conv), w_mkv, w_o)


def _layer_fwd(l, x, mem, wts, gains, tabs, late=None, h=None, next_g_pre=None, loss_target=None):
    wt_in = wts[0]
    g_pre, g_q, g_kv, g_mem, g_post = gains
    q_tab, k_tab = tabs
    tag = f"l{l}_"
    if h is None:
        h = _rmsnorm_fwd(x, g_pre, 0, D_MODEL, tag + "pre_norm")
    z = _matmul(h, wt_in, "nt", BF16, tag + "in_proj", tm_cap=1024, tn_cap=1664)
    w_uq, w_ukv, conv_w, w_mkv, w_o = wts[1:] if late is None else late(z)
    wts = (wt_in, w_uq, w_ukv, conv_w, w_mkv, w_o)
    qn, kvn, kpe = _latent_prep(z, g_q, g_kv, *k_tab, tag + "latent_prep")
    q_raw = _matmul(qn, w_uq, "nn", BF16, tag + "uq", tm_cap=1024)
    kv = _matmul(kvn, w_ukv, "nn", BF16, tag + "ukv")
    cat, a_lse = _attn_fwd(q_raw, kv, kpe, kv, q_tab, MLA_HEADS, QPAD, 0, 0, 2, 1, 2, QK_HEAD ** -0.5, 512,
                           tag + "mla_fwd", tk_cap=1024, o_into=(MIX_W, 0, None))
    cat = _conv_fwd(z, conv_w, cat, tag + "conv_fwd")
    mem_n = _rmsnorm_fwd(mem, g_mem, 0, D_MODEL, tag + "mem_norm")
    mkv = _matmul(mem_n, w_mkv, "nn", BF16, tag + "mem_kv")
    cat, m_lse = _attn_fwd(z, mkv, None, mkv, None, MEM_HEADS, LANES, Z_QMEM // LANES, 0, 1, MEM_HEADS, 1,
                           MEM_HEAD ** -0.5, 4096, tag + "mem_fwd", o_into=(MIX_W, (MLA_W + CONV_W) // LANES, cat))
    y = _gate_fwd(cat, z, tag + "gate_fwd")
    o = _matmul(y, w_o, "nn", BF16, tag + "out_proj", tm_cap=1024)
    if loss_target is not None:
        x_new, h_next = _loss_head(x, o, g_post, loss_target, tag + "post_norm_loss"), None
    else:
        x_new, h_next = _post_norm_residual(x, o, g_post, next_g_pre, tag + "post_norm")
    saved = (x, h, z, qn, kvn, q_raw, kv, kpe, a_lse, mem_n, mkv, m_lse, cat, y, o)
    return x_new, saved, h_next


def _layer_bwd(l, g, mem, saved, wts, gains, tabs, split_exchange=False):
    wt_in, w_uq, w_ukv, conv_w, w_mkv, w_o = wts
    g_pre, g_q, g_kv, g_mem, g_post = gains
    q_tab, k_tab = tabs
    x, h, z, qn, kvn, q_raw, kv, kpe, a_lse, mem_n, mkv, m_lse, cat, y, o = saved
    tag = f"l{l}_"
    do, dg_post = _rmsnorm_bwd(o, g_post, g, None, 0, D_MODEL, BF16, tag + "post_norm_bwd")
    dcat, dgate = _out_proj_dx_gate_bwd(do, w_o, cat, z, tag + "out_proj_dx")
    dw_o = _matmul(y, do, "tn", BF16, tag + "out_proj_dw", tm_cap=1024)
    dq, dkv, dkpe_h = _attn_bwd(q_raw, kv, kpe, kv, cat, dcat, a_lse, q_tab, MLA_HEADS, QPAD, 0, 0, 2, 1, 2, 0,
                                QK_HEAD ** -0.5, 1024, tag + "mla_bwd")
    dw_ukv = _matmul(kvn, dkv, "tn", BF16, tag + "ukv_dw")
    dkvn = _matmul(dkv, w_ukv, "nt", BF16, tag + "ukv_dx")
    dw_uq = _matmul(qn, dq, "tn", BF16, tag + "uq_dw")
    dqn = _matmul(dq, w_uq, "nt", BF16, tag + "uq_dx")
    dq_lat, dkv_lat, dkpe, dg_q, dg_kv = _latent_prep_bwd(z, g_q, g_kv, dqn, dkvn, dkpe_h, *k_tab, MLA_HEADS,
                                                          tag + "latent_prep_bwd")
    dgb, dgc, dxin, dconv_w = _conv_bwd(z, conv_w, dcat, tag + "conv_bwd")
    dq_mem, dmk, dmv = _attn_bwd(z, mkv, None, mkv, cat, dcat, m_lse, None, MEM_HEADS, LANES, Z_QMEM // LANES, 0, 1,
                                 MEM_HEADS, 1, (MLA_W + CONV_W) // LANES, MEM_HEAD ** -0.5, 2048, tag + "mem_bwd")
    dmkv = jnp.concatenate([dmk, dmv], axis=-1)
    dw_mkv = _matmul(mem_n, dmkv, "tn", BF16, tag + "mem_kv_dw")
    dmem_n = _matmul(dmkv, w_mkv, "nt", F32, tag + "mem_kv_dx")
    _, dg_mem = _rmsnorm_bwd(mem, g_mem, dmem_n, None, 0, D_MODEL, BF16, tag + "mem_norm_bwd")
    others = (_cols_to_shards(_w_uq_unpad(dw_uq)), dw_ukv, _cols_to_shards(dconv_w), dw_mkv, dw_o)
    early = None
    if split_exchange:
        early_plan = _scatter_plan(None, *others, part="rest")
        early = (early_plan, _comm_start(early_plan, [dmem_n], tag + "exchange_rest_start"))
        g_pre = g_pre + early[1][2][0:1, 0:1]
    dz = jnp.concatenate([dgate, dq_lat, dkv_lat, dkpe, dgb, dgc, dxin, dq_mem], axis=-1)
    dwt_in = _matmul(dz, h, "tn", BF16, tag + "in_proj_dw", tm_cap=1664, tk_cap=2048)
    contrib = _scatter_plan(dwt_in, *others, part="in" if split_exchange else "all")
    late = None
    if split_exchange:
        late = (contrib, _comm_start(contrib, [dwt_in], tag + "exchange_in_start"))
    dh = _matmul(dz, wt_in, "nn", BF16, tag + "in_proj_dx", tm_cap=1024, tk_cap=1664, after=late[1][2] if late else None)
    dx, dg_pre = _rmsnorm_bwd(x, g_pre, dh, g, 0, D_MODEL, F32, tag + "pre_norm_bwd")
    return dx, contrib, (dg_pre, dg_q, dg_kv, dg_mem, dg_post), (early, late)


GAIN_WIDTHS = (D_MODEL, Q_RANK, KV_RANK, D_MODEL, D_MODEL)


def _pack_gains(parts):
    return jnp.concatenate([p.reshape(-1) for p in parts]).reshape(-1, LANES)


def _unpack_gains(packed, depth):
    flat = packed.reshape(-1)
    out, at = [], 0
    for width in GAIN_WIDTHS:
        out.append(flat[at:at + depth * width].reshape(depth, width))
        at += depth * width
    return out


def kernel(x, mem, positions, pre_norm_g, w_in, q_norm_g, w_uq, kv_norm_g, w_ukv, conv_w, mem_norm_g, w_mk, w_mv, w_o, post_norm_g, loss_target, m_pre_norm_g, m_w_in, m_q_norm_g, m_w_uq, m_kv_norm_g, m_w_ukv, m_conv_w, m_mem_norm_g, m_w_mk, m_w_mv, m_w_o, m_post_norm_g, v_pre_norm_g, v_w_in, v_q_norm_g, v_w_uq, v_kv_norm_g, v_w_ukv, v_conv_w, v_mem_norm_g, v_w_mk, v_w_mv, v_w_o, v_post_norm_g):
    depth = w_in.shape[0]
    x0, mem0, target = x[0], mem[0], loss_target[0]
    tabs = _rope_tables(positions[0])

    flip = lambda t: jnp.transpose(t, (0, 2, 1))
    w_in, m_w_in, v_w_in = flip(w_in), flip(m_w_in), flip(v_w_in)
    shards = [w_in.astype(BF16), w_uq.astype(BF16), w_ukv.astype(BF16), conv_w, w_mk.astype(BF16), w_mv.astype(BF16),
              w_o.astype(BF16)]
    zero_rows = lambda: jnp.zeros((LANES - ROPE, D_MODEL), BF16)

    def layer_gains(l):
        return tuple(g[l][None, :] for g in (pre_norm_g, q_norm_g, kv_norm_g, mem_norm_g, post_norm_g))

    wts, saved = [None] * depth, [None] * depth
    plan_in, plan_early, plan_late = (_gather_plan(0, shards, zero_rows(), part) for part in ("in", "early", "late"))
    started_in = _comm_start(plan_in, [positions], "l0_gather_in_start")
    started_early = _comm_start(plan_early, [started_in[2]], "l0_gather_early_start")
    started_late = _comm_start(plan_late, [started_early[2]], "l0_gather_late_start")
    h0 = _rmsnorm_fwd(x0, layer_gains(0)[0], 0, D_MODEL, "l0_pre_norm")
    wts[0] = tuple(_comm_finish(plan_in, started_in, [started_late[2], h0, tabs[0][0]], "l0_gather_in_wait"))

    next_gather = {}

    def start_next_gather(l, after):
        plan = _gather_plan(l + 1, shards, zero_rows())
        next_gather[l + 1] = (plan, _comm_start(plan, [after], f"l{l + 1}_gather_start"))
        return next_gather[l + 1][1][2][0:1, 0:1]

    def rest_of_layer0(z):
        early = _comm_finish(plan_early, started_early, [z], "l0_gather_early_wait")
        late = _comm_finish(plan_late, started_late, [early[1]], "l0_gather_late_wait")
        got = _late_weights(early + late)
        wts[0] = wts[0] + got
        if depth > 1:
            got = (got[0] + start_next_gather(0, early[1]).astype(BF16),) + got[1:]
        return got

    act, h_act = x0, h0
    for l in range(depth):
        gains = layer_gains(l)
        if 0 < l < depth - 1:
            gains = gains[:1] + (gains[1] + start_next_gather(l, wts[l][5]),) + gains[2:]
        act, saved[l], h_act = _layer_fwd(l, act, mem0, wts[l], gains, tabs, rest_of_layer0 if l == 0 else None, h_act,
                                          layer_gains(l + 1)[0] if l + 1 < depth else None,
                                          target if l + 1 == depth else None)
        if l + 1 < depth:
            plan, started = next_gather[l + 1]
            wts[l + 1] = _layer_weights(_comm_finish(plan, started, [act], f"l{l + 1}_gather_wait"))
    grad, loss_part = act
    loss = lax.psum(loss_part[0, 0], ("x", "y", "c"))

    names = ("w_in", "w_uq", "w_ukv", "conv_w", "w_mk", "w_mv", "w_o")
    w_shards = (w_in, w_uq, w_ukv, conv_w, w_mk, w_mv, w_o)
    m_shards = (m_w_in, m_w_uq, m_w_ukv, m_conv_w, m_w_mk, m_w_mv, m_w_o)
    v_shards = (v_w_in, v_w_uq, v_w_ukv, v_conv_w, v_w_mk, v_w_mv, v_w_o)
    stacked = [None] * len(names)

    def sum_and_send(l, received):
        partial = [_sum_slots(r, f"l{l}_grad_sum_{names[i]}") for i, r in enumerate(received)]
        plan = _sibling_plan(partial)
        return l, partial, plan, _comm_start(plan, [partial[0]], f"l{l}_sibling_start")

    def receive_and_update(state, after):
        l, partial, plan, started = state
        other = _comm_finish(plan, started, [after], f"l{l}_sibling_wait")
        for i, name in enumerate(names):
            stacked[i] = _adamw_layer(l, w_shards[i], partial[i], other[i], m_shards[i], v_shards[i], stacked[i],
                                      f"l{l}_adamw_{name}")

    dgs = [None] * depth
    pending = None
    in_flight = None
    for l in reversed(range(depth)):
        gains = layer_gains(l)
        for token in ([pending[1][2]] if pending else []) + ([in_flight[3][2]] if in_flight else []):
            gains = gains[:4] + (gains[4] + token[0:1, 0:1],)
        grad, contrib, dgs[l], early = _layer_bwd(l, grad, mem0, saved[l], wts[l], gains, tabs, l == 0)
        if in_flight is not None:
            receive_and_update(in_flight, grad)
            in_flight = None
        if pending is not None:
            in_flight = sum_and_send(l + 1, _comm_finish(pending[0], pending[1], [grad], f"l{l + 1}_exchange_wait"))
        if l > 0:
            pending = (contrib, _comm_start(contrib, [grad], f"l{l}_exchange_start"))
    early, late = early
    got_in = _comm_finish(late[0], late[1], [grad], "l0_exchange_in_wait")
    last = sum_and_send(0, got_in + _comm_finish(early[0], early[1], [got_in[0]], "l0_exchange_rest_wait"))
    if in_flight is not None:
        receive_and_update(in_flight, last[1][0])
    receive_and_update(last, stacked[0][0] if depth > 1 else last[1][0])
    grad_x = grad[None]
    results = {name: tuple(stacked[i]) for i, name in enumerate(names)}
    results["w_in"] = tuple(flip(t) for t in results["w_in"])

    gain_names = ("pre_norm_g", "q_norm_g", "kv_norm_g", "mem_norm_g", "post_norm_g")
    dg_packed = _pack_gains([jnp.concatenate([dgs[l][i] for l in range(depth)], axis=0) for i in range(5)])
    dg_total = _sum_slots(_gather_all(dg_packed, stacked[0][0], "gain_gather"), "gain_sum")
    gain_outs = (dg_total,) + tuple(_adamw(
        _pack_gains((pre_norm_g, q_norm_g, kv_norm_g, mem_norm_g, post_norm_g)), dg_total,
        _pack_gains((m_pre_norm_g, m_q_norm_g, m_kv_norm_g, m_mem_norm_g, m_post_norm_g)),
        _pack_gains((v_pre_norm_g, v_q_norm_g, v_kv_norm_g, v_mem_norm_g, v_post_norm_g)), "adamw_gains"))
    gain_outs = [_unpack_gains(t, depth) for t in gain_outs]
    for i, name in enumerate(gain_names):
        results[name] = tuple(gain_outs[k][i] for k in range(4))

    order = ("pre_norm_g", "w_in", "q_norm_g", "w_uq", "kv_norm_g", "w_ukv", "conv_w", "mem_norm_g", "w_mk", "w_mv", "w_o",
             "post_norm_g")
    out = [loss, grad_x]
    for k in range(4):
        out += [results[name][k] for name in order]
    return tuple(out)
```
